```python
import jax, jax.numpy as jnp
from jax import lax
import numpy as np

D_MODEL = 2048
BATCH = 8
SEQ = 2048
DEPTH = 1

CHUNK = 64
N_MEM = 256
MIX_WIDTH = D_MODEL
W_A = MIX_WIDTH // 2
G_A = 8
GA_DIM = W_A // G_A
SGU_BLOCK = 128
W_B = MIX_WIDTH - W_A
H_B = 8
DH_B = W_B // H_B
Q_BLOCK = 128
X_HEADS = 4
X_DH = D_MODEL // X_HEADS
D_FF = 5632
EPS = 1e-6

kernel_name = "hybrid_sgu_stickbreak_macaron_block"


def rmsnorm(x, g):
    xf = x.astype(jnp.float32)
    y = xf * lax.rsqrt(jnp.mean(xf * xf, axis=-1, keepdims=True) + EPS)
    return (y * g.astype(jnp.float32)).astype(x.dtype)


def layernorm(x, g, b):
    xf = x.astype(jnp.float32)
    mu = jnp.mean(xf, axis=-1, keepdims=True)
    var = jnp.mean(jnp.square(xf - mu), axis=-1, keepdims=True)
    y = (xf - mu) * lax.rsqrt(var + EPS)
    return (y * g.astype(jnp.float32) + b.astype(jnp.float32)).astype(x.dtype)


def swiglu(x, w_in, w_out):
    gate, up = jnp.split(x @ w_in, 2, axis=-1)
    return (jax.nn.silu(gate) * up) @ w_out


def spatial_gating_unit(za, ln_g, ln_b, w_s, b_s):
    bsz, seq, _ = za.shape
    u, v = jnp.split(za, 2, axis=-1)
    v = layernorm(v.reshape(bsz, seq, G_A, GA_DIM),
                  ln_g.reshape(G_A, GA_DIM), ln_b.reshape(G_A, GA_DIM))
    v = v.reshape(bsz, seq // SGU_BLOCK, SGU_BLOCK, G_A, GA_DIM)
    cidx = jnp.arange(SGU_BLOCK) // CHUNK
    mask = cidx[None, :] <= cidx[:, None]
    w = jnp.where(mask[None], w_s, jnp.zeros((), w_s.dtype))
    mixed = jnp.einsum('gts,bnsgc->bntgc', w, v) + b_s.T[:, :, None]
    return u * mixed.reshape(bsz, seq, W_A)


def stick_breaking_attention(q, k, v):
    seq = q.shape[1]
    q = q * (DH_B ** -0.5)
    outs = []
    for i in range(seq // Q_BLOCK):
        end = (i + 1) * Q_BLOCK
        q_blk = q[:, i * Q_BLOCK:end]
        k_pre, v_pre = k[:, :end], v[:, :end]
        z = jnp.einsum('bqhd,bkhd->bhqk', q_blk, k_pre).astype(jnp.float32)
        t_pos = i * Q_BLOCK + jnp.arange(Q_BLOCK)
        s_pos = jnp.arange(end)
        causal = s_pos[None, :] < t_pos[:, None]
        log_beta = jax.nn.log_sigmoid(z)
        log_1m = jnp.where(causal, jax.nn.log_sigmoid(-z), 0.0)
        rest = lax.cumsum(log_1m, axis=3, reverse=True) - log_1m
        a = jnp.where(causal, jnp.exp(log_beta + rest), 0.0)
        outs.append(jnp.einsum('bhqk,bkhd->bqhd', a.astype(v.dtype), v_pre))
    return jnp.concatenate(outs, axis=1)


def memory_cross_attention(xn, memn, w_cq, w_ckv, w_co):
    bsz, seq, _ = xn.shape
    q = (xn @ w_cq).reshape(bsz, seq, X_HEADS, X_DH) * (X_DH ** -0.5)
    k, v = jnp.split(memn @ w_ckv, 2, axis=-1)
    k = k.reshape(bsz, N_MEM, X_HEADS, X_DH)
    v = v.reshape(bsz, N_MEM, X_HEADS, X_DH)
    s = jnp.einsum('bshd,bmhd->bhsm', q, k).astype(jnp.float32)
    p = jax.nn.softmax(s, axis=-1).astype(v.dtype)
    o = jnp.einsum('bhsm,bmhd->bshd', p, v).reshape(bsz, seq, D_MODEL)
    return o @ w_co


def _fwd_setup_inputs(seed: int = 0) -> dict:
    key = jax.random.key(seed)
    ks = jax.random.split(key, 32)
    f32 = jnp.float32

    def w(k, shape, fan_in):
        return jax.random.normal(k, shape, f32) * (fan_in ** -0.5)

    def gain(k, shape):
        return 1.0 + 0.05 * jax.random.normal(k, shape, f32)

    L = DEPTH
    return {
        "x": jax.random.normal(ks[0], (BATCH, SEQ, D_MODEL), f32),
        "mem": jax.random.normal(ks[1], (BATCH, N_MEM, D_MODEL), f32),
        "ffn1_norm": gain(ks[2], (L, D_MODEL)),
        "ffn1_w_in": w(ks[3], (L, D_MODEL, 2 * D_FF), D_MODEL),
        "ffn1_w_out": w(ks[4], (L, D_FF, D_MODEL), D_FF),
        "mix_norm": gain(ks[5], (L, D_MODEL)),
        "w_mix_in": w(ks[6], (L, D_MODEL, 2 * W_A + 3 * W_B), D_MODEL),
        "ln_v_gain": gain(ks[7], (L, W_A)),
        "ln_v_bias": 0.02 * jax.random.normal(ks[8], (L, W_A), f32),
        "spatial_w": w(ks[9], (L, G_A, SGU_BLOCK, SGU_BLOCK), SGU_BLOCK),
        "spatial_b": 1.0 + 0.1 * jax.random.normal(ks[10], (L, G_A, SGU_BLOCK), f32),
        "gnorm_a": gain(ks[11], (L, W_A)),
        "gnorm_b": gain(ks[12], (L, W_B)),
        "w_mix_out": w(ks[13], (L, MIX_WIDTH, D_MODEL), MIX_WIDTH),
        "cross_norm": gain(ks[14], (L, D_MODEL)),
        "mem_norm": gain(ks[15], (L, D_MODEL)),
        "w_cq": w(ks[16], (L, D_MODEL, D_MODEL), D_MODEL),
        "w_ckv": w(ks[17], (L, D_MODEL, 2 * D_MODEL), D_MODEL),
        "w_co": w(ks[18], (L, D_MODEL, D_MODEL), D_MODEL),
        "ffn2_norm": gain(ks[19], (L, D_MODEL)),
        "ffn2_w_in": w(ks[20], (L, D_MODEL, 2 * D_FF), D_MODEL),
        "ffn2_w_out": w(ks[21], (L, D_FF, D_MODEL), D_FF),
        "final_norm": gain(ks[22], (D_MODEL,)),
    }


def _fwd_reference(x, mem, ffn1_norm, ffn1_w_in, ffn1_w_out, mix_norm, w_mix_in,
              ln_v_gain, ln_v_bias, spatial_w, spatial_b, gnorm_a, gnorm_b,
              w_mix_out, cross_norm, mem_norm, w_cq, w_ckv, w_co,
              ffn2_norm, ffn2_w_in, ffn2_w_out, final_norm):
    bsz, seq, _ = x.shape
    h = x
    for l in range(DEPTH):
        h = h + 0.5 * swiglu(rmsnorm(h, ffn1_norm[l]), ffn1_w_in[l], ffn1_w_out[l])

        z = rmsnorm(h, mix_norm[l]) @ w_mix_in[l]
        za = jax.nn.gelu(z[..., :2 * W_A])
        q, k, v = jnp.split(z[..., 2 * W_A:], 3, axis=-1)
        y_a = spatial_gating_unit(za, ln_v_gain[l], ln_v_bias[l], spatial_w[l], spatial_b[l])
        y_b = stick_breaking_attention(q.reshape(bsz, seq, H_B, DH_B),
                                       k.reshape(bsz, seq, H_B, DH_B),
                                       v.reshape(bsz, seq, H_B, DH_B)).reshape(bsz, seq, W_B)
        y = jnp.concatenate([rmsnorm(y_a, gnorm_a[l]), rmsnorm(y_b, gnorm_b[l])], axis=-1)
        h = h + y @ w_mix_out[l]

        h = h + memory_cross_attention(rmsnorm(h, cross_norm[l]), rmsnorm(mem, mem_norm[l]),
                                       w_cq[l], w_ckv[l], w_co[l])

        h = h + 0.5 * swiglu(rmsnorm(h, ffn2_norm[l]), ffn2_w_in[l], ffn2_w_out[l])
    return rmsnorm(h, final_norm)


import jax as _jax
import jax.numpy as _jnp

TWIN_FORMAT = 'train_step'
FWD_PARAMS = ['x', 'mem', 'ffn1_norm', 'ffn1_w_in', 'ffn1_w_out', 'mix_norm', 'w_mix_in', 'ln_v_gain', 'ln_v_bias', 'spatial_w', 'spatial_b', 'gnorm_a', 'gnorm_b', 'w_mix_out', 'cross_norm', 'mem_norm', 'w_cq', 'w_ckv', 'w_co', 'ffn2_norm', 'ffn2_w_in', 'ffn2_w_out', 'final_norm']
TWIN_WEIGHTS = ['ffn1_norm', 'ffn1_w_in', 'ffn1_w_out', 'mix_norm', 'w_mix_in', 'ln_v_gain', 'ln_v_bias', 'spatial_w', 'spatial_b', 'gnorm_a', 'gnorm_b', 'w_mix_out', 'cross_norm', 'mem_norm', 'w_cq', 'w_ckv', 'w_co', 'ffn2_norm', 'ffn2_w_in', 'ffn2_w_out', 'final_norm']
TWIN_DIFF_INPUT = 'x'
TWIN_INPUTS = ['x', 'mem', 'ffn1_norm', 'ffn1_w_in', 'ffn1_w_out', 'mix_norm', 'w_mix_in', 'ln_v_gain', 'ln_v_bias', 'spatial_w', 'spatial_b', 'gnorm_a', 'gnorm_b', 'w_mix_out', 'cross_norm', 'mem_norm', 'w_cq', 'w_ckv', 'w_co', 'ffn2_norm', 'ffn2_w_in', 'ffn2_w_out', 'final_norm', 'loss_target', 'm_ffn1_norm', 'm_ffn1_w_in', 'm_ffn1_w_out', 'm_mix_norm', 'm_w_mix_in', 'm_ln_v_gain', 'm_ln_v_bias', 'm_spatial_w', 'm_spatial_b', 'm_gnorm_a', 'm_gnorm_b', 'm_w_mix_out', 'm_cross_norm', 'm_mem_norm', 'm_w_cq', 'm_w_ckv', 'm_w_co', 'm_ffn2_norm', 'm_ffn2_w_in', 'm_ffn2_w_out', 'm_final_norm', 'v_ffn1_norm', 'v_ffn1_w_in', 'v_ffn1_w_out', 'v_mix_norm', 'v_w_mix_in', 'v_ln_v_gain', 'v_ln_v_bias', 'v_spatial_w', 'v_spatial_b', 'v_gnorm_a', 'v_gnorm_b', 'v_w_mix_out', 'v_cross_norm', 'v_mem_norm', 'v_w_cq', 'v_w_ckv', 'v_w_co', 'v_ffn2_norm', 'v_ffn2_w_in', 'v_ffn2_w_out', 'v_final_norm']
TWIN_OUTPUTS = ['loss', 'grad_x', 'grad_ffn1_norm', 'grad_ffn1_w_in', 'grad_ffn1_w_out', 'grad_mix_norm', 'grad_w_mix_in', 'grad_ln_v_gain', 'grad_ln_v_bias', 'grad_spatial_w', 'grad_spatial_b', 'grad_gnorm_a', 'grad_gnorm_b', 'grad_w_mix_out', 'grad_cross_norm', 'grad_mem_norm', 'grad_w_cq', 'grad_w_ckv', 'grad_w_co', 'grad_ffn2_norm', 'grad_ffn2_w_in', 'grad_ffn2_w_out', 'grad_final_norm', 'delta_ffn1_norm', 'delta_ffn1_w_in', 'delta_ffn1_w_out', 'delta_mix_norm', 'delta_w_mix_in', 'delta_ln_v_gain', 'delta_ln_v_bias', 'delta_spatial_w', 'delta_spatial_b', 'delta_gnorm_a', 'delta_gnorm_b', 'delta_w_mix_out', 'delta_cross_norm', 'delta_mem_norm', 'delta_w_cq', 'delta_w_ckv', 'delta_w_co', 'delta_ffn2_norm', 'delta_ffn2_w_in', 'delta_ffn2_w_out', 'delta_final_norm', 'new_m_ffn1_norm', 'new_m_ffn1_w_in', 'new_m_ffn1_w_out', 'new_m_mix_norm', 'new_m_w_mix_in', 'new_m_ln_v_gain', 'new_m_ln_v_bias', 'new_m_spatial_w', 'new_m_spatial_b', 'new_m_gnorm_a', 'new_m_gnorm_b', 'new_m_w_mix_out', 'new_m_cross_norm', 'new_m_mem_norm', 'new_m_w_cq', 'new_m_w_ckv', 'new_m_w_co', 'new_m_ffn2_norm', 'new_m_ffn2_w_in', 'new_m_ffn2_w_out', 'new_m_final_norm', 'new_v_ffn1_norm', 'new_v_ffn1_w_in', 'new_v_ffn1_w_out', 'new_v_mix_norm', 'new_v_w_mix_in', 'new_v_ln_v_gain', 'new_v_ln_v_bias', 'new_v_spatial_w', 'new_v_spatial_b', 'new_v_gnorm_a', 'new_v_gnorm_b', 'new_v_w_mix_out', 'new_v_cross_norm', 'new_v_mem_norm', 'new_v_w_cq', 'new_v_w_ckv', 'new_v_w_co', 'new_v_ffn2_norm', 'new_v_ffn2_w_in', 'new_v_ffn2_w_out', 'new_v_final_norm']
TWIN_LEAF_KINDS = {'loss': 'loss', 'grad_x': 'grad_x', 'grad_ffn1_norm': 'grad_w', 'grad_ffn1_w_in': 'grad_w', 'grad_ffn1_w_out': 'grad_w', 'grad_mix_norm': 'grad_w', 'grad_w_mix_in': 'grad_w', 'grad_ln_v_gain': 'grad_w', 'grad_ln_v_bias': 'grad_w', 'grad_spatial_w': 'grad_w', 'grad_spatial_b': 'grad_w', 'grad_gnorm_a': 'grad_w', 'grad_gnorm_b': 'grad_w', 'grad_w_mix_out': 'grad_w', 'grad_cross_norm': 'grad_w', 'grad_mem_norm': 'grad_w', 'grad_w_cq': 'grad_w', 'grad_w_ckv': 'grad_w', 'grad_w_co': 'grad_w', 'grad_ffn2_norm': 'grad_w', 'grad_ffn2_w_in': 'grad_w', 'grad_ffn2_w_out': 'grad_w', 'grad_final_norm': 'grad_w', 'delta_ffn1_norm': 'delta_w', 'delta_ffn1_w_in': 'delta_w', 'delta_ffn1_w_out': 'delta_w', 'delta_mix_norm': 'delta_w', 'delta_w_mix_in': 'delta_w', 'delta_ln_v_gain': 'delta_w', 'delta_ln_v_bias': 'delta_w', 'delta_spatial_w': 'delta_w', 'delta_spatial_b': 'delta_w', 'delta_gnorm_a': 'delta_w', 'delta_gnorm_b': 'delta_w', 'delta_w_mix_out': 'delta_w', 'delta_cross_norm': 'delta_w', 'delta_mem_norm': 'delta_w', 'delta_w_cq': 'delta_w', 'delta_w_ckv': 'delta_w', 'delta_w_co': 'delta_w', 'delta_ffn2_norm': 'delta_w', 'delta_ffn2_w_in': 'delta_w', 'delta_ffn2_w_out': 'delta_w', 'delta_final_norm': 'delta_w', 'new_m_ffn1_norm': 'new_m', 'new_m_ffn1_w_in': 'new_m', 'new_m_ffn1_w_out': 'new_m', 'new_m_mix_norm': 'new_m', 'new_m_w_mix_in': 'new_m', 'new_m_ln_v_gain': 'new_m', 'new_m_ln_v_bias': 'new_m', 'new_m_spatial_w': 'new_m', 'new_m_spatial_b': 'new_m', 'new_m_gnorm_a': 'new_m', 'new_m_gnorm_b': 'new_m', 'new_m_w_mix_out': 'new_m', 'new_m_cross_norm': 'new_m', 'new_m_mem_norm': 'new_m', 'new_m_w_cq': 'new_m', 'new_m_w_ckv': 'new_m', 'new_m_w_co': 'new_m', 'new_m_ffn2_norm': 'new_m', 'new_m_ffn2_w_in': 'new_m', 'new_m_ffn2_w_out': 'new_m', 'new_m_final_norm': 'new_m', 'new_v_ffn1_norm': 'new_v', 'new_v_ffn1_w_in': 'new_v', 'new_v_ffn1_w_out': 'new_v', 'new_v_mix_norm': 'new_v', 'new_v_w_mix_in': 'new_v', 'new_v_ln_v_gain': 'new_v', 'new_v_ln_v_bias': 'new_v', 'new_v_spatial_w': 'new_v', 'new_v_spatial_b': 'new_v', 'new_v_gnorm_a': 'new_v', 'new_v_gnorm_b': 'new_v', 'new_v_w_mix_out': 'new_v', 'new_v_cross_norm': 'new_v', 'new_v_mem_norm': 'new_v', 'new_v_w_cq': 'new_v', 'new_v_w_ckv': 'new_v', 'new_v_w_co': 'new_v', 'new_v_ffn2_norm': 'new_v', 'new_v_ffn2_w_in': 'new_v', 'new_v_ffn2_w_out': 'new_v', 'new_v_final_norm': 'new_v'}


def _forward(args):
    return _fwd_reference(*[args[k] for k in FWD_PARAMS])


def _output_shape():
    out = _jax.eval_shape(lambda: _forward(_fwd_setup_inputs(0)))
    return out.shape, out.dtype

N_MICROBATCH = 1
ADAM_LR = 0.001
ADAM_B1 = 0.9
ADAM_B2 = 0.999
ADAM_EPS = 1e-08
ADAM_WD = 0.01
ADAM_STEP = 10
PER_EXAMPLE_BATCH_AXIS = {'x': 0, 'mem': 0, 'loss_target': 0}
SHARED_INPUTS = []
_WEIGHT_DTYPES = {'ffn1_norm': _jnp.float32, 'ffn1_w_in': _jnp.float32, 'ffn1_w_out': _jnp.float32, 'mix_norm': _jnp.float32, 'w_mix_in': _jnp.float32, 'ln_v_gain': _jnp.float32, 'ln_v_bias': _jnp.float32, 'spatial_w': _jnp.float32, 'spatial_b': _jnp.float32, 'gnorm_a': _jnp.float32, 'gnorm_b': _jnp.float32, 'w_mix_out': _jnp.float32, 'cross_norm': _jnp.float32, 'mem_norm': _jnp.float32, 'w_cq': _jnp.float32, 'w_ckv': _jnp.float32, 'w_co': _jnp.float32, 'ffn2_norm': _jnp.float32, 'ffn2_w_in': _jnp.float32, 'ffn2_w_out': _jnp.float32, 'final_norm': _jnp.float32}
MOMENT_SCALE = {'ffn1_norm': 3.186004e-02, 'ffn1_w_in': 1.264202e-02, 'ffn1_w_out': 2.062212e-02, 'mix_norm': 5.656615e-02, 'w_mix_in': 3.485542e-02, 'ln_v_gain': 2.929891e-02, 'ln_v_bias': 2.831975e-02, 'spatial_w': 2.859675e-02, 'spatial_b': 3.225672e-02, 'gnorm_a': 4.957246e-02, 'gnorm_b': 4.421767e-02, 'w_mix_out': 4.705200e-02, 'cross_norm': 4.539220e-03, 'mem_norm': 6.800684e-03, 'w_cq': 4.519355e-03, 'w_ckv': 4.721448e-03, 'w_co': 4.967376e-03, 'ffn2_norm': 1.892714e-02, 'ffn2_w_in': 7.806716e-03, 'ffn2_w_out': 1.279465e-02, 'final_norm': 8.004039e+00}


def _to_microbatches(a, axis):
    t = _jnp.moveaxis(a, axis, 0)
    t = t.reshape((N_MICROBATCH, t.shape[0] // N_MICROBATCH) + t.shape[1:])
    return _jnp.moveaxis(t, 1, axis + 1)


def setup_inputs(seed: int = 0) -> dict:
    inp = _fwd_setup_inputs(seed)
    key = _jax.random.fold_in(_jax.random.key(seed), 7919)
    shape, _ = _output_shape()
    out = dict(inp)
    out["loss_target"] = _jax.random.normal(_jax.random.fold_in(key, 0), shape, _jnp.float32)
    for i, name in enumerate(TWIN_WEIGHTS):
        w = inp[name].astype(_jnp.float32)
        if MOMENT_SCALE is None:
            s = _jnp.sqrt(_jnp.mean(_jnp.square(w)) + 1e-30)
        else:
            s = MOMENT_SCALE[name]
        km, kv = _jax.random.split(_jax.random.fold_in(key, i + 1))
        out[name] = w
        out["m_" + name] = s * _jax.random.normal(km, w.shape, _jnp.float32)
        out["v_" + name] = (s * s) * _jax.random.uniform(kv, w.shape, _jnp.float32, 0.5, 1.5)
    if N_MICROBATCH > 1:
        for name, axis in PER_EXAMPLE_BATCH_AXIS.items():
            out[name] = _to_microbatches(out[name], axis)
    return {'x': out['x'], 'mem': out['mem'], 'ffn1_norm': out['ffn1_norm'], 'ffn1_w_in': out['ffn1_w_in'], 'ffn1_w_out': out['ffn1_w_out'], 'mix_norm': out['mix_norm'], 'w_mix_in': out['w_mix_in'], 'ln_v_gain': out['ln_v_gain'], 'ln_v_bias': out['ln_v_bias'], 'spatial_w': out['spatial_w'], 'spatial_b': out['spatial_b'], 'gnorm_a': out['gnorm_a'], 'gnorm_b': out['gnorm_b'], 'w_mix_out': out['w_mix_out'], 'cross_norm': out['cross_norm'], 'mem_norm': out['mem_norm'], 'w_cq': out['w_cq'], 'w_ckv': out['w_ckv'], 'w_co': out['w_co'], 'ffn2_norm': out['ffn2_norm'], 'ffn2_w_in': out['ffn2_w_in'], 'ffn2_w_out': out['ffn2_w_out'], 'final_norm': out['final_norm'], 'loss_target': out['loss_target'], 'm_ffn1_norm': out['m_ffn1_norm'], 'm_ffn1_w_in': out['m_ffn1_w_in'], 'm_ffn1_w_out': out['m_ffn1_w_out'], 'm_mix_norm': out['m_mix_norm'], 'm_w_mix_in': out['m_w_mix_in'], 'm_ln_v_gain': out['m_ln_v_gain'], 'm_ln_v_bias': out['m_ln_v_bias'], 'm_spatial_w': out['m_spatial_w'], 'm_spatial_b': out['m_spatial_b'], 'm_gnorm_a': out['m_gnorm_a'], 'm_gnorm_b': out['m_gnorm_b'], 'm_w_mix_out': out['m_w_mix_out'], 'm_cross_norm': out['m_cross_norm'], 'm_mem_norm': out['m_mem_norm'], 'm_w_cq': out['m_w_cq'], 'm_w_ckv': out['m_w_ckv'], 'm_w_co': out['m_w_co'], 'm_ffn2_norm': out['m_ffn2_norm'], 'm_ffn2_w_in': out['m_ffn2_w_in'], 'm_ffn2_w_out': out['m_ffn2_w_out'], 'm_final_norm': out['m_final_norm'], 'v_ffn1_norm': out['v_ffn1_norm'], 'v_ffn1_w_in': out['v_ffn1_w_in'], 'v_ffn1_w_out': out['v_ffn1_w_out'], 'v_mix_norm': out['v_mix_norm'], 'v_w_mix_in': out['v_w_mix_in'], 'v_ln_v_gain': out['v_ln_v_gain'], 'v_ln_v_bias': out['v_ln_v_bias'], 'v_spatial_w': out['v_spatial_w'], 'v_spatial_b': out['v_spatial_b'], 'v_gnorm_a': out['v_gnorm_a'], 'v_gnorm_b': out['v_gnorm_b'], 'v_w_mix_out': out['v_w_mix_out'], 'v_cross_norm': out['v_cross_norm'], 'v_mem_norm': out['v_mem_norm'], 'v_w_cq': out['v_w_cq'], 'v_w_ckv': out['v_w_ckv'], 'v_w_co': out['v_w_co'], 'v_ffn2_norm': out['v_ffn2_norm'], 'v_ffn2_w_in': out['v_ffn2_w_in'], 'v_ffn2_w_out': out['v_ffn2_w_out'], 'v_final_norm': out['v_final_norm']}


def _loss(weights, diff, rest, loss_target):
    with _jax.named_scope("forward"):
        args = {**rest, TWIN_DIFF_INPUT: diff, **{k: w.astype(_WEIGHT_DTYPES[k]) for k, w in weights.items()}}
        y = _forward(args)
    with _jax.named_scope("loss_head"):
        err = _jnp.square(y.astype(_jnp.float32) - loss_target)
        return 0.5 * _jnp.sum(_jnp.mean(err, axis=-1)) if err.ndim else 0.5 * err


def _adamw(w, g, m, v):
    m = ADAM_B1 * m + (1.0 - ADAM_B1) * g
    v = ADAM_B2 * v + (1.0 - ADAM_B2) * _jnp.square(g)
    m_hat = m / (1.0 - ADAM_B1 ** ADAM_STEP)
    v_hat = v / (1.0 - ADAM_B2 ** ADAM_STEP)
    delta = -ADAM_LR * (m_hat / (_jnp.sqrt(v_hat) + ADAM_EPS) + ADAM_WD * w)
    return delta, m, v


def reference(x, mem, ffn1_norm, ffn1_w_in, ffn1_w_out, mix_norm, w_mix_in, ln_v_gain, ln_v_bias, spatial_w, spatial_b, gnorm_a, gnorm_b, w_mix_out, cross_norm, mem_norm, w_cq, w_ckv, w_co, ffn2_norm, ffn2_w_in, ffn2_w_out, final_norm, loss_target, m_ffn1_norm, m_ffn1_w_in, m_ffn1_w_out, m_mix_norm, m_w_mix_in, m_ln_v_gain, m_ln_v_bias, m_spatial_w, m_spatial_b, m_gnorm_a, m_gnorm_b, m_w_mix_out, m_cross_norm, m_mem_norm, m_w_cq, m_w_ckv, m_w_co, m_ffn2_norm, m_ffn2_w_in, m_ffn2_w_out, m_final_norm, v_ffn1_norm, v_ffn1_w_in, v_ffn1_w_out, v_mix_norm, v_w_mix_in, v_ln_v_gain, v_ln_v_bias, v_spatial_w, v_spatial_b, v_gnorm_a, v_gnorm_b, v_w_mix_out, v_cross_norm, v_mem_norm, v_w_cq, v_w_ckv, v_w_co, v_ffn2_norm, v_ffn2_w_in, v_ffn2_w_out, v_final_norm):
    given = dict(x=x, mem=mem, ffn1_norm=ffn1_norm, ffn1_w_in=ffn1_w_in, ffn1_w_out=ffn1_w_out, mix_norm=mix_norm, w_mix_in=w_mix_in, ln_v_gain=ln_v_gain, ln_v_bias=ln_v_bias, spatial_w=spatial_w, spatial_b=spatial_b, gnorm_a=gnorm_a, gnorm_b=gnorm_b, w_mix_out=w_mix_out, cross_norm=cross_norm, mem_norm=mem_norm, w_cq=w_cq, w_ckv=w_ckv, w_co=w_co, ffn2_norm=ffn2_norm, ffn2_w_in=ffn2_w_in, ffn2_w_out=ffn2_w_out, final_norm=final_norm, loss_target=loss_target, m_ffn1_norm=m_ffn1_norm, m_ffn1_w_in=m_ffn1_w_in, m_ffn1_w_out=m_ffn1_w_out, m_mix_norm=m_mix_norm, m_w_mix_in=m_w_mix_in, m_ln_v_gain=m_ln_v_gain, m_ln_v_bias=m_ln_v_bias, m_spatial_w=m_spatial_w, m_spatial_b=m_spatial_b, m_gnorm_a=m_gnorm_a, m_gnorm_b=m_gnorm_b, m_w_mix_out=m_w_mix_out, m_cross_norm=m_cross_norm, m_mem_norm=m_mem_norm, m_w_cq=m_w_cq, m_w_ckv=m_w_ckv, m_w_co=m_w_co, m_ffn2_norm=m_ffn2_norm, m_ffn2_w_in=m_ffn2_w_in, m_ffn2_w_out=m_ffn2_w_out, m_final_norm=m_final_norm, v_ffn1_norm=v_ffn1_norm, v_ffn1_w_in=v_ffn1_w_in, v_ffn1_w_out=v_ffn1_w_out, v_mix_norm=v_mix_norm, v_w_mix_in=v_w_mix_in, v_ln_v_gain=v_ln_v_gain, v_ln_v_bias=v_ln_v_bias, v_spatial_w=v_spatial_w, v_spatial_b=v_spatial_b, v_gnorm_a=v_gnorm_a, v_gnorm_b=v_gnorm_b, v_w_mix_out=v_w_mix_out, v_cross_norm=v_cross_norm, v_mem_norm=v_mem_norm, v_w_cq=v_w_cq, v_w_ckv=v_w_ckv, v_w_co=v_w_co, v_ffn2_norm=v_ffn2_norm, v_ffn2_w_in=v_ffn2_w_in, v_ffn2_w_out=v_ffn2_w_out, v_final_norm=v_final_norm)
    weights = {n: given[n] for n in TWIN_WEIGHTS}
    shared = {n: given[n] for n in SHARED_INPUTS}
    per_example = {n: given[n] for n in ['x', 'mem']}
    grad_fn = _jax.value_and_grad(_loss, argnums=(0, 1))

    def one_microbatch(ex, loss_target):
        ex = dict(ex)
        diff = ex.pop(TWIN_DIFF_INPUT)
        return grad_fn(weights, diff, {**shared, **ex}, loss_target)

    if N_MICROBATCH == 1:
        loss, (grad_w, grad_x) = one_microbatch(per_example, given["loss_target"])
    else:
        def body(carry, xs):
            loss_sum, grad_sum = carry
            l_k, (gw_k, gx_k) = one_microbatch(xs[0], xs[1])
            with _jax.named_scope("update"):
                return (loss_sum + l_k, _jax.tree.map(_jnp.add, grad_sum, gw_k)), gx_k

        init = (_jnp.zeros((), _jnp.float32), _jax.tree.map(_jnp.zeros_like, weights))
        (loss, grad_w), grad_x = _jax.lax.scan(body, init, (per_example, given["loss_target"]))
    with _jax.named_scope("update"):
        delta_w, new_m, new_v = {}, {}, {}
        for n in TWIN_WEIGHTS:
            delta_w[n], new_m[n], new_v[n] = _adamw(weights[n], grad_w[n], given["m_" + n], given["v_" + n])
    return (loss, grad_x, *[grad_w[n] for n in TWIN_WEIGHTS], *[delta_w[n] for n in TWIN_WEIGHTS],
            *[new_m[n] for n in TWIN_WEIGHTS], *[new_v[n] for n in TWIN_WEIGHTS])
```

```python
import functools
import math

import jax
import jax.numpy as jnp
from jax import lax
from jax.experimental import pallas as pl
from jax.experimental.pallas import tpu as pltpu

F32 = jnp.float32
BF16 = jnp.bfloat16
MESH = pl.DeviceIdType.MESH

EPS = 1e-6
CHUNK = 64
SGU_BLOCK = 128
GROUP_DIM = 128
X_HEADS = 4
N_CHIPS = 4
N_DEV = 8
LANE = 128
SUBLANE = 8
BF16_ROWS = 16

ADAM_LR = 0.001
ADAM_B1 = 0.9
ADAM_B2 = 0.999
ADAM_EPS = 1e-08
ADAM_WD = 0.01
ADAM_STEP = 10

V7X_VMEM_BYTES = 64 << 20
VMEM_LIMIT = V7X_VMEM_BYTES - (8 << 20)


def _params(n_grid):
    return pltpu.CompilerParams(dimension_semantics=("arbitrary",) * n_grid,
                                vmem_limit_bytes=VMEM_LIMIT)


def _pick(pref, dims, unit=None):
    g = functools.reduce(math.gcd, dims)
    if unit is None:
        unit = LANE if g % LANE == 0 else SUBLANE
    cands = [d for d in range(unit, g + 1, unit) if g % d == 0] or [g]
    return min(cands, key=lambda d: abs(math.log(d / pref)))


def _any_spec():
    return pl.BlockSpec(memory_space=pl.ANY)


class Mat:
    def __init__(self, arr, kind="c"):
        if arr.ndim == 2:
            arr = arr[None]
        self.arr, self.kind = arr, kind
        self.P, self.prow, self.pcol = arr.shape
        self.rows = self.prow * (self.P if kind == "r" else 1)
        self.cols = self.pcol * (self.P if kind == "c" else 1)
        self.dtype = arr.dtype

    def spec(self, tr, tc, rc_fn):
        if self.kind == "c":
            per = self.pcol // tc
            assert per * tc == self.pcol, (self.pcol, tc)

            def imap(*g):
                i, j = rc_fn(*g)
                return (j // per, i, j % per)
        else:
            per = self.prow // tr
            assert per * tr == self.prow, (self.prow, tr)

            def imap(*g):
                i, j = rc_fn(*g)
                return (i // per, i % per, j)
        return pl.BlockSpec((None, tr, tc), imap)

    def two_d(self):
        assert self.P == 1
        return self.arr[0]


def _out_mat(kind, P, rows, cols, dtype):
    shape = (P, rows, cols // P) if kind == "c" else (P, rows // P, cols)
    return jax.ShapeDtypeStruct(shape, dtype)


def _matmul(name, A, B, mode, outs, *, tm=1024, tn=1024, tk=2048, extras=(), epi=None):
    if mode == "nn":
        M, K, N = A.rows, A.cols, B.cols
        assert B.rows == K
    elif mode == "nt":
        M, K, N = A.rows, A.cols, B.rows
        assert B.cols == K
    else:
        K, M, N = A.rows, A.cols, B.cols
        assert B.rows == K
    mdims, ndims, kdims = [M], [N], [K]
    if mode == "tn":
        assert A.kind == "c" and B.kind == "c"
        mdims.append(A.pcol)
        ndims.append(B.pcol)
    else:
        (mdims if A.kind == "r" else kdims).append(A.prow if A.kind == "r" else A.pcol)
        if mode == "nn":
            (kdims if B.kind == "r" else ndims).append(B.prow if B.kind == "r" else B.pcol)
        else:
            (ndims if B.kind == "r" else kdims).append(B.prow if B.kind == "r" else B.pcol)
    for o in list(outs) + list(extras):
        if isinstance(o, Mat):
            (mdims if o.kind == "r" else ndims).append(o.prow if o.kind == "r" else o.pcol)
        elif isinstance(o[0], str):
            (mdims if o[0] == "r" else ndims).append((M if o[0] == "r" else N) // o[1])
    tm, tn = _pick(tm, mdims), _pick(tn, ndims)
    tk = K if mode == "tn" else _pick(tk, kdims)
    nk = K // tk
    grid = (M // tm, N // tn, nk)

    if mode == "tn":
        a_spec = A.spec(K, tm, lambda m, n, k: (0, m))
        b_spec = B.spec(K, tn, lambda m, n, k: (0, n))
    else:
        a_spec = A.spec(tm, tk, lambda m, n, k: (m, k))
        if mode == "nn":
            b_spec = B.spec(tk, tn, lambda m, n, k: (k, n))
        else:
            b_spec = B.spec(tn, tk, lambda m, n, k: (n, k))

    def mn_spec(o):
        if isinstance(o, Mat):
            return o.spec(tm, tn, lambda m, n, k: (m, n))
        if isinstance(o[0], str):
            kind, P = o[0], o[1]
            fake = Mat.__new__(Mat)
            fake.kind, fake.P = kind, P
            fake.prow = M // P if kind == "r" else M
            fake.pcol = N // P if kind == "c" else N
            return Mat.spec(fake, tm, tn, lambda m, n, k: (m, n))
        return o[1](tm, tn)

    out_shapes = tuple(_out_mat(o[0], o[1], M, N, o[2]) if isinstance(o[0], str) else o[0]
                       for o in outs)
    out_specs = tuple(mn_spec(o) for o in outs)
    extra_arrays = tuple(e.arr if isinstance(e, Mat) else e[0] for e in extras)
    extra_specs = tuple(mn_spec(e) for e in extras)
    n_ex, n_out = len(extras), len(outs)
    tt = _pick(256, [tm])
    dims = (((1,), (1 if mode == "nt" else 0,)), ((), ()))

    def body(*refs):
        a_ref, b_ref = refs[:2]
        ex_refs = refs[2:2 + n_ex]
        out_refs = refs[2 + n_ex:2 + n_ex + n_out]
        scratch = refs[2 + n_ex + n_out:]
        if mode == "tn":
            at_ref = scratch[0]

            @pl.when(pl.program_id(1) == 0)
            def _():
                for c0 in range(0, tm, tt):
                    at_ref[c0:c0 + tt, :] = a_ref[:, c0:c0 + tt].astype(F32).T.astype(BF16)

            lhs = at_ref[...]
        else:
            lhs = a_ref[...].astype(BF16)
        part = lax.dot_general(lhs, b_ref[...].astype(BF16), dims, preferred_element_type=F32)

        def finish(acc):
            if epi is None:
                out_refs[0][...] = acc.astype(out_refs[0].dtype)
            else:
                epi(acc, ex_refs, out_refs)

        if nk == 1:
            finish(part)
        else:
            acc_ref = scratch[0]
            k = pl.program_id(2)

            @pl.when(k == 0)
            def _():
                acc_ref[...] = part

            @pl.when(k > 0)
            def _():
                acc_ref[...] += part

            @pl.when(k == nk - 1)
            def _():
                finish(acc_ref[...])

    scratch_shapes = []
    if mode == "tn":
        scratch_shapes.append(pltpu.VMEM((tm, K), BF16))
    elif nk > 1:
        scratch_shapes.append(pltpu.VMEM((tm, tn), F32))
    res = pl.pallas_call(
        body, name=name, grid=grid,
        in_specs=[a_spec, b_spec, *extra_specs], out_specs=out_specs, out_shape=out_shapes,
        scratch_shapes=scratch_shapes, compiler_params=_params(3),
    )(A.arr, B.arr, *extra_arrays)
    return res


def _row_tile(T):
    return _pick(256, [T])


def _rmsnorm_fwd(name, x, g, *, into=None, col=0):
    T, W = x.shape
    tr = _row_tile(T)

    def body(x_ref, g_ref, *rest):
        o_ref = rest[-1]
        xv = x_ref[...]
        rstd = lax.rsqrt(jnp.mean(xv * xv, axis=-1, keepdims=True) + EPS)
        o_ref[...] = (xv * rstd * g_ref[...]).astype(o_ref.dtype)

    in_specs = [pl.BlockSpec((tr, W), lambda i: (i, 0)), pl.BlockSpec((1, W), lambda i: (0, 0))]
    args = [x, g]
    kwargs = {}
    if into is None:
        out_shape = jax.ShapeDtypeStruct((T, W), BF16)
    else:
        out_shape = jax.ShapeDtypeStruct(into.shape, into.dtype)
        in_specs.append(_any_spec())
        args.append(into)
        kwargs["input_output_aliases"] = {2: 0}
    return pl.pallas_call(
        body, name=name, grid=(T // tr,), in_specs=in_specs,
        out_specs=pl.BlockSpec((tr, W), lambda i: (i, col)), out_shape=out_shape,
        compiler_params=_params(1), **kwargs)(*args)


def _rmsnorm_bwd(name, x, g, dn, *, dn_col=0, dres=None, want_dx=True, want_bf16=True):
    T, W = x.shape
    tr = _row_tile(T)
    has_res = dres is not None

    def body(*refs):
        x_ref, g_ref, dn_ref = refs[:3]
        pos = 3
        dres_ref = None
        if has_res:
            dres_ref = refs[pos]
            pos += 1
        outs = refs[pos:]
        dg_ref = outs[-1]
        xv = x_ref[...]
        rstd = lax.rsqrt(jnp.mean(xv * xv, axis=-1, keepdims=True) + EPS)
        xhat = xv * rstd
        dnv = dn_ref[...].astype(F32)

        @pl.when(pl.program_id(0) == 0)
        def _():
            dg_ref[...] = jnp.zeros_like(dg_ref)

        dg_ref[...] += jnp.sum(dnv * xhat, axis=0, keepdims=True)
        if want_dx:
            t = dnv * g_ref[...]
            dx = rstd * (t - xhat * jnp.mean(t * xhat, axis=-1, keepdims=True))
            if has_res:
                dx = dx + dres_ref[...]
            outs[0][...] = dx
            if want_bf16:
                outs[1][...] = dx.astype(BF16)

    row = pl.BlockSpec((tr, W), lambda i: (i, 0))
    in_specs = [row, pl.BlockSpec((1, W), lambda i: (0, 0)),
                pl.BlockSpec((tr, W), lambda i: (i, dn_col))]
    args = [x, g, dn]
    if has_res:
        in_specs.append(row)
        args.append(dres)
    out_shape, out_specs = [], []
    if want_dx:
        out_shape.append(jax.ShapeDtypeStruct((T, W), F32))
        out_specs.append(row)
        if want_bf16:
            out_shape.append(jax.ShapeDtypeStruct((T, W), BF16))
            out_specs.append(row)
    out_shape.append(jax.ShapeDtypeStruct((1, W), F32))
    out_specs.append(pl.BlockSpec((1, W), lambda i: (0, 0)))
    return pl.pallas_call(
        body, name=name, grid=(T // tr,), in_specs=in_specs, out_specs=out_specs,
        out_shape=out_shape, compiler_params=_params(1))(*args)


def _loss_head(name, h, g, target):
    T, W = h.shape
    tr = _row_tile(T)

    def body(h_ref, g_ref, t_ref, loss_ref, dx_ref, dxb_ref, dg_ref):
        xv = h_ref[...]
        gv = g_ref[...]
        rstd = lax.rsqrt(jnp.mean(xv * xv, axis=-1, keepdims=True) + EPS)
        xhat = xv * rstd
        diff = xhat * gv - t_ref[...]

        @pl.when(pl.program_id(0) == 0)
        def _():
            dg_ref[...] = jnp.zeros_like(dg_ref)
            loss_ref[...] = jnp.zeros_like(loss_ref)

        loss_ref[...] += 0.5 * jnp.sum(jnp.mean(diff * diff, axis=-1, keepdims=True))
        dnv = diff * (1.0 / W)
        dg_ref[...] += jnp.sum(dnv * xhat, axis=0, keepdims=True)
        t = dnv * gv
        dx = rstd * (t - xhat * jnp.mean(t * xhat, axis=-1, keepdims=True))
        dx_ref[...] = dx
        dxb_ref[...] = dx.astype(BF16)

    row = pl.BlockSpec((tr, W), lambda i: (i, 0))
    vec = pl.BlockSpec((1, W), lambda i: (0, 0))
    return pl.pallas_call(
        body, name=name, grid=(T // tr,), in_specs=[row, vec, row],
        out_specs=[pl.BlockSpec((SUBLANE, LANE), lambda i: (0, 0)), row, row, vec],
        out_shape=[jax.ShapeDtypeStruct((SUBLANE, LANE), F32), jax.ShapeDtypeStruct((T, W), F32),
                   jax.ShapeDtypeStruct((T, W), BF16), jax.ShapeDtypeStruct((1, W), F32)],
        compiler_params=_params(1))(h, g, target)


def _sigmoid(x):
    return 1.0 / (1.0 + jnp.exp(-x))


def _ffn_in(name, n, W):
    T, D = n.shape
    F = W.cols // 2
    tm = _pick(2048, [T])
    tn = _pick(512, [W.pcol])
    per = W.pcol // tn

    def body(a_ref, wg_ref, wu_ref, gu_ref, act_ref):
        a = a_ref[...]
        gate = jnp.dot(a, wg_ref[...], preferred_element_type=F32)
        up = jnp.dot(a, wu_ref[...], preferred_element_type=F32)
        gu_ref[0] = gate.astype(BF16)
        gu_ref[1] = up.astype(BF16)
        act_ref[...] = (gate * _sigmoid(gate) * up).astype(BF16)

    return pl.pallas_call(
        body, name=name, grid=(T // tm, F // tn),
        in_specs=[pl.BlockSpec((tm, D), lambda m, j: (m, 0)),
                  pl.BlockSpec((None, D, tn), lambda m, j: (j // per, 0, j % per)),
                  pl.BlockSpec((None, D, tn), lambda m, j: (2 + j // per, 0, j % per))],
        out_specs=[pl.BlockSpec((2, tm, tn), lambda m, j: (0, m, j)),
                   pl.BlockSpec((tm, tn), lambda m, j: (m, j))],
        out_shape=[jax.ShapeDtypeStruct((2, T, F), BF16), jax.ShapeDtypeStruct((T, F), BF16)],
        compiler_params=_params(2))(n, W.arr, W.arr)


def _ffn_forward(tag, h, norm_g, w_in, w_out):
    n = _rmsnorm_fwd(f"{tag}_norm", h, norm_g)
    gu, act = _ffn_in(f"{tag}_in", n, w_in)

    def epi(acc, ex, out):
        out[0][...] = ex[0][...] + 0.5 * acc

    (h_out,) = _matmul(f"{tag}_out", Mat(act), w_out, "nn", [("c", 1, F32)],
                       tm=1024, tn=1024, tk=1408, extras=[Mat(h)], epi=epi)
    return h_out[0], (n, gu, act)


def _ffn_backward(tag, h_in, norm_g, w_in, w_out, saved, dh, dh_bf):
    n, gu, act = saved
    T, F = act.shape

    def epi(acc, ex, out):
        dact = 0.5 * acc
        gate = ex[0][0].astype(F32)
        up = ex[0][1].astype(F32)
        sig = _sigmoid(gate)
        out[0][0] = (dact * up * sig * (1.0 + gate * (1.0 - sig))).astype(BF16)
        out[0][1] = (dact * gate * sig).astype(BF16)

    def pair_spec(tm, tn):
        return pl.BlockSpec((2, tm, tn), lambda m, j, k: (0, m, j))

    (dgu,) = _matmul(f"{tag}_dact", Mat(dh_bf), w_out, "nt",
                     [(jax.ShapeDtypeStruct((2, T, F), BF16), pair_spec)],
                     tm=512, tn=1408, extras=[(gu, pair_spec)], epi=epi)

    def half(acc, ex, out):
        out[0][...] = (0.5 * acc).astype(out[0].dtype)

    (dw_out,) = _matmul(f"{tag}_dwout", Mat(act), Mat(dh_bf), "tn", [("r", N_CHIPS, BF16)],
                        tm=1408, tn=512, epi=half)
    dgu_m = Mat(dgu)
    (dn,) = _matmul(f"{tag}_dn", dgu_m, w_in, "nt", [("c", 1, F32)], tm=1024, tn=1024, tk=2816)
    (dw_in,) = _matmul(f"{tag}_dwin", Mat(n), dgu_m, "tn", [("c", N_CHIPS, BF16)],
                       tm=512, tn=1408)
    dh_in, dh_in_bf, dg = _rmsnorm_bwd(f"{tag}_dnorm", h_in, norm_g, dn[0], dres=dh)
    return dh_in, dh_in_bf, dg, dw_in, dw_out


_GELU_C = math.sqrt(2.0 / math.pi)
_GELU_A = 0.044715


def _gelu(x):
    return 0.5 * x * (1.0 + jnp.tanh(_GELU_C * (x + _GELU_A * x * x * x)))


def _gelu_grad(x):
    th = jnp.tanh(_GELU_C * (x + _GELU_A * x * x * x))
    return 0.5 * (1.0 + th) + 0.5 * x * (1.0 - th * th) * _GELU_C * (1.0 + 3.0 * _GELU_A * x * x)


def _chunk_mask():
    t = lax.broadcasted_iota(jnp.int32, (SGU_BLOCK, SGU_BLOCK), 0) // CHUNK
    s = lax.broadcasted_iota(jnp.int32, (SGU_BLOCK, SGU_BLOCK), 1) // CHUNK
    return s <= t


def _sgu_group_forward(v_g, lg, lb, wm_bf, b_col):
    mu = jnp.mean(v_g, axis=-1, keepdims=True)
    xc = v_g - mu
    rstd = lax.rsqrt(jnp.mean(xc * xc, axis=-1, keepdims=True) + EPS)
    vhat = xc * rstd
    vn = vhat * lg + lb
    mixed = jnp.dot(wm_bf, vn.astype(BF16), preferred_element_type=F32) + b_col
    return vhat, rstd, vn, mixed


def _sgu_forward(name, z, ln_g, ln_b, w_s, b_t, gn, d_model):
    T = z.shape[0]
    W_A = ln_g.shape[1]
    G = W_A // GROUP_DIM

    def body(z_ref, lg_ref, lb_ref, w_ref, bt_ref, gn_ref, y_ref):
        mask = _chunk_mask()
        u = _gelu(z_ref[:, :W_A])
        v = _gelu(z_ref[:, W_A:])
        cols = []
        for g in range(G):
            sl = slice(g * GROUP_DIM, (g + 1) * GROUP_DIM)
            wm = jnp.where(mask, w_ref[g], 0.0).astype(BF16)
            _, _, _, mixed = _sgu_group_forward(v[:, sl], lg_ref[:, sl], lb_ref[:, sl], wm,
                                                bt_ref[:, g:g + 1])
            cols.append(u[:, sl] * mixed)
        ya = jnp.concatenate(cols, axis=1)
        rstd = lax.rsqrt(jnp.mean(ya * ya, axis=-1, keepdims=True) + EPS)
        y_ref[...] = (ya * rstd * gn_ref[...]).astype(BF16)

    vec = pl.BlockSpec((1, W_A), lambda i: (0, 0))
    return pl.pallas_call(
        body, name=name, grid=(T // SGU_BLOCK,),
        in_specs=[pl.BlockSpec((SGU_BLOCK, 2 * W_A), lambda i: (i, 0)), vec, vec,
                  pl.BlockSpec((G, SGU_BLOCK, SGU_BLOCK), lambda i: (0, 0, 0)),
                  pl.BlockSpec((SGU_BLOCK, G), lambda i: (0, 0)), vec],
        out_specs=pl.BlockSpec((SGU_BLOCK, W_A), lambda i: (i, 0)),
        out_shape=jax.ShapeDtypeStruct((T, d_model), BF16),
        compiler_params=_params(1))(z, ln_g, ln_b, w_s, b_t, gn)


def _sgu_backward(name, z, dy, ln_g, ln_b, w_s, b_t, gn):
    T = z.shape[0]
    W_A = ln_g.shape[1]
    G = W_A // GROUP_DIM

    def body(z_ref, dy_ref, lg_ref, lb_ref, w_ref, bt_ref, gn_ref,
             dz_ref, dlg_ref, dlb_ref, dw_ref, db_ref, dgn_ref):
        @pl.when(pl.program_id(0) == 0)
        def _():
            for r in (dlg_ref, dlb_ref, dw_ref, db_ref, dgn_ref):
                r[...] = jnp.zeros_like(r)

        mask = _chunk_mask()
        zu = z_ref[:, :W_A]
        zv = z_ref[:, W_A:]
        u = _gelu(zu)
        v = _gelu(zv)
        saved, cols = [], []
        for g in range(G):
            sl = slice(g * GROUP_DIM, (g + 1) * GROUP_DIM)
            wm = jnp.where(mask, w_ref[g], 0.0)
            vhat, rstd, vn, mixed = _sgu_group_forward(
                v[:, sl], lg_ref[:, sl], lb_ref[:, sl], wm.astype(BF16), bt_ref[:, g:g + 1])
            saved.append((wm, vhat, rstd, vn, mixed))
            cols.append(u[:, sl] * mixed)
        ya = jnp.concatenate(cols, axis=1)
        rstd_a = lax.rsqrt(jnp.mean(ya * ya, axis=-1, keepdims=True) + EPS)
        ya_hat = ya * rstd_a
        dyv = dy_ref[...].astype(F32)
        dgn_ref[...] += jnp.sum(dyv * ya_hat, axis=0, keepdims=True)
        t = dyv * gn_ref[...]
        dya = rstd_a * (t - ya_hat * jnp.mean(t * ya_hat, axis=-1, keepdims=True))
        du_cols, dv_cols, dlg_cols, dlb_cols = [], [], [], []
        for g in range(G):
            sl = slice(g * GROUP_DIM, (g + 1) * GROUP_DIM)
            wm, vhat, rstd, vn, mixed = saved[g]
            dya_g = dya[:, sl]
            du_cols.append(dya_g * mixed)
            dmix = dya_g * u[:, sl]
            dmix_bf = dmix.astype(BF16)
            db_ref[g] += jnp.sum(dmix, axis=1, keepdims=True)
            dw = lax.dot_general(dmix_bf, vn.astype(BF16), (((1,), (1,)), ((), ())),
                                 preferred_element_type=F32)
            dw_ref[g] += jnp.where(mask, dw, 0.0)
            dvn = jnp.dot(wm.T.astype(BF16), dmix_bf, preferred_element_type=F32)
            dlg_cols.append(jnp.sum(dvn * vhat, axis=0, keepdims=True))
            dlb_cols.append(jnp.sum(dvn, axis=0, keepdims=True))
            dvhat = dvn * lg_ref[:, sl]
            dv_cols.append(rstd * (dvhat - jnp.mean(dvhat, axis=-1, keepdims=True)
                                   - vhat * jnp.mean(dvhat * vhat, axis=-1, keepdims=True)))
        dlg_ref[...] += jnp.concatenate(dlg_cols, axis=1)
        dlb_ref[...] += jnp.concatenate(dlb_cols, axis=1)
        dz_ref[:, :W_A] = (jnp.concatenate(du_cols, axis=1) * _gelu_grad(zu)).astype(BF16)
        dz_ref[:, W_A:] = (jnp.concatenate(dv_cols, axis=1) * _gelu_grad(zv)).astype(BF16)

    vec = pl.BlockSpec((1, W_A), lambda i: (0, 0))
    wspec = pl.BlockSpec((G, SGU_BLOCK, SGU_BLOCK), lambda i: (0, 0, 0))
    return pl.pallas_call(
        body, name=name, grid=(T // SGU_BLOCK,),
        in_specs=[pl.BlockSpec((SGU_BLOCK, 2 * W_A), lambda i: (i, 0)),
                  pl.BlockSpec((SGU_BLOCK, W_A), lambda i: (i, 0)), vec, vec, wspec,
                  pl.BlockSpec((SGU_BLOCK, G), lambda i: (0, 0)), vec],
        out_specs=[pl.BlockSpec((SGU_BLOCK, 2 * W_A), lambda i: (i, 0)), vec, vec, wspec,
                   pl.BlockSpec((G, SGU_BLOCK, 1), lambda i: (0, 0, 0)), vec],
        out_shape=[jax.ShapeDtypeStruct((T, 2 * W_A), BF16), jax.ShapeDtypeStruct((1, W_A), F32),
                   jax.ShapeDtypeStruct((1, W_A), F32),
                   jax.ShapeDtypeStruct((G, SGU_BLOCK, SGU_BLOCK), F32),
                   jax.ShapeDtypeStruct((G, SGU_BLOCK, 1), F32),
                   jax.ShapeDtypeStruct((1, W_A), F32)],
        compiler_params=_params(1))(z, dy, ln_g, ln_b, w_s, b_t, gn)


def _split_dot(x, tri):
    hi = x.astype(BF16)
    lo = (x - hi.astype(F32)).astype(BF16)
    return (jnp.dot(hi, tri, preferred_element_type=F32)
            + jnp.dot(lo, tri, preferred_element_type=F32))


def _tri(n, rel):
    r = lax.broadcasted_iota(jnp.int32, (n, n), 0)
    c = lax.broadcasted_iota(jnp.int32, (n, n), 1)
    return rel(r, c).astype(BF16)


def _dot_nt(a, b):
    return lax.dot_general(a, b, (((1,), (1,)), ((), ())), preferred_element_type=F32)


def _dot_tn(a, b):
    return jnp.dot(a.astype(F32).T.astype(BF16), b, preferred_element_type=F32)


def _sb_scores(qs, kj, q0, k0):
    zz = _dot_nt(qs, kj)
    tq, tk = zz.shape
    tpos = q0 + lax.broadcasted_iota(jnp.int32, (tq, tk), 0)
    spos = k0 + lax.broadcasted_iota(jnp.int32, (tq, tk), 1)
    mask = spos < tpos
    log_beta = jnp.minimum(zz, 0.0) - jnp.log(1.0 + jnp.exp(-jnp.abs(zz)))
    log_1m = jnp.where(mask, log_beta - zz, 0.0)
    return log_beta, log_1m, mask


def _sb_cols(w_a, w_b):
    base = 2 * w_a // GROUP_DIM
    per = w_b // GROUP_DIM
    return base, base + per, base + 2 * per


def _sb_forward(name, z, w_a, w_b):
    T = z.shape[0]
    H = w_b // GROUP_DIM
    tb = _pick(256, [T])
    qc, kc, vc = _sb_cols(w_a, w_b)
    scale = GROUP_DIM ** -0.5

    def body(q_ref, k_ref, v_ref, y_ref, tot_ref):
        i = pl.program_id(1)
        qs = (q_ref[...] * scale).astype(BF16)
        upper = _tri(tb, lambda r, c: r > c)

        def step(jj, carry):
            acc, later = carry
            j = i - jj
            k0 = pl.multiple_of(j * tb, tb)
            kj = k_ref[pl.ds(k0, tb), :].astype(BF16)
            vj = v_ref[pl.ds(k0, tb), :].astype(BF16)
            log_beta, log_1m, mask = _sb_scores(qs, kj, i * tb, k0)
            rest = _split_dot(log_1m, upper) + later
            a = jnp.where(mask, jnp.exp(log_beta + rest), 0.0)
            acc = acc + jnp.dot(a.astype(BF16), vj, preferred_element_type=F32)
            return acc, later + jnp.sum(log_1m, axis=1, keepdims=True)

        acc, total = lax.fori_loop(
            0, i + 1, step, (jnp.zeros((tb, GROUP_DIM), F32), jnp.zeros((tb, 1), F32)))
        y_ref[...] = acc
        tot_ref[...] = total

    return pl.pallas_call(
        body, name=name, grid=(H, T // tb),
        in_specs=[pl.BlockSpec((tb, GROUP_DIM), lambda h, i: (i, qc + h)),
                  pl.BlockSpec((T, GROUP_DIM), lambda h, i: (0, kc + h)),
                  pl.BlockSpec((T, GROUP_DIM), lambda h, i: (0, vc + h))],
        out_specs=[pl.BlockSpec((tb, GROUP_DIM), lambda h, i: (i, h)),
                   pl.BlockSpec((None, tb, 1), lambda h, i: (h, i, 0))],
        out_shape=[jax.ShapeDtypeStruct((T, w_b), F32), jax.ShapeDtypeStruct((H, T, 1), F32)],
        compiler_params=_params(2))(z, z, z)


def _sb_backward(name, z, do, total, w_a, w_b):
    T = z.shape[0]
    H = w_b // GROUP_DIM
    tb = _pick(256, [T])
    qc, kc, vc = _sb_cols(w_a, w_b)
    scale = GROUP_DIM ** -0.5

    def body(q_ref, k_ref, v_ref, do_ref, tot_ref, dq_ref, dkv_ref):
        i = pl.program_id(1)

        @pl.when(i == 0)
        def _():
            dkv_ref[...] = jnp.zeros_like(dkv_ref)

        qs = (q_ref[...] * scale).astype(BF16)
        dob = do_ref[...].astype(BF16)
        total_v = tot_ref[...]
        upto = _tri(tb, lambda r, c: r <= c)
        before = _tri(tb, lambda r, c: r < c)

        def step(j, carry):
            dq, seen, e_seen = carry
            k0 = pl.multiple_of(j * tb, tb)
            kj = k_ref[pl.ds(k0, tb), :].astype(BF16)
            vj = v_ref[pl.ds(k0, tb), :].astype(BF16)
            log_beta, log_1m, mask = _sb_scores(qs, kj, i * tb, k0)
            rest = total_v - (seen + _split_dot(log_1m, upto))
            a = jnp.where(mask, jnp.exp(log_beta + rest), 0.0)
            e = a * _dot_nt(dob, vj)
            e_before = e_seen + _split_dot(e, before)
            beta = jnp.exp(log_beta)
            dz = jnp.where(mask, e * (1.0 - beta) - beta * e_before, 0.0).astype(BF16)
            dq = dq + jnp.dot(dz, kj, preferred_element_type=F32)
            dkv_ref[0, pl.ds(k0, tb), :] += _dot_tn(dz, qs)
            dkv_ref[1, pl.ds(k0, tb), :] += _dot_tn(a.astype(BF16), dob)
            return (dq, seen + jnp.sum(log_1m, axis=1, keepdims=True),
                    e_seen + jnp.sum(e, axis=1, keepdims=True))

        zero_col = jnp.zeros((tb, 1), F32)
        dq, _, _ = lax.fori_loop(0, i + 1, step,
                                 (jnp.zeros((tb, GROUP_DIM), F32), zero_col, zero_col))
        dq_ref[...] = (dq * scale).astype(BF16)

    return pl.pallas_call(
        body, name=name, grid=(H, T // tb),
        in_specs=[pl.BlockSpec((tb, GROUP_DIM), lambda h, i: (i, qc + h)),
                  pl.BlockSpec((T, GROUP_DIM), lambda h, i: (0, kc + h)),
                  pl.BlockSpec((T, GROUP_DIM), lambda h, i: (0, vc + h)),
                  pl.BlockSpec((tb, GROUP_DIM), lambda h, i: (i, h)),
                  pl.BlockSpec((None, tb, 1), lambda h, i: (h, i, 0))],
        out_specs=[pl.BlockSpec((tb, GROUP_DIM), lambda h, i: (i, h)),
                   pl.BlockSpec((2, T, GROUP_DIM), lambda h, i: (0, 0, h))],
        out_shape=[jax.ShapeDtypeStruct((T, w_b), BF16), jax.ShapeDtypeStruct((2, T, w_b), F32)],
        compiler_params=_params(2))(z, z, z, do, total)


def _softmax_rows(s):
    m = jnp.max(s, axis=-1, keepdims=True)
    p = jnp.exp(s - m)
    return p / jnp.sum(p, axis=-1, keepdims=True)


def _xattn_forward(name, q, kv):
    T, D = q.shape
    Nm = kv.shape[0]
    dh = D // X_HEADS
    tq = _pick(512, [T])

    def body(q_ref, k_ref, v_ref, o_ref):
        p = _softmax_rows(_dot_nt(q_ref[...], k_ref[...]))
        o_ref[...] = jnp.dot(p.astype(BF16), v_ref[...], preferred_element_type=F32).astype(BF16)

    return pl.pallas_call(
        body, name=name, grid=(T // tq, X_HEADS),
        in_specs=[pl.BlockSpec((tq, dh), lambda i, h: (i, h)),
                  pl.BlockSpec((Nm, dh), lambda i, h: (0, h)),
                  pl.BlockSpec((Nm, dh), lambda i, h: (0, X_HEADS + h))],
        out_specs=pl.BlockSpec((tq, dh), lambda i, h: (i, h)),
        out_shape=jax.ShapeDtypeStruct((T, D), BF16),
        compiler_params=_params(2))(q, kv, kv)


def _xattn_backward(name, q, kv, do):
    T, D = q.shape
    Nm = kv.shape[0]
    dh = D // X_HEADS
    tq = _pick(512, [T])
    scale = dh ** -0.5

    def body(q_ref, k_ref, v_ref, do_ref, dq_ref, dkv_ref):
        @pl.when(pl.program_id(1) == 0)
        def _():
            dkv_ref[...] = jnp.zeros_like(dkv_ref)

        qv, kk, vv, dov = q_ref[...], k_ref[...], v_ref[...], do_ref[...]
        p = _softmax_rows(_dot_nt(qv, kk))
        dp = _dot_nt(dov, vv)
        ds = (p * (dp - jnp.sum(dp * p, axis=-1, keepdims=True))).astype(BF16)
        dq_ref[...] = (jnp.dot(ds, kk, preferred_element_type=F32) * scale).astype(BF16)
        dkv_ref[0] += _dot_tn(ds, qv)
        dkv_ref[1] += _dot_tn(p.astype(BF16), dov)

    blk = pl.BlockSpec((tq, dh), lambda h, i: (i, h))
    return pl.pallas_call(
        body, name=name, grid=(X_HEADS, T // tq),
        in_specs=[blk, pl.BlockSpec((Nm, dh), lambda h, i: (0, h)),
                  pl.BlockSpec((Nm, dh), lambda h, i: (0, X_HEADS + h)), blk],
        out_specs=[blk, pl.BlockSpec((2, Nm, dh), lambda h, i: (0, 0, h))],
        out_shape=[jax.ShapeDtypeStruct((T, D), BF16), jax.ShapeDtypeStruct((2, Nm, D), F32)],
        compiler_params=_params(2))(q, kv, kv, do)


def _position():
    x, y, c = lax.axis_index("x"), lax.axis_index("y"), lax.axis_index("c")
    other_chips = [(1 - x, y), (x, 1 - y), (1 - x, 1 - y)]
    return x, y, c, other_chips


def _half(ref, c, rows):
    return pl.ds(c * rows, rows)


def _gather_weights(shards):
    n = len(shards)

    def body(*refs):
        ins, outs = refs[:n], refs[n:2 * n]
        local_sems, send_sems, recv_sems, fwd_send_sems, fwd_recv_sems = refs[2 * n:]
        x, y, c, chips = _position()
        me = 2 * x + y
        local = [pltpu.make_async_copy(ins[i], outs[i].at[me], local_sems.at[i]) for i in range(n)]
        for cp in local:
            cp.start()
        sends, fwds = [], []
        for i in range(n):
            rows = ins[i].shape[0] // 2
            for j, (px, py) in enumerate(chips):
                cp = pltpu.make_async_remote_copy(
                    src_ref=ins[i].at[pl.ds(c * rows, rows), :],
                    dst_ref=outs[i].at[me, pl.ds(c * rows, rows), :],
                    send_sem=send_sems.at[i, j], recv_sem=recv_sems.at[i, j],
                    device_id=(px, py, c), device_id_type=MESH)
                cp.start()
                sends.append(cp)
        for i in range(n):
            rows = ins[i].shape[0] // 2
            for j, (px, py) in enumerate(chips):
                piece = outs[i].at[2 * px + py, pl.ds(c * rows, rows), :]
                pltpu.make_async_remote_copy(
                    src_ref=piece, dst_ref=piece, send_sem=send_sems.at[i, j],
                    recv_sem=recv_sems.at[i, j], device_id=(px, py, c),
                    device_id_type=MESH).wait_recv()
                cp = pltpu.make_async_remote_copy(
                    src_ref=piece, dst_ref=piece, send_sem=fwd_send_sems.at[i, j],
                    recv_sem=fwd_recv_sems.at[i, j], device_id=(x, y, 1 - c),
                    device_id_type=MESH)
                cp.start()
                fwds.append(cp)
        for i in range(n):
            rows = ins[i].shape[0] // 2
            for j, (px, py) in enumerate(chips):
                piece = outs[i].at[2 * px + py, pl.ds((1 - c) * rows, rows), :]
                pltpu.make_async_remote_copy(
                    src_ref=piece, dst_ref=piece, send_sem=fwd_send_sems.at[i, j],
                    recv_sem=fwd_recv_sems.at[i, j], device_id=(x, y, 1 - c),
                    device_id_type=MESH).wait_recv()
        for cp in sends + fwds:
            cp.wait_send()
        for cp in local:
            cp.wait()

    return pl.pallas_call(
        body, name="gather_weights",
        in_specs=[_any_spec()] * n, out_specs=[_any_spec()] * n,
        out_shape=[jax.ShapeDtypeStruct((N_CHIPS,) + s.shape, s.dtype) for s in shards],
        scratch_shapes=[pltpu.SemaphoreType.DMA((n,))] + [pltpu.SemaphoreType.DMA((n, 3))] * 4,
    )(*shards)


def _swap_halves(grads):
    n = len(grads)

    def body(*refs):
        ins, outs = refs[:n], refs[n:2 * n]
        send_sems, recv_sems = refs[2 * n:]
        x, y, c, _ = _position()
        copies = []
        for i in range(n):
            rows = ins[i].shape[1] // 2
            cp = pltpu.make_async_remote_copy(
                src_ref=ins[i].at[:, pl.ds((1 - c) * rows, rows), :], dst_ref=outs[i],
                send_sem=send_sems.at[i], recv_sem=recv_sems.at[i],
                device_id=(x, y, 1 - c), device_id_type=MESH)
            cp.start()
            copies.append(cp)
        for cp in copies:
            cp.wait()

    return pl.pallas_call(
        body, name="swap_halves",
        in_specs=[_any_spec()] * n, out_specs=[_any_spec()] * n,
        out_shape=[jax.ShapeDtypeStruct((g.shape[0], g.shape[1] // 2, g.shape[2]), g.dtype)
                   for g in grads],
        scratch_shapes=[pltpu.SemaphoreType.DMA((n,))] * 2,
    )(*grads)


def _scatter_panels(sums):
    n = len(sums)

    def body(*refs):
        ins, outs = refs[:n], refs[n:2 * n]
        send_sems, recv_sems = refs[2 * n:]
        x, y, c, chips = _position()
        copies = []
        for i in range(n):
            for j, (px, py) in enumerate(chips):
                cp = pltpu.make_async_remote_copy(
                    src_ref=ins[i].at[2 * px + py], dst_ref=outs[i].at[j],
                    send_sem=send_sems.at[i, j], recv_sem=recv_sems.at[i, j],
                    device_id=(px, py, c), device_id_type=MESH)
                cp.start()
                copies.append(cp)
        for cp in copies:
            cp.wait()

    return pl.pallas_call(
        body, name="scatter_panels",
        in_specs=[_any_spec()] * n, out_specs=[_any_spec()] * n,
        out_shape=[jax.ShapeDtypeStruct((3,) + s.shape[1:], s.dtype) for s in sums],
        scratch_shapes=[pltpu.SemaphoreType.DMA((n, 3))] * 2,
    )(*sums)


def _share_halves(halves):
    n = len(halves)

    def body(*refs):
        ins, outs = refs[:n], refs[n:2 * n]
        local_sems, send_sems, recv_sems = refs[2 * n:]
        x, y, c, _ = _position()
        copies = []
        for i in range(n):
            rows = ins[i].shape[0]
            mine = outs[i].at[pl.ds(c * rows, rows), :]
            lc = pltpu.make_async_copy(ins[i], mine, local_sems.at[i])
            lc.start()
            cp = pltpu.make_async_remote_copy(
                src_ref=ins[i], dst_ref=mine, send_sem=send_sems.at[i], recv_sem=recv_sems.at[i],
                device_id=(x, y, 1 - c), device_id_type=MESH)
            cp.start()
            copies.append((lc, cp, rows))
        for i, (lc, cp, rows) in enumerate(copies):
            theirs = outs[i].at[pl.ds((1 - c) * rows, rows), :]
            pltpu.make_async_remote_copy(
                src_ref=ins[i], dst_ref=theirs, send_sem=send_sems.at[i],
                recv_sem=recv_sems.at[i], device_id=(x, y, 1 - c), device_id_type=MESH).wait_recv()
            cp.wait_send()
            lc.wait()

    return pl.pallas_call(
        body, name="share_halves",
        in_specs=[_any_spec()] * n, out_specs=[_any_spec()] * n,
        out_shape=[jax.ShapeDtypeStruct((2 * h.shape[0], h.shape[1]), h.dtype) for h in halves],
        scratch_shapes=[pltpu.SemaphoreType.DMA((n,))] * 3,
    )(*halves)


def _gather_small(packed):
    def body(in_ref, out_ref, local_sem, send_sems, recv_sems):
        x, y, c, _ = _position()
        me = 4 * x + 2 * y + c
        lc = pltpu.make_async_copy(in_ref, out_ref.at[me], local_sem)
        lc.start()
        copies = []
        for r in range(1, N_DEV):
            fx, fy, fc = (r >> 2) & 1, (r >> 1) & 1, r & 1
            peer = (x ^ fx, y ^ fy, c ^ fc)
            cp = pltpu.make_async_remote_copy(
                src_ref=in_ref, dst_ref=out_ref.at[me], send_sem=send_sems.at[r - 1],
                recv_sem=recv_sems.at[r - 1], device_id=peer, device_id_type=MESH)
            cp.start()
            copies.append(cp)
        for r in range(1, N_DEV):
            fx, fy, fc = (r >> 2) & 1, (r >> 1) & 1, r & 1
            src = 4 * (x ^ fx) + 2 * (y ^ fy) + (c ^ fc)
            pltpu.make_async_remote_copy(
                src_ref=in_ref, dst_ref=out_ref.at[src], send_sem=send_sems.at[r - 1],
                recv_sem=recv_sems.at[r - 1], device_id=(x ^ fx, y ^ fy, c ^ fc),
                device_id_type=MESH).wait_recv()
        for cp in copies:
            cp.wait_send()
        lc.wait()

    return pl.pallas_call(
        body, name="gather_small", in_specs=[_any_spec()], out_specs=_any_spec(),
        out_shape=jax.ShapeDtypeStruct((N_DEV,) + packed.shape, packed.dtype),
        scratch_shapes=[pltpu.SemaphoreType.DMA, pltpu.SemaphoreType.DMA((N_DEV - 1,)),
                        pltpu.SemaphoreType.DMA((N_DEV - 1,))],
    )(packed)


def _block_rows(rows, cols, itemsize=4, target=1 << 20):
    return _pick(max(BF16_ROWS, target // (cols * itemsize)), [rows], unit=BF16_ROWS)


def _pair_sum(name, place, grad, received):
    P, rows, cols = received.shape
    tr = _block_rows(rows, cols)
    nb = rows // tr

    def body(place_ref, g_ref, r_ref, o_ref):
        o_ref[...] = (g_ref[...].astype(F32) + r_ref[...].astype(F32)).astype(BF16)

    grid_spec = pltpu.PrefetchScalarGridSpec(
        num_scalar_prefetch=1, grid=(P, nb),
        in_specs=[pl.BlockSpec((None, tr, cols), lambda p, r, pr: (p, pr[1] * nb + r, 0)),
                  pl.BlockSpec((None, tr, cols), lambda p, r, pr: (p, r, 0))],
        out_specs=pl.BlockSpec((None, tr, cols), lambda p, r, pr: (p, r, 0)))
    return pl.pallas_call(
        body, name=name, grid_spec=grid_spec,
        out_shape=jax.ShapeDtypeStruct(received.shape, BF16),
        compiler_params=_params(2))(place, grad, received)


def _final_sum(name, place, grad, received, from_chips):
    _, rows, cols = received.shape
    tr = _block_rows(rows, cols)
    nb = rows // tr

    def body(place_ref, g_ref, r_ref, c_ref, o_ref):
        acc = g_ref[...].astype(F32) + r_ref[...].astype(F32)
        for j in range(3):
            acc = acc + c_ref[j].astype(F32)
        o_ref[...] = acc

    grid_spec = pltpu.PrefetchScalarGridSpec(
        num_scalar_prefetch=1, grid=(nb,),
        in_specs=[pl.BlockSpec((None, tr, cols), lambda r, pr: (pr[0], pr[1] * nb + r, 0)),
                  pl.BlockSpec((None, tr, cols), lambda r, pr: (pr[0], r, 0)),
                  pl.BlockSpec((3, tr, cols), lambda r, pr: (0, r, 0))],
        out_specs=pl.BlockSpec((tr, cols), lambda r, pr: (r, 0)))
    return pl.pallas_call(
        body, name=name, grid_spec=grid_spec,
        out_shape=jax.ShapeDtypeStruct((rows, cols), F32),
        compiler_params=_params(1))(place, grad, received, from_chips)


def _sum_devices(name, gathered):
    n_dev, rows, cols = gathered.shape
    tr = _pick(256, [rows])

    def body(g_ref, o_ref):
        acc = g_ref[0]
        for d in range(1, n_dev):
            acc = acc + g_ref[d]
        o_ref[...] = acc

    return pl.pallas_call(
        body, name=name, grid=(rows // tr,),
        in_specs=[pl.BlockSpec((n_dev, tr, cols), lambda r: (0, r, 0))],
        out_specs=pl.BlockSpec((tr, cols), lambda r: (r, 0)),
        out_shape=jax.ShapeDtypeStruct((rows, cols), F32),
        compiler_params=_params(1))(gathered)


def _adamw(name, w, g, m, v):
    rows, cols = w.shape
    tr = _block_rows(rows, cols)
    c1 = 1.0 / (1.0 - ADAM_B1 ** ADAM_STEP)
    c2 = 1.0 / (1.0 - ADAM_B2 ** ADAM_STEP)

    def body(w_ref, g_ref, m_ref, v_ref, d_ref, nm_ref, nv_ref):
        gv = g_ref[...]
        nm = ADAM_B1 * m_ref[...] + (1.0 - ADAM_B1) * gv
        nv = ADAM_B2 * v_ref[...] + (1.0 - ADAM_B2) * (gv * gv)
        nm_ref[...] = nm
        nv_ref[...] = nv
        d_ref[...] = -ADAM_LR * ((nm * c1) / (jnp.sqrt(nv * c2) + ADAM_EPS) + ADAM_WD * w_ref[...])

    blk = pl.BlockSpec((tr, cols), lambda r: (r, 0))
    shape = jax.ShapeDtypeStruct((rows, cols), F32)
    return pl.pallas_call(
        body, name=name, grid=(rows // tr,), in_specs=[blk] * 4, out_specs=[blk] * 3,
        out_shape=[shape] * 3, compiler_params=_params(1))(w, g, m, v)


BIG = ("ffn1_w_in", "ffn1_w_out", "w_mix_in", "w_mix_out", "w_cq", "w_ckv", "w_co",
       "ffn2_w_in", "ffn2_w_out")
BIG_KIND = {"ffn1_w_in": "c", "ffn1_w_out": "r", "w_mix_in": "c", "w_mix_out": "r", "w_cq": "r",
            "w_ckv": "c", "w_co": "r", "ffn2_w_in": "c", "ffn2_w_out": "r"}
SMALL = ("ffn1_norm", "mix_norm", "ln_v_gain", "ln_v_bias", "spatial_w", "spatial_b", "gnorm_a",
         "gnorm_b", "cross_norm", "mem_norm", "ffn2_norm", "final_norm")
WEIGHTS = ("ffn1_norm", "ffn1_w_in", "ffn1_w_out", "mix_norm", "w_mix_in", "ln_v_gain",
           "ln_v_bias", "spatial_w", "spatial_b", "gnorm_a", "gnorm_b", "w_mix_out", "cross_norm",
           "mem_norm", "w_cq", "w_ckv", "w_co", "ffn2_norm", "ffn2_w_in", "ffn2_w_out",
           "final_norm")


def _pack(arrays):
    return jnp.concatenate([a.reshape(-1, LANE) for a in arrays], axis=0)


def _unpack(packed, like):
    out, row = [], 0
    for a in like:
        rows = a.size // LANE
        out.append(packed[row:row + rows].reshape(a.shape))
        row += rows
    return out


def _local_step(x, mem, target, small, big):
    T, D = x.shape
    vec = lambda name: small[name].reshape(1, -1)
    w_a = small["ln_v_gain"].size
    w_b = small["gnorm_b"].size
    G = w_a // GROUP_DIM
    w_s = small["spatial_w"].reshape(G, SGU_BLOCK, SGU_BLOCK)
    b_t = small["spatial_b"].reshape(G, SGU_BLOCK).T

    h1, ffn1_saved = _ffn_forward("ffn1", x, vec("ffn1_norm"), big["ffn1_w_in"], big["ffn1_w_out"])
    n2 = _rmsnorm_fwd("mix_norm", h1, vec("mix_norm"))
    (z,) = _matmul("mix_in", Mat(n2), big["w_mix_in"], "nn", [("c", 1, F32)], tm=2048, tn=256)
    z = z[0]
    y = _sgu_forward("sgu", z, vec("ln_v_gain"), vec("ln_v_bias"), w_s, b_t, vec("gnorm_a"), D)
    yb, sb_total = _sb_forward("stickbreak", z, w_a, w_b)
    y = _rmsnorm_fwd("gnorm_b", yb, vec("gnorm_b"), into=y, col=w_a // w_b)

    def add_res(acc, ex, out):
        out[0][...] = ex[0][...] + acc

    (h2,) = _matmul("mix_out", Mat(y), big["w_mix_out"], "nn", [("c", 1, F32)],
                    tm=1024, tn=1024, extras=[Mat(h1)], epi=add_res)
    h2 = h2[0]
    n3 = _rmsnorm_fwd("cross_norm", h2, vec("cross_norm"))
    memn = _rmsnorm_fwd("mem_norm", mem, vec("mem_norm"))
    x_scale = (D // X_HEADS) ** -0.5

    def scaled(acc, ex, out):
        out[0][...] = (acc * x_scale).astype(BF16)

    (q,) = _matmul("cross_q", Mat(n3), big["w_cq"], "nn", [("c", 1, BF16)],
                   tm=1024, tn=1024, epi=scaled)
    (kv,) = _matmul("cross_kv", Mat(memn), big["w_ckv"], "nn", [("c", 1, BF16)], tm=256, tn=1024)
    q, kv = q[0], kv[0]
    o = _xattn_forward("cross_attn", q, kv)
    (h3,) = _matmul("cross_out", Mat(o), big["w_co"], "nn", [("c", 1, F32)],
                    tm=1024, tn=1024, extras=[Mat(h2)], epi=add_res)
    h3 = h3[0]
    h4, ffn2_saved = _ffn_forward("ffn2", h3, vec("ffn2_norm"), big["ffn2_w_in"], big["ffn2_w_out"])

    gs, gb = {}, {}
    loss_tile, dh4, dh4_bf, gs["final_norm"] = _loss_head("loss_head", h4, vec("final_norm"), target)
    dh3, dh3_bf, gs["ffn2_norm"], gb["ffn2_w_in"], gb["ffn2_w_out"] = _ffn_backward(
        "ffn2", h3, vec("ffn2_norm"), big["ffn2_w_in"], big["ffn2_w_out"], ffn2_saved, dh4, dh4_bf)

    (do,) = _matmul("cross_do", Mat(dh3_bf), big["w_co"], "nt", [("c", 1, BF16)], tm=1024, tn=512)
    (gb["w_co"],) = _matmul("cross_dwo", Mat(o), Mat(dh3_bf), "tn", [("r", N_CHIPS, BF16)],
                            tm=512, tn=1024)
    dq, dkv = _xattn_backward("cross_attn_bwd", q, kv, do[0])
    (dn3,) = _matmul("cross_dn", Mat(dq), big["w_cq"], "nt", [("c", 1, F32)], tm=1024, tn=512)
    (gb["w_cq"],) = _matmul("cross_dwq", Mat(n3), Mat(dq), "tn", [("r", N_CHIPS, BF16)],
                            tm=512, tn=1024)
    dkv_m = Mat(dkv)
    (dmemn,) = _matmul("cross_dmem", dkv_m, big["w_ckv"], "nt", [("c", 1, F32)],
                       tm=256, tn=1024, tk=1024)
    (gb["w_ckv"],) = _matmul("cross_dwkv", Mat(memn), dkv_m, "tn", [("c", N_CHIPS, BF16)],
                             tm=1024, tn=1024)
    (gs["mem_norm"],) = _rmsnorm_bwd("mem_dnorm", mem, vec("mem_norm"), dmemn[0], want_dx=False)
    dh2, dh2_bf, gs["cross_norm"] = _rmsnorm_bwd("cross_dnorm", h2, vec("cross_norm"), dn3[0],
                                                 dres=dh3)

    (dy,) = _matmul("mix_dy", Mat(dh2_bf), big["w_mix_out"], "nt", [("c", 1, F32)], tm=1024, tn=512)
    dy = dy[0]
    (gb["w_mix_out"],) = _matmul("mix_dwout", Mat(y), Mat(dh2_bf), "tn", [("r", N_CHIPS, BF16)],
                                 tm=512, tn=1024)
    dza, gs["ln_v_gain"], gs["ln_v_bias"], gs["spatial_w"], db, gs["gnorm_a"] = _sgu_backward(
        "sgu_bwd", z, dy, vec("ln_v_gain"), vec("ln_v_bias"), w_s, b_t, vec("gnorm_a"))
    gs["spatial_b"] = db.reshape(G, SGU_BLOCK)
    dob, gs["gnorm_b"] = _rmsnorm_bwd("gnorm_b_bwd", yb, vec("gnorm_b"), dy, dn_col=w_a // w_b,
                                      want_bf16=False)
    dqb, dkvb = _sb_backward("stickbreak_bwd", z, dob, sb_total, w_a, w_b)
    dz = jnp.concatenate([dza, dqb, dkvb[0].astype(BF16), dkvb[1].astype(BF16)], axis=1)
    dz_m = Mat(dz)
    (dn2,) = _matmul("mix_dn", dz_m, big["w_mix_in"], "nt", [("c", 1, F32)],
                     tm=1024, tn=1024, tk=1280)
    (gb["w_mix_in"],) = _matmul("mix_dwin", Mat(n2), dz_m, "tn", [("c", N_CHIPS, BF16)],
                                tm=1024, tn=1280)
    dh1, dh1_bf, gs["mix_norm"] = _rmsnorm_bwd("mix_dnorm", h1, vec("mix_norm"), dn2[0], dres=dh2)

    dx, _, gs["ffn1_norm"], gb["ffn1_w_in"], gb["ffn1_w_out"] = _ffn_backward(
        "ffn1", x, vec("ffn1_norm"), big["ffn1_w_in"], big["ffn1_w_out"], ffn1_saved, dh1, dh1_bf)
    gs = {k: g.reshape(small[k].shape) for k, g in gs.items()}
    return loss_tile, dx, gs, gb


def kernel(x, mem, ffn1_norm, ffn1_w_in, ffn1_w_out, mix_norm, w_mix_in, ln_v_gain, ln_v_bias, spatial_w, spatial_b, gnorm_a, gnorm_b, w_mix_out, cross_norm, mem_norm, w_cq, w_ckv, w_co, ffn2_norm, ffn2_w_in, ffn2_w_out, final_norm, loss_target, m_ffn1_norm, m_ffn1_w_in, m_ffn1_w_out, m_mix_norm, m_w_mix_in, m_ln_v_gain, m_ln_v_bias, m_spatial_w, m_spatial_b, m_gnorm_a, m_gnorm_b, m_w_mix_out, m_cross_norm, m_mem_norm, m_w_cq, m_w_ckv, m_w_co, m_ffn2_norm, m_ffn2_w_in, m_ffn2_w_out, m_final_norm, v_ffn1_norm, v_ffn1_w_in, v_ffn1_w_out, v_mix_norm, v_w_mix_in, v_ln_v_gain, v_ln_v_bias, v_spatial_w, v_spatial_b, v_gnorm_a, v_gnorm_b, v_w_mix_out, v_cross_norm, v_mem_norm, v_w_cq, v_w_ckv, v_w_co, v_ffn2_norm, v_ffn2_w_in, v_ffn2_w_out, v_final_norm):
    given = dict(locals())
    w = {k: given[k] for k in WEIGHTS}
    m = {k: given["m_" + k] for k in WEIGHTS}
    v = {k: given["v_" + k] for k in WEIGHTS}

    shards = [w[k][0].astype(BF16) for k in BIG]
    gathered = _gather_weights(shards)
    big = {k: Mat(g, BIG_KIND[k]) for k, g in zip(BIG, gathered)}
    small = {k: w[k] for k in SMALL}

    loss_tile, grad_x, gs, gb = _local_step(x[0], mem[0], loss_target[0], small, big)

    cx, cy, cc = lax.axis_index("x"), lax.axis_index("y"), lax.axis_index("c")
    place = jnp.stack([2 * cx + cy, cc]).astype(jnp.int32)
    partial = [gb[k] for k in BIG]
    from_sibling = _swap_halves(partial)
    sums = [_pair_sum(f"pair_sum_{k}", place, g, r) for k, g, r in zip(BIG, partial, from_sibling)]
    from_chips = _scatter_panels(sums)
    halves = [_final_sum(f"final_sum_{k}", place, g, r, f)
              for k, g, r, f in zip(BIG, partial, from_sibling, from_chips)]
    grads = dict(zip(BIG, _share_halves(halves)))

    packed = _pack([gs[k] for k in SMALL] + [loss_tile])
    total = _sum_devices("sum_small", _gather_small(packed))
    n_small = total.shape[0] - SUBLANE
    loss = total[n_small, 0]
    small_g = total[:n_small]
    d_s, m_s, v_s = _adamw("adamw_small", _pack([w[k] for k in SMALL]), small_g,
                           _pack([m[k] for k in SMALL]), _pack([v[k] for k in SMALL]))
    like = [w[k] for k in SMALL]
    grad, delta, new_m, new_v = {}, {}, {}, {}
    for k, g_, d_, m_, v_ in zip(SMALL, _unpack(small_g, like), _unpack(d_s, like),
                                 _unpack(m_s, like), _unpack(v_s, like)):
        grad[k], delta[k], new_m[k], new_v[k] = g_, d_, m_, v_
    for k in BIG:
        d_, m_, v_ = _adamw(f"adamw_{k}", w[k][0], grads[k], m[k][0], v[k][0])
        grad[k], delta[k], new_m[k], new_v[k] = grads[k][None], d_[None], m_[None], v_[None]

    return (loss, grad_x[None], *[grad[k] for k in WEIGHTS], *[delta[k] for k in WEIGHTS],
            *[new_m[k] for k in WEIGHTS], *[new_v[k] for k in WEIGHTS])
```

```python
import functools
import math

import jax
import jax.numpy as jnp
from jax import lax
from jax.experimental import pallas as pl
from jax.experimental.pallas import tpu as pltpu

F32 = jnp.float32
BF16 = jnp.bfloat16
MESH = pl.DeviceIdType.MESH

EPS = 1e-6
CHUNK = 64
SGU_BLOCK = 128
GROUP_DIM = 128
X_HEADS = 4
N_CHIPS = 4
N_DEV = 8
LANE = 128
SUBLANE = 8
BF16_ROWS = 16

ADAM_LR = 0.001
ADAM_B1 = 0.9
ADAM_B2 = 0.999
ADAM_EPS = 1e-08
ADAM_WD = 0.01
ADAM_STEP = 10

V7X_VMEM_BYTES = 64 << 20
VMEM_LIMIT = V7X_VMEM_BYTES - (8 << 20)


def _params(n_grid):
    return pltpu.CompilerParams(dimension_semantics=("arbitrary",) * n_grid,
                                vmem_limit_bytes=VMEM_LIMIT)


def _pick(pref, dims, unit=None):
    g = functools.reduce(math.gcd, dims)
    if unit is None:
        unit = LANE if g % LANE == 0 else SUBLANE
    cands = [d for d in range(unit, g + 1, unit) if g % d == 0] or [g]
    return min(cands, key=lambda d: abs(math.log(d / pref)))


def _any_spec():
    return pl.BlockSpec(memory_space=pl.ANY)


class Mat:
    def __init__(self, arr, kind="c"):
        if arr.ndim == 2:
            arr = arr[None]
        self.arr, self.kind = arr, kind
        self.P, self.prow, self.pcol = arr.shape
        self.rows = self.prow * (self.P if kind == "r" else 1)
        self.cols = self.pcol * (self.P if kind == "c" else 1)
        self.dtype = arr.dtype

    def spec(self, tr, tc, rc_fn):
        if self.kind == "c":
            per = self.pcol // tc
            assert per * tc == self.pcol, (self.pcol, tc)

            def imap(*g):
                i, j = rc_fn(*g)
                return (j // per, i, j % per)
        else:
            per = self.prow // tr
            assert per * tr == self.prow, (self.prow, tr)

            def imap(*g):
                i, j = rc_fn(*g)
                return (i // per, i % per, j)
        return pl.BlockSpec((None, tr, tc), imap)

    def two_d(self):
        assert self.P == 1
        return self.arr[0]


def _out_mat(kind, P, rows, cols, dtype):
    shape = (P, rows, cols // P) if kind == "c" else (P, rows // P, cols)
    return jax.ShapeDtypeStruct(shape, dtype)


def _matmul(name, A, B, mode, outs, *, tm=1024, tn=1024, tk=2048, extras=(), epi=None):
    if mode == "nn":
        M, K, N = A.rows, A.cols, B.cols
        assert B.rows == K
    elif mode == "nt":
        M, K, N = A.rows, A.cols, B.rows
        assert B.cols == K
    else:
        K, M, N = A.rows, A.cols, B.cols
        assert B.rows == K
    mdims, ndims, kdims = [M], [N], [K]
    if mode == "tn":
        assert A.kind == "c" and B.kind == "c"
        mdims.append(A.pcol)
        ndims.append(B.pcol)
    else:
        (mdims if A.kind == "r" else kdims).append(A.prow if A.kind == "r" else A.pcol)
        if mode == "nn":
            (kdims if B.kind == "r" else ndims).append(B.prow if B.kind == "r" else B.pcol)
        else:
            (ndims if B.kind == "r" else kdims).append(B.prow if B.kind == "r" else B.pcol)
    for o in list(outs) + list(extras):
        if isinstance(o, Mat):
            (mdims if o.kind == "r" else ndims).append(o.prow if o.kind == "r" else o.pcol)
        elif isinstance(o[0], str):
            (mdims if o[0] == "r" else ndims).append((M if o[0] == "r" else N) // o[1])
    tm, tn = _pick(tm, mdims), _pick(tn, ndims)
    tk = K if mode == "tn" else _pick(tk, kdims)
    nk = K // tk
    grid = (M // tm, N // tn, nk)

    if mode == "tn":
        a_spec = A.spec(K, tm, lambda m, n, k: (0, m))
        b_spec = B.spec(K, tn, lambda m, n, k: (0, n))
    else:
        a_spec = A.spec(tm, tk, lambda m, n, k: (m, k))
        if mode == "nn":
            b_spec = B.spec(tk, tn, lambda m, n, k: (k, n))
        else:
            b_spec = B.spec(tn, tk, lambda m, n, k: (n, k))

    def mn_spec(o):
        if isinstance(o, Mat):
            return o.spec(tm, tn, lambda m, n, k: (m, n))
        if isinstance(o[0], str):
            kind, P = o[0], o[1]
            fake = Mat.__new__(Mat)
            fake.kind, fake.P = kind, P
            fake.prow = M // P if kind == "r" else M
            fake.pcol = N // P if kind == "c" else N
            return Mat.spec(fake, tm, tn, lambda m, n, k: (m, n))
        return o[1](tm, tn)

    out_shapes = tuple(_out_mat(o[0], o[1], M, N, o[2]) if isinstance(o[0], str) else o[0]
                       for o in outs)
    out_specs = tuple(mn_spec(o) for o in outs)
    extra_arrays = tuple(e.arr if isinstance(e, Mat) else e[0] for e in extras)
    extra_specs = tuple(mn_spec(e) for e in extras)
    n_ex, n_out = len(extras), len(outs)
    tt = _pick(256, [tm])
    dims = (((1,), (1 if mode == "nt" else 0,)), ((), ()))

    def body(*refs):
        a_ref, b_ref = refs[:2]
        ex_refs = refs[2:2 + n_ex]
        out_refs = refs[2 + n_ex:2 + n_ex + n_out]
        scratch = refs[2 + n_ex + n_out:]
        if mode == "tn":
            at_ref = scratch[0]

            @pl.when(pl.program_id(1) == 0)
            def _():
                for c0 in range(0, tm, tt):
                    at_ref[c0:c0 + tt, :] = a_ref[:, c0:c0 + tt].astype(F32).T.astype(BF16)

            lhs = at_ref[...]
        else:
            lhs = a_ref[...].astype(BF16)
        part = lax.dot_general(lhs, b_ref[...].astype(BF16), dims, preferred_element_type=F32)

        def finish(acc):
            if epi is None:
                out_refs[0][...] = acc.astype(out_refs[0].dtype)
            else:
                epi(acc, ex_refs, out_refs)

        if nk == 1:
            finish(part)
        else:
            acc_ref = scratch[0]
            k = pl.program_id(2)

            @pl.when(k == 0)
            def _():
                acc_ref[...] = part

            @pl.when(k > 0)
            def _():
                acc_ref[...] += part

            @pl.when(k == nk - 1)
            def _():
                finish(acc_ref[...])

    scratch_shapes = []
    if mode == "tn":
        scratch_shapes.append(pltpu.VMEM((tm, K), BF16))
    elif nk > 1:
        scratch_shapes.append(pltpu.VMEM((tm, tn), F32))
    res = pl.pallas_call(
        body, name=name, grid=grid,
        in_specs=[a_spec, b_spec, *extra_specs], out_specs=out_specs, out_shape=out_shapes,
        scratch_shapes=scratch_shapes, compiler_params=_params(3),
    )(A.arr, B.arr, *extra_arrays)
    return res


def _row_tile(T):
    return _pick(256, [T])


def _tie(name, x, deps):
    def body(*refs):
        refs[-1][...] = jnp.zeros_like(refs[-1])

    return pl.pallas_call(
        body, name=name, in_specs=[_any_spec()] * (1 + len(deps)),
        out_specs=(_any_spec(), pl.BlockSpec(memory_space=pltpu.VMEM)),
        out_shape=(jax.ShapeDtypeStruct(x.shape, x.dtype),
                   jax.ShapeDtypeStruct((SUBLANE, LANE), F32)),
        input_output_aliases={0: 0},
    )(x, *deps)[0]


def _rmsnorm_fwd(name, x, g, *, into=None, col=0, deps=()):
    T, W = x.shape
    tr = _row_tile(T)

    def body(x_ref, g_ref, *rest):
        o_ref = rest[-1]
        xv = x_ref[...]
        rstd = lax.rsqrt(jnp.mean(xv * xv, axis=-1, keepdims=True) + EPS)
        o_ref[...] = (xv * rstd * g_ref[...]).astype(o_ref.dtype)

    in_specs = [pl.BlockSpec((tr, W), lambda i: (i, 0)), pl.BlockSpec((1, W), lambda i: (0, 0))]
    args = [x, g]
    kwargs = {}
    if into is None:
        out_shape = jax.ShapeDtypeStruct((T, W), BF16)
    else:
        out_shape = jax.ShapeDtypeStruct(into.shape, into.dtype)
        in_specs.append(_any_spec())
        args.append(into)
        kwargs["input_output_aliases"] = {2: 0}
    in_specs += [_any_spec()] * len(deps)
    args += list(deps)
    return pl.pallas_call(
        body, name=name, grid=(T // tr,), in_specs=in_specs,
        out_specs=pl.BlockSpec((tr, W), lambda i: (i, col)), out_shape=out_shape,
        compiler_params=_params(1), **kwargs)(*args)


def _rmsnorm_bwd(name, x, g, dn, *, dn_col=0, dres=None, want_dx=True, want_bf16=True):
    T, W = x.shape
    tr = _row_tile(T)
    has_res = dres is not None

    def body(*refs):
        x_ref, g_ref, dn_ref = refs[:3]
        pos = 3
        dres_ref = None
        if has_res:
            dres_ref = refs[pos]
            pos += 1
        outs = refs[pos:]
        dg_ref = outs[-1]
        xv = x_ref[...]
        rstd = lax.rsqrt(jnp.mean(xv * xv, axis=-1, keepdims=True) + EPS)
        xhat = xv * rstd
        dnv = dn_ref[...].astype(F32)

        @pl.when(pl.program_id(0) == 0)
        def _():
            dg_ref[...] = jnp.zeros_like(dg_ref)

        dg_ref[...] += jnp.sum(dnv * xhat, axis=0, keepdims=True)
        if want_dx:
            t = dnv * g_ref[...]
            dx = rstd * (t - xhat * jnp.mean(t * xhat, axis=-1, keepdims=True))
            if has_res:
                dx = dx + dres_ref[...]
            outs[0][...] = dx
            if want_bf16:
                outs[1][...] = dx.astype(BF16)

    row = pl.BlockSpec((tr, W), lambda i: (i, 0))
    in_specs = [row, pl.BlockSpec((1, W), lambda i: (0, 0)),
                pl.BlockSpec((tr, W), lambda i: (i, dn_col))]
    args = [x, g, dn]
    if has_res:
        in_specs.append(row)
        args.append(dres)
    out_shape, out_specs = [], []
    if want_dx:
        out_shape.append(jax.ShapeDtypeStruct((T, W), F32))
        out_specs.append(row)
        if want_bf16:
            out_shape.append(jax.ShapeDtypeStruct((T, W), BF16))
            out_specs.append(row)
    out_shape.append(jax.ShapeDtypeStruct((1, W), F32))
    out_specs.append(pl.BlockSpec((1, W), lambda i: (0, 0)))
    return pl.pallas_call(
        body, name=name, grid=(T // tr,), in_specs=in_specs, out_specs=out_specs,
        out_shape=out_shape, compiler_params=_params(1))(*args)


def _loss_head(name, h, g, target):
    T, W = h.shape
    tr = _row_tile(T)

    def body(h_ref, g_ref, t_ref, loss_ref, dx_ref, dxb_ref, dg_ref):
        xv = h_ref[...]
        gv = g_ref[...]
        rstd = lax.rsqrt(jnp.mean(xv * xv, axis=-1, keepdims=True) + EPS)
        xhat = xv * rstd
        diff = xhat * gv - t_ref[...]

        @pl.when(pl.program_id(0) == 0)
        def _():
            dg_ref[...] = jnp.zeros_like(dg_ref)
            loss_ref[...] = jnp.zeros_like(loss_ref)

        loss_ref[...] += 0.5 * jnp.sum(jnp.mean(diff * diff, axis=-1, keepdims=True))
        dnv = diff * (1.0 / W)
        dg_ref[...] += jnp.sum(dnv * xhat, axis=0, keepdims=True)
        t = dnv * gv
        dx = rstd * (t - xhat * jnp.mean(t * xhat, axis=-1, keepdims=True))
        dx_ref[...] = dx
        dxb_ref[...] = dx.astype(BF16)

    row = pl.BlockSpec((tr, W), lambda i: (i, 0))
    vec = pl.BlockSpec((1, W), lambda i: (0, 0))
    return pl.pallas_call(
        body, name=name, grid=(T // tr,), in_specs=[row, vec, row],
        out_specs=[pl.BlockSpec((SUBLANE, LANE), lambda i: (0, 0)), row, row, vec],
        out_shape=[jax.ShapeDtypeStruct((SUBLANE, LANE), F32), jax.ShapeDtypeStruct((T, W), F32),
                   jax.ShapeDtypeStruct((T, W), BF16), jax.ShapeDtypeStruct((1, W), F32)],
        compiler_params=_params(1))(h, g, target)


def _sigmoid(x):
    return 1.0 / (1.0 + jnp.exp(-x))


def _ffn_in(name, n, W):
    T, D = n.shape
    F = W.cols // 2
    tm = _pick(2048, [T])
    tn = _pick(512, [W.pcol])
    per = W.pcol // tn

    def body(a_ref, wg_ref, wu_ref, gu_ref, act_ref):
        a = a_ref[...]
        gate = jnp.dot(a, wg_ref[...], preferred_element_type=F32)
        up = jnp.dot(a, wu_ref[...], preferred_element_type=F32)
        gu_ref[0] = gate.astype(BF16)
        gu_ref[1] = up.astype(BF16)
        act_ref[...] = (gate * _sigmoid(gate) * up).astype(BF16)

    return pl.pallas_call(
        body, name=name, grid=(T // tm, F // tn),
        in_specs=[pl.BlockSpec((tm, D), lambda m, j: (m, 0)),
                  pl.BlockSpec((None, D, tn), lambda m, j: (j // per, 0, j % per)),
                  pl.BlockSpec((None, D, tn), lambda m, j: (2 + j // per, 0, j % per))],
        out_specs=[pl.BlockSpec((2, tm, tn), lambda m, j: (0, m, j)),
                   pl.BlockSpec((tm, tn), lambda m, j: (m, j))],
        out_shape=[jax.ShapeDtypeStruct((2, T, F), BF16), jax.ShapeDtypeStruct((T, F), BF16)],
        compiler_params=_params(2))(n, W.arr, W.arr)


def _ffn_forward(tag, h, norm_g, weights_of, deps=()):
    n = _rmsnorm_fwd(f"{tag}_norm", h, norm_g, deps=deps)
    w_in = weights_of(f"{tag}_in", n)[f"{tag}_w_in"]
    gu, act = _ffn_in(f"{tag}_in", n, w_in)
    w_out = weights_of(f"{tag}_out", act)[f"{tag}_w_out"]

    def epi(acc, ex, out):
        out[0][...] = ex[0][...] + 0.5 * acc

    (h_out,) = _matmul(f"{tag}_out", Mat(act), w_out, "nn", [("c", 1, F32)],
                       tm=1024, tn=1024, tk=1408, extras=[Mat(h)], epi=epi)
    return h_out[0], (n, gu, act, w_in, w_out)


def _ffn_backward(tag, h_in, norm_g, saved, dh, dh_bf, grads_done):
    n, gu, act, w_in, w_out = saved
    T, F = act.shape

    def epi(acc, ex, out):
        dact = 0.5 * acc
        gate = ex[0][0].astype(F32)
        up = ex[0][1].astype(F32)
        sig = _sigmoid(gate)
        out[0][0] = (dact * up * sig * (1.0 + gate * (1.0 - sig))).astype(BF16)
        out[0][1] = (dact * gate * sig).astype(BF16)

    def pair_spec(tm, tn):
        return pl.BlockSpec((2, tm, tn), lambda m, j, k: (0, m, j))

    (dgu,) = _matmul(f"{tag}_dact", Mat(dh_bf), w_out, "nt",
                     [(jax.ShapeDtypeStruct((2, T, F), BF16), pair_spec)],
                     tm=512, tn=1408, extras=[(gu, pair_spec)], epi=epi)

    def half(acc, ex, out):
        out[0][...] = (0.5 * acc).astype(out[0].dtype)

    (dw_out,) = _matmul(f"{tag}_dwout", Mat(act), Mat(dh_bf), "tn", [("r", N_CHIPS, BF16)],
                        tm=1408, tn=512, epi=half)
    (dw_in,) = _matmul(f"{tag}_dwin", Mat(n), Mat(dgu), "tn", [("c", N_CHIPS, BF16)],
                       tm=512, tn=1408)
    token = grads_done(tag, {f"{tag}_w_in": dw_in, f"{tag}_w_out": dw_out})
    dgu = _tie(f"{tag}_dgu_after_scatter", dgu, [token])
    (dn,) = _matmul(f"{tag}_dn", Mat(dgu), w_in, "nt", [("c", 1, F32)],
                    tm=1024, tn=1024, tk=2816)
    return _rmsnorm_bwd(f"{tag}_dnorm", h_in, norm_g, dn[0], dres=dh)


_GELU_C = math.sqrt(2.0 / math.pi)
_GELU_A = 0.044715


def _gelu(x):
    return 0.5 * x * (1.0 + jnp.tanh(_GELU_C * (x + _GELU_A * x * x * x)))


def _gelu_grad(x):
    th = jnp.tanh(_GELU_C * (x + _GELU_A * x * x * x))
    return 0.5 * (1.0 + th) + 0.5 * x * (1.0 - th * th) * _GELU_C * (1.0 + 3.0 * _GELU_A * x * x)


def _chunk_mask():
    t = lax.broadcasted_iota(jnp.int32, (SGU_BLOCK, SGU_BLOCK), 0) // CHUNK
    s = lax.broadcasted_iota(jnp.int32, (SGU_BLOCK, SGU_BLOCK), 1) // CHUNK
    return s <= t


def _sgu_group_forward(v_g, lg, lb, wm_bf, b_col):
    mu = jnp.mean(v_g, axis=-1, keepdims=True)
    xc = v_g - mu
    rstd = lax.rsqrt(jnp.mean(xc * xc, axis=-1, keepdims=True) + EPS)
    vhat = xc * rstd
    vn = vhat * lg + lb
    mixed = jnp.dot(wm_bf, vn.astype(BF16), preferred_element_type=F32) + b_col
    return vhat, rstd, vn, mixed


def _sgu_forward(name, z, ln_g, ln_b, w_s, b_t, gn, d_model):
    T = z.shape[0]
    W_A = ln_g.shape[1]
    G = W_A // GROUP_DIM

    def body(z_ref, lg_ref, lb_ref, w_ref, bt_ref, gn_ref, y_ref):
        mask = _chunk_mask()
        u = _gelu(z_ref[:, :W_A])
        v = _gelu(z_ref[:, W_A:])
        cols = []
        for g in range(G):
            sl = slice(g * GROUP_DIM, (g + 1) * GROUP_DIM)
            wm = jnp.where(mask, w_ref[g], 0.0).astype(BF16)
            _, _, _, mixed = _sgu_group_forward(v[:, sl], lg_ref[:, sl], lb_ref[:, sl], wm,
                                                bt_ref[:, g:g + 1])
            cols.append(u[:, sl] * mixed)
        ya = jnp.concatenate(cols, axis=1)
        rstd = lax.rsqrt(jnp.mean(ya * ya, axis=-1, keepdims=True) + EPS)
        y_ref[...] = (ya * rstd * gn_ref[...]).astype(BF16)

    vec = pl.BlockSpec((1, W_A), lambda i: (0, 0))
    return pl.pallas_call(
        body, name=name, grid=(T // SGU_BLOCK,),
        in_specs=[pl.BlockSpec((SGU_BLOCK, 2 * W_A), lambda i: (i, 0)), vec, vec,
                  pl.BlockSpec((G, SGU_BLOCK, SGU_BLOCK), lambda i: (0, 0, 0)),
                  pl.BlockSpec((SGU_BLOCK, G), lambda i: (0, 0)), vec],
        out_specs=pl.BlockSpec((SGU_BLOCK, W_A), lambda i: (i, 0)),
        out_shape=jax.ShapeDtypeStruct((T, d_model), BF16),
        compiler_params=_params(1))(z, ln_g, ln_b, w_s, b_t, gn)


def _sgu_backward(name, z, dy, ln_g, ln_b, w_s, b_t, gn):
    T = z.shape[0]
    W_A = ln_g.shape[1]
    G = W_A // GROUP_DIM

    def body(z_ref, dy_ref, lg_ref, lb_ref, w_ref, bt_ref, gn_ref,
             dz_ref, dlg_ref, dlb_ref, dw_ref, db_ref, dgn_ref):
        @pl.when(pl.program_id(0) == 0)
        def _():
            for r in (dlg_ref, dlb_ref, dw_ref, db_ref, dgn_ref):
                r[...] = jnp.zeros_like(r)

        mask = _chunk_mask()
        zu = z_ref[:, :W_A]
        zv = z_ref[:, W_A:]
        u = _gelu(zu)
        v = _gelu(zv)
        saved, cols = [], []
        for g in range(G):
            sl = slice(g * GROUP_DIM, (g + 1) * GROUP_DIM)
            wm = jnp.where(mask, w_ref[g], 0.0)
            vhat, rstd, vn, mixed = _sgu_group_forward(
                v[:, sl], lg_ref[:, sl], lb_ref[:, sl], wm.astype(BF16), bt_ref[:, g:g + 1])
            saved.append((wm, vhat, rstd, vn, mixed))
            cols.append(u[:, sl] * mixed)
        ya = jnp.concatenate(cols, axis=1)
        rstd_a = lax.rsqrt(jnp.mean(ya * ya, axis=-1, keepdims=True) + EPS)
        ya_hat = ya * rstd_a
        dyv = dy_ref[...].astype(F32)
        dgn_ref[...] += jnp.sum(dyv * ya_hat, axis=0, keepdims=True)
        t = dyv * gn_ref[...]
        dya = rstd_a * (t - ya_hat * jnp.mean(t * ya_hat, axis=-1, keepdims=True))
        du_cols, dv_cols, dlg_cols, dlb_cols = [], [], [], []
        for g in range(G):
            sl = slice(g * GROUP_DIM, (g + 1) * GROUP_DIM)
            wm, vhat, rstd, vn, mixed = saved[g]
            dya_g = dya[:, sl]
            du_cols.append(dya_g * mixed)
            dmix = dya_g * u[:, sl]
            dmix_bf = dmix.astype(BF16)
            db_ref[g] += jnp.sum(dmix, axis=1, keepdims=True)
            dw = lax.dot_general(dmix_bf, vn.astype(BF16), (((1,), (1,)), ((), ())),
                                 preferred_element_type=F32)
            dw_ref[g] += jnp.where(mask, dw, 0.0)
            dvn = jnp.dot(wm.T.astype(BF16), dmix_bf, preferred_element_type=F32)
            dlg_cols.append(jnp.sum(dvn * vhat, axis=0, keepdims=True))
            dlb_cols.append(jnp.sum(dvn, axis=0, keepdims=True))
            dvhat = dvn * lg_ref[:, sl]
            dv_cols.append(rstd * (dvhat - jnp.mean(dvhat, axis=-1, keepdims=True)
                                   - vhat * jnp.mean(dvhat * vhat, axis=-1, keepdims=True)))
        dlg_ref[...] += jnp.concatenate(dlg_cols, axis=1)
        dlb_ref[...] += jnp.concatenate(dlb_cols, axis=1)
        dz_ref[:, :W_A] = (jnp.concatenate(du_cols, axis=1) * _gelu_grad(zu)).astype(BF16)
        dz_ref[:, W_A:] = (jnp.concatenate(dv_cols, axis=1) * _gelu_grad(zv)).astype(BF16)

    vec = pl.BlockSpec((1, W_A), lambda i: (0, 0))
    wspec = pl.BlockSpec((G, SGU_BLOCK, SGU_BLOCK), lambda i: (0, 0, 0))
    return pl.pallas_call(
        body, name=name, grid=(T // SGU_BLOCK,),
        in_specs=[pl.BlockSpec((SGU_BLOCK, 2 * W_A), lambda i: (i, 0)),
                  pl.BlockSpec((SGU_BLOCK, W_A), lambda i: (i, 0)), vec, vec, wspec,
                  pl.BlockSpec((SGU_BLOCK, G), lambda i: (0, 0)), vec],
        out_specs=[pl.BlockSpec((SGU_BLOCK, 2 * W_A), lambda i: (i, 0)), vec, vec, wspec,
                   pl.BlockSpec((G, SGU_BLOCK, 1), lambda i: (0, 0, 0)), vec],
        out_shape=[jax.ShapeDtypeStruct((T, 2 * W_A), BF16), jax.ShapeDtypeStruct((1, W_A), F32),
                   jax.ShapeDtypeStruct((1, W_A), F32),
                   jax.ShapeDtypeStruct((G, SGU_BLOCK, SGU_BLOCK), F32),
                   jax.ShapeDtypeStruct((G, SGU_BLOCK, 1), F32),
                   jax.ShapeDtypeStruct((1, W_A), F32)],
        compiler_params=_params(1))(z, dy, ln_g, ln_b, w_s, b_t, gn)


def _split_dot(x, tri):
    hi = x.astype(BF16)
    lo = (x - hi.astype(F32)).astype(BF16)
    return (jnp.dot(hi, tri, preferred_element_type=F32)
            + jnp.dot(lo, tri, preferred_element_type=F32))


def _tri(n, rel):
    r = lax.broadcasted_iota(jnp.int32, (n, n), 0)
    c = lax.broadcasted_iota(jnp.int32, (n, n), 1)
    return rel(r, c).astype(BF16)


def _dot_nt(a, b):
    return lax.dot_general(a, b, (((1,), (1,)), ((), ())), preferred_element_type=F32)


def _dot_tn(a, b):
    return jnp.dot(a.astype(F32).T.astype(BF16), b, preferred_element_type=F32)


def _sb_scores(qs, kj, q0, k0):
    zz = _dot_nt(qs, kj)
    tq, tk = zz.shape
    tpos = q0 + lax.broadcasted_iota(jnp.int32, (tq, tk), 0)
    spos = k0 + lax.broadcasted_iota(jnp.int32, (tq, tk), 1)
    mask = spos < tpos
    log_beta = jnp.minimum(zz, 0.0) - jnp.log(1.0 + jnp.exp(-jnp.abs(zz)))
    log_1m = jnp.where(mask, log_beta - zz, 0.0)
    return log_beta, log_1m, mask


def _sb_cols(w_a, w_b):
    base = 2 * w_a // GROUP_DIM
    per = w_b // GROUP_DIM
    return base, base + per, base + 2 * per


def _sb_forward(name, z, w_a, w_b):
    T = z.shape[0]
    H = w_b // GROUP_DIM
    tb = _pick(256, [T])
    qc, kc, vc = _sb_cols(w_a, w_b)
    scale = GROUP_DIM ** -0.5

    def body(q_ref, k_ref, v_ref, y_ref, tot_ref):
        i = pl.program_id(1)
        qs = (q_ref[...] * scale).astype(BF16)
        upper = _tri(tb, lambda r, c: r > c)

        def step(jj, carry):
            acc, later = carry
            j = i - jj
            k0 = pl.multiple_of(j * tb, tb)
            kj = k_ref[pl.ds(k0, tb), :].astype(BF16)
            vj = v_ref[pl.ds(k0, tb), :].astype(BF16)
            log_beta, log_1m, mask = _sb_scores(qs, kj, i * tb, k0)
            rest = _split_dot(log_1m, upper) + later
            a = jnp.where(mask, jnp.exp(log_beta + rest), 0.0)
            acc = acc + jnp.dot(a.astype(BF16), vj, preferred_element_type=F32)
            return acc, later + jnp.sum(log_1m, axis=1, keepdims=True)

        acc, total = lax.fori_loop(
            0, i + 1, step, (jnp.zeros((tb, GROUP_DIM), F32), jnp.zeros((tb, 1), F32)))
        y_ref[...] = acc
        tot_ref[...] = total

    return pl.pallas_call(
        body, name=name, grid=(H, T // tb),
        in_specs=[pl.BlockSpec((tb, GROUP_DIM), lambda h, i: (i, qc + h)),
                  pl.BlockSpec((T, GROUP_DIM), lambda h, i: (0, kc + h)),
                  pl.BlockSpec((T, GROUP_DIM), lambda h, i: (0, vc + h))],
        out_specs=[pl.BlockSpec((tb, GROUP_DIM), lambda h, i: (i, h)),
                   pl.BlockSpec((None, tb, 1), lambda h, i: (h, i, 0))],
        out_shape=[jax.ShapeDtypeStruct((T, w_b), F32), jax.ShapeDtypeStruct((H, T, 1), F32)],
        compiler_params=_params(2))(z, z, z)


def _sb_backward(name, z, do, total, w_a, w_b):
    T = z.shape[0]
    H = w_b // GROUP_DIM
    tb = _pick(256, [T])
    qc, kc, vc = _sb_cols(w_a, w_b)
    scale = GROUP_DIM ** -0.5

    def body(q_ref, k_ref, v_ref, do_ref, tot_ref, dq_ref, dkv_ref):
        i = pl.program_id(1)

        @pl.when(i == 0)
        def _():
            dkv_ref[...] = jnp.zeros_like(dkv_ref)

        qs = (q_ref[...] * scale).astype(BF16)
        dob = do_ref[...].astype(BF16)
        total_v = tot_ref[...]
        upto = _tri(tb, lambda r, c: r <= c)
        before = _tri(tb, lambda r, c: r < c)

        def step(j, carry):
            dq, seen, e_seen = carry
            k0 = pl.multiple_of(j * tb, tb)
            kj = k_ref[pl.ds(k0, tb), :].astype(BF16)
            vj = v_ref[pl.ds(k0, tb), :].astype(BF16)
            log_beta, log_1m, mask = _sb_scores(qs, kj, i * tb, k0)
            rest = total_v - (seen + _split_dot(log_1m, upto))
            a = jnp.where(mask, jnp.exp(log_beta + rest), 0.0)
            e = a * _dot_nt(dob, vj)
            e_before = e_seen + _split_dot(e, before)
            beta = jnp.exp(log_beta)
            dz = jnp.where(mask, e * (1.0 - beta) - beta * e_before, 0.0).astype(BF16)
            dq = dq + jnp.dot(dz, kj, preferred_element_type=F32)
            dkv_ref[0, pl.ds(k0, tb), :] += _dot_tn(dz, qs)
            dkv_ref[1, pl.ds(k0, tb), :] += _dot_tn(a.astype(BF16), dob)
            return (dq, seen + jnp.sum(log_1m, axis=1, keepdims=True),
                    e_seen + jnp.sum(e, axis=1, keepdims=True))

        zero_col = jnp.zeros((tb, 1), F32)
        dq, _, _ = lax.fori_loop(0, i + 1, step,
                                 (jnp.zeros((tb, GROUP_DIM), F32), zero_col, zero_col))
        dq_ref[...] = (dq * scale).astype(BF16)

    return pl.pallas_call(
        body, name=name, grid=(H, T // tb),
        in_specs=[pl.BlockSpec((tb, GROUP_DIM), lambda h, i: (i, qc + h)),
                  pl.BlockSpec((T, GROUP_DIM), lambda h, i: (0, kc + h)),
                  pl.BlockSpec((T, GROUP_DIM), lambda h, i: (0, vc + h)),
                  pl.BlockSpec((tb, GROUP_DIM), lambda h, i: (i, h)),
                  pl.BlockSpec((None, tb, 1), lambda h, i: (h, i, 0))],
        out_specs=[pl.BlockSpec((tb, GROUP_DIM), lambda h, i: (i, h)),
                   pl.BlockSpec((2, T, GROUP_DIM), lambda h, i: (0, 0, h))],
        out_shape=[jax.ShapeDtypeStruct((T, w_b), BF16), jax.ShapeDtypeStruct((2, T, w_b), F32)],
        compiler_params=_params(2))(z, z, z, do, total)


def _softmax_rows(s):
    m = jnp.max(s, axis=-1, keepdims=True)
    p = jnp.exp(s - m)
    return p / jnp.sum(p, axis=-1, keepdims=True)


def _xattn_forward(name, q, kv):
    T, D = q.shape
    Nm = kv.shape[0]
    dh = D // X_HEADS
    tq = _pick(512, [T])

    def body(q_ref, k_ref, v_ref, o_ref):
        p = _softmax_rows(_dot_nt(q_ref[...], k_ref[...]))
        o_ref[...] = jnp.dot(p.astype(BF16), v_ref[...], preferred_element_type=F32).astype(BF16)

    return pl.pallas_call(
        body, name=name, grid=(T // tq, X_HEADS),
        in_specs=[pl.BlockSpec((tq, dh), lambda i, h: (i, h)),
                  pl.BlockSpec((Nm, dh), lambda i, h: (0, h)),
                  pl.BlockSpec((Nm, dh), lambda i, h: (0, X_HEADS + h))],
        out_specs=pl.BlockSpec((tq, dh), lambda i, h: (i, h)),
        out_shape=jax.ShapeDtypeStruct((T, D), BF16),
        compiler_params=_params(2))(q, kv, kv)


def _xattn_backward(name, q, kv, do):
    T, D = q.shape
    Nm = kv.shape[0]
    dh = D // X_HEADS
    tq = _pick(512, [T])
    scale = dh ** -0.5

    def body(q_ref, k_ref, v_ref, do_ref, dq_ref, dkv_ref):
        @pl.when(pl.program_id(1) == 0)
        def _():
            dkv_ref[...] = jnp.zeros_like(dkv_ref)

        qv, kk, vv, dov = q_ref[...], k_ref[...], v_ref[...], do_ref[...]
        p = _softmax_rows(_dot_nt(qv, kk))
        dp = _dot_nt(dov, vv)
        ds = (p * (dp - jnp.sum(dp * p, axis=-1, keepdims=True))).astype(BF16)
        dq_ref[...] = (jnp.dot(ds, kk, preferred_element_type=F32) * scale).astype(BF16)
        dkv_ref[0] += _dot_tn(ds, qv)
        dkv_ref[1] += _dot_tn(p.astype(BF16), dov)

    blk = pl.BlockSpec((tq, dh), lambda h, i: (i, h))
    return pl.pallas_call(
        body, name=name, grid=(X_HEADS, T // tq),
        in_specs=[blk, pl.BlockSpec((Nm, dh), lambda h, i: (0, h)),
                  pl.BlockSpec((Nm, dh), lambda h, i: (0, X_HEADS + h)), blk],
        out_specs=[blk, pl.BlockSpec((2, Nm, dh), lambda h, i: (0, 0, h))],
        out_shape=[jax.ShapeDtypeStruct((T, D), BF16), jax.ShapeDtypeStruct((2, Nm, D), F32)],
        compiler_params=_params(2))(q, kv, kv, do)


def _position():
    x, y, c = lax.axis_index("x"), lax.axis_index("y"), lax.axis_index("c")
    other_chips = [(1 - x, y), (x, 1 - y), (1 - x, 1 - y)]
    return x, y, c, other_chips


def _hbm_spec():
    return pl.BlockSpec(memory_space=pltpu.HBM)


def _sem_spec():
    return pl.BlockSpec(memory_space=pltpu.SEMAPHORE)


def _split_start(name, arrays, make_copies, n_sets):
    n = len(arrays)

    def body(*refs):
        ins = refs[:n]
        send_sems, recv_sems = refs[n], refs[n + 1]
        token = refs[-1]
        for cp in make_copies(ins, send_sems, recv_sems):
            cp.start()
        token[...] = jnp.zeros_like(token)

    res = pl.pallas_call(
        body, name=name,
        out_shape=(pltpu.SemaphoreType.DMA((3 * n_sets,)), pltpu.SemaphoreType.DMA((3 * n_sets,)),
                   *[pltpu.HBM(a.shape, a.dtype) for a in arrays],
                   jax.ShapeDtypeStruct((SUBLANE, LANE), F32)),
        in_specs=[_hbm_spec()] * n,
        out_specs=(_sem_spec(), _sem_spec(), *[_hbm_spec()] * n,
                   pl.BlockSpec(memory_space=pltpu.VMEM)),
        input_output_aliases={i: 2 + i for i in range(n)},
        compiler_params=pltpu.CompilerParams(
            has_side_effects=pltpu.SideEffectType.DATAFLOW_SIDE_EFFECTING),
    )(*[pltpu.with_memory_space_constraint(a, pltpu.HBM) for a in arrays])
    return res[0], res[1], list(res[2:2 + n]), res[-1]


def _split_wait(name, arrays, send_sems, recv_sems, after, make_copies):
    n = len(arrays)

    def body(*refs):
        ins = refs[:n]
        send_ref, recv_ref = refs[n], refs[n + 1]
        for cp in make_copies(ins, send_ref, recv_ref):
            cp.wait_send()
            cp.wait_recv()

    return pl.pallas_call(
        body, name=name,
        out_shape=tuple(pltpu.HBM(a.shape, a.dtype) for a in arrays),
        in_specs=[_hbm_spec()] * n + [_sem_spec(), _sem_spec(), _any_spec()],
        out_specs=tuple(_hbm_spec() for _ in arrays),
        input_output_aliases={i: i for i in range(n)},
        compiler_params=pltpu.CompilerParams(
            has_side_effects=pltpu.SideEffectType.DATAFLOW_SIDE_EFFECTING),
    )(*arrays, send_sems, recv_sems, after)


def _gather_copies(refs, send_sems, recv_sems):
    x, y, c, chips = _position()
    me = 2 * x + y
    copies = []
    for i, ref in enumerate(refs):
        rows = ref.shape[1] // 2
        piece = ref.at[me, pl.ds(c * rows, rows), :]
        for j, (px, py) in enumerate(chips):
            copies.append(pltpu.make_async_remote_copy(
                src_ref=piece, dst_ref=piece, send_sem=send_sems.at[3 * i + j],
                recv_sem=recv_sems.at[3 * i + j], device_id=(px, py, c), device_id_type=MESH))
    return copies


def _scatter_copies(refs, send_sems, recv_sems):
    x, y, c, chips = _position()
    n = len(refs) // 2
    copies = []
    for i in range(n):
        for j, (px, py) in enumerate(chips):
            copies.append(pltpu.make_async_remote_copy(
                src_ref=refs[i].at[2 * px + py], dst_ref=refs[n + i].at[j],
                send_sem=send_sems.at[3 * i + j], recv_sem=recv_sems.at[3 * i + j],
                device_id=(px, py, c), device_id_type=MESH))
    return copies


def _cast_own(name, place, shard):
    rows, cols = shard.shape
    tr = _block_rows(rows, cols)

    def body(place_ref, w_ref, o_ref):
        o_ref[...] = w_ref[...].astype(BF16)

    grid_spec = pltpu.PrefetchScalarGridSpec(
        num_scalar_prefetch=1, grid=(rows // tr,),
        in_specs=[pl.BlockSpec((tr, cols), lambda r, pr: (r, 0))],
        out_specs=pl.BlockSpec((None, tr, cols), lambda r, pr: (pr[0], r, 0)))
    return pl.pallas_call(
        body, name=name, grid_spec=grid_spec,
        out_shape=jax.ShapeDtypeStruct((N_CHIPS, rows, cols), BF16),
        compiler_params=_params(1))(place, shard)


def _forward_to_sibling(name, arrays):
    n = len(arrays)

    def body(*refs):
        ins = refs[:n]
        send_sems, recv_sems = refs[2 * n:]
        x, y, c, chips = _position()
        sends = []
        for i in range(n):
            rows = ins[i].shape[1] // 2
            for j, (px, py) in enumerate(chips):
                piece = ins[i].at[2 * px + py, pl.ds(c * rows, rows), :]
                cp = pltpu.make_async_remote_copy(
                    src_ref=piece, dst_ref=piece, send_sem=send_sems.at[i, j],
                    recv_sem=recv_sems.at[i, j], device_id=(x, y, 1 - c), device_id_type=MESH)
                cp.start()
                sends.append(cp)
        for i in range(n):
            rows = ins[i].shape[1] // 2
            for j, (px, py) in enumerate(chips):
                piece = ins[i].at[2 * px + py, pl.ds((1 - c) * rows, rows), :]
                pltpu.make_async_remote_copy(
                    src_ref=piece, dst_ref=piece, send_sem=send_sems.at[i, j],
                    recv_sem=recv_sems.at[i, j], device_id=(x, y, 1 - c),
                    device_id_type=MESH).wait_recv()
        for cp in sends:
            cp.wait_send()

    return pl.pallas_call(
        body, name=name,
        in_specs=[_any_spec()] * n, out_specs=[_any_spec()] * n,
        out_shape=[jax.ShapeDtypeStruct(a.shape, a.dtype) for a in arrays],
        input_output_aliases={i: i for i in range(n)},
        scratch_shapes=[pltpu.SemaphoreType.DMA((n, 3))] * 2,
    )(*arrays)


def _swap_halves(name, grads):
    n = len(grads)

    def body(*refs):
        ins, outs = refs[:n], refs[n:2 * n]
        send_sems, recv_sems = refs[2 * n:]
        x, y, c, _ = _position()
        copies = []
        for i in range(n):
            rows = ins[i].shape[1] // 2
            cp = pltpu.make_async_remote_copy(
                src_ref=ins[i].at[:, pl.ds((1 - c) * rows, rows), :], dst_ref=outs[i],
                send_sem=send_sems.at[i], recv_sem=recv_sems.at[i],
                device_id=(x, y, 1 - c), device_id_type=MESH)
            cp.start()
            copies.append(cp)
        for cp in copies:
            cp.wait()

    return pl.pallas_call(
        body, name=name,
        in_specs=[_any_spec()] * n, out_specs=[_any_spec()] * n,
        out_shape=[jax.ShapeDtypeStruct((g.shape[0], g.shape[1] // 2, g.shape[2]), g.dtype)
                   for g in grads],
        scratch_shapes=[pltpu.SemaphoreType.DMA((n,))] * 2,
    )(*grads)


def _share_halves(name, shards):
    n = len(shards)

    def body(*refs):
        ins = refs[:n]
        send_sems, recv_sems = refs[2 * n:]
        x, y, c, _ = _position()
        copies = []
        for i in range(n):
            rows = ins[i].shape[0] // 2
            mine = ins[i].at[pl.ds(c * rows, rows), :]
            cp = pltpu.make_async_remote_copy(
                src_ref=mine, dst_ref=mine, send_sem=send_sems.at[i], recv_sem=recv_sems.at[i],
                device_id=(x, y, 1 - c), device_id_type=MESH)
            cp.start()
            copies.append(cp)
        for i, cp in enumerate(copies):
            rows = ins[i].shape[0] // 2
            theirs = ins[i].at[pl.ds((1 - c) * rows, rows), :]
            pltpu.make_async_remote_copy(
                src_ref=theirs, dst_ref=theirs, send_sem=send_sems.at[i],
                recv_sem=recv_sems.at[i], device_id=(x, y, 1 - c), device_id_type=MESH).wait_recv()
            cp.wait_send()

    return pl.pallas_call(
        body, name=name,
        in_specs=[_any_spec()] * n, out_specs=[_any_spec()] * n,
        out_shape=[jax.ShapeDtypeStruct(s.shape, s.dtype) for s in shards],
        input_output_aliases={i: i for i in range(n)},
        scratch_shapes=[pltpu.SemaphoreType.DMA((n,))] * 2,
    )(*shards)


def _gather_small(packed):
    def body(in_ref, out_ref, local_sem, send_sems, recv_sems):
        x, y, c, _ = _position()
        me = 4 * x + 2 * y + c
        lc = pltpu.make_async_copy(in_ref, out_ref.at[me], local_sem)
        lc.start()
        copies = []
        for r in range(1, N_DEV):
            fx, fy, fc = (r >> 2) & 1, (r >> 1) & 1, r & 1
            peer = (x ^ fx, y ^ fy, c ^ fc)
            cp = pltpu.make_async_remote_copy(
                src_ref=in_ref, dst_ref=out_ref.at[me], send_sem=send_sems.at[r - 1],
                recv_sem=recv_sems.at[r - 1], device_id=peer, device_id_type=MESH)
            cp.start()
            copies.append(cp)
        for r in range(1, N_DEV):
            fx, fy, fc = (r >> 2) & 1, (r >> 1) & 1, r & 1
            src = 4 * (x ^ fx) + 2 * (y ^ fy) + (c ^ fc)
            pltpu.make_async_remote_copy(
                src_ref=in_ref, dst_ref=out_ref.at[src], send_sem=send_sems.at[r - 1],
                recv_sem=recv_sems.at[r - 1], device_id=(x ^ fx, y ^ fy, c ^ fc),
                device_id_type=MESH).wait_recv()
        for cp in copies:
            cp.wait_send()
        lc.wait()

    return pl.pallas_call(
        body, name="gather_small", in_specs=[_any_spec()], out_specs=_any_spec(),
        out_shape=jax.ShapeDtypeStruct((N_DEV,) + packed.shape, packed.dtype),
        scratch_shapes=[pltpu.SemaphoreType.DMA, pltpu.SemaphoreType.DMA((N_DEV - 1,)),
                        pltpu.SemaphoreType.DMA((N_DEV - 1,))],
    )(packed)


def _block_rows(rows, cols, itemsize=4, target=1 << 20):
    return _pick(max(BF16_ROWS, target // (cols * itemsize)), [rows], unit=BF16_ROWS)


def _pair_sum(name, place, grad, received):
    P, rows, cols = received.shape
    tr = _block_rows(rows, cols)
    nb = rows // tr

    def body(place_ref, g_ref, r_ref, o_ref):
        o_ref[...] = (g_ref[...].astype(F32) + r_ref[...].astype(F32)).astype(BF16)

    grid_spec = pltpu.PrefetchScalarGridSpec(
        num_scalar_prefetch=1, grid=(P, nb),
        in_specs=[pl.BlockSpec((None, tr, cols), lambda p, r, pr: (p, pr[1] * nb + r, 0)),
                  pl.BlockSpec((None, tr, cols), lambda p, r, pr: (p, r, 0))],
        out_specs=pl.BlockSpec((None, tr, cols), lambda p, r, pr: (p, r, 0)))
    return pl.pallas_call(
        body, name=name, grid_spec=grid_spec,
        out_shape=jax.ShapeDtypeStruct(received.shape, BF16),
        compiler_params=_params(2))(place, grad, received)


def _final_sum(name, place, grad, received, from_chips):
    _, rows, cols = received.shape
    tr = _block_rows(rows, cols)
    nb = rows // tr

    def body(place_ref, g_ref, r_ref, c_ref, o_ref):
        acc = g_ref[...].astype(F32) + r_ref[...].astype(F32)
        for j in range(3):
            acc = acc + c_ref[j].astype(F32)
        o_ref[...] = acc

    grid_spec = pltpu.PrefetchScalarGridSpec(
        num_scalar_prefetch=1, grid=(nb,),
        in_specs=[pl.BlockSpec((None, tr, cols), lambda r, pr: (pr[0], pr[1] * nb + r, 0)),
                  pl.BlockSpec((None, tr, cols), lambda r, pr: (pr[0], r, 0)),
                  pl.BlockSpec((3, tr, cols), lambda r, pr: (0, r, 0))],
        out_specs=pl.BlockSpec((tr, cols), lambda r, pr: (pr[1] * nb + r, 0)))
    return pl.pallas_call(
        body, name=name, grid_spec=grid_spec,
        out_shape=jax.ShapeDtypeStruct((2 * rows, cols), F32),
        compiler_params=_params(1))(place, grad, received, from_chips)


def _sum_devices(name, gathered):
    n_dev, rows, cols = gathered.shape
    tr = _pick(256, [rows])

    def body(g_ref, o_ref):
        acc = g_ref[0]
        for d in range(1, n_dev):
            acc = acc + g_ref[d]
        o_ref[...] = acc

    return pl.pallas_call(
        body, name=name, grid=(rows // tr,),
        in_specs=[pl.BlockSpec((n_dev, tr, cols), lambda r: (0, r, 0))],
        out_specs=pl.BlockSpec((tr, cols), lambda r: (r, 0)),
        out_shape=jax.ShapeDtypeStruct((rows, cols), F32),
        compiler_params=_params(1))(gathered)


def _adamw(name, w, g, m, v):
    rows, cols = w.shape
    tr = _block_rows(rows, cols)
    c1 = 1.0 / (1.0 - ADAM_B1 ** ADAM_STEP)
    c2 = 1.0 / (1.0 - ADAM_B2 ** ADAM_STEP)

    def body(w_ref, g_ref, m_ref, v_ref, go_ref, d_ref, nm_ref, nv_ref):
        gv = g_ref[...]
        go_ref[...] = gv
        nm = ADAM_B1 * m_ref[...] + (1.0 - ADAM_B1) * gv
        nv = ADAM_B2 * v_ref[...] + (1.0 - ADAM_B2) * (gv * gv)
        nm_ref[...] = nm
        nv_ref[...] = nv
        d_ref[...] = -ADAM_LR * ((nm * c1) / (jnp.sqrt(nv * c2) + ADAM_EPS) + ADAM_WD * w_ref[...])

    blk = pl.BlockSpec((tr, cols), lambda r: (r, 0))
    shape = jax.ShapeDtypeStruct((rows, cols), F32)
    return pl.pallas_call(
        body, name=name, grid=(rows // tr,), in_specs=[blk] * 4, out_specs=[blk] * 4,
        out_shape=[shape] * 4, compiler_params=_params(1))(w, g, m, v)


BIG = ("ffn1_w_in", "ffn1_w_out", "w_mix_in", "w_mix_out", "w_cq", "w_ckv", "w_co",
       "ffn2_w_in", "ffn2_w_out")
BIG_KIND = {"ffn1_w_in": "c", "ffn1_w_out": "r", "w_mix_in": "c", "w_mix_out": "r", "w_cq": "r",
            "w_ckv": "c", "w_co": "r", "ffn2_w_in": "c", "ffn2_w_out": "r"}
GATHER_GROUPS = (("ffn1_in", ("ffn1_w_in",)), ("ffn1_out", ("ffn1_w_out",)),
                 ("mix", ("w_mix_in", "w_mix_out")), ("cross", ("w_cq", "w_ckv", "w_co")),
                 ("ffn2_in", ("ffn2_w_in",)), ("ffn2_out", ("ffn2_w_out",)))
SCATTER_ORDER = ("ffn2", "cross", "mix", "ffn1")
SMALL = ("ffn1_norm", "mix_norm", "ln_v_gain", "ln_v_bias", "spatial_w", "spatial_b", "gnorm_a",
         "gnorm_b", "cross_norm", "mem_norm", "ffn2_norm", "final_norm")
WEIGHTS = ("ffn1_norm", "ffn1_w_in", "ffn1_w_out", "mix_norm", "w_mix_in", "ln_v_gain",
           "ln_v_bias", "spatial_w", "spatial_b", "gnorm_a", "gnorm_b", "w_mix_out", "cross_norm",
           "mem_norm", "w_cq", "w_ckv", "w_co", "ffn2_norm", "ffn2_w_in", "ffn2_w_out",
           "final_norm")


def _pack(arrays):
    return jnp.concatenate([a.reshape(-1, LANE) for a in arrays], axis=0)


def _unpack(packed, like):
    out, row = [], 0
    for a in like:
        rows = a.size // LANE
        out.append(packed[row:row + rows].reshape(a.shape))
        row += rows
    return out


def _local_step(x, mem, target, small, weights_of, start_tokens, grads_done):
    T, D = x.shape
    vec = lambda name: small[name].reshape(1, -1)
    w_a = small["ln_v_gain"].size
    w_b = small["gnorm_b"].size
    G = w_a // GROUP_DIM
    w_s = small["spatial_w"].reshape(G, SGU_BLOCK, SGU_BLOCK)
    b_t = small["spatial_b"].reshape(G, SGU_BLOCK).T

    h1, ffn1_saved = _ffn_forward("ffn1", x, vec("ffn1_norm"), weights_of, deps=start_tokens)
    n2 = _rmsnorm_fwd("mix_norm", h1, vec("mix_norm"))
    big = weights_of("mix", n2)
    (z,) = _matmul("mix_in", Mat(n2), big["w_mix_in"], "nn", [("c", 1, F32)], tm=2048, tn=256)
    z = z[0]
    y = _sgu_forward("sgu", z, vec("ln_v_gain"), vec("ln_v_bias"), w_s, b_t, vec("gnorm_a"), D)
    yb, sb_total = _sb_forward("stickbreak", z, w_a, w_b)
    y = _rmsnorm_fwd("gnorm_b", yb, vec("gnorm_b"), into=y, col=w_a // w_b)

    def add_res(acc, ex, out):
        out[0][...] = ex[0][...] + acc

    (h2,) = _matmul("mix_out", Mat(y), big["w_mix_out"], "nn", [("c", 1, F32)],
                    tm=1024, tn=1024, extras=[Mat(h1)], epi=add_res)
    h2 = h2[0]
    n3 = _rmsnorm_fwd("cross_norm", h2, vec("cross_norm"))
    memn = _rmsnorm_fwd("mem_norm", mem, vec("mem_norm"))
    big.update(weights_of("cross", n3))
    x_scale = (D // X_HEADS) ** -0.5

    def scaled(acc, ex, out):
        out[0][...] = (acc * x_scale).astype(BF16)

    (q,) = _matmul("cross_q", Mat(n3), big["w_cq"], "nn", [("c", 1, BF16)],
                   tm=1024, tn=1024, epi=scaled)
    (kv,) = _matmul("cross_kv", Mat(memn), big["w_ckv"], "nn", [("c", 1, BF16)], tm=256, tn=1024)
    q, kv = q[0], kv[0]
    o = _xattn_forward("cross_attn", q, kv)
    (h3,) = _matmul("cross_out", Mat(o), big["w_co"], "nn", [("c", 1, F32)],
                    tm=1024, tn=1024, extras=[Mat(h2)], epi=add_res)
    h3 = h3[0]
    h4, ffn2_saved = _ffn_forward("ffn2", h3, vec("ffn2_norm"), weights_of)

    gs = {}
    loss_tile, dh4, dh4_bf, gs["final_norm"] = _loss_head("loss_head", h4, vec("final_norm"), target)
    dh3, dh3_bf, gs["ffn2_norm"] = _ffn_backward(
        "ffn2", h3, vec("ffn2_norm"), ffn2_saved, dh4, dh4_bf, grads_done)

    (do,) = _matmul("cross_do", Mat(dh3_bf), big["w_co"], "nt", [("c", 1, BF16)], tm=1024, tn=512)
    (dw_co,) = _matmul("cross_dwo", Mat(o), Mat(dh3_bf), "tn", [("r", N_CHIPS, BF16)],
                       tm=512, tn=1024)
    dq, dkv = _xattn_backward("cross_attn_bwd", q, kv, do[0])
    (dw_cq,) = _matmul("cross_dwq", Mat(n3), Mat(dq), "tn", [("r", N_CHIPS, BF16)],
                       tm=512, tn=1024)
    (dw_ckv,) = _matmul("cross_dwkv", Mat(memn), Mat(dkv), "tn", [("c", N_CHIPS, BF16)],
                        tm=1024, tn=1024)
    token = grads_done("cross", {"w_cq": dw_cq, "w_ckv": dw_ckv, "w_co": dw_co})
    dq = _tie("cross_dq_after_scatter", dq, [token])
    (dn3,) = _matmul("cross_dn", Mat(dq), big["w_cq"], "nt", [("c", 1, F32)], tm=1024, tn=512)
    (dmemn,) = _matmul("cross_dmem", Mat(dkv), big["w_ckv"], "nt", [("c", 1, F32)],
                       tm=256, tn=1024, tk=1024)
    (gs["mem_norm"],) = _rmsnorm_bwd("mem_dnorm", mem, vec("mem_norm"), dmemn[0], want_dx=False)
    dh2, dh2_bf, gs["cross_norm"] = _rmsnorm_bwd("cross_dnorm", h2, vec("cross_norm"), dn3[0],
                                                 dres=dh3)

    (dy,) = _matmul("mix_dy", Mat(dh2_bf), big["w_mix_out"], "nt", [("c", 1, F32)], tm=1024, tn=512)
    dy = dy[0]
    (dw_mix_out,) = _matmul("mix_dwout", Mat(y), Mat(dh2_bf), "tn", [("r", N_CHIPS, BF16)],
                            tm=512, tn=1024)
    dza, gs["ln_v_gain"], gs["ln_v_bias"], gs["spatial_w"], db, gs["gnorm_a"] = _sgu_backward(
        "sgu_bwd", z, dy, vec("ln_v_gain"), vec("ln_v_bias"), w_s, b_t, vec("gnorm_a"))
    gs["spatial_b"] = db.reshape(G, SGU_BLOCK)
    dob, gs["gnorm_b"] = _rmsnorm_bwd("gnorm_b_bwd", yb, vec("gnorm_b"), dy, dn_col=w_a // w_b,
                                      want_bf16=False)
    dqb, dkvb = _sb_backward("stickbreak_bwd", z, dob, sb_total, w_a, w_b)
    dz = jnp.concatenate([dza, dqb, dkvb[0].astype(BF16), dkvb[1].astype(BF16)], axis=1)
    (dw_mix_in,) = _matmul("mix_dwin", Mat(n2), Mat(dz), "tn", [("c", N_CHIPS, BF16)],
                           tm=1024, tn=1280)
    token = grads_done("mix", {"w_mix_in": dw_mix_in, "w_mix_out": dw_mix_out})
    dz = _tie("mix_dz_after_scatter", dz, [token])
    (dn2,) = _matmul("mix_dn", Mat(dz), big["w_mix_in"], "nt", [("c", 1, F32)],
                     tm=1024, tn=1024, tk=1280)
    dh1, dh1_bf, gs["mix_norm"] = _rmsnorm_bwd("mix_dnorm", h1, vec("mix_norm"), dn2[0], dres=dh2)

    dx, _, gs["ffn1_norm"] = _ffn_backward(
        "ffn1", x, vec("ffn1_norm"), ffn1_saved, dh1, dh1_bf, grads_done)
    gs = {k: g.reshape(small[k].shape) for k, g in gs.items()}
    return loss_tile, dx, gs


def kernel(x, mem, ffn1_norm, ffn1_w_in, ffn1_w_out, mix_norm, w_mix_in, ln_v_gain, ln_v_bias, spatial_w, spatial_b, gnorm_a, gnorm_b, w_mix_out, cross_norm, mem_norm, w_cq, w_ckv, w_co, ffn2_norm, ffn2_w_in, ffn2_w_out, final_norm, loss_target, m_ffn1_norm, m_ffn1_w_in, m_ffn1_w_out, m_mix_norm, m_w_mix_in, m_ln_v_gain, m_ln_v_bias, m_spatial_w, m_spatial_b, m_gnorm_a, m_gnorm_b, m_w_mix_out, m_cross_norm, m_mem_norm, m_w_cq, m_w_ckv, m_w_co, m_ffn2_norm, m_ffn2_w_in, m_ffn2_w_out, m_final_norm, v_ffn1_norm, v_ffn1_w_in, v_ffn1_w_out, v_mix_norm, v_w_mix_in, v_ln_v_gain, v_ln_v_bias, v_spatial_w, v_spatial_b, v_gnorm_a, v_gnorm_b, v_w_mix_out, v_cross_norm, v_mem_norm, v_w_cq, v_w_ckv, v_w_co, v_ffn2_norm, v_ffn2_w_in, v_ffn2_w_out, v_final_norm):
    given = dict(locals())
    w = {k: given[k] for k in WEIGHTS}
    m = {k: given["m_" + k] for k in WEIGHTS}
    v = {k: given["v_" + k] for k in WEIGHTS}

    cx, cy, cc = lax.axis_index("x"), lax.axis_index("y"), lax.axis_index("c")
    place = jnp.stack([2 * cx + cy, cc]).astype(jnp.int32)

    gathers, start_tokens = {}, []
    for group, names in GATHER_GROUPS:
        own = [_cast_own(f"cast_{k}", place, w[k][0]) for k in names]
        send, recv, arrays, token = _split_start(f"gather_start_{group}", own, _gather_copies,
                                                 len(names))
        gathers[group] = (names, send, recv, arrays)
        start_tokens.append(token)

    def weights_of(group, after):
        names, send, recv, arrays = gathers[group]
        arrays = _split_wait(f"gather_wait_{group}", arrays, send, recv, after, _gather_copies)
        arrays = _forward_to_sibling(f"gather_forward_{group}", list(arrays))
        return {k: Mat(a, BIG_KIND[k]) for k, a in zip(names, arrays)}

    scatters = {}

    def grads_done(group, partial):
        names = list(partial)
        grads_ = [partial[k] for k in names]
        from_sibling = _swap_halves(f"swap_{group}", grads_)
        sums = [_pair_sum(f"pair_sum_{k}", place, g, r)
                for k, g, r in zip(names, grads_, from_sibling)]
        lands = [lax.empty((3,) + s.shape[1:], s.dtype) for s in sums]
        send, recv, arrays, token = _split_start(f"scatter_start_{group}", sums + lands,
                                                 _scatter_copies, len(names))
        scatters[group] = (names, grads_, from_sibling, send, recv, arrays)
        return token

    small = {k: w[k] for k in SMALL}
    loss_tile, grad_x, gs = _local_step(x[0], mem[0], loss_target[0], small, weights_of,
                                        start_tokens, grads_done)

    grad, delta, new_m, new_v = {}, {}, {}, {}
    for group in SCATTER_ORDER:
        names, grads_, from_sibling, send, recv, arrays = scatters[group]
        arrays = _split_wait(f"scatter_wait_{group}", arrays, send, recv, grad_x, _scatter_copies)
        from_chips = arrays[len(names):]
        shards = [_final_sum(f"final_sum_{k}", place, g, r, f)
                  for k, g, r, f in zip(names, grads_, from_sibling, from_chips)]
        shards = _share_halves(f"share_{group}", shards)
        for k, g_ in zip(names, shards):
            g_, d_, m_, v_ = _adamw(f"adamw_{k}", w[k][0], g_, m[k][0], v[k][0])
            grad[k], delta[k], new_m[k], new_v[k] = g_[None], d_[None], m_[None], v_[None]

    packed = _pack([gs[k] for k in SMALL] + [loss_tile])
    total = _sum_devices("sum_small", _gather_small(packed))
    n_small = total.shape[0] - SUBLANE
    loss = total[n_small, 0]
    small_g = total[:n_small]
    g_s, d_s, m_s, v_s = _adamw("adamw_small", _pack([w[k] for k in SMALL]), small_g,
                                _pack([m[k] for k in SMALL]), _pack([v[k] for k in SMALL]))
    like = [w[k] for k in SMALL]
    for k, g_, d_, m_, v_ in zip(SMALL, _unpack(g_s, like), _unpack(d_s, like),
                                 _unpack(m_s, like), _unpack(v_s, like)):
        grad[k], delta[k], new_m[k], new_v[k] = g_, d_, m_, v_

    return (loss, grad_x[None], *[grad[k] for k in WEIGHTS], *[delta[k] for k in WEIGHTS],
            *[new_m[k] for k in WEIGHTS], *[new_v[k] for k in WEIGHTS])
```

```python
import functools
import math

import jax
import jax.numpy as jnp
from jax import lax
from jax.experimental import pallas as pl
from jax.experimental.pallas import tpu as pltpu

F32 = jnp.float32
BF16 = jnp.bfloat16
MESH = pl.DeviceIdType.MESH

EPS = 1e-6
CHUNK = 64
SGU_BLOCK = 128
GROUP_DIM = 128
X_HEADS = 4
N_CHIPS = 4
N_DEV = 8
LANE = 128
SUBLANE = 8
BF16_ROWS = 16

ADAM_LR = 0.001
ADAM_B1 = 0.9
ADAM_B2 = 0.999
ADAM_EPS = 1e-08
ADAM_WD = 0.01
ADAM_STEP = 10

V7X_VMEM_BYTES = 64 << 20
VMEM_LIMIT = V7X_VMEM_BYTES - (8 << 20)


def _params(n_grid):
    return pltpu.CompilerParams(dimension_semantics=("arbitrary",) * n_grid,
                                vmem_limit_bytes=VMEM_LIMIT)


def _pick(pref, dims, unit=None):
    g = functools.reduce(math.gcd, dims)
    if unit is None:
        unit = LANE if g % LANE == 0 else SUBLANE
    cands = [d for d in range(unit, g + 1, unit) if g % d == 0] or [g]
    return min(cands, key=lambda d: abs(math.log(d / pref)))


def _any_spec():
    return pl.BlockSpec(memory_space=pl.ANY)


class Mat:
    def __init__(self, arr, kind="c"):
        if arr.ndim == 2:
            arr = arr[None]
        self.arr, self.kind = arr, kind
        self.P, self.prow, self.pcol = arr.shape
        self.rows = self.prow * (self.P if kind == "r" else 1)
        self.cols = self.pcol * (self.P if kind == "c" else 1)
        self.dtype = arr.dtype

    def spec(self, tr, tc, rc_fn):
        if self.kind == "c":
            per = self.pcol // tc
            assert per * tc == self.pcol, (self.pcol, tc)

            def imap(*g):
                i, j = rc_fn(*g)
                return (j // per, i, j % per)
        else:
            per = self.prow // tr
            assert per * tr == self.prow, (self.prow, tr)

            def imap(*g):
                i, j = rc_fn(*g)
                return (i // per, i % per, j)
        return pl.BlockSpec((None, tr, tc), imap)

    def two_d(self):
        assert self.P == 1
        return self.arr[0]


def _out_mat(kind, P, rows, cols, dtype):
    shape = (P, rows, cols // P) if kind == "c" else (P, rows // P, cols)
    return jax.ShapeDtypeStruct(shape, dtype)


def _matmul(name, A, B, mode, outs, *, tm=1024, tn=1024, tk=2048, extras=(), epi=None):
    if mode == "nn":
        M, K, N = A.rows, A.cols, B.cols
        assert B.rows == K
    elif mode == "nt":
        M, K, N = A.rows, A.cols, B.rows
        assert B.cols == K
    else:
        K, M, N = A.rows, A.cols, B.cols
        assert B.rows == K
    mdims, ndims, kdims = [M], [N], [K]
    if mode == "tn":
        assert A.kind == "c" and B.kind == "c"
        mdims.append(A.pcol)
        ndims.append(B.pcol)
    else:
        (mdims if A.kind == "r" else kdims).append(A.prow if A.kind == "r" else A.pcol)
        if mode == "nn":
            (kdims if B.kind == "r" else ndims).append(B.prow if B.kind == "r" else B.pcol)
        else:
            (ndims if B.kind == "r" else kdims).append(B.prow if B.kind == "r" else B.pcol)
    for o in list(outs) + list(extras):
        if isinstance(o, Mat):
            (mdims if o.kind == "r" else ndims).append(o.prow if o.kind == "r" else o.pcol)
        elif isinstance(o[0], str):
            (mdims if o[0] == "r" else ndims).append((M if o[0] == "r" else N) // o[1])
    tm, tn = _pick(tm, mdims), _pick(tn, ndims)
    tk = K if mode == "tn" else _pick(tk, kdims)
    nk = K // tk
    grid = (M // tm, N // tn, nk)

    if mode == "tn":
        a_spec = A.spec(K, tm, lambda m, n, k: (0, m))
        b_spec = B.spec(K, tn, lambda m, n, k: (0, n))
    else:
        a_spec = A.spec(tm, tk, lambda m, n, k: (m, k))
        if mode == "nn":
            b_spec = B.spec(tk, tn, lambda m, n, k: (k, n))
        else:
            b_spec = B.spec(tn, tk, lambda m, n, k: (n, k))

    def mn_spec(o):
        if isinstance(o, Mat):
            return o.spec(tm, tn, lambda m, n, k: (m, n))
        if isinstance(o[0], str):
            kind, P = o[0], o[1]
            fake = Mat.__new__(Mat)
            fake.kind, fake.P = kind, P
            fake.prow = M // P if kind == "r" else M
            fake.pcol = N // P if kind == "c" else N
            return Mat.spec(fake, tm, tn, lambda m, n, k: (m, n))
        return o[1](tm, tn)

    out_shapes = tuple(_out_mat(o[0], o[1], M, N, o[2]) if isinstance(o[0], str) else o[0]
                       for o in outs)
    out_specs = tuple(mn_spec(o) for o in outs)
    extra_arrays = tuple(e.arr if isinstance(e, Mat) else e[0] for e in extras)
    extra_specs = tuple(mn_spec(e) for e in extras)
    n_ex, n_out = len(extras), len(outs)
    tt = _pick(256, [tm])
    dims = (((1,), (1 if mode == "nt" else 0,)), ((), ()))

    def body(*refs):
        a_ref, b_ref = refs[:2]
        ex_refs = refs[2:2 + n_ex]
        out_refs = refs[2 + n_ex:2 + n_ex + n_out]
        scratch = refs[2 + n_ex + n_out:]
        if mode == "tn":
            at_ref = scratch[0]

            @pl.when(pl.program_id(1) == 0)
            def _():
                for c0 in range(0, tm, tt):
                    at_ref[c0:c0 + tt, :] = a_ref[:, c0:c0 + tt].astype(F32).T.astype(BF16)

            lhs = at_ref[...]
        else:
            lhs = a_ref[...].astype(BF16)
        part = lax.dot_general(lhs, b_ref[...].astype(BF16), dims, preferred_element_type=F32)

        def finish(acc):
            if epi is None:
                out_refs[0][...] = acc.astype(out_refs[0].dtype)
            else:
                epi(acc, ex_refs, out_refs)

        if nk == 1:
            finish(part)
        else:
            acc_ref = scratch[0]
            k = pl.program_id(2)

            @pl.when(k == 0)
            def _():
                acc_ref[...] = part

            @pl.when(k > 0)
            def _():
                acc_ref[...] += part

            @pl.when(k == nk - 1)
            def _():
                finish(acc_ref[...])

    scratch_shapes = []
    if mode == "tn":
        scratch_shapes.append(pltpu.VMEM((tm, K), BF16))
    elif nk > 1:
        scratch_shapes.append(pltpu.VMEM((tm, tn), F32))
    res = pl.pallas_call(
        body, name=name, grid=grid,
        in_specs=[a_spec, b_spec, *extra_specs], out_specs=out_specs, out_shape=out_shapes,
        scratch_shapes=scratch_shapes, compiler_params=_params(3),
    )(A.arr, B.arr, *extra_arrays)
    return res


def _row_tile(T):
    return _pick(256, [T])


def _tie(name, x, deps):
    def body(*refs):
        refs[-1][...] = jnp.zeros_like(refs[-1])

    return pl.pallas_call(
        body, name=name, in_specs=[_any_spec()] * (1 + len(deps)),
        out_specs=(_any_spec(), pl.BlockSpec(memory_space=pltpu.VMEM)),
        out_shape=(jax.ShapeDtypeStruct(x.shape, x.dtype),
                   jax.ShapeDtypeStruct((SUBLANE, LANE), F32)),
        input_output_aliases={0: 0},
    )(x, *deps)[0]


def _rmsnorm_fwd(name, x, g, *, into=None, col=0, deps=()):
    T, W = x.shape
    tr = _row_tile(T)

    def body(x_ref, g_ref, *rest):
        o_ref = rest[-1]
        xv = x_ref[...]
        rstd = lax.rsqrt(jnp.mean(xv * xv, axis=-1, keepdims=True) + EPS)
        o_ref[...] = (xv * rstd * g_ref[...]).astype(o_ref.dtype)

    in_specs = [pl.BlockSpec((tr, W), lambda i: (i, 0)), pl.BlockSpec((1, W), lambda i: (0, 0))]
    args = [x, g]
    kwargs = {}
    if into is None:
        out_shape = jax.ShapeDtypeStruct((T, W), BF16)
    else:
        out_shape = jax.ShapeDtypeStruct(into.shape, into.dtype)
        in_specs.append(_any_spec())
        args.append(into)
        kwargs["input_output_aliases"] = {2: 0}
    in_specs += [_any_spec()] * len(deps)
    args += list(deps)
    return pl.pallas_call(
        body, name=name, grid=(T // tr,), in_specs=in_specs,
        out_specs=pl.BlockSpec((tr, W), lambda i: (i, col)), out_shape=out_shape,
        compiler_params=_params(1), **kwargs)(*args)


def _rmsnorm_bwd(name, x, g, dn, *, dn_col=0, dres=None, want_dx=True, want_bf16=True):
    T, W = x.shape
    tr = _row_tile(T)
    has_res = dres is not None

    def body(*refs):
        x_ref, g_ref, dn_ref = refs[:3]
        pos = 3
        dres_ref = None
        if has_res:
            dres_ref = refs[pos]
            pos += 1
        outs = refs[pos:]
        dg_ref = outs[-1]
        xv = x_ref[...]
        rstd = lax.rsqrt(jnp.mean(xv * xv, axis=-1, keepdims=True) + EPS)
        xhat = xv * rstd
        dnv = dn_ref[...].astype(F32)

        @pl.when(pl.program_id(0) == 0)
        def _():
            dg_ref[...] = jnp.zeros_like(dg_ref)

        dg_ref[...] += jnp.sum(dnv * xhat, axis=0, keepdims=True)
        if want_dx:
            t = dnv * g_ref[...]
            dx = rstd * (t - xhat * jnp.mean(t * xhat, axis=-1, keepdims=True))
            if has_res:
                dx = dx + dres_ref[...]
            outs[0][...] = dx
            if want_bf16:
                outs[1][...] = dx.astype(BF16)

    row = pl.BlockSpec((tr, W), lambda i: (i, 0))
    in_specs = [row, pl.BlockSpec((1, W), lambda i: (0, 0)),
                pl.BlockSpec((tr, W), lambda i: (i, dn_col))]
    args = [x, g, dn]
    if has_res:
        in_specs.append(row)
        args.append(dres)
    out_shape, out_specs = [], []
    if want_dx:
        out_shape.append(jax.ShapeDtypeStruct((T, W), F32))
        out_specs.append(row)
        if want_bf16:
            out_shape.append(jax.ShapeDtypeStruct((T, W), BF16))
            out_specs.append(row)
    out_shape.append(jax.ShapeDtypeStruct((1, W), F32))
    out_specs.append(pl.BlockSpec((1, W), lambda i: (0, 0)))
    return pl.pallas_call(
        body, name=name, grid=(T // tr,), in_specs=in_specs, out_specs=out_specs,
        out_shape=out_shape, compiler_params=_params(1))(*args)


def _loss_head(name, h, g, target):
    T, W = h.shape
    tr = _row_tile(T)

    def body(h_ref, g_ref, t_ref, loss_ref, dx_ref, dxb_ref, dg_ref):
        xv = h_ref[...]
        gv = g_ref[...]
        rstd = lax.rsqrt(jnp.mean(xv * xv, axis=-1, keepdims=True) + EPS)
        xhat = xv * rstd
        diff = xhat * gv - t_ref[...]

        @pl.when(pl.program_id(0) == 0)
        def _():
            dg_ref[...] = jnp.zeros_like(dg_ref)
            loss_ref[...] = jnp.zeros_like(loss_ref)

        loss_ref[...] += 0.5 * jnp.sum(jnp.mean(diff * diff, axis=-1, keepdims=True))
        dnv = diff * (1.0 / W)
        dg_ref[...] += jnp.sum(dnv * xhat, axis=0, keepdims=True)
        t = dnv * gv
        dx = rstd * (t - xhat * jnp.mean(t * xhat, axis=-1, keepdims=True))
        dx_ref[...] = dx
        dxb_ref[...] = dx.astype(BF16)

    row = pl.BlockSpec((tr, W), lambda i: (i, 0))
    vec = pl.BlockSpec((1, W), lambda i: (0, 0))
    return pl.pallas_call(
        body, name=name, grid=(T // tr,), in_specs=[row, vec, row],
        out_specs=[pl.BlockSpec((SUBLANE, LANE), lambda i: (0, 0)), row, row, vec],
        out_shape=[jax.ShapeDtypeStruct((SUBLANE, LANE), F32), jax.ShapeDtypeStruct((T, W), F32),
                   jax.ShapeDtypeStruct((T, W), BF16), jax.ShapeDtypeStruct((1, W), F32)],
        compiler_params=_params(1))(h, g, target)


def _sigmoid(x):
    return 1.0 / (1.0 + jnp.exp(-x))


def _ffn_in(name, n, W):
    T, D = n.shape
    F = W.cols // 2
    tm = _pick(2048, [T])
    tn = _pick(512, [W.pcol])
    per = W.pcol // tn

    def body(a_ref, wg_ref, wu_ref, gu_ref, act_ref):
        a = a_ref[...]
        gate = jnp.dot(a, wg_ref[...], preferred_element_type=F32)
        up = jnp.dot(a, wu_ref[...], preferred_element_type=F32)
        gu_ref[0] = gate.astype(BF16)
        gu_ref[1] = up.astype(BF16)
        act_ref[...] = (gate * _sigmoid(gate) * up).astype(BF16)

    return pl.pallas_call(
        body, name=name, grid=(T // tm, F // tn),
        in_specs=[pl.BlockSpec((tm, D), lambda m, j: (m, 0)),
                  pl.BlockSpec((None, D, tn), lambda m, j: (j // per, 0, j % per)),
                  pl.BlockSpec((None, D, tn), lambda m, j: (2 + j // per, 0, j % per))],
        out_specs=[pl.BlockSpec((2, tm, tn), lambda m, j: (0, m, j)),
                   pl.BlockSpec((tm, tn), lambda m, j: (m, j))],
        out_shape=[jax.ShapeDtypeStruct((2, T, F), BF16), jax.ShapeDtypeStruct((T, F), BF16)],
        compiler_params=_params(2))(n, W.arr, W.arr)


def _ffn_forward(tag, h, norm_g, weights_of, deps=()):
    n = _rmsnorm_fwd(f"{tag}_norm", h, norm_g, deps=deps)
    w_in = weights_of(f"{tag}_in", n)[f"{tag}_w_in"]
    gu, act = _ffn_in(f"{tag}_in", n, w_in)
    w_out = weights_of(f"{tag}_out", act)[f"{tag}_w_out"]

    def epi(acc, ex, out):
        out[0][...] = ex[0][...] + 0.5 * acc

    (h_out,) = _matmul(f"{tag}_out", Mat(act), w_out, "nn", [("c", 1, F32)],
                       tm=1024, tn=1024, tk=1408, extras=[Mat(h)], epi=epi)
    return h_out[0], (n, gu, act, w_in, w_out)


def _ffn_backward(tag, h_in, norm_g, saved, dh, dh_bf, grads_done):
    n, gu, act, w_in, w_out = saved
    T, F = act.shape

    def epi(acc, ex, out):
        dact = 0.5 * acc
        gate = ex[0][0].astype(F32)
        up = ex[0][1].astype(F32)
        sig = _sigmoid(gate)
        out[0][0] = (dact * up * sig * (1.0 + gate * (1.0 - sig))).astype(BF16)
        out[0][1] = (dact * gate * sig).astype(BF16)

    def pair_spec(tm, tn):
        return pl.BlockSpec((2, tm, tn), lambda m, j, k: (0, m, j))

    (dgu,) = _matmul(f"{tag}_dact", Mat(dh_bf), w_out, "nt",
                     [(jax.ShapeDtypeStruct((2, T, F), BF16), pair_spec)],
                     tm=512, tn=1408, extras=[(gu, pair_spec)], epi=epi)

    def half(acc, ex, out):
        out[0][...] = (0.5 * acc).astype(out[0].dtype)

    (dw_out,) = _matmul(f"{tag}_dwout", Mat(act), Mat(dh_bf), "tn", [("r", N_CHIPS, BF16)],
                        tm=1408, tn=512, epi=half)
    (dw_in,) = _matmul(f"{tag}_dwin", Mat(n), Mat(dgu), "tn", [("c", N_CHIPS, BF16)],
                       tm=512, tn=1408)
    token = grads_done(tag, {f"{tag}_w_in": dw_in, f"{tag}_w_out": dw_out})
    dgu = _tie(f"{tag}_dgu_after_scatter", dgu, [token])
    (dn,) = _matmul(f"{tag}_dn", Mat(dgu), w_in, "nt", [("c", 1, F32)],
                    tm=1024, tn=1024, tk=2816)
    return _rmsnorm_bwd(f"{tag}_dnorm", h_in, norm_g, dn[0], dres=dh)


_GELU_C = math.sqrt(2.0 / math.pi)
_GELU_A = 0.044715


def _gelu(x):
    return 0.5 * x * (1.0 + jnp.tanh(_GELU_C * (x + _GELU_A * x * x * x)))


def _gelu_grad(x):
    th = jnp.tanh(_GELU_C * (x + _GELU_A * x * x * x))
    return 0.5 * (1.0 + th) + 0.5 * x * (1.0 - th * th) * _GELU_C * (1.0 + 3.0 * _GELU_A * x * x)


def _chunk_mask():
    t = lax.broadcasted_iota(jnp.int32, (SGU_BLOCK, SGU_BLOCK), 0) // CHUNK
    s = lax.broadcasted_iota(jnp.int32, (SGU_BLOCK, SGU_BLOCK), 1) // CHUNK
    return s <= t


def _sgu_group_forward(v_g, lg, lb, wm_bf, b_col):
    mu = jnp.mean(v_g, axis=-1, keepdims=True)
    xc = v_g - mu
    rstd = lax.rsqrt(jnp.mean(xc * xc, axis=-1, keepdims=True) + EPS)
    vhat = xc * rstd
    vn = vhat * lg + lb
    mixed = jnp.dot(wm_bf, vn.astype(BF16), preferred_element_type=F32) + b_col
    return vhat, rstd, vn, mixed


def _sgu_forward(name, z, ln_g, ln_b, w_s, b_t, gn, d_model):
    T = z.shape[0]
    W_A = ln_g.shape[1]
    G = W_A // GROUP_DIM

    def body(z_ref, lg_ref, lb_ref, w_ref, bt_ref, gn_ref, y_ref):
        mask = _chunk_mask()
        u = _gelu(z_ref[:, :W_A])
        v = _gelu(z_ref[:, W_A:])
        cols = []
        for g in range(G):
            sl = slice(g * GROUP_DIM, (g + 1) * GROUP_DIM)
            wm = jnp.where(mask, w_ref[g], 0.0).astype(BF16)
            _, _, _, mixed = _sgu_group_forward(v[:, sl], lg_ref[:, sl], lb_ref[:, sl], wm,
                                                bt_ref[:, g:g + 1])
            cols.append(u[:, sl] * mixed)
        ya = jnp.concatenate(cols, axis=1)
        rstd = lax.rsqrt(jnp.mean(ya * ya, axis=-1, keepdims=True) + EPS)
        y_ref[...] = (ya * rstd * gn_ref[...]).astype(BF16)

    vec = pl.BlockSpec((1, W_A), lambda i: (0, 0))
    return pl.pallas_call(
        body, name=name, grid=(T // SGU_BLOCK,),
        in_specs=[pl.BlockSpec((SGU_BLOCK, 2 * W_A), lambda i: (i, 0)), vec, vec,
                  pl.BlockSpec((G, SGU_BLOCK, SGU_BLOCK), lambda i: (0, 0, 0)),
                  pl.BlockSpec((SGU_BLOCK, G), lambda i: (0, 0)), vec],
        out_specs=pl.BlockSpec((SGU_BLOCK, W_A), lambda i: (i, 0)),
        out_shape=jax.ShapeDtypeStruct((T, d_model), BF16),
        compiler_params=_params(1))(z, ln_g, ln_b, w_s, b_t, gn)


def _sgu_backward(name, z, dy, ln_g, ln_b, w_s, b_t, gn):
    T = z.shape[0]
    W_A = ln_g.shape[1]
    G = W_A // GROUP_DIM

    def body(z_ref, dy_ref, lg_ref, lb_ref, w_ref, bt_ref, gn_ref,
             dz_ref, dlg_ref, dlb_ref, dw_ref, db_ref, dgn_ref):
        @pl.when(pl.program_id(0) == 0)
        def _():
            for r in (dlg_ref, dlb_ref, dw_ref, db_ref, dgn_ref):
                r[...] = jnp.zeros_like(r)

        mask = _chunk_mask()
        zu = z_ref[:, :W_A]
        zv = z_ref[:, W_A:]
        u = _gelu(zu)
        v = _gelu(zv)
        saved, cols = [], []
        for g in range(G):
            sl = slice(g * GROUP_DIM, (g + 1) * GROUP_DIM)
            wm = jnp.where(mask, w_ref[g], 0.0)
            vhat, rstd, vn, mixed = _sgu_group_forward(
                v[:, sl], lg_ref[:, sl], lb_ref[:, sl], wm.astype(BF16), bt_ref[:, g:g + 1])
            saved.append((wm, vhat, rstd, vn, mixed))
            cols.append(u[:, sl] * mixed)
        ya = jnp.concatenate(cols, axis=1)
        rstd_a = lax.rsqrt(jnp.mean(ya * ya, axis=-1, keepdims=True) + EPS)
        ya_hat = ya * rstd_a
        dyv = dy_ref[...].astype(F32)
        dgn_ref[...] += jnp.sum(dyv * ya_hat, axis=0, keepdims=True)
        t = dyv * gn_ref[...]
        dya = rstd_a * (t - ya_hat * jnp.mean(t * ya_hat, axis=-1, keepdims=True))
        du_cols, dv_cols, dlg_cols, dlb_cols = [], [], [], []
        for g in range(G):
            sl = slice(g * GROUP_DIM, (g + 1) * GROUP_DIM)
            wm, vhat, rstd, vn, mixed = saved[g]
            dya_g = dya[:, sl]
            du_cols.append(dya_g * mixed)
            dmix = dya_g * u[:, sl]
            dmix_bf = dmix.astype(BF16)
            db_ref[g] += jnp.sum(dmix, axis=1, keepdims=True)
            dw = lax.dot_general(dmix_bf, vn.astype(BF16), (((1,), (1,)), ((), ())),
                                 preferred_element_type=F32)
            dw_ref[g] += jnp.where(mask, dw, 0.0)
            dvn = jnp.dot(wm.T.astype(BF16), dmix_bf, preferred_element_type=F32)
            dlg_cols.append(jnp.sum(dvn * vhat, axis=0, keepdims=True))
            dlb_cols.append(jnp.sum(dvn, axis=0, keepdims=True))
            dvhat = dvn * lg_ref[:, sl]
            dv_cols.append(rstd * (dvhat - jnp.mean(dvhat, axis=-1, keepdims=True)
                                   - vhat * jnp.mean(dvhat * vhat, axis=-1, keepdims=True)))
        dlg_ref[...] += jnp.concatenate(dlg_cols, axis=1)
        dlb_ref[...] += jnp.concatenate(dlb_cols, axis=1)
        dz_ref[:, :W_A] = (jnp.concatenate(du_cols, axis=1) * _gelu_grad(zu)).astype(BF16)
        dz_ref[:, W_A:] = (jnp.concatenate(dv_cols, axis=1) * _gelu_grad(zv)).astype(BF16)

    vec = pl.BlockSpec((1, W_A), lambda i: (0, 0))
    wspec = pl.BlockSpec((G, SGU_BLOCK, SGU_BLOCK), lambda i: (0, 0, 0))
    return pl.pallas_call(
        body, name=name, grid=(T // SGU_BLOCK,),
        in_specs=[pl.BlockSpec((SGU_BLOCK, 2 * W_A), lambda i: (i, 0)),
                  pl.BlockSpec((SGU_BLOCK, W_A), lambda i: (i, 0)), vec, vec, wspec,
                  pl.BlockSpec((SGU_BLOCK, G), lambda i: (0, 0)), vec],
        out_specs=[pl.BlockSpec((SGU_BLOCK, 2 * W_A), lambda i: (i, 0)), vec, vec, wspec,
                   pl.BlockSpec((G, SGU_BLOCK, 1), lambda i: (0, 0, 0)), vec],
        out_shape=[jax.ShapeDtypeStruct((T, 2 * W_A), BF16), jax.ShapeDtypeStruct((1, W_A), F32),
                   jax.ShapeDtypeStruct((1, W_A), F32),
                   jax.ShapeDtypeStruct((G, SGU_BLOCK, SGU_BLOCK), F32),
                   jax.ShapeDtypeStruct((G, SGU_BLOCK, 1), F32),
                   jax.ShapeDtypeStruct((1, W_A), F32)],
        compiler_params=_params(1))(z, dy, ln_g, ln_b, w_s, b_t, gn)


def _split_dot(x, tri):
    hi = x.astype(BF16)
    lo = (x - hi.astype(F32)).astype(BF16)
    return (jnp.dot(hi, tri, preferred_element_type=F32)
            + jnp.dot(lo, tri, preferred_element_type=F32))


def _tri(n, rel):
    r = lax.broadcasted_iota(jnp.int32, (n, n), 0)
    c = lax.broadcasted_iota(jnp.int32, (n, n), 1)
    return rel(r, c).astype(BF16)


def _dot_nt(a, b):
    return lax.dot_general(a, b, (((1,), (1,)), ((), ())), preferred_element_type=F32)


def _dot_tn(a, b):
    return lax.dot_general(a, b, (((0,), (0,)), ((), ())), preferred_element_type=F32)


def _sb_scores(qs, kj, q0, k0):
    zz = _dot_nt(qs, kj)
    tq, tk = zz.shape
    tpos = q0 + lax.broadcasted_iota(jnp.int32, (tq, tk), 0)
    spos = k0 + lax.broadcasted_iota(jnp.int32, (tq, tk), 1)
    mask = spos < tpos
    log_beta = jnp.minimum(zz, 0.0) - jnp.log(1.0 + jnp.exp(-jnp.abs(zz)))
    log_1m = jnp.where(mask, log_beta - zz, 0.0)
    return log_beta, log_1m, mask


def _sb_tiles(T):
    tk = _pick(256, [T])
    tq = 2 * tk if T % (2 * tk) == 0 else tk
    return tq, tk


def _sb_cols(w_a, w_b):
    base = 2 * w_a // GROUP_DIM
    per = w_b // GROUP_DIM
    return base, base + per, base + 2 * per


def _sb_forward(name, z, w_a, w_b):
    T = z.shape[0]
    H = w_b // GROUP_DIM
    tq, tk = _sb_tiles(T)
    qc, kc, vc = _sb_cols(w_a, w_b)
    scale = GROUP_DIM ** -0.5

    def body(q_ref, k_ref, v_ref, y_ref, tot_ref):
        i = pl.program_id(1)
        qs = (q_ref[...] * scale).astype(BF16)
        upper = _tri(tk, lambda r, c: r > c)
        n_blocks = (i + 1) * (tq // tk)

        def step(jj, carry):
            acc, later = carry
            k0 = pl.multiple_of((n_blocks - 1 - jj) * tk, tk)
            kj = k_ref[pl.ds(k0, tk), :].astype(BF16)
            vj = v_ref[pl.ds(k0, tk), :].astype(BF16)
            log_beta, log_1m, mask = _sb_scores(qs, kj, i * tq, k0)
            rest = _split_dot(log_1m, upper) + later
            a = jnp.where(mask, jnp.exp(log_beta + rest), 0.0)
            acc = acc + jnp.dot(a.astype(BF16), vj, preferred_element_type=F32)
            return acc, later + jnp.sum(log_1m, axis=1, keepdims=True)

        acc, total = lax.fori_loop(
            0, n_blocks, step, (jnp.zeros((tq, GROUP_DIM), F32), jnp.zeros((tq, 1), F32)))
        y_ref[...] = acc
        tot_ref[...] = total

    return pl.pallas_call(
        body, name=name, grid=(H, T // tq),
        in_specs=[pl.BlockSpec((tq, GROUP_DIM), lambda h, i: (i, qc + h)),
                  pl.BlockSpec((T, GROUP_DIM), lambda h, i: (0, kc + h)),
                  pl.BlockSpec((T, GROUP_DIM), lambda h, i: (0, vc + h))],
        out_specs=[pl.BlockSpec((tq, GROUP_DIM), lambda h, i: (i, h)),
                   pl.BlockSpec((None, tq, 1), lambda h, i: (h, i, 0))],
        out_shape=[jax.ShapeDtypeStruct((T, w_b), F32), jax.ShapeDtypeStruct((H, T, 1), F32)],
        compiler_params=_params(2))(z, z, z)


def _sb_backward(name, z, do, total, w_a, w_b):
    T = z.shape[0]
    H = w_b // GROUP_DIM
    tq, tk = _sb_tiles(T)
    qc, kc, vc = _sb_cols(w_a, w_b)
    scale = GROUP_DIM ** -0.5

    def body(q_ref, k_ref, v_ref, do_ref, tot_ref, dq_ref, dkv_ref):
        i = pl.program_id(1)

        @pl.when(i == 0)
        def _():
            dkv_ref[...] = jnp.zeros_like(dkv_ref)

        qs = (q_ref[...] * scale).astype(BF16)
        dob = do_ref[...].astype(BF16)
        total_v = tot_ref[...]
        upto = _tri(tk, lambda r, c: r <= c)
        before = _tri(tk, lambda r, c: r < c)

        def step(j, carry):
            dq, seen, e_seen = carry
            k0 = pl.multiple_of(j * tk, tk)
            kj = k_ref[pl.ds(k0, tk), :].astype(BF16)
            vj = v_ref[pl.ds(k0, tk), :].astype(BF16)
            log_beta, log_1m, mask = _sb_scores(qs, kj, i * tq, k0)
            rest = total_v - (seen + _split_dot(log_1m, upto))
            a = jnp.where(mask, jnp.exp(log_beta + rest), 0.0)
            e = a * _dot_nt(dob, vj)
            e_before = e_seen + jnp.dot(e.astype(BF16), before, preferred_element_type=F32)
            beta = jnp.exp(log_beta)
            dz = jnp.where(mask, e * (1.0 - beta) - beta * e_before, 0.0).astype(BF16)
            dq = dq + jnp.dot(dz, kj, preferred_element_type=F32)
            dkv_ref[0, pl.ds(k0, tk), :] += _dot_tn(dz, qs)
            dkv_ref[1, pl.ds(k0, tk), :] += _dot_tn(a.astype(BF16), dob)
            return (dq, seen + jnp.sum(log_1m, axis=1, keepdims=True),
                    e_seen + jnp.sum(e, axis=1, keepdims=True))

        zero_col = jnp.zeros((tq, 1), F32)
        dq, _, _ = lax.fori_loop(0, (i + 1) * (tq // tk), step,
                                 (jnp.zeros((tq, GROUP_DIM), F32), zero_col, zero_col))
        dq_ref[...] = (dq * scale).astype(BF16)

    return pl.pallas_call(
        body, name=name, grid=(H, T // tq),
        in_specs=[pl.BlockSpec((tq, GROUP_DIM), lambda h, i: (i, qc + h)),
                  pl.BlockSpec((T, GROUP_DIM), lambda h, i: (0, kc + h)),
                  pl.BlockSpec((T, GROUP_DIM), lambda h, i: (0, vc + h)),
                  pl.BlockSpec((tq, GROUP_DIM), lambda h, i: (i, h)),
                  pl.BlockSpec((None, tq, 1), lambda h, i: (h, i, 0))],
        out_specs=[pl.BlockSpec((tq, GROUP_DIM), lambda h, i: (i, h)),
                   pl.BlockSpec((2, T, GROUP_DIM), lambda h, i: (0, 0, h))],
        out_shape=[jax.ShapeDtypeStruct((T, w_b), BF16), jax.ShapeDtypeStruct((2, T, w_b), F32)],
        compiler_params=_params(2))(z, z, z, do, total)


def _softmax_rows(s):
    m = jnp.max(s, axis=-1, keepdims=True)
    p = jnp.exp(s - m)
    return p / jnp.sum(p, axis=-1, keepdims=True)


def _xattn_forward(name, q, kv):
    T, D = q.shape
    Nm = kv.shape[0]
    dh = D // X_HEADS
    tq = _pick(512, [T])

    def body(q_ref, k_ref, v_ref, o_ref):
        p = _softmax_rows(_dot_nt(q_ref[...], k_ref[...]))
        o_ref[...] = jnp.dot(p.astype(BF16), v_ref[...], preferred_element_type=F32).astype(BF16)

    return pl.pallas_call(
        body, name=name, grid=(T // tq, X_HEADS),
        in_specs=[pl.BlockSpec((tq, dh), lambda i, h: (i, h)),
                  pl.BlockSpec((Nm, dh), lambda i, h: (0, h)),
                  pl.BlockSpec((Nm, dh), lambda i, h: (0, X_HEADS + h))],
        out_specs=pl.BlockSpec((tq, dh), lambda i, h: (i, h)),
        out_shape=jax.ShapeDtypeStruct((T, D), BF16),
        compiler_params=_params(2))(q, kv, kv)


def _xattn_backward(name, q, kv, do):
    T, D = q.shape
    Nm = kv.shape[0]
    dh = D // X_HEADS
    tq = _pick(512, [T])
    scale = dh ** -0.5

    def body(q_ref, k_ref, v_ref, do_ref, dq_ref, dkv_ref):
        @pl.when(pl.program_id(1) == 0)
        def _():
            dkv_ref[...] = jnp.zeros_like(dkv_ref)

        qv, kk, vv, dov = q_ref[...], k_ref[...], v_ref[...], do_ref[...]
        p = _softmax_rows(_dot_nt(qv, kk))
        dp = _dot_nt(dov, vv)
        ds = (p * (dp - jnp.sum(dp * p, axis=-1, keepdims=True))).astype(BF16)
        dq_ref[...] = (jnp.dot(ds, kk, preferred_element_type=F32) * scale).astype(BF16)
        dkv_ref[0] += _dot_tn(ds, qv)
        dkv_ref[1] += _dot_tn(p.astype(BF16), dov)

    blk = pl.BlockSpec((tq, dh), lambda h, i: (i, h))
    return pl.pallas_call(
        body, name=name, grid=(X_HEADS, T // tq),
        in_specs=[blk, pl.BlockSpec((Nm, dh), lambda h, i: (0, h)),
                  pl.BlockSpec((Nm, dh), lambda h, i: (0, X_HEADS + h)), blk],
        out_specs=[blk, pl.BlockSpec((2, Nm, dh), lambda h, i: (0, 0, h))],
        out_shape=[jax.ShapeDtypeStruct((T, D), BF16), jax.ShapeDtypeStruct((2, Nm, D), F32)],
        compiler_params=_params(2))(q, kv, kv, do)


def _position():
    x, y, c = lax.axis_index("x"), lax.axis_index("y"), lax.axis_index("c")
    other_chips = [(1 - x, y), (x, 1 - y), (1 - x, 1 - y)]
    return x, y, c, other_chips


def _hbm_spec():
    return pl.BlockSpec(memory_space=pltpu.HBM)


def _sem_spec():
    return pl.BlockSpec(memory_space=pltpu.SEMAPHORE)


def _split_start(name, arrays, make_copies, n_sets, deps=()):
    n, d = len(arrays), len(deps)

    def body(*refs):
        ins = refs[:n]
        send_sems, recv_sems = refs[n + d], refs[n + d + 1]
        token = refs[-1]
        for cp in make_copies(ins, send_sems, recv_sems):
            cp.start()
        token[...] = jnp.zeros_like(token)

    res = pl.pallas_call(
        body, name=name,
        out_shape=(pltpu.SemaphoreType.DMA((3 * n_sets,)), pltpu.SemaphoreType.DMA((3 * n_sets,)),
                   *[pltpu.HBM(a.shape, a.dtype) for a in arrays],
                   jax.ShapeDtypeStruct((SUBLANE, LANE), F32)),
        in_specs=[_hbm_spec()] * n + [_any_spec()] * d,
        out_specs=(_sem_spec(), _sem_spec(), *[_hbm_spec()] * n,
                   pl.BlockSpec(memory_space=pltpu.VMEM)),
        input_output_aliases={i: 2 + i for i in range(n)},
        compiler_params=pltpu.CompilerParams(
            has_side_effects=pltpu.SideEffectType.DATAFLOW_SIDE_EFFECTING),
    )(*[pltpu.with_memory_space_constraint(a, pltpu.HBM) for a in arrays], *deps)
    return res[0], res[1], list(res[2:2 + n]), res[-1]


def _split_wait(name, arrays, send_sems, recv_sems, after, make_copies):
    n = len(arrays)

    def body(*refs):
        ins = refs[:n]
        send_ref, recv_ref = refs[n], refs[n + 1]
        for cp in make_copies(ins, send_ref, recv_ref):
            cp.wait_send()
            cp.wait_recv()

    return pl.pallas_call(
        body, name=name,
        out_shape=tuple(pltpu.HBM(a.shape, a.dtype) for a in arrays),
        in_specs=[_hbm_spec()] * n + [_sem_spec(), _sem_spec(), _any_spec()],
        out_specs=tuple(_hbm_spec() for _ in arrays),
        input_output_aliases={i: i for i in range(n)},
        compiler_params=pltpu.CompilerParams(
            has_side_effects=pltpu.SideEffectType.DATAFLOW_SIDE_EFFECTING),
    )(*arrays, send_sems, recv_sems, after)


def _gather_copies(refs, send_sems, recv_sems):
    x, y, c, chips = _position()
    me = 2 * x + y
    copies = []
    for i, ref in enumerate(refs):
        rows = ref.shape[1] // 2
        piece = ref.at[me, pl.ds(c * rows, rows), :]
        for j, (px, py) in enumerate(chips):
            copies.append(pltpu.make_async_remote_copy(
                src_ref=piece, dst_ref=piece, send_sem=send_sems.at[3 * i + j],
                recv_sem=recv_sems.at[3 * i + j], device_id=(px, py, c), device_id_type=MESH))
    return copies


def _scatter_copies(refs, send_sems, recv_sems):
    x, y, c, chips = _position()
    n = len(refs) // 2
    copies = []
    for i in range(n):
        for j, (px, py) in enumerate(chips):
            copies.append(pltpu.make_async_remote_copy(
                src_ref=refs[i].at[2 * px + py], dst_ref=refs[n + i].at[j],
                send_sem=send_sems.at[3 * i + j], recv_sem=recv_sems.at[3 * i + j],
                device_id=(px, py, c), device_id_type=MESH))
    return copies


def _cast_own(name, place, shard):
    rows, cols = shard.shape
    tr = _block_rows(rows, cols)

    def body(place_ref, w_ref, o_ref):
        o_ref[...] = w_ref[...].astype(BF16)

    grid_spec = pltpu.PrefetchScalarGridSpec(
        num_scalar_prefetch=1, grid=(rows // tr,),
        in_specs=[pl.BlockSpec((tr, cols), lambda r, pr: (r, 0))],
        out_specs=pl.BlockSpec((None, tr, cols), lambda r, pr: (pr[0], r, 0)))
    return pl.pallas_call(
        body, name=name, grid_spec=grid_spec,
        out_shape=jax.ShapeDtypeStruct((N_CHIPS, rows, cols), BF16),
        compiler_params=_params(1))(place, shard)


def _forward_to_sibling(name, arrays, deps=()):
    n = len(arrays)

    def body(*refs):
        ins = refs[:n]
        send_sems, recv_sems = refs[-2:]
        x, y, c, chips = _position()
        sends = []
        for i in range(n):
            rows = ins[i].shape[1] // 2
            for j, (px, py) in enumerate(chips):
                piece = ins[i].at[2 * px + py, pl.ds(c * rows, rows), :]
                cp = pltpu.make_async_remote_copy(
                    src_ref=piece, dst_ref=piece, send_sem=send_sems.at[i, j],
                    recv_sem=recv_sems.at[i, j], device_id=(x, y, 1 - c), device_id_type=MESH)
                cp.start()
                sends.append(cp)
        for i in range(n):
            rows = ins[i].shape[1] // 2
            for j, (px, py) in enumerate(chips):
                piece = ins[i].at[2 * px + py, pl.ds((1 - c) * rows, rows), :]
                pltpu.make_async_remote_copy(
                    src_ref=piece, dst_ref=piece, send_sem=send_sems.at[i, j],
                    recv_sem=recv_sems.at[i, j], device_id=(x, y, 1 - c),
                    device_id_type=MESH).wait_recv()
        for cp in sends:
            cp.wait_send()

    return pl.pallas_call(
        body, name=name,
        in_specs=[_any_spec()] * (n + len(deps)), out_specs=[_any_spec()] * n,
        out_shape=[jax.ShapeDtypeStruct(a.shape, a.dtype) for a in arrays],
        input_output_aliases={i: i for i in range(n)},
        scratch_shapes=[pltpu.SemaphoreType.DMA((n, 3))] * 2,
    )(*arrays, *deps)


def _swap_halves(name, grads):
    n = len(grads)

    def body(*refs):
        ins, outs = refs[:n], refs[n:2 * n]
        send_sems, recv_sems = refs[2 * n:]
        x, y, c, _ = _position()
        copies = []
        for i in range(n):
            rows = ins[i].shape[1] // 2
            cp = pltpu.make_async_remote_copy(
                src_ref=ins[i].at[:, pl.ds((1 - c) * rows, rows), :], dst_ref=outs[i],
                send_sem=send_sems.at[i], recv_sem=recv_sems.at[i],
                device_id=(x, y, 1 - c), device_id_type=MESH)
            cp.start()
            copies.append(cp)
        for cp in copies:
            cp.wait()

    return pl.pallas_call(
        body, name=name,
        in_specs=[_any_spec()] * n, out_specs=[_any_spec()] * n,
        out_shape=[jax.ShapeDtypeStruct((g.shape[0], g.shape[1] // 2, g.shape[2]), g.dtype)
                   for g in grads],
        scratch_shapes=[pltpu.SemaphoreType.DMA((n,))] * 2,
    )(*grads)


def _share_halves(name, shards):
    n = len(shards)

    def body(*refs):
        ins = refs[:n]
        send_sems, recv_sems = refs[2 * n:]
        x, y, c, _ = _position()
        copies = []
        for i in range(n):
            rows = ins[i].shape[0] // 2
            mine = ins[i].at[pl.ds(c * rows, rows), :]
            cp = pltpu.make_async_remote_copy(
                src_ref=mine, dst_ref=mine, send_sem=send_sems.at[i], recv_sem=recv_sems.at[i],
                device_id=(x, y, 1 - c), device_id_type=MESH)
            cp.start()
            copies.append(cp)
        for i, cp in enumerate(copies):
            rows = ins[i].shape[0] // 2
            theirs = ins[i].at[pl.ds((1 - c) * rows, rows), :]
            pltpu.make_async_remote_copy(
                src_ref=theirs, dst_ref=theirs, send_sem=send_sems.at[i],
                recv_sem=recv_sems.at[i], device_id=(x, y, 1 - c), device_id_type=MESH).wait_recv()
            cp.wait_send()

    return pl.pallas_call(
        body, name=name,
        in_specs=[_any_spec()] * n, out_specs=[_any_spec()] * n,
        out_shape=[jax.ShapeDtypeStruct(s.shape, s.dtype) for s in shards],
        input_output_aliases={i: i for i in range(n)},
        scratch_shapes=[pltpu.SemaphoreType.DMA((n,))] * 2,
    )(*shards)


def _gather_small(packed, deps=()):
    def body(in_ref, *rest):
        out_ref, send_sems, recv_sems = rest[-3:]
        x, y, c, _ = _position()
        me = 4 * x + 2 * y + c
        copies = []
        for r in range(1, N_DEV):
            fx, fy, fc = (r >> 2) & 1, (r >> 1) & 1, r & 1
            peer = (x ^ fx, y ^ fy, c ^ fc)
            cp = pltpu.make_async_remote_copy(
                src_ref=in_ref, dst_ref=out_ref.at[me], send_sem=send_sems.at[r - 1],
                recv_sem=recv_sems.at[r - 1], device_id=peer, device_id_type=MESH)
            cp.start()
            copies.append(cp)
        for r in range(1, N_DEV):
            fx, fy, fc = (r >> 2) & 1, (r >> 1) & 1, r & 1
            src = 4 * (x ^ fx) + 2 * (y ^ fy) + (c ^ fc)
            pltpu.make_async_remote_copy(
                src_ref=in_ref, dst_ref=out_ref.at[src], send_sem=send_sems.at[r - 1],
                recv_sem=recv_sems.at[r - 1], device_id=(x ^ fx, y ^ fy, c ^ fc),
                device_id_type=MESH).wait_recv()
        for cp in copies:
            cp.wait_send()

    slots = jnp.zeros((N_DEV,) + packed.shape, packed.dtype)
    return pl.pallas_call(
        body, name="gather_small", in_specs=[_any_spec()] * (2 + len(deps)),
        out_specs=_any_spec(), out_shape=jax.ShapeDtypeStruct(slots.shape, slots.dtype),
        input_output_aliases={1: 0},
        scratch_shapes=[pltpu.SemaphoreType.DMA((N_DEV - 1,)),
                        pltpu.SemaphoreType.DMA((N_DEV - 1,))],
    )(packed, slots, *deps)


def _block_rows(rows, cols, itemsize=4, target=1 << 20):
    return _pick(max(BF16_ROWS, target // (cols * itemsize)), [rows], unit=BF16_ROWS)


def _pair_sum(name, place, grad, received):
    P, rows, cols = received.shape
    tr = _block_rows(rows, cols)
    nb = rows // tr

    def body(place_ref, g_ref, r_ref, o_ref):
        o_ref[...] = (g_ref[...].astype(F32) + r_ref[...].astype(F32)).astype(BF16)

    grid_spec = pltpu.PrefetchScalarGridSpec(
        num_scalar_prefetch=1, grid=(P, nb),
        in_specs=[pl.BlockSpec((None, tr, cols), lambda p, r, pr: (p, pr[1] * nb + r, 0)),
                  pl.BlockSpec((None, tr, cols), lambda p, r, pr: (p, r, 0))],
        out_specs=pl.BlockSpec((None, tr, cols), lambda p, r, pr: (p, r, 0)))
    return pl.pallas_call(
        body, name=name, grid_spec=grid_spec,
        out_shape=jax.ShapeDtypeStruct(received.shape, BF16),
        compiler_params=_params(2))(place, grad, received)


def _final_sum(name, place, grad, received, from_chips):
    _, rows, cols = received.shape
    tr = _block_rows(rows, cols)
    nb = rows // tr

    def body(place_ref, g_ref, r_ref, c_ref, o_ref):
        acc = g_ref[...].astype(F32) + r_ref[...].astype(F32)
        for j in range(3):
            acc = acc + c_ref[j].astype(F32)
        o_ref[...] = acc

    grid_spec = pltpu.PrefetchScalarGridSpec(
        num_scalar_prefetch=1, grid=(nb,),
        in_specs=[pl.BlockSpec((None, tr, cols), lambda r, pr: (pr[0], pr[1] * nb + r, 0)),
                  pl.BlockSpec((None, tr, cols), lambda r, pr: (pr[0], r, 0)),
                  pl.BlockSpec((3, tr, cols), lambda r, pr: (0, r, 0))],
        out_specs=pl.BlockSpec((tr, cols), lambda r, pr: (pr[1] * nb + r, 0)))
    return pl.pallas_call(
        body, name=name, grid_spec=grid_spec,
        out_shape=jax.ShapeDtypeStruct((2 * rows, cols), F32),
        compiler_params=_params(1))(place, grad, received, from_chips)


def _sum_devices(name, me, gathered, own):
    n_dev, rows, cols = gathered.shape
    tr = _pick(256, [rows])

    def body(me_ref, g_ref, own_ref, o_ref):
        term = lambda d: jnp.where(me_ref[0] == d, own_ref[...], g_ref[d])
        acc = term(0)
        for d in range(1, n_dev):
            acc = acc + term(d)
        o_ref[...] = acc

    grid_spec = pltpu.PrefetchScalarGridSpec(
        num_scalar_prefetch=1, grid=(rows // tr,),
        in_specs=[pl.BlockSpec((n_dev, tr, cols), lambda r, me_ref: (0, r, 0)),
                  pl.BlockSpec((tr, cols), lambda r, me_ref: (r, 0))],
        out_specs=pl.BlockSpec((tr, cols), lambda r, me_ref: (r, 0)))
    return pl.pallas_call(
        body, name=name, grid_spec=grid_spec,
        out_shape=jax.ShapeDtypeStruct((rows, cols), F32),
        compiler_params=_params(1))(me, gathered, own)


def _adamw(name, w, g, m, v):
    rows, cols = w.shape
    tr = _block_rows(rows, cols)
    c1 = 1.0 / (1.0 - ADAM_B1 ** ADAM_STEP)
    c2 = 1.0 / (1.0 - ADAM_B2 ** ADAM_STEP)

    def body(w_ref, g_ref, m_ref, v_ref, go_ref, d_ref, nm_ref, nv_ref):
        gv = g_ref[...]
        go_ref[...] = gv
        nm = ADAM_B1 * m_ref[...] + (1.0 - ADAM_B1) * gv
        nv = ADAM_B2 * v_ref[...] + (1.0 - ADAM_B2) * (gv * gv)
        nm_ref[...] = nm
        nv_ref[...] = nv
        d_ref[...] = -ADAM_LR * ((nm * c1) / (jnp.sqrt(nv * c2) + ADAM_EPS) + ADAM_WD * w_ref[...])

    blk = pl.BlockSpec((tr, cols), lambda r: (r, 0))
    shape = jax.ShapeDtypeStruct((rows, cols), F32)
    return pl.pallas_call(
        body, name=name, grid=(rows // tr,), in_specs=[blk] * 4, out_specs=[blk] * 4,
        out_shape=[shape] * 4, compiler_params=_params(1))(w, g, m, v)


BIG = ("ffn1_w_in", "ffn1_w_out", "w_mix_in", "w_mix_out", "w_cq", "w_ckv", "w_co",
       "ffn2_w_in", "ffn2_w_out")
BIG_KIND = {"ffn1_w_in": "c", "ffn1_w_out": "r", "w_mix_in": "c", "w_mix_out": "r", "w_cq": "r",
            "w_ckv": "c", "w_co": "r", "ffn2_w_in": "c", "ffn2_w_out": "r"}
GATHER_GROUPS = (("ffn1_in", ("ffn1_w_in",)), ("ffn1_out", ("ffn1_w_out",)),
                 ("mix", ("w_mix_in", "w_mix_out")), ("cross", ("w_cq", "w_ckv", "w_co")),
                 ("ffn2_in", ("ffn2_w_in",)), ("ffn2_out", ("ffn2_w_out",)))
GATHER_AFTER = (("ffn1_in", None), ("ffn1_out", "ffn1_in"), ("mix", "ffn1_out"), ("cross", "mix"),
                ("ffn2_in", "mix"), ("ffn2_out", "cross"))
SCATTER_ORDER = ("ffn2", "cross", "mix", "ffn1")
SMALL = ("ffn1_norm", "mix_norm", "ln_v_gain", "ln_v_bias", "spatial_w", "spatial_b", "gnorm_a",
         "gnorm_b", "cross_norm", "mem_norm", "ffn2_norm", "final_norm")
WEIGHTS = ("ffn1_norm", "ffn1_w_in", "ffn1_w_out", "mix_norm", "w_mix_in", "ln_v_gain",
           "ln_v_bias", "spatial_w", "spatial_b", "gnorm_a", "gnorm_b", "w_mix_out", "cross_norm",
           "mem_norm", "w_cq", "w_ckv", "w_co", "ffn2_norm", "ffn2_w_in", "ffn2_w_out",
           "final_norm")


def _pack(arrays):
    return jnp.concatenate([a.reshape(-1, LANE) for a in arrays], axis=0)


def _unpack(packed, like):
    out, row = [], 0
    for a in like:
        rows = a.size // LANE
        out.append(packed[row:row + rows].reshape(a.shape))
        row += rows
    return out


def _local_step(x, mem, target, small, weights_of, start_tokens, grads_done):
    T, D = x.shape
    vec = lambda name: small[name].reshape(1, -1)
    w_a = small["ln_v_gain"].size
    w_b = small["gnorm_b"].size
    G = w_a // GROUP_DIM
    w_s = small["spatial_w"].reshape(G, SGU_BLOCK, SGU_BLOCK)
    b_t = small["spatial_b"].reshape(G, SGU_BLOCK).T

    h1, ffn1_saved = _ffn_forward("ffn1", x, vec("ffn1_norm"), weights_of, deps=start_tokens)
    n2 = _rmsnorm_fwd("mix_norm", h1, vec("mix_norm"))
    big = weights_of("mix", n2)
    (z,) = _matmul("mix_in", Mat(n2), big["w_mix_in"], "nn", [("c", 1, F32)], tm=2048, tn=256)
    z = z[0]
    y = _sgu_forward("sgu", z, vec("ln_v_gain"), vec("ln_v_bias"), w_s, b_t, vec("gnorm_a"), D)
    yb, sb_total = _sb_forward("stickbreak", z, w_a, w_b)
    y = _rmsnorm_fwd("gnorm_b", yb, vec("gnorm_b"), into=y, col=w_a // w_b)

    def add_res(acc, ex, out):
        out[0][...] = ex[0][...] + acc

    (h2,) = _matmul("mix_out", Mat(y), big["w_mix_out"], "nn", [("c", 1, F32)],
                    tm=1024, tn=1024, extras=[Mat(h1)], epi=add_res)
    h2 = h2[0]
    n3 = _rmsnorm_fwd("cross_norm", h2, vec("cross_norm"))
    memn = _rmsnorm_fwd("mem_norm", mem, vec("mem_norm"))
    big.update(weights_of("cross", n3))
    x_scale = (D // X_HEADS) ** -0.5

    def scaled(acc, ex, out):
        out[0][...] = (acc * x_scale).astype(BF16)

    (q,) = _matmul("cross_q", Mat(n3), big["w_cq"], "nn", [("c", 1, BF16)],
                   tm=1024, tn=1024, epi=scaled)
    (kv,) = _matmul("cross_kv", Mat(memn), big["w_ckv"], "nn", [("c", 1, BF16)], tm=256, tn=1024)
    q, kv = q[0], kv[0]
    o = _xattn_forward("cross_attn", q, kv)
    (h3,) = _matmul("cross_out", Mat(o), big["w_co"], "nn", [("c", 1, F32)],
                    tm=1024, tn=1024, extras=[Mat(h2)], epi=add_res)
    h3 = h3[0]
    h4, ffn2_saved = _ffn_forward("ffn2", h3, vec("ffn2_norm"), weights_of)

    gs = {}
    loss_tile, dh4, dh4_bf, gs["final_norm"] = _loss_head("loss_head", h4, vec("final_norm"), target)
    dh3, dh3_bf, gs["ffn2_norm"] = _ffn_backward(
        "ffn2", h3, vec("ffn2_norm"), ffn2_saved, dh4, dh4_bf, grads_done)

    (do,) = _matmul("cross_do", Mat(dh3_bf), big["w_co"], "nt", [("c", 1, BF16)], tm=1024, tn=512)
    (dw_co,) = _matmul("cross_dwo", Mat(o), Mat(dh3_bf), "tn", [("r", N_CHIPS, BF16)],
                       tm=512, tn=1024)
    dq, dkv = _xattn_backward("cross_attn_bwd", q, kv, do[0])
    (dw_cq,) = _matmul("cross_dwq", Mat(n3), Mat(dq), "tn", [("r", N_CHIPS, BF16)],
                       tm=512, tn=1024)
    (dw_ckv,) = _matmul("cross_dwkv", Mat(memn), Mat(dkv), "tn", [("c", N_CHIPS, BF16)],
                        tm=1024, tn=1024)
    token = grads_done("cross", {"w_cq": dw_cq, "w_ckv": dw_ckv, "w_co": dw_co})
    dq = _tie("cross_dq_after_scatter", dq, [token])
    (dn3,) = _matmul("cross_dn", Mat(dq), big["w_cq"], "nt", [("c", 1, F32)], tm=1024, tn=512)
    (dmemn,) = _matmul("cross_dmem", Mat(dkv), big["w_ckv"], "nt", [("c", 1, F32)],
                       tm=256, tn=1024, tk=1024)
    (gs["mem_norm"],) = _rmsnorm_bwd("mem_dnorm", mem, vec("mem_norm"), dmemn[0], want_dx=False)
    dh2, dh2_bf, gs["cross_norm"] = _rmsnorm_bwd("cross_dnorm", h2, vec("cross_norm"), dn3[0],
                                                 dres=dh3)

    (dy,) = _matmul("mix_dy", Mat(dh2_bf), big["w_mix_out"], "nt", [("c", 1, F32)], tm=1024, tn=512)
    dy = dy[0]
    (dw_mix_out,) = _matmul("mix_dwout", Mat(y), Mat(dh2_bf), "tn", [("r", N_CHIPS, BF16)],
                            tm=512, tn=1024)
    dza, gs["ln_v_gain"], gs["ln_v_bias"], gs["spatial_w"], db, gs["gnorm_a"] = _sgu_backward(
        "sgu_bwd", z, dy, vec("ln_v_gain"), vec("ln_v_bias"), w_s, b_t, vec("gnorm_a"))
    gs["spatial_b"] = db.reshape(G, SGU_BLOCK)
    dob, gs["gnorm_b"] = _rmsnorm_bwd("gnorm_b_bwd", yb, vec("gnorm_b"), dy, dn_col=w_a // w_b,
                                      want_bf16=False)
    dqb, dkvb = _sb_backward("stickbreak_bwd", z, dob, sb_total, w_a, w_b)
    dz = jnp.concatenate([dza, dqb, dkvb[0].astype(BF16), dkvb[1].astype(BF16)], axis=1)
    (dw_mix_in,) = _matmul("mix_dwin", Mat(n2), Mat(dz), "tn", [("c", N_CHIPS, BF16)],
                           tm=1024, tn=1280)
    token = grads_done("mix", {"w_mix_in": dw_mix_in, "w_mix_out": dw_mix_out})
    dz = _tie("mix_dz_after_scatter", dz, [token])
    (dn2,) = _matmul("mix_dn", Mat(dz), big["w_mix_in"], "nt", [("c", 1, F32)],
                     tm=1024, tn=1024, tk=1280)
    dh1, dh1_bf, gs["mix_norm"] = _rmsnorm_bwd("mix_dnorm", h1, vec("mix_norm"), dn2[0], dres=dh2)

    dx, _, gs["ffn1_norm"] = _ffn_backward(
        "ffn1", x, vec("ffn1_norm"), ffn1_saved, dh1, dh1_bf, grads_done)
    gs = {k: g.reshape(small[k].shape) for k, g in gs.items()}
    return loss_tile, dx, gs


def kernel(x, mem, ffn1_norm, ffn1_w_in, ffn1_w_out, mix_norm, w_mix_in, ln_v_gain, ln_v_bias, spatial_w, spatial_b, gnorm_a, gnorm_b, w_mix_out, cross_norm, mem_norm, w_cq, w_ckv, w_co, ffn2_norm, ffn2_w_in, ffn2_w_out, final_norm, loss_target, m_ffn1_norm, m_ffn1_w_in, m_ffn1_w_out, m_mix_norm, m_w_mix_in, m_ln_v_gain, m_ln_v_bias, m_spatial_w, m_spatial_b, m_gnorm_a, m_gnorm_b, m_w_mix_out, m_cross_norm, m_mem_norm, m_w_cq, m_w_ckv, m_w_co, m_ffn2_norm, m_ffn2_w_in, m_ffn2_w_out, m_final_norm, v_ffn1_norm, v_ffn1_w_in, v_ffn1_w_out, v_mix_norm, v_w_mix_in, v_ln_v_gain, v_ln_v_bias, v_spatial_w, v_spatial_b, v_gnorm_a, v_gnorm_b, v_w_mix_out, v_cross_norm, v_mem_norm, v_w_cq, v_w_ckv, v_w_co, v_ffn2_norm, v_ffn2_w_in, v_ffn2_w_out, v_final_norm):
    given = dict(locals())
    w = {k: given[k] for k in WEIGHTS}
    m = {k: given["m_" + k] for k in WEIGHTS}
    v = {k: given["v_" + k] for k in WEIGHTS}

    cx, cy, cc = lax.axis_index("x"), lax.axis_index("y"), lax.axis_index("c")
    place = jnp.stack([2 * cx + cy, cc]).astype(jnp.int32)

    names_of = dict(GATHER_GROUPS)
    own = {g: [_cast_own(f"cast_{k}", place, w[k][0]) for k in names] for g, names in GATHER_GROUPS}
    gathers = {}

    def start_gather(group, deps):
        send, recv, arrays, token = _split_start(f"gather_start_{group}", own[group],
                                                 _gather_copies, len(own[group]), deps)
        gathers[group] = (send, recv, arrays)
        return token

    start_tokens = [start_gather(g, ()) for g, after in GATHER_AFTER if after is None]
    start_tokens += [a for g, after in GATHER_AFTER if after is not None for a in own[g]]

    def weights_of(group, after):
        send, recv, arrays = gathers[group]
        arrays = _split_wait(f"gather_wait_{group}", arrays, send, recv, after, _gather_copies)
        tokens = [start_gather(g, (arrays[0],)) for g, a in GATHER_AFTER if a == group]
        arrays = _forward_to_sibling(f"gather_forward_{group}", list(arrays), tokens)
        return {k: Mat(a, BIG_KIND[k]) for k, a in zip(names_of[group], arrays)}

    scatters = {}

    def grads_done(group, partial):
        names = list(partial)
        grads_ = [partial[k] for k in names]
        from_sibling = _swap_halves(f"swap_{group}", grads_)
        sums = [_pair_sum(f"pair_sum_{k}", place, g, r)
                for k, g, r in zip(names, grads_, from_sibling)]
        lands = [lax.empty((3,) + s.shape[1:], s.dtype) for s in sums]
        send, recv, arrays, token = _split_start(f"scatter_start_{group}", sums + lands,
                                                 _scatter_copies, len(names))
        scatters[group] = (names, grads_, from_sibling, send, recv, arrays)
        return token

    small = {k: w[k] for k in SMALL}
    loss_tile, grad_x, gs = _local_step(x[0], mem[0], loss_target[0], small, weights_of,
                                        start_tokens, grads_done)

    grad, delta, new_m, new_v = {}, {}, {}, {}
    for group in SCATTER_ORDER:
        names, grads_, from_sibling, send, recv, arrays = scatters[group]
        arrays = _split_wait(f"scatter_wait_{group}", arrays, send, recv, grad_x, _scatter_copies)
        from_chips = arrays[len(names):]
        shards = [_final_sum(f"final_sum_{k}", place, g, r, f)
                  for k, g, r, f in zip(names, grads_, from_sibling, from_chips)]
        shards = _share_halves(f"share_{group}", shards)
        for k, g_ in zip(names, shards):
            g_, d_, m_, v_ = _adamw(f"adamw_{k}", w[k][0], g_, m[k][0], v[k][0])
            grad[k], delta[k], new_m[k], new_v[k] = g_[None], d_[None], m_[None], v_[None]

    packed = _pack([gs[k] for k in SMALL] + [loss_tile])
    me = (4 * cx + 2 * cy + cc).astype(jnp.int32).reshape(1)
    total = _sum_devices("sum_small", me, _gather_small(packed, [shards[0]]), packed)
    n_small = total.shape[0] - SUBLANE
    loss = total[n_small, 0]
    small_g = total[:n_small]
    g_s, d_s, m_s, v_s = _adamw("adamw_small", _pack([w[k] for k in SMALL]), small_g,
                                _pack([m[k] for k in SMALL]), _pack([v[k] for k in SMALL]))
    like = [w[k] for k in SMALL]
    for k, g_, d_, m_, v_ in zip(SMALL, _unpack(g_s, like), _unpack(d_s, like),
                                 _unpack(m_s, like), _unpack(v_s, like)):
        grad[k], delta[k], new_m[k], new_v[k] = g_, d_, m_, v_

    return (loss, grad_x[None], *[grad[k] for k in WEIGHTS], *[delta[k] for k in WEIGHTS],
            *[new_m[k] for k in WEIGHTS], *[new_v[k] for k in WEIGHTS])
```

```python
import functools
import math

import jax
import jax.numpy as jnp
from jax import lax
from jax.experimental import pallas as pl
from jax.experimental.pallas import tpu as pltpu

F32 = jnp.float32
BF16 = jnp.bfloat16
MESH = pl.DeviceIdType.MESH

EPS = 1e-6
CHUNK = 64
SGU_BLOCK = 128
GROUP_DIM = 128
X_HEADS = 4
N_CHIPS = 4
N_DEV = 8
LANE = 128
SUBLANE = 8
BF16_ROWS = 16

ADAM_LR = 0.001
ADAM_B1 = 0.9
ADAM_B2 = 0.999
ADAM_EPS = 1e-08
ADAM_WD = 0.01
ADAM_STEP = 10

V7X_VMEM_BYTES = 64 << 20
VMEM_LIMIT = V7X_VMEM_BYTES - (8 << 20)


def _params(n_grid):
    return pltpu.CompilerParams(dimension_semantics=("arbitrary",) * n_grid,
                                vmem_limit_bytes=VMEM_LIMIT)


def _pick(pref, dims, unit=None):
    g = functools.reduce(math.gcd, dims)
    if unit is None:
        unit = LANE if g % LANE == 0 else SUBLANE
    cands = [d for d in range(unit, g + 1, unit) if g % d == 0] or [g]
    return min(cands, key=lambda d: abs(math.log(d / pref)))


def _any_spec():
    return pl.BlockSpec(memory_space=pl.ANY)


class Mat:
    def __init__(self, arr, kind="c"):
        if arr.ndim == 2:
            arr = arr[None]
        self.arr, self.kind = arr, kind
        self.P, self.prow, self.pcol = arr.shape
        self.rows = self.prow * (self.P if kind == "r" else 1)
        self.cols = self.pcol * (self.P if kind == "c" else 1)
        self.dtype = arr.dtype

    def spec(self, tr, tc, rc_fn):
        if self.kind == "c":
            per = self.pcol // tc
            assert per * tc == self.pcol, (self.pcol, tc)

            def imap(*g):
                i, j = rc_fn(*g)
                return (j // per, i, j % per)
        else:
            per = self.prow // tr
            assert per * tr == self.prow, (self.prow, tr)

            def imap(*g):
                i, j = rc_fn(*g)
                return (i // per, i % per, j)
        return pl.BlockSpec((None, tr, tc), imap)

    def two_d(self):
        assert self.P == 1
        return self.arr[0]


def _out_mat(kind, P, rows, cols, dtype):
    shape = (P, rows, cols // P) if kind == "c" else (P, rows // P, cols)
    return jax.ShapeDtypeStruct(shape, dtype)


def _matmul(name, A, B, mode, outs, *, tm=1024, tn=1024, tk=2048, extras=(), epi=None):
    if mode == "nn":
        M, K, N = A.rows, A.cols, B.cols
        assert B.rows == K
    elif mode == "nt":
        M, K, N = A.rows, A.cols, B.rows
        assert B.cols == K
    else:
        K, M, N = A.rows, A.cols, B.cols
        assert B.rows == K
    mdims, ndims, kdims = [M], [N], [K]
    whole_b = mode == "nn" and B.kind == "r" and B.P > 1 and K <= tk
    if whole_b:
        kdims.append(A.pcol)
        ndims.append(B.pcol)
    elif mode == "tn":
        assert A.kind == "c" and B.kind == "c"
        mdims.append(A.pcol)
        ndims.append(B.pcol)
    else:
        (mdims if A.kind == "r" else kdims).append(A.prow if A.kind == "r" else A.pcol)
        if mode == "nn":
            (kdims if B.kind == "r" else ndims).append(B.prow if B.kind == "r" else B.pcol)
        else:
            (ndims if B.kind == "r" else kdims).append(B.prow if B.kind == "r" else B.pcol)
    for o in list(outs) + list(extras):
        if isinstance(o, Mat):
            (mdims if o.kind == "r" else ndims).append(o.prow if o.kind == "r" else o.pcol)
        elif isinstance(o[0], str):
            (mdims if o[0] == "r" else ndims).append((M if o[0] == "r" else N) // o[1])
    tm, tn = _pick(tm, mdims), _pick(tn, ndims)
    tk = K if mode == "tn" else _pick(tk, kdims)
    nk = K // tk
    grid = (M // tm, N // tn, nk)

    if mode == "tn":
        a_spec = A.spec(K, tm, lambda m, n, k: (0, m))
        b_spec = B.spec(K, tn, lambda m, n, k: (0, n))
    else:
        a_spec = A.spec(tm, tk, lambda m, n, k: (m, k))
        if whole_b:
            b_spec = pl.BlockSpec((B.P, B.prow, tn), lambda m, n, k: (0, 0, n))
        elif mode == "nn":
            b_spec = B.spec(tk, tn, lambda m, n, k: (k, n))
        else:
            b_spec = B.spec(tn, tk, lambda m, n, k: (n, k))

    def mn_spec(o):
        if isinstance(o, Mat):
            return o.spec(tm, tn, lambda m, n, k: (m, n))
        if isinstance(o[0], str):
            kind, P = o[0], o[1]
            fake = Mat.__new__(Mat)
            fake.kind, fake.P = kind, P
            fake.prow = M // P if kind == "r" else M
            fake.pcol = N // P if kind == "c" else N
            return Mat.spec(fake, tm, tn, lambda m, n, k: (m, n))
        return o[1](tm, tn)

    out_shapes = tuple(_out_mat(o[0], o[1], M, N, o[2]) if isinstance(o[0], str) else o[0]
                       for o in outs)
    out_specs = tuple(mn_spec(o) for o in outs)
    extra_arrays = tuple(e.arr if isinstance(e, Mat) else e[0] for e in extras)
    extra_specs = tuple(mn_spec(e) for e in extras)
    n_ex, n_out = len(extras), len(outs)
    tt = _pick(256, [tm])
    dims = (((1,), (1 if mode == "nt" else 0,)), ((), ()))

    def body(*refs):
        a_ref, b_ref = refs[:2]
        ex_refs = refs[2:2 + n_ex]
        out_refs = refs[2 + n_ex:2 + n_ex + n_out]
        scratch = refs[2 + n_ex + n_out:]
        if mode == "tn":
            at_ref = scratch[0]

            @pl.when(pl.program_id(1) == 0)
            def _():
                for c0 in range(0, tm, tt):
                    at_ref[c0:c0 + tt, :] = a_ref[:, c0:c0 + tt].astype(F32).T.astype(BF16)

            lhs = at_ref[...]
        else:
            lhs = a_ref[...].astype(BF16)
        rhs = b_ref[...].reshape(K, tn) if whole_b else b_ref[...]
        part = lax.dot_general(lhs, rhs.astype(BF16), dims, preferred_element_type=F32)

        def finish(acc):
            if epi is None:
                out_refs[0][...] = acc.astype(out_refs[0].dtype)
            else:
                epi(acc, ex_refs, out_refs)

        if nk == 1:
            finish(part)
        else:
            acc_ref = scratch[0]
            k = pl.program_id(2)

            @pl.when(k == 0)
            def _():
                acc_ref[...] = part

            @pl.when(k > 0)
            def _():
                acc_ref[...] += part

            @pl.when(k == nk - 1)
            def _():
                finish(acc_ref[...])

    scratch_shapes = []
    if mode == "tn":
        scratch_shapes.append(pltpu.VMEM((tm, K), BF16))
    elif nk > 1:
        scratch_shapes.append(pltpu.VMEM((tm, tn), F32))
    res = pl.pallas_call(
        body, name=name, grid=grid,
        in_specs=[a_spec, b_spec, *extra_specs], out_specs=out_specs, out_shape=out_shapes,
        scratch_shapes=scratch_shapes, compiler_params=_params(3),
    )(A.arr, B.arr, *extra_arrays)
    return res


def _row_tile(T):
    return _pick(256, [T])


def _tie(name, x, deps):
    def body(*refs):
        refs[-1][...] = jnp.zeros_like(refs[-1])

    return pl.pallas_call(
        body, name=name, in_specs=[_any_spec()] * (1 + len(deps)),
        out_specs=(_any_spec(), pl.BlockSpec(memory_space=pltpu.VMEM)),
        out_shape=(jax.ShapeDtypeStruct(x.shape, x.dtype),
                   jax.ShapeDtypeStruct((SUBLANE, LANE), F32)),
        input_output_aliases={0: 0},
    )(x, *deps)[0]


def _rmsnorm_fwd(name, x, g, *, into=None, col=0, deps=()):
    T, W = x.shape
    tr = _row_tile(T)

    def body(x_ref, g_ref, *rest):
        o_ref = rest[-1]
        xv = x_ref[...]
        rstd = lax.rsqrt(jnp.mean(xv * xv, axis=-1, keepdims=True) + EPS)
        o_ref[...] = (xv * rstd * g_ref[...]).astype(o_ref.dtype)

    in_specs = [pl.BlockSpec((tr, W), lambda i: (i, 0)), pl.BlockSpec((1, W), lambda i: (0, 0))]
    args = [x, g]
    kwargs = {}
    if into is None:
        out_shape = jax.ShapeDtypeStruct((T, W), BF16)
    else:
        out_shape = jax.ShapeDtypeStruct(into.shape, into.dtype)
        in_specs.append(_any_spec())
        args.append(into)
        kwargs["input_output_aliases"] = {2: 0}
    in_specs += [_any_spec()] * len(deps)
    args += list(deps)
    return pl.pallas_call(
        body, name=name, grid=(T // tr,), in_specs=in_specs,
        out_specs=pl.BlockSpec((tr, W), lambda i: (i, col)), out_shape=out_shape,
        compiler_params=_params(1), **kwargs)(*args)


def _rmsnorm_bwd(name, x, g, dn, *, dn_col=0, dres=None, want_dx=True, want_bf16=True):
    T, W = x.shape
    tr = _row_tile(T)
    has_res = dres is not None

    def body(*refs):
        x_ref, g_ref, dn_ref = refs[:3]
        pos = 3
        dres_ref = None
        if has_res:
            dres_ref = refs[pos]
            pos += 1
        outs = refs[pos:]
        dg_ref = outs[-1]
        xv = x_ref[...]
        rstd = lax.rsqrt(jnp.mean(xv * xv, axis=-1, keepdims=True) + EPS)
        xhat = xv * rstd
        dnv = dn_ref[...].astype(F32)

        @pl.when(pl.program_id(0) == 0)
        def _():
            dg_ref[...] = jnp.zeros_like(dg_ref)

        dg_ref[...] += jnp.sum(dnv * xhat, axis=0, keepdims=True)
        if want_dx:
            t = dnv * g_ref[...]
            dx = rstd * (t - xhat * jnp.mean(t * xhat, axis=-1, keepdims=True))
            if has_res:
                dx = dx + dres_ref[...]
            outs[0][...] = dx
            if want_bf16:
                outs[1][...] = dx.astype(BF16)

    row = pl.BlockSpec((tr, W), lambda i: (i, 0))
    in_specs = [row, pl.BlockSpec((1, W), lambda i: (0, 0)),
                pl.BlockSpec((tr, W), lambda i: (i, dn_col))]
    args = [x, g, dn]
    if has_res:
        in_specs.append(row)
        args.append(dres)
    out_shape, out_specs = [], []
    if want_dx:
        out_shape.append(jax.ShapeDtypeStruct((T, W), F32))
        out_specs.append(row)
        if want_bf16:
            out_shape.append(jax.ShapeDtypeStruct((T, W), BF16))
            out_specs.append(row)
    out_shape.append(jax.ShapeDtypeStruct((1, W), F32))
    out_specs.append(pl.BlockSpec((1, W), lambda i: (0, 0)))
    return pl.pallas_call(
        body, name=name, grid=(T // tr,), in_specs=in_specs, out_specs=out_specs,
        out_shape=out_shape, compiler_params=_params(1))(*args)


def _loss_head(name, h, g, target):
    T, W = h.shape
    tr = _row_tile(T)

    def body(h_ref, g_ref, t_ref, loss_ref, dx_ref, dxb_ref, dg_ref):
        xv = h_ref[...]
        gv = g_ref[...]
        rstd = lax.rsqrt(jnp.mean(xv * xv, axis=-1, keepdims=True) + EPS)
        xhat = xv * rstd
        diff = xhat * gv - t_ref[...]

        @pl.when(pl.program_id(0) == 0)
        def _():
            dg_ref[...] = jnp.zeros_like(dg_ref)
            loss_ref[...] = jnp.zeros_like(loss_ref)

        loss_ref[...] += 0.5 * jnp.sum(jnp.mean(diff * diff, axis=-1, keepdims=True))
        dnv = diff * (1.0 / W)
        dg_ref[...] += jnp.sum(dnv * xhat, axis=0, keepdims=True)
        t = dnv * gv
        dx = rstd * (t - xhat * jnp.mean(t * xhat, axis=-1, keepdims=True))
        dx_ref[...] = dx
        dxb_ref[...] = dx.astype(BF16)

    row = pl.BlockSpec((tr, W), lambda i: (i, 0))
    vec = pl.BlockSpec((1, W), lambda i: (0, 0))
    return pl.pallas_call(
        body, name=name, grid=(T // tr,), in_specs=[row, vec, row],
        out_specs=[pl.BlockSpec((SUBLANE, LANE), lambda i: (0, 0)), row, row, vec],
        out_shape=[jax.ShapeDtypeStruct((SUBLANE, LANE), F32), jax.ShapeDtypeStruct((T, W), F32),
                   jax.ShapeDtypeStruct((T, W), BF16), jax.ShapeDtypeStruct((1, W), F32)],
        compiler_params=_params(1))(h, g, target)


def _sigmoid(x):
    return 1.0 / (1.0 + jnp.exp(-x))


def _ffn_in(name, n, W):
    T, D = n.shape
    F = W.cols // 2
    tm = _pick(2048, [T])
    tn = _pick(512, [W.pcol])
    per = W.pcol // tn

    def body(a_ref, wg_ref, wu_ref, gu_ref, act_ref):
        a = a_ref[...]
        gate = jnp.dot(a, wg_ref[...], preferred_element_type=F32)
        up = jnp.dot(a, wu_ref[...], preferred_element_type=F32)
        gu_ref[0] = gate.astype(BF16)
        gu_ref[1] = up.astype(BF16)
        act_ref[...] = (gate * _sigmoid(gate) * up).astype(BF16)

    return pl.pallas_call(
        body, name=name, grid=(T // tm, F // tn),
        in_specs=[pl.BlockSpec((tm, D), lambda m, j: (m, 0)),
                  pl.BlockSpec((None, D, tn), lambda m, j: (j // per, 0, j % per)),
                  pl.BlockSpec((None, D, tn), lambda m, j: (2 + j // per, 0, j % per))],
        out_specs=[pl.BlockSpec((2, tm, tn), lambda m, j: (0, m, j)),
                   pl.BlockSpec((tm, tn), lambda m, j: (m, j))],
        out_shape=[jax.ShapeDtypeStruct((2, T, F), BF16), jax.ShapeDtypeStruct((T, F), BF16)],
        compiler_params=_params(2))(n, W.arr, W.arr)


def _ffn_forward(tag, h, norm_g, weights_of, deps=()):
    n = _rmsnorm_fwd(f"{tag}_norm", h, norm_g, deps=deps)
    w_in = weights_of(f"{tag}_in", n)[f"{tag}_w_in"]
    gu, act = _ffn_in(f"{tag}_in", n, w_in)
    w_out = weights_of(f"{tag}_out", act)[f"{tag}_w_out"]

    def epi(acc, ex, out):
        out[0][...] = ex[0][...] + 0.5 * acc

    (h_out,) = _matmul(f"{tag}_out", Mat(act), w_out, "nn", [("c", 1, F32)],
                       tm=1024, tn=1024, tk=1408, extras=[Mat(h)], epi=epi)
    return h_out[0], (n, gu, act, w_in, w_out)


def _ffn_backward(tag, h_in, norm_g, saved, dh, dh_bf, grads_ready, grads_flush):
    n, gu, act, w_in, w_out = saved
    T, F = act.shape

    def epi(acc, ex, out):
        dact = 0.5 * acc
        gate = ex[0][0].astype(F32)
        up = ex[0][1].astype(F32)
        sig = _sigmoid(gate)
        out[0][0] = (dact * up * sig * (1.0 + gate * (1.0 - sig))).astype(BF16)
        out[0][1] = (dact * gate * sig).astype(BF16)

    def pair_spec(tm, tn):
        return pl.BlockSpec((2, tm, tn), lambda m, j, k: (0, m, j))

    (dgu,) = _matmul(f"{tag}_dact", Mat(dh_bf), w_out, "nt",
                     [(jax.ShapeDtypeStruct((2, T, F), BF16), pair_spec)],
                     tm=512, tn=1408, extras=[(gu, pair_spec)], epi=epi)

    def half(acc, ex, out):
        out[0][...] = (0.5 * acc).astype(out[0].dtype)

    (dw_out,) = _matmul(f"{tag}_dwout", Mat(act), Mat(dh_bf), "tn", [("r", N_CHIPS, BF16)],
                        tm=1408, tn=512, epi=half)
    (dw_in,) = _matmul(f"{tag}_dwin", Mat(n), Mat(dgu), "tn", [("c", N_CHIPS, BF16)],
                       tm=512, tn=1408)
    token = grads_ready(tag, {f"{tag}_w_in": dw_in, f"{tag}_w_out": dw_out})
    dgu = _tie(f"{tag}_dgu_after_swap", dgu, [token])
    (dn,) = _matmul(f"{tag}_dn", Mat(dgu), w_in, "nt", [("c", 1, F32)],
                    tm=1024, tn=1024, tk=2816)
    dn = _tie(f"{tag}_dn_after_scatter", dn, [grads_flush(tag, dn)])
    return _rmsnorm_bwd(f"{tag}_dnorm", h_in, norm_g, dn[0], dres=dh)


_GELU_C = math.sqrt(2.0 / math.pi)
_GELU_A = 0.044715


def _gelu(x):
    return 0.5 * x * (1.0 + jnp.tanh(_GELU_C * (x + _GELU_A * x * x * x)))


def _gelu_grad(x):
    th = jnp.tanh(_GELU_C * (x + _GELU_A * x * x * x))
    return 0.5 * (1.0 + th) + 0.5 * x * (1.0 - th * th) * _GELU_C * (1.0 + 3.0 * _GELU_A * x * x)


def _chunk_mask():
    t = lax.broadcasted_iota(jnp.int32, (SGU_BLOCK, SGU_BLOCK), 0) // CHUNK
    s = lax.broadcasted_iota(jnp.int32, (SGU_BLOCK, SGU_BLOCK), 1) // CHUNK
    return s <= t


def _sgu_group_forward(v_g, lg, lb, wm_bf, b_col):
    mu = jnp.mean(v_g, axis=-1, keepdims=True)
    xc = v_g - mu
    rstd = lax.rsqrt(jnp.mean(xc * xc, axis=-1, keepdims=True) + EPS)
    vhat = xc * rstd
    vn = vhat * lg + lb
    mixed = jnp.dot(wm_bf, vn.astype(BF16), preferred_element_type=F32) + b_col
    return vhat, rstd, vn, mixed


def _sgu_forward(name, z, ln_g, ln_b, w_s, b_t, gn, d_model):
    T = z.shape[0]
    W_A = ln_g.shape[1]
    G = W_A // GROUP_DIM

    def body(z_ref, lg_ref, lb_ref, w_ref, bt_ref, gn_ref, y_ref):
        mask = _chunk_mask()
        u = _gelu(z_ref[:, :W_A])
        v = _gelu(z_ref[:, W_A:])
        cols = []
        for g in range(G):
            sl = slice(g * GROUP_DIM, (g + 1) * GROUP_DIM)
            wm = jnp.where(mask, w_ref[g], 0.0).astype(BF16)
            _, _, _, mixed = _sgu_group_forward(v[:, sl], lg_ref[:, sl], lb_ref[:, sl], wm,
                                                bt_ref[:, g:g + 1])
            cols.append(u[:, sl] * mixed)
        ya = jnp.concatenate(cols, axis=1)
        rstd = lax.rsqrt(jnp.mean(ya * ya, axis=-1, keepdims=True) + EPS)
        y_ref[...] = (ya * rstd * gn_ref[...]).astype(BF16)

    vec = pl.BlockSpec((1, W_A), lambda i: (0, 0))
    return pl.pallas_call(
        body, name=name, grid=(T // SGU_BLOCK,),
        in_specs=[pl.BlockSpec((SGU_BLOCK, 2 * W_A), lambda i: (i, 0)), vec, vec,
                  pl.BlockSpec((G, SGU_BLOCK, SGU_BLOCK), lambda i: (0, 0, 0)),
                  pl.BlockSpec((SGU_BLOCK, G), lambda i: (0, 0)), vec],
        out_specs=pl.BlockSpec((SGU_BLOCK, W_A), lambda i: (i, 0)),
        out_shape=jax.ShapeDtypeStruct((T, d_model), BF16),
        compiler_params=_params(1))(z, ln_g, ln_b, w_s, b_t, gn)


def _sgu_backward(name, z, dy, ln_g, ln_b, w_s, b_t, gn):
    T = z.shape[0]
    W_A = ln_g.shape[1]
    G = W_A // GROUP_DIM

    def body(z_ref, dy_ref, lg_ref, lb_ref, w_ref, bt_ref, gn_ref,
             dz_ref, dlg_ref, dlb_ref, dw_ref, db_ref, dgn_ref):
        @pl.when(pl.program_id(0) == 0)
        def _():
            for r in (dlg_ref, dlb_ref, dw_ref, db_ref, dgn_ref):
                r[...] = jnp.zeros_like(r)

        mask = _chunk_mask()
        zu = z_ref[:, :W_A]
        zv = z_ref[:, W_A:]
        u = _gelu(zu)
        v = _gelu(zv)
        saved, cols = [], []
        for g in range(G):
            sl = slice(g * GROUP_DIM, (g + 1) * GROUP_DIM)
            wm = jnp.where(mask, w_ref[g], 0.0)
            vhat, rstd, vn, mixed = _sgu_group_forward(
                v[:, sl], lg_ref[:, sl], lb_ref[:, sl], wm.astype(BF16), bt_ref[:, g:g + 1])
            saved.append((wm, vhat, rstd, vn, mixed))
            cols.append(u[:, sl] * mixed)
        ya = jnp.concatenate(cols, axis=1)
        rstd_a = lax.rsqrt(jnp.mean(ya * ya, axis=-1, keepdims=True) + EPS)
        ya_hat = ya * rstd_a
        dyv = dy_ref[...].astype(F32)
        dgn_ref[...] += jnp.sum(dyv * ya_hat, axis=0, keepdims=True)
        t = dyv * gn_ref[...]
        dya = rstd_a * (t - ya_hat * jnp.mean(t * ya_hat, axis=-1, keepdims=True))
        du_cols, dv_cols, dlg_cols, dlb_cols = [], [], [], []
        for g in range(G):
            sl = slice(g * GROUP_DIM, (g + 1) * GROUP_DIM)
            wm, vhat, rstd, vn, mixed = saved[g]
            dya_g = dya[:, sl]
            du_cols.append(dya_g * mixed)
            dmix = dya_g * u[:, sl]
            dmix_bf = dmix.astype(BF16)
            db_ref[g] += jnp.sum(dmix, axis=1, keepdims=True)
            dw = lax.dot_general(dmix_bf, vn.astype(BF16), (((1,), (1,)), ((), ())),
                                 preferred_element_type=F32)
            dw_ref[g] += jnp.where(mask, dw, 0.0)
            dvn = jnp.dot(wm.T.astype(BF16), dmix_bf, preferred_element_type=F32)
            dlg_cols.append(jnp.sum(dvn * vhat, axis=0, keepdims=True))
            dlb_cols.append(jnp.sum(dvn, axis=0, keepdims=True))
            dvhat = dvn * lg_ref[:, sl]
            dv_cols.append(rstd * (dvhat - jnp.mean(dvhat, axis=-1, keepdims=True)
                                   - vhat * jnp.mean(dvhat * vhat, axis=-1, keepdims=True)))
        dlg_ref[...] += jnp.concatenate(dlg_cols, axis=1)
        dlb_ref[...] += jnp.concatenate(dlb_cols, axis=1)
        dz_ref[:, :W_A] = (jnp.concatenate(du_cols, axis=1) * _gelu_grad(zu)).astype(BF16)
        dz_ref[:, W_A:] = (jnp.concatenate(dv_cols, axis=1) * _gelu_grad(zv)).astype(BF16)

    vec = pl.BlockSpec((1, W_A), lambda i: (0, 0))
    wspec = pl.BlockSpec((G, SGU_BLOCK, SGU_BLOCK), lambda i: (0, 0, 0))
    return pl.pallas_call(
        body, name=name, grid=(T // SGU_BLOCK,),
        in_specs=[pl.BlockSpec((SGU_BLOCK, 2 * W_A), lambda i: (i, 0)),
                  pl.BlockSpec((SGU_BLOCK, W_A), lambda i: (i, 0)), vec, vec, wspec,
                  pl.BlockSpec((SGU_BLOCK, G), lambda i: (0, 0)), vec],
        out_specs=[pl.BlockSpec((SGU_BLOCK, 2 * W_A), lambda i: (i, 0)), vec, vec, wspec,
                   pl.BlockSpec((G, SGU_BLOCK, 1), lambda i: (0, 0, 0)), vec],
        out_shape=[jax.ShapeDtypeStruct((T, 2 * W_A), BF16), jax.ShapeDtypeStruct((1, W_A), F32),
                   jax.ShapeDtypeStruct((1, W_A), F32),
                   jax.ShapeDtypeStruct((G, SGU_BLOCK, SGU_BLOCK), F32),
                   jax.ShapeDtypeStruct((G, SGU_BLOCK, 1), F32),
                   jax.ShapeDtypeStruct((1, W_A), F32)],
        compiler_params=_params(1))(z, dy, ln_g, ln_b, w_s, b_t, gn)


def _split_dot(x, tri):
    hi = x.astype(BF16)
    lo = (x - hi.astype(F32)).astype(BF16)
    return (jnp.dot(hi, tri, preferred_element_type=F32)
            + jnp.dot(lo, tri, preferred_element_type=F32))


def _tri(n, rel):
    r = lax.broadcasted_iota(jnp.int32, (n, n), 0)
    c = lax.broadcasted_iota(jnp.int32, (n, n), 1)
    return rel(r, c).astype(BF16)


def _dot_nt(a, b):
    return lax.dot_general(a, b, (((1,), (1,)), ((), ())), preferred_element_type=F32)


def _dot_tn(a, b):
    return lax.dot_general(a, b, (((0,), (0,)), ((), ())), preferred_element_type=F32)


def _sb_scores(qs, kj, q0, k0):
    zz = _dot_nt(qs, kj)
    tq, tk = zz.shape
    tpos = q0 + lax.broadcasted_iota(jnp.int32, (tq, tk), 0)
    spos = k0 + lax.broadcasted_iota(jnp.int32, (tq, tk), 1)
    mask = spos < tpos
    log_beta = jnp.minimum(zz, 0.0) - jnp.log(1.0 + jnp.exp(-jnp.abs(zz)))
    log_1m = jnp.where(mask, log_beta - zz, 0.0)
    return log_beta, log_1m, mask


def _sb_tiles(T):
    tk = _pick(256, [T])
    tq = 2 * tk if T % (2 * tk) == 0 else tk
    return tq, tk


def _sb_cols(w_a, w_b):
    base = 2 * w_a // GROUP_DIM
    per = w_b // GROUP_DIM
    return base, base + per, base + 2 * per


def _sb_forward(name, z, w_a, w_b):
    T = z.shape[0]
    H = w_b // GROUP_DIM
    tq, tk = _sb_tiles(T)
    qc, kc, vc = _sb_cols(w_a, w_b)
    scale = GROUP_DIM ** -0.5

    def body(q_ref, k_ref, v_ref, y_ref, tot_ref):
        i = pl.program_id(1)
        qs = (q_ref[...] * scale).astype(BF16)
        upper = _tri(tk, lambda r, c: r > c)
        n_blocks = (i + 1) * (tq // tk)

        def step(jj, carry):
            acc, later = carry
            k0 = pl.multiple_of((n_blocks - 1 - jj) * tk, tk)
            kj = k_ref[pl.ds(k0, tk), :].astype(BF16)
            vj = v_ref[pl.ds(k0, tk), :].astype(BF16)
            log_beta, log_1m, mask = _sb_scores(qs, kj, i * tq, k0)
            rest = _split_dot(log_1m, upper) + later
            a = jnp.where(mask, jnp.exp(log_beta + rest), 0.0)
            acc = acc + jnp.dot(a.astype(BF16), vj, preferred_element_type=F32)
            return acc, later + jnp.sum(log_1m, axis=1, keepdims=True)

        acc, total = lax.fori_loop(
            0, n_blocks, step, (jnp.zeros((tq, GROUP_DIM), F32), jnp.zeros((tq, 1), F32)))
        y_ref[...] = acc
        tot_ref[...] = total

    return pl.pallas_call(
        body, name=name, grid=(H, T // tq),
        in_specs=[pl.BlockSpec((tq, GROUP_DIM), lambda h, i: (i, qc + h)),
                  pl.BlockSpec((T, GROUP_DIM), lambda h, i: (0, kc + h)),
                  pl.BlockSpec((T, GROUP_DIM), lambda h, i: (0, vc + h))],
        out_specs=[pl.BlockSpec((tq, GROUP_DIM), lambda h, i: (i, h)),
                   pl.BlockSpec((None, tq, 1), lambda h, i: (h, i, 0))],
        out_shape=[jax.ShapeDtypeStruct((T, w_b), F32), jax.ShapeDtypeStruct((H, T, 1), F32)],
        compiler_params=_params(2))(z, z, z)


def _sb_backward(name, z, do, total, w_a, w_b):
    T = z.shape[0]
    H = w_b // GROUP_DIM
    tq, tk = _sb_tiles(T)
    qc, kc, vc = _sb_cols(w_a, w_b)
    scale = GROUP_DIM ** -0.5

    def body(q_ref, k_ref, v_ref, do_ref, tot_ref, dq_ref, dkv_ref):
        i = pl.program_id(1)

        @pl.when(i == 0)
        def _():
            dkv_ref[...] = jnp.zeros_like(dkv_ref)

        qs = (q_ref[...] * scale).astype(BF16)
        dob = do_ref[...].astype(BF16)
        total_v = tot_ref[...]
        upto = _tri(tk, lambda r, c: r <= c)
        before = _tri(tk, lambda r, c: r < c)

        def step(j, carry):
            dq, seen, e_seen = carry
            k0 = pl.multiple_of(j * tk, tk)
            kj = k_ref[pl.ds(k0, tk), :].astype(BF16)
            vj = v_ref[pl.ds(k0, tk), :].astype(BF16)
            log_beta, log_1m, mask = _sb_scores(qs, kj, i * tq, k0)
            rest = total_v - (seen + _split_dot(log_1m, upto))
            a = jnp.where(mask, jnp.exp(log_beta + rest), 0.0)
            e = a * _dot_nt(dob, vj)
            e_before = e_seen + jnp.dot(e.astype(BF16), before, preferred_element_type=F32)
            beta = jnp.exp(log_beta)
            dz = jnp.where(mask, e * (1.0 - beta) - beta * e_before, 0.0).astype(BF16)
            dq = dq + jnp.dot(dz, kj, preferred_element_type=F32)
            dkv_ref[0, pl.ds(k0, tk), :] += _dot_tn(dz, qs)
            dkv_ref[1, pl.ds(k0, tk), :] += _dot_tn(a.astype(BF16), dob)
            return (dq, seen + jnp.sum(log_1m, axis=1, keepdims=True),
                    e_seen + jnp.sum(e, axis=1, keepdims=True))

        zero_col = jnp.zeros((tq, 1), F32)
        dq, _, _ = lax.fori_loop(0, (i + 1) * (tq // tk), step,
                                 (jnp.zeros((tq, GROUP_DIM), F32), zero_col, zero_col))
        dq_ref[...] = (dq * scale).astype(BF16)

    return pl.pallas_call(
        body, name=name, grid=(H, T // tq),
        in_specs=[pl.BlockSpec((tq, GROUP_DIM), lambda h, i: (i, qc + h)),
                  pl.BlockSpec((T, GROUP_DIM), lambda h, i: (0, kc + h)),
                  pl.BlockSpec((T, GROUP_DIM), lambda h, i: (0, vc + h)),
                  pl.BlockSpec((tq, GROUP_DIM), lambda h, i: (i, h)),
                  pl.BlockSpec((None, tq, 1), lambda h, i: (h, i, 0))],
        out_specs=[pl.BlockSpec((tq, GROUP_DIM), lambda h, i: (i, h)),
                   pl.BlockSpec((2, T, GROUP_DIM), lambda h, i: (0, 0, h))],
        out_shape=[jax.ShapeDtypeStruct((T, w_b), BF16), jax.ShapeDtypeStruct((2, T, w_b), F32)],
        compiler_params=_params(2))(z, z, z, do, total)


def _softmax_rows(s):
    m = jnp.max(s, axis=-1, keepdims=True)
    p = jnp.exp(s - m)
    return p / jnp.sum(p, axis=-1, keepdims=True)


def _xattn_forward(name, q, kv):
    T, D = q.shape
    Nm = kv.shape[0]
    dh = D // X_HEADS
    tq = _pick(512, [T])

    def body(q_ref, k_ref, v_ref, o_ref):
        p = _softmax_rows(_dot_nt(q_ref[...], k_ref[...]))
        o_ref[...] = jnp.dot(p.astype(BF16), v_ref[...], preferred_element_type=F32).astype(BF16)

    return pl.pallas_call(
        body, name=name, grid=(T // tq, X_HEADS),
        in_specs=[pl.BlockSpec((tq, dh), lambda i, h: (i, h)),
                  pl.BlockSpec((Nm, dh), lambda i, h: (0, h)),
                  pl.BlockSpec((Nm, dh), lambda i, h: (0, X_HEADS + h))],
        out_specs=pl.BlockSpec((tq, dh), lambda i, h: (i, h)),
        out_shape=jax.ShapeDtypeStruct((T, D), BF16),
        compiler_params=_params(2))(q, kv, kv)


def _xattn_backward(name, q, kv, do):
    T, D = q.shape
    Nm = kv.shape[0]
    dh = D // X_HEADS
    tq = _pick(512, [T])
    scale = dh ** -0.5

    def body(q_ref, k_ref, v_ref, do_ref, dq_ref, dkv_ref):
        @pl.when(pl.program_id(1) == 0)
        def _():
            dkv_ref[...] = jnp.zeros_like(dkv_ref)

        qv, kk, vv, dov = q_ref[...], k_ref[...], v_ref[...], do_ref[...]
        p = _softmax_rows(_dot_nt(qv, kk))
        dp = _dot_nt(dov, vv)
        ds = (p * (dp - jnp.sum(dp * p, axis=-1, keepdims=True))).astype(BF16)
        dq_ref[...] = (jnp.dot(ds, kk, preferred_element_type=F32) * scale).astype(BF16)
        dkv_ref[0] += _dot_tn(ds, qv)
        dkv_ref[1] += _dot_tn(p.astype(BF16), dov)

    blk = pl.BlockSpec((tq, dh), lambda h, i: (i, h))
    return pl.pallas_call(
        body, name=name, grid=(X_HEADS, T // tq),
        in_specs=[blk, pl.BlockSpec((Nm, dh), lambda h, i: (0, h)),
                  pl.BlockSpec((Nm, dh), lambda h, i: (0, X_HEADS + h)), blk],
        out_specs=[blk, pl.BlockSpec((2, Nm, dh), lambda h, i: (0, 0, h))],
        out_shape=[jax.ShapeDtypeStruct((T, D), BF16), jax.ShapeDtypeStruct((2, Nm, D), F32)],
        compiler_params=_params(2))(q, kv, kv, do)


def _position():
    x, y, c = lax.axis_index("x"), lax.axis_index("y"), lax.axis_index("c")
    other_chips = [(1 - x, y), (x, 1 - y), (1 - x, 1 - y)]
    return x, y, c, other_chips


def _hbm_spec():
    return pl.BlockSpec(memory_space=pltpu.HBM)


def _sem_spec():
    return pl.BlockSpec(memory_space=pltpu.SEMAPHORE)


def _split_start(name, arrays, make_copies, n_sems, deps=()):
    n, d = len(arrays), len(deps)

    def body(*refs):
        ins = refs[:n]
        send_sems, recv_sems = refs[n + d], refs[n + d + 1]
        token = refs[-1]
        for cp in make_copies(ins, send_sems, recv_sems):
            cp.start()
        token[...] = jnp.zeros_like(token)

    res = pl.pallas_call(
        body, name=name,
        out_shape=(pltpu.SemaphoreType.DMA((n_sems,)), pltpu.SemaphoreType.DMA((n_sems,)),
                   *[pltpu.HBM(a.shape, a.dtype) for a in arrays],
                   jax.ShapeDtypeStruct((SUBLANE, LANE), F32)),
        in_specs=[_hbm_spec()] * n + [_any_spec()] * d,
        out_specs=(_sem_spec(), _sem_spec(), *[_hbm_spec()] * n,
                   pl.BlockSpec(memory_space=pltpu.VMEM)),
        input_output_aliases={i: 2 + i for i in range(n)},
        compiler_params=pltpu.CompilerParams(
            has_side_effects=pltpu.SideEffectType.DATAFLOW_SIDE_EFFECTING),
    )(*[pltpu.with_memory_space_constraint(a, pltpu.HBM) for a in arrays], *deps)
    return res[0], res[1], list(res[2:2 + n]), res[-1]


def _split_wait(name, arrays, send_sems, recv_sems, after, make_copies):
    n = len(arrays)

    def body(*refs):
        ins = refs[:n]
        send_ref, recv_ref = refs[n], refs[n + 1]
        for cp in make_copies(ins, send_ref, recv_ref):
            cp.wait_send()
            cp.wait_recv()

    return pl.pallas_call(
        body, name=name,
        out_shape=tuple(pltpu.HBM(a.shape, a.dtype) for a in arrays),
        in_specs=[_hbm_spec()] * n + [_sem_spec(), _sem_spec(), _any_spec()],
        out_specs=tuple(_hbm_spec() for _ in arrays),
        input_output_aliases={i: i for i in range(n)},
        compiler_params=pltpu.CompilerParams(
            has_side_effects=pltpu.SideEffectType.DATAFLOW_SIDE_EFFECTING),
    )(*arrays, send_sems, recv_sems, after)


def _gather_copies(refs, send_sems, recv_sems):
    x, y, c, chips = _position()
    me = 2 * x + y
    copies = []
    for i, ref in enumerate(refs):
        rows = ref.shape[1] // 2
        piece = ref.at[me, pl.ds(c * rows, rows), :]
        for j, (px, py) in enumerate(chips):
            copies.append(pltpu.make_async_remote_copy(
                src_ref=piece, dst_ref=piece, send_sem=send_sems.at[3 * i + j],
                recv_sem=recv_sems.at[3 * i + j], device_id=(px, py, c), device_id_type=MESH))
    return copies


def _scatter_copies(refs, send_sems, recv_sems):
    x, y, c, chips = _position()
    n = len(refs) // 2
    copies = []
    for i in range(n):
        for j, (px, py) in enumerate(chips):
            copies.append(pltpu.make_async_remote_copy(
                src_ref=refs[i].at[2 * px + py], dst_ref=refs[n + i].at[j],
                send_sem=send_sems.at[3 * i + j], recv_sem=recv_sems.at[3 * i + j],
                device_id=(px, py, c), device_id_type=MESH))
    return copies


def _cast_own(name, place, shard):
    rows, cols = shard.shape
    tr = _block_rows(rows, cols)

    def body(place_ref, w_ref, o_ref):
        o_ref[...] = w_ref[...].astype(BF16)

    grid_spec = pltpu.PrefetchScalarGridSpec(
        num_scalar_prefetch=1, grid=(rows // tr,),
        in_specs=[pl.BlockSpec((tr, cols), lambda r, pr: (r, 0))],
        out_specs=pl.BlockSpec((None, tr, cols), lambda r, pr: (pr[0], r, 0)))
    return pl.pallas_call(
        body, name=name, grid_spec=grid_spec,
        out_shape=jax.ShapeDtypeStruct((N_CHIPS, rows, cols), BF16),
        compiler_params=_params(1))(place, shard)


def _forward_to_sibling(name, arrays, deps=()):
    n = len(arrays)

    def body(*refs):
        ins = refs[:n]
        send_sems, recv_sems = refs[-2:]
        x, y, c, chips = _position()
        sends = []
        for i in range(n):
            rows = ins[i].shape[1] // 2
            for j, (px, py) in enumerate(chips):
                piece = ins[i].at[2 * px + py, pl.ds(c * rows, rows), :]
                cp = pltpu.make_async_remote_copy(
                    src_ref=piece, dst_ref=piece, send_sem=send_sems.at[i, j],
                    recv_sem=recv_sems.at[i, j], device_id=(x, y, 1 - c), device_id_type=MESH)
                cp.start()
                sends.append(cp)
        for i in range(n):
            rows = ins[i].shape[1] // 2
            for j, (px, py) in enumerate(chips):
                piece = ins[i].at[2 * px + py, pl.ds((1 - c) * rows, rows), :]
                pltpu.make_async_remote_copy(
                    src_ref=piece, dst_ref=piece, send_sem=send_sems.at[i, j],
                    recv_sem=recv_sems.at[i, j], device_id=(x, y, 1 - c),
                    device_id_type=MESH).wait_recv()
        for cp in sends:
            cp.wait_send()

    return pl.pallas_call(
        body, name=name,
        in_specs=[_any_spec()] * (n + len(deps)), out_specs=[_any_spec()] * n,
        out_shape=[jax.ShapeDtypeStruct(a.shape, a.dtype) for a in arrays],
        input_output_aliases={i: i for i in range(n)},
        scratch_shapes=[pltpu.SemaphoreType.DMA((n, 3))] * 2,
    )(*arrays, *deps)


def _swap_copies(refs, send_sems, recv_sems):
    x, y, c, _ = _position()
    n = len(refs) // 2
    copies = []
    for i in range(n):
        rows = refs[i].shape[1] // 2
        copies.append(pltpu.make_async_remote_copy(
            src_ref=refs[i].at[:, pl.ds((1 - c) * rows, rows), :], dst_ref=refs[n + i],
            send_sem=send_sems.at[i], recv_sem=recv_sems.at[i],
            device_id=(x, y, 1 - c), device_id_type=MESH))
    return copies


def _share_halves(name, shards):
    n = len(shards)

    def body(*refs):
        ins = refs[:n]
        send_sems, recv_sems = refs[2 * n:]
        x, y, c, _ = _position()
        copies = []
        for i in range(n):
            rows = ins[i].shape[0] // 2
            mine = ins[i].at[pl.ds(c * rows, rows), :]
            cp = pltpu.make_async_remote_copy(
                src_ref=mine, dst_ref=mine, send_sem=send_sems.at[i], recv_sem=recv_sems.at[i],
                device_id=(x, y, 1 - c), device_id_type=MESH)
            cp.start()
            copies.append(cp)
        for i, cp in enumerate(copies):
            rows = ins[i].shape[0] // 2
            theirs = ins[i].at[pl.ds((1 - c) * rows, rows), :]
            pltpu.make_async_remote_copy(
                src_ref=theirs, dst_ref=theirs, send_sem=send_sems.at[i],
                recv_sem=recv_sems.at[i], device_id=(x, y, 1 - c), device_id_type=MESH).wait_recv()
            cp.wait_send()

    return pl.pallas_call(
        body, name=name,
        in_specs=[_any_spec()] * n, out_specs=[_any_spec()] * n,
        out_shape=[jax.ShapeDtypeStruct(s.shape, s.dtype) for s in shards],
        input_output_aliases={i: i for i in range(n)},
        scratch_shapes=[pltpu.SemaphoreType.DMA((n,))] * 2,
    )(*shards)


def _small_copies(refs, send_sems, recv_sems):
    packed, slots = refs
    x, y, c, _ = _position()
    me = 4 * x + 2 * y + c
    copies = []
    for r in range(1, N_DEV):
        peer = (x ^ ((r >> 2) & 1), y ^ ((r >> 1) & 1), c ^ (r & 1))
        copies.append(pltpu.make_async_remote_copy(
            src_ref=packed, dst_ref=slots.at[me], send_sem=send_sems.at[r - 1],
            recv_sem=recv_sems.at[r - 1], device_id=peer, device_id_type=MESH))
    return copies


def _block_rows(rows, cols, itemsize=4, target=1 << 20):
    return _pick(max(BF16_ROWS, target // (cols * itemsize)), [rows], unit=BF16_ROWS)


def _pair_sum(name, place, grad, received):
    P, rows, cols = received.shape
    tr = _block_rows(rows, cols)
    nb = rows // tr

    def body(place_ref, g_ref, r_ref, o_ref):
        o_ref[...] = (g_ref[...].astype(F32) + r_ref[...].astype(F32)).astype(BF16)

    grid_spec = pltpu.PrefetchScalarGridSpec(
        num_scalar_prefetch=1, grid=(P, nb),
        in_specs=[pl.BlockSpec((None, tr, cols), lambda p, r, pr: (p, pr[1] * nb + r, 0)),
                  pl.BlockSpec((None, tr, cols), lambda p, r, pr: (p, r, 0))],
        out_specs=pl.BlockSpec((None, tr, cols), lambda p, r, pr: (p, r, 0)))
    return pl.pallas_call(
        body, name=name, grid_spec=grid_spec,
        out_shape=jax.ShapeDtypeStruct(received.shape, BF16),
        compiler_params=_params(2))(place, grad, received)


def _final_sum(name, place, grad, received, from_chips):
    _, rows, cols = received.shape
    tr = _block_rows(rows, cols)
    nb = rows // tr

    def body(place_ref, g_ref, r_ref, c_ref, o_ref):
        acc = g_ref[...].astype(F32) + r_ref[...].astype(F32)
        for j in range(3):
            acc = acc + c_ref[j].astype(F32)
        o_ref[...] = acc

    grid_spec = pltpu.PrefetchScalarGridSpec(
        num_scalar_prefetch=1, grid=(nb,),
        in_specs=[pl.BlockSpec((None, tr, cols), lambda r, pr: (pr[0], pr[1] * nb + r, 0)),
                  pl.BlockSpec((None, tr, cols), lambda r, pr: (pr[0], r, 0)),
                  pl.BlockSpec((3, tr, cols), lambda r, pr: (0, r, 0))],
        out_specs=pl.BlockSpec((tr, cols), lambda r, pr: (pr[1] * nb + r, 0)))
    return pl.pallas_call(
        body, name=name, grid_spec=grid_spec,
        out_shape=jax.ShapeDtypeStruct((2 * rows, cols), F32),
        compiler_params=_params(1))(place, grad, received, from_chips)


def _sum_devices(name, me, gathered, own):
    n_dev, rows, cols = gathered.shape
    tr = _pick(256, [rows])

    def body(me_ref, g_ref, own_ref, o_ref):
        term = lambda d: jnp.where(me_ref[0] == d, own_ref[...], g_ref[d])
        acc = term(0)
        for d in range(1, n_dev):
            acc = acc + term(d)
        o_ref[...] = acc

    grid_spec = pltpu.PrefetchScalarGridSpec(
        num_scalar_prefetch=1, grid=(rows // tr,),
        in_specs=[pl.BlockSpec((n_dev, tr, cols), lambda r, me_ref: (0, r, 0)),
                  pl.BlockSpec((tr, cols), lambda r, me_ref: (r, 0))],
        out_specs=pl.BlockSpec((tr, cols), lambda r, me_ref: (r, 0)))
    return pl.pallas_call(
        body, name=name, grid_spec=grid_spec,
        out_shape=jax.ShapeDtypeStruct((rows, cols), F32),
        compiler_params=_params(1))(me, gathered, own)


def _adamw(name, w, g, m, v):
    rows, cols = w.shape
    tr = _block_rows(rows, cols)
    c1 = 1.0 / (1.0 - ADAM_B1 ** ADAM_STEP)
    c2 = 1.0 / (1.0 - ADAM_B2 ** ADAM_STEP)

    def body(w_ref, g_ref, m_ref, v_ref, go_ref, d_ref, nm_ref, nv_ref):
        gv = g_ref[...]
        go_ref[...] = gv
        nm = ADAM_B1 * m_ref[...] + (1.0 - ADAM_B1) * gv
        nv = ADAM_B2 * v_ref[...] + (1.0 - ADAM_B2) * (gv * gv)
        nm_ref[...] = nm
        nv_ref[...] = nv
        d_ref[...] = -ADAM_LR * ((nm * c1) / (jnp.sqrt(nv * c2) + ADAM_EPS) + ADAM_WD * w_ref[...])

    blk = pl.BlockSpec((tr, cols), lambda r: (r, 0))
    shape = jax.ShapeDtypeStruct((rows, cols), F32)
    return pl.pallas_call(
        body, name=name, grid=(rows // tr,), in_specs=[blk] * 4, out_specs=[blk] * 4,
        out_shape=[shape] * 4, compiler_params=_params(1))(w, g, m, v)


BIG = ("ffn1_w_in", "ffn1_w_out", "w_mix_in", "w_mix_out", "w_cq", "w_ckv", "w_co",
       "ffn2_w_in", "ffn2_w_out")
BIG_KIND = {"ffn1_w_in": "c", "ffn1_w_out": "r", "w_mix_in": "c", "w_mix_out": "r", "w_cq": "r",
            "w_ckv": "c", "w_co": "r", "ffn2_w_in": "c", "ffn2_w_out": "r"}
GATHER_GROUPS = (("ffn1_in", ("ffn1_w_in",)), ("ffn1_out", ("ffn1_w_out",)),
                 ("mix", ("w_mix_in", "w_mix_out")), ("cross", ("w_cq", "w_ckv", "w_co")),
                 ("ffn2_in", ("ffn2_w_in",)), ("ffn2_out", ("ffn2_w_out",)))
GATHER_AFTER = (("ffn1_in", None), ("ffn1_out", "ffn1_in"), ("mix", "ffn1_out"), ("cross", "mix"),
                ("ffn2_in", "mix"), ("ffn2_out", "cross"))
SCATTER_ORDER = ("ffn2", "cross", "mix", "ffn1")
SMALL = ("ffn1_norm", "mix_norm", "ln_v_gain", "ln_v_bias", "spatial_w", "spatial_b", "gnorm_a",
         "gnorm_b", "cross_norm", "mem_norm", "ffn2_norm", "final_norm")
WEIGHTS = ("ffn1_norm", "ffn1_w_in", "ffn1_w_out", "mix_norm", "w_mix_in", "ln_v_gain",
           "ln_v_bias", "spatial_w", "spatial_b", "gnorm_a", "gnorm_b", "w_mix_out", "cross_norm",
           "mem_norm", "w_cq", "w_ckv", "w_co", "ffn2_norm", "ffn2_w_in", "ffn2_w_out",
           "final_norm")


def _pack(arrays):
    return jnp.concatenate([a.reshape(-1, LANE) for a in arrays], axis=0)


def _unpack(packed, like):
    out, row = [], 0
    for a in like:
        rows = a.size // LANE
        out.append(packed[row:row + rows].reshape(a.shape))
        row += rows
    return out


def _local_step(x, mem, target, small, weights_of, start_tokens, grads_ready, grads_flush):
    T, D = x.shape
    vec = lambda name: small[name].reshape(1, -1)
    w_a = small["ln_v_gain"].size
    w_b = small["gnorm_b"].size
    G = w_a // GROUP_DIM
    w_s = small["spatial_w"].reshape(G, SGU_BLOCK, SGU_BLOCK)
    b_t = small["spatial_b"].reshape(G, SGU_BLOCK).T

    h1, ffn1_saved = _ffn_forward("ffn1", x, vec("ffn1_norm"), weights_of, deps=start_tokens)
    n2 = _rmsnorm_fwd("mix_norm", h1, vec("mix_norm"))
    big = weights_of("mix", n2)
    (z,) = _matmul("mix_in", Mat(n2), big["w_mix_in"], "nn", [("c", 1, F32)], tm=2048, tn=256)
    z = z[0]
    y = _sgu_forward("sgu", z, vec("ln_v_gain"), vec("ln_v_bias"), w_s, b_t, vec("gnorm_a"), D)
    yb, sb_total = _sb_forward("stickbreak", z, w_a, w_b)
    y = _rmsnorm_fwd("gnorm_b", yb, vec("gnorm_b"), into=y, col=w_a // w_b)

    def add_res(acc, ex, out):
        out[0][...] = ex[0][...] + acc

    (h2,) = _matmul("mix_out", Mat(y), big["w_mix_out"], "nn", [("c", 1, F32)],
                    tm=1024, tn=1024, extras=[Mat(h1)], epi=add_res)
    h2 = h2[0]
    n3 = _rmsnorm_fwd("cross_norm", h2, vec("cross_norm"))
    memn = _rmsnorm_fwd("mem_norm", mem, vec("mem_norm"))
    big.update(weights_of("cross", n3))
    x_scale = (D // X_HEADS) ** -0.5

    def scaled(acc, ex, out):
        out[0][...] = (acc * x_scale).astype(BF16)

    (q,) = _matmul("cross_q", Mat(n3), big["w_cq"], "nn", [("c", 1, BF16)],
                   tm=1024, tn=1024, epi=scaled)
    (kv,) = _matmul("cross_kv", Mat(memn), big["w_ckv"], "nn", [("c", 1, BF16)], tm=256, tn=1024)
    q, kv = q[0], kv[0]
    o = _xattn_forward("cross_attn", q, kv)
    (h3,) = _matmul("cross_out", Mat(o), big["w_co"], "nn", [("c", 1, F32)],
                    tm=1024, tn=1024, extras=[Mat(h2)], epi=add_res)
    h3 = h3[0]
    h4, ffn2_saved = _ffn_forward("ffn2", h3, vec("ffn2_norm"), weights_of)

    gs = {}
    loss_tile, dh4, dh4_bf, gs["final_norm"] = _loss_head("loss_head", h4, vec("final_norm"), target)
    dh3, dh3_bf, gs["ffn2_norm"] = _ffn_backward(
        "ffn2", h3, vec("ffn2_norm"), ffn2_saved, dh4, dh4_bf, grads_ready, grads_flush)

    (do,) = _matmul("cross_do", Mat(dh3_bf), big["w_co"], "nt", [("c", 1, BF16)], tm=1024, tn=512)
    (dw_co,) = _matmul("cross_dwo", Mat(o), Mat(dh3_bf), "tn", [("r", N_CHIPS, BF16)],
                       tm=512, tn=1024)
    dq, dkv = _xattn_backward("cross_attn_bwd", q, kv, do[0])
    (dw_cq,) = _matmul("cross_dwq", Mat(n3), Mat(dq), "tn", [("r", N_CHIPS, BF16)],
                       tm=512, tn=1024)
    (dw_ckv,) = _matmul("cross_dwkv", Mat(memn), Mat(dkv), "tn", [("c", N_CHIPS, BF16)],
                        tm=1024, tn=1024)
    token = grads_ready("cross", {"w_cq": dw_cq, "w_ckv": dw_ckv, "w_co": dw_co})
    dq = _tie("cross_dq_after_swap", dq, [token])
    (dn3,) = _matmul("cross_dn", Mat(dq), big["w_cq"], "nt", [("c", 1, F32)], tm=1024, tn=512)
    (dmemn,) = _matmul("cross_dmem", Mat(dkv), big["w_ckv"], "nt", [("c", 1, F32)],
                       tm=256, tn=1024, tk=1024)
    (gs["mem_norm"],) = _rmsnorm_bwd("mem_dnorm", mem, vec("mem_norm"), dmemn[0], want_dx=False)
    dn3 = _tie("cross_dn_after_scatter", dn3, [grads_flush("cross", gs["mem_norm"])])
    dh2, dh2_bf, gs["cross_norm"] = _rmsnorm_bwd("cross_dnorm", h2, vec("cross_norm"), dn3[0],
                                                 dres=dh3)

    (dy,) = _matmul("mix_dy", Mat(dh2_bf), big["w_mix_out"], "nt", [("c", 1, F32)], tm=1024, tn=512)
    dy = dy[0]
    (dw_mix_out,) = _matmul("mix_dwout", Mat(y), Mat(dh2_bf), "tn", [("r", N_CHIPS, BF16)],
                            tm=512, tn=1024)
    dza, gs["ln_v_gain"], gs["ln_v_bias"], gs["spatial_w"], db, gs["gnorm_a"] = _sgu_backward(
        "sgu_bwd", z, dy, vec("ln_v_gain"), vec("ln_v_bias"), w_s, b_t, vec("gnorm_a"))
    gs["spatial_b"] = db.reshape(G, SGU_BLOCK)
    dob, gs["gnorm_b"] = _rmsnorm_bwd("gnorm_b_bwd", yb, vec("gnorm_b"), dy, dn_col=w_a // w_b,
                                      want_bf16=False)
    dqb, dkvb = _sb_backward("stickbreak_bwd", z, dob, sb_total, w_a, w_b)
    dz = jnp.concatenate([dza, dqb, dkvb[0].astype(BF16), dkvb[1].astype(BF16)], axis=1)
    (dw_mix_in,) = _matmul("mix_dwin", Mat(n2), Mat(dz), "tn", [("c", N_CHIPS, BF16)],
                           tm=1024, tn=1280)
    token = grads_ready("mix", {"w_mix_in": dw_mix_in, "w_mix_out": dw_mix_out})
    dz = _tie("mix_dz_after_swap", dz, [token])
    (dn2,) = _matmul("mix_dn", Mat(dz), big["w_mix_in"], "nt", [("c", 1, F32)],
                     tm=1024, tn=1024, tk=1280)
    dn2 = _tie("mix_dn_after_scatter", dn2, [grads_flush("mix", dn2)])
    dh1, dh1_bf, gs["mix_norm"] = _rmsnorm_bwd("mix_dnorm", h1, vec("mix_norm"), dn2[0], dres=dh2)

    dx, _, gs["ffn1_norm"] = _ffn_backward(
        "ffn1", x, vec("ffn1_norm"), ffn1_saved, dh1, dh1_bf, grads_ready, grads_flush)
    gs = {k: g.reshape(small[k].shape) for k, g in gs.items()}
    return loss_tile, dx, gs


def kernel(x, mem, ffn1_norm, ffn1_w_in, ffn1_w_out, mix_norm, w_mix_in, ln_v_gain, ln_v_bias, spatial_w, spatial_b, gnorm_a, gnorm_b, w_mix_out, cross_norm, mem_norm, w_cq, w_ckv, w_co, ffn2_norm, ffn2_w_in, ffn2_w_out, final_norm, loss_target, m_ffn1_norm, m_ffn1_w_in, m_ffn1_w_out, m_mix_norm, m_w_mix_in, m_ln_v_gain, m_ln_v_bias, m_spatial_w, m_spatial_b, m_gnorm_a, m_gnorm_b, m_w_mix_out, m_cross_norm, m_mem_norm, m_w_cq, m_w_ckv, m_w_co, m_ffn2_norm, m_ffn2_w_in, m_ffn2_w_out, m_final_norm, v_ffn1_norm, v_ffn1_w_in, v_ffn1_w_out, v_mix_norm, v_w_mix_in, v_ln_v_gain, v_ln_v_bias, v_spatial_w, v_spatial_b, v_gnorm_a, v_gnorm_b, v_w_mix_out, v_cross_norm, v_mem_norm, v_w_cq, v_w_ckv, v_w_co, v_ffn2_norm, v_ffn2_w_in, v_ffn2_w_out, v_final_norm):
    given = dict(locals())
    w = {k: given[k] for k in WEIGHTS}
    m = {k: given["m_" + k] for k in WEIGHTS}
    v = {k: given["v_" + k] for k in WEIGHTS}

    cx, cy, cc = lax.axis_index("x"), lax.axis_index("y"), lax.axis_index("c")
    place = jnp.stack([2 * cx + cy, cc]).astype(jnp.int32)

    names_of = dict(GATHER_GROUPS)
    own = {g: [_cast_own(f"cast_{k}", place, w[k][0]) for k in names] for g, names in GATHER_GROUPS}
    gathers = {}

    def start_gather(group, deps):
        send, recv, arrays, token = _split_start(f"gather_start_{group}", own[group],
                                                 _gather_copies, 3 * len(own[group]), deps)
        gathers[group] = (send, recv, arrays)
        return token

    start_tokens = [start_gather(g, ()) for g, after in GATHER_AFTER if after is None]
    start_tokens += [a for g, after in GATHER_AFTER if after is not None for a in own[g]]

    def weights_of(group, after):
        send, recv, arrays = gathers[group]
        arrays = _split_wait(f"gather_wait_{group}", arrays, send, recv, after, _gather_copies)
        tokens = [start_gather(g, (arrays[0],)) for g, a in GATHER_AFTER if a == group]
        arrays = _forward_to_sibling(f"gather_forward_{group}", list(arrays), tokens)
        return {k: Mat(a, BIG_KIND[k]) for k, a in zip(names_of[group], arrays)}

    swaps, scatters = {}, {}

    def grads_ready(group, partial):
        names = list(partial)
        grads_ = [partial[k] for k in names]
        lands = [lax.empty((g.shape[0], g.shape[1] // 2, g.shape[2]), g.dtype) for g in grads_]
        send, recv, arrays, token = _split_start(f"swap_start_{group}", grads_ + lands,
                                                 _swap_copies, len(names))
        swaps[group] = (names, send, recv, arrays)
        return token

    def grads_flush(group, after):
        names, send, recv, arrays = swaps[group]
        arrays = _split_wait(f"swap_wait_{group}", arrays, send, recv, after, _swap_copies)
        grads_, from_sibling = arrays[:len(names)], arrays[len(names):]
        sums = [_pair_sum(f"pair_sum_{k}", place, g, r)
                for k, g, r in zip(names, grads_, from_sibling)]
        lands = [lax.empty((3,) + s.shape[1:], s.dtype) for s in sums]
        send, recv, arrays, token = _split_start(f"scatter_start_{group}", sums + lands,
                                                 _scatter_copies, 3 * len(names))
        scatters[group] = (names, grads_, from_sibling, send, recv, arrays)
        return token

    small = {k: w[k] for k in SMALL}
    loss_tile, grad_x, gs = _local_step(x[0], mem[0], loss_target[0], small, weights_of,
                                        start_tokens, grads_ready, grads_flush)

    packed = _pack([gs[k] for k in SMALL] + [loss_tile])
    slots = jnp.zeros((N_DEV,) + packed.shape, packed.dtype)
    small_send, small_recv, small_arrays, _ = _split_start(
        "small_start", [packed, slots], _small_copies, N_DEV - 1)

    grad, delta, new_m, new_v = {}, {}, {}, {}
    after = grad_x
    for group in SCATTER_ORDER:
        names, grads_, from_sibling, send, recv, arrays = scatters[group]
        arrays = _split_wait(f"scatter_wait_{group}", arrays, send, recv, after, _scatter_copies)
        from_chips = arrays[len(names):]
        shards = [_final_sum(f"final_sum_{k}", place, g, r, f)
                  for k, g, r, f in zip(names, grads_, from_sibling, from_chips)]
        shards = _share_halves(f"share_{group}", shards)
        for k, g_ in zip(names, shards):
            g_, d_, m_, v_ = _adamw(f"adamw_{k}", w[k][0], g_, m[k][0], v[k][0])
            grad[k], delta[k], new_m[k], new_v[k] = g_[None], d_[None], m_[None], v_[None]
            after = v_

    packed, slots = _split_wait("small_wait", small_arrays, small_send, small_recv, after,
                                _small_copies)
    me = (4 * cx + 2 * cy + cc).astype(jnp.int32).reshape(1)
    total = _sum_devices("sum_small", me, slots, packed)
    n_small = total.shape[0] - SUBLANE
    loss = total[n_small, 0]
    small_g = total[:n_small]
    g_s, d_s, m_s, v_s = _adamw("adamw_small", _pack([w[k] for k in SMALL]), small_g,
                                _pack([m[k] for k in SMALL]), _pack([v[k] for k in SMALL]))
    like = [w[k] for k in SMALL]
    for k, g_, d_, m_, v_ in zip(SMALL, _unpack(g_s, like), _unpack(d_s, like),
                                 _unpack(m_s, like), _unpack(v_s, like)):
        grad[k], delta[k], new_m[k], new_v[k] = g_, d_, m_, v_

    return (loss, grad_x[None], *[grad[k] for k in WEIGHTS], *[delta[k] for k in WEIGHTS],
            *[new_m[k] for k in WEIGHTS], *[new_v[k] for k in WEIGHTS])
```

```python
import functools
import math

import jax
import jax.numpy as jnp
from jax import lax
from jax.experimental import pallas as pl
from jax.experimental.pallas import tpu as pltpu

F32 = jnp.float32
BF16 = jnp.bfloat16
MESH = pl.DeviceIdType.MESH

EPS = 1e-6
CHUNK = 64
SGU_BLOCK = 128
GROUP_DIM = 128
X_HEADS = 4
N_CHIPS = 4
N_DEV = 8
LANE = 128
SUBLANE = 8
BF16_ROWS = 16

ADAM_LR = 0.001
ADAM_B1 = 0.9
ADAM_B2 = 0.999
ADAM_EPS = 1e-08
ADAM_WD = 0.01
ADAM_STEP = 10

V7X_VMEM_BYTES = 64 << 20
VMEM_LIMIT = V7X_VMEM_BYTES - (8 << 20)


def _params(n_grid):
    return pltpu.CompilerParams(dimension_semantics=("arbitrary",) * n_grid,
                                vmem_limit_bytes=VMEM_LIMIT)


def _pick(pref, dims, unit=None):
    g = functools.reduce(math.gcd, dims)
    if unit is None:
        unit = LANE if g % LANE == 0 else SUBLANE
    cands = [d for d in range(unit, g + 1, unit) if g % d == 0] or [g]
    return min(cands, key=lambda d: abs(math.log(d / pref)))


def _any_spec():
    return pl.BlockSpec(memory_space=pl.ANY)


class Mat:
    def __init__(self, arr, kind="c"):
        if arr.ndim == 2:
            arr = arr[None]
        self.arr, self.kind = arr, kind
        self.P, self.prow, self.pcol = arr.shape
        self.rows = self.prow * (self.P if kind == "r" else 1)
        self.cols = self.pcol * (self.P if kind == "c" else 1)
        self.dtype = arr.dtype

    def spec(self, tr, tc, rc_fn):
        if self.kind == "c":
            per = self.pcol // tc
            assert per * tc == self.pcol, (self.pcol, tc)

            def imap(*g):
                i, j = rc_fn(*g)
                return (j // per, i, j % per)
        else:
            per = self.prow // tr
            assert per * tr == self.prow, (self.prow, tr)

            def imap(*g):
                i, j = rc_fn(*g)
                return (i // per, i % per, j)
        return pl.BlockSpec((None, tr, tc), imap)

    def two_d(self):
        assert self.P == 1
        return self.arr[0]


def _out_mat(kind, P, rows, cols, dtype):
    shape = (P, rows, cols // P) if kind == "c" else (P, rows // P, cols)
    return jax.ShapeDtypeStruct(shape, dtype)


def _matmul(name, A, B, mode, outs, *, tm=1024, tn=1024, tk=2048, extras=(), epi=None):
    if mode == "nn":
        M, K, N = A.rows, A.cols, B.cols
        assert B.rows == K
    elif mode == "nt":
        M, K, N = A.rows, A.cols, B.rows
        assert B.cols == K
    else:
        K, M, N = A.rows, A.cols, B.cols
        assert B.rows == K
    mdims, ndims, kdims = [M], [N], [K]
    whole_b = mode == "nn" and B.kind == "r" and B.P > 1 and K <= tk
    if whole_b:
        kdims.append(A.pcol)
        ndims.append(B.pcol)
    elif mode == "tn":
        assert A.kind == "c" and B.kind == "c"
        mdims.append(A.pcol)
        ndims.append(B.pcol)
    else:
        (mdims if A.kind == "r" else kdims).append(A.prow if A.kind == "r" else A.pcol)
        if mode == "nn":
            (kdims if B.kind == "r" else ndims).append(B.prow if B.kind == "r" else B.pcol)
        else:
            (ndims if B.kind == "r" else kdims).append(B.prow if B.kind == "r" else B.pcol)
    for o in list(outs) + list(extras):
        if isinstance(o, Mat):
            (mdims if o.kind == "r" else ndims).append(o.prow if o.kind == "r" else o.pcol)
        elif isinstance(o[0], str):
            (mdims if o[0] == "r" else ndims).append((M if o[0] == "r" else N) // o[1])
    tm, tn = _pick(tm, mdims), _pick(tn, ndims)
    tk = K if mode == "tn" else _pick(tk, kdims)
    nk = K // tk
    grid = (M // tm, N // tn, nk)

    if mode == "tn":
        a_spec = A.spec(K, tm, lambda m, n, k: (0, m))
        b_spec = B.spec(K, tn, lambda m, n, k: (0, n))
    else:
        a_spec = A.spec(tm, tk, lambda m, n, k: (m, k))
        if whole_b:
            b_spec = pl.BlockSpec((B.P, B.prow, tn), lambda m, n, k: (0, 0, n))
        elif mode == "nn":
            b_spec = B.spec(tk, tn, lambda m, n, k: (k, n))
        else:
            b_spec = B.spec(tn, tk, lambda m, n, k: (n, k))

    def mn_spec(o):
        if isinstance(o, Mat):
            return o.spec(tm, tn, lambda m, n, k: (m, n))
        if isinstance(o[0], str):
            kind, P = o[0], o[1]
            fake = Mat.__new__(Mat)
            fake.kind, fake.P = kind, P
            fake.prow = M // P if kind == "r" else M
            fake.pcol = N // P if kind == "c" else N
            return Mat.spec(fake, tm, tn, lambda m, n, k: (m, n))
        return o[1](tm, tn)

    out_shapes = tuple(_out_mat(o[0], o[1], M, N, o[2]) if isinstance(o[0], str) else o[0]
                       for o in outs)
    out_specs = tuple(mn_spec(o) for o in outs)
    extra_arrays = tuple(e.arr if isinstance(e, Mat) else e[0] for e in extras)
    extra_specs = tuple(mn_spec(e) for e in extras)
    n_ex, n_out = len(extras), len(outs)
    tt = _pick(256, [tm])
    dims = (((1,), (1 if mode == "nt" else 0,)), ((), ()))

    def body(*refs):
        a_ref, b_ref = refs[:2]
        ex_refs = refs[2:2 + n_ex]
        out_refs = refs[2 + n_ex:2 + n_ex + n_out]
        scratch = refs[2 + n_ex + n_out:]
        if mode == "tn":
            at_ref = scratch[0]

            @pl.when(pl.program_id(1) == 0)
            def _():
                for c0 in range(0, tm, tt):
                    at_ref[c0:c0 + tt, :] = a_ref[:, c0:c0 + tt].astype(F32).T.astype(BF16)

            lhs = at_ref[...]
        else:
            lhs = a_ref[...].astype(BF16)
        rhs = b_ref[...].reshape(K, tn) if whole_b else b_ref[...]
        part = lax.dot_general(lhs, rhs.astype(BF16), dims, preferred_element_type=F32)

        def finish(acc):
            if epi is None:
                out_refs[0][...] = acc.astype(out_refs[0].dtype)
            else:
                epi(acc, ex_refs, out_refs)

        if nk == 1:
            finish(part)
        else:
            acc_ref = scratch[0]
            k = pl.program_id(2)

            @pl.when(k == 0)
            def _():
                acc_ref[...] = part

            @pl.when(k > 0)
            def _():
                acc_ref[...] += part

            @pl.when(k == nk - 1)
            def _():
                finish(acc_ref[...])

    scratch_shapes = []
    if mode == "tn":
        scratch_shapes.append(pltpu.VMEM((tm, K), BF16))
    elif nk > 1:
        scratch_shapes.append(pltpu.VMEM((tm, tn), F32))
    res = pl.pallas_call(
        body, name=name, grid=grid,
        in_specs=[a_spec, b_spec, *extra_specs], out_specs=out_specs, out_shape=out_shapes,
        scratch_shapes=scratch_shapes, compiler_params=_params(3),
    )(A.arr, B.arr, *extra_arrays)
    return res


def _row_tile(T):
    return _pick(256, [T])


def _tie(name, x, deps):
    def body(*refs):
        refs[-1][...] = jnp.zeros_like(refs[-1])

    return pl.pallas_call(
        body, name=name, in_specs=[_any_spec()] * (1 + len(deps)),
        out_specs=(_any_spec(), pl.BlockSpec(memory_space=pltpu.VMEM)),
        out_shape=(jax.ShapeDtypeStruct(x.shape, x.dtype),
                   jax.ShapeDtypeStruct((SUBLANE, LANE), F32)),
        input_output_aliases={0: 0},
    )(x, *deps)[0]


def _rmsnorm_fwd(name, x, g, *, into=None, col=0, deps=()):
    T, W = x.shape
    tr = _row_tile(T)

    def body(x_ref, g_ref, *rest):
        o_ref = rest[-1]
        xv = x_ref[...]
        rstd = lax.rsqrt(jnp.mean(xv * xv, axis=-1, keepdims=True) + EPS)
        o_ref[...] = (xv * rstd * g_ref[...]).astype(o_ref.dtype)

    in_specs = [pl.BlockSpec((tr, W), lambda i: (i, 0)), pl.BlockSpec((1, W), lambda i: (0, 0))]
    args = [x, g]
    kwargs = {}
    if into is None:
        out_shape = jax.ShapeDtypeStruct((T, W), BF16)
    else:
        out_shape = jax.ShapeDtypeStruct(into.shape, into.dtype)
        in_specs.append(_any_spec())
        args.append(into)
        kwargs["input_output_aliases"] = {2: 0}
    in_specs += [_any_spec()] * len(deps)
    args += list(deps)
    return pl.pallas_call(
        body, name=name, grid=(T // tr,), in_specs=in_specs,
        out_specs=pl.BlockSpec((tr, W), lambda i: (i, col)), out_shape=out_shape,
        compiler_params=_params(1), **kwargs)(*args)


def _rmsnorm_bwd(name, x, g, dn, *, dn_col=0, dres=None, want_dx=True, want_bf16=True):
    T, W = x.shape
    tr = _row_tile(T)
    has_res = dres is not None

    def body(*refs):
        x_ref, g_ref, dn_ref = refs[:3]
        pos = 3
        dres_ref = None
        if has_res:
            dres_ref = refs[pos]
            pos += 1
        outs = refs[pos:]
        dg_ref = outs[-1]
        xv = x_ref[...]
        rstd = lax.rsqrt(jnp.mean(xv * xv, axis=-1, keepdims=True) + EPS)
        xhat = xv * rstd
        dnv = dn_ref[...].astype(F32)

        @pl.when(pl.program_id(0) == 0)
        def _():
            dg_ref[...] = jnp.zeros_like(dg_ref)

        dg_ref[...] += jnp.sum(dnv * xhat, axis=0, keepdims=True)
        if want_dx:
            t = dnv * g_ref[...]
            dx = rstd * (t - xhat * jnp.mean(t * xhat, axis=-1, keepdims=True))
            if has_res:
                dx = dx + dres_ref[...]
            outs[0][...] = dx
            if want_bf16:
                outs[1][...] = dx.astype(BF16)

    row = pl.BlockSpec((tr, W), lambda i: (i, 0))
    in_specs = [row, pl.BlockSpec((1, W), lambda i: (0, 0)),
                pl.BlockSpec((tr, W), lambda i: (i, dn_col))]
    args = [x, g, dn]
    if has_res:
        in_specs.append(row)
        args.append(dres)
    out_shape, out_specs = [], []
    if want_dx:
        out_shape.append(jax.ShapeDtypeStruct((T, W), F32))
        out_specs.append(row)
        if want_bf16:
            out_shape.append(jax.ShapeDtypeStruct((T, W), BF16))
            out_specs.append(row)
    out_shape.append(jax.ShapeDtypeStruct((1, W), F32))
    out_specs.append(pl.BlockSpec((1, W), lambda i: (0, 0)))
    return pl.pallas_call(
        body, name=name, grid=(T // tr,), in_specs=in_specs, out_specs=out_specs,
        out_shape=out_shape, compiler_params=_params(1))(*args)


def _loss_head(name, h, g, target):
    T, W = h.shape
    tr = _row_tile(T)

    def body(h_ref, g_ref, t_ref, loss_ref, dx_ref, dxb_ref, dg_ref):
        xv = h_ref[...]
        gv = g_ref[...]
        rstd = lax.rsqrt(jnp.mean(xv * xv, axis=-1, keepdims=True) + EPS)
        xhat = xv * rstd
        diff = xhat * gv - t_ref[...]

        @pl.when(pl.program_id(0) == 0)
        def _():
            dg_ref[...] = jnp.zeros_like(dg_ref)
            loss_ref[...] = jnp.zeros_like(loss_ref)

        loss_ref[...] += 0.5 * jnp.sum(jnp.mean(diff * diff, axis=-1, keepdims=True))
        dnv = diff * (1.0 / W)
        dg_ref[...] += jnp.sum(dnv * xhat, axis=0, keepdims=True)
        t = dnv * gv
        dx = rstd * (t - xhat * jnp.mean(t * xhat, axis=-1, keepdims=True))
        dx_ref[...] = dx
        dxb_ref[...] = dx.astype(BF16)

    row = pl.BlockSpec((tr, W), lambda i: (i, 0))
    vec = pl.BlockSpec((1, W), lambda i: (0, 0))
    return pl.pallas_call(
        body, name=name, grid=(T // tr,), in_specs=[row, vec, row],
        out_specs=[pl.BlockSpec((SUBLANE, LANE), lambda i: (0, 0)), row, row, vec],
        out_shape=[jax.ShapeDtypeStruct((SUBLANE, LANE), F32), jax.ShapeDtypeStruct((T, W), F32),
                   jax.ShapeDtypeStruct((T, W), BF16), jax.ShapeDtypeStruct((1, W), F32)],
        compiler_params=_params(1))(h, g, target)


def _sigmoid(x):
    return 1.0 / (1.0 + jnp.exp(-x))


def _ffn_in(name, n, W):
    T, D = n.shape
    F = W.cols // 2
    tm = _pick(2048, [T])
    tn = _pick(512, [W.pcol])
    per = W.pcol // tn

    def body(a_ref, wg_ref, wu_ref, gu_ref, act_ref):
        a = a_ref[...]
        gate = jnp.dot(a, wg_ref[...], preferred_element_type=F32)
        up = jnp.dot(a, wu_ref[...], preferred_element_type=F32)
        gu_ref[0] = gate.astype(BF16)
        gu_ref[1] = up.astype(BF16)
        act_ref[...] = (gate * _sigmoid(gate) * up).astype(BF16)

    return pl.pallas_call(
        body, name=name, grid=(T // tm, F // tn),
        in_specs=[pl.BlockSpec((tm, D), lambda m, j: (m, 0)),
                  pl.BlockSpec((None, D, tn), lambda m, j: (j // per, 0, j % per)),
                  pl.BlockSpec((None, D, tn), lambda m, j: (2 + j // per, 0, j % per))],
        out_specs=[pl.BlockSpec((2, tm, tn), lambda m, j: (0, m, j)),
                   pl.BlockSpec((tm, tn), lambda m, j: (m, j))],
        out_shape=[jax.ShapeDtypeStruct((2, T, F), BF16), jax.ShapeDtypeStruct((T, F), BF16)],
        compiler_params=_params(2))(n, W.arr, W.arr)


def _ffn_forward(tag, h, norm_g, weights_of, deps=()):
    n = _rmsnorm_fwd(f"{tag}_norm", h, norm_g, deps=deps)
    w_in = weights_of(f"{tag}_in", n)[f"{tag}_w_in"]
    gu, act = _ffn_in(f"{tag}_in", n, w_in)
    w_out = weights_of(f"{tag}_out", act)[f"{tag}_w_out"]

    def epi(acc, ex, out):
        out[0][...] = ex[0][...] + 0.5 * acc

    (h_out,) = _matmul(f"{tag}_out", Mat(act), w_out, "nn", [("c", 1, F32)],
                       tm=1024, tn=1024, tk=1408, extras=[Mat(h)], epi=epi)
    return h_out[0], (n, gu, act, w_in, w_out)


def _ffn_backward(tag, h_in, norm_g, saved, dh, dh_bf, grads_ready, grads_flush):
    n, gu, act, w_in, w_out = saved
    T, F = act.shape

    def epi(acc, ex, out):
        dact = 0.5 * acc
        gate = ex[0][0].astype(F32)
        up = ex[0][1].astype(F32)
        sig = _sigmoid(gate)
        out[0][0] = (dact * up * sig * (1.0 + gate * (1.0 - sig))).astype(BF16)
        out[0][1] = (dact * gate * sig).astype(BF16)

    def pair_spec(tm, tn):
        return pl.BlockSpec((2, tm, tn), lambda m, j, k: (0, m, j))

    (dgu,) = _matmul(f"{tag}_dact", Mat(dh_bf), w_out, "nt",
                     [(jax.ShapeDtypeStruct((2, T, F), BF16), pair_spec)],
                     tm=512, tn=1408, extras=[(gu, pair_spec)], epi=epi)

    def half(acc, ex, out):
        out[0][...] = (0.5 * acc).astype(out[0].dtype)

    (dw_out,) = _matmul(f"{tag}_dwout", Mat(act), Mat(dh_bf), "tn", [("r", N_CHIPS, BF16)],
                        tm=1408, tn=512, epi=half)
    (dw_in,) = _matmul(f"{tag}_dwin", Mat(n), Mat(dgu), "tn", [("c", N_CHIPS, BF16)],
                       tm=512, tn=1408)
    token = grads_ready(tag, {f"{tag}_w_in": dw_in, f"{tag}_w_out": dw_out})
    dgu = _tie(f"{tag}_dgu_after_swap", dgu, [token])
    (dn,) = _matmul(f"{tag}_dn", Mat(dgu), w_in, "nt", [("c", 1, F32)],
                    tm=1024, tn=1024, tk=2816)
    dn = _tie(f"{tag}_dn_after_scatter", dn, [grads_flush(tag, dn)])
    return _rmsnorm_bwd(f"{tag}_dnorm", h_in, norm_g, dn[0], dres=dh)


_GELU_C = math.sqrt(2.0 / math.pi)
_GELU_A = 0.044715


def _gelu(x):
    return 0.5 * x * (1.0 + jnp.tanh(_GELU_C * (x + _GELU_A * x * x * x)))


def _gelu_grad(x):
    th = jnp.tanh(_GELU_C * (x + _GELU_A * x * x * x))
    return 0.5 * (1.0 + th) + 0.5 * x * (1.0 - th * th) * _GELU_C * (1.0 + 3.0 * _GELU_A * x * x)


def _chunk_mask():
    t = lax.broadcasted_iota(jnp.int32, (SGU_BLOCK, SGU_BLOCK), 0) // CHUNK
    s = lax.broadcasted_iota(jnp.int32, (SGU_BLOCK, SGU_BLOCK), 1) // CHUNK
    return s <= t


def _sgu_group_forward(v_g, lg, lb, wm_bf, b_col):
    mu = jnp.mean(v_g, axis=-1, keepdims=True)
    xc = v_g - mu
    rstd = lax.rsqrt(jnp.mean(xc * xc, axis=-1, keepdims=True) + EPS)
    vhat = xc * rstd
    vn = vhat * lg + lb
    mixed = jnp.dot(wm_bf, vn.astype(BF16), preferred_element_type=F32) + b_col
    return vhat, rstd, vn, mixed


def _sgu_forward(name, z, ln_g, ln_b, w_s, b_t, gn, d_model):
    T = z.shape[0]
    W_A = ln_g.shape[1]
    G = W_A // GROUP_DIM

    def body(z_ref, lg_ref, lb_ref, w_ref, bt_ref, gn_ref, y_ref):
        mask = _chunk_mask()
        u = _gelu(z_ref[:, :W_A])
        v = _gelu(z_ref[:, W_A:])
        cols = []
        for g in range(G):
            sl = slice(g * GROUP_DIM, (g + 1) * GROUP_DIM)
            wm = jnp.where(mask, w_ref[g], 0.0).astype(BF16)
            _, _, _, mixed = _sgu_group_forward(v[:, sl], lg_ref[:, sl], lb_ref[:, sl], wm,
                                                bt_ref[:, g:g + 1])
            cols.append(u[:, sl] * mixed)
        ya = jnp.concatenate(cols, axis=1)
        rstd = lax.rsqrt(jnp.mean(ya * ya, axis=-1, keepdims=True) + EPS)
        y_ref[...] = (ya * rstd * gn_ref[...]).astype(BF16)

    vec = pl.BlockSpec((1, W_A), lambda i: (0, 0))
    return pl.pallas_call(
        body, name=name, grid=(T // SGU_BLOCK,),
        in_specs=[pl.BlockSpec((SGU_BLOCK, 2 * W_A), lambda i: (i, 0)), vec, vec,
                  pl.BlockSpec((G, SGU_BLOCK, SGU_BLOCK), lambda i: (0, 0, 0)),
                  pl.BlockSpec((SGU_BLOCK, G), lambda i: (0, 0)), vec],
        out_specs=pl.BlockSpec((SGU_BLOCK, W_A), lambda i: (i, 0)),
        out_shape=jax.ShapeDtypeStruct((T, d_model), BF16),
        compiler_params=_params(1))(z, ln_g, ln_b, w_s, b_t, gn)


def _sgu_backward(name, z, dy, ln_g, ln_b, w_s, b_t, gn):
    T = z.shape[0]
    W_A = ln_g.shape[1]
    G = W_A // GROUP_DIM

    def body(z_ref, dy_ref, lg_ref, lb_ref, w_ref, bt_ref, gn_ref,
             dz_ref, dlg_ref, dlb_ref, dw_ref, db_ref, dgn_ref):
        @pl.when(pl.program_id(0) == 0)
        def _():
            for r in (dlg_ref, dlb_ref, dw_ref, db_ref, dgn_ref):
                r[...] = jnp.zeros_like(r)

        mask = _chunk_mask()
        zu = z_ref[:, :W_A]
        zv = z_ref[:, W_A:]
        u = _gelu(zu)
        v = _gelu(zv)
        saved, cols = [], []
        for g in range(G):
            sl = slice(g * GROUP_DIM, (g + 1) * GROUP_DIM)
            wm = jnp.where(mask, w_ref[g], 0.0)
            vhat, rstd, vn, mixed = _sgu_group_forward(
                v[:, sl], lg_ref[:, sl], lb_ref[:, sl], wm.astype(BF16), bt_ref[:, g:g + 1])
            saved.append((wm, vhat, rstd, vn, mixed))
            cols.append(u[:, sl] * mixed)
        ya = jnp.concatenate(cols, axis=1)
        rstd_a = lax.rsqrt(jnp.mean(ya * ya, axis=-1, keepdims=True) + EPS)
        ya_hat = ya * rstd_a
        dyv = dy_ref[...].astype(F32)
        dgn_ref[...] += jnp.sum(dyv * ya_hat, axis=0, keepdims=True)
        t = dyv * gn_ref[...]
        dya = rstd_a * (t - ya_hat * jnp.mean(t * ya_hat, axis=-1, keepdims=True))
        du_cols, dv_cols, dlg_cols, dlb_cols = [], [], [], []
        for g in range(G):
            sl = slice(g * GROUP_DIM, (g + 1) * GROUP_DIM)
            wm, vhat, rstd, vn, mixed = saved[g]
            dya_g = dya[:, sl]
            du_cols.append(dya_g * mixed)
            dmix = dya_g * u[:, sl]
            dmix_bf = dmix.astype(BF16)
            db_ref[g] += jnp.sum(dmix, axis=1, keepdims=True)
            dw = lax.dot_general(dmix_bf, vn.astype(BF16), (((1,), (1,)), ((), ())),
                                 preferred_element_type=F32)
            dw_ref[g] += jnp.where(mask, dw, 0.0)
            dvn = jnp.dot(wm.T.astype(BF16), dmix_bf, preferred_element_type=F32)
            dlg_cols.append(jnp.sum(dvn * vhat, axis=0, keepdims=True))
            dlb_cols.append(jnp.sum(dvn, axis=0, keepdims=True))
            dvhat = dvn * lg_ref[:, sl]
            dv_cols.append(rstd * (dvhat - jnp.mean(dvhat, axis=-1, keepdims=True)
                                   - vhat * jnp.mean(dvhat * vhat, axis=-1, keepdims=True)))
        dlg_ref[...] += jnp.concatenate(dlg_cols, axis=1)
        dlb_ref[...] += jnp.concatenate(dlb_cols, axis=1)
        dz_ref[:, :W_A] = (jnp.concatenate(du_cols, axis=1) * _gelu_grad(zu)).astype(BF16)
        dz_ref[:, W_A:] = (jnp.concatenate(dv_cols, axis=1) * _gelu_grad(zv)).astype(BF16)

    vec = pl.BlockSpec((1, W_A), lambda i: (0, 0))
    wspec = pl.BlockSpec((G, SGU_BLOCK, SGU_BLOCK), lambda i: (0, 0, 0))
    return pl.pallas_call(
        body, name=name, grid=(T // SGU_BLOCK,),
        in_specs=[pl.BlockSpec((SGU_BLOCK, 2 * W_A), lambda i: (i, 0)),
                  pl.BlockSpec((SGU_BLOCK, W_A), lambda i: (i, 0)), vec, vec, wspec,
                  pl.BlockSpec((SGU_BLOCK, G), lambda i: (0, 0)), vec],
        out_specs=[pl.BlockSpec((SGU_BLOCK, 2 * W_A), lambda i: (i, 0)), vec, vec, wspec,
                   pl.BlockSpec((G, SGU_BLOCK, 1), lambda i: (0, 0, 0)), vec],
        out_shape=[jax.ShapeDtypeStruct((T, 2 * W_A), BF16), jax.ShapeDtypeStruct((1, W_A), F32),
                   jax.ShapeDtypeStruct((1, W_A), F32),
                   jax.ShapeDtypeStruct((G, SGU_BLOCK, SGU_BLOCK), F32),
                   jax.ShapeDtypeStruct((G, SGU_BLOCK, 1), F32),
                   jax.ShapeDtypeStruct((1, W_A), F32)],
        compiler_params=_params(1))(z, dy, ln_g, ln_b, w_s, b_t, gn)


def _split_dot(x, tri):
    hi = x.astype(BF16)
    lo = (x - hi.astype(F32)).astype(BF16)
    return (jnp.dot(hi, tri, preferred_element_type=F32)
            + jnp.dot(lo, tri, preferred_element_type=F32))


def _tri(n, rel):
    r = lax.broadcasted_iota(jnp.int32, (n, n), 0)
    c = lax.broadcasted_iota(jnp.int32, (n, n), 1)
    return rel(r, c).astype(BF16)


def _dot_nt(a, b):
    return lax.dot_general(a, b, (((1,), (1,)), ((), ())), preferred_element_type=F32)


def _dot_tn(a, b):
    return lax.dot_general(a, b, (((0,), (0,)), ((), ())), preferred_element_type=F32)


def _sb_scores(qs, kj, mask):
    zz = _dot_nt(qs, kj)
    log_beta = jnp.minimum(zz, 0.0) - jnp.log(1.0 + jnp.exp(-jnp.abs(zz)))
    log_1m = log_beta - zz
    if mask is not None:
        log_1m = jnp.where(mask, log_1m, 0.0)
    return log_beta, log_1m


def _masked(mask, x):
    return x if mask is None else jnp.where(mask, x, 0.0)


def _sb_tiles(T):
    tk = _pick(256, [T])
    tq = 2 * tk if T % (2 * tk) == 0 else tk
    return tq, tk


def _sb_cols(w_a, w_b):
    base = 2 * w_a // GROUP_DIM
    per = w_b // GROUP_DIM
    return base, base + per, base + 2 * per


def _sb_forward(name, z, w_a, w_b):
    T = z.shape[0]
    H = w_b // GROUP_DIM
    tq, tk = _sb_tiles(T)
    per = tq // tk
    qc, kc, vc = _sb_cols(w_a, w_b)
    scale = GROUP_DIM ** -0.5

    def body(q_ref, k_ref, v_ref, y_ref, tot_ref):
        i = pl.program_id(1)
        qs = (q_ref[...] * scale).astype(BF16)
        upper = _tri(tk, lambda r, c: r > c)
        ahead = (lax.broadcasted_iota(jnp.int32, (tq, tk), 1)
                 - lax.broadcasted_iota(jnp.int32, (tq, tk), 0))

        def step(j, carry, masked):
            acc, later = carry
            k0 = pl.multiple_of(j * tk, tk)
            kj = k_ref[pl.ds(k0, tk), :].astype(BF16)
            vj = v_ref[pl.ds(k0, tk), :].astype(BF16)
            mask = ahead < i * tq - k0 if masked else None
            log_beta, log_1m = _sb_scores(qs, kj, mask)
            rest = _split_dot(log_1m, upper) + later
            a = _masked(mask, jnp.exp(log_beta + rest))
            acc = acc + jnp.dot(a.astype(BF16), vj, preferred_element_type=F32)
            return acc, later + jnp.sum(log_1m, axis=1, keepdims=True)

        carry = (jnp.zeros((tq, GROUP_DIM), F32), jnp.zeros((tq, 1), F32))
        first = i * per
        for d in reversed(range(per)):
            carry = step(first + d, carry, True)
        acc, total = lax.fori_loop(0, first, lambda jj, c: step(first - 1 - jj, c, False), carry)
        y_ref[...] = acc
        tot_ref[...] = total

    return pl.pallas_call(
        body, name=name, grid=(H, T // tq),
        in_specs=[pl.BlockSpec((tq, GROUP_DIM), lambda h, i: (i, qc + h)),
                  pl.BlockSpec((T, GROUP_DIM), lambda h, i: (0, kc + h)),
                  pl.BlockSpec((T, GROUP_DIM), lambda h, i: (0, vc + h))],
        out_specs=[pl.BlockSpec((tq, GROUP_DIM), lambda h, i: (i, h)),
                   pl.BlockSpec((None, tq, 1), lambda h, i: (h, i, 0))],
        out_shape=[jax.ShapeDtypeStruct((T, w_b), F32), jax.ShapeDtypeStruct((H, T, 1), F32)],
        compiler_params=_params(2))(z, z, z)


def _sb_backward(name, z, do, total, w_a, w_b):
    T = z.shape[0]
    H = w_b // GROUP_DIM
    tq, tk = _sb_tiles(T)
    per = tq // tk
    qc, kc, vc = _sb_cols(w_a, w_b)
    scale = GROUP_DIM ** -0.5

    def body(q_ref, k_ref, v_ref, do_ref, tot_ref, dq_ref, dkv_ref):
        i = pl.program_id(1)

        @pl.when(i == 0)
        def _():
            dkv_ref[...] = jnp.zeros_like(dkv_ref)

        qs = (q_ref[...] * scale).astype(BF16)
        dob = do_ref[...].astype(BF16)
        upto = _tri(tk, lambda r, c: r <= c)
        before = _tri(tk, lambda r, c: r < c)
        ahead = (lax.broadcasted_iota(jnp.int32, (tq, tk), 1)
                 - lax.broadcasted_iota(jnp.int32, (tq, tk), 0))

        def step(j, carry, masked):
            dq, left, e_seen = carry
            k0 = pl.multiple_of(j * tk, tk)
            kj = k_ref[pl.ds(k0, tk), :].astype(BF16)
            vj = v_ref[pl.ds(k0, tk), :].astype(BF16)
            mask = ahead < i * tq - k0 if masked else None
            log_beta, log_1m = _sb_scores(qs, kj, mask)
            rest = left - _split_dot(log_1m, upto)
            a = _masked(mask, jnp.exp(log_beta + rest))
            e = a * _dot_nt(dob, vj)
            e_before = e_seen + jnp.dot(e.astype(BF16), before, preferred_element_type=F32)
            beta = jnp.exp(log_beta)
            dz = _masked(mask, e * (1.0 - beta) - beta * e_before).astype(BF16)
            dq = dq + jnp.dot(dz, kj, preferred_element_type=F32)
            dkv_ref[0, pl.ds(k0, tk), :] += _dot_tn(dz, qs)
            dkv_ref[1, pl.ds(k0, tk), :] += _dot_tn(a.astype(BF16), dob)
            return (dq, left - jnp.sum(log_1m, axis=1, keepdims=True),
                    e_seen + jnp.sum(e, axis=1, keepdims=True))

        first = i * per
        carry = (jnp.zeros((tq, GROUP_DIM), F32), tot_ref[...], jnp.zeros((tq, 1), F32))
        carry = lax.fori_loop(0, first, lambda j, c: step(j, c, False), carry)
        for d in range(per):
            carry = step(first + d, carry, True)
        dq_ref[...] = (carry[0] * scale).astype(BF16)

    return pl.pallas_call(
        body, name=name, grid=(H, T // tq),
        in_specs=[pl.BlockSpec((tq, GROUP_DIM), lambda h, i: (i, qc + h)),
                  pl.BlockSpec((T, GROUP_DIM), lambda h, i: (0, kc + h)),
                  pl.BlockSpec((T, GROUP_DIM), lambda h, i: (0, vc + h)),
                  pl.BlockSpec((tq, GROUP_DIM), lambda h, i: (i, h)),
                  pl.BlockSpec((None, tq, 1), lambda h, i: (h, i, 0))],
        out_specs=[pl.BlockSpec((tq, GROUP_DIM), lambda h, i: (i, h)),
                   pl.BlockSpec((2, T, GROUP_DIM), lambda h, i: (0, 0, h))],
        out_shape=[jax.ShapeDtypeStruct((T, w_b), BF16), jax.ShapeDtypeStruct((2, T, w_b), F32)],
        compiler_params=_params(2))(z, z, z, do, total)


def _softmax_rows(s):
    m = jnp.max(s, axis=-1, keepdims=True)
    p = jnp.exp(s - m)
    return p / jnp.sum(p, axis=-1, keepdims=True)


def _xattn_forward(name, q, kv):
    T, D = q.shape
    Nm = kv.shape[0]
    dh = D // X_HEADS
    tq = _pick(512, [T])

    def body(q_ref, k_ref, v_ref, o_ref):
        p = _softmax_rows(_dot_nt(q_ref[...], k_ref[...]))
        o_ref[...] = jnp.dot(p.astype(BF16), v_ref[...], preferred_element_type=F32).astype(BF16)

    return pl.pallas_call(
        body, name=name, grid=(T // tq, X_HEADS),
        in_specs=[pl.BlockSpec((tq, dh), lambda i, h: (i, h)),
                  pl.BlockSpec((Nm, dh), lambda i, h: (0, h)),
                  pl.BlockSpec((Nm, dh), lambda i, h: (0, X_HEADS + h))],
        out_specs=pl.BlockSpec((tq, dh), lambda i, h: (i, h)),
        out_shape=jax.ShapeDtypeStruct((T, D), BF16),
        compiler_params=_params(2))(q, kv, kv)


def _xattn_backward(name, q, kv, do):
    T, D = q.shape
    Nm = kv.shape[0]
    dh = D // X_HEADS
    tq = _pick(512, [T])
    scale = dh ** -0.5

    def body(q_ref, k_ref, v_ref, do_ref, dq_ref, dkv_ref):
        @pl.when(pl.program_id(1) == 0)
        def _():
            dkv_ref[...] = jnp.zeros_like(dkv_ref)

        qv, kk, vv, dov = q_ref[...], k_ref[...], v_ref[...], do_ref[...]
        p = _softmax_rows(_dot_nt(qv, kk))
        dp = _dot_nt(dov, vv)
        ds = (p * (dp - jnp.sum(dp * p, axis=-1, keepdims=True))).astype(BF16)
        dq_ref[...] = (jnp.dot(ds, kk, preferred_element_type=F32) * scale).astype(BF16)
        dkv_ref[0] += _dot_tn(ds, qv)
        dkv_ref[1] += _dot_tn(p.astype(BF16), dov)

    blk = pl.BlockSpec((tq, dh), lambda h, i: (i, h))
    return pl.pallas_call(
        body, name=name, grid=(X_HEADS, T // tq),
        in_specs=[blk, pl.BlockSpec((Nm, dh), lambda h, i: (0, h)),
                  pl.BlockSpec((Nm, dh), lambda h, i: (0, X_HEADS + h)), blk],
        out_specs=[blk, pl.BlockSpec((2, Nm, dh), lambda h, i: (0, 0, h))],
        out_shape=[jax.ShapeDtypeStruct((T, D), BF16), jax.ShapeDtypeStruct((2, Nm, D), F32)],
        compiler_params=_params(2))(q, kv, kv, do)


def _position():
    x, y, c = lax.axis_index("x"), lax.axis_index("y"), lax.axis_index("c")
    other_chips = [(1 - x, y), (x, 1 - y), (1 - x, 1 - y)]
    return x, y, c, other_chips


def _hbm_spec():
    return pl.BlockSpec(memory_space=pltpu.HBM)


def _sem_spec():
    return pl.BlockSpec(memory_space=pltpu.SEMAPHORE)


def _split_start(name, arrays, make_copies, n_sems, deps=()):
    n, d = len(arrays), len(deps)

    def body(*refs):
        ins = refs[:n]
        send_sems, recv_sems = refs[n + d], refs[n + d + 1]
        token = refs[-1]
        for cp in make_copies(ins, send_sems, recv_sems):
            cp.start()
        token[...] = jnp.zeros_like(token)

    res = pl.pallas_call(
        body, name=name,
        out_shape=(pltpu.SemaphoreType.DMA((n_sems,)), pltpu.SemaphoreType.DMA((n_sems,)),
                   *[pltpu.HBM(a.shape, a.dtype) for a in arrays],
                   jax.ShapeDtypeStruct((SUBLANE, LANE), F32)),
        in_specs=[_hbm_spec()] * n + [_any_spec()] * d,
        out_specs=(_sem_spec(), _sem_spec(), *[_hbm_spec()] * n,
                   pl.BlockSpec(memory_space=pltpu.VMEM)),
        input_output_aliases={i: 2 + i for i in range(n)},
        compiler_params=pltpu.CompilerParams(
            has_side_effects=pltpu.SideEffectType.DATAFLOW_SIDE_EFFECTING),
    )(*[pltpu.with_memory_space_constraint(a, pltpu.HBM) for a in arrays], *deps)
    return res[0], res[1], list(res[2:2 + n]), res[-1]


def _split_wait(name, arrays, send_sems, recv_sems, after, make_copies):
    n = len(arrays)
    after = list(after) if isinstance(after, (list, tuple)) else [after]

    def body(*refs):
        ins = refs[:n]
        send_ref, recv_ref = refs[n], refs[n + 1]
        for cp in make_copies(ins, send_ref, recv_ref):
            cp.wait_send()
            cp.wait_recv()

    return pl.pallas_call(
        body, name=name,
        out_shape=tuple(pltpu.HBM(a.shape, a.dtype) for a in arrays),
        in_specs=[_hbm_spec()] * n + [_sem_spec(), _sem_spec()] + [_any_spec()] * len(after),
        out_specs=tuple(_hbm_spec() for _ in arrays),
        input_output_aliases={i: i for i in range(n)},
        compiler_params=pltpu.CompilerParams(
            has_side_effects=pltpu.SideEffectType.DATAFLOW_SIDE_EFFECTING),
    )(*arrays, send_sems, recv_sems, *after)


def _gather_copies(refs, send_sems, recv_sems):
    x, y, c, chips = _position()
    me = 2 * x + y
    copies = []
    for i, ref in enumerate(refs):
        rows = ref.shape[1] // 2
        piece = ref.at[me, pl.ds(c * rows, rows), :]
        for j, (px, py) in enumerate(chips):
            copies.append(pltpu.make_async_remote_copy(
                src_ref=piece, dst_ref=piece, send_sem=send_sems.at[3 * i + j],
                recv_sem=recv_sems.at[3 * i + j], device_id=(px, py, c), device_id_type=MESH))
    return copies


def _scatter_copies(refs, send_sems, recv_sems):
    x, y, c, chips = _position()
    n = len(refs) // 2
    copies = []
    for i in range(n):
        for j, (px, py) in enumerate(chips):
            copies.append(pltpu.make_async_remote_copy(
                src_ref=refs[i].at[2 * px + py], dst_ref=refs[n + i].at[j],
                send_sem=send_sems.at[3 * i + j], recv_sem=recv_sems.at[3 * i + j],
                device_id=(px, py, c), device_id_type=MESH))
    return copies


def _cast_own(name, place, shard):
    rows, cols = shard.shape
    tr = _block_rows(rows, cols)

    def body(place_ref, w_ref, o_ref):
        o_ref[...] = w_ref[...].astype(BF16)

    grid_spec = pltpu.PrefetchScalarGridSpec(
        num_scalar_prefetch=1, grid=(rows // tr,),
        in_specs=[pl.BlockSpec((tr, cols), lambda r, pr: (r, 0))],
        out_specs=pl.BlockSpec((None, tr, cols), lambda r, pr: (pr[0], r, 0)))
    return pl.pallas_call(
        body, name=name, grid_spec=grid_spec,
        out_shape=jax.ShapeDtypeStruct((N_CHIPS, rows, cols), BF16),
        compiler_params=_params(1))(place, shard)


def _forward_to_sibling(name, arrays, deps=()):
    n = len(arrays)

    def body(*refs):
        ins = refs[:n]
        send_sems, recv_sems = refs[-2:]
        x, y, c, chips = _position()
        sends = []
        for i in range(n):
            rows = ins[i].shape[1] // 2
            for j, (px, py) in enumerate(chips):
                piece = ins[i].at[2 * px + py, pl.ds(c * rows, rows), :]
                cp = pltpu.make_async_remote_copy(
                    src_ref=piece, dst_ref=piece, send_sem=send_sems.at[i, j],
                    recv_sem=recv_sems.at[i, j], device_id=(x, y, 1 - c), device_id_type=MESH)
                cp.start()
                sends.append(cp)
        for i in range(n):
            rows = ins[i].shape[1] // 2
            for j, (px, py) in enumerate(chips):
                piece = ins[i].at[2 * px + py, pl.ds((1 - c) * rows, rows), :]
                pltpu.make_async_remote_copy(
                    src_ref=piece, dst_ref=piece, send_sem=send_sems.at[i, j],
                    recv_sem=recv_sems.at[i, j], device_id=(x, y, 1 - c),
                    device_id_type=MESH).wait_recv()
        for cp in sends:
            cp.wait_send()

    return pl.pallas_call(
        body, name=name,
        in_specs=[_any_spec()] * (n + len(deps)), out_specs=[_any_spec()] * n,
        out_shape=[jax.ShapeDtypeStruct(a.shape, a.dtype) for a in arrays],
        input_output_aliases={i: i for i in range(n)},
        scratch_shapes=[pltpu.SemaphoreType.DMA((n, 3))] * 2,
    )(*arrays, *deps)


def _swap_copies(refs, send_sems, recv_sems):
    x, y, c, _ = _position()
    n = len(refs) // 2
    copies = []
    for i in range(n):
        rows = refs[i].shape[1] // 2
        copies.append(pltpu.make_async_remote_copy(
            src_ref=refs[i].at[:, pl.ds((1 - c) * rows, rows), :], dst_ref=refs[n + i],
            send_sem=send_sems.at[i], recv_sem=recv_sems.at[i],
            device_id=(x, y, 1 - c), device_id_type=MESH))
    return copies


def _share_halves(name, shards):
    n = len(shards)

    def body(*refs):
        ins = refs[:n]
        send_sems, recv_sems = refs[2 * n:]
        x, y, c, _ = _position()
        copies = []
        for i in range(n):
            rows = ins[i].shape[0] // 2
            mine = ins[i].at[pl.ds(c * rows, rows), :]
            cp = pltpu.make_async_remote_copy(
                src_ref=mine, dst_ref=mine, send_sem=send_sems.at[i], recv_sem=recv_sems.at[i],
                device_id=(x, y, 1 - c), device_id_type=MESH)
            cp.start()
            copies.append(cp)
        for i, cp in enumerate(copies):
            rows = ins[i].shape[0] // 2
            theirs = ins[i].at[pl.ds((1 - c) * rows, rows), :]
            pltpu.make_async_remote_copy(
                src_ref=theirs, dst_ref=theirs, send_sem=send_sems.at[i],
                recv_sem=recv_sems.at[i], device_id=(x, y, 1 - c), device_id_type=MESH).wait_recv()
            cp.wait_send()

    return pl.pallas_call(
        body, name=name,
        in_specs=[_any_spec()] * n, out_specs=[_any_spec()] * n,
        out_shape=[jax.ShapeDtypeStruct(s.shape, s.dtype) for s in shards],
        input_output_aliases={i: i for i in range(n)},
        scratch_shapes=[pltpu.SemaphoreType.DMA((n,))] * 2,
    )(*shards)


def _small_copies(refs, send_sems, recv_sems):
    packed, slots = refs
    x, y, c, _ = _position()
    me = 4 * x + 2 * y + c
    copies = []
    for r in range(1, N_DEV):
        peer = (x ^ ((r >> 2) & 1), y ^ ((r >> 1) & 1), c ^ (r & 1))
        copies.append(pltpu.make_async_remote_copy(
            src_ref=packed, dst_ref=slots.at[me], send_sem=send_sems.at[r - 1],
            recv_sem=recv_sems.at[r - 1], device_id=peer, device_id_type=MESH))
    return copies


def _block_rows(rows, cols, itemsize=4, target=1 << 20):
    return _pick(max(BF16_ROWS, target // (cols * itemsize)), [rows], unit=BF16_ROWS)


def _pair_sum(name, place, grad, received):
    P, rows, cols = received.shape
    tr = _block_rows(rows, cols, itemsize=2, target=2 << 20)
    nb = rows // tr

    def body(place_ref, g_ref, r_ref, o_ref):
        o_ref[...] = (g_ref[...].astype(F32) + r_ref[...].astype(F32)).astype(BF16)

    grid_spec = pltpu.PrefetchScalarGridSpec(
        num_scalar_prefetch=1, grid=(P, nb),
        in_specs=[pl.BlockSpec((None, tr, cols), lambda p, r, pr: (p, pr[1] * nb + r, 0)),
                  pl.BlockSpec((None, tr, cols), lambda p, r, pr: (p, r, 0))],
        out_specs=pl.BlockSpec((None, tr, cols), lambda p, r, pr: (p, r, 0)))
    return pl.pallas_call(
        body, name=name, grid_spec=grid_spec,
        out_shape=jax.ShapeDtypeStruct(received.shape, BF16),
        compiler_params=_params(2))(place, grad, received)


def _final_sum(name, place, grad, received, from_chips):
    _, rows, cols = received.shape
    tr = _block_rows(rows, cols)
    nb = rows // tr

    def body(place_ref, g_ref, r_ref, c_ref, o_ref):
        acc = g_ref[...].astype(F32) + r_ref[...].astype(F32)
        for j in range(3):
            acc = acc + c_ref[j].astype(F32)
        o_ref[...] = acc

    grid_spec = pltpu.PrefetchScalarGridSpec(
        num_scalar_prefetch=1, grid=(nb,),
        in_specs=[pl.BlockSpec((None, tr, cols), lambda r, pr: (pr[0], pr[1] * nb + r, 0)),
                  pl.BlockSpec((None, tr, cols), lambda r, pr: (pr[0], r, 0)),
                  pl.BlockSpec((3, tr, cols), lambda r, pr: (0, r, 0))],
        out_specs=pl.BlockSpec((tr, cols), lambda r, pr: (pr[1] * nb + r, 0)))
    return pl.pallas_call(
        body, name=name, grid_spec=grid_spec,
        out_shape=jax.ShapeDtypeStruct((2 * rows, cols), F32),
        compiler_params=_params(1))(place, grad, received, from_chips)


def _sum_devices(name, me, gathered, own):
    n_dev, rows, cols = gathered.shape
    tr = _pick(256, [rows])

    def body(me_ref, g_ref, own_ref, o_ref):
        term = lambda d: jnp.where(me_ref[0] == d, own_ref[...], g_ref[d])
        acc = term(0)
        for d in range(1, n_dev):
            acc = acc + term(d)
        o_ref[...] = acc

    grid_spec = pltpu.PrefetchScalarGridSpec(
        num_scalar_prefetch=1, grid=(rows // tr,),
        in_specs=[pl.BlockSpec((n_dev, tr, cols), lambda r, me_ref: (0, r, 0)),
                  pl.BlockSpec((tr, cols), lambda r, me_ref: (r, 0))],
        out_specs=pl.BlockSpec((tr, cols), lambda r, me_ref: (r, 0)))
    return pl.pallas_call(
        body, name=name, grid_spec=grid_spec,
        out_shape=jax.ShapeDtypeStruct((rows, cols), F32),
        compiler_params=_params(1))(me, gathered, own)


def _adamw(name, w, g, m, v):
    rows, cols = w.shape
    tr = _block_rows(rows, cols)
    c1 = 1.0 / (1.0 - ADAM_B1 ** ADAM_STEP)
    c2 = 1.0 / (1.0 - ADAM_B2 ** ADAM_STEP)

    def body(w_ref, g_ref, m_ref, v_ref, go_ref, d_ref, nm_ref, nv_ref):
        gv = g_ref[...]
        go_ref[...] = gv
        nm = ADAM_B1 * m_ref[...] + (1.0 - ADAM_B1) * gv
        nv = ADAM_B2 * v_ref[...] + (1.0 - ADAM_B2) * (gv * gv)
        nm_ref[...] = nm
        nv_ref[...] = nv
        d_ref[...] = -ADAM_LR * ((nm * c1) / (jnp.sqrt(nv * c2) + ADAM_EPS) + ADAM_WD * w_ref[...])

    blk = pl.BlockSpec((tr, cols), lambda r: (r, 0))
    shape = jax.ShapeDtypeStruct((rows, cols), F32)
    return pl.pallas_call(
        body, name=name, grid=(rows // tr,), in_specs=[blk] * 4, out_specs=[blk] * 4,
        out_shape=[shape] * 4, compiler_params=_params(1))(w, g, m, v)


BIG = ("ffn1_w_in", "ffn1_w_out", "w_mix_in", "w_mix_out", "w_cq", "w_ckv", "w_co",
       "ffn2_w_in", "ffn2_w_out")
BIG_KIND = {"ffn1_w_in": "c", "ffn1_w_out": "r", "w_mix_in": "c", "w_mix_out": "r", "w_cq": "r",
            "w_ckv": "c", "w_co": "r", "ffn2_w_in": "c", "ffn2_w_out": "r"}
GATHER_GROUPS = (("ffn1_in", ("ffn1_w_in",)), ("ffn1_out", ("ffn1_w_out",)),
                 ("mix", ("w_mix_in", "w_mix_out")), ("cross", ("w_cq", "w_ckv", "w_co")),
                 ("ffn2_in", ("ffn2_w_in",)), ("ffn2_out", ("ffn2_w_out",)))
GATHER_AFTER = (("ffn1_in", None), ("ffn1_out", "ffn1_in"), ("mix", "ffn1_out"), ("cross", "mix"),
                ("ffn2_in", "mix"), ("ffn2_out", "cross"))
SCATTER_ORDER = ("ffn2", "cross", "mix", "ffn1")
SMALL = ("ffn1_norm", "mix_norm", "ln_v_gain", "ln_v_bias", "spatial_w", "spatial_b", "gnorm_a",
         "gnorm_b", "cross_norm", "mem_norm", "ffn2_norm", "final_norm")
WEIGHTS = ("ffn1_norm", "ffn1_w_in", "ffn1_w_out", "mix_norm", "w_mix_in", "ln_v_gain",
           "ln_v_bias", "spatial_w", "spatial_b", "gnorm_a", "gnorm_b", "w_mix_out", "cross_norm",
           "mem_norm", "w_cq", "w_ckv", "w_co", "ffn2_norm", "ffn2_w_in", "ffn2_w_out",
           "final_norm")


def _pack(arrays):
    return jnp.concatenate([a.reshape(-1, LANE) for a in arrays], axis=0)


def _unpack(packed, like):
    out, row = [], 0
    for a in like:
        rows = a.size // LANE
        out.append(packed[row:row + rows].reshape(a.shape))
        row += rows
    return out


def _local_step(x, mem, target, small, weights_of, start_tokens, grads_ready, grads_flush):
    T, D = x.shape
    vec = lambda name: small[name].reshape(1, -1)
    w_a = small["ln_v_gain"].size
    w_b = small["gnorm_b"].size
    G = w_a // GROUP_DIM
    w_s = small["spatial_w"].reshape(G, SGU_BLOCK, SGU_BLOCK)
    b_t = small["spatial_b"].reshape(G, SGU_BLOCK).T

    h1, ffn1_saved = _ffn_forward("ffn1", x, vec("ffn1_norm"), weights_of, deps=start_tokens)
    n2 = _rmsnorm_fwd("mix_norm", h1, vec("mix_norm"))
    big = weights_of("mix", n2)
    (z,) = _matmul("mix_in", Mat(n2), big["w_mix_in"], "nn", [("c", 1, F32)], tm=2048, tn=256)
    z = z[0]
    y = _sgu_forward("sgu", z, vec("ln_v_gain"), vec("ln_v_bias"), w_s, b_t, vec("gnorm_a"), D)
    yb, sb_total = _sb_forward("stickbreak", z, w_a, w_b)
    y = _rmsnorm_fwd("gnorm_b", yb, vec("gnorm_b"), into=y, col=w_a // w_b)

    def add_res(acc, ex, out):
        out[0][...] = ex[0][...] + acc

    (h2,) = _matmul("mix_out", Mat(y), big["w_mix_out"], "nn", [("c", 1, F32)],
                    tm=1024, tn=1024, extras=[Mat(h1)], epi=add_res)
    h2 = h2[0]
    n3 = _rmsnorm_fwd("cross_norm", h2, vec("cross_norm"))
    memn = _rmsnorm_fwd("mem_norm", mem, vec("mem_norm"))
    big.update(weights_of("cross", n3))
    x_scale = (D // X_HEADS) ** -0.5

    def scaled(acc, ex, out):
        out[0][...] = (acc * x_scale).astype(BF16)

    (q,) = _matmul("cross_q", Mat(n3), big["w_cq"], "nn", [("c", 1, BF16)],
                   tm=1024, tn=1024, epi=scaled)
    (kv,) = _matmul("cross_kv", Mat(memn), big["w_ckv"], "nn", [("c", 1, BF16)], tm=256, tn=1024)
    q, kv = q[0], kv[0]
    o = _xattn_forward("cross_attn", q, kv)
    (h3,) = _matmul("cross_out", Mat(o), big["w_co"], "nn", [("c", 1, F32)],
                    tm=1024, tn=1024, extras=[Mat(h2)], epi=add_res)
    h3 = h3[0]
    h4, ffn2_saved = _ffn_forward("ffn2", h3, vec("ffn2_norm"), weights_of)

    gs = {}
    loss_tile, dh4, dh4_bf, gs["final_norm"] = _loss_head("loss_head", h4, vec("final_norm"), target)
    dh3, dh3_bf, gs["ffn2_norm"] = _ffn_backward(
        "ffn2", h3, vec("ffn2_norm"), ffn2_saved, dh4, dh4_bf, grads_ready, grads_flush)

    (do,) = _matmul("cross_do", Mat(dh3_bf), big["w_co"], "nt", [("c", 1, BF16)], tm=1024, tn=512)
    (dw_co,) = _matmul("cross_dwo", Mat(o), Mat(dh3_bf), "tn", [("r", N_CHIPS, BF16)],
                       tm=512, tn=1024)
    dq, dkv = _xattn_backward("cross_attn_bwd", q, kv, do[0])
    (dw_cq,) = _matmul("cross_dwq", Mat(n3), Mat(dq), "tn", [("r", N_CHIPS, BF16)],
                       tm=512, tn=1024)
    (dw_ckv,) = _matmul("cross_dwkv", Mat(memn), Mat(dkv), "tn", [("c", N_CHIPS, BF16)],
                        tm=1024, tn=1024)
    token = grads_ready("cross", {"w_cq": dw_cq, "w_ckv": dw_ckv, "w_co": dw_co})
    dq = _tie("cross_dq_after_swap", dq, [token])
    (dn3,) = _matmul("cross_dn", Mat(dq), big["w_cq"], "nt", [("c", 1, F32)], tm=1024, tn=512)
    (dmemn,) = _matmul("cross_dmem", Mat(dkv), big["w_ckv"], "nt", [("c", 1, F32)],
                       tm=256, tn=1024, tk=1024)
    (gs["mem_norm"],) = _rmsnorm_bwd("mem_dnorm", mem, vec("mem_norm"), dmemn[0], want_dx=False)
    dn3 = _tie("cross_dn_after_scatter", dn3, [grads_flush("cross", gs["mem_norm"])])
    dh2, dh2_bf, gs["cross_norm"] = _rmsnorm_bwd("cross_dnorm", h2, vec("cross_norm"), dn3[0],
                                                 dres=dh3)

    (dy,) = _matmul("mix_dy", Mat(dh2_bf), big["w_mix_out"], "nt", [("c", 1, F32)], tm=1024, tn=512)
    dy = dy[0]
    (dw_mix_out,) = _matmul("mix_dwout", Mat(y), Mat(dh2_bf), "tn", [("r", N_CHIPS, BF16)],
                            tm=512, tn=1024)
    dza, gs["ln_v_gain"], gs["ln_v_bias"], gs["spatial_w"], db, gs["gnorm_a"] = _sgu_backward(
        "sgu_bwd", z, dy, vec("ln_v_gain"), vec("ln_v_bias"), w_s, b_t, vec("gnorm_a"))
    gs["spatial_b"] = db.reshape(G, SGU_BLOCK)
    dob, gs["gnorm_b"] = _rmsnorm_bwd("gnorm_b_bwd", yb, vec("gnorm_b"), dy, dn_col=w_a // w_b,
                                      want_bf16=False)
    dqb, dkvb = _sb_backward("stickbreak_bwd", z, dob, sb_total, w_a, w_b)
    dz = jnp.concatenate([dza, dqb, dkvb[0].astype(BF16), dkvb[1].astype(BF16)], axis=1)
    (dw_mix_in,) = _matmul("mix_dwin", Mat(n2), Mat(dz), "tn", [("c", N_CHIPS, BF16)],
                           tm=1024, tn=1280)
    token = grads_ready("mix", {"w_mix_in": dw_mix_in, "w_mix_out": dw_mix_out})
    dz = _tie("mix_dz_after_swap", dz, [token])
    (dn2,) = _matmul("mix_dn", Mat(dz), big["w_mix_in"], "nt", [("c", 1, F32)],
                     tm=1024, tn=1024, tk=1280)
    dn2 = _tie("mix_dn_after_scatter", dn2, [grads_flush("mix", dn2)])
    dh1, dh1_bf, gs["mix_norm"] = _rmsnorm_bwd("mix_dnorm", h1, vec("mix_norm"), dn2[0], dres=dh2)

    dx, _, gs["ffn1_norm"] = _ffn_backward(
        "ffn1", x, vec("ffn1_norm"), ffn1_saved, dh1, dh1_bf, grads_ready, grads_flush)
    gs = {k: g.reshape(small[k].shape) for k, g in gs.items()}
    return loss_tile, dx, gs


def kernel(x, mem, ffn1_norm, ffn1_w_in, ffn1_w_out, mix_norm, w_mix_in, ln_v_gain, ln_v_bias, spatial_w, spatial_b, gnorm_a, gnorm_b, w_mix_out, cross_norm, mem_norm, w_cq, w_ckv, w_co, ffn2_norm, ffn2_w_in, ffn2_w_out, final_norm, loss_target, m_ffn1_norm, m_ffn1_w_in, m_ffn1_w_out, m_mix_norm, m_w_mix_in, m_ln_v_gain, m_ln_v_bias, m_spatial_w, m_spatial_b, m_gnorm_a, m_gnorm_b, m_w_mix_out, m_cross_norm, m_mem_norm, m_w_cq, m_w_ckv, m_w_co, m_ffn2_norm, m_ffn2_w_in, m_ffn2_w_out, m_final_norm, v_ffn1_norm, v_ffn1_w_in, v_ffn1_w_out, v_mix_norm, v_w_mix_in, v_ln_v_gain, v_ln_v_bias, v_spatial_w, v_spatial_b, v_gnorm_a, v_gnorm_b, v_w_mix_out, v_cross_norm, v_mem_norm, v_w_cq, v_w_ckv, v_w_co, v_ffn2_norm, v_ffn2_w_in, v_ffn2_w_out, v_final_norm):
    given = dict(locals())
    w = {k: given[k] for k in WEIGHTS}
    m = {k: given["m_" + k] for k in WEIGHTS}
    v = {k: given["v_" + k] for k in WEIGHTS}

    cx, cy, cc = lax.axis_index("x"), lax.axis_index("y"), lax.axis_index("c")
    place = jnp.stack([2 * cx + cy, cc]).astype(jnp.int32)

    names_of = dict(GATHER_GROUPS)
    own = {g: [_cast_own(f"cast_{k}", place, w[k][0]) for k in names] for g, names in GATHER_GROUPS}
    gathers = {}

    def start_gather(group, deps):
        send, recv, arrays, token = _split_start(f"gather_start_{group}", own[group],
                                                 _gather_copies, 3 * len(own[group]), deps)
        gathers[group] = (send, recv, arrays)
        return token

    start_tokens = [start_gather(g, ()) for g, after in GATHER_AFTER if after is None]
    start_tokens += [a for g, after in GATHER_AFTER if after is not None for a in own[g]]

    def weights_of(group, after):
        send, recv, arrays = gathers[group]
        arrays = _split_wait(f"gather_wait_{group}", arrays, send, recv, after, _gather_copies)
        tokens = [start_gather(g, (arrays[0],)) for g, a in GATHER_AFTER if a == group]
        arrays = _forward_to_sibling(f"gather_forward_{group}", list(arrays), tokens)
        return {k: Mat(a, BIG_KIND[k]) for k, a in zip(names_of[group], arrays)}

    swaps, scatters = {}, {}

    def grads_ready(group, partial):
        names = list(partial)
        grads_ = [partial[k] for k in names]
        lands = [lax.empty((g.shape[0], g.shape[1] // 2, g.shape[2]), g.dtype) for g in grads_]
        send, recv, arrays, token = _split_start(f"swap_start_{group}", grads_ + lands,
                                                 _swap_copies, len(names))
        swaps[group] = (names, send, recv, arrays)
        return token

    def grads_flush(group, after):
        names, send, recv, arrays = swaps[group]
        arrays = _split_wait(f"swap_wait_{group}", arrays, send, recv, after, _swap_copies)
        grads_, from_sibling = arrays[:len(names)], arrays[len(names):]
        sums = [_pair_sum(f"pair_sum_{k}", place, g, r)
                for k, g, r in zip(names, grads_, from_sibling)]
        lands = [lax.empty((3,) + s.shape[1:], s.dtype) for s in sums]
        send, recv, arrays, token = _split_start(f"scatter_start_{group}", sums + lands,
                                                 _scatter_copies, 3 * len(names))
        scatters[group] = (names, grads_, from_sibling, send, recv, arrays)
        return token

    small = {k: w[k] for k in SMALL}
    loss_tile, grad_x, gs = _local_step(x[0], mem[0], loss_target[0], small, weights_of,
                                        start_tokens, grads_ready, grads_flush)

    packed = _pack([gs[k] for k in SMALL] + [loss_tile])
    slots = jnp.zeros((N_DEV,) + packed.shape, packed.dtype)
    small_send, small_recv, small_arrays, _ = _split_start(
        "small_start", [packed, slots], _small_copies, N_DEV - 1)

    grad, delta, new_m, new_v = {}, {}, {}, {}
    after = [grad_x]
    for group in SCATTER_ORDER:
        names, grads_, from_sibling, send, recv, arrays = scatters[group]
        arrays = _split_wait(f"scatter_wait_{group}", arrays, send, recv, after, _scatter_copies)
        from_chips = arrays[len(names):]
        shards = [_final_sum(f"final_sum_{k}", place, g, r, f)
                  for k, g, r, f in zip(names, grads_, from_sibling, from_chips)]
        shards = _share_halves(f"share_{group}", shards)
        after = []
        for k, g_ in zip(names, shards):
            g_, d_, m_, v_ = _adamw(f"adamw_{k}", w[k][0], g_, m[k][0], v[k][0])
            grad[k], delta[k], new_m[k], new_v[k] = g_[None], d_[None], m_[None], v_[None]
            after.append(v_)

    packed, slots = _split_wait("small_wait", small_arrays, small_send, small_recv, after,
                                _small_copies)
    me = (4 * cx + 2 * cy + cc).astype(jnp.int32).reshape(1)
    total = _sum_devices("sum_small", me, slots, packed)
    n_small = total.shape[0] - SUBLANE
    loss = total[n_small, 0]
    small_g = total[:n_small]
    g_s, d_s, m_s, v_s = _adamw("adamw_small", _pack([w[k] for k in SMALL]), small_g,
                                _pack([m[k] for k in SMALL]), _pack([v[k] for k in SMALL]))
    like = [w[k] for k in SMALL]
    for k, g_, d_, m_, v_ in zip(SMALL, _unpack(g_s, like), _unpack(d_s, like),
                                 _unpack(m_s, like), _unpack(v_s, like)):
        grad[k], delta[k], new_m[k], new_v[k] = g_, d_, m_, v_

    return (loss, grad_x[None], *[grad[k] for k in WEIGHTS], *[delta[k] for k in WEIGHTS],
            *[new_m[k] for k in WEIGHTS], *[new_v[k] for k in WEIGHTS])
```

```python
import functools
import math

import jax
import jax.numpy as jnp
from jax import lax
from jax.experimental import pallas as pl
from jax.experimental.pallas import tpu as pltpu

F32 = jnp.float32
BF16 = jnp.bfloat16
MESH = pl.DeviceIdType.MESH

EPS = 1e-6
CHUNK = 64
SGU_BLOCK = 128
GROUP_DIM = 128
X_HEADS = 4
N_CHIPS = 4
N_DEV = 8
LANE = 128
SUBLANE = 8
BF16_ROWS = 16

ADAM_LR = 0.001
ADAM_B1 = 0.9
ADAM_B2 = 0.999
ADAM_EPS = 1e-08
ADAM_WD = 0.01
ADAM_STEP = 10

V7X_VMEM_BYTES = 64 << 20
VMEM_LIMIT = V7X_VMEM_BYTES - (8 << 20)


def _params(n_grid):
    return pltpu.CompilerParams(dimension_semantics=("arbitrary",) * n_grid,
                                vmem_limit_bytes=VMEM_LIMIT)


def _pick(pref, dims, unit=None):
    g = functools.reduce(math.gcd, dims)
    if unit is None:
        unit = LANE if g % LANE == 0 else SUBLANE
    cands = [d for d in range(unit, g + 1, unit) if g % d == 0] or [g]
    return min(cands, key=lambda d: abs(math.log(d / pref)))


def _any_spec():
    return pl.BlockSpec(memory_space=pl.ANY)


class Mat:
    def __init__(self, arr, kind="c"):
        if arr.ndim == 2:
            arr = arr[None]
        self.arr, self.kind = arr, kind
        self.P, self.prow, self.pcol = arr.shape
        self.rows = self.prow * (self.P if kind == "r" else 1)
        self.cols = self.pcol * (self.P if kind == "c" else 1)
        self.dtype = arr.dtype

    def spec(self, tr, tc, rc_fn):
        if self.kind == "c":
            per = self.pcol // tc
            assert per * tc == self.pcol, (self.pcol, tc)

            def imap(*g):
                i, j = rc_fn(*g)
                return (j // per, i, j % per)
        else:
            per = self.prow // tr
            assert per * tr == self.prow, (self.prow, tr)

            def imap(*g):
                i, j = rc_fn(*g)
                return (i // per, i % per, j)
        return pl.BlockSpec((None, tr, tc), imap)

    def two_d(self):
        assert self.P == 1
        return self.arr[0]


def _out_mat(kind, P, rows, cols, dtype):
    shape = (P, rows, cols // P) if kind == "c" else (P, rows // P, cols)
    return jax.ShapeDtypeStruct(shape, dtype)


def _matmul(name, A, B, mode, outs, *, tm=1024, tn=1024, tk=2048, extras=(), epi=None):
    if mode == "nn":
        M, K, N = A.rows, A.cols, B.cols
        assert B.rows == K
    elif mode == "nt":
        M, K, N = A.rows, A.cols, B.rows
        assert B.cols == K
    else:
        K, M, N = A.rows, A.cols, B.cols
        assert B.rows == K
    mdims, ndims, kdims = [M], [N], [K]
    whole_b = mode == "nn" and B.kind == "r" and B.P > 1 and K <= tk
    if whole_b:
        kdims.append(A.pcol)
        ndims.append(B.pcol)
    elif mode == "tn":
        assert A.kind == "c" and B.kind == "c"
        mdims.append(A.pcol)
        ndims.append(B.pcol)
    else:
        (mdims if A.kind == "r" else kdims).append(A.prow if A.kind == "r" else A.pcol)
        if mode == "nn":
            (kdims if B.kind == "r" else ndims).append(B.prow if B.kind == "r" else B.pcol)
        else:
            (ndims if B.kind == "r" else kdims).append(B.prow if B.kind == "r" else B.pcol)
    for o in list(outs) + list(extras):
        if isinstance(o, Mat):
            (mdims if o.kind == "r" else ndims).append(o.prow if o.kind == "r" else o.pcol)
        elif isinstance(o[0], str):
            (mdims if o[0] == "r" else ndims).append((M if o[0] == "r" else N) // o[1])
    tm, tn = _pick(tm, mdims), _pick(tn, ndims)
    tk = K if mode == "tn" else _pick(tk, kdims)
    nk = K // tk
    grid = (M // tm, N // tn, nk)

    if mode == "tn":
        a_spec = A.spec(K, tm, lambda m, n, k: (0, m))
        b_spec = B.spec(K, tn, lambda m, n, k: (0, n))
    else:
        a_spec = A.spec(tm, tk, lambda m, n, k: (m, k))
        if whole_b:
            b_spec = pl.BlockSpec((B.P, B.prow, tn), lambda m, n, k: (0, 0, n))
        elif mode == "nn":
            b_spec = B.spec(tk, tn, lambda m, n, k: (k, n))
        else:
            b_spec = B.spec(tn, tk, lambda m, n, k: (n, k))

    def mn_spec(o):
        if isinstance(o, Mat):
            return o.spec(tm, tn, lambda m, n, k: (m, n))
        if isinstance(o[0], str):
            kind, P = o[0], o[1]
            fake = Mat.__new__(Mat)
            fake.kind, fake.P = kind, P
            fake.prow = M // P if kind == "r" else M
            fake.pcol = N // P if kind == "c" else N
            return Mat.spec(fake, tm, tn, lambda m, n, k: (m, n))
        return o[1](tm, tn)

    out_shapes = tuple(_out_mat(o[0], o[1], M, N, o[2]) if isinstance(o[0], str) else o[0]
                       for o in outs)
    out_specs = tuple(mn_spec(o) for o in outs)
    extra_arrays = tuple(e.arr if isinstance(e, Mat) else e[0] for e in extras)
    extra_specs = tuple(mn_spec(e) for e in extras)
    n_ex, n_out = len(extras), len(outs)
    tt = _pick(256, [tm])
    dims = (((1,), (1 if mode == "nt" else 0,)), ((), ()))

    def body(*refs):
        a_ref, b_ref = refs[:2]
        ex_refs = refs[2:2 + n_ex]
        out_refs = refs[2 + n_ex:2 + n_ex + n_out]
        scratch = refs[2 + n_ex + n_out:]
        if mode == "tn":
            at_ref = scratch[0]

            @pl.when(pl.program_id(1) == 0)
            def _():
                for c0 in range(0, tm, tt):
                    at_ref[c0:c0 + tt, :] = a_ref[:, c0:c0 + tt].astype(F32).T.astype(BF16)

            lhs = at_ref[...]
        else:
            lhs = a_ref[...].astype(BF16)
        rhs = b_ref[...].reshape(K, tn) if whole_b else b_ref[...]
        part = lax.dot_general(lhs, rhs.astype(BF16), dims, preferred_element_type=F32)

        def finish(acc):
            if epi is None:
                out_refs[0][...] = acc.astype(out_refs[0].dtype)
            else:
                epi(acc, ex_refs, out_refs)

        if nk == 1:
            finish(part)
        else:
            acc_ref = scratch[0]
            k = pl.program_id(2)

            @pl.when(k == 0)
            def _():
                acc_ref[...] = part

            @pl.when(k > 0)
            def _():
                acc_ref[...] += part

            @pl.when(k == nk - 1)
            def _():
                finish(acc_ref[...])

    scratch_shapes = []
    if mode == "tn":
        scratch_shapes.append(pltpu.VMEM((tm, K), BF16))
    elif nk > 1:
        scratch_shapes.append(pltpu.VMEM((tm, tn), F32))
    res = pl.pallas_call(
        body, name=name, grid=grid,
        in_specs=[a_spec, b_spec, *extra_specs], out_specs=out_specs, out_shape=out_shapes,
        scratch_shapes=scratch_shapes, compiler_params=_params(3),
    )(A.arr, B.arr, *extra_arrays)
    return res


def _row_tile(T):
    return _pick(256, [T])


def _tie(name, x, deps):
    def body(*refs):
        refs[-1][...] = jnp.zeros_like(refs[-1])

    return pl.pallas_call(
        body, name=name, in_specs=[_any_spec()] * (1 + len(deps)),
        out_specs=(_any_spec(), pl.BlockSpec(memory_space=pltpu.VMEM)),
        out_shape=(jax.ShapeDtypeStruct(x.shape, x.dtype),
                   jax.ShapeDtypeStruct((SUBLANE, LANE), F32)),
        input_output_aliases={0: 0},
    )(x, *deps)[0]


def _rmsnorm_fwd(name, x, g, *, into=None, col=0, deps=()):
    T, W = x.shape
    tr = _row_tile(T)

    def body(x_ref, g_ref, *rest):
        o_ref = rest[-1]
        xv = x_ref[...]
        rstd = lax.rsqrt(jnp.mean(xv * xv, axis=-1, keepdims=True) + EPS)
        o_ref[...] = (xv * rstd * g_ref[...]).astype(o_ref.dtype)

    in_specs = [pl.BlockSpec((tr, W), lambda i: (i, 0)), pl.BlockSpec((1, W), lambda i: (0, 0))]
    args = [x, g]
    kwargs = {}
    if into is None:
        out_shape = jax.ShapeDtypeStruct((T, W), BF16)
    else:
        out_shape = jax.ShapeDtypeStruct(into.shape, into.dtype)
        in_specs.append(_any_spec())
        args.append(into)
        kwargs["input_output_aliases"] = {2: 0}
    in_specs += [_any_spec()] * len(deps)
    args += list(deps)
    return pl.pallas_call(
        body, name=name, grid=(T // tr,), in_specs=in_specs,
        out_specs=pl.BlockSpec((tr, W), lambda i: (i, col)), out_shape=out_shape,
        compiler_params=_params(1), **kwargs)(*args)


def _rmsnorm_bwd(name, x, g, dn, *, dn_col=0, dres=None, want_dx=True, want_bf16=True):
    T, W = x.shape
    tr = _row_tile(T)
    has_res = dres is not None

    def body(*refs):
        x_ref, g_ref, dn_ref = refs[:3]
        pos = 3
        dres_ref = None
        if has_res:
            dres_ref = refs[pos]
            pos += 1
        outs = refs[pos:]
        dg_ref = outs[-1]
        xv = x_ref[...]
        rstd = lax.rsqrt(jnp.mean(xv * xv, axis=-1, keepdims=True) + EPS)
        xhat = xv * rstd
        dnv = dn_ref[...].astype(F32)

        @pl.when(pl.program_id(0) == 0)
        def _():
            dg_ref[...] = jnp.zeros_like(dg_ref)

        dg_ref[...] += jnp.sum(dnv * xhat, axis=0, keepdims=True)
        if want_dx:
            t = dnv * g_ref[...]
            dx = rstd * (t - xhat * jnp.mean(t * xhat, axis=-1, keepdims=True))
            if has_res:
                dx = dx + dres_ref[...]
            outs[0][...] = dx
            if want_bf16:
                outs[1][...] = dx.astype(BF16)

    row = pl.BlockSpec((tr, W), lambda i: (i, 0))
    in_specs = [row, pl.BlockSpec((1, W), lambda i: (0, 0)),
                pl.BlockSpec((tr, W), lambda i: (i, dn_col))]
    args = [x, g, dn]
    if has_res:
        in_specs.append(row)
        args.append(dres)
    out_shape, out_specs = [], []
    if want_dx:
        out_shape.append(jax.ShapeDtypeStruct((T, W), F32))
        out_specs.append(row)
        if want_bf16:
            out_shape.append(jax.ShapeDtypeStruct((T, W), BF16))
            out_specs.append(row)
    out_shape.append(jax.ShapeDtypeStruct((1, W), F32))
    out_specs.append(pl.BlockSpec((1, W), lambda i: (0, 0)))
    return pl.pallas_call(
        body, name=name, grid=(T // tr,), in_specs=in_specs, out_specs=out_specs,
        out_shape=out_shape, compiler_params=_params(1))(*args)


def _loss_head(name, h, g, target):
    T, W = h.shape
    tr = _row_tile(T)

    def body(h_ref, g_ref, t_ref, loss_ref, dx_ref, dxb_ref, dg_ref):
        xv = h_ref[...]
        gv = g_ref[...]
        rstd = lax.rsqrt(jnp.mean(xv * xv, axis=-1, keepdims=True) + EPS)
        xhat = xv * rstd
        diff = xhat * gv - t_ref[...]

        @pl.when(pl.program_id(0) == 0)
        def _():
            dg_ref[...] = jnp.zeros_like(dg_ref)
            loss_ref[...] = jnp.zeros_like(loss_ref)

        loss_ref[...] += 0.5 * jnp.sum(jnp.mean(diff * diff, axis=-1, keepdims=True))
        dnv = diff * (1.0 / W)
        dg_ref[...] += jnp.sum(dnv * xhat, axis=0, keepdims=True)
        t = dnv * gv
        dx = rstd * (t - xhat * jnp.mean(t * xhat, axis=-1, keepdims=True))
        dx_ref[...] = dx
        dxb_ref[...] = dx.astype(BF16)

    row = pl.BlockSpec((tr, W), lambda i: (i, 0))
    vec = pl.BlockSpec((1, W), lambda i: (0, 0))
    return pl.pallas_call(
        body, name=name, grid=(T // tr,), in_specs=[row, vec, row],
        out_specs=[pl.BlockSpec((SUBLANE, LANE), lambda i: (0, 0)), row, row, vec],
        out_shape=[jax.ShapeDtypeStruct((SUBLANE, LANE), F32), jax.ShapeDtypeStruct((T, W), F32),
                   jax.ShapeDtypeStruct((T, W), BF16), jax.ShapeDtypeStruct((1, W), F32)],
        compiler_params=_params(1))(h, g, target)


def _sigmoid(x):
    return 1.0 / (1.0 + jnp.exp(-x))


def _ffn_in(name, n, W):
    T, D = n.shape
    F = W.cols // 2
    tm = _pick(2048, [T])
    tn = _pick(512, [W.pcol])
    per = W.pcol // tn

    def body(a_ref, wg_ref, wu_ref, gu_ref, act_ref):
        a = a_ref[...]
        gate = jnp.dot(a, wg_ref[...], preferred_element_type=F32)
        up = jnp.dot(a, wu_ref[...], preferred_element_type=F32)
        gu_ref[0] = gate.astype(BF16)
        gu_ref[1] = up.astype(BF16)
        act_ref[...] = (gate * _sigmoid(gate) * up).astype(BF16)

    return pl.pallas_call(
        body, name=name, grid=(T // tm, F // tn),
        in_specs=[pl.BlockSpec((tm, D), lambda m, j: (m, 0)),
                  pl.BlockSpec((None, D, tn), lambda m, j: (j // per, 0, j % per)),
                  pl.BlockSpec((None, D, tn), lambda m, j: (2 + j // per, 0, j % per))],
        out_specs=[pl.BlockSpec((2, tm, tn), lambda m, j: (0, m, j)),
                   pl.BlockSpec((tm, tn), lambda m, j: (m, j))],
        out_shape=[jax.ShapeDtypeStruct((2, T, F), BF16), jax.ShapeDtypeStruct((T, F), BF16)],
        compiler_params=_params(2))(n, W.arr, W.arr)


def _ffn_forward(tag, h, norm_g, weights_of, deps=()):
    n = _rmsnorm_fwd(f"{tag}_norm", h, norm_g, deps=deps)
    w_in = weights_of(f"{tag}_in", n)[f"{tag}_w_in"]
    gu, act = _ffn_in(f"{tag}_in", n, w_in)
    w_out = weights_of(f"{tag}_out", act)[f"{tag}_w_out"]

    def epi(acc, ex, out):
        out[0][...] = ex[0][...] + 0.5 * acc

    (h_out,) = _matmul(f"{tag}_out", Mat(act), w_out, "nn", [("c", 1, F32)],
                       tm=1024, tn=1024, tk=1408, extras=[Mat(h)], epi=epi)
    return h_out[0], (n, gu, act, w_in, w_out)


def _ffn_backward(tag, h_in, norm_g, saved, dh, dh_bf, grads_ready, grads_flush):
    n, gu, act, w_in, w_out = saved
    T, F = act.shape

    def epi(acc, ex, out):
        dact = 0.5 * acc
        gate = ex[0][0].astype(F32)
        up = ex[0][1].astype(F32)
        sig = _sigmoid(gate)
        out[0][0] = (dact * up * sig * (1.0 + gate * (1.0 - sig))).astype(BF16)
        out[0][1] = (dact * gate * sig).astype(BF16)

    def pair_spec(tm, tn):
        return pl.BlockSpec((2, tm, tn), lambda m, j, k: (0, m, j))

    (dgu,) = _matmul(f"{tag}_dact", Mat(dh_bf), w_out, "nt",
                     [(jax.ShapeDtypeStruct((2, T, F), BF16), pair_spec)],
                     tm=512, tn=1408, extras=[(gu, pair_spec)], epi=epi)

    def half(acc, ex, out):
        out[0][...] = (0.5 * acc).astype(out[0].dtype)

    (dw_out,) = _matmul(f"{tag}_dwout", Mat(act), Mat(dh_bf), "tn", [("r", N_CHIPS, BF16)],
                        tm=1408, tn=512, epi=half)
    (dw_in,) = _matmul(f"{tag}_dwin", Mat(n), Mat(dgu), "tn", [("c", N_CHIPS, BF16)],
                       tm=512, tn=1408)
    token = grads_ready(tag, {f"{tag}_w_in": dw_in, f"{tag}_w_out": dw_out})
    dgu = _tie(f"{tag}_dgu_after_swap", dgu, [token])
    (dn,) = _matmul(f"{tag}_dn", Mat(dgu), w_in, "nt", [("c", 1, F32)],
                    tm=1024, tn=1024, tk=2816)
    dn = _tie(f"{tag}_dn_after_scatter", dn, [grads_flush(tag, dn)])
    return _rmsnorm_bwd(f"{tag}_dnorm", h_in, norm_g, dn[0], dres=dh)


_GELU_C = math.sqrt(2.0 / math.pi)
_GELU_A = 0.044715


def _gelu(x):
    return 0.5 * x * (1.0 + jnp.tanh(_GELU_C * (x + _GELU_A * x * x * x)))


def _gelu_grad(x):
    th = jnp.tanh(_GELU_C * (x + _GELU_A * x * x * x))
    return 0.5 * (1.0 + th) + 0.5 * x * (1.0 - th * th) * _GELU_C * (1.0 + 3.0 * _GELU_A * x * x)


def _chunk_mask():
    t = lax.broadcasted_iota(jnp.int32, (SGU_BLOCK, SGU_BLOCK), 0) // CHUNK
    s = lax.broadcasted_iota(jnp.int32, (SGU_BLOCK, SGU_BLOCK), 1) // CHUNK
    return s <= t


def _sgu_group_forward(v_g, lg, lb, wm_bf, b_col):
    mu = jnp.mean(v_g, axis=-1, keepdims=True)
    xc = v_g - mu
    rstd = lax.rsqrt(jnp.mean(xc * xc, axis=-1, keepdims=True) + EPS)
    vhat = xc * rstd
    vn = vhat * lg + lb
    mixed = jnp.dot(wm_bf, vn.astype(BF16), preferred_element_type=F32) + b_col
    return vhat, rstd, vn, mixed


def _sgu_forward(name, z, ln_g, ln_b, w_s, b_t, gn, d_model):
    T = z.shape[0]
    W_A = ln_g.shape[1]
    G = W_A // GROUP_DIM

    def body(z_ref, lg_ref, lb_ref, w_ref, bt_ref, gn_ref, y_ref):
        mask = _chunk_mask()
        u = _gelu(z_ref[:, :W_A])
        v = _gelu(z_ref[:, W_A:])
        cols = []
        for g in range(G):
            sl = slice(g * GROUP_DIM, (g + 1) * GROUP_DIM)
            wm = jnp.where(mask, w_ref[g], 0.0).astype(BF16)
            _, _, _, mixed = _sgu_group_forward(v[:, sl], lg_ref[:, sl], lb_ref[:, sl], wm,
                                                bt_ref[:, g:g + 1])
            cols.append(u[:, sl] * mixed)
        ya = jnp.concatenate(cols, axis=1)
        rstd = lax.rsqrt(jnp.mean(ya * ya, axis=-1, keepdims=True) + EPS)
        y_ref[...] = (ya * rstd * gn_ref[...]).astype(BF16)

    vec = pl.BlockSpec((1, W_A), lambda i: (0, 0))
    return pl.pallas_call(
        body, name=name, grid=(T // SGU_BLOCK,),
        in_specs=[pl.BlockSpec((SGU_BLOCK, 2 * W_A), lambda i: (i, 0)), vec, vec,
                  pl.BlockSpec((G, SGU_BLOCK, SGU_BLOCK), lambda i: (0, 0, 0)),
                  pl.BlockSpec((SGU_BLOCK, G), lambda i: (0, 0)), vec],
        out_specs=pl.BlockSpec((SGU_BLOCK, W_A), lambda i: (i, 0)),
        out_shape=jax.ShapeDtypeStruct((T, d_model), BF16),
        compiler_params=_params(1))(z, ln_g, ln_b, w_s, b_t, gn)


def _sgu_backward(name, z, dy, ln_g, ln_b, w_s, b_t, gn):
    T = z.shape[0]
    W_A = ln_g.shape[1]
    G = W_A // GROUP_DIM

    def body(z_ref, dy_ref, lg_ref, lb_ref, w_ref, bt_ref, gn_ref,
             dz_ref, dlg_ref, dlb_ref, dw_ref, db_ref, dgn_ref):
        @pl.when(pl.program_id(0) == 0)
        def _():
            for r in (dlg_ref, dlb_ref, dw_ref, db_ref, dgn_ref):
                r[...] = jnp.zeros_like(r)

        mask = _chunk_mask()
        zu = z_ref[:, :W_A]
        zv = z_ref[:, W_A:]
        u = _gelu(zu)
        v = _gelu(zv)
        saved, cols = [], []
        for g in range(G):
            sl = slice(g * GROUP_DIM, (g + 1) * GROUP_DIM)
            wm = jnp.where(mask, w_ref[g], 0.0)
            vhat, rstd, vn, mixed = _sgu_group_forward(
                v[:, sl], lg_ref[:, sl], lb_ref[:, sl], wm.astype(BF16), bt_ref[:, g:g + 1])
            saved.append((wm, vhat, rstd, vn, mixed))
            cols.append(u[:, sl] * mixed)
        ya = jnp.concatenate(cols, axis=1)
        rstd_a = lax.rsqrt(jnp.mean(ya * ya, axis=-1, keepdims=True) + EPS)
        ya_hat = ya * rstd_a
        dyv = dy_ref[...].astype(F32)
        dgn_ref[...] += jnp.sum(dyv * ya_hat, axis=0, keepdims=True)
        t = dyv * gn_ref[...]
        dya = rstd_a * (t - ya_hat * jnp.mean(t * ya_hat, axis=-1, keepdims=True))
        du_cols, dv_cols, dlg_cols, dlb_cols = [], [], [], []
        for g in range(G):
            sl = slice(g * GROUP_DIM, (g + 1) * GROUP_DIM)
            wm, vhat, rstd, vn, mixed = saved[g]
            dya_g = dya[:, sl]
            du_cols.append(dya_g * mixed)
            dmix = dya_g * u[:, sl]
            dmix_bf = dmix.astype(BF16)
            db_ref[g] += jnp.sum(dmix, axis=1, keepdims=True)
            dw = lax.dot_general(dmix_bf, vn.astype(BF16), (((1,), (1,)), ((), ())),
                                 preferred_element_type=F32)
            dw_ref[g] += jnp.where(mask, dw, 0.0)
            dvn = jnp.dot(wm.T.astype(BF16), dmix_bf, preferred_element_type=F32)
            dlg_cols.append(jnp.sum(dvn * vhat, axis=0, keepdims=True))
            dlb_cols.append(jnp.sum(dvn, axis=0, keepdims=True))
            dvhat = dvn * lg_ref[:, sl]
            dv_cols.append(rstd * (dvhat - jnp.mean(dvhat, axis=-1, keepdims=True)
                                   - vhat * jnp.mean(dvhat * vhat, axis=-1, keepdims=True)))
        dlg_ref[...] += jnp.concatenate(dlg_cols, axis=1)
        dlb_ref[...] += jnp.concatenate(dlb_cols, axis=1)
        dz_ref[:, :W_A] = (jnp.concatenate(du_cols, axis=1) * _gelu_grad(zu)).astype(BF16)
        dz_ref[:, W_A:] = (jnp.concatenate(dv_cols, axis=1) * _gelu_grad(zv)).astype(BF16)

    vec = pl.BlockSpec((1, W_A), lambda i: (0, 0))
    wspec = pl.BlockSpec((G, SGU_BLOCK, SGU_BLOCK), lambda i: (0, 0, 0))
    return pl.pallas_call(
        body, name=name, grid=(T // SGU_BLOCK,),
        in_specs=[pl.BlockSpec((SGU_BLOCK, 2 * W_A), lambda i: (i, 0)),
                  pl.BlockSpec((SGU_BLOCK, W_A), lambda i: (i, 0)), vec, vec, wspec,
                  pl.BlockSpec((SGU_BLOCK, G), lambda i: (0, 0)), vec],
        out_specs=[pl.BlockSpec((SGU_BLOCK, 2 * W_A), lambda i: (i, 0)), vec, vec, wspec,
                   pl.BlockSpec((G, SGU_BLOCK, 1), lambda i: (0, 0, 0)), vec],
        out_shape=[jax.ShapeDtypeStruct((T, 2 * W_A), BF16), jax.ShapeDtypeStruct((1, W_A), F32),
                   jax.ShapeDtypeStruct((1, W_A), F32),
                   jax.ShapeDtypeStruct((G, SGU_BLOCK, SGU_BLOCK), F32),
                   jax.ShapeDtypeStruct((G, SGU_BLOCK, 1), F32),
                   jax.ShapeDtypeStruct((1, W_A), F32)],
        compiler_params=_params(1))(z, dy, ln_g, ln_b, w_s, b_t, gn)


def _split_dot(x, tri):
    hi = x.astype(BF16)
    lo = (x - hi.astype(F32)).astype(BF16)
    return (jnp.dot(hi, tri, preferred_element_type=F32)
            + jnp.dot(lo, tri, preferred_element_type=F32))


def _tri(n, rel):
    r = lax.broadcasted_iota(jnp.int32, (n, n), 0)
    c = lax.broadcasted_iota(jnp.int32, (n, n), 1)
    return rel(r, c).astype(BF16)


def _dot_nt(a, b):
    return lax.dot_general(a, b, (((1,), (1,)), ((), ())), preferred_element_type=F32)


def _dot_tn(a, b):
    return lax.dot_general(a, b, (((0,), (0,)), ((), ())), preferred_element_type=F32)


def _sb_scores(qs, kj, mask):
    zz = _dot_nt(qs, kj)
    log_beta = jnp.minimum(zz, 0.0) - jnp.log(1.0 + jnp.exp(-jnp.abs(zz)))
    log_1m = log_beta - zz
    if mask is not None:
        log_1m = jnp.where(mask, log_1m, 0.0)
    return log_beta, log_1m


def _masked(mask, x):
    return x if mask is None else jnp.where(mask, x, 0.0)


def _sb_tiles(T):
    tk = _pick(256, [T])
    tq = 2 * tk if T % (2 * tk) == 0 else tk
    return tq, tk


def _sb_cols(w_a, w_b):
    base = 2 * w_a // GROUP_DIM
    per = w_b // GROUP_DIM
    return base, base + per, base + 2 * per


def _sb_forward(name, z, w_a, w_b):
    T = z.shape[0]
    H = w_b // GROUP_DIM
    tq, tk = _sb_tiles(T)
    per = tq // tk
    qc, kc, vc = _sb_cols(w_a, w_b)
    scale = GROUP_DIM ** -0.5

    def body(q_ref, k_ref, v_ref, y_ref, tot_ref):
        i = pl.program_id(1)
        qs = (q_ref[...] * scale).astype(BF16)
        upper = _tri(tk, lambda r, c: r > c)
        ahead = (lax.broadcasted_iota(jnp.int32, (tq, tk), 1)
                 - lax.broadcasted_iota(jnp.int32, (tq, tk), 0))

        def step(j, carry, masked):
            acc, later = carry
            k0 = pl.multiple_of(j * tk, tk)
            kj = k_ref[pl.ds(k0, tk), :].astype(BF16)
            vj = v_ref[pl.ds(k0, tk), :].astype(BF16)
            mask = ahead < i * tq - k0 if masked else None
            log_beta, log_1m = _sb_scores(qs, kj, mask)
            rest = _split_dot(log_1m, upper) + later
            a = _masked(mask, jnp.exp(log_beta + rest))
            acc = acc + jnp.dot(a.astype(BF16), vj, preferred_element_type=F32)
            return acc, later + jnp.sum(log_1m, axis=1, keepdims=True)

        carry = (jnp.zeros((tq, GROUP_DIM), F32), jnp.zeros((tq, 1), F32))
        first = i * per
        for d in reversed(range(per)):
            carry = step(first + d, carry, True)
        acc, total = lax.fori_loop(0, first, lambda jj, c: step(first - 1 - jj, c, False), carry)
        y_ref[...] = acc
        tot_ref[...] = total

    return pl.pallas_call(
        body, name=name, grid=(H, T // tq),
        in_specs=[pl.BlockSpec((tq, GROUP_DIM), lambda h, i: (i, qc + h)),
                  pl.BlockSpec((T, GROUP_DIM), lambda h, i: (0, kc + h)),
                  pl.BlockSpec((T, GROUP_DIM), lambda h, i: (0, vc + h))],
        out_specs=[pl.BlockSpec((tq, GROUP_DIM), lambda h, i: (i, h)),
                   pl.BlockSpec((None, tq, 1), lambda h, i: (h, i, 0))],
        out_shape=[jax.ShapeDtypeStruct((T, w_b), F32), jax.ShapeDtypeStruct((H, T, 1), F32)],
        compiler_params=_params(2))(z, z, z)


def _sb_backward(name, z, do, total, w_a, w_b):
    T = z.shape[0]
    H = w_b // GROUP_DIM
    tq, tk = _sb_tiles(T)
    per = tq // tk
    qc, kc, vc = _sb_cols(w_a, w_b)
    scale = GROUP_DIM ** -0.5

    def body(q_ref, k_ref, v_ref, do_ref, tot_ref, dq_ref, dkv_ref):
        i = pl.program_id(1)

        @pl.when(i == 0)
        def _():
            dkv_ref[...] = jnp.zeros_like(dkv_ref)

        qs = (q_ref[...] * scale).astype(BF16)
        dob = do_ref[...].astype(BF16)
        upto = _tri(tk, lambda r, c: r <= c)
        before = _tri(tk, lambda r, c: r < c)
        ahead = (lax.broadcasted_iota(jnp.int32, (tq, tk), 1)
                 - lax.broadcasted_iota(jnp.int32, (tq, tk), 0))

        def step(j, carry, masked):
            dq, left, e_seen = carry
            k0 = pl.multiple_of(j * tk, tk)
            kj = k_ref[pl.ds(k0, tk), :].astype(BF16)
            vj = v_ref[pl.ds(k0, tk), :].astype(BF16)
            mask = ahead < i * tq - k0 if masked else None
            log_beta, log_1m = _sb_scores(qs, kj, mask)
            rest = left - _split_dot(log_1m, upto)
            a = _masked(mask, jnp.exp(log_beta + rest))
            e = a * _dot_nt(dob, vj)
            e_before = e_seen + jnp.dot(e.astype(BF16), before, preferred_element_type=F32)
            beta = jnp.exp(log_beta)
            dz = _masked(mask, e * (1.0 - beta) - beta * e_before).astype(BF16)
            dq = dq + jnp.dot(dz, kj, preferred_element_type=F32)
            dkv_ref[0, pl.ds(k0, tk), :] += _dot_tn(dz, qs)
            dkv_ref[1, pl.ds(k0, tk), :] += _dot_tn(a.astype(BF16), dob)
            return (dq, left - jnp.sum(log_1m, axis=1, keepdims=True),
                    e_seen + jnp.sum(e, axis=1, keepdims=True))

        first = i * per
        carry = (jnp.zeros((tq, GROUP_DIM), F32), tot_ref[...], jnp.zeros((tq, 1), F32))
        carry = lax.fori_loop(0, first, lambda j, c: step(j, c, False), carry)
        for d in range(per):
            carry = step(first + d, carry, True)
        dq_ref[...] = (carry[0] * scale).astype(BF16)

    return pl.pallas_call(
        body, name=name, grid=(H, T // tq),
        in_specs=[pl.BlockSpec((tq, GROUP_DIM), lambda h, i: (i, qc + h)),
                  pl.BlockSpec((T, GROUP_DIM), lambda h, i: (0, kc + h)),
                  pl.BlockSpec((T, GROUP_DIM), lambda h, i: (0, vc + h)),
                  pl.BlockSpec((tq, GROUP_DIM), lambda h, i: (i, h)),
                  pl.BlockSpec((None, tq, 1), lambda h, i: (h, i, 0))],
        out_specs=[pl.BlockSpec((tq, GROUP_DIM), lambda h, i: (i, h)),
                   pl.BlockSpec((2, T, GROUP_DIM), lambda h, i: (0, 0, h))],
        out_shape=[jax.ShapeDtypeStruct((T, w_b), BF16), jax.ShapeDtypeStruct((2, T, w_b), F32)],
        compiler_params=_params(2))(z, z, z, do, total)


def _softmax_rows(s):
    m = jnp.max(s, axis=-1, keepdims=True)
    p = jnp.exp(s - m)
    return p / jnp.sum(p, axis=-1, keepdims=True)


def _xattn_forward(name, q, kv):
    T, D = q.shape
    Nm = kv.shape[0]
    dh = D // X_HEADS
    tq = _pick(512, [T])

    def body(q_ref, k_ref, v_ref, o_ref):
        p = _softmax_rows(_dot_nt(q_ref[...], k_ref[...]))
        o_ref[...] = jnp.dot(p.astype(BF16), v_ref[...], preferred_element_type=F32).astype(BF16)

    return pl.pallas_call(
        body, name=name, grid=(T // tq, X_HEADS),
        in_specs=[pl.BlockSpec((tq, dh), lambda i, h: (i, h)),
                  pl.BlockSpec((Nm, dh), lambda i, h: (0, h)),
                  pl.BlockSpec((Nm, dh), lambda i, h: (0, X_HEADS + h))],
        out_specs=pl.BlockSpec((tq, dh), lambda i, h: (i, h)),
        out_shape=jax.ShapeDtypeStruct((T, D), BF16),
        compiler_params=_params(2))(q, kv, kv)


def _xattn_backward(name, q, kv, do):
    T, D = q.shape
    Nm = kv.shape[0]
    dh = D // X_HEADS
    tq = _pick(512, [T])
    scale = dh ** -0.5

    def body(q_ref, k_ref, v_ref, do_ref, dq_ref, dkv_ref):
        @pl.when(pl.program_id(1) == 0)
        def _():
            dkv_ref[...] = jnp.zeros_like(dkv_ref)

        qv, kk, vv, dov = q_ref[...], k_ref[...], v_ref[...], do_ref[...]
        p = _softmax_rows(_dot_nt(qv, kk))
        dp = _dot_nt(dov, vv)
        ds = (p * (dp - jnp.sum(dp * p, axis=-1, keepdims=True))).astype(BF16)
        dq_ref[...] = (jnp.dot(ds, kk, preferred_element_type=F32) * scale).astype(BF16)
        dkv_ref[0] += _dot_tn(ds, qv)
        dkv_ref[1] += _dot_tn(p.astype(BF16), dov)

    blk = pl.BlockSpec((tq, dh), lambda h, i: (i, h))
    return pl.pallas_call(
        body, name=name, grid=(X_HEADS, T // tq),
        in_specs=[blk, pl.BlockSpec((Nm, dh), lambda h, i: (0, h)),
                  pl.BlockSpec((Nm, dh), lambda h, i: (0, X_HEADS + h)), blk],
        out_specs=[blk, pl.BlockSpec((2, Nm, dh), lambda h, i: (0, 0, h))],
        out_shape=[jax.ShapeDtypeStruct((T, D), BF16), jax.ShapeDtypeStruct((2, Nm, D), F32)],
        compiler_params=_params(2))(q, kv, kv, do)


def _position():
    x, y, c = lax.axis_index("x"), lax.axis_index("y"), lax.axis_index("c")
    other_chips = [(1 - x, y), (x, 1 - y), (1 - x, 1 - y)]
    return x, y, c, other_chips


def _hbm_spec():
    return pl.BlockSpec(memory_space=pltpu.HBM)


def _sem_spec():
    return pl.BlockSpec(memory_space=pltpu.SEMAPHORE)


def _split_start(name, arrays, make_copies, n_sems, deps=()):
    n, d = len(arrays), len(deps)

    def body(*refs):
        ins = refs[:n]
        send_sems, recv_sems = refs[n + d], refs[n + d + 1]
        token = refs[-1]
        for cp in make_copies(ins, send_sems, recv_sems):
            cp.start()
        token[...] = jnp.zeros_like(token)

    res = pl.pallas_call(
        body, name=name,
        out_shape=(pltpu.SemaphoreType.DMA((n_sems,)), pltpu.SemaphoreType.DMA((n_sems,)),
                   *[pltpu.HBM(a.shape, a.dtype) for a in arrays],
                   jax.ShapeDtypeStruct((SUBLANE, LANE), F32)),
        in_specs=[_hbm_spec()] * n + [_any_spec()] * d,
        out_specs=(_sem_spec(), _sem_spec(), *[_hbm_spec()] * n,
                   pl.BlockSpec(memory_space=pltpu.VMEM)),
        input_output_aliases={i: 2 + i for i in range(n)},
        compiler_params=pltpu.CompilerParams(
            has_side_effects=pltpu.SideEffectType.DATAFLOW_SIDE_EFFECTING),
    )(*[pltpu.with_memory_space_constraint(a, pltpu.HBM) for a in arrays], *deps)
    return res[0], res[1], list(res[2:2 + n]), res[-1]


def _split_wait(name, arrays, send_sems, recv_sems, after, make_copies):
    n = len(arrays)
    after = list(after) if isinstance(after, (list, tuple)) else [after]

    def body(*refs):
        ins = refs[:n]
        send_ref, recv_ref = refs[n], refs[n + 1]
        for cp in make_copies(ins, send_ref, recv_ref):
            cp.wait_send()
            cp.wait_recv()

    return pl.pallas_call(
        body, name=name,
        out_shape=tuple(pltpu.HBM(a.shape, a.dtype) for a in arrays),
        in_specs=[_hbm_spec()] * n + [_sem_spec(), _sem_spec()] + [_any_spec()] * len(after),
        out_specs=tuple(_hbm_spec() for _ in arrays),
        input_output_aliases={i: i for i in range(n)},
        compiler_params=pltpu.CompilerParams(
            has_side_effects=pltpu.SideEffectType.DATAFLOW_SIDE_EFFECTING),
    )(*arrays, send_sems, recv_sems, *after)


def _gather_copies(refs, send_sems, recv_sems):
    x, y, c, chips = _position()
    me = 2 * x + y
    copies = []
    for i, ref in enumerate(refs):
        rows = ref.shape[1] // 2
        piece = ref.at[me, pl.ds(c * rows, rows), :]
        for j, (px, py) in enumerate(chips):
            copies.append(pltpu.make_async_remote_copy(
                src_ref=piece, dst_ref=piece, send_sem=send_sems.at[3 * i + j],
                recv_sem=recv_sems.at[3 * i + j], device_id=(px, py, c), device_id_type=MESH))
    return copies


def _near_copies(refs, send_sems, recv_sems):
    x, y, c, chips = _position()
    me = 2 * x + y
    copies = []
    for i, ref in enumerate(refs):
        rows = ref.shape[1] // 2
        piece = ref.at[me, pl.ds(c * rows, rows), :]
        for j, (px, py) in enumerate(chips[:2]):
            copies.append(pltpu.make_async_remote_copy(
                src_ref=piece, dst_ref=piece, send_sem=send_sems.at[2 * i + j],
                recv_sem=recv_sems.at[2 * i + j], device_id=(px, py, c), device_id_type=MESH))
    return copies


def _relay_copies(refs, send_sems, recv_sems):
    x, y, c, chips = _position()
    copies = []
    for i, ref in enumerate(refs):
        rows = ref.shape[1] // 4
        for j, (px, py) in enumerate(chips[:2]):
            ox, oy = chips[1 - j]
            piece = ref.at[2 * ox + oy, pl.ds((2 * c + j) * rows, rows), :]
            copies.append(pltpu.make_async_remote_copy(
                src_ref=piece, dst_ref=piece, send_sem=send_sems.at[2 * i + j],
                recv_sem=recv_sems.at[2 * i + j], device_id=(px, py, c), device_id_type=MESH))
    return copies


def _share_copies(refs, send_sems, recv_sems):
    x, y, c, _ = _position()
    copies = []
    for i, ref in enumerate(refs):
        rows = ref.shape[0] // 2
        mine = ref.at[pl.ds(c * rows, rows), :]
        copies.append(pltpu.make_async_remote_copy(
            src_ref=mine, dst_ref=mine, send_sem=send_sems.at[i], recv_sem=recv_sems.at[i],
            device_id=(x, y, 1 - c), device_id_type=MESH))
    return copies


def _scatter_copies(refs, send_sems, recv_sems):
    x, y, c, chips = _position()
    n = len(refs) // 2
    copies = []
    for i in range(n):
        for j, (px, py) in enumerate(chips):
            copies.append(pltpu.make_async_remote_copy(
                src_ref=refs[i].at[2 * px + py], dst_ref=refs[n + i].at[j],
                send_sem=send_sems.at[3 * i + j], recv_sem=recv_sems.at[3 * i + j],
                device_id=(px, py, c), device_id_type=MESH))
    return copies


def _cast_own(name, place, shard):
    rows, cols = shard.shape
    tr = _block_rows(rows, cols)

    def body(place_ref, w_ref, o_ref):
        o_ref[...] = w_ref[...].astype(BF16)

    grid_spec = pltpu.PrefetchScalarGridSpec(
        num_scalar_prefetch=1, grid=(rows // tr,),
        in_specs=[pl.BlockSpec((tr, cols), lambda r, pr: (r, 0))],
        out_specs=pl.BlockSpec((None, tr, cols), lambda r, pr: (pr[0], r, 0)))
    return pl.pallas_call(
        body, name=name, grid_spec=grid_spec,
        out_shape=jax.ShapeDtypeStruct((N_CHIPS, rows, cols), BF16),
        compiler_params=_params(1))(place, shard)


def _forward_to_sibling(name, arrays, deps=()):
    n = len(arrays)

    def body(*refs):
        ins = refs[:n]
        send_sems, recv_sems = refs[-2:]
        x, y, c, chips = _position()
        sends = []
        for i in range(n):
            rows = ins[i].shape[1] // 2
            for j, (px, py) in enumerate(chips):
                piece = ins[i].at[2 * px + py, pl.ds(c * rows, rows), :]
                cp = pltpu.make_async_remote_copy(
                    src_ref=piece, dst_ref=piece, send_sem=send_sems.at[i, j],
                    recv_sem=recv_sems.at[i, j], device_id=(x, y, 1 - c), device_id_type=MESH)
                cp.start()
                sends.append(cp)
        for i in range(n):
            rows = ins[i].shape[1] // 2
            for j, (px, py) in enumerate(chips):
                piece = ins[i].at[2 * px + py, pl.ds((1 - c) * rows, rows), :]
                pltpu.make_async_remote_copy(
                    src_ref=piece, dst_ref=piece, send_sem=send_sems.at[i, j],
                    recv_sem=recv_sems.at[i, j], device_id=(x, y, 1 - c),
                    device_id_type=MESH).wait_recv()
        for cp in sends:
            cp.wait_send()

    return pl.pallas_call(
        body, name=name,
        in_specs=[_any_spec()] * (n + len(deps)), out_specs=[_any_spec()] * n,
        out_shape=[jax.ShapeDtypeStruct(a.shape, a.dtype) for a in arrays],
        input_output_aliases={i: i for i in range(n)},
        scratch_shapes=[pltpu.SemaphoreType.DMA((n, 3))] * 2,
    )(*arrays, *deps)


def _swap_copies(refs, send_sems, recv_sems):
    x, y, c, _ = _position()
    n = len(refs) // 2
    copies = []
    for i in range(n):
        rows = refs[i].shape[1] // 2
        copies.append(pltpu.make_async_remote_copy(
            src_ref=refs[i].at[:, pl.ds((1 - c) * rows, rows), :], dst_ref=refs[n + i],
            send_sem=send_sems.at[i], recv_sem=recv_sems.at[i],
            device_id=(x, y, 1 - c), device_id_type=MESH))
    return copies


def _small_copies(refs, send_sems, recv_sems):
    packed, slots = refs
    x, y, c, _ = _position()
    me = 4 * x + 2 * y + c
    copies = []
    for r in range(1, N_DEV):
        peer = (x ^ ((r >> 2) & 1), y ^ ((r >> 1) & 1), c ^ (r & 1))
        copies.append(pltpu.make_async_remote_copy(
            src_ref=packed, dst_ref=slots.at[me], send_sem=send_sems.at[r - 1],
            recv_sem=recv_sems.at[r - 1], device_id=peer, device_id_type=MESH))
    return copies


def _block_rows(rows, cols, itemsize=4, target=1 << 20):
    return _pick(max(BF16_ROWS, target // (cols * itemsize)), [rows], unit=BF16_ROWS)


def _pair_sum(name, place, grad, received):
    P, rows, cols = received.shape
    tr = _block_rows(rows, cols, itemsize=2, target=2 << 20)
    nb = rows // tr

    def body(place_ref, g_ref, r_ref, o_ref):
        o_ref[...] = (g_ref[...].astype(F32) + r_ref[...].astype(F32)).astype(BF16)

    grid_spec = pltpu.PrefetchScalarGridSpec(
        num_scalar_prefetch=1, grid=(P, nb),
        in_specs=[pl.BlockSpec((None, tr, cols), lambda p, r, pr: (p, pr[1] * nb + r, 0)),
                  pl.BlockSpec((None, tr, cols), lambda p, r, pr: (p, r, 0))],
        out_specs=pl.BlockSpec((None, tr, cols), lambda p, r, pr: (p, r, 0)))
    return pl.pallas_call(
        body, name=name, grid_spec=grid_spec,
        out_shape=jax.ShapeDtypeStruct(received.shape, BF16),
        compiler_params=_params(2))(place, grad, received)


def _final_sum(name, place, grad, received, from_chips):
    _, rows, cols = received.shape
    tr = _block_rows(rows, cols)
    nb = rows // tr

    def body(place_ref, g_ref, r_ref, c_ref, o_ref):
        acc = g_ref[...].astype(F32) + r_ref[...].astype(F32)
        for j in range(3):
            acc = acc + c_ref[j].astype(F32)
        o_ref[...] = acc

    grid_spec = pltpu.PrefetchScalarGridSpec(
        num_scalar_prefetch=1, grid=(nb,),
        in_specs=[pl.BlockSpec((None, tr, cols), lambda r, pr: (pr[0], pr[1] * nb + r, 0)),
                  pl.BlockSpec((None, tr, cols), lambda r, pr: (pr[0], r, 0)),
                  pl.BlockSpec((3, tr, cols), lambda r, pr: (0, r, 0))],
        out_specs=pl.BlockSpec((tr, cols), lambda r, pr: (pr[1] * nb + r, 0)))
    return pl.pallas_call(
        body, name=name, grid_spec=grid_spec,
        out_shape=jax.ShapeDtypeStruct((2 * rows, cols), F32),
        compiler_params=_params(1))(place, grad, received, from_chips)


def _sum_devices(name, me, gathered, own):
    n_dev, rows, cols = gathered.shape
    tr = _pick(256, [rows])

    def body(me_ref, g_ref, own_ref, o_ref):
        term = lambda d: jnp.where(me_ref[0] == d, own_ref[...], g_ref[d])
        acc = term(0)
        for d in range(1, n_dev):
            acc = acc + term(d)
        o_ref[...] = acc

    grid_spec = pltpu.PrefetchScalarGridSpec(
        num_scalar_prefetch=1, grid=(rows // tr,),
        in_specs=[pl.BlockSpec((n_dev, tr, cols), lambda r, me_ref: (0, r, 0)),
                  pl.BlockSpec((tr, cols), lambda r, me_ref: (r, 0))],
        out_specs=pl.BlockSpec((tr, cols), lambda r, me_ref: (r, 0)))
    return pl.pallas_call(
        body, name=name, grid_spec=grid_spec,
        out_shape=jax.ShapeDtypeStruct((rows, cols), F32),
        compiler_params=_params(1))(me, gathered, own)


def _adamw(name, w, g, m, v):
    rows, cols = w.shape
    tr = _block_rows(rows, cols)
    c1 = 1.0 / (1.0 - ADAM_B1 ** ADAM_STEP)
    c2 = 1.0 / (1.0 - ADAM_B2 ** ADAM_STEP)

    def body(w_ref, g_ref, m_ref, v_ref, go_ref, d_ref, nm_ref, nv_ref):
        gv = g_ref[...]
        go_ref[...] = gv
        nm = ADAM_B1 * m_ref[...] + (1.0 - ADAM_B1) * gv
        nv = ADAM_B2 * v_ref[...] + (1.0 - ADAM_B2) * (gv * gv)
        nm_ref[...] = nm
        nv_ref[...] = nv
        d_ref[...] = -ADAM_LR * ((nm * c1) / (jnp.sqrt(nv * c2) + ADAM_EPS) + ADAM_WD * w_ref[...])

    blk = pl.BlockSpec((tr, cols), lambda r: (r, 0))
    shape = jax.ShapeDtypeStruct((rows, cols), F32)
    return pl.pallas_call(
        body, name=name, grid=(rows // tr,), in_specs=[blk] * 4, out_specs=[blk] * 4,
        out_shape=[shape] * 4, compiler_params=_params(1))(w, g, m, v)


BIG = ("ffn1_w_in", "ffn1_w_out", "w_mix_in", "w_mix_out", "w_cq", "w_ckv", "w_co",
       "ffn2_w_in", "ffn2_w_out")
BIG_KIND = {"ffn1_w_in": "c", "ffn1_w_out": "r", "w_mix_in": "c", "w_mix_out": "r", "w_cq": "r",
            "w_ckv": "c", "w_co": "r", "ffn2_w_in": "c", "ffn2_w_out": "r"}
GATHER_GROUPS = (("ffn1_in", ("ffn1_w_in",)), ("ffn1_out", ("ffn1_w_out",)),
                 ("mix", ("w_mix_in", "w_mix_out")), ("cross", ("w_cq", "w_ckv", "w_co")),
                 ("ffn2_in", ("ffn2_w_in",)), ("ffn2_out", ("ffn2_w_out",)))
GATHER_AFTER = (("ffn1_in", None), ("ffn1_out", "ffn1_in"), ("mix", "ffn1_out"), ("cross", "mix"),
                ("ffn2_in", "mix"), ("ffn2_out", "cross"))
RELAYED = ("ffn1_in",)
TAIL_STAGES = (("sum", "ffn2"), ("sum", "cross"), ("sum", "mix"), ("update", "ffn2"),
               ("update", "cross"), ("sum", "ffn1"), ("update", "mix"), ("update", "ffn1"))
SMALL = ("ffn1_norm", "mix_norm", "ln_v_gain", "ln_v_bias", "spatial_w", "spatial_b", "gnorm_a",
         "gnorm_b", "cross_norm", "mem_norm", "ffn2_norm", "final_norm")
WEIGHTS = ("ffn1_norm", "ffn1_w_in", "ffn1_w_out", "mix_norm", "w_mix_in", "ln_v_gain",
           "ln_v_bias", "spatial_w", "spatial_b", "gnorm_a", "gnorm_b", "w_mix_out", "cross_norm",
           "mem_norm", "w_cq", "w_ckv", "w_co", "ffn2_norm", "ffn2_w_in", "ffn2_w_out",
           "final_norm")


def _pack(arrays):
    return jnp.concatenate([a.reshape(-1, LANE) for a in arrays], axis=0)


def _unpack(packed, like):
    out, row = [], 0
    for a in like:
        rows = a.size // LANE
        out.append(packed[row:row + rows].reshape(a.shape))
        row += rows
    return out


def _local_step(x, mem, target, small, weights_of, start_tokens, grads_ready, grads_flush):
    T, D = x.shape
    vec = lambda name: small[name].reshape(1, -1)
    w_a = small["ln_v_gain"].size
    w_b = small["gnorm_b"].size
    G = w_a // GROUP_DIM
    w_s = small["spatial_w"].reshape(G, SGU_BLOCK, SGU_BLOCK)
    b_t = small["spatial_b"].reshape(G, SGU_BLOCK).T

    h1, ffn1_saved = _ffn_forward("ffn1", x, vec("ffn1_norm"), weights_of, deps=start_tokens)
    n2 = _rmsnorm_fwd("mix_norm", h1, vec("mix_norm"))
    big = weights_of("mix", n2)
    (z,) = _matmul("mix_in", Mat(n2), big["w_mix_in"], "nn", [("c", 1, F32)], tm=2048, tn=256)
    z = z[0]
    y = _sgu_forward("sgu", z, vec("ln_v_gain"), vec("ln_v_bias"), w_s, b_t, vec("gnorm_a"), D)
    yb, sb_total = _sb_forward("stickbreak", z, w_a, w_b)
    y = _rmsnorm_fwd("gnorm_b", yb, vec("gnorm_b"), into=y, col=w_a // w_b)

    def add_res(acc, ex, out):
        out[0][...] = ex[0][...] + acc

    (h2,) = _matmul("mix_out", Mat(y), big["w_mix_out"], "nn", [("c", 1, F32)],
                    tm=1024, tn=1024, extras=[Mat(h1)], epi=add_res)
    h2 = h2[0]
    n3 = _rmsnorm_fwd("cross_norm", h2, vec("cross_norm"))
    memn = _rmsnorm_fwd("mem_norm", mem, vec("mem_norm"))
    big.update(weights_of("cross", n3))
    x_scale = (D // X_HEADS) ** -0.5

    def scaled(acc, ex, out):
        out[0][...] = (acc * x_scale).astype(BF16)

    (q,) = _matmul("cross_q", Mat(n3), big["w_cq"], "nn", [("c", 1, BF16)],
                   tm=1024, tn=1024, epi=scaled)
    (kv,) = _matmul("cross_kv", Mat(memn), big["w_ckv"], "nn", [("c", 1, BF16)], tm=256, tn=1024)
    q, kv = q[0], kv[0]
    o = _xattn_forward("cross_attn", q, kv)
    (h3,) = _matmul("cross_out", Mat(o), big["w_co"], "nn", [("c", 1, F32)],
                    tm=1024, tn=1024, extras=[Mat(h2)], epi=add_res)
    h3 = h3[0]
    h4, ffn2_saved = _ffn_forward("ffn2", h3, vec("ffn2_norm"), weights_of)

    gs = {}
    loss_tile, dh4, dh4_bf, gs["final_norm"] = _loss_head("loss_head", h4, vec("final_norm"), target)
    dh3, dh3_bf, gs["ffn2_norm"] = _ffn_backward(
        "ffn2", h3, vec("ffn2_norm"), ffn2_saved, dh4, dh4_bf, grads_ready, grads_flush)

    (do,) = _matmul("cross_do", Mat(dh3_bf), big["w_co"], "nt", [("c", 1, BF16)], tm=1024, tn=512)
    (dw_co,) = _matmul("cross_dwo", Mat(o), Mat(dh3_bf), "tn", [("r", N_CHIPS, BF16)],
                       tm=512, tn=1024)
    dq, dkv = _xattn_backward("cross_attn_bwd", q, kv, do[0])
    (dw_cq,) = _matmul("cross_dwq", Mat(n3), Mat(dq), "tn", [("r", N_CHIPS, BF16)],
                       tm=512, tn=1024)
    (dw_ckv,) = _matmul("cross_dwkv", Mat(memn), Mat(dkv), "tn", [("c", N_CHIPS, BF16)],
                        tm=1024, tn=1024)
    token = grads_ready("cross", {"w_cq": dw_cq, "w_ckv": dw_ckv, "w_co": dw_co})
    dq = _tie("cross_dq_after_swap", dq, [token])
    (dn3,) = _matmul("cross_dn", Mat(dq), big["w_cq"], "nt", [("c", 1, F32)], tm=1024, tn=512)
    (dmemn,) = _matmul("cross_dmem", Mat(dkv), big["w_ckv"], "nt", [("c", 1, F32)],
                       tm=256, tn=1024, tk=1024)
    (gs["mem_norm"],) = _rmsnorm_bwd("mem_dnorm", mem, vec("mem_norm"), dmemn[0], want_dx=False)
    dn3 = _tie("cross_dn_after_scatter", dn3, [grads_flush("cross", gs["mem_norm"])])
    dh2, dh2_bf, gs["cross_norm"] = _rmsnorm_bwd("cross_dnorm", h2, vec("cross_norm"), dn3[0],
                                                 dres=dh3)

    (dy,) = _matmul("mix_dy", Mat(dh2_bf), big["w_mix_out"], "nt", [("c", 1, F32)], tm=1024, tn=512)
    dy = dy[0]
    (dw_mix_out,) = _matmul("mix_dwout", Mat(y), Mat(dh2_bf), "tn", [("r", N_CHIPS, BF16)],
                            tm=512, tn=1024)
    dza, gs["ln_v_gain"], gs["ln_v_bias"], gs["spatial_w"], db, gs["gnorm_a"] = _sgu_backward(
        "sgu_bwd", z, dy, vec("ln_v_gain"), vec("ln_v_bias"), w_s, b_t, vec("gnorm_a"))
    gs["spatial_b"] = db.reshape(G, SGU_BLOCK)
    dob, gs["gnorm_b"] = _rmsnorm_bwd("gnorm_b_bwd", yb, vec("gnorm_b"), dy, dn_col=w_a // w_b,
                                      want_bf16=False)
    dqb, dkvb = _sb_backward("stickbreak_bwd", z, dob, sb_total, w_a, w_b)
    dz = jnp.concatenate([dza, dqb, dkvb[0].astype(BF16), dkvb[1].astype(BF16)], axis=1)
    (dw_mix_in,) = _matmul("mix_dwin", Mat(n2), Mat(dz), "tn", [("c", N_CHIPS, BF16)],
                           tm=1024, tn=1280)
    token = grads_ready("mix", {"w_mix_in": dw_mix_in, "w_mix_out": dw_mix_out})
    dz = _tie("mix_dz_after_swap", dz, [token])
    (dn2,) = _matmul("mix_dn", Mat(dz), big["w_mix_in"], "nt", [("c", 1, F32)],
                     tm=1024, tn=1024, tk=1280)
    dn2 = _tie("mix_dn_after_scatter", dn2, [grads_flush("mix", dn2)])
    dh1, dh1_bf, gs["mix_norm"] = _rmsnorm_bwd("mix_dnorm", h1, vec("mix_norm"), dn2[0], dres=dh2)

    dx, _, gs["ffn1_norm"] = _ffn_backward(
        "ffn1", x, vec("ffn1_norm"), ffn1_saved, dh1, dh1_bf, grads_ready, grads_flush)
    gs = {k: g.reshape(small[k].shape) for k, g in gs.items()}
    return loss_tile, dx, gs


def kernel(x, mem, ffn1_norm, ffn1_w_in, ffn1_w_out, mix_norm, w_mix_in, ln_v_gain, ln_v_bias, spatial_w, spatial_b, gnorm_a, gnorm_b, w_mix_out, cross_norm, mem_norm, w_cq, w_ckv, w_co, ffn2_norm, ffn2_w_in, ffn2_w_out, final_norm, loss_target, m_ffn1_norm, m_ffn1_w_in, m_ffn1_w_out, m_mix_norm, m_w_mix_in, m_ln_v_gain, m_ln_v_bias, m_spatial_w, m_spatial_b, m_gnorm_a, m_gnorm_b, m_w_mix_out, m_cross_norm, m_mem_norm, m_w_cq, m_w_ckv, m_w_co, m_ffn2_norm, m_ffn2_w_in, m_ffn2_w_out, m_final_norm, v_ffn1_norm, v_ffn1_w_in, v_ffn1_w_out, v_mix_norm, v_w_mix_in, v_ln_v_gain, v_ln_v_bias, v_spatial_w, v_spatial_b, v_gnorm_a, v_gnorm_b, v_w_mix_out, v_cross_norm, v_mem_norm, v_w_cq, v_w_ckv, v_w_co, v_ffn2_norm, v_ffn2_w_in, v_ffn2_w_out, v_final_norm):
    given = dict(locals())
    w = {k: given[k] for k in WEIGHTS}
    m = {k: given["m_" + k] for k in WEIGHTS}
    v = {k: given["v_" + k] for k in WEIGHTS}

    cx, cy, cc = lax.axis_index("x"), lax.axis_index("y"), lax.axis_index("c")
    place = jnp.stack([2 * cx + cy, cc]).astype(jnp.int32)

    names_of = dict(GATHER_GROUPS)
    own = {g: [_cast_own(f"cast_{k}", place, w[k][0]) for k in names] for g, names in GATHER_GROUPS}
    gathers = {}

    def start_gather(group, deps):
        first_hop = _near_copies if group in RELAYED else _gather_copies
        n_sems = (2 if group in RELAYED else 3) * len(own[group])
        send, recv, arrays, token = _split_start(f"gather_start_{group}", own[group],
                                                 first_hop, n_sems, deps)
        gathers[group] = (send, recv, arrays)
        return token

    start_tokens = [start_gather(g, ()) for g, after in GATHER_AFTER if after is None]
    start_tokens += [a for g, after in GATHER_AFTER if after is not None for a in own[g]]

    def weights_of(group, after):
        send, recv, arrays = gathers[group]
        if group in RELAYED:
            arrays = _split_wait(f"gather_wait_{group}", arrays, send, recv, after, _near_copies)
            send, recv, arrays, _ = _split_start(f"gather_relay_{group}", list(arrays),
                                                 _relay_copies, 2 * len(arrays))
            arrays = _split_wait(f"gather_relay_wait_{group}", arrays, send, recv, after,
                                 _relay_copies)
        else:
            arrays = _split_wait(f"gather_wait_{group}", arrays, send, recv, after, _gather_copies)
        tokens = [start_gather(g, (arrays[0],)) for g, a in GATHER_AFTER if a == group]
        arrays = _forward_to_sibling(f"gather_forward_{group}", list(arrays), tokens)
        return {k: Mat(a, BIG_KIND[k]) for k, a in zip(names_of[group], arrays)}

    swaps, scatters = {}, {}

    def grads_ready(group, partial):
        names = list(partial)
        grads_ = [partial[k] for k in names]
        lands = [lax.empty((g.shape[0], g.shape[1] // 2, g.shape[2]), g.dtype) for g in grads_]
        send, recv, arrays, token = _split_start(f"swap_start_{group}", grads_ + lands,
                                                 _swap_copies, len(names))
        swaps[group] = (names, send, recv, arrays)
        return token

    def grads_flush(group, after):
        names, send, recv, arrays = swaps[group]
        arrays = _split_wait(f"swap_wait_{group}", arrays, send, recv, after, _swap_copies)
        grads_, from_sibling = arrays[:len(names)], arrays[len(names):]
        sums = [_pair_sum(f"pair_sum_{k}", place, g, r)
                for k, g, r in zip(names, grads_, from_sibling)]
        lands = [lax.empty((3,) + s.shape[1:], s.dtype) for s in sums]
        send, recv, arrays, token = _split_start(f"scatter_start_{group}", sums + lands,
                                                 _scatter_copies, 3 * len(names))
        scatters[group] = (names, grads_, from_sibling, send, recv, arrays)
        return token

    small = {k: w[k] for k in SMALL}
    loss_tile, grad_x, gs = _local_step(x[0], mem[0], loss_target[0], small, weights_of,
                                        start_tokens, grads_ready, grads_flush)

    packed = _pack([gs[k] for k in SMALL] + [loss_tile])
    slots = jnp.zeros((N_DEV,) + packed.shape, packed.dtype)
    small_send, small_recv, small_arrays, _ = _split_start(
        "small_start", [packed, slots], _small_copies, N_DEV - 1)

    grad, delta, new_m, new_v = {}, {}, {}, {}
    shares = {}
    after = [grad_x]
    for stage, group in TAIL_STAGES:
        if stage == "sum":
            names, grads_, from_sibling, send, recv, arrays = scatters[group]
            arrays = _split_wait(f"scatter_wait_{group}", arrays, send, recv, after,
                                 _scatter_copies)
            from_chips = arrays[len(names):]
            shards = [_final_sum(f"final_sum_{k}", place, g, r, f)
                      for k, g, r, f in zip(names, grads_, from_sibling, from_chips)]
            send, recv, shards, token = _split_start(f"share_start_{group}", shards,
                                                     _share_copies, len(names))
            shares[group] = (names, send, recv, shards)
            after = [token]
        else:
            names, send, recv, shards = shares[group]
            shards = _split_wait(f"share_wait_{group}", shards, send, recv, after, _share_copies)
            after = []
            for k, g_ in zip(names, shards):
                g_, d_, m_, v_ = _adamw(f"adamw_{k}", w[k][0], g_, m[k][0], v[k][0])
                grad[k], delta[k], new_m[k], new_v[k] = g_[None], d_[None], m_[None], v_[None]
                after.append(v_)

    packed, slots = _split_wait("small_wait", small_arrays, small_send, small_recv, after,
                                _small_copies)
    me = (4 * cx + 2 * cy + cc).astype(jnp.int32).reshape(1)
    total = _sum_devices("sum_small", me, slots, packed)
    n_small = total.shape[0] - SUBLANE
    loss = total[n_small, 0]
    small_g = total[:n_small]
    g_s, d_s, m_s, v_s = _adamw("adamw_small", _pack([w[k] for k in SMALL]), small_g,
                                _pack([m[k] for k in SMALL]), _pack([v[k] for k in SMALL]))
    like = [w[k] for k in SMALL]
    for k, g_, d_, m_, v_ in zip(SMALL, _unpack(g_s, like), _unpack(d_s, like),
                                 _unpack(m_s, like), _unpack(v_s, like)):
        grad[k], delta[k], new_m[k], new_v[k] = g_, d_, m_, v_

    return (loss, grad_x[None], *[grad[k] for k in WEIGHTS], *[delta[k] for k in WEIGHTS],
            *[new_m[k] for k in WEIGHTS], *[new_v[k] for k in WEIGHTS])
```

```python
import functools
import math

import jax
import jax.numpy as jnp
from jax import lax
from jax.experimental import pallas as pl
from jax.experimental.pallas import tpu as pltpu

F32 = jnp.float32
BF16 = jnp.bfloat16
MESH = pl.DeviceIdType.MESH

EPS = 1e-6
CHUNK = 64
SGU_BLOCK = 128
GROUP_DIM = 128
X_HEADS = 4
N_CHIPS = 4
N_DEV = 8
LANE = 128
SUBLANE = 8
BF16_ROWS = 16

ADAM_LR = 0.001
ADAM_B1 = 0.9
ADAM_B2 = 0.999
ADAM_EPS = 1e-08
ADAM_WD = 0.01
ADAM_STEP = 10

V7X_VMEM_BYTES = 64 << 20
VMEM_LIMIT = V7X_VMEM_BYTES - (8 << 20)


def _params(n_grid):
    return pltpu.CompilerParams(dimension_semantics=("arbitrary",) * n_grid,
                                vmem_limit_bytes=VMEM_LIMIT)


def _pick(pref, dims, unit=None):
    g = functools.reduce(math.gcd, dims)
    if unit is None:
        unit = LANE if g % LANE == 0 else SUBLANE
    cands = [d for d in range(unit, g + 1, unit) if g % d == 0] or [g]
    return min(cands, key=lambda d: abs(math.log(d / pref)))


def _any_spec():
    return pl.BlockSpec(memory_space=pl.ANY)


class Mat:
    def __init__(self, arr, kind="c"):
        if arr.ndim == 2:
            arr = arr[None]
        self.arr, self.kind = arr, kind
        self.P, self.prow, self.pcol = arr.shape
        self.rows = self.prow * (self.P if kind == "r" else 1)
        self.cols = self.pcol * (self.P if kind == "c" else 1)
        self.dtype = arr.dtype

    def spec(self, tr, tc, rc_fn):
        if self.kind == "c":
            per = self.pcol // tc
            assert per * tc == self.pcol, (self.pcol, tc)

            def imap(*g):
                i, j = rc_fn(*g)
                return (j // per, i, j % per)
        else:
            per = self.prow // tr
            assert per * tr == self.prow, (self.prow, tr)

            def imap(*g):
                i, j = rc_fn(*g)
                return (i // per, i % per, j)
        return pl.BlockSpec((None, tr, tc), imap)

    def two_d(self):
        assert self.P == 1
        return self.arr[0]


def _out_mat(kind, P, rows, cols, dtype):
    shape = (P, rows, cols // P) if kind == "c" else (P, rows // P, cols)
    return jax.ShapeDtypeStruct(shape, dtype)


def _matmul(name, A, B, mode, outs, *, tm=1024, tn=1024, tk=2048, extras=(), epi=None):
    if mode == "nn":
        M, K, N = A.rows, A.cols, B.cols
        assert B.rows == K
    elif mode == "nt":
        M, K, N = A.rows, A.cols, B.rows
        assert B.cols == K
    else:
        K, M, N = A.rows, A.cols, B.cols
        assert B.rows == K
    mdims, ndims, kdims = [M], [N], [K]
    whole_b = mode == "nn" and B.kind == "r" and B.P > 1 and K <= tk
    if whole_b:
        kdims.append(A.pcol)
        ndims.append(B.pcol)
    elif mode == "tn":
        assert A.kind == "c" and B.kind == "c"
        mdims.append(A.pcol)
        ndims.append(B.pcol)
    else:
        (mdims if A.kind == "r" else kdims).append(A.prow if A.kind == "r" else A.pcol)
        if mode == "nn":
            (kdims if B.kind == "r" else ndims).append(B.prow if B.kind == "r" else B.pcol)
        else:
            (ndims if B.kind == "r" else kdims).append(B.prow if B.kind == "r" else B.pcol)
    for o in list(outs) + list(extras):
        if isinstance(o, Mat):
            (mdims if o.kind == "r" else ndims).append(o.prow if o.kind == "r" else o.pcol)
        elif isinstance(o[0], str):
            (mdims if o[0] == "r" else ndims).append((M if o[0] == "r" else N) // o[1])
    tm, tn = _pick(tm, mdims), _pick(tn, ndims)
    tk = K if mode == "tn" else _pick(tk, kdims)
    nk = K // tk
    grid = (M // tm, N // tn, nk)

    if mode == "tn":
        a_spec = A.spec(K, tm, lambda m, n, k: (0, m))
        b_spec = B.spec(K, tn, lambda m, n, k: (0, n))
    else:
        a_spec = A.spec(tm, tk, lambda m, n, k: (m, k))
        if whole_b:
            b_spec = pl.BlockSpec((B.P, B.prow, tn), lambda m, n, k: (0, 0, n))
        elif mode == "nn":
            b_spec = B.spec(tk, tn, lambda m, n, k: (k, n))
        else:
            b_spec = B.spec(tn, tk, lambda m, n, k: (n, k))

    def mn_spec(o):
        if isinstance(o, Mat):
            return o.spec(tm, tn, lambda m, n, k: (m, n))
        if isinstance(o[0], str):
            kind, P = o[0], o[1]
            fake = Mat.__new__(Mat)
            fake.kind, fake.P = kind, P
            fake.prow = M // P if kind == "r" else M
            fake.pcol = N // P if kind == "c" else N
            return Mat.spec(fake, tm, tn, lambda m, n, k: (m, n))
        return o[1](tm, tn)

    out_shapes = tuple(_out_mat(o[0], o[1], M, N, o[2]) if isinstance(o[0], str) else o[0]
                       for o in outs)
    out_specs = tuple(mn_spec(o) for o in outs)
    extra_arrays = tuple(e.arr if isinstance(e, Mat) else e[0] for e in extras)
    extra_specs = tuple(mn_spec(e) for e in extras)
    n_ex, n_out = len(extras), len(outs)
    tt = _pick(256, [tm])
    dims = (((1,), (1 if mode == "nt" else 0,)), ((), ()))

    def body(*refs):
        a_ref, b_ref = refs[:2]
        ex_refs = refs[2:2 + n_ex]
        out_refs = refs[2 + n_ex:2 + n_ex + n_out]
        scratch = refs[2 + n_ex + n_out:]
        if mode == "tn":
            at_ref = scratch[0]

            @pl.when(pl.program_id(1) == 0)
            def _():
                for c0 in range(0, tm, tt):
                    at_ref[c0:c0 + tt, :] = a_ref[:, c0:c0 + tt].astype(F32).T.astype(BF16)

            lhs = at_ref[...]
        else:
            lhs = a_ref[...].astype(BF16)
        rhs = b_ref[...].reshape(K, tn) if whole_b else b_ref[...]
        part = lax.dot_general(lhs, rhs.astype(BF16), dims, preferred_element_type=F32)

        def finish(acc):
            if epi is None:
                out_refs[0][...] = acc.astype(out_refs[0].dtype)
            else:
                epi(acc, ex_refs, out_refs)

        if nk == 1:
            finish(part)
        else:
            acc_ref = scratch[0]
            k = pl.program_id(2)

            @pl.when(k == 0)
            def _():
                acc_ref[...] = part

            @pl.when(k > 0)
            def _():
                acc_ref[...] += part

            @pl.when(k == nk - 1)
            def _():
                finish(acc_ref[...])

    scratch_shapes = []
    if mode == "tn":
        scratch_shapes.append(pltpu.VMEM((tm, K), BF16))
    elif nk > 1:
        scratch_shapes.append(pltpu.VMEM((tm, tn), F32))
    res = pl.pallas_call(
        body, name=name, grid=grid,
        in_specs=[a_spec, b_spec, *extra_specs], out_specs=out_specs, out_shape=out_shapes,
        scratch_shapes=scratch_shapes, compiler_params=_params(3),
    )(A.arr, B.arr, *extra_arrays)
    return res


def _row_tile(T):
    return _pick(256, [T])


def _tie(name, x, deps):
    def body(*refs):
        refs[-1][...] = jnp.zeros_like(refs[-1])

    return pl.pallas_call(
        body, name=name, in_specs=[_any_spec()] * (1 + len(deps)),
        out_specs=(_any_spec(), pl.BlockSpec(memory_space=pltpu.VMEM)),
        out_shape=(jax.ShapeDtypeStruct(x.shape, x.dtype),
                   jax.ShapeDtypeStruct((SUBLANE, LANE), F32)),
        input_output_aliases={0: 0},
    )(x, *deps)[0]


def _rmsnorm_fwd(name, x, g, *, into=None, col=0, deps=()):
    T, W = x.shape
    tr = _row_tile(T)

    def body(x_ref, g_ref, *rest):
        o_ref = rest[-1]
        xv = x_ref[...]
        rstd = lax.rsqrt(jnp.mean(xv * xv, axis=-1, keepdims=True) + EPS)
        o_ref[...] = (xv * rstd * g_ref[...]).astype(o_ref.dtype)

    in_specs = [pl.BlockSpec((tr, W), lambda i: (i, 0)), pl.BlockSpec((1, W), lambda i: (0, 0))]
    args = [x, g]
    kwargs = {}
    if into is None:
        out_shape = jax.ShapeDtypeStruct((T, W), BF16)
    else:
        out_shape = jax.ShapeDtypeStruct(into.shape, into.dtype)
        in_specs.append(_any_spec())
        args.append(into)
        kwargs["input_output_aliases"] = {2: 0}
    in_specs += [_any_spec()] * len(deps)
    args += list(deps)
    return pl.pallas_call(
        body, name=name, grid=(T // tr,), in_specs=in_specs,
        out_specs=pl.BlockSpec((tr, W), lambda i: (i, col)), out_shape=out_shape,
        compiler_params=_params(1), **kwargs)(*args)


def _rmsnorm_bwd(name, x, g, dn, *, dn_col=0, dres=None, want_dx=True, want_bf16=True):
    T, W = x.shape
    tr = _row_tile(T)
    has_res = dres is not None

    def body(*refs):
        x_ref, g_ref, dn_ref = refs[:3]
        pos = 3
        dres_ref = None
        if has_res:
            dres_ref = refs[pos]
            pos += 1
        outs = refs[pos:]
        dg_ref = outs[-1]
        xv = x_ref[...]
        rstd = lax.rsqrt(jnp.mean(xv * xv, axis=-1, keepdims=True) + EPS)
        xhat = xv * rstd
        dnv = dn_ref[...].astype(F32)

        @pl.when(pl.program_id(0) == 0)
        def _():
            dg_ref[...] = jnp.zeros_like(dg_ref)

        dg_ref[...] += jnp.sum(dnv * xhat, axis=0, keepdims=True)
        if want_dx:
            t = dnv * g_ref[...]
            dx = rstd * (t - xhat * jnp.mean(t * xhat, axis=-1, keepdims=True))
            if has_res:
                dx = dx + dres_ref[...]
            outs[0][...] = dx
            if want_bf16:
                outs[1][...] = dx.astype(BF16)

    row = pl.BlockSpec((tr, W), lambda i: (i, 0))
    in_specs = [row, pl.BlockSpec((1, W), lambda i: (0, 0)),
                pl.BlockSpec((tr, W), lambda i: (i, dn_col))]
    args = [x, g, dn]
    if has_res:
        in_specs.append(row)
        args.append(dres)
    out_shape, out_specs = [], []
    if want_dx:
        out_shape.append(jax.ShapeDtypeStruct((T, W), F32))
        out_specs.append(row)
        if want_bf16:
            out_shape.append(jax.ShapeDtypeStruct((T, W), BF16))
            out_specs.append(row)
    out_shape.append(jax.ShapeDtypeStruct((1, W), F32))
    out_specs.append(pl.BlockSpec((1, W), lambda i: (0, 0)))
    return pl.pallas_call(
        body, name=name, grid=(T // tr,), in_specs=in_specs, out_specs=out_specs,
        out_shape=out_shape, compiler_params=_params(1))(*args)


def _loss_head(name, h, g, target):
    T, W = h.shape
    tr = _row_tile(T)

    def body(h_ref, g_ref, t_ref, loss_ref, dx_ref, dxb_ref, dg_ref):
        xv = h_ref[...]
        gv = g_ref[...]
        rstd = lax.rsqrt(jnp.mean(xv * xv, axis=-1, keepdims=True) + EPS)
        xhat = xv * rstd
        diff = xhat * gv - t_ref[...]

        @pl.when(pl.program_id(0) == 0)
        def _():
            dg_ref[...] = jnp.zeros_like(dg_ref)
            loss_ref[...] = jnp.zeros_like(loss_ref)

        loss_ref[...] += 0.5 * jnp.sum(jnp.mean(diff * diff, axis=-1, keepdims=True))
        dnv = diff * (1.0 / W)
        dg_ref[...] += jnp.sum(dnv * xhat, axis=0, keepdims=True)
        t = dnv * gv
        dx = rstd * (t - xhat * jnp.mean(t * xhat, axis=-1, keepdims=True))
        dx_ref[...] = dx
        dxb_ref[...] = dx.astype(BF16)

    row = pl.BlockSpec((tr, W), lambda i: (i, 0))
    vec = pl.BlockSpec((1, W), lambda i: (0, 0))
    return pl.pallas_call(
        body, name=name, grid=(T // tr,), in_specs=[row, vec, row],
        out_specs=[pl.BlockSpec((SUBLANE, LANE), lambda i: (0, 0)), row, row, vec],
        out_shape=[jax.ShapeDtypeStruct((SUBLANE, LANE), F32), jax.ShapeDtypeStruct((T, W), F32),
                   jax.ShapeDtypeStruct((T, W), BF16), jax.ShapeDtypeStruct((1, W), F32)],
        compiler_params=_params(1))(h, g, target)


def _sigmoid(x):
    return 1.0 / (1.0 + jnp.exp(-x))


def _ffn_in(name, n, W):
    T, D = n.shape
    F = W.cols // 2
    tm = _pick(2048, [T])
    tn = _pick(512, [W.pcol])
    per = W.pcol // tn

    def body(a_ref, wg_ref, wu_ref, gu_ref, act_ref):
        a = a_ref[...]
        gate = jnp.dot(a, wg_ref[...], preferred_element_type=F32)
        up = jnp.dot(a, wu_ref[...], preferred_element_type=F32)
        gu_ref[0] = gate.astype(BF16)
        gu_ref[1] = up.astype(BF16)
        act_ref[...] = (gate * _sigmoid(gate) * up).astype(BF16)

    return pl.pallas_call(
        body, name=name, grid=(T // tm, F // tn),
        in_specs=[pl.BlockSpec((tm, D), lambda m, j: (m, 0)),
                  pl.BlockSpec((None, D, tn), lambda m, j: (j // per, 0, j % per)),
                  pl.BlockSpec((None, D, tn), lambda m, j: (2 + j // per, 0, j % per))],
        out_specs=[pl.BlockSpec((2, tm, tn), lambda m, j: (0, m, j)),
                   pl.BlockSpec((tm, tn), lambda m, j: (m, j))],
        out_shape=[jax.ShapeDtypeStruct((2, T, F), BF16), jax.ShapeDtypeStruct((T, F), BF16)],
        compiler_params=_params(2))(n, W.arr, W.arr)


def _ffn_forward(tag, h, norm_g, weights_of, deps=()):
    n = _rmsnorm_fwd(f"{tag}_norm", h, norm_g, deps=deps)
    w_in = weights_of(f"{tag}_in", n)[f"{tag}_w_in"]
    gu, act = _ffn_in(f"{tag}_in", n, w_in)
    w_out = weights_of(f"{tag}_out", act)[f"{tag}_w_out"]

    def epi(acc, ex, out):
        out[0][...] = ex[0][...] + 0.5 * acc

    (h_out,) = _matmul(f"{tag}_out", Mat(act), w_out, "nn", [("c", 1, F32)],
                       tm=1024, tn=1024, tk=1408, extras=[Mat(h)], epi=epi)
    return h_out[0], (n, gu, act, w_in, w_out)


def _ffn_backward(tag, h_in, norm_g, saved, dh, dh_bf, grads_ready, grads_flush,
                  early_out=False):
    n, gu, act, w_in, w_out = saved
    T, F = act.shape

    def epi(acc, ex, out):
        dact = 0.5 * acc
        gate = ex[0][0].astype(F32)
        up = ex[0][1].astype(F32)
        sig = _sigmoid(gate)
        out[0][0] = (dact * up * sig * (1.0 + gate * (1.0 - sig))).astype(BF16)
        out[0][1] = (dact * gate * sig).astype(BF16)

    def pair_spec(tm, tn):
        return pl.BlockSpec((2, tm, tn), lambda m, j, k: (0, m, j))

    def half(acc, ex, out):
        out[0][...] = (0.5 * acc).astype(out[0].dtype)

    (dw_out,) = _matmul(f"{tag}_dwout", Mat(act), Mat(dh_bf), "tn", [("r", N_CHIPS, BF16)],
                        tm=1408, tn=512, epi=half)
    if early_out:
        token = grads_ready(f"{tag}_out", {f"{tag}_w_out": dw_out})
        dh_bf = _tie(f"{tag}_dh_after_swap", dh_bf, [token])
    (dgu,) = _matmul(f"{tag}_dact", Mat(dh_bf), w_out, "nt",
                     [(jax.ShapeDtypeStruct((2, T, F), BF16), pair_spec)],
                     tm=512, tn=1408, extras=[(gu, pair_spec)], epi=epi)
    if early_out:
        dgu = _tie(f"{tag}_dgu_after_scatter", dgu, [grads_flush(f"{tag}_out", dgu)])
    (dw_in,) = _matmul(f"{tag}_dwin", Mat(n), Mat(dgu), "tn", [("c", N_CHIPS, BF16)],
                       tm=512, tn=1408)
    if early_out:
        group, partial = f"{tag}_in", {f"{tag}_w_in": dw_in}
    else:
        group, partial = tag, {f"{tag}_w_in": dw_in, f"{tag}_w_out": dw_out}
    dgu = _tie(f"{tag}_dgu_after_swap", dgu, [grads_ready(group, partial)])
    (dn,) = _matmul(f"{tag}_dn", Mat(dgu), w_in, "nt", [("c", 1, F32)],
                    tm=1024, tn=1024, tk=2816)
    dn = _tie(f"{tag}_dn_after_scatter", dn, [grads_flush(group, dn)])
    return _rmsnorm_bwd(f"{tag}_dnorm", h_in, norm_g, dn[0], dres=dh)


_GELU_C = math.sqrt(2.0 / math.pi)
_GELU_A = 0.044715


def _gelu(x):
    return 0.5 * x * (1.0 + jnp.tanh(_GELU_C * (x + _GELU_A * x * x * x)))


def _gelu_grad(x):
    th = jnp.tanh(_GELU_C * (x + _GELU_A * x * x * x))
    return 0.5 * (1.0 + th) + 0.5 * x * (1.0 - th * th) * _GELU_C * (1.0 + 3.0 * _GELU_A * x * x)


def _chunk_mask():
    t = lax.broadcasted_iota(jnp.int32, (SGU_BLOCK, SGU_BLOCK), 0) // CHUNK
    s = lax.broadcasted_iota(jnp.int32, (SGU_BLOCK, SGU_BLOCK), 1) // CHUNK
    return s <= t


def _sgu_group_forward(v_g, lg, lb, wm_bf, b_col):
    mu = jnp.mean(v_g, axis=-1, keepdims=True)
    xc = v_g - mu
    rstd = lax.rsqrt(jnp.mean(xc * xc, axis=-1, keepdims=True) + EPS)
    vhat = xc * rstd
    vn = vhat * lg + lb
    mixed = jnp.dot(wm_bf, vn.astype(BF16), preferred_element_type=F32) + b_col
    return vhat, rstd, vn, mixed


def _sgu_forward(name, z, ln_g, ln_b, w_s, b_t, gn, d_model):
    T = z.shape[0]
    W_A = ln_g.shape[1]
    G = W_A // GROUP_DIM

    def body(z_ref, lg_ref, lb_ref, w_ref, bt_ref, gn_ref, y_ref):
        mask = _chunk_mask()
        u = _gelu(z_ref[:, :W_A])
        v = _gelu(z_ref[:, W_A:])
        cols = []
        for g in range(G):
            sl = slice(g * GROUP_DIM, (g + 1) * GROUP_DIM)
            wm = jnp.where(mask, w_ref[g], 0.0).astype(BF16)
            _, _, _, mixed = _sgu_group_forward(v[:, sl], lg_ref[:, sl], lb_ref[:, sl], wm,
                                                bt_ref[:, g:g + 1])
            cols.append(u[:, sl] * mixed)
        ya = jnp.concatenate(cols, axis=1)
        rstd = lax.rsqrt(jnp.mean(ya * ya, axis=-1, keepdims=True) + EPS)
        y_ref[...] = (ya * rstd * gn_ref[...]).astype(BF16)

    vec = pl.BlockSpec((1, W_A), lambda i: (0, 0))
    return pl.pallas_call(
        body, name=name, grid=(T // SGU_BLOCK,),
        in_specs=[pl.BlockSpec((SGU_BLOCK, 2 * W_A), lambda i: (i, 0)), vec, vec,
                  pl.BlockSpec((G, SGU_BLOCK, SGU_BLOCK), lambda i: (0, 0, 0)),
                  pl.BlockSpec((SGU_BLOCK, G), lambda i: (0, 0)), vec],
        out_specs=pl.BlockSpec((SGU_BLOCK, W_A), lambda i: (i, 0)),
        out_shape=jax.ShapeDtypeStruct((T, d_model), BF16),
        compiler_params=_params(1))(z, ln_g, ln_b, w_s, b_t, gn)


def _sgu_backward(name, z, dy, ln_g, ln_b, w_s, b_t, gn):
    T = z.shape[0]
    W_A = ln_g.shape[1]
    G = W_A // GROUP_DIM

    def body(z_ref, dy_ref, lg_ref, lb_ref, w_ref, bt_ref, gn_ref,
             dz_ref, dlg_ref, dlb_ref, dw_ref, db_ref, dgn_ref):
        @pl.when(pl.program_id(0) == 0)
        def _():
            for r in (dlg_ref, dlb_ref, dw_ref, db_ref, dgn_ref):
                r[...] = jnp.zeros_like(r)

        mask = _chunk_mask()
        zu = z_ref[:, :W_A]
        zv = z_ref[:, W_A:]
        u = _gelu(zu)
        v = _gelu(zv)
        saved, cols = [], []
        for g in range(G):
            sl = slice(g * GROUP_DIM, (g + 1) * GROUP_DIM)
            wm = jnp.where(mask, w_ref[g], 0.0)
            vhat, rstd, vn, mixed = _sgu_group_forward(
                v[:, sl], lg_ref[:, sl], lb_ref[:, sl], wm.astype(BF16), bt_ref[:, g:g + 1])
            saved.append((wm, vhat, rstd, vn, mixed))
            cols.append(u[:, sl] * mixed)
        ya = jnp.concatenate(cols, axis=1)
        rstd_a = lax.rsqrt(jnp.mean(ya * ya, axis=-1, keepdims=True) + EPS)
        ya_hat = ya * rstd_a
        dyv = dy_ref[...].astype(F32)
        dgn_ref[...] += jnp.sum(dyv * ya_hat, axis=0, keepdims=True)
        t = dyv * gn_ref[...]
        dya = rstd_a * (t - ya_hat * jnp.mean(t * ya_hat, axis=-1, keepdims=True))
        du_cols, dv_cols, dlg_cols, dlb_cols = [], [], [], []
        for g in range(G):
            sl = slice(g * GROUP_DIM, (g + 1) * GROUP_DIM)
            wm, vhat, rstd, vn, mixed = saved[g]
            dya_g = dya[:, sl]
            du_cols.append(dya_g * mixed)
            dmix = dya_g * u[:, sl]
            dmix_bf = dmix.astype(BF16)
            db_ref[g] += jnp.sum(dmix, axis=1, keepdims=True)
            dw = lax.dot_general(dmix_bf, vn.astype(BF16), (((1,), (1,)), ((), ())),
                                 preferred_element_type=F32)
            dw_ref[g] += jnp.where(mask, dw, 0.0)
            dvn = jnp.dot(wm.T.astype(BF16), dmix_bf, preferred_element_type=F32)
            dlg_cols.append(jnp.sum(dvn * vhat, axis=0, keepdims=True))
            dlb_cols.append(jnp.sum(dvn, axis=0, keepdims=True))
            dvhat = dvn * lg_ref[:, sl]
            dv_cols.append(rstd * (dvhat - jnp.mean(dvhat, axis=-1, keepdims=True)
                                   - vhat * jnp.mean(dvhat * vhat, axis=-1, keepdims=True)))
        dlg_ref[...] += jnp.concatenate(dlg_cols, axis=1)
        dlb_ref[...] += jnp.concatenate(dlb_cols, axis=1)
        dz_ref[:, :W_A] = (jnp.concatenate(du_cols, axis=1) * _gelu_grad(zu)).astype(BF16)
        dz_ref[:, W_A:] = (jnp.concatenate(dv_cols, axis=1) * _gelu_grad(zv)).astype(BF16)

    vec = pl.BlockSpec((1, W_A), lambda i: (0, 0))
    wspec = pl.BlockSpec((G, SGU_BLOCK, SGU_BLOCK), lambda i: (0, 0, 0))
    return pl.pallas_call(
        body, name=name, grid=(T // SGU_BLOCK,),
        in_specs=[pl.BlockSpec((SGU_BLOCK, 2 * W_A), lambda i: (i, 0)),
                  pl.BlockSpec((SGU_BLOCK, W_A), lambda i: (i, 0)), vec, vec, wspec,
                  pl.BlockSpec((SGU_BLOCK, G), lambda i: (0, 0)), vec],
        out_specs=[pl.BlockSpec((SGU_BLOCK, 2 * W_A), lambda i: (i, 0)), vec, vec, wspec,
                   pl.BlockSpec((G, SGU_BLOCK, 1), lambda i: (0, 0, 0)), vec],
        out_shape=[jax.ShapeDtypeStruct((T, 2 * W_A), BF16), jax.ShapeDtypeStruct((1, W_A), F32),
                   jax.ShapeDtypeStruct((1, W_A), F32),
                   jax.ShapeDtypeStruct((G, SGU_BLOCK, SGU_BLOCK), F32),
                   jax.ShapeDtypeStruct((G, SGU_BLOCK, 1), F32),
                   jax.ShapeDtypeStruct((1, W_A), F32)],
        compiler_params=_params(1))(z, dy, ln_g, ln_b, w_s, b_t, gn)


def _split_dot(x, tri):
    hi = x.astype(BF16)
    lo = (x - hi.astype(F32)).astype(BF16)
    return (jnp.dot(hi, tri, preferred_element_type=F32)
            + jnp.dot(lo, tri, preferred_element_type=F32))


def _tri(n, rel):
    r = lax.broadcasted_iota(jnp.int32, (n, n), 0)
    c = lax.broadcasted_iota(jnp.int32, (n, n), 1)
    return rel(r, c).astype(BF16)


def _dot_nt(a, b):
    return lax.dot_general(a, b, (((1,), (1,)), ((), ())), preferred_element_type=F32)


def _dot_tn(a, b):
    return lax.dot_general(a, b, (((0,), (0,)), ((), ())), preferred_element_type=F32)


def _sb_scores(qs, kj, mask):
    zz = _dot_nt(qs, kj)
    log_beta = jnp.minimum(zz, 0.0) - jnp.log(1.0 + jnp.exp(-jnp.abs(zz)))
    log_1m = log_beta - zz
    if mask is not None:
        log_1m = jnp.where(mask, log_1m, 0.0)
    return log_beta, log_1m


def _masked(mask, x):
    return x if mask is None else jnp.where(mask, x, 0.0)


def _sb_tiles(T):
    tk = _pick(256, [T])
    tq = 2 * tk if T % (2 * tk) == 0 else tk
    return tq, tk


def _sb_cols(w_a, w_b):
    base = 2 * w_a // GROUP_DIM
    per = w_b // GROUP_DIM
    return base, base + per, base + 2 * per


def _sb_forward(name, z, w_a, w_b):
    T = z.shape[0]
    H = w_b // GROUP_DIM
    tq, tk = _sb_tiles(T)
    per = tq // tk
    qc, kc, vc = _sb_cols(w_a, w_b)
    scale = GROUP_DIM ** -0.5

    def body(q_ref, k_ref, v_ref, y_ref, tot_ref):
        i = pl.program_id(1)
        qs = (q_ref[...] * scale).astype(BF16)
        upper = _tri(tk, lambda r, c: r > c)
        ahead = (lax.broadcasted_iota(jnp.int32, (tq, tk), 1)
                 - lax.broadcasted_iota(jnp.int32, (tq, tk), 0))

        def step(j, carry, masked):
            acc, later = carry
            k0 = pl.multiple_of(j * tk, tk)
            kj = k_ref[pl.ds(k0, tk), :].astype(BF16)
            vj = v_ref[pl.ds(k0, tk), :].astype(BF16)
            mask = ahead < i * tq - k0 if masked else None
            log_beta, log_1m = _sb_scores(qs, kj, mask)
            rest = _split_dot(log_1m, upper) + later
            a = _masked(mask, jnp.exp(log_beta + rest))
            acc = acc + jnp.dot(a.astype(BF16), vj, preferred_element_type=F32)
            return acc, later + jnp.sum(log_1m, axis=1, keepdims=True)

        carry = (jnp.zeros((tq, GROUP_DIM), F32), jnp.zeros((tq, 1), F32))
        first = i * per
        for d in reversed(range(per)):
            carry = step(first + d, carry, True)
        acc, total = lax.fori_loop(0, first, lambda jj, c: step(first - 1 - jj, c, False), carry)
        y_ref[...] = acc
        tot_ref[...] = total

    return pl.pallas_call(
        body, name=name, grid=(H, T // tq),
        in_specs=[pl.BlockSpec((tq, GROUP_DIM), lambda h, i: (i, qc + h)),
                  pl.BlockSpec((T, GROUP_DIM), lambda h, i: (0, kc + h)),
                  pl.BlockSpec((T, GROUP_DIM), lambda h, i: (0, vc + h))],
        out_specs=[pl.BlockSpec((tq, GROUP_DIM), lambda h, i: (i, h)),
                   pl.BlockSpec((None, tq, 1), lambda h, i: (h, i, 0))],
        out_shape=[jax.ShapeDtypeStruct((T, w_b), F32), jax.ShapeDtypeStruct((H, T, 1), F32)],
        compiler_params=_params(2))(z, z, z)


def _sb_backward(name, z, do, total, w_a, w_b):
    T = z.shape[0]
    H = w_b // GROUP_DIM
    tq, tk = _sb_tiles(T)
    per = tq // tk
    qc, kc, vc = _sb_cols(w_a, w_b)
    scale = GROUP_DIM ** -0.5

    def body(q_ref, k_ref, v_ref, do_ref, tot_ref, dq_ref, dkv_ref):
        i = pl.program_id(1)

        @pl.when(i == 0)
        def _():
            dkv_ref[...] = jnp.zeros_like(dkv_ref)

        qs = (q_ref[...] * scale).astype(BF16)
        dob = do_ref[...].astype(BF16)
        upto = _tri(tk, lambda r, c: r <= c)
        before = _tri(tk, lambda r, c: r < c)
        ahead = (lax.broadcasted_iota(jnp.int32, (tq, tk), 1)
                 - lax.broadcasted_iota(jnp.int32, (tq, tk), 0))

        def step(j, carry, masked):
            dq, left, e_seen = carry
            k0 = pl.multiple_of(j * tk, tk)
            kj = k_ref[pl.ds(k0, tk), :].astype(BF16)
            vj = v_ref[pl.ds(k0, tk), :].astype(BF16)
            mask = ahead < i * tq - k0 if masked else None
            log_beta, log_1m = _sb_scores(qs, kj, mask)
            rest = left - _split_dot(log_1m, upto)
            a = _masked(mask, jnp.exp(log_beta + rest))
            e = a * _dot_nt(dob, vj)
            e_before = e_seen + jnp.dot(e.astype(BF16), before, preferred_element_type=F32)
            beta = jnp.exp(log_beta)
            dz = _masked(mask, e * (1.0 - beta) - beta * e_before).astype(BF16)
            dq = dq + jnp.dot(dz, kj, preferred_element_type=F32)
            dkv_ref[0, pl.ds(k0, tk), :] += _dot_tn(dz, qs)
            dkv_ref[1, pl.ds(k0, tk), :] += _dot_tn(a.astype(BF16), dob)
            return (dq, left - jnp.sum(log_1m, axis=1, keepdims=True),
                    e_seen + jnp.sum(e, axis=1, keepdims=True))

        first = i * per
        carry = (jnp.zeros((tq, GROUP_DIM), F32), tot_ref[...], jnp.zeros((tq, 1), F32))
        carry = lax.fori_loop(0, first, lambda j, c: step(j, c, False), carry)
        for d in range(per):
            carry = step(first + d, carry, True)
        dq_ref[...] = (carry[0] * scale).astype(BF16)

    return pl.pallas_call(
        body, name=name, grid=(H, T // tq),
        in_specs=[pl.BlockSpec((tq, GROUP_DIM), lambda h, i: (i, qc + h)),
                  pl.BlockSpec((T, GROUP_DIM), lambda h, i: (0, kc + h)),
                  pl.BlockSpec((T, GROUP_DIM), lambda h, i: (0, vc + h)),
                  pl.BlockSpec((tq, GROUP_DIM), lambda h, i: (i, h)),
                  pl.BlockSpec((None, tq, 1), lambda h, i: (h, i, 0))],
        out_specs=[pl.BlockSpec((tq, GROUP_DIM), lambda h, i: (i, h)),
                   pl.BlockSpec((2, T, GROUP_DIM), lambda h, i: (0, 0, h))],
        out_shape=[jax.ShapeDtypeStruct((T, w_b), BF16), jax.ShapeDtypeStruct((2, T, w_b), F32)],
        compiler_params=_params(2))(z, z, z, do, total)


def _softmax_rows(s):
    m = jnp.max(s, axis=-1, keepdims=True)
    p = jnp.exp(s - m)
    return p / jnp.sum(p, axis=-1, keepdims=True)


def _xattn_forward(name, q, kv):
    T, D = q.shape
    Nm = kv.shape[0]
    dh = D // X_HEADS
    tq = _pick(512, [T])

    def body(q_ref, k_ref, v_ref, o_ref):
        p = _softmax_rows(_dot_nt(q_ref[...], k_ref[...]))
        o_ref[...] = jnp.dot(p.astype(BF16), v_ref[...], preferred_element_type=F32).astype(BF16)

    return pl.pallas_call(
        body, name=name, grid=(T // tq, X_HEADS),
        in_specs=[pl.BlockSpec((tq, dh), lambda i, h: (i, h)),
                  pl.BlockSpec((Nm, dh), lambda i, h: (0, h)),
                  pl.BlockSpec((Nm, dh), lambda i, h: (0, X_HEADS + h))],
        out_specs=pl.BlockSpec((tq, dh), lambda i, h: (i, h)),
        out_shape=jax.ShapeDtypeStruct((T, D), BF16),
        compiler_params=_params(2))(q, kv, kv)


def _xattn_backward(name, q, kv, do):
    T, D = q.shape
    Nm = kv.shape[0]
    dh = D // X_HEADS
    tq = _pick(512, [T])
    scale = dh ** -0.5

    def body(q_ref, k_ref, v_ref, do_ref, dq_ref, dkv_ref):
        @pl.when(pl.program_id(1) == 0)
        def _():
            dkv_ref[...] = jnp.zeros_like(dkv_ref)

        qv, kk, vv, dov = q_ref[...], k_ref[...], v_ref[...], do_ref[...]
        p = _softmax_rows(_dot_nt(qv, kk))
        dp = _dot_nt(dov, vv)
        ds = (p * (dp - jnp.sum(dp * p, axis=-1, keepdims=True))).astype(BF16)
        dq_ref[...] = (jnp.dot(ds, kk, preferred_element_type=F32) * scale).astype(BF16)
        dkv_ref[0] += _dot_tn(ds, qv)
        dkv_ref[1] += _dot_tn(p.astype(BF16), dov)

    blk = pl.BlockSpec((tq, dh), lambda h, i: (i, h))
    return pl.pallas_call(
        body, name=name, grid=(X_HEADS, T // tq),
        in_specs=[blk, pl.BlockSpec((Nm, dh), lambda h, i: (0, h)),
                  pl.BlockSpec((Nm, dh), lambda h, i: (0, X_HEADS + h)), blk],
        out_specs=[blk, pl.BlockSpec((2, Nm, dh), lambda h, i: (0, 0, h))],
        out_shape=[jax.ShapeDtypeStruct((T, D), BF16), jax.ShapeDtypeStruct((2, Nm, D), F32)],
        compiler_params=_params(2))(q, kv, kv, do)


def _position():
    x, y, c = lax.axis_index("x"), lax.axis_index("y"), lax.axis_index("c")
    other_chips = [(1 - x, y), (x, 1 - y), (1 - x, 1 - y)]
    return x, y, c, other_chips


def _hbm_spec():
    return pl.BlockSpec(memory_space=pltpu.HBM)


def _sem_spec():
    return pl.BlockSpec(memory_space=pltpu.SEMAPHORE)


def _split_start(name, arrays, make_copies, n_sems, deps=()):
    n, d = len(arrays), len(deps)

    def body(*refs):
        ins = refs[:n]
        send_sems, recv_sems = refs[n + d], refs[n + d + 1]
        token = refs[-1]
        for cp in make_copies(ins, send_sems, recv_sems):
            cp.start()
        token[...] = jnp.zeros_like(token)

    res = pl.pallas_call(
        body, name=name,
        out_shape=(pltpu.SemaphoreType.DMA((n_sems,)), pltpu.SemaphoreType.DMA((n_sems,)),
                   *[pltpu.HBM(a.shape, a.dtype) for a in arrays],
                   jax.ShapeDtypeStruct((SUBLANE, LANE), F32)),
        in_specs=[_hbm_spec()] * n + [_any_spec()] * d,
        out_specs=(_sem_spec(), _sem_spec(), *[_hbm_spec()] * n,
                   pl.BlockSpec(memory_space=pltpu.VMEM)),
        input_output_aliases={i: 2 + i for i in range(n)},
        compiler_params=pltpu.CompilerParams(
            has_side_effects=pltpu.SideEffectType.DATAFLOW_SIDE_EFFECTING),
    )(*[pltpu.with_memory_space_constraint(a, pltpu.HBM) for a in arrays], *deps)
    return res[0], res[1], list(res[2:2 + n]), res[-1]


def _split_wait(name, arrays, send_sems, recv_sems, after, make_copies):
    n = len(arrays)
    after = list(after) if isinstance(after, (list, tuple)) else [after]

    def body(*refs):
        ins = refs[:n]
        send_ref, recv_ref = refs[n], refs[n + 1]
        for cp in make_copies(ins, send_ref, recv_ref):
            cp.wait_send()
            cp.wait_recv()

    return pl.pallas_call(
        body, name=name,
        out_shape=tuple(pltpu.HBM(a.shape, a.dtype) for a in arrays),
        in_specs=[_hbm_spec()] * n + [_sem_spec(), _sem_spec()] + [_any_spec()] * len(after),
        out_specs=tuple(_hbm_spec() for _ in arrays),
        input_output_aliases={i: i for i in range(n)},
        compiler_params=pltpu.CompilerParams(
            has_side_effects=pltpu.SideEffectType.DATAFLOW_SIDE_EFFECTING),
    )(*arrays, send_sems, recv_sems, *after)


def _gather_copies(refs, send_sems, recv_sems):
    x, y, c, chips = _position()
    me = 2 * x + y
    copies = []
    for i, ref in enumerate(refs):
        rows = ref.shape[1] // 2
        piece = ref.at[me, pl.ds(c * rows, rows), :]
        for j, (px, py) in enumerate(chips):
            copies.append(pltpu.make_async_remote_copy(
                src_ref=piece, dst_ref=piece, send_sem=send_sems.at[3 * i + j],
                recv_sem=recv_sems.at[3 * i + j], device_id=(px, py, c), device_id_type=MESH))
    return copies


def _near_copies(refs, send_sems, recv_sems):
    x, y, c, chips = _position()
    me = 2 * x + y
    copies = []
    for i, ref in enumerate(refs):
        rows = ref.shape[1] // 2
        piece = ref.at[me, pl.ds(c * rows, rows), :]
        for j, (px, py) in enumerate(chips[:2]):
            copies.append(pltpu.make_async_remote_copy(
                src_ref=piece, dst_ref=piece, send_sem=send_sems.at[2 * i + j],
                recv_sem=recv_sems.at[2 * i + j], device_id=(px, py, c), device_id_type=MESH))
    return copies


def _relay_copies(refs, send_sems, recv_sems):
    x, y, c, chips = _position()
    copies = []
    for i, ref in enumerate(refs):
        rows = ref.shape[1] // 4
        for j, (px, py) in enumerate(chips[:2]):
            ox, oy = chips[1 - j]
            piece = ref.at[2 * ox + oy, pl.ds((2 * c + j) * rows, rows), :]
            copies.append(pltpu.make_async_remote_copy(
                src_ref=piece, dst_ref=piece, send_sem=send_sems.at[2 * i + j],
                recv_sem=recv_sems.at[2 * i + j], device_id=(px, py, c), device_id_type=MESH))
    return copies


def _share_copies(refs, send_sems, recv_sems):
    x, y, c, _ = _position()
    copies = []
    for i, ref in enumerate(refs):
        rows = ref.shape[0] // 2
        mine = ref.at[pl.ds(c * rows, rows), :]
        copies.append(pltpu.make_async_remote_copy(
            src_ref=mine, dst_ref=mine, send_sem=send_sems.at[i], recv_sem=recv_sems.at[i],
            device_id=(x, y, 1 - c), device_id_type=MESH))
    return copies


def _scatter_copies(refs, send_sems, recv_sems):
    x, y, c, chips = _position()
    n = len(refs) // 2
    copies = []
    for i in range(n):
        for j, (px, py) in enumerate(chips):
            copies.append(pltpu.make_async_remote_copy(
                src_ref=refs[i].at[2 * px + py], dst_ref=refs[n + i].at[j],
                send_sem=send_sems.at[3 * i + j], recv_sem=recv_sems.at[3 * i + j],
                device_id=(px, py, c), device_id_type=MESH))
    return copies


def _cast_own(name, place, shard):
    rows, cols = shard.shape
    tr = _block_rows(rows, cols)

    def body(place_ref, w_ref, o_ref):
        o_ref[...] = w_ref[...].astype(BF16)

    grid_spec = pltpu.PrefetchScalarGridSpec(
        num_scalar_prefetch=1, grid=(rows // tr,),
        in_specs=[pl.BlockSpec((tr, cols), lambda r, pr: (r, 0))],
        out_specs=pl.BlockSpec((None, tr, cols), lambda r, pr: (pr[0], r, 0)))
    return pl.pallas_call(
        body, name=name, grid_spec=grid_spec,
        out_shape=jax.ShapeDtypeStruct((N_CHIPS, rows, cols), BF16),
        compiler_params=_params(1))(place, shard)


def _forward_to_sibling(name, arrays, deps=()):
    n = len(arrays)

    def body(*refs):
        ins = refs[:n]
        send_sems, recv_sems = refs[-2:]
        x, y, c, chips = _position()
        sends = []
        for i in range(n):
            rows = ins[i].shape[1] // 2
            for j, (px, py) in enumerate(chips):
                piece = ins[i].at[2 * px + py, pl.ds(c * rows, rows), :]
                cp = pltpu.make_async_remote_copy(
                    src_ref=piece, dst_ref=piece, send_sem=send_sems.at[i, j],
                    recv_sem=recv_sems.at[i, j], device_id=(x, y, 1 - c), device_id_type=MESH)
                cp.start()
                sends.append(cp)
        for i in range(n):
            rows = ins[i].shape[1] // 2
            for j, (px, py) in enumerate(chips):
                piece = ins[i].at[2 * px + py, pl.ds((1 - c) * rows, rows), :]
                pltpu.make_async_remote_copy(
                    src_ref=piece, dst_ref=piece, send_sem=send_sems.at[i, j],
                    recv_sem=recv_sems.at[i, j], device_id=(x, y, 1 - c),
                    device_id_type=MESH).wait_recv()
        for cp in sends:
            cp.wait_send()

    return pl.pallas_call(
        body, name=name,
        in_specs=[_any_spec()] * (n + len(deps)), out_specs=[_any_spec()] * n,
        out_shape=[jax.ShapeDtypeStruct(a.shape, a.dtype) for a in arrays],
        input_output_aliases={i: i for i in range(n)},
        scratch_shapes=[pltpu.SemaphoreType.DMA((n, 3))] * 2,
    )(*arrays, *deps)


def _swap_copies(refs, send_sems, recv_sems):
    x, y, c, _ = _position()
    n = len(refs) // 2
    copies = []
    for i in range(n):
        rows = refs[i].shape[1] // 2
        copies.append(pltpu.make_async_remote_copy(
            src_ref=refs[i].at[:, pl.ds((1 - c) * rows, rows), :], dst_ref=refs[n + i],
            send_sem=send_sems.at[i], recv_sem=recv_sems.at[i],
            device_id=(x, y, 1 - c), device_id_type=MESH))
    return copies


def _small_copies(refs, send_sems, recv_sems):
    packed, slots = refs
    x, y, c, _ = _position()
    me = 4 * x + 2 * y + c
    copies = []
    for r in range(1, N_DEV):
        peer = (x ^ ((r >> 2) & 1), y ^ ((r >> 1) & 1), c ^ (r & 1))
        copies.append(pltpu.make_async_remote_copy(
            src_ref=packed, dst_ref=slots.at[me], send_sem=send_sems.at[r - 1],
            recv_sem=recv_sems.at[r - 1], device_id=peer, device_id_type=MESH))
    return copies


def _block_rows(rows, cols, itemsize=4, target=1 << 20):
    return _pick(max(BF16_ROWS, target // (cols * itemsize)), [rows], unit=BF16_ROWS)


def _pair_sum(name, place, grad, received):
    P, rows, cols = received.shape
    tr = _block_rows(rows, cols, itemsize=2, target=2 << 20)
    nb = rows // tr

    def body(place_ref, g_ref, r_ref, o_ref):
        o_ref[...] = (g_ref[...].astype(F32) + r_ref[...].astype(F32)).astype(BF16)

    grid_spec = pltpu.PrefetchScalarGridSpec(
        num_scalar_prefetch=1, grid=(P, nb),
        in_specs=[pl.BlockSpec((None, tr, cols), lambda p, r, pr: (p, pr[1] * nb + r, 0)),
                  pl.BlockSpec((None, tr, cols), lambda p, r, pr: (p, r, 0))],
        out_specs=pl.BlockSpec((None, tr, cols), lambda p, r, pr: (p, r, 0)))
    return pl.pallas_call(
        body, name=name, grid_spec=grid_spec,
        out_shape=jax.ShapeDtypeStruct(received.shape, BF16),
        compiler_params=_params(2))(place, grad, received)


def _final_sum(name, place, grad, received, from_chips):
    _, rows, cols = received.shape
    tr = _block_rows(rows, cols)
    nb = rows // tr

    def body(place_ref, g_ref, r_ref, c_ref, o_ref):
        acc = g_ref[...].astype(F32) + r_ref[...].astype(F32)
        for j in range(3):
            acc = acc + c_ref[j].astype(F32)
        o_ref[...] = acc

    grid_spec = pltpu.PrefetchScalarGridSpec(
        num_scalar_prefetch=1, grid=(nb,),
        in_specs=[pl.BlockSpec((None, tr, cols), lambda r, pr: (pr[0], pr[1] * nb + r, 0)),
                  pl.BlockSpec((None, tr, cols), lambda r, pr: (pr[0], r, 0)),
                  pl.BlockSpec((3, tr, cols), lambda r, pr: (0, r, 0))],
        out_specs=pl.BlockSpec((tr, cols), lambda r, pr: (pr[1] * nb + r, 0)))
    return pl.pallas_call(
        body, name=name, grid_spec=grid_spec,
        out_shape=jax.ShapeDtypeStruct((2 * rows, cols), F32),
        compiler_params=_params(1))(place, grad, received, from_chips)


def _sum_devices(name, me, gathered, own):
    n_dev, rows, cols = gathered.shape
    tr = _pick(256, [rows])

    def body(me_ref, g_ref, own_ref, o_ref):
        term = lambda d: jnp.where(me_ref[0] == d, own_ref[...], g_ref[d])
        acc = term(0)
        for d in range(1, n_dev):
            acc = acc + term(d)
        o_ref[...] = acc

    grid_spec = pltpu.PrefetchScalarGridSpec(
        num_scalar_prefetch=1, grid=(rows // tr,),
        in_specs=[pl.BlockSpec((n_dev, tr, cols), lambda r, me_ref: (0, r, 0)),
                  pl.BlockSpec((tr, cols), lambda r, me_ref: (r, 0))],
        out_specs=pl.BlockSpec((tr, cols), lambda r, me_ref: (r, 0)))
    return pl.pallas_call(
        body, name=name, grid_spec=grid_spec,
        out_shape=jax.ShapeDtypeStruct((rows, cols), F32),
        compiler_params=_params(1))(me, gathered, own)


def _adamw(name, w, g, m, v):
    rows, cols = w.shape
    tr = _block_rows(rows, cols)
    c1 = 1.0 / (1.0 - ADAM_B1 ** ADAM_STEP)
    c2 = 1.0 / (1.0 - ADAM_B2 ** ADAM_STEP)

    def body(w_ref, g_ref, m_ref, v_ref, go_ref, d_ref, nm_ref, nv_ref):
        gv = g_ref[...]
        go_ref[...] = gv
        nm = ADAM_B1 * m_ref[...] + (1.0 - ADAM_B1) * gv
        nv = ADAM_B2 * v_ref[...] + (1.0 - ADAM_B2) * (gv * gv)
        nm_ref[...] = nm
        nv_ref[...] = nv
        d_ref[...] = -ADAM_LR * ((nm * c1) / (jnp.sqrt(nv * c2) + ADAM_EPS) + ADAM_WD * w_ref[...])

    blk = pl.BlockSpec((tr, cols), lambda r: (r, 0))
    shape = jax.ShapeDtypeStruct((rows, cols), F32)
    return pl.pallas_call(
        body, name=name, grid=(rows // tr,), in_specs=[blk] * 4, out_specs=[blk] * 4,
        out_shape=[shape] * 4, compiler_params=_params(1))(w, g, m, v)


BIG = ("ffn1_w_in", "ffn1_w_out", "w_mix_in", "w_mix_out", "w_cq", "w_ckv", "w_co",
       "ffn2_w_in", "ffn2_w_out")
BIG_KIND = {"ffn1_w_in": "c", "ffn1_w_out": "r", "w_mix_in": "c", "w_mix_out": "r", "w_cq": "r",
            "w_ckv": "c", "w_co": "r", "ffn2_w_in": "c", "ffn2_w_out": "r"}
GATHER_GROUPS = (("ffn1_in", ("ffn1_w_in",)), ("ffn1_out", ("ffn1_w_out",)),
                 ("mix_in", ("w_mix_in",)), ("mix_out", ("w_mix_out",)),
                 ("cross", ("w_cq", "w_ckv", "w_co")),
                 ("ffn2_in", ("ffn2_w_in",)), ("ffn2_out", ("ffn2_w_out",)))
GATHER_AFTER = (("ffn1_in", None), ("ffn1_out", "ffn1_in"), ("mix_in", "ffn1_out"),
                ("mix_out", "mix_in"), ("cross", "mix_in"), ("ffn2_in", "mix_in"),
                ("ffn2_out", "cross"))
RELAYED = ("ffn1_in",)
TAIL_STAGES = (("sum", "ffn2"), ("sum", "cross"), ("sum", "mix"), ("sum", "ffn1_out"),
               ("update", "ffn2"), ("update", "cross"), ("sum", "ffn1_in"), ("update", "mix"),
               ("update", "ffn1_out"), ("update", "ffn1_in"))
SMALL = ("ffn1_norm", "mix_norm", "ln_v_gain", "ln_v_bias", "spatial_w", "spatial_b", "gnorm_a",
         "gnorm_b", "cross_norm", "mem_norm", "ffn2_norm", "final_norm")
WEIGHTS = ("ffn1_norm", "ffn1_w_in", "ffn1_w_out", "mix_norm", "w_mix_in", "ln_v_gain",
           "ln_v_bias", "spatial_w", "spatial_b", "gnorm_a", "gnorm_b", "w_mix_out", "cross_norm",
           "mem_norm", "w_cq", "w_ckv", "w_co", "ffn2_norm", "ffn2_w_in", "ffn2_w_out",
           "final_norm")


def _pack(arrays):
    return jnp.concatenate([a.reshape(-1, LANE) for a in arrays], axis=0)


def _unpack(packed, like):
    out, row = [], 0
    for a in like:
        rows = a.size // LANE
        out.append(packed[row:row + rows].reshape(a.shape))
        row += rows
    return out


def _local_step(x, mem, target, small, weights_of, start_tokens, grads_ready, grads_flush):
    T, D = x.shape
    vec = lambda name: small[name].reshape(1, -1)
    w_a = small["ln_v_gain"].size
    w_b = small["gnorm_b"].size
    G = w_a // GROUP_DIM
    w_s = small["spatial_w"].reshape(G, SGU_BLOCK, SGU_BLOCK)
    b_t = small["spatial_b"].reshape(G, SGU_BLOCK).T

    h1, ffn1_saved = _ffn_forward("ffn1", x, vec("ffn1_norm"), weights_of, deps=start_tokens)
    n2 = _rmsnorm_fwd("mix_norm", h1, vec("mix_norm"))
    big = weights_of("mix_in", n2)
    (z,) = _matmul("mix_in", Mat(n2), big["w_mix_in"], "nn", [("c", 1, F32)], tm=2048, tn=256)
    z = z[0]
    y = _sgu_forward("sgu", z, vec("ln_v_gain"), vec("ln_v_bias"), w_s, b_t, vec("gnorm_a"), D)
    yb, sb_total = _sb_forward("stickbreak", z, w_a, w_b)
    y = _rmsnorm_fwd("gnorm_b", yb, vec("gnorm_b"), into=y, col=w_a // w_b)

    def add_res(acc, ex, out):
        out[0][...] = ex[0][...] + acc

    big.update(weights_of("mix_out", y))
    (h2,) = _matmul("mix_out", Mat(y), big["w_mix_out"], "nn", [("c", 1, F32)],
                    tm=1024, tn=1024, extras=[Mat(h1)], epi=add_res)
    h2 = h2[0]
    n3 = _rmsnorm_fwd("cross_norm", h2, vec("cross_norm"))
    memn = _rmsnorm_fwd("mem_norm", mem, vec("mem_norm"))
    big.update(weights_of("cross", n3))
    x_scale = (D // X_HEADS) ** -0.5

    def scaled(acc, ex, out):
        out[0][...] = (acc * x_scale).astype(BF16)

    (q,) = _matmul("cross_q", Mat(n3), big["w_cq"], "nn", [("c", 1, BF16)],
                   tm=1024, tn=1024, epi=scaled)
    (kv,) = _matmul("cross_kv", Mat(memn), big["w_ckv"], "nn", [("c", 1, BF16)], tm=256, tn=1024)
    q, kv = q[0], kv[0]
    o = _xattn_forward("cross_attn", q, kv)
    (h3,) = _matmul("cross_out", Mat(o), big["w_co"], "nn", [("c", 1, F32)],
                    tm=1024, tn=1024, extras=[Mat(h2)], epi=add_res)
    h3 = h3[0]
    h4, ffn2_saved = _ffn_forward("ffn2", h3, vec("ffn2_norm"), weights_of)

    gs = {}
    loss_tile, dh4, dh4_bf, gs["final_norm"] = _loss_head("loss_head", h4, vec("final_norm"), target)
    dh3, dh3_bf, gs["ffn2_norm"] = _ffn_backward(
        "ffn2", h3, vec("ffn2_norm"), ffn2_saved, dh4, dh4_bf, grads_ready, grads_flush)

    (do,) = _matmul("cross_do", Mat(dh3_bf), big["w_co"], "nt", [("c", 1, BF16)], tm=1024, tn=512)
    (dw_co,) = _matmul("cross_dwo", Mat(o), Mat(dh3_bf), "tn", [("r", N_CHIPS, BF16)],
                       tm=512, tn=1024)
    dq, dkv = _xattn_backward("cross_attn_bwd", q, kv, do[0])
    (dw_cq,) = _matmul("cross_dwq", Mat(n3), Mat(dq), "tn", [("r", N_CHIPS, BF16)],
                       tm=512, tn=1024)
    (dw_ckv,) = _matmul("cross_dwkv", Mat(memn), Mat(dkv), "tn", [("c", N_CHIPS, BF16)],
                        tm=1024, tn=1024)
    token = grads_ready("cross", {"w_cq": dw_cq, "w_ckv": dw_ckv, "w_co": dw_co})
    dq = _tie("cross_dq_after_swap", dq, [token])
    (dn3,) = _matmul("cross_dn", Mat(dq), big["w_cq"], "nt", [("c", 1, F32)], tm=1024, tn=512)
    (dmemn,) = _matmul("cross_dmem", Mat(dkv), big["w_ckv"], "nt", [("c", 1, F32)],
                       tm=256, tn=1024, tk=1024)
    (gs["mem_norm"],) = _rmsnorm_bwd("mem_dnorm", mem, vec("mem_norm"), dmemn[0], want_dx=False)
    dn3 = _tie("cross_dn_after_scatter", dn3, [grads_flush("cross", gs["mem_norm"])])
    dh2, dh2_bf, gs["cross_norm"] = _rmsnorm_bwd("cross_dnorm", h2, vec("cross_norm"), dn3[0],
                                                 dres=dh3)

    (dy,) = _matmul("mix_dy", Mat(dh2_bf), big["w_mix_out"], "nt", [("c", 1, F32)], tm=1024, tn=512)
    dy = dy[0]
    (dw_mix_out,) = _matmul("mix_dwout", Mat(y), Mat(dh2_bf), "tn", [("r", N_CHIPS, BF16)],
                            tm=512, tn=1024)
    dza, gs["ln_v_gain"], gs["ln_v_bias"], gs["spatial_w"], db, gs["gnorm_a"] = _sgu_backward(
        "sgu_bwd", z, dy, vec("ln_v_gain"), vec("ln_v_bias"), w_s, b_t, vec("gnorm_a"))
    gs["spatial_b"] = db.reshape(G, SGU_BLOCK)
    dob, gs["gnorm_b"] = _rmsnorm_bwd("gnorm_b_bwd", yb, vec("gnorm_b"), dy, dn_col=w_a // w_b,
                                      want_bf16=False)
    dqb, dkvb = _sb_backward("stickbreak_bwd", z, dob, sb_total, w_a, w_b)
    dz = jnp.concatenate([dza, dqb, dkvb[0].astype(BF16), dkvb[1].astype(BF16)], axis=1)
    (dw_mix_in,) = _matmul("mix_dwin", Mat(n2), Mat(dz), "tn", [("c", N_CHIPS, BF16)],
                           tm=1024, tn=1280)
    token = grads_ready("mix", {"w_mix_in": dw_mix_in, "w_mix_out": dw_mix_out})
    dz = _tie("mix_dz_after_swap", dz, [token])
    (dn2,) = _matmul("mix_dn", Mat(dz), big["w_mix_in"], "nt", [("c", 1, F32)],
                     tm=1024, tn=1024, tk=1280)
    dn2 = _tie("mix_dn_after_scatter", dn2, [grads_flush("mix", dn2)])
    dh1, dh1_bf, gs["mix_norm"] = _rmsnorm_bwd("mix_dnorm", h1, vec("mix_norm"), dn2[0], dres=dh2)

    dx, _, gs["ffn1_norm"] = _ffn_backward(
        "ffn1", x, vec("ffn1_norm"), ffn1_saved, dh1, dh1_bf, grads_ready, grads_flush,
        early_out=True)
    gs = {k: g.reshape(small[k].shape) for k, g in gs.items()}
    return loss_tile, dx, gs


def kernel(x, mem, ffn1_norm, ffn1_w_in, ffn1_w_out, mix_norm, w_mix_in, ln_v_gain, ln_v_bias, spatial_w, spatial_b, gnorm_a, gnorm_b, w_mix_out, cross_norm, mem_norm, w_cq, w_ckv, w_co, ffn2_norm, ffn2_w_in, ffn2_w_out, final_norm, loss_target, m_ffn1_norm, m_ffn1_w_in, m_ffn1_w_out, m_mix_norm, m_w_mix_in, m_ln_v_gain, m_ln_v_bias, m_spatial_w, m_spatial_b, m_gnorm_a, m_gnorm_b, m_w_mix_out, m_cross_norm, m_mem_norm, m_w_cq, m_w_ckv, m_w_co, m_ffn2_norm, m_ffn2_w_in, m_ffn2_w_out, m_final_norm, v_ffn1_norm, v_ffn1_w_in, v_ffn1_w_out, v_mix_norm, v_w_mix_in, v_ln_v_gain, v_ln_v_bias, v_spatial_w, v_spatial_b, v_gnorm_a, v_gnorm_b, v_w_mix_out, v_cross_norm, v_mem_norm, v_w_cq, v_w_ckv, v_w_co, v_ffn2_norm, v_ffn2_w_in, v_ffn2_w_out, v_final_norm):
    given = dict(locals())
    w = {k: given[k] for k in WEIGHTS}
    m = {k: given["m_" + k] for k in WEIGHTS}
    v = {k: given["v_" + k] for k in WEIGHTS}

    cx, cy, cc = lax.axis_index("x"), lax.axis_index("y"), lax.axis_index("c")
    place = jnp.stack([2 * cx + cy, cc]).astype(jnp.int32)

    names_of = dict(GATHER_GROUPS)
    own = {g: [_cast_own(f"cast_{k}", place, w[k][0]) for k in names] for g, names in GATHER_GROUPS}
    gathers = {}

    def start_gather(group, deps):
        first_hop = _near_copies if group in RELAYED else _gather_copies
        n_sems = (2 if group in RELAYED else 3) * len(own[group])
        send, recv, arrays, token = _split_start(f"gather_start_{group}", own[group],
                                                 first_hop, n_sems, deps)
        gathers[group] = (send, recv, arrays)
        return token

    start_tokens = [start_gather(g, ()) for g, after in GATHER_AFTER if after is None]
    start_tokens += [a for g, after in GATHER_AFTER if after is not None for a in own[g]]

    def weights_of(group, after):
        send, recv, arrays = gathers[group]
        if group in RELAYED:
            arrays = _split_wait(f"gather_wait_{group}", arrays, send, recv, after, _near_copies)
            send, recv, arrays, _ = _split_start(f"gather_relay_{group}", list(arrays),
                                                 _relay_copies, 2 * len(arrays))
            arrays = _split_wait(f"gather_relay_wait_{group}", arrays, send, recv, after,
                                 _relay_copies)
        else:
            arrays = _split_wait(f"gather_wait_{group}", arrays, send, recv, after, _gather_copies)
        tokens = [start_gather(g, (arrays[0],)) for g, a in GATHER_AFTER if a == group]
        arrays = _forward_to_sibling(f"gather_forward_{group}", list(arrays), tokens)
        return {k: Mat(a, BIG_KIND[k]) for k, a in zip(names_of[group], arrays)}

    swaps, scatters = {}, {}

    def grads_ready(group, partial):
        names = list(partial)
        grads_ = [partial[k] for k in names]
        lands = [lax.empty((g.shape[0], g.shape[1] // 2, g.shape[2]), g.dtype) for g in grads_]
        send, recv, arrays, token = _split_start(f"swap_start_{group}", grads_ + lands,
                                                 _swap_copies, len(names))
        swaps[group] = (names, send, recv, arrays)
        return token

    def grads_flush(group, after):
        names, send, recv, arrays = swaps[group]
        arrays = _split_wait(f"swap_wait_{group}", arrays, send, recv, after, _swap_copies)
        grads_, from_sibling = arrays[:len(names)], arrays[len(names):]
        sums = [_pair_sum(f"pair_sum_{k}", place, g, r)
                for k, g, r in zip(names, grads_, from_sibling)]
        lands = [lax.empty((3,) + s.shape[1:], s.dtype) for s in sums]
        send, recv, arrays, token = _split_start(f"scatter_start_{group}", sums + lands,
                                                 _scatter_copies, 3 * len(names))
        scatters[group] = (names, grads_, from_sibling, send, recv, arrays)
        return token

    small = {k: w[k] for k in SMALL}
    loss_tile, grad_x, gs = _local_step(x[0], mem[0], loss_target[0], small, weights_of,
                                        start_tokens, grads_ready, grads_flush)

    packed = _pack([gs[k] for k in SMALL] + [loss_tile])
    slots = jnp.zeros((N_DEV,) + packed.shape, packed.dtype)
    small_send, small_recv, small_arrays, _ = _split_start(
        "small_start", [packed, slots], _small_copies, N_DEV - 1)

    grad, delta, new_m, new_v = {}, {}, {}, {}
    shares = {}
    after = [grad_x]
    for stage, group in TAIL_STAGES:
        if stage == "sum":
            names, grads_, from_sibling, send, recv, arrays = scatters[group]
            arrays = _split_wait(f"scatter_wait_{group}", arrays, send, recv, after,
                                 _scatter_copies)
            from_chips = arrays[len(names):]
            shards = [_final_sum(f"final_sum_{k}", place, g, r, f)
                      for k, g, r, f in zip(names, grads_, from_sibling, from_chips)]
            send, recv, shards, token = _split_start(f"share_start_{group}", shards,
                                                     _share_copies, len(names))
            shares[group] = (names, send, recv, shards)
            after = [token]
        else:
            names, send, recv, shards = shares[group]
            shards = _split_wait(f"share_wait_{group}", shards, send, recv, after, _share_copies)
            after = []
            for k, g_ in zip(names, shards):
                g_, d_, m_, v_ = _adamw(f"adamw_{k}", w[k][0], g_, m[k][0], v[k][0])
                grad[k], delta[k], new_m[k], new_v[k] = g_[None], d_[None], m_[None], v_[None]
                after.append(v_)

    packed, slots = _split_wait("small_wait", small_arrays, small_send, small_recv, after,
                                _small_copies)
    me = (4 * cx + 2 * cy + cc).astype(jnp.int32).reshape(1)
    total = _sum_devices("sum_small", me, slots, packed)
    n_small = total.shape[0] - SUBLANE
    loss = total[n_small, 0]
    small_g = total[:n_small]
    g_s, d_s, m_s, v_s = _adamw("adamw_small", _pack([w[k] for k in SMALL]), small_g,
                                _pack([m[k] for k in SMALL]), _pack([v[k] for k in SMALL]))
    like = [w[k] for k in SMALL]
    for k, g_, d_, m_, v_ in zip(SMALL, _unpack(g_s, like), _unpack(d_s, like),
                                 _unpack(m_s, like), _unpack(v_s, like)):
        grad[k], delta[k], new_m[k], new_v[k] = g_, d_, m_, v_

    return (loss, grad_x[None], *[grad[k] for k in WEIGHTS], *[delta[k] for k in WEIGHTS],
            *[new_m[k] for k in WEIGHTS], *[new_v[k] for k in WEIGHTS])
```

```python
import functools
import math

import jax
import jax.numpy as jnp
from jax import lax
from jax.experimental import pallas as pl
from jax.experimental.pallas import tpu as pltpu

F32 = jnp.float32
BF16 = jnp.bfloat16
MESH = pl.DeviceIdType.MESH

EPS = 1e-6
CHUNK = 64
SGU_BLOCK = 128
GROUP_DIM = 128
X_HEADS = 4
N_CHIPS = 4
N_DEV = 8
LANE = 128
SUBLANE = 8
BF16_ROWS = 16

ADAM_LR = 0.001
ADAM_B1 = 0.9
ADAM_B2 = 0.999
ADAM_EPS = 1e-08
ADAM_WD = 0.01
ADAM_STEP = 10

V7X_VMEM_BYTES = 64 << 20
VMEM_LIMIT = V7X_VMEM_BYTES - (8 << 20)


def _params(n_grid):
    return pltpu.CompilerParams(dimension_semantics=("arbitrary",) * n_grid,
                                vmem_limit_bytes=VMEM_LIMIT)


def _pick(pref, dims, unit=None):
    g = functools.reduce(math.gcd, dims)
    if unit is None:
        unit = LANE if g % LANE == 0 else SUBLANE
    cands = [d for d in range(unit, g + 1, unit) if g % d == 0] or [g]
    return min(cands, key=lambda d: abs(math.log(d / pref)))


def _any_spec():
    return pl.BlockSpec(memory_space=pl.ANY)


class Mat:
    def __init__(self, arr, kind="c"):
        if arr.ndim == 2:
            arr = arr[None]
        self.arr, self.kind = arr, kind
        self.P, self.prow, self.pcol = arr.shape
        self.rows = self.prow * (self.P if kind == "r" else 1)
        self.cols = self.pcol * (self.P if kind == "c" else 1)
        self.dtype = arr.dtype

    def spec(self, tr, tc, rc_fn):
        if self.kind == "c":
            per = self.pcol // tc
            assert per * tc == self.pcol, (self.pcol, tc)

            def imap(*g):
                i, j = rc_fn(*g)
                return (j // per, i, j % per)
        else:
            per = self.prow // tr
            assert per * tr == self.prow, (self.prow, tr)

            def imap(*g):
                i, j = rc_fn(*g)
                return (i // per, i % per, j)
        return pl.BlockSpec((None, tr, tc), imap)

    def two_d(self):
        assert self.P == 1
        return self.arr[0]


def _out_mat(kind, P, rows, cols, dtype):
    shape = (P, rows, cols // P) if kind == "c" else (P, rows // P, cols)
    return jax.ShapeDtypeStruct(shape, dtype)


def _matmul(name, A, B, mode, outs, *, tm=1024, tn=1024, tk=2048, extras=(), epi=None):
    if mode == "nn":
        M, K, N = A.rows, A.cols, B.cols
        assert B.rows == K
    elif mode == "nt":
        M, K, N = A.rows, A.cols, B.rows
        assert B.cols == K
    else:
        K, M, N = A.rows, A.cols, B.cols
        assert B.rows == K
    mdims, ndims, kdims = [M], [N], [K]
    whole_b = mode == "nn" and B.kind == "r" and B.P > 1 and K <= tk
    if whole_b:
        kdims.append(A.pcol)
        ndims.append(B.pcol)
    elif mode == "tn":
        assert A.kind == "c" and B.kind == "c"
        mdims.append(A.pcol)
        ndims.append(B.pcol)
    else:
        (mdims if A.kind == "r" else kdims).append(A.prow if A.kind == "r" else A.pcol)
        if mode == "nn":
            (kdims if B.kind == "r" else ndims).append(B.prow if B.kind == "r" else B.pcol)
        else:
            (ndims if B.kind == "r" else kdims).append(B.prow if B.kind == "r" else B.pcol)
    for o in list(outs) + list(extras):
        if isinstance(o, Mat):
            (mdims if o.kind == "r" else ndims).append(o.prow if o.kind == "r" else o.pcol)
        elif isinstance(o[0], str):
            (mdims if o[0] == "r" else ndims).append((M if o[0] == "r" else N) // o[1])
    tm, tn = _pick(tm, mdims), _pick(tn, ndims)
    tk = K if mode == "tn" else _pick(tk, kdims)
    nk = K // tk
    grid = (M // tm, N // tn, nk)

    if mode == "tn":
        a_spec = A.spec(K, tm, lambda m, n, k: (0, m))
        b_spec = B.spec(K, tn, lambda m, n, k: (0, n))
    else:
        a_spec = A.spec(tm, tk, lambda m, n, k: (m, k))
        if whole_b:
            b_spec = pl.BlockSpec((B.P, B.prow, tn), lambda m, n, k: (0, 0, n))
        elif mode == "nn":
            b_spec = B.spec(tk, tn, lambda m, n, k: (k, n))
        else:
            b_spec = B.spec(tn, tk, lambda m, n, k: (n, k))

    def mn_spec(o):
        if isinstance(o, Mat):
            return o.spec(tm, tn, lambda m, n, k: (m, n))
        if isinstance(o[0], str):
            kind, P = o[0], o[1]
            fake = Mat.__new__(Mat)
            fake.kind, fake.P = kind, P
            fake.prow = M // P if kind == "r" else M
            fake.pcol = N // P if kind == "c" else N
            return Mat.spec(fake, tm, tn, lambda m, n, k: (m, n))
        return o[1](tm, tn)

    out_shapes = tuple(_out_mat(o[0], o[1], M, N, o[2]) if isinstance(o[0], str) else o[0]
                       for o in outs)
    out_specs = tuple(mn_spec(o) for o in outs)
    extra_arrays = tuple(e.arr if isinstance(e, Mat) else e[0] for e in extras)
    extra_specs = tuple(mn_spec(e) for e in extras)
    n_ex, n_out = len(extras), len(outs)
    tt = _pick(256, [tm])
    dims = (((1,), (1 if mode == "nt" else 0,)), ((), ()))

    def body(*refs):
        a_ref, b_ref = refs[:2]
        ex_refs = refs[2:2 + n_ex]
        out_refs = refs[2 + n_ex:2 + n_ex + n_out]
        scratch = refs[2 + n_ex + n_out:]
        if mode == "tn":
            at_ref = scratch[0]

            @pl.when(pl.program_id(1) == 0)
            def _():
                for c0 in range(0, tm, tt):
                    at_ref[c0:c0 + tt, :] = a_ref[:, c0:c0 + tt].astype(F32).T.astype(BF16)

            lhs = at_ref[...]
        else:
            lhs = a_ref[...].astype(BF16)
        rhs = b_ref[...].reshape(K, tn) if whole_b else b_ref[...]
        part = lax.dot_general(lhs, rhs.astype(BF16), dims, preferred_element_type=F32)

        def finish(acc):
            if epi is None:
                out_refs[0][...] = acc.astype(out_refs[0].dtype)
            else:
                epi(acc, ex_refs, out_refs)

        if nk == 1:
            finish(part)
        else:
            acc_ref = scratch[0]
            k = pl.program_id(2)

            @pl.when(k == 0)
            def _():
                acc_ref[...] = part

            @pl.when(k > 0)
            def _():
                acc_ref[...] += part

            @pl.when(k == nk - 1)
            def _():
                finish(acc_ref[...])

    scratch_shapes = []
    if mode == "tn":
        scratch_shapes.append(pltpu.VMEM((tm, K), BF16))
    elif nk > 1:
        scratch_shapes.append(pltpu.VMEM((tm, tn), F32))
    res = pl.pallas_call(
        body, name=name, grid=grid,
        in_specs=[a_spec, b_spec, *extra_specs], out_specs=out_specs, out_shape=out_shapes,
        scratch_shapes=scratch_shapes, compiler_params=_params(3),
    )(A.arr, B.arr, *extra_arrays)
    return res


def _row_tile(T):
    return _pick(256, [T])


def _tie(name, x, deps):
    def body(*refs):
        refs[-1][...] = jnp.zeros_like(refs[-1])

    return pl.pallas_call(
        body, name=name, in_specs=[_any_spec()] * (1 + len(deps)),
        out_specs=(_any_spec(), pl.BlockSpec(memory_space=pltpu.VMEM)),
        out_shape=(jax.ShapeDtypeStruct(x.shape, x.dtype),
                   jax.ShapeDtypeStruct((SUBLANE, LANE), F32)),
        input_output_aliases={0: 0},
    )(x, *deps)[0]


def _rmsnorm_fwd(name, x, g, *, into=None, col=0, deps=()):
    T, W = x.shape
    tr = _row_tile(T)

    def body(x_ref, g_ref, *rest):
        o_ref = rest[-1]
        xv = x_ref[...]
        rstd = lax.rsqrt(jnp.mean(xv * xv, axis=-1, keepdims=True) + EPS)
        o_ref[...] = (xv * rstd * g_ref[...]).astype(o_ref.dtype)

    in_specs = [pl.BlockSpec((tr, W), lambda i: (i, 0)), pl.BlockSpec((1, W), lambda i: (0, 0))]
    args = [x, g]
    kwargs = {}
    if into is None:
        out_shape = jax.ShapeDtypeStruct((T, W), BF16)
    else:
        out_shape = jax.ShapeDtypeStruct(into.shape, into.dtype)
        in_specs.append(_any_spec())
        args.append(into)
        kwargs["input_output_aliases"] = {2: 0}
    in_specs += [_any_spec()] * len(deps)
    args += list(deps)
    return pl.pallas_call(
        body, name=name, grid=(T // tr,), in_specs=in_specs,
        out_specs=pl.BlockSpec((tr, W), lambda i: (i, col)), out_shape=out_shape,
        compiler_params=_params(1), **kwargs)(*args)


def _rmsnorm_bwd(name, x, g, dn, *, dn_col=0, dres=None, want_dx=True, want_bf16=True):
    T, W = x.shape
    tr = _row_tile(T)
    has_res = dres is not None

    def body(*refs):
        x_ref, g_ref, dn_ref = refs[:3]
        pos = 3
        dres_ref = None
        if has_res:
            dres_ref = refs[pos]
            pos += 1
        outs = refs[pos:]
        dg_ref = outs[-1]
        xv = x_ref[...]
        rstd = lax.rsqrt(jnp.mean(xv * xv, axis=-1, keepdims=True) + EPS)
        xhat = xv * rstd
        dnv = dn_ref[...].astype(F32)

        @pl.when(pl.program_id(0) == 0)
        def _():
            dg_ref[...] = jnp.zeros_like(dg_ref)

        dg_ref[...] += jnp.sum(dnv * xhat, axis=0, keepdims=True)
        if want_dx:
            t = dnv * g_ref[...]
            dx = rstd * (t - xhat * jnp.mean(t * xhat, axis=-1, keepdims=True))
            if has_res:
                dx = dx + dres_ref[...]
            outs[0][...] = dx
            if want_bf16:
                outs[1][...] = dx.astype(BF16)

    row = pl.BlockSpec((tr, W), lambda i: (i, 0))
    in_specs = [row, pl.BlockSpec((1, W), lambda i: (0, 0)),
                pl.BlockSpec((tr, W), lambda i: (i, dn_col))]
    args = [x, g, dn]
    if has_res:
        in_specs.append(row)
        args.append(dres)
    out_shape, out_specs = [], []
    if want_dx:
        out_shape.append(jax.ShapeDtypeStruct((T, W), F32))
        out_specs.append(row)
        if want_bf16:
            out_shape.append(jax.ShapeDtypeStruct((T, W), BF16))
            out_specs.append(row)
    out_shape.append(jax.ShapeDtypeStruct((1, W), F32))
    out_specs.append(pl.BlockSpec((1, W), lambda i: (0, 0)))
    return pl.pallas_call(
        body, name=name, grid=(T // tr,), in_specs=in_specs, out_specs=out_specs,
        out_shape=out_shape, compiler_params=_params(1))(*args)


def _loss_head(name, h, g, target):
    T, W = h.shape
    tr = _row_tile(T)

    def body(h_ref, g_ref, t_ref, loss_ref, dx_ref, dxb_ref, dg_ref):
        xv = h_ref[...]
        gv = g_ref[...]
        rstd = lax.rsqrt(jnp.mean(xv * xv, axis=-1, keepdims=True) + EPS)
        xhat = xv * rstd
        diff = xhat * gv - t_ref[...]

        @pl.when(pl.program_id(0) == 0)
        def _():
            dg_ref[...] = jnp.zeros_like(dg_ref)
            loss_ref[...] = jnp.zeros_like(loss_ref)

        loss_ref[...] += 0.5 * jnp.sum(jnp.mean(diff * diff, axis=-1, keepdims=True))
        dnv = diff * (1.0 / W)
        dg_ref[...] += jnp.sum(dnv * xhat, axis=0, keepdims=True)
        t = dnv * gv
        dx = rstd * (t - xhat * jnp.mean(t * xhat, axis=-1, keepdims=True))
        dx_ref[...] = dx
        dxb_ref[...] = dx.astype(BF16)

    row = pl.BlockSpec((tr, W), lambda i: (i, 0))
    vec = pl.BlockSpec((1, W), lambda i: (0, 0))
    return pl.pallas_call(
        body, name=name, grid=(T // tr,), in_specs=[row, vec, row],
        out_specs=[pl.BlockSpec((SUBLANE, LANE), lambda i: (0, 0)), row, row, vec],
        out_shape=[jax.ShapeDtypeStruct((SUBLANE, LANE), F32), jax.ShapeDtypeStruct((T, W), F32),
                   jax.ShapeDtypeStruct((T, W), BF16), jax.ShapeDtypeStruct((1, W), F32)],
        compiler_params=_params(1))(h, g, target)


def _sigmoid(x):
    return 1.0 / (1.0 + jnp.exp(-x))


def _ffn_in(name, n, W):
    T, D = n.shape
    F = W.cols // 2
    tm = _pick(2048, [T])
    tn = _pick(512, [W.pcol])
    per = W.pcol // tn

    def body(a_ref, wg_ref, wu_ref, gu_ref, act_ref):
        a = a_ref[...]
        gate = jnp.dot(a, wg_ref[...], preferred_element_type=F32)
        up = jnp.dot(a, wu_ref[...], preferred_element_type=F32)
        gu_ref[0] = gate.astype(BF16)
        gu_ref[1] = up.astype(BF16)
        act_ref[...] = (gate * _sigmoid(gate) * up).astype(BF16)

    return pl.pallas_call(
        body, name=name, grid=(T // tm, F // tn),
        in_specs=[pl.BlockSpec((tm, D), lambda m, j: (m, 0)),
                  pl.BlockSpec((None, D, tn), lambda m, j: (j // per, 0, j % per)),
                  pl.BlockSpec((None, D, tn), lambda m, j: (2 + j // per, 0, j % per))],
        out_specs=[pl.BlockSpec((2, tm, tn), lambda m, j: (0, m, j)),
                   pl.BlockSpec((tm, tn), lambda m, j: (m, j))],
        out_shape=[jax.ShapeDtypeStruct((2, T, F), BF16), jax.ShapeDtypeStruct((T, F), BF16)],
        compiler_params=_params(2))(n, W.arr, W.arr)


def _ffn_forward(tag, h, norm_g, weights_of, deps=()):
    n = _rmsnorm_fwd(f"{tag}_norm", h, norm_g, deps=deps)
    w_in = weights_of(f"{tag}_in", n)[f"{tag}_w_in"]
    gu, act = _ffn_in(f"{tag}_in", n, w_in)
    w_out = weights_of(f"{tag}_out", act)[f"{tag}_w_out"]

    def epi(acc, ex, out):
        out[0][...] = ex[0][...] + 0.5 * acc

    (h_out,) = _matmul(f"{tag}_out", Mat(act), w_out, "nn", [("c", 1, F32)],
                       tm=1024, tn=512, tk=8192, extras=[Mat(h)], epi=epi)
    return h_out[0], (n, gu, act, w_in, w_out)


def _ffn_backward(tag, h_in, norm_g, saved, dh, dh_bf, grads_ready, grads_flush,
                  early_out=False):
    n, gu, act, w_in, w_out = saved
    T, F = act.shape

    def epi(acc, ex, out):
        dact = 0.5 * acc
        gate = ex[0][0].astype(F32)
        up = ex[0][1].astype(F32)
        sig = _sigmoid(gate)
        out[0][0] = (dact * up * sig * (1.0 + gate * (1.0 - sig))).astype(BF16)
        out[0][1] = (dact * gate * sig).astype(BF16)

    def pair_spec(tm, tn):
        return pl.BlockSpec((2, tm, tn), lambda m, j, k: (0, m, j))

    def half(acc, ex, out):
        out[0][...] = (0.5 * acc).astype(out[0].dtype)

    (dw_out,) = _matmul(f"{tag}_dwout", Mat(act), Mat(dh_bf), "tn", [("r", N_CHIPS, BF16)],
                        tm=1408, tn=512, epi=half)
    if early_out:
        token = grads_ready(f"{tag}_out", {f"{tag}_w_out": dw_out})
        dh_bf = _tie(f"{tag}_dh_after_swap", dh_bf, [token])
    (dgu,) = _matmul(f"{tag}_dact", Mat(dh_bf), w_out, "nt",
                     [(jax.ShapeDtypeStruct((2, T, F), BF16), pair_spec)],
                     tm=512, tn=1408, extras=[(gu, pair_spec)], epi=epi)
    if early_out:
        dgu = _tie(f"{tag}_dgu_after_scatter", dgu, [grads_flush(f"{tag}_out", dgu)])
    (dw_in,) = _matmul(f"{tag}_dwin", Mat(n), Mat(dgu), "tn", [("c", N_CHIPS, BF16)],
                       tm=1024, tn=1408)
    if early_out:
        group, partial = f"{tag}_in", {f"{tag}_w_in": dw_in}
    else:
        group, partial = tag, {f"{tag}_w_in": dw_in, f"{tag}_w_out": dw_out}
    dgu = _tie(f"{tag}_dgu_after_swap", dgu, [grads_ready(group, partial)])
    (dn,) = _matmul(f"{tag}_dn", Mat(dgu), w_in, "nt", [("c", 1, F32)],
                    tm=1024, tn=1024, tk=2816)
    dn = _tie(f"{tag}_dn_after_scatter", dn, [grads_flush(group, dn)])
    return _rmsnorm_bwd(f"{tag}_dnorm", h_in, norm_g, dn[0], dres=dh)


_GELU_C = math.sqrt(2.0 / math.pi)
_GELU_A = 0.044715


def _gelu(x):
    return 0.5 * x * (1.0 + jnp.tanh(_GELU_C * (x + _GELU_A * x * x * x)))


def _gelu_grad(x):
    th = jnp.tanh(_GELU_C * (x + _GELU_A * x * x * x))
    return 0.5 * (1.0 + th) + 0.5 * x * (1.0 - th * th) * _GELU_C * (1.0 + 3.0 * _GELU_A * x * x)


def _chunk_mask():
    t = lax.broadcasted_iota(jnp.int32, (SGU_BLOCK, SGU_BLOCK), 0) // CHUNK
    s = lax.broadcasted_iota(jnp.int32, (SGU_BLOCK, SGU_BLOCK), 1) // CHUNK
    return s <= t


def _sgu_group_forward(v_g, lg, lb, wm_bf, b_col):
    mu = jnp.mean(v_g, axis=-1, keepdims=True)
    xc = v_g - mu
    rstd = lax.rsqrt(jnp.mean(xc * xc, axis=-1, keepdims=True) + EPS)
    vhat = xc * rstd
    vn = vhat * lg + lb
    mixed = jnp.dot(wm_bf, vn.astype(BF16), preferred_element_type=F32) + b_col
    return vhat, rstd, vn, mixed


def _sgu_forward(name, z, ln_g, ln_b, w_s, b_t, gn, d_model):
    T = z.shape[0]
    W_A = ln_g.shape[1]
    G = W_A // GROUP_DIM

    def body(z_ref, lg_ref, lb_ref, w_ref, bt_ref, gn_ref, y_ref):
        mask = _chunk_mask()
        u = _gelu(z_ref[:, :W_A])
        v = _gelu(z_ref[:, W_A:])
        cols = []
        for g in range(G):
            sl = slice(g * GROUP_DIM, (g + 1) * GROUP_DIM)
            wm = jnp.where(mask, w_ref[g], 0.0).astype(BF16)
            _, _, _, mixed = _sgu_group_forward(v[:, sl], lg_ref[:, sl], lb_ref[:, sl], wm,
                                                bt_ref[:, g:g + 1])
            cols.append(u[:, sl] * mixed)
        ya = jnp.concatenate(cols, axis=1)
        rstd = lax.rsqrt(jnp.mean(ya * ya, axis=-1, keepdims=True) + EPS)
        y_ref[...] = (ya * rstd * gn_ref[...]).astype(BF16)

    vec = pl.BlockSpec((1, W_A), lambda i: (0, 0))
    return pl.pallas_call(
        body, name=name, grid=(T // SGU_BLOCK,),
        in_specs=[pl.BlockSpec((SGU_BLOCK, 2 * W_A), lambda i: (i, 0)), vec, vec,
                  pl.BlockSpec((G, SGU_BLOCK, SGU_BLOCK), lambda i: (0, 0, 0)),
                  pl.BlockSpec((SGU_BLOCK, G), lambda i: (0, 0)), vec],
        out_specs=pl.BlockSpec((SGU_BLOCK, W_A), lambda i: (i, 0)),
        out_shape=jax.ShapeDtypeStruct((T, d_model), BF16),
        compiler_params=_params(1))(z, ln_g, ln_b, w_s, b_t, gn)


def _sgu_backward(name, z, dy, ln_g, ln_b, w_s, b_t, gn):
    T = z.shape[0]
    W_A = ln_g.shape[1]
    G = W_A // GROUP_DIM

    def body(z_ref, dy_ref, lg_ref, lb_ref, w_ref, bt_ref, gn_ref,
             dz_ref, dlg_ref, dlb_ref, dw_ref, db_ref, dgn_ref):
        @pl.when(pl.program_id(0) == 0)
        def _():
            for r in (dlg_ref, dlb_ref, dw_ref, db_ref, dgn_ref):
                r[...] = jnp.zeros_like(r)

        mask = _chunk_mask()
        zu = z_ref[:, :W_A]
        zv = z_ref[:, W_A:]
        u = _gelu(zu)
        v = _gelu(zv)
        saved, cols = [], []
        for g in range(G):
            sl = slice(g * GROUP_DIM, (g + 1) * GROUP_DIM)
            wm = jnp.where(mask, w_ref[g], 0.0)
            vhat, rstd, vn, mixed = _sgu_group_forward(
                v[:, sl], lg_ref[:, sl], lb_ref[:, sl], wm.astype(BF16), bt_ref[:, g:g + 1])
            saved.append((wm, vhat, rstd, vn, mixed))
            cols.append(u[:, sl] * mixed)
        ya = jnp.concatenate(cols, axis=1)
        rstd_a = lax.rsqrt(jnp.mean(ya * ya, axis=-1, keepdims=True) + EPS)
        ya_hat = ya * rstd_a
        dyv = dy_ref[...].astype(F32)
        dgn_ref[...] += jnp.sum(dyv * ya_hat, axis=0, keepdims=True)
        t = dyv * gn_ref[...]
        dya = rstd_a * (t - ya_hat * jnp.mean(t * ya_hat, axis=-1, keepdims=True))
        du_cols, dv_cols, dlg_cols, dlb_cols = [], [], [], []
        for g in range(G):
            sl = slice(g * GROUP_DIM, (g + 1) * GROUP_DIM)
            wm, vhat, rstd, vn, mixed = saved[g]
            dya_g = dya[:, sl]
            du_cols.append(dya_g * mixed)
            dmix = dya_g * u[:, sl]
            dmix_bf = dmix.astype(BF16)
            db_ref[g] += jnp.sum(dmix, axis=1, keepdims=True)
            dw = lax.dot_general(dmix_bf, vn.astype(BF16), (((1,), (1,)), ((), ())),
                                 preferred_element_type=F32)
            dw_ref[g] += jnp.where(mask, dw, 0.0)
            dvn = jnp.dot(wm.T.astype(BF16), dmix_bf, preferred_element_type=F32)
            dlg_cols.append(jnp.sum(dvn * vhat, axis=0, keepdims=True))
            dlb_cols.append(jnp.sum(dvn, axis=0, keepdims=True))
            dvhat = dvn * lg_ref[:, sl]
            dv_cols.append(rstd * (dvhat - jnp.mean(dvhat, axis=-1, keepdims=True)
                                   - vhat * jnp.mean(dvhat * vhat, axis=-1, keepdims=True)))
        dlg_ref[...] += jnp.concatenate(dlg_cols, axis=1)
        dlb_ref[...] += jnp.concatenate(dlb_cols, axis=1)
        dz_ref[:, :W_A] = (jnp.concatenate(du_cols, axis=1) * _gelu_grad(zu)).astype(BF16)
        dz_ref[:, W_A:] = (jnp.concatenate(dv_cols, axis=1) * _gelu_grad(zv)).astype(BF16)

    vec = pl.BlockSpec((1, W_A), lambda i: (0, 0))
    wspec = pl.BlockSpec((G, SGU_BLOCK, SGU_BLOCK), lambda i: (0, 0, 0))
    return pl.pallas_call(
        body, name=name, grid=(T // SGU_BLOCK,),
        in_specs=[pl.BlockSpec((SGU_BLOCK, 2 * W_A), lambda i: (i, 0)),
                  pl.BlockSpec((SGU_BLOCK, W_A), lambda i: (i, 0)), vec, vec, wspec,
                  pl.BlockSpec((SGU_BLOCK, G), lambda i: (0, 0)), vec],
        out_specs=[pl.BlockSpec((SGU_BLOCK, 2 * W_A), lambda i: (i, 0)), vec, vec, wspec,
                   pl.BlockSpec((G, SGU_BLOCK, 1), lambda i: (0, 0, 0)), vec],
        out_shape=[jax.ShapeDtypeStruct((T, 2 * W_A), BF16), jax.ShapeDtypeStruct((1, W_A), F32),
                   jax.ShapeDtypeStruct((1, W_A), F32),
                   jax.ShapeDtypeStruct((G, SGU_BLOCK, SGU_BLOCK), F32),
                   jax.ShapeDtypeStruct((G, SGU_BLOCK, 1), F32),
                   jax.ShapeDtypeStruct((1, W_A), F32)],
        compiler_params=_params(1))(z, dy, ln_g, ln_b, w_s, b_t, gn)


def _split_dot(x, tri):
    hi = x.astype(BF16)
    lo = (x - hi.astype(F32)).astype(BF16)
    return (jnp.dot(hi, tri, preferred_element_type=F32)
            + jnp.dot(lo, tri, preferred_element_type=F32))


def _tri(n, rel):
    r = lax.broadcasted_iota(jnp.int32, (n, n), 0)
    c = lax.broadcasted_iota(jnp.int32, (n, n), 1)
    return rel(r, c).astype(BF16)


def _dot_nt(a, b):
    return lax.dot_general(a, b, (((1,), (1,)), ((), ())), preferred_element_type=F32)


def _dot_tn(a, b):
    return lax.dot_general(a, b, (((0,), (0,)), ((), ())), preferred_element_type=F32)


def _sb_scores(qs, kj, mask):
    zz = _dot_nt(qs, kj)
    log_beta = jnp.minimum(zz, 0.0) - jnp.log(1.0 + jnp.exp(-jnp.abs(zz)))
    log_1m = log_beta - zz
    if mask is not None:
        log_1m = jnp.where(mask, log_1m, 0.0)
    return log_beta, log_1m


def _masked(mask, x):
    return x if mask is None else jnp.where(mask, x, 0.0)


def _sb_tiles(T):
    tk = _pick(256, [T])
    tq = 2 * tk if T % (2 * tk) == 0 else tk
    return tq, tk


def _sb_cols(w_a, w_b):
    base = 2 * w_a // GROUP_DIM
    per = w_b // GROUP_DIM
    return base, base + per, base + 2 * per


def _sb_forward(name, z, w_a, w_b):
    T = z.shape[0]
    H = w_b // GROUP_DIM
    tq, tk = _sb_tiles(T)
    per = tq // tk
    qc, kc, vc = _sb_cols(w_a, w_b)
    scale = GROUP_DIM ** -0.5

    def body(q_ref, k_ref, v_ref, y_ref, tot_ref):
        i = pl.program_id(1)
        qs = (q_ref[...] * scale).astype(BF16)
        upper = _tri(tk, lambda r, c: r > c)
        ahead = (lax.broadcasted_iota(jnp.int32, (tq, tk), 1)
                 - lax.broadcasted_iota(jnp.int32, (tq, tk), 0))

        def step(j, carry, masked):
            acc, later = carry
            k0 = pl.multiple_of(j * tk, tk)
            kj = k_ref[pl.ds(k0, tk), :].astype(BF16)
            vj = v_ref[pl.ds(k0, tk), :].astype(BF16)
            mask = ahead < i * tq - k0 if masked else None
            log_beta, log_1m = _sb_scores(qs, kj, mask)
            rest = _split_dot(log_1m, upper) + later
            a = _masked(mask, jnp.exp(log_beta + rest))
            acc = acc + jnp.dot(a.astype(BF16), vj, preferred_element_type=F32)
            return acc, later + jnp.sum(log_1m, axis=1, keepdims=True)

        carry = (jnp.zeros((tq, GROUP_DIM), F32), jnp.zeros((tq, 1), F32))
        first = i * per
        for d in reversed(range(per)):
            carry = step(first + d, carry, True)
        acc, total = lax.fori_loop(0, first, lambda jj, c: step(first - 1 - jj, c, False), carry)
        y_ref[...] = acc
        tot_ref[...] = total

    return pl.pallas_call(
        body, name=name, grid=(H, T // tq),
        in_specs=[pl.BlockSpec((tq, GROUP_DIM), lambda h, i: (i, qc + h)),
                  pl.BlockSpec((T, GROUP_DIM), lambda h, i: (0, kc + h)),
                  pl.BlockSpec((T, GROUP_DIM), lambda h, i: (0, vc + h))],
        out_specs=[pl.BlockSpec((tq, GROUP_DIM), lambda h, i: (i, h)),
                   pl.BlockSpec((None, tq, 1), lambda h, i: (h, i, 0))],
        out_shape=[jax.ShapeDtypeStruct((T, w_b), F32), jax.ShapeDtypeStruct((H, T, 1), F32)],
        compiler_params=_params(2))(z, z, z)


def _sb_backward(name, z, do, total, w_a, w_b):
    T = z.shape[0]
    H = w_b // GROUP_DIM
    tq, tk = _sb_tiles(T)
    per = tq // tk
    qc, kc, vc = _sb_cols(w_a, w_b)
    scale = GROUP_DIM ** -0.5

    def body(q_ref, k_ref, v_ref, do_ref, tot_ref, dq_ref, dkv_ref):
        i = pl.program_id(1)

        @pl.when(i == 0)
        def _():
            dkv_ref[...] = jnp.zeros_like(dkv_ref)

        qs = (q_ref[...] * scale).astype(BF16)
        dob = do_ref[...].astype(BF16)
        upto = _tri(tk, lambda r, c: r <= c)
        before = _tri(tk, lambda r, c: r < c)
        ahead = (lax.broadcasted_iota(jnp.int32, (tq, tk), 1)
                 - lax.broadcasted_iota(jnp.int32, (tq, tk), 0))

        def step(j, carry, masked):
            dq, left, e_seen = carry
            k0 = pl.multiple_of(j * tk, tk)
            kj = k_ref[pl.ds(k0, tk), :].astype(BF16)
            vj = v_ref[pl.ds(k0, tk), :].astype(BF16)
            mask = ahead < i * tq - k0 if masked else None
            log_beta, log_1m = _sb_scores(qs, kj, mask)
            rest = left - _split_dot(log_1m, upto)
            a = _masked(mask, jnp.exp(log_beta + rest))
            e = a * _dot_nt(dob, vj)
            e_before = e_seen + jnp.dot(e.astype(BF16), before, preferred_element_type=F32)
            beta = jnp.exp(log_beta)
            dz = _masked(mask, e * (1.0 - beta) - beta * e_before).astype(BF16)
            dq = dq + jnp.dot(dz, kj, preferred_element_type=F32)
            dkv_ref[0, pl.ds(k0, tk), :] += _dot_tn(dz, qs)
            dkv_ref[1, pl.ds(k0, tk), :] += _dot_tn(a.astype(BF16), dob)
            return (dq, left - jnp.sum(log_1m, axis=1, keepdims=True),
                    e_seen + jnp.sum(e, axis=1, keepdims=True))

        first = i * per
        carry = (jnp.zeros((tq, GROUP_DIM), F32), tot_ref[...], jnp.zeros((tq, 1), F32))
        carry = lax.fori_loop(0, first, lambda j, c: step(j, c, False), carry)
        for d in range(per):
            carry = step(first + d, carry, True)
        dq_ref[...] = (carry[0] * scale).astype(BF16)

    return pl.pallas_call(
        body, name=name, grid=(H, T // tq),
        in_specs=[pl.BlockSpec((tq, GROUP_DIM), lambda h, i: (i, qc + h)),
                  pl.BlockSpec((T, GROUP_DIM), lambda h, i: (0, kc + h)),
                  pl.BlockSpec((T, GROUP_DIM), lambda h, i: (0, vc + h)),
                  pl.BlockSpec((tq, GROUP_DIM), lambda h, i: (i, h)),
                  pl.BlockSpec((None, tq, 1), lambda h, i: (h, i, 0))],
        out_specs=[pl.BlockSpec((tq, GROUP_DIM), lambda h, i: (i, h)),
                   pl.BlockSpec((2, T, GROUP_DIM), lambda h, i: (0, 0, h))],
        out_shape=[jax.ShapeDtypeStruct((T, w_b), BF16), jax.ShapeDtypeStruct((2, T, w_b), F32)],
        compiler_params=_params(2))(z, z, z, do, total)


def _softmax_rows(s):
    m = jnp.max(s, axis=-1, keepdims=True)
    p = jnp.exp(s - m)
    return p / jnp.sum(p, axis=-1, keepdims=True)


def _xattn_forward(name, q, kv):
    T, D = q.shape
    Nm = kv.shape[0]
    dh = D // X_HEADS
    tq = _pick(512, [T])

    def body(q_ref, k_ref, v_ref, o_ref):
        p = _softmax_rows(_dot_nt(q_ref[...], k_ref[...]))
        o_ref[...] = jnp.dot(p.astype(BF16), v_ref[...], preferred_element_type=F32).astype(BF16)

    return pl.pallas_call(
        body, name=name, grid=(T // tq, X_HEADS),
        in_specs=[pl.BlockSpec((tq, dh), lambda i, h: (i, h)),
                  pl.BlockSpec((Nm, dh), lambda i, h: (0, h)),
                  pl.BlockSpec((Nm, dh), lambda i, h: (0, X_HEADS + h))],
        out_specs=pl.BlockSpec((tq, dh), lambda i, h: (i, h)),
        out_shape=jax.ShapeDtypeStruct((T, D), BF16),
        compiler_params=_params(2))(q, kv, kv)


def _xattn_backward(name, q, kv, do):
    T, D = q.shape
    Nm = kv.shape[0]
    dh = D // X_HEADS
    tq = _pick(512, [T])
    scale = dh ** -0.5

    def body(q_ref, k_ref, v_ref, do_ref, dq_ref, dkv_ref):
        @pl.when(pl.program_id(1) == 0)
        def _():
            dkv_ref[...] = jnp.zeros_like(dkv_ref)

        qv, kk, vv, dov = q_ref[...], k_ref[...], v_ref[...], do_ref[...]
        p = _softmax_rows(_dot_nt(qv, kk))
        dp = _dot_nt(dov, vv)
        ds = (p * (dp - jnp.sum(dp * p, axis=-1, keepdims=True))).astype(BF16)
        dq_ref[...] = (jnp.dot(ds, kk, preferred_element_type=F32) * scale).astype(BF16)
        dkv_ref[0] += _dot_tn(ds, qv)
        dkv_ref[1] += _dot_tn(p.astype(BF16), dov)

    blk = pl.BlockSpec((tq, dh), lambda h, i: (i, h))
    return pl.pallas_call(
        body, name=name, grid=(X_HEADS, T // tq),
        in_specs=[blk, pl.BlockSpec((Nm, dh), lambda h, i: (0, h)),
                  pl.BlockSpec((Nm, dh), lambda h, i: (0, X_HEADS + h)), blk],
        out_specs=[blk, pl.BlockSpec((2, Nm, dh), lambda h, i: (0, 0, h))],
        out_shape=[jax.ShapeDtypeStruct((T, D), BF16), jax.ShapeDtypeStruct((2, Nm, D), F32)],
        compiler_params=_params(2))(q, kv, kv, do)


def _position():
    x, y, c = lax.axis_index("x"), lax.axis_index("y"), lax.axis_index("c")
    other_chips = [(1 - x, y), (x, 1 - y), (1 - x, 1 - y)]
    return x, y, c, other_chips


def _hbm_spec():
    return pl.BlockSpec(memory_space=pltpu.HBM)


def _sem_spec():
    return pl.BlockSpec(memory_space=pltpu.SEMAPHORE)


def _split_start(name, arrays, make_copies, n_sems, deps=()):
    n, d = len(arrays), len(deps)

    def body(*refs):
        ins = refs[:n]
        send_sems, recv_sems = refs[n + d], refs[n + d + 1]
        token = refs[-1]
        for cp in make_copies(ins, send_sems, recv_sems):
            cp.start()
        token[...] = jnp.zeros_like(token)

    res = pl.pallas_call(
        body, name=name,
        out_shape=(pltpu.SemaphoreType.DMA((n_sems,)), pltpu.SemaphoreType.DMA((n_sems,)),
                   *[pltpu.HBM(a.shape, a.dtype) for a in arrays],
                   jax.ShapeDtypeStruct((SUBLANE, LANE), F32)),
        in_specs=[_hbm_spec()] * n + [_any_spec()] * d,
        out_specs=(_sem_spec(), _sem_spec(), *[_hbm_spec()] * n,
                   pl.BlockSpec(memory_space=pltpu.VMEM)),
        input_output_aliases={i: 2 + i for i in range(n)},
        compiler_params=pltpu.CompilerParams(
            has_side_effects=pltpu.SideEffectType.DATAFLOW_SIDE_EFFECTING),
    )(*[pltpu.with_memory_space_constraint(a, pltpu.HBM) for a in arrays], *deps)
    return res[0], res[1], list(res[2:2 + n]), res[-1]


def _split_wait(name, arrays, send_sems, recv_sems, after, make_copies):
    n = len(arrays)
    after = list(after) if isinstance(after, (list, tuple)) else [after]

    def body(*refs):
        ins = refs[:n]
        send_ref, recv_ref = refs[n], refs[n + 1]
        for cp in make_copies(ins, send_ref, recv_ref):
            cp.wait_send()
            cp.wait_recv()

    return pl.pallas_call(
        body, name=name,
        out_shape=tuple(pltpu.HBM(a.shape, a.dtype) for a in arrays),
        in_specs=[_hbm_spec()] * n + [_sem_spec(), _sem_spec()] + [_any_spec()] * len(after),
        out_specs=tuple(_hbm_spec() for _ in arrays),
        input_output_aliases={i: i for i in range(n)},
        compiler_params=pltpu.CompilerParams(
            has_side_effects=pltpu.SideEffectType.DATAFLOW_SIDE_EFFECTING),
    )(*arrays, send_sems, recv_sems, *after)


def _gather_copies(refs, send_sems, recv_sems):
    x, y, c, chips = _position()
    me = 2 * x + y
    copies = []
    for i, ref in enumerate(refs):
        rows = ref.shape[1] // 2
        piece = ref.at[me, pl.ds(c * rows, rows), :]
        for j, (px, py) in enumerate(chips):
            copies.append(pltpu.make_async_remote_copy(
                src_ref=piece, dst_ref=piece, send_sem=send_sems.at[3 * i + j],
                recv_sem=recv_sems.at[3 * i + j], device_id=(px, py, c), device_id_type=MESH))
    return copies


def _near_copies(refs, send_sems, recv_sems):
    x, y, c, chips = _position()
    me = 2 * x + y
    copies = []
    for i, ref in enumerate(refs):
        rows = ref.shape[1] // 2
        piece = ref.at[me, pl.ds(c * rows, rows), :]
        for j, (px, py) in enumerate(chips[:2]):
            copies.append(pltpu.make_async_remote_copy(
                src_ref=piece, dst_ref=piece, send_sem=send_sems.at[2 * i + j],
                recv_sem=recv_sems.at[2 * i + j], device_id=(px, py, c), device_id_type=MESH))
    return copies


def _relay_copies(refs, send_sems, recv_sems):
    x, y, c, chips = _position()
    copies = []
    for i, ref in enumerate(refs):
        rows = ref.shape[1] // 4
        for j, (px, py) in enumerate(chips[:2]):
            ox, oy = chips[1 - j]
            piece = ref.at[2 * ox + oy, pl.ds((2 * c + j) * rows, rows), :]
            copies.append(pltpu.make_async_remote_copy(
                src_ref=piece, dst_ref=piece, send_sem=send_sems.at[2 * i + j],
                recv_sem=recv_sems.at[2 * i + j], device_id=(px, py, c), device_id_type=MESH))
    return copies


def _share_copies(refs, send_sems, recv_sems):
    x, y, c, _ = _position()
    copies = []
    for i, ref in enumerate(refs):
        rows = ref.shape[0] // 2
        mine = ref.at[pl.ds(c * rows, rows), :]
        copies.append(pltpu.make_async_remote_copy(
            src_ref=mine, dst_ref=mine, send_sem=send_sems.at[i], recv_sem=recv_sems.at[i],
            device_id=(x, y, 1 - c), device_id_type=MESH))
    return copies


def _scatter_copies(refs, send_sems, recv_sems):
    x, y, c, chips = _position()
    n = len(refs) // 2
    copies = []
    for i in range(n):
        for j, (px, py) in enumerate(chips):
            copies.append(pltpu.make_async_remote_copy(
                src_ref=refs[i].at[2 * px + py], dst_ref=refs[n + i].at[j],
                send_sem=send_sems.at[3 * i + j], recv_sem=recv_sems.at[3 * i + j],
                device_id=(px, py, c), device_id_type=MESH))
    return copies


def _cast_own(name, place, shard):
    rows, cols = shard.shape
    tr = _block_rows(rows, cols)

    def body(place_ref, w_ref, o_ref):
        o_ref[...] = w_ref[...].astype(BF16)

    grid_spec = pltpu.PrefetchScalarGridSpec(
        num_scalar_prefetch=1, grid=(rows // tr,),
        in_specs=[pl.BlockSpec((tr, cols), lambda r, pr: (r, 0))],
        out_specs=pl.BlockSpec((None, tr, cols), lambda r, pr: (pr[0], r, 0)))
    return pl.pallas_call(
        body, name=name, grid_spec=grid_spec,
        out_shape=jax.ShapeDtypeStruct((N_CHIPS, rows, cols), BF16),
        compiler_params=_params(1))(place, shard)


def _forward_to_sibling(name, arrays, deps=()):
    n = len(arrays)

    def body(*refs):
        ins = refs[:n]
        send_sems, recv_sems = refs[-2:]
        x, y, c, chips = _position()
        sends = []
        for i in range(n):
            rows = ins[i].shape[1] // 2
            for j, (px, py) in enumerate(chips):
                piece = ins[i].at[2 * px + py, pl.ds(c * rows, rows), :]
                cp = pltpu.make_async_remote_copy(
                    src_ref=piece, dst_ref=piece, send_sem=send_sems.at[i, j],
                    recv_sem=recv_sems.at[i, j], device_id=(x, y, 1 - c), device_id_type=MESH)
                cp.start()
                sends.append(cp)
        for i in range(n):
            rows = ins[i].shape[1] // 2
            for j, (px, py) in enumerate(chips):
                piece = ins[i].at[2 * px + py, pl.ds((1 - c) * rows, rows), :]
                pltpu.make_async_remote_copy(
                    src_ref=piece, dst_ref=piece, send_sem=send_sems.at[i, j],
                    recv_sem=recv_sems.at[i, j], device_id=(x, y, 1 - c),
                    device_id_type=MESH).wait_recv()
        for cp in sends:
            cp.wait_send()

    return pl.pallas_call(
        body, name=name,
        in_specs=[_any_spec()] * (n + len(deps)), out_specs=[_any_spec()] * n,
        out_shape=[jax.ShapeDtypeStruct(a.shape, a.dtype) for a in arrays],
        input_output_aliases={i: i for i in range(n)},
        scratch_shapes=[pltpu.SemaphoreType.DMA((n, 3))] * 2,
    )(*arrays, *deps)


def _swap_copies(refs, send_sems, recv_sems):
    x, y, c, _ = _position()
    n = len(refs) // 2
    copies = []
    for i in range(n):
        rows = refs[i].shape[1] // 2
        copies.append(pltpu.make_async_remote_copy(
            src_ref=refs[i].at[:, pl.ds((1 - c) * rows, rows), :], dst_ref=refs[n + i],
            send_sem=send_sems.at[i], recv_sem=recv_sems.at[i],
            device_id=(x, y, 1 - c), device_id_type=MESH))
    return copies


def _small_copies(refs, send_sems, recv_sems):
    packed, slots = refs
    x, y, c, _ = _position()
    me = 4 * x + 2 * y + c
    copies = []
    for r in range(1, N_DEV):
        peer = (x ^ ((r >> 2) & 1), y ^ ((r >> 1) & 1), c ^ (r & 1))
        copies.append(pltpu.make_async_remote_copy(
            src_ref=packed, dst_ref=slots.at[me], send_sem=send_sems.at[r - 1],
            recv_sem=recv_sems.at[r - 1], device_id=peer, device_id_type=MESH))
    return copies


def _block_rows(rows, cols, itemsize=4, target=1 << 20):
    return _pick(max(BF16_ROWS, target // (cols * itemsize)), [rows], unit=BF16_ROWS)


def _pair_sum(name, place, grad, received):
    P, rows, cols = received.shape
    tr = _block_rows(rows, cols, itemsize=2, target=2 << 20)
    nb = rows // tr

    def body(place_ref, g_ref, r_ref, o_ref):
        o_ref[...] = (g_ref[...].astype(F32) + r_ref[...].astype(F32)).astype(BF16)

    def panel(j, pr):
        return pr[0] ^ jnp.where(j == 2, 3, 2 - j)

    grid_spec = pltpu.PrefetchScalarGridSpec(
        num_scalar_prefetch=1, grid=(P - 1, nb),
        in_specs=[pl.BlockSpec((None, tr, cols),
                               lambda j, r, pr: (panel(j, pr), pr[1] * nb + r, 0)),
                  pl.BlockSpec((None, tr, cols), lambda j, r, pr: (panel(j, pr), r, 0))],
        out_specs=pl.BlockSpec((None, tr, cols), lambda j, r, pr: (panel(j, pr), r, 0)))
    return pl.pallas_call(
        body, name=name, grid_spec=grid_spec,
        out_shape=jax.ShapeDtypeStruct(received.shape, BF16),
        compiler_params=_params(2))(place, grad, received)


def _final_sum(name, place, grad, received, from_chips):
    _, rows, cols = received.shape
    tr = _block_rows(rows, cols, target=2 << 20)
    nb = rows // tr

    def body(place_ref, g_ref, r_ref, c_ref, o_ref):
        acc = g_ref[...].astype(F32) + r_ref[...].astype(F32)
        for j in range(3):
            acc = acc + c_ref[j].astype(F32)
        o_ref[...] = acc

    grid_spec = pltpu.PrefetchScalarGridSpec(
        num_scalar_prefetch=1, grid=(nb,),
        in_specs=[pl.BlockSpec((None, tr, cols), lambda r, pr: (pr[0], pr[1] * nb + r, 0)),
                  pl.BlockSpec((None, tr, cols), lambda r, pr: (pr[0], r, 0)),
                  pl.BlockSpec((3, tr, cols), lambda r, pr: (0, r, 0))],
        out_specs=pl.BlockSpec((tr, cols), lambda r, pr: (pr[1] * nb + r, 0)))
    return pl.pallas_call(
        body, name=name, grid_spec=grid_spec,
        out_shape=jax.ShapeDtypeStruct((2 * rows, cols), F32),
        compiler_params=_params(1))(place, grad, received, from_chips)


def _sum_devices(name, me, gathered, own):
    n_dev, rows, cols = gathered.shape
    tr = _pick(256, [rows])

    def body(me_ref, g_ref, own_ref, o_ref):
        term = lambda d: jnp.where(me_ref[0] == d, own_ref[...], g_ref[d])
        acc = term(0)
        for d in range(1, n_dev):
            acc = acc + term(d)
        o_ref[...] = acc

    grid_spec = pltpu.PrefetchScalarGridSpec(
        num_scalar_prefetch=1, grid=(rows // tr,),
        in_specs=[pl.BlockSpec((n_dev, tr, cols), lambda r, me_ref: (0, r, 0)),
                  pl.BlockSpec((tr, cols), lambda r, me_ref: (r, 0))],
        out_specs=pl.BlockSpec((tr, cols), lambda r, me_ref: (r, 0)))
    return pl.pallas_call(
        body, name=name, grid_spec=grid_spec,
        out_shape=jax.ShapeDtypeStruct((rows, cols), F32),
        compiler_params=_params(1))(me, gathered, own)


def _adamw(name, w, g, m, v):
    rows, cols = w.shape
    tr = _block_rows(rows, cols)
    c1 = 1.0 / (1.0 - ADAM_B1 ** ADAM_STEP)
    c2 = 1.0 / (1.0 - ADAM_B2 ** ADAM_STEP)

    def body(w_ref, g_ref, m_ref, v_ref, go_ref, d_ref, nm_ref, nv_ref):
        gv = g_ref[...]
        go_ref[...] = gv
        nm = ADAM_B1 * m_ref[...] + (1.0 - ADAM_B1) * gv
        nv = ADAM_B2 * v_ref[...] + (1.0 - ADAM_B2) * (gv * gv)
        nm_ref[...] = nm
        nv_ref[...] = nv
        d_ref[...] = -ADAM_LR * ((nm * c1) / (jnp.sqrt(nv * c2) + ADAM_EPS) + ADAM_WD * w_ref[...])

    blk = pl.BlockSpec((tr, cols), lambda r: (r, 0))
    shape = jax.ShapeDtypeStruct((rows, cols), F32)
    return pl.pallas_call(
        body, name=name, grid=(rows // tr,), in_specs=[blk] * 4, out_specs=[blk] * 4,
        out_shape=[shape] * 4, compiler_params=_params(1))(w, g, m, v)


BIG = ("ffn1_w_in", "ffn1_w_out", "w_mix_in", "w_mix_out", "w_cq", "w_ckv", "w_co",
       "ffn2_w_in", "ffn2_w_out")
BIG_KIND = {"ffn1_w_in": "c", "ffn1_w_out": "r", "w_mix_in": "c", "w_mix_out": "r", "w_cq": "r",
            "w_ckv": "c", "w_co": "r", "ffn2_w_in": "c", "ffn2_w_out": "r"}
GATHER_GROUPS = (("ffn1_in", ("ffn1_w_in",)), ("ffn1_out", ("ffn1_w_out",)),
                 ("mix_in", ("w_mix_in",)), ("mix_out", ("w_mix_out",)),
                 ("cross", ("w_cq", "w_ckv", "w_co")),
                 ("ffn2_in", ("ffn2_w_in",)), ("ffn2_out", ("ffn2_w_out",)))
GATHER_AFTER = (("ffn1_in", None), ("ffn1_out", "ffn1_in"), ("mix_in", "ffn1_out"),
                ("mix_out", "mix_in"), ("cross", "mix_in"), ("ffn2_in", "mix_in"),
                ("ffn2_out", "cross"))
RELAYED = ("ffn1_in",)
TAIL_STAGES = (("sum", "ffn2"), ("sum", "cross"), ("sum", "mix"), ("sum", "ffn1_out"),
               ("update", "ffn2"), ("update", "cross"), ("sum", "ffn1_in"), ("update", "mix"),
               ("update", "ffn1_out"), ("update", "ffn1_in"))
SMALL = ("ffn1_norm", "mix_norm", "ln_v_gain", "ln_v_bias", "spatial_w", "spatial_b", "gnorm_a",
         "gnorm_b", "cross_norm", "mem_norm", "ffn2_norm", "final_norm")
WEIGHTS = ("ffn1_norm", "ffn1_w_in", "ffn1_w_out", "mix_norm", "w_mix_in", "ln_v_gain",
           "ln_v_bias", "spatial_w", "spatial_b", "gnorm_a", "gnorm_b", "w_mix_out", "cross_norm",
           "mem_norm", "w_cq", "w_ckv", "w_co", "ffn2_norm", "ffn2_w_in", "ffn2_w_out",
           "final_norm")


def _pack(arrays):
    return jnp.concatenate([a.reshape(-1, LANE) for a in arrays], axis=0)


def _unpack(packed, like):
    out, row = [], 0
    for a in like:
        rows = a.size // LANE
        out.append(packed[row:row + rows].reshape(a.shape))
        row += rows
    return out


def _local_step(x, mem, target, small, weights_of, start_tokens, grads_ready, grads_flush):
    T, D = x.shape
    vec = lambda name: small[name].reshape(1, -1)
    w_a = small["ln_v_gain"].size
    w_b = small["gnorm_b"].size
    G = w_a // GROUP_DIM
    w_s = small["spatial_w"].reshape(G, SGU_BLOCK, SGU_BLOCK)
    b_t = small["spatial_b"].reshape(G, SGU_BLOCK).T

    h1, ffn1_saved = _ffn_forward("ffn1", x, vec("ffn1_norm"), weights_of, deps=start_tokens)
    n2 = _rmsnorm_fwd("mix_norm", h1, vec("mix_norm"))
    big = weights_of("mix_in", n2)
    (z,) = _matmul("mix_in", Mat(n2), big["w_mix_in"], "nn", [("c", 1, F32)], tm=2048, tn=256)
    z = z[0]
    y = _sgu_forward("sgu", z, vec("ln_v_gain"), vec("ln_v_bias"), w_s, b_t, vec("gnorm_a"), D)
    yb, sb_total = _sb_forward("stickbreak", z, w_a, w_b)
    y = _rmsnorm_fwd("gnorm_b", yb, vec("gnorm_b"), into=y, col=w_a // w_b)

    def add_res(acc, ex, out):
        out[0][...] = ex[0][...] + acc

    big.update(weights_of("mix_out", y))
    (h2,) = _matmul("mix_out", Mat(y), big["w_mix_out"], "nn", [("c", 1, F32)],
                    tm=1024, tn=1024, extras=[Mat(h1)], epi=add_res)
    h2 = h2[0]
    n3 = _rmsnorm_fwd("cross_norm", h2, vec("cross_norm"))
    memn = _rmsnorm_fwd("mem_norm", mem, vec("mem_norm"))
    big.update(weights_of("cross", n3))
    x_scale = (D // X_HEADS) ** -0.5

    def scaled(acc, ex, out):
        out[0][...] = (acc * x_scale).astype(BF16)

    (q,) = _matmul("cross_q", Mat(n3), big["w_cq"], "nn", [("c", 1, BF16)],
                   tm=1024, tn=1024, epi=scaled)
    (kv,) = _matmul("cross_kv", Mat(memn), big["w_ckv"], "nn", [("c", 1, BF16)], tm=256, tn=1024)
    q, kv = q[0], kv[0]
    o = _xattn_forward("cross_attn", q, kv)
    (h3,) = _matmul("cross_out", Mat(o), big["w_co"], "nn", [("c", 1, F32)],
                    tm=1024, tn=1024, extras=[Mat(h2)], epi=add_res)
    h3 = h3[0]
    h4, ffn2_saved = _ffn_forward("ffn2", h3, vec("ffn2_norm"), weights_of)

    gs = {}
    loss_tile, dh4, dh4_bf, gs["final_norm"] = _loss_head("loss_head", h4, vec("final_norm"), target)
    dh3, dh3_bf, gs["ffn2_norm"] = _ffn_backward(
        "ffn2", h3, vec("ffn2_norm"), ffn2_saved, dh4, dh4_bf, grads_ready, grads_flush)

    (do,) = _matmul("cross_do", Mat(dh3_bf), big["w_co"], "nt", [("c", 1, BF16)], tm=1024, tn=512)
    (dw_co,) = _matmul("cross_dwo", Mat(o), Mat(dh3_bf), "tn", [("r", N_CHIPS, BF16)],
                       tm=512, tn=1024)
    dq, dkv = _xattn_backward("cross_attn_bwd", q, kv, do[0])
    (dw_cq,) = _matmul("cross_dwq", Mat(n3), Mat(dq), "tn", [("r", N_CHIPS, BF16)],
                       tm=512, tn=1024)
    (dw_ckv,) = _matmul("cross_dwkv", Mat(memn), Mat(dkv), "tn", [("c", N_CHIPS, BF16)],
                        tm=1024, tn=1024)
    token = grads_ready("cross", {"w_cq": dw_cq, "w_ckv": dw_ckv, "w_co": dw_co})
    dq = _tie("cross_dq_after_swap", dq, [token])
    (dn3,) = _matmul("cross_dn", Mat(dq), big["w_cq"], "nt", [("c", 1, F32)], tm=1024, tn=512)
    (dmemn,) = _matmul("cross_dmem", Mat(dkv), big["w_ckv"], "nt", [("c", 1, F32)],
                       tm=256, tn=1024, tk=1024)
    (gs["mem_norm"],) = _rmsnorm_bwd("mem_dnorm", mem, vec("mem_norm"), dmemn[0], want_dx=False)
    dn3 = _tie("cross_dn_after_scatter", dn3, [grads_flush("cross", gs["mem_norm"])])
    dh2, dh2_bf, gs["cross_norm"] = _rmsnorm_bwd("cross_dnorm", h2, vec("cross_norm"), dn3[0],
                                                 dres=dh3)

    (dy,) = _matmul("mix_dy", Mat(dh2_bf), big["w_mix_out"], "nt", [("c", 1, F32)], tm=1024, tn=512)
    dy = dy[0]
    (dw_mix_out,) = _matmul("mix_dwout", Mat(y), Mat(dh2_bf), "tn", [("r", N_CHIPS, BF16)],
                            tm=512, tn=1024)
    dza, gs["ln_v_gain"], gs["ln_v_bias"], gs["spatial_w"], db, gs["gnorm_a"] = _sgu_backward(
        "sgu_bwd", z, dy, vec("ln_v_gain"), vec("ln_v_bias"), w_s, b_t, vec("gnorm_a"))
    gs["spatial_b"] = db.reshape(G, SGU_BLOCK)
    dob, gs["gnorm_b"] = _rmsnorm_bwd("gnorm_b_bwd", yb, vec("gnorm_b"), dy, dn_col=w_a // w_b,
                                      want_bf16=False)
    dqb, dkvb = _sb_backward("stickbreak_bwd", z, dob, sb_total, w_a, w_b)
    dz = jnp.concatenate([dza, dqb, dkvb[0].astype(BF16), dkvb[1].astype(BF16)], axis=1)
    (dw_mix_in,) = _matmul("mix_dwin", Mat(n2), Mat(dz), "tn", [("c", N_CHIPS, BF16)],
                           tm=1024, tn=1280)
    token = grads_ready("mix", {"w_mix_in": dw_mix_in, "w_mix_out": dw_mix_out})
    dz = _tie("mix_dz_after_swap", dz, [token])
    (dn2,) = _matmul("mix_dn", Mat(dz), big["w_mix_in"], "nt", [("c", 1, F32)],
                     tm=1024, tn=1024, tk=1280)
    dn2 = _tie("mix_dn_after_scatter", dn2, [grads_flush("mix", dn2)])
    dh1, dh1_bf, gs["mix_norm"] = _rmsnorm_bwd("mix_dnorm", h1, vec("mix_norm"), dn2[0], dres=dh2)

    dx, _, gs["ffn1_norm"] = _ffn_backward(
        "ffn1", x, vec("ffn1_norm"), ffn1_saved, dh1, dh1_bf, grads_ready, grads_flush,
        early_out=True)
    gs = {k: g.reshape(small[k].shape) for k, g in gs.items()}
    return loss_tile, dx, gs


def kernel(x, mem, ffn1_norm, ffn1_w_in, ffn1_w_out, mix_norm, w_mix_in, ln_v_gain, ln_v_bias, spatial_w, spatial_b, gnorm_a, gnorm_b, w_mix_out, cross_norm, mem_norm, w_cq, w_ckv, w_co, ffn2_norm, ffn2_w_in, ffn2_w_out, final_norm, loss_target, m_ffn1_norm, m_ffn1_w_in, m_ffn1_w_out, m_mix_norm, m_w_mix_in, m_ln_v_gain, m_ln_v_bias, m_spatial_w, m_spatial_b, m_gnorm_a, m_gnorm_b, m_w_mix_out, m_cross_norm, m_mem_norm, m_w_cq, m_w_ckv, m_w_co, m_ffn2_norm, m_ffn2_w_in, m_ffn2_w_out, m_final_norm, v_ffn1_norm, v_ffn1_w_in, v_ffn1_w_out, v_mix_norm, v_w_mix_in, v_ln_v_gain, v_ln_v_bias, v_spatial_w, v_spatial_b, v_gnorm_a, v_gnorm_b, v_w_mix_out, v_cross_norm, v_mem_norm, v_w_cq, v_w_ckv, v_w_co, v_ffn2_norm, v_ffn2_w_in, v_ffn2_w_out, v_final_norm):
    given = dict(locals())
    w = {k: given[k] for k in WEIGHTS}
    m = {k: given["m_" + k] for k in WEIGHTS}
    v = {k: given["v_" + k] for k in WEIGHTS}

    cx, cy, cc = lax.axis_index("x"), lax.axis_index("y"), lax.axis_index("c")
    place = jnp.stack([2 * cx + cy, cc]).astype(jnp.int32)

    names_of = dict(GATHER_GROUPS)
    own = {g: [_cast_own(f"cast_{k}", place, w[k][0]) for k in names] for g, names in GATHER_GROUPS}
    gathers = {}

    def start_gather(group, deps):
        first_hop = _near_copies if group in RELAYED else _gather_copies
        n_sems = (2 if group in RELAYED else 3) * len(own[group])
        send, recv, arrays, token = _split_start(f"gather_start_{group}", own[group],
                                                 first_hop, n_sems, deps)
        gathers[group] = (send, recv, arrays)
        return token

    start_tokens = [start_gather(g, ()) for g, after in GATHER_AFTER if after is None]
    start_tokens += [a for g, after in GATHER_AFTER if after is not None for a in own[g]]

    def weights_of(group, after):
        send, recv, arrays = gathers[group]
        if group in RELAYED:
            arrays = _split_wait(f"gather_wait_{group}", arrays, send, recv, after, _near_copies)
            send, recv, arrays, _ = _split_start(f"gather_relay_{group}", list(arrays),
                                                 _relay_copies, 2 * len(arrays))
            arrays = _split_wait(f"gather_relay_wait_{group}", arrays, send, recv, after,
                                 _relay_copies)
        else:
            arrays = _split_wait(f"gather_wait_{group}", arrays, send, recv, after, _gather_copies)
        tokens = [start_gather(g, (arrays[0],)) for g, a in GATHER_AFTER if a == group]
        arrays = _forward_to_sibling(f"gather_forward_{group}", list(arrays), tokens)
        return {k: Mat(a, BIG_KIND[k]) for k, a in zip(names_of[group], arrays)}

    swaps, scatters = {}, {}

    def grads_ready(group, partial):
        names = list(partial)
        grads_ = [partial[k] for k in names]
        lands = [lax.empty((g.shape[0], g.shape[1] // 2, g.shape[2]), g.dtype) for g in grads_]
        send, recv, arrays, token = _split_start(f"swap_start_{group}", grads_ + lands,
                                                 _swap_copies, len(names))
        swaps[group] = (names, send, recv, arrays)
        return token

    def grads_flush(group, after):
        names, send, recv, arrays = swaps[group]
        arrays = _split_wait(f"swap_wait_{group}", arrays, send, recv, after, _swap_copies)
        grads_, from_sibling = arrays[:len(names)], arrays[len(names):]
        sums = [_pair_sum(f"pair_sum_{k}", place, g, r)
                for k, g, r in zip(names, grads_, from_sibling)]
        lands = [lax.empty((3,) + s.shape[1:], s.dtype) for s in sums]
        send, recv, arrays, token = _split_start(f"scatter_start_{group}", sums + lands,
                                                 _scatter_copies, 3 * len(names))
        scatters[group] = (names, grads_, from_sibling, send, recv, arrays)
        return token

    small = {k: w[k] for k in SMALL}
    loss_tile, grad_x, gs = _local_step(x[0], mem[0], loss_target[0], small, weights_of,
                                        start_tokens, grads_ready, grads_flush)

    packed = _pack([gs[k] for k in SMALL] + [loss_tile])
    slots = jnp.zeros((N_DEV,) + packed.shape, packed.dtype)
    small_send, small_recv, small_arrays, _ = _split_start(
        "small_start", [packed, slots], _small_copies, N_DEV - 1)

    grad, delta, new_m, new_v = {}, {}, {}, {}
    shares = {}
    after = [grad_x]
    for stage, group in TAIL_STAGES:
        if stage == "sum":
            names, grads_, from_sibling, send, recv, arrays = scatters[group]
            arrays = _split_wait(f"scatter_wait_{group}", arrays, send, recv, after,
                                 _scatter_copies)
            from_chips = arrays[len(names):]
            shards = [_final_sum(f"final_sum_{k}", place, g, r, f)
                      for k, g, r, f in zip(names, grads_, from_sibling, from_chips)]
            send, recv, shards, token = _split_start(f"share_start_{group}", shards,
                                                     _share_copies, len(names))
            shares[group] = (names, send, recv, shards)
            after = [token]
        else:
            names, send, recv, shards = shares[group]
            shards = _split_wait(f"share_wait_{group}", shards, send, recv, after, _share_copies)
            after = []
            for k, g_ in zip(names, shards):
                g_, d_, m_, v_ = _adamw(f"adamw_{k}", w[k][0], g_, m[k][0], v[k][0])
                grad[k], delta[k], new_m[k], new_v[k] = g_[None], d_[None], m_[None], v_[None]
                after.append(v_)

    packed, slots = _split_wait("small_wait", small_arrays, small_send, small_recv, after,
                                _small_copies)
    me = (4 * cx + 2 * cy + cc).astype(jnp.int32).reshape(1)
    total = _sum_devices("sum_small", me, slots, packed)
    n_small = total.shape[0] - SUBLANE
    loss = total[n_small, 0]
    small_g = total[:n_small]
    g_s, d_s, m_s, v_s = _adamw("adamw_small", _pack([w[k] for k in SMALL]), small_g,
                                _pack([m[k] for k in SMALL]), _pack([v[k] for k in SMALL]))
    like = [w[k] for k in SMALL]
    for k, g_, d_, m_, v_ in zip(SMALL, _unpack(g_s, like), _unpack(d_s, like),
                                 _unpack(m_s, like), _unpack(v_s, like)):
        grad[k], delta[k], new_m[k], new_v[k] = g_, d_, m_, v_

    return (loss, grad_x[None], *[grad[k] for k in WEIGHTS], *[delta[k] for k in WEIGHTS],
            *[new_m[k] for k in WEIGHTS], *[new_v[k] for k in WEIGHTS])
```

```python
import functools
import math

import jax
import jax.numpy as jnp
from jax import lax
from jax.experimental import pallas as pl
from jax.experimental.pallas import tpu as pltpu

F32 = jnp.float32
BF16 = jnp.bfloat16
MESH = pl.DeviceIdType.MESH

EPS = 1e-6
CHUNK = 64
SGU_BLOCK = 128
GROUP_DIM = 128
X_HEADS = 4
N_CHIPS = 4
N_DEV = 8
LANE = 128
SUBLANE = 8
BF16_ROWS = 16

ADAM_LR = 0.001
ADAM_B1 = 0.9
ADAM_B2 = 0.999
ADAM_EPS = 1e-08
ADAM_WD = 0.01
ADAM_STEP = 10

V7X_VMEM_BYTES = 64 << 20
VMEM_LIMIT = V7X_VMEM_BYTES - (8 << 20)


def _params(n_grid):
    return pltpu.CompilerParams(dimension_semantics=("arbitrary",) * n_grid,
                                vmem_limit_bytes=VMEM_LIMIT)


def _pick(pref, dims, unit=None):
    g = functools.reduce(math.gcd, dims)
    if unit is None:
        unit = LANE if g % LANE == 0 else SUBLANE
    cands = [d for d in range(unit, g + 1, unit) if g % d == 0] or [g]
    return min(cands, key=lambda d: abs(math.log(d / pref)))


def _any_spec():
    return pl.BlockSpec(memory_space=pl.ANY)


class Mat:
    def __init__(self, arr, kind="c"):
        if arr.ndim == 2:
            arr = arr[None]
        self.arr, self.kind = arr, kind
        self.P, self.prow, self.pcol = arr.shape
        self.rows = self.prow * (self.P if kind == "r" else 1)
        self.cols = self.pcol * (self.P if kind == "c" else 1)
        self.dtype = arr.dtype

    def spec(self, tr, tc, rc_fn):
        if self.kind == "c":
            per = self.pcol // tc
            assert per * tc == self.pcol, (self.pcol, tc)

            def imap(*g):
                i, j = rc_fn(*g)
                return (j // per, i, j % per)
        else:
            per = self.prow // tr
            assert per * tr == self.prow, (self.prow, tr)

            def imap(*g):
                i, j = rc_fn(*g)
                return (i // per, i % per, j)
        return pl.BlockSpec((None, tr, tc), imap)

    def two_d(self):
        assert self.P == 1
        return self.arr[0]


def _out_mat(kind, P, rows, cols, dtype):
    shape = (P, rows, cols // P) if kind == "c" else (P, rows // P, cols)
    return jax.ShapeDtypeStruct(shape, dtype)


def _matmul(name, A, B, mode, outs, *, tm=1024, tn=1024, tk=2048, extras=(), epi=None):
    if mode == "nn":
        M, K, N = A.rows, A.cols, B.cols
        assert B.rows == K
    elif mode == "nt":
        M, K, N = A.rows, A.cols, B.rows
        assert B.cols == K
    else:
        K, M, N = A.rows, A.cols, B.cols
        assert B.rows == K
    mdims, ndims, kdims = [M], [N], [K]
    whole_b = mode == "nn" and B.kind == "r" and B.P > 1 and K <= tk
    if whole_b:
        kdims.append(A.pcol)
        ndims.append(B.pcol)
    elif mode == "tn":
        assert A.kind == "c" and B.kind == "c"
        mdims.append(A.pcol)
        ndims.append(B.pcol)
    else:
        (mdims if A.kind == "r" else kdims).append(A.prow if A.kind == "r" else A.pcol)
        if mode == "nn":
            (kdims if B.kind == "r" else ndims).append(B.prow if B.kind == "r" else B.pcol)
        else:
            (ndims if B.kind == "r" else kdims).append(B.prow if B.kind == "r" else B.pcol)
    for o in list(outs) + list(extras):
        if isinstance(o, Mat):
            (mdims if o.kind == "r" else ndims).append(o.prow if o.kind == "r" else o.pcol)
        elif isinstance(o[0], str):
            (mdims if o[0] == "r" else ndims).append((M if o[0] == "r" else N) // o[1])
    tm, tn = _pick(tm, mdims), _pick(tn, ndims)
    tk = K if mode == "tn" else _pick(tk, kdims)
    nk = K // tk
    grid = (M // tm, N // tn, nk)

    if mode == "tn":
        a_spec = A.spec(K, tm, lambda m, n, k: (0, m))
        b_spec = B.spec(K, tn, lambda m, n, k: (0, n))
    else:
        a_spec = A.spec(tm, tk, lambda m, n, k: (m, k))
        if whole_b:
            b_spec = pl.BlockSpec((B.P, B.prow, tn), lambda m, n, k: (0, 0, n))
        elif mode == "nn":
            b_spec = B.spec(tk, tn, lambda m, n, k: (k, n))
        else:
            b_spec = B.spec(tn, tk, lambda m, n, k: (n, k))

    def mn_spec(o):
        if isinstance(o, Mat):
            return o.spec(tm, tn, lambda m, n, k: (m, n))
        if isinstance(o[0], str):
            kind, P = o[0], o[1]
            fake = Mat.__new__(Mat)
            fake.kind, fake.P = kind, P
            fake.prow = M // P if kind == "r" else M
            fake.pcol = N // P if kind == "c" else N
            return Mat.spec(fake, tm, tn, lambda m, n, k: (m, n))
        return o[1](tm, tn)

    out_shapes = tuple(_out_mat(o[0], o[1], M, N, o[2]) if isinstance(o[0], str) else o[0]
                       for o in outs)
    out_specs = tuple(mn_spec(o) for o in outs)
    extra_arrays = tuple(e.arr if isinstance(e, Mat) else e[0] for e in extras)
    extra_specs = tuple(mn_spec(e) for e in extras)
    n_ex, n_out = len(extras), len(outs)
    tt = _pick(256, [tm])
    dims = (((1,), (1 if mode == "nt" else 0,)), ((), ()))

    def body(*refs):
        a_ref, b_ref = refs[:2]
        ex_refs = refs[2:2 + n_ex]
        out_refs = refs[2 + n_ex:2 + n_ex + n_out]
        scratch = refs[2 + n_ex + n_out:]
        if mode == "tn":
            at_ref = scratch[0]

            @pl.when(pl.program_id(1) == 0)
            def _():
                for c0 in range(0, tm, tt):
                    at_ref[c0:c0 + tt, :] = a_ref[:, c0:c0 + tt].astype(F32).T.astype(BF16)

            lhs = at_ref[...]
        else:
            lhs = a_ref[...].astype(BF16)
        rhs = b_ref[...].reshape(K, tn) if whole_b else b_ref[...]
        part = lax.dot_general(lhs, rhs.astype(BF16), dims, preferred_element_type=F32)

        def finish(acc):
            if epi is None:
                out_refs[0][...] = acc.astype(out_refs[0].dtype)
            else:
                epi(acc, ex_refs, out_refs)

        if nk == 1:
            finish(part)
        else:
            acc_ref = scratch[0]
            k = pl.program_id(2)

            @pl.when(k == 0)
            def _():
                acc_ref[...] = part

            @pl.when(k > 0)
            def _():
                acc_ref[...] += part

            @pl.when(k == nk - 1)
            def _():
                finish(acc_ref[...])

    scratch_shapes = []
    if mode == "tn":
        scratch_shapes.append(pltpu.VMEM((tm, K), BF16))
    elif nk > 1:
        scratch_shapes.append(pltpu.VMEM((tm, tn), F32))
    res = pl.pallas_call(
        body, name=name, grid=grid,
        in_specs=[a_spec, b_spec, *extra_specs], out_specs=out_specs, out_shape=out_shapes,
        scratch_shapes=scratch_shapes, compiler_params=_params(3),
    )(A.arr, B.arr, *extra_arrays)
    return res


def _row_tile(T):
    return _pick(256, [T])


def _tie(name, x, deps):
    def body(*refs):
        refs[-1][...] = jnp.zeros_like(refs[-1])

    return pl.pallas_call(
        body, name=name, in_specs=[_any_spec()] * (1 + len(deps)),
        out_specs=(_any_spec(), pl.BlockSpec(memory_space=pltpu.VMEM)),
        out_shape=(jax.ShapeDtypeStruct(x.shape, x.dtype),
                   jax.ShapeDtypeStruct((SUBLANE, LANE), F32)),
        input_output_aliases={0: 0},
    )(x, *deps)[0]


def _rmsnorm_fwd(name, x, g, *, into=None, col=0, deps=()):
    T, W = x.shape
    tr = _row_tile(T)

    def body(x_ref, g_ref, *rest):
        o_ref = rest[-1]
        xv = x_ref[...]
        rstd = lax.rsqrt(jnp.mean(xv * xv, axis=-1, keepdims=True) + EPS)
        o_ref[...] = (xv * rstd * g_ref[...]).astype(o_ref.dtype)

    in_specs = [pl.BlockSpec((tr, W), lambda i: (i, 0)), pl.BlockSpec((1, W), lambda i: (0, 0))]
    args = [x, g]
    kwargs = {}
    if into is None:
        out_shape = jax.ShapeDtypeStruct((T, W), BF16)
    else:
        out_shape = jax.ShapeDtypeStruct(into.shape, into.dtype)
        in_specs.append(_any_spec())
        args.append(into)
        kwargs["input_output_aliases"] = {2: 0}
    in_specs += [_any_spec()] * len(deps)
    args += list(deps)
    return pl.pallas_call(
        body, name=name, grid=(T // tr,), in_specs=in_specs,
        out_specs=pl.BlockSpec((tr, W), lambda i: (i, col)), out_shape=out_shape,
        compiler_params=_params(1), **kwargs)(*args)


def _rmsnorm_bwd(name, x, g, dn, *, dn_col=0, dres=None, want_dx=True, want_bf16=True):
    T, W = x.shape
    tr = _row_tile(T)
    has_res = dres is not None

    def body(*refs):
        x_ref, g_ref, dn_ref = refs[:3]
        pos = 3
        dres_ref = None
        if has_res:
            dres_ref = refs[pos]
            pos += 1
        outs = refs[pos:]
        dg_ref = outs[-1]
        xv = x_ref[...]
        rstd = lax.rsqrt(jnp.mean(xv * xv, axis=-1, keepdims=True) + EPS)
        xhat = xv * rstd
        dnv = dn_ref[...].astype(F32)

        @pl.when(pl.program_id(0) == 0)
        def _():
            dg_ref[...] = jnp.zeros_like(dg_ref)

        dg_ref[...] += jnp.sum(dnv * xhat, axis=0, keepdims=True)
        if want_dx:
            t = dnv * g_ref[...]
            dx = rstd * (t - xhat * jnp.mean(t * xhat, axis=-1, keepdims=True))
            if has_res:
                dx = dx + dres_ref[...]
            outs[0][...] = dx
            if want_bf16:
                outs[1][...] = dx.astype(BF16)

    row = pl.BlockSpec((tr, W), lambda i: (i, 0))
    in_specs = [row, pl.BlockSpec((1, W), lambda i: (0, 0)),
                pl.BlockSpec((tr, W), lambda i: (i, dn_col))]
    args = [x, g, dn]
    if has_res:
        in_specs.append(row)
        args.append(dres)
    out_shape, out_specs = [], []
    if want_dx:
        out_shape.append(jax.ShapeDtypeStruct((T, W), F32))
        out_specs.append(row)
        if want_bf16:
            out_shape.append(jax.ShapeDtypeStruct((T, W), BF16))
            out_specs.append(row)
    out_shape.append(jax.ShapeDtypeStruct((1, W), F32))
    out_specs.append(pl.BlockSpec((1, W), lambda i: (0, 0)))
    return pl.pallas_call(
        body, name=name, grid=(T // tr,), in_specs=in_specs, out_specs=out_specs,
        out_shape=out_shape, compiler_params=_params(1))(*args)


def _loss_head(name, h, g, target):
    T, W = h.shape
    tr = _row_tile(T)

    def body(h_ref, g_ref, t_ref, loss_ref, dx_ref, dxb_ref, dg_ref):
        xv = h_ref[...]
        gv = g_ref[...]
        rstd = lax.rsqrt(jnp.mean(xv * xv, axis=-1, keepdims=True) + EPS)
        xhat = xv * rstd
        diff = xhat * gv - t_ref[...]

        @pl.when(pl.program_id(0) == 0)
        def _():
            dg_ref[...] = jnp.zeros_like(dg_ref)
            loss_ref[...] = jnp.zeros_like(loss_ref)

        loss_ref[...] += 0.5 * jnp.sum(jnp.mean(diff * diff, axis=-1, keepdims=True))
        dnv = diff * (1.0 / W)
        dg_ref[...] += jnp.sum(dnv * xhat, axis=0, keepdims=True)
        t = dnv * gv
        dx = rstd * (t - xhat * jnp.mean(t * xhat, axis=-1, keepdims=True))
        dx_ref[...] = dx
        dxb_ref[...] = dx.astype(BF16)

    row = pl.BlockSpec((tr, W), lambda i: (i, 0))
    vec = pl.BlockSpec((1, W), lambda i: (0, 0))
    return pl.pallas_call(
        body, name=name, grid=(T // tr,), in_specs=[row, vec, row],
        out_specs=[pl.BlockSpec((SUBLANE, LANE), lambda i: (0, 0)), row, row, vec],
        out_shape=[jax.ShapeDtypeStruct((SUBLANE, LANE), F32), jax.ShapeDtypeStruct((T, W), F32),
                   jax.ShapeDtypeStruct((T, W), BF16), jax.ShapeDtypeStruct((1, W), F32)],
        compiler_params=_params(1))(h, g, target)


def _sigmoid(x):
    return 1.0 / (1.0 + jnp.exp(-x))


def _ffn_in(name, n, W):
    T, D = n.shape
    F = W.cols // 2
    tm = _pick(2048, [T])
    tn = _pick(512, [W.pcol])
    per = W.pcol // tn

    def body(a_ref, wg_ref, wu_ref, gu_ref, act_ref):
        a = a_ref[...]
        gate = jnp.dot(a, wg_ref[...], preferred_element_type=F32)
        up = jnp.dot(a, wu_ref[...], preferred_element_type=F32)
        gu_ref[0] = gate.astype(BF16)
        gu_ref[1] = up.astype(BF16)
        act_ref[...] = (gate * _sigmoid(gate) * up).astype(BF16)

    return pl.pallas_call(
        body, name=name, grid=(T // tm, F // tn),
        in_specs=[pl.BlockSpec((tm, D), lambda m, j: (m, 0)),
                  pl.BlockSpec((None, D, tn), lambda m, j: (j // per, 0, j % per)),
                  pl.BlockSpec((None, D, tn), lambda m, j: (2 + j // per, 0, j % per))],
        out_specs=[pl.BlockSpec((2, tm, tn), lambda m, j: (0, m, j)),
                   pl.BlockSpec((tm, tn), lambda m, j: (m, j))],
        out_shape=[jax.ShapeDtypeStruct((2, T, F), BF16), jax.ShapeDtypeStruct((T, F), BF16)],
        compiler_params=_params(2))(n, W.arr, W.arr)


def _ffn_forward(tag, h, norm_g, weights_of, deps=()):
    n = _rmsnorm_fwd(f"{tag}_norm", h, norm_g, deps=deps)
    w_in = weights_of(f"{tag}_in", n)[f"{tag}_w_in"]
    gu, act = _ffn_in(f"{tag}_in", n, w_in)
    w_out = weights_of(f"{tag}_out", act)[f"{tag}_w_out"]

    def epi(acc, ex, out):
        out[0][...] = ex[0][...] + 0.5 * acc

    (h_out,) = _matmul(f"{tag}_out", Mat(act), w_out, "nn", [("c", 1, F32)],
                       tm=1024, tn=512, tk=8192, extras=[Mat(h)], epi=epi)
    return h_out[0], (n, gu, act, w_in, w_out)


def _ffn_backward(tag, h_in, norm_g, saved, dh, dh_bf, grads_ready, grads_flush,
                  early_out=False):
    n, gu, act, w_in, w_out = saved
    T, F = act.shape

    def epi(acc, ex, out):
        dact = 0.5 * acc
        gate = ex[0][0].astype(F32)
        up = ex[0][1].astype(F32)
        sig = _sigmoid(gate)
        out[0][0] = (dact * up * sig * (1.0 + gate * (1.0 - sig))).astype(BF16)
        out[0][1] = (dact * gate * sig).astype(BF16)

    def pair_spec(tm, tn):
        return pl.BlockSpec((2, tm, tn), lambda m, j, k: (0, m, j))

    def half(acc, ex, out):
        out[0][...] = (0.5 * acc).astype(out[0].dtype)

    (dw_out,) = _matmul(f"{tag}_dwout", Mat(act), Mat(dh_bf), "tn", [("r", N_CHIPS, BF16)],
                        tm=1408, tn=512, epi=half)
    if early_out:
        token = grads_ready(f"{tag}_out", {f"{tag}_w_out": dw_out})
        dh_bf = _tie(f"{tag}_dh_after_swap", dh_bf, [token])
    (dgu,) = _matmul(f"{tag}_dact", Mat(dh_bf), w_out, "nt",
                     [(jax.ShapeDtypeStruct((2, T, F), BF16), pair_spec)],
                     tm=512, tn=1408, extras=[(gu, pair_spec)], epi=epi)
    if early_out:
        dgu = _tie(f"{tag}_dgu_after_scatter", dgu, [grads_flush(f"{tag}_out", dgu)])
    (dw_in,) = _matmul(f"{tag}_dwin", Mat(n), Mat(dgu), "tn", [("c", N_CHIPS, BF16)],
                       tm=1024, tn=1408)
    if early_out:
        group, partial = f"{tag}_in", {f"{tag}_w_in": dw_in}
    else:
        group, partial = tag, {f"{tag}_w_in": dw_in, f"{tag}_w_out": dw_out}
    dgu = _tie(f"{tag}_dgu_after_swap", dgu, [grads_ready(group, partial)])
    (dn,) = _matmul(f"{tag}_dn", Mat(dgu), w_in, "nt", [("c", 1, F32)],
                    tm=1024, tn=1024, tk=2816)
    dn = _tie(f"{tag}_dn_after_scatter", dn, [grads_flush(group, dn)])
    return _rmsnorm_bwd(f"{tag}_dnorm", h_in, norm_g, dn[0], dres=dh)


_GELU_C = math.sqrt(2.0 / math.pi)
_GELU_A = 0.044715


def _gelu(x):
    return 0.5 * x * (1.0 + jnp.tanh(_GELU_C * (x + _GELU_A * x * x * x)))


def _gelu_grad(x):
    th = jnp.tanh(_GELU_C * (x + _GELU_A * x * x * x))
    return 0.5 * (1.0 + th) + 0.5 * x * (1.0 - th * th) * _GELU_C * (1.0 + 3.0 * _GELU_A * x * x)


def _chunk_mask():
    t = lax.broadcasted_iota(jnp.int32, (SGU_BLOCK, SGU_BLOCK), 0) // CHUNK
    s = lax.broadcasted_iota(jnp.int32, (SGU_BLOCK, SGU_BLOCK), 1) // CHUNK
    return s <= t


def _sgu_group_forward(v_g, lg, lb, wm_bf, b_col):
    mu = jnp.mean(v_g, axis=-1, keepdims=True)
    xc = v_g - mu
    rstd = lax.rsqrt(jnp.mean(xc * xc, axis=-1, keepdims=True) + EPS)
    vhat = xc * rstd
    vn = vhat * lg + lb
    mixed = jnp.dot(wm_bf, vn.astype(BF16), preferred_element_type=F32) + b_col
    return vhat, rstd, vn, mixed


def _sgu_forward(name, z, ln_g, ln_b, w_s, b_t, gn, d_model):
    T = z.shape[0]
    W_A = ln_g.shape[1]
    G = W_A // GROUP_DIM

    def body(z_ref, lg_ref, lb_ref, w_ref, bt_ref, gn_ref, y_ref):
        mask = _chunk_mask()
        u = _gelu(z_ref[:, :W_A])
        v = _gelu(z_ref[:, W_A:])
        cols = []
        for g in range(G):
            sl = slice(g * GROUP_DIM, (g + 1) * GROUP_DIM)
            wm = jnp.where(mask, w_ref[g], 0.0).astype(BF16)
            _, _, _, mixed = _sgu_group_forward(v[:, sl], lg_ref[:, sl], lb_ref[:, sl], wm,
                                                bt_ref[:, g:g + 1])
            cols.append(u[:, sl] * mixed)
        ya = jnp.concatenate(cols, axis=1)
        rstd = lax.rsqrt(jnp.mean(ya * ya, axis=-1, keepdims=True) + EPS)
        y_ref[...] = (ya * rstd * gn_ref[...]).astype(BF16)

    vec = pl.BlockSpec((1, W_A), lambda i: (0, 0))
    return pl.pallas_call(
        body, name=name, grid=(T // SGU_BLOCK,),
        in_specs=[pl.BlockSpec((SGU_BLOCK, 2 * W_A), lambda i: (i, 0)), vec, vec,
                  pl.BlockSpec((G, SGU_BLOCK, SGU_BLOCK), lambda i: (0, 0, 0)),
                  pl.BlockSpec((SGU_BLOCK, G), lambda i: (0, 0)), vec],
        out_specs=pl.BlockSpec((SGU_BLOCK, W_A), lambda i: (i, 0)),
        out_shape=jax.ShapeDtypeStruct((T, d_model), BF16),
        compiler_params=_params(1))(z, ln_g, ln_b, w_s, b_t, gn)


def _sgu_backward(name, z, dy, ln_g, ln_b, w_s, b_t, gn):
    T = z.shape[0]
    W_A = ln_g.shape[1]
    G = W_A // GROUP_DIM

    def body(z_ref, dy_ref, lg_ref, lb_ref, w_ref, bt_ref, gn_ref,
             dz_ref, dlg_ref, dlb_ref, dw_ref, db_ref, dgn_ref):
        @pl.when(pl.program_id(0) == 0)
        def _():
            for r in (dlg_ref, dlb_ref, dw_ref, db_ref, dgn_ref):
                r[...] = jnp.zeros_like(r)

        mask = _chunk_mask()
        zu = z_ref[:, :W_A]
        zv = z_ref[:, W_A:]
        u = _gelu(zu)
        v = _gelu(zv)
        saved, cols = [], []
        for g in range(G):
            sl = slice(g * GROUP_DIM, (g + 1) * GROUP_DIM)
            wm = jnp.where(mask, w_ref[g], 0.0)
            vhat, rstd, vn, mixed = _sgu_group_forward(
                v[:, sl], lg_ref[:, sl], lb_ref[:, sl], wm.astype(BF16), bt_ref[:, g:g + 1])
            saved.append((wm, vhat, rstd, vn, mixed))
            cols.append(u[:, sl] * mixed)
        ya = jnp.concatenate(cols, axis=1)
        rstd_a = lax.rsqrt(jnp.mean(ya * ya, axis=-1, keepdims=True) + EPS)
        ya_hat = ya * rstd_a
        dyv = dy_ref[...].astype(F32)
        dgn_ref[...] += jnp.sum(dyv * ya_hat, axis=0, keepdims=True)
        t = dyv * gn_ref[...]
        dya = rstd_a * (t - ya_hat * jnp.mean(t * ya_hat, axis=-1, keepdims=True))
        du_cols, dv_cols, dlg_cols, dlb_cols = [], [], [], []
        for g in range(G):
            sl = slice(g * GROUP_DIM, (g + 1) * GROUP_DIM)
            wm, vhat, rstd, vn, mixed = saved[g]
            dya_g = dya[:, sl]
            du_cols.append(dya_g * mixed)
            dmix = dya_g * u[:, sl]
            dmix_bf = dmix.astype(BF16)
            db_ref[g] += jnp.sum(dmix, axis=1, keepdims=True)
            dw = lax.dot_general(dmix_bf, vn.astype(BF16), (((1,), (1,)), ((), ())),
                                 preferred_element_type=F32)
            dw_ref[g] += jnp.where(mask, dw, 0.0)
            dvn = jnp.dot(wm.T.astype(BF16), dmix_bf, preferred_element_type=F32)
            dlg_cols.append(jnp.sum(dvn * vhat, axis=0, keepdims=True))
            dlb_cols.append(jnp.sum(dvn, axis=0, keepdims=True))
            dvhat = dvn * lg_ref[:, sl]
            dv_cols.append(rstd * (dvhat - jnp.mean(dvhat, axis=-1, keepdims=True)
                                   - vhat * jnp.mean(dvhat * vhat, axis=-1, keepdims=True)))
        dlg_ref[...] += jnp.concatenate(dlg_cols, axis=1)
        dlb_ref[...] += jnp.concatenate(dlb_cols, axis=1)
        dz_ref[:, :W_A] = (jnp.concatenate(du_cols, axis=1) * _gelu_grad(zu)).astype(BF16)
        dz_ref[:, W_A:] = (jnp.concatenate(dv_cols, axis=1) * _gelu_grad(zv)).astype(BF16)

    vec = pl.BlockSpec((1, W_A), lambda i: (0, 0))
    wspec = pl.BlockSpec((G, SGU_BLOCK, SGU_BLOCK), lambda i: (0, 0, 0))
    return pl.pallas_call(
        body, name=name, grid=(T // SGU_BLOCK,),
        in_specs=[pl.BlockSpec((SGU_BLOCK, 2 * W_A), lambda i: (i, 0)),
                  pl.BlockSpec((SGU_BLOCK, W_A), lambda i: (i, 0)), vec, vec, wspec,
                  pl.BlockSpec((SGU_BLOCK, G), lambda i: (0, 0)), vec],
        out_specs=[pl.BlockSpec((SGU_BLOCK, 2 * W_A), lambda i: (i, 0)), vec, vec, wspec,
                   pl.BlockSpec((G, SGU_BLOCK, 1), lambda i: (0, 0, 0)), vec],
        out_shape=[jax.ShapeDtypeStruct((T, 2 * W_A), BF16), jax.ShapeDtypeStruct((1, W_A), F32),
                   jax.ShapeDtypeStruct((1, W_A), F32),
                   jax.ShapeDtypeStruct((G, SGU_BLOCK, SGU_BLOCK), F32),
                   jax.ShapeDtypeStruct((G, SGU_BLOCK, 1), F32),
                   jax.ShapeDtypeStruct((1, W_A), F32)],
        compiler_params=_params(1))(z, dy, ln_g, ln_b, w_s, b_t, gn)


def _split_dot(x, tri):
    hi = x.astype(BF16)
    lo = (x - hi.astype(F32)).astype(BF16)
    return (jnp.dot(hi, tri, preferred_element_type=F32)
            + jnp.dot(lo, tri, preferred_element_type=F32))


def _tri(n, rel):
    r = lax.broadcasted_iota(jnp.int32, (n, n), 0)
    c = lax.broadcasted_iota(jnp.int32, (n, n), 1)
    return rel(r, c).astype(BF16)


def _dot_nt(a, b):
    return lax.dot_general(a, b, (((1,), (1,)), ((), ())), preferred_element_type=F32)


def _dot_tn(a, b):
    return lax.dot_general(a, b, (((0,), (0,)), ((), ())), preferred_element_type=F32)


def _sb_scores(qs, kj, mask):
    zz = _dot_nt(qs, kj)
    log_beta = jnp.minimum(zz, 0.0) - jnp.log(1.0 + jnp.exp(-jnp.abs(zz)))
    log_1m = log_beta - zz
    if mask is not None:
        log_1m = jnp.where(mask, log_1m, 0.0)
    return log_beta, log_1m


def _masked(mask, x):
    return x if mask is None else jnp.where(mask, x, 0.0)


def _below(old, new, row0):
    if row0 == 0:
        return tuple(new)
    return tuple(jnp.concatenate([o[:row0], n], axis=0) for o, n in zip(old, new))


def _sb_tiles(T):
    tk = _pick(256, [T])
    tq = 2 * tk if T % (2 * tk) == 0 else tk
    return tq, tk


def _sb_cols(w_a, w_b):
    base = 2 * w_a // GROUP_DIM
    per = w_b // GROUP_DIM
    return base, base + per, base + 2 * per


def _sb_forward(name, z, w_a, w_b):
    T = z.shape[0]
    H = w_b // GROUP_DIM
    tq, tk = _sb_tiles(T)
    per = tq // tk
    qc, kc, vc = _sb_cols(w_a, w_b)
    scale = GROUP_DIM ** -0.5

    def body(q_ref, k_ref, v_ref, y_ref, tot_ref):
        i = pl.program_id(1)
        qs = (q_ref[...] * scale).astype(BF16)
        upper = _tri(tk, lambda r, c: r > c)
        ahead = (lax.broadcasted_iota(jnp.int32, (tq, tk), 1)
                 - lax.broadcasted_iota(jnp.int32, (tq, tk), 0))

        def step(j, carry, masked, row0=0):
            acc, later = (c[row0:] for c in carry)
            k0 = pl.multiple_of(j * tk, tk)
            kj = k_ref[pl.ds(k0, tk), :].astype(BF16)
            vj = v_ref[pl.ds(k0, tk), :].astype(BF16)
            mask = ahead[row0:] < i * tq - k0 if masked else None
            log_beta, log_1m = _sb_scores(qs[row0:], kj, mask)
            rest = _split_dot(log_1m, upper) + later
            a = _masked(mask, jnp.exp(log_beta + rest))
            acc = acc + jnp.dot(a.astype(BF16), vj, preferred_element_type=F32)
            later = later + jnp.sum(log_1m, axis=1, keepdims=True)
            return _below(carry, (acc, later), row0)

        def blocks(p, c):
            for d in reversed(range(per)):
                c = step(p * per + d, c, False)
            return c

        carry = (jnp.zeros((tq, GROUP_DIM), F32), jnp.zeros((tq, 1), F32))
        for d in reversed(range(per)):
            carry = step(i * per + d, carry, True, d * tk)
        acc, total = lax.fori_loop(0, i, lambda pp, c: blocks(i - 1 - pp, c), carry)
        y_ref[...] = acc
        tot_ref[...] = total

    return pl.pallas_call(
        body, name=name, grid=(H, T // tq),
        in_specs=[pl.BlockSpec((tq, GROUP_DIM), lambda h, i: (i, qc + h)),
                  pl.BlockSpec((T, GROUP_DIM), lambda h, i: (0, kc + h)),
                  pl.BlockSpec((T, GROUP_DIM), lambda h, i: (0, vc + h))],
        out_specs=[pl.BlockSpec((tq, GROUP_DIM), lambda h, i: (i, h)),
                   pl.BlockSpec((None, tq, 1), lambda h, i: (h, i, 0))],
        out_shape=[jax.ShapeDtypeStruct((T, w_b), F32), jax.ShapeDtypeStruct((H, T, 1), F32)],
        compiler_params=_params(2))(z, z, z)


def _sb_backward(name, z, do, total, w_a, w_b):
    T = z.shape[0]
    H = w_b // GROUP_DIM
    tq, tk = _sb_tiles(T)
    per = tq // tk
    qc, kc, vc = _sb_cols(w_a, w_b)
    scale = GROUP_DIM ** -0.5

    def body(q_ref, k_ref, v_ref, do_ref, tot_ref, dq_ref, dkv_ref):
        i = pl.program_id(1)

        @pl.when(i == 0)
        def _():
            dkv_ref[...] = jnp.zeros_like(dkv_ref)

        qs = (q_ref[...] * scale).astype(BF16)
        dob = do_ref[...].astype(BF16)
        upto = _tri(tk, lambda r, c: r <= c)
        before = _tri(tk, lambda r, c: r < c)
        ahead = (lax.broadcasted_iota(jnp.int32, (tq, tk), 1)
                 - lax.broadcasted_iota(jnp.int32, (tq, tk), 0))

        def step(j, carry, masked, row0=0):
            dq, left, e_seen = (c[row0:] for c in carry)
            qr, dor = qs[row0:], dob[row0:]
            k0 = pl.multiple_of(j * tk, tk)
            kj = k_ref[pl.ds(k0, tk), :].astype(BF16)
            vj = v_ref[pl.ds(k0, tk), :].astype(BF16)
            mask = ahead[row0:] < i * tq - k0 if masked else None
            log_beta, log_1m = _sb_scores(qr, kj, mask)
            rest = left - _split_dot(log_1m, upto)
            a = _masked(mask, jnp.exp(log_beta + rest))
            e = a * _dot_nt(dor, vj)
            e_before = e_seen + jnp.dot(e.astype(BF16), before, preferred_element_type=F32)
            beta = jnp.exp(log_beta)
            dz = _masked(mask, e * (1.0 - beta) - beta * e_before).astype(BF16)
            dq = dq + jnp.dot(dz, kj, preferred_element_type=F32)
            dkv_ref[0, pl.ds(k0, tk), :] += _dot_tn(dz, qr)
            dkv_ref[1, pl.ds(k0, tk), :] += _dot_tn(a.astype(BF16), dor)
            left = left - jnp.sum(log_1m, axis=1, keepdims=True)
            e_seen = e_seen + jnp.sum(e, axis=1, keepdims=True)
            return _below(carry, (dq, left, e_seen), row0)

        def blocks(p, c):
            for d in range(per):
                c = step(p * per + d, c, False)
            return c

        carry = (jnp.zeros((tq, GROUP_DIM), F32), tot_ref[...], jnp.zeros((tq, 1), F32))
        carry = lax.fori_loop(0, i, blocks, carry)
        for d in range(per):
            carry = step(i * per + d, carry, True, d * tk)
        dq_ref[...] = (carry[0] * scale).astype(BF16)

    return pl.pallas_call(
        body, name=name, grid=(H, T // tq),
        in_specs=[pl.BlockSpec((tq, GROUP_DIM), lambda h, i: (i, qc + h)),
                  pl.BlockSpec((T, GROUP_DIM), lambda h, i: (0, kc + h)),
                  pl.BlockSpec((T, GROUP_DIM), lambda h, i: (0, vc + h)),
                  pl.BlockSpec((tq, GROUP_DIM), lambda h, i: (i, h)),
                  pl.BlockSpec((None, tq, 1), lambda h, i: (h, i, 0))],
        out_specs=[pl.BlockSpec((tq, GROUP_DIM), lambda h, i: (i, h)),
                   pl.BlockSpec((2, T, GROUP_DIM), lambda h, i: (0, 0, h))],
        out_shape=[jax.ShapeDtypeStruct((T, w_b), BF16), jax.ShapeDtypeStruct((2, T, w_b), F32)],
        compiler_params=_params(2))(z, z, z, do, total)


def _softmax_rows(s):
    m = jnp.max(s, axis=-1, keepdims=True)
    p = jnp.exp(s - m)
    return p / jnp.sum(p, axis=-1, keepdims=True)


def _xattn_forward(name, q, kv):
    T, D = q.shape
    Nm = kv.shape[0]
    dh = D // X_HEADS
    tq = _pick(512, [T])

    def body(q_ref, k_ref, v_ref, o_ref):
        p = _softmax_rows(_dot_nt(q_ref[...], k_ref[...]))
        o_ref[...] = jnp.dot(p.astype(BF16), v_ref[...], preferred_element_type=F32).astype(BF16)

    return pl.pallas_call(
        body, name=name, grid=(T // tq, X_HEADS),
        in_specs=[pl.BlockSpec((tq, dh), lambda i, h: (i, h)),
                  pl.BlockSpec((Nm, dh), lambda i, h: (0, h)),
                  pl.BlockSpec((Nm, dh), lambda i, h: (0, X_HEADS + h))],
        out_specs=pl.BlockSpec((tq, dh), lambda i, h: (i, h)),
        out_shape=jax.ShapeDtypeStruct((T, D), BF16),
        compiler_params=_params(2))(q, kv, kv)


def _xattn_backward(name, q, kv, do):
    T, D = q.shape
    Nm = kv.shape[0]
    dh = D // X_HEADS
    tq = _pick(512, [T])
    scale = dh ** -0.5

    def body(q_ref, k_ref, v_ref, do_ref, dq_ref, dkv_ref):
        @pl.when(pl.program_id(1) == 0)
        def _():
            dkv_ref[...] = jnp.zeros_like(dkv_ref)

        qv, kk, vv, dov = q_ref[...], k_ref[...], v_ref[...], do_ref[...]
        p = _softmax_rows(_dot_nt(qv, kk))
        dp = _dot_nt(dov, vv)
        ds = (p * (dp - jnp.sum(dp * p, axis=-1, keepdims=True))).astype(BF16)
        dq_ref[...] = (jnp.dot(ds, kk, preferred_element_type=F32) * scale).astype(BF16)
        dkv_ref[0] += _dot_tn(ds, qv)
        dkv_ref[1] += _dot_tn(p.astype(BF16), dov)

    blk = pl.BlockSpec((tq, dh), lambda h, i: (i, h))
    return pl.pallas_call(
        body, name=name, grid=(X_HEADS, T // tq),
        in_specs=[blk, pl.BlockSpec((Nm, dh), lambda h, i: (0, h)),
                  pl.BlockSpec((Nm, dh), lambda h, i: (0, X_HEADS + h)), blk],
        out_specs=[blk, pl.BlockSpec((2, Nm, dh), lambda h, i: (0, 0, h))],
        out_shape=[jax.ShapeDtypeStruct((T, D), BF16), jax.ShapeDtypeStruct((2, Nm, D), F32)],
        compiler_params=_params(2))(q, kv, kv, do)


def _position():
    x, y, c = lax.axis_index("x"), lax.axis_index("y"), lax.axis_index("c")
    other_chips = [(1 - x, y), (x, 1 - y), (1 - x, 1 - y)]
    return x, y, c, other_chips


def _hbm_spec():
    return pl.BlockSpec(memory_space=pltpu.HBM)


def _sem_spec():
    return pl.BlockSpec(memory_space=pltpu.SEMAPHORE)


def _split_start(name, arrays, make_copies, n_sems, deps=()):
    n, d = len(arrays), len(deps)

    def body(*refs):
        ins = refs[:n]
        send_sems, recv_sems = refs[n + d], refs[n + d + 1]
        token = refs[-1]
        for cp in make_copies(ins, send_sems, recv_sems):
            cp.start()
        token[...] = jnp.zeros_like(token)

    res = pl.pallas_call(
        body, name=name,
        out_shape=(pltpu.SemaphoreType.DMA((n_sems,)), pltpu.SemaphoreType.DMA((n_sems,)),
                   *[pltpu.HBM(a.shape, a.dtype) for a in arrays],
                   jax.ShapeDtypeStruct((SUBLANE, LANE), F32)),
        in_specs=[_hbm_spec()] * n + [_any_spec()] * d,
        out_specs=(_sem_spec(), _sem_spec(), *[_hbm_spec()] * n,
                   pl.BlockSpec(memory_space=pltpu.VMEM)),
        input_output_aliases={i: 2 + i for i in range(n)},
        compiler_params=pltpu.CompilerParams(
            has_side_effects=pltpu.SideEffectType.DATAFLOW_SIDE_EFFECTING),
    )(*[pltpu.with_memory_space_constraint(a, pltpu.HBM) for a in arrays], *deps)
    return res[0], res[1], list(res[2:2 + n]), res[-1]


def _split_wait(name, arrays, send_sems, recv_sems, after, make_copies):
    n = len(arrays)
    after = list(after) if isinstance(after, (list, tuple)) else [after]

    def body(*refs):
        ins = refs[:n]
        send_ref, recv_ref = refs[n], refs[n + 1]
        for cp in make_copies(ins, send_ref, recv_ref):
            cp.wait_send()
            cp.wait_recv()

    return pl.pallas_call(
        body, name=name,
        out_shape=tuple(pltpu.HBM(a.shape, a.dtype) for a in arrays),
        in_specs=[_hbm_spec()] * n + [_sem_spec(), _sem_spec()] + [_any_spec()] * len(after),
        out_specs=tuple(_hbm_spec() for _ in arrays),
        input_output_aliases={i: i for i in range(n)},
        compiler_params=pltpu.CompilerParams(
            has_side_effects=pltpu.SideEffectType.DATAFLOW_SIDE_EFFECTING),
    )(*arrays, send_sems, recv_sems, *after)


def _gather_copies(refs, send_sems, recv_sems):
    x, y, c, chips = _position()
    me = 2 * x + y
    copies = []
    for i, ref in enumerate(refs):
        rows = ref.shape[1] // 2
        piece = ref.at[me, pl.ds(c * rows, rows), :]
        for j, (px, py) in enumerate(chips):
            copies.append(pltpu.make_async_remote_copy(
                src_ref=piece, dst_ref=piece, send_sem=send_sems.at[3 * i + j],
                recv_sem=recv_sems.at[3 * i + j], device_id=(px, py, c), device_id_type=MESH))
    return copies


def _near_copies(refs, send_sems, recv_sems):
    x, y, c, chips = _position()
    me = 2 * x + y
    copies = []
    for i, ref in enumerate(refs):
        rows = ref.shape[1] // 2
        piece = ref.at[me, pl.ds(c * rows, rows), :]
        for j, (px, py) in enumerate(chips[:2]):
            copies.append(pltpu.make_async_remote_copy(
                src_ref=piece, dst_ref=piece, send_sem=send_sems.at[2 * i + j],
                recv_sem=recv_sems.at[2 * i + j], device_id=(px, py, c), device_id_type=MESH))
    return copies


def _relay_copies(refs, send_sems, recv_sems):
    x, y, c, chips = _position()
    copies = []
    for i, ref in enumerate(refs):
        rows = ref.shape[1] // 4
        for j, (px, py) in enumerate(chips[:2]):
            ox, oy = chips[1 - j]
            piece = ref.at[2 * ox + oy, pl.ds((2 * c + j) * rows, rows), :]
            copies.append(pltpu.make_async_remote_copy(
                src_ref=piece, dst_ref=piece, send_sem=send_sems.at[2 * i + j],
                recv_sem=recv_sems.at[2 * i + j], device_id=(px, py, c), device_id_type=MESH))
    return copies


def _share_copies(refs, send_sems, recv_sems):
    x, y, c, _ = _position()
    copies = []
    for i, ref in enumerate(refs):
        rows = ref.shape[0] // 2
        mine = ref.at[pl.ds(c * rows, rows), :]
        copies.append(pltpu.make_async_remote_copy(
            src_ref=mine, dst_ref=mine, send_sem=send_sems.at[i], recv_sem=recv_sems.at[i],
            device_id=(x, y, 1 - c), device_id_type=MESH))
    return copies


def _scatter_copies(refs, send_sems, recv_sems):
    x, y, c, chips = _position()
    n = len(refs) // 2
    copies = []
    for i in range(n):
        for j, (px, py) in enumerate(chips):
            copies.append(pltpu.make_async_remote_copy(
                src_ref=refs[i].at[2 * px + py], dst_ref=refs[n + i].at[j],
                send_sem=send_sems.at[3 * i + j], recv_sem=recv_sems.at[3 * i + j],
                device_id=(px, py, c), device_id_type=MESH))
    return copies


def _cast_own(name, place, shard):
    rows, cols = shard.shape
    tr = _block_rows(rows, cols)

    def body(place_ref, w_ref, o_ref):
        o_ref[...] = w_ref[...].astype(BF16)

    grid_spec = pltpu.PrefetchScalarGridSpec(
        num_scalar_prefetch=1, grid=(rows // tr,),
        in_specs=[pl.BlockSpec((tr, cols), lambda r, pr: (r, 0))],
        out_specs=pl.BlockSpec((None, tr, cols), lambda r, pr: (pr[0], r, 0)))
    return pl.pallas_call(
        body, name=name, grid_spec=grid_spec,
        out_shape=jax.ShapeDtypeStruct((N_CHIPS, rows, cols), BF16),
        compiler_params=_params(1))(place, shard)


def _forward_to_sibling(name, arrays, deps=()):
    n = len(arrays)

    def body(*refs):
        ins = refs[:n]
        send_sems, recv_sems = refs[-2:]
        x, y, c, chips = _position()
        sends = []
        for i in range(n):
            rows = ins[i].shape[1] // 2
            for j, (px, py) in enumerate(chips):
                piece = ins[i].at[2 * px + py, pl.ds(c * rows, rows), :]
                cp = pltpu.make_async_remote_copy(
                    src_ref=piece, dst_ref=piece, send_sem=send_sems.at[i, j],
                    recv_sem=recv_sems.at[i, j], device_id=(x, y, 1 - c), device_id_type=MESH)
                cp.start()
                sends.append(cp)
        for i in range(n):
            rows = ins[i].shape[1] // 2
            for j, (px, py) in enumerate(chips):
                piece = ins[i].at[2 * px + py, pl.ds((1 - c) * rows, rows), :]
                pltpu.make_async_remote_copy(
                    src_ref=piece, dst_ref=piece, send_sem=send_sems.at[i, j],
                    recv_sem=recv_sems.at[i, j], device_id=(x, y, 1 - c),
                    device_id_type=MESH).wait_recv()
        for cp in sends:
            cp.wait_send()

    return pl.pallas_call(
        body, name=name,
        in_specs=[_any_spec()] * (n + len(deps)), out_specs=[_any_spec()] * n,
        out_shape=[jax.ShapeDtypeStruct(a.shape, a.dtype) for a in arrays],
        input_output_aliases={i: i for i in range(n)},
        scratch_shapes=[pltpu.SemaphoreType.DMA((n, 3))] * 2,
    )(*arrays, *deps)


def _swap_copies(refs, send_sems, recv_sems):
    x, y, c, _ = _position()
    n = len(refs) // 2
    copies = []
    for i in range(n):
        rows = refs[i].shape[1] // 2
        copies.append(pltpu.make_async_remote_copy(
            src_ref=refs[i].at[:, pl.ds((1 - c) * rows, rows), :], dst_ref=refs[n + i],
            send_sem=send_sems.at[i], recv_sem=recv_sems.at[i],
            device_id=(x, y, 1 - c), device_id_type=MESH))
    return copies


def _small_copies(refs, send_sems, recv_sems):
    packed, slots = refs
    x, y, c, _ = _position()
    me = 4 * x + 2 * y + c
    copies = []
    for r in range(1, N_DEV):
        peer = (x ^ ((r >> 2) & 1), y ^ ((r >> 1) & 1), c ^ (r & 1))
        copies.append(pltpu.make_async_remote_copy(
            src_ref=packed, dst_ref=slots.at[me], send_sem=send_sems.at[r - 1],
            recv_sem=recv_sems.at[r - 1], device_id=peer, device_id_type=MESH))
    return copies


def _block_rows(rows, cols, itemsize=4, target=1 << 20):
    return _pick(max(BF16_ROWS, target // (cols * itemsize)), [rows], unit=BF16_ROWS)


def _pair_sum(name, place, grad, received):
    P, rows, cols = received.shape
    tr = _block_rows(rows, cols, itemsize=2, target=2 << 20)
    nb = rows // tr

    def body(place_ref, g_ref, r_ref, o_ref):
        o_ref[...] = (g_ref[...].astype(F32) + r_ref[...].astype(F32)).astype(BF16)

    def panel(j, pr):
        return pr[0] ^ jnp.where(j == 2, 3, 2 - j)

    grid_spec = pltpu.PrefetchScalarGridSpec(
        num_scalar_prefetch=1, grid=(P - 1, nb),
        in_specs=[pl.BlockSpec((None, tr, cols),
                               lambda j, r, pr: (panel(j, pr), pr[1] * nb + r, 0)),
                  pl.BlockSpec((None, tr, cols), lambda j, r, pr: (panel(j, pr), r, 0))],
        out_specs=pl.BlockSpec((None, tr, cols), lambda j, r, pr: (panel(j, pr), r, 0)))
    return pl.pallas_call(
        body, name=name, grid_spec=grid_spec,
        out_shape=jax.ShapeDtypeStruct(received.shape, BF16),
        compiler_params=_params(2))(place, grad, received)


def _final_sum(name, place, grad, received, from_chips):
    _, rows, cols = received.shape
    tr = _block_rows(rows, cols, target=2 << 20)
    nb = rows // tr

    def body(place_ref, g_ref, r_ref, c_ref, o_ref):
        acc = g_ref[...].astype(F32) + r_ref[...].astype(F32)
        for j in range(3):
            acc = acc + c_ref[j].astype(F32)
        o_ref[...] = acc

    grid_spec = pltpu.PrefetchScalarGridSpec(
        num_scalar_prefetch=1, grid=(nb,),
        in_specs=[pl.BlockSpec((None, tr, cols), lambda r, pr: (pr[0], pr[1] * nb + r, 0)),
                  pl.BlockSpec((None, tr, cols), lambda r, pr: (pr[0], r, 0)),
                  pl.BlockSpec((3, tr, cols), lambda r, pr: (0, r, 0))],
        out_specs=pl.BlockSpec((tr, cols), lambda r, pr: (pr[1] * nb + r, 0)))
    return pl.pallas_call(
        body, name=name, grid_spec=grid_spec,
        out_shape=jax.ShapeDtypeStruct((2 * rows, cols), F32),
        compiler_params=_params(1))(place, grad, received, from_chips)


def _sum_devices(name, me, gathered, own):
    n_dev, rows, cols = gathered.shape
    tr = _pick(256, [rows])

    def body(me_ref, g_ref, own_ref, o_ref):
        term = lambda d: jnp.where(me_ref[0] == d, own_ref[...], g_ref[d])
        acc = term(0)
        for d in range(1, n_dev):
            acc = acc + term(d)
        o_ref[...] = acc

    grid_spec = pltpu.PrefetchScalarGridSpec(
        num_scalar_prefetch=1, grid=(rows // tr,),
        in_specs=[pl.BlockSpec((n_dev, tr, cols), lambda r, me_ref: (0, r, 0)),
                  pl.BlockSpec((tr, cols), lambda r, me_ref: (r, 0))],
        out_specs=pl.BlockSpec((tr, cols), lambda r, me_ref: (r, 0)))
    return pl.pallas_call(
        body, name=name, grid_spec=grid_spec,
        out_shape=jax.ShapeDtypeStruct((rows, cols), F32),
        compiler_params=_params(1))(me, gathered, own)


def _adamw(name, w, g, m, v):
    rows, cols = w.shape
    tr = _block_rows(rows, cols)
    c1 = 1.0 / (1.0 - ADAM_B1 ** ADAM_STEP)
    c2 = 1.0 / (1.0 - ADAM_B2 ** ADAM_STEP)

    def body(w_ref, g_ref, m_ref, v_ref, go_ref, d_ref, nm_ref, nv_ref):
        gv = g_ref[...]
        go_ref[...] = gv
        nm = ADAM_B1 * m_ref[...] + (1.0 - ADAM_B1) * gv
        nv = ADAM_B2 * v_ref[...] + (1.0 - ADAM_B2) * (gv * gv)
        nm_ref[...] = nm
        nv_ref[...] = nv
        d_ref[...] = -ADAM_LR * ((nm * c1) / (jnp.sqrt(nv * c2) + ADAM_EPS) + ADAM_WD * w_ref[...])

    blk = pl.BlockSpec((tr, cols), lambda r: (r, 0))
    shape = jax.ShapeDtypeStruct((rows, cols), F32)
    return pl.pallas_call(
        body, name=name, grid=(rows // tr,), in_specs=[blk] * 4, out_specs=[blk] * 4,
        out_shape=[shape] * 4, compiler_params=_params(1))(w, g, m, v)


BIG = ("ffn1_w_in", "ffn1_w_out", "w_mix_in", "w_mix_out", "w_cq", "w_ckv", "w_co",
       "ffn2_w_in", "ffn2_w_out")
BIG_KIND = {"ffn1_w_in": "c", "ffn1_w_out": "r", "w_mix_in": "c", "w_mix_out": "r", "w_cq": "r",
            "w_ckv": "c", "w_co": "r", "ffn2_w_in": "c", "ffn2_w_out": "r"}
GATHER_GROUPS = (("ffn1_in", ("ffn1_w_in",)), ("ffn1_out", ("ffn1_w_out",)),
                 ("mix_in", ("w_mix_in",)), ("mix_out", ("w_mix_out",)),
                 ("cross", ("w_cq", "w_ckv", "w_co")),
                 ("ffn2_in", ("ffn2_w_in",)), ("ffn2_out", ("ffn2_w_out",)))
GATHER_AFTER = (("ffn1_in", None), ("ffn1_out", "ffn1_in"), ("mix_in", "ffn1_out"),
                ("mix_out", "mix_in"), ("cross", "mix_in"), ("ffn2_in", "mix_in"),
                ("ffn2_out", "cross"))
RELAYED = ("ffn1_in",)
TAIL_STAGES = (("sum", "ffn2"), ("sum", "cross"), ("sum", "mix"), ("sum", "ffn1_out"),
               ("update", "ffn2"), ("update", "cross"), ("sum", "ffn1_in"), ("update", "mix"),
               ("update", "ffn1_out"), ("update", "ffn1_in"))
SMALL = ("ffn1_norm", "mix_norm", "ln_v_gain", "ln_v_bias", "spatial_w", "spatial_b", "gnorm_a",
         "gnorm_b", "cross_norm", "mem_norm", "ffn2_norm", "final_norm")
WEIGHTS = ("ffn1_norm", "ffn1_w_in", "ffn1_w_out", "mix_norm", "w_mix_in", "ln_v_gain",
           "ln_v_bias", "spatial_w", "spatial_b", "gnorm_a", "gnorm_b", "w_mix_out", "cross_norm",
           "mem_norm", "w_cq", "w_ckv", "w_co", "ffn2_norm", "ffn2_w_in", "ffn2_w_out",
           "final_norm")


def _pack(arrays):
    return jnp.concatenate([a.reshape(-1, LANE) for a in arrays], axis=0)


def _unpack(packed, like):
    out, row = [], 0
    for a in like:
        rows = a.size // LANE
        out.append(packed[row:row + rows].reshape(a.shape))
        row += rows
    return out


def _local_step(x, mem, target, small, weights_of, start_tokens, grads_ready, grads_flush):
    T, D = x.shape
    vec = lambda name: small[name].reshape(1, -1)
    w_a = small["ln_v_gain"].size
    w_b = small["gnorm_b"].size
    G = w_a // GROUP_DIM
    w_s = small["spatial_w"].reshape(G, SGU_BLOCK, SGU_BLOCK)
    b_t = small["spatial_b"].reshape(G, SGU_BLOCK).T

    h1, ffn1_saved = _ffn_forward("ffn1", x, vec("ffn1_norm"), weights_of, deps=start_tokens)
    n2 = _rmsnorm_fwd("mix_norm", h1, vec("mix_norm"))
    big = weights_of("mix_in", n2)
    (z,) = _matmul("mix_in", Mat(n2), big["w_mix_in"], "nn", [("c", 1, F32)], tm=2048, tn=256)
    z = z[0]
    y = _sgu_forward("sgu", z, vec("ln_v_gain"), vec("ln_v_bias"), w_s, b_t, vec("gnorm_a"), D)
    yb, sb_total = _sb_forward("stickbreak", z, w_a, w_b)
    y = _rmsnorm_fwd("gnorm_b", yb, vec("gnorm_b"), into=y, col=w_a // w_b)

    def add_res(acc, ex, out):
        out[0][...] = ex[0][...] + acc

    big.update(weights_of("mix_out", y))
    (h2,) = _matmul("mix_out", Mat(y), big["w_mix_out"], "nn", [("c", 1, F32)],
                    tm=1024, tn=1024, extras=[Mat(h1)], epi=add_res)
    h2 = h2[0]
    n3 = _rmsnorm_fwd("cross_norm", h2, vec("cross_norm"))
    memn = _rmsnorm_fwd("mem_norm", mem, vec("mem_norm"))
    big.update(weights_of("cross", n3))
    x_scale = (D // X_HEADS) ** -0.5

    def scaled(acc, ex, out):
        out[0][...] = (acc * x_scale).astype(BF16)

    (q,) = _matmul("cross_q", Mat(n3), big["w_cq"], "nn", [("c", 1, BF16)],
                   tm=1024, tn=1024, epi=scaled)
    (kv,) = _matmul("cross_kv", Mat(memn), big["w_ckv"], "nn", [("c", 1, BF16)], tm=256, tn=1024)
    q, kv = q[0], kv[0]
    o = _xattn_forward("cross_attn", q, kv)
    (h3,) = _matmul("cross_out", Mat(o), big["w_co"], "nn", [("c", 1, F32)],
                    tm=1024, tn=1024, extras=[Mat(h2)], epi=add_res)
    h3 = h3[0]
    h4, ffn2_saved = _ffn_forward("ffn2", h3, vec("ffn2_norm"), weights_of)

    gs = {}
    loss_tile, dh4, dh4_bf, gs["final_norm"] = _loss_head("loss_head", h4, vec("final_norm"), target)
    dh3, dh3_bf, gs["ffn2_norm"] = _ffn_backward(
        "ffn2", h3, vec("ffn2_norm"), ffn2_saved, dh4, dh4_bf, grads_ready, grads_flush)

    (do,) = _matmul("cross_do", Mat(dh3_bf), big["w_co"], "nt", [("c", 1, BF16)], tm=1024, tn=512)
    (dw_co,) = _matmul("cross_dwo", Mat(o), Mat(dh3_bf), "tn", [("r", N_CHIPS, BF16)],
                       tm=512, tn=1024)
    dq, dkv = _xattn_backward("cross_attn_bwd", q, kv, do[0])
    (dw_cq,) = _matmul("cross_dwq", Mat(n3), Mat(dq), "tn", [("r", N_CHIPS, BF16)],
                       tm=512, tn=1024)
    (dw_ckv,) = _matmul("cross_dwkv", Mat(memn), Mat(dkv), "tn", [("c", N_CHIPS, BF16)],
                        tm=1024, tn=1024)
    token = grads_ready("cross", {"w_cq": dw_cq, "w_ckv": dw_ckv, "w_co": dw_co})
    dq = _tie("cross_dq_after_swap", dq, [token])
    (dn3,) = _matmul("cross_dn", Mat(dq), big["w_cq"], "nt", [("c", 1, F32)], tm=1024, tn=512)
    (dmemn,) = _matmul("cross_dmem", Mat(dkv), big["w_ckv"], "nt", [("c", 1, F32)],
                       tm=256, tn=1024, tk=1024)
    (gs["mem_norm"],) = _rmsnorm_bwd("mem_dnorm", mem, vec("mem_norm"), dmemn[0], want_dx=False)
    dn3 = _tie("cross_dn_after_scatter", dn3, [grads_flush("cross", gs["mem_norm"])])
    dh2, dh2_bf, gs["cross_norm"] = _rmsnorm_bwd("cross_dnorm", h2, vec("cross_norm"), dn3[0],
                                                 dres=dh3)

    (dy,) = _matmul("mix_dy", Mat(dh2_bf), big["w_mix_out"], "nt", [("c", 1, F32)], tm=1024, tn=512)
    dy = dy[0]
    (dw_mix_out,) = _matmul("mix_dwout", Mat(y), Mat(dh2_bf), "tn", [("r", N_CHIPS, BF16)],
                            tm=512, tn=1024)
    dza, gs["ln_v_gain"], gs["ln_v_bias"], gs["spatial_w"], db, gs["gnorm_a"] = _sgu_backward(
        "sgu_bwd", z, dy, vec("ln_v_gain"), vec("ln_v_bias"), w_s, b_t, vec("gnorm_a"))
    gs["spatial_b"] = db.reshape(G, SGU_BLOCK)
    dob, gs["gnorm_b"] = _rmsnorm_bwd("gnorm_b_bwd", yb, vec("gnorm_b"), dy, dn_col=w_a // w_b,
                                      want_bf16=False)
    dqb, dkvb = _sb_backward("stickbreak_bwd", z, dob, sb_total, w_a, w_b)
    dz = jnp.concatenate([dza, dqb, dkvb[0].astype(BF16), dkvb[1].astype(BF16)], axis=1)
    (dw_mix_in,) = _matmul("mix_dwin", Mat(n2), Mat(dz), "tn", [("c", N_CHIPS, BF16)],
                           tm=1024, tn=1280)
    token = grads_ready("mix", {"w_mix_in": dw_mix_in, "w_mix_out": dw_mix_out})
    dz = _tie("mix_dz_after_swap", dz, [token])
    (dn2,) = _matmul("mix_dn", Mat(dz), big["w_mix_in"], "nt", [("c", 1, F32)],
                     tm=1024, tn=1024, tk=1280)
    dn2 = _tie("mix_dn_after_scatter", dn2, [grads_flush("mix", dn2)])
    dh1, dh1_bf, gs["mix_norm"] = _rmsnorm_bwd("mix_dnorm", h1, vec("mix_norm"), dn2[0], dres=dh2)

    dx, _, gs["ffn1_norm"] = _ffn_backward(
        "ffn1", x, vec("ffn1_norm"), ffn1_saved, dh1, dh1_bf, grads_ready, grads_flush,
        early_out=True)
    gs = {k: g.reshape(small[k].shape) for k, g in gs.items()}
    return loss_tile, dx, gs


def kernel(x, mem, ffn1_norm, ffn1_w_in, ffn1_w_out, mix_norm, w_mix_in, ln_v_gain, ln_v_bias, spatial_w, spatial_b, gnorm_a, gnorm_b, w_mix_out, cross_norm, mem_norm, w_cq, w_ckv, w_co, ffn2_norm, ffn2_w_in, ffn2_w_out, final_norm, loss_target, m_ffn1_norm, m_ffn1_w_in, m_ffn1_w_out, m_mix_norm, m_w_mix_in, m_ln_v_gain, m_ln_v_bias, m_spatial_w, m_spatial_b, m_gnorm_a, m_gnorm_b, m_w_mix_out, m_cross_norm, m_mem_norm, m_w_cq, m_w_ckv, m_w_co, m_ffn2_norm, m_ffn2_w_in, m_ffn2_w_out, m_final_norm, v_ffn1_norm, v_ffn1_w_in, v_ffn1_w_out, v_mix_norm, v_w_mix_in, v_ln_v_gain, v_ln_v_bias, v_spatial_w, v_spatial_b, v_gnorm_a, v_gnorm_b, v_w_mix_out, v_cross_norm, v_mem_norm, v_w_cq, v_w_ckv, v_w_co, v_ffn2_norm, v_ffn2_w_in, v_ffn2_w_out, v_final_norm):
    given = dict(locals())
    w = {k: given[k] for k in WEIGHTS}
    m = {k: given["m_" + k] for k in WEIGHTS}
    v = {k: given["v_" + k] for k in WEIGHTS}

    cx, cy, cc = lax.axis_index("x"), lax.axis_index("y"), lax.axis_index("c")
    place = jnp.stack([2 * cx + cy, cc]).astype(jnp.int32)

    names_of = dict(GATHER_GROUPS)
    own = {g: [_cast_own(f"cast_{k}", place, w[k][0]) for k in names] for g, names in GATHER_GROUPS}
    gathers = {}

    def start_gather(group, deps):
        first_hop = _near_copies if group in RELAYED else _gather_copies
        n_sems = (2 if group in RELAYED else 3) * len(own[group])
        send, recv, arrays, token = _split_start(f"gather_start_{group}", own[group],
                                                 first_hop, n_sems, deps)
        gathers[group] = (send, recv, arrays)
        return token

    start_tokens = [start_gather(g, ()) for g, after in GATHER_AFTER if after is None]
    start_tokens += [a for g, after in GATHER_AFTER if after is not None for a in own[g]]

    def weights_of(group, after):
        send, recv, arrays = gathers[group]
        if group in RELAYED:
            arrays = _split_wait(f"gather_wait_{group}", arrays, send, recv, after, _near_copies)
            send, recv, arrays, _ = _split_start(f"gather_relay_{group}", list(arrays),
                                                 _relay_copies, 2 * len(arrays))
            arrays = _split_wait(f"gather_relay_wait_{group}", arrays, send, recv, after,
                                 _relay_copies)
        else:
            arrays = _split_wait(f"gather_wait_{group}", arrays, send, recv, after, _gather_copies)
        tokens = [start_gather(g, (arrays[0],)) for g, a in GATHER_AFTER if a == group]
        arrays = _forward_to_sibling(f"gather_forward_{group}", list(arrays), tokens)
        return {k: Mat(a, BIG_KIND[k]) for k, a in zip(names_of[group], arrays)}

    swaps, scatters = {}, {}

    def grads_ready(group, partial):
        names = list(partial)
        grads_ = [partial[k] for k in names]
        lands = [lax.empty((g.shape[0], g.shape[1] // 2, g.shape[2]), g.dtype) for g in grads_]
        send, recv, arrays, token = _split_start(f"swap_start_{group}", grads_ + lands,
                                                 _swap_copies, len(names))
        swaps[group] = (names, send, recv, arrays)
        return token

    def grads_flush(group, after):
        names, send, recv, arrays = swaps[group]
        arrays = _split_wait(f"swap_wait_{group}", arrays, send, recv, after, _swap_copies)
        grads_, from_sibling = arrays[:len(names)], arrays[len(names):]
        sums = [_pair_sum(f"pair_sum_{k}", place, g, r)
                for k, g, r in zip(names, grads_, from_sibling)]
        lands = [lax.empty((3,) + s.shape[1:], s.dtype) for s in sums]
        send, recv, arrays, token = _split_start(f"scatter_start_{group}", sums + lands,
                                                 _scatter_copies, 3 * len(names))
        scatters[group] = (names, grads_, from_sibling, send, recv, arrays)
        return token

    small = {k: w[k] for k in SMALL}
    loss_tile, grad_x, gs = _local_step(x[0], mem[0], loss_target[0], small, weights_of,
                                        start_tokens, grads_ready, grads_flush)

    packed = _pack([gs[k] for k in SMALL] + [loss_tile])
    slots = jnp.zeros((N_DEV,) + packed.shape, packed.dtype)
    small_send, small_recv, small_arrays, _ = _split_start(
        "small_start", [packed, slots], _small_copies, N_DEV - 1)

    grad, delta, new_m, new_v = {}, {}, {}, {}
    shares = {}
    after = [grad_x]
    for stage, group in TAIL_STAGES:
        if stage == "sum":
            names, grads_, from_sibling, send, recv, arrays = scatters[group]
            arrays = _split_wait(f"scatter_wait_{group}", arrays, send, recv, after,
                                 _scatter_copies)
            from_chips = arrays[len(names):]
            shards = [_final_sum(f"final_sum_{k}", place, g, r, f)
                      for k, g, r, f in zip(names, grads_, from_sibling, from_chips)]
            send, recv, shards, token = _split_start(f"share_start_{group}", shards,
                                                     _share_copies, len(names))
            shares[group] = (names, send, recv, shards)
            after = [token]
        else:
            names, send, recv, shards = shares[group]
            shards = _split_wait(f"share_wait_{group}", shards, send, recv, after, _share_copies)
            after = []
            for k, g_ in zip(names, shards):
                g_, d_, m_, v_ = _adamw(f"adamw_{k}", w[k][0], g_, m[k][0], v[k][0])
                grad[k], delta[k], new_m[k], new_v[k] = g_[None], d_[None], m_[None], v_[None]
                after.append(v_)

    packed, slots = _split_wait("small_wait", small_arrays, small_send, small_recv, after,
                                _small_copies)
    me = (4 * cx + 2 * cy + cc).astype(jnp.int32).reshape(1)
    total = _sum_devices("sum_small", me, slots, packed)
    n_small = total.shape[0] - SUBLANE
    loss = total[n_small, 0]
    small_g = total[:n_small]
    g_s, d_s, m_s, v_s = _adamw("adamw_small", _pack([w[k] for k in SMALL]), small_g,
                                _pack([m[k] for k in SMALL]), _pack([v[k] for k in SMALL]))
    like = [w[k] for k in SMALL]
    for k, g_, d_, m_, v_ in zip(SMALL, _unpack(g_s, like), _unpack(d_s, like),
                                 _unpack(m_s, like), _unpack(v_s, like)):
        grad[k], delta[k], new_m[k], new_v[k] = g_, d_, m_, v_

    return (loss, grad_x[None], *[grad[k] for k in WEIGHTS], *[delta[k] for k in WEIGHTS],
            *[new_m[k] for k in WEIGHTS], *[new_v[k] for k in WEIGHTS])
```

```python
import functools
import math

import jax
import jax.numpy as jnp
from jax import lax
from jax.experimental import pallas as pl
from jax.experimental.pallas import tpu as pltpu

F32 = jnp.float32
BF16 = jnp.bfloat16
MESH = pl.DeviceIdType.MESH

EPS = 1e-6
CHUNK = 64
SGU_BLOCK = 128
GROUP_DIM = 128
X_HEADS = 4
N_CHIPS = 4
N_DEV = 8
LANE = 128
SUBLANE = 8
BF16_ROWS = 16

ADAM_LR = 0.001
ADAM_B1 = 0.9
ADAM_B2 = 0.999
ADAM_EPS = 1e-08
ADAM_WD = 0.01
ADAM_STEP = 10

V7X_VMEM_BYTES = 64 << 20
VMEM_LIMIT = V7X_VMEM_BYTES - (8 << 20)


def _params(n_grid):
    return pltpu.CompilerParams(dimension_semantics=("arbitrary",) * n_grid,
                                vmem_limit_bytes=VMEM_LIMIT)


def _pick(pref, dims, unit=None):
    g = functools.reduce(math.gcd, dims)
    if unit is None:
        unit = LANE if g % LANE == 0 else SUBLANE
    cands = [d for d in range(unit, g + 1, unit) if g % d == 0] or [g]
    return min(cands, key=lambda d: abs(math.log(d / pref)))


def _any_spec():
    return pl.BlockSpec(memory_space=pl.ANY)


class Mat:
    def __init__(self, arr, kind="c"):
        if arr.ndim == 2:
            arr = arr[None]
        self.arr, self.kind = arr, kind
        self.P, self.prow, self.pcol = arr.shape
        self.rows = self.prow * (self.P if kind == "r" else 1)
        self.cols = self.pcol * (self.P if kind == "c" else 1)
        self.dtype = arr.dtype

    def spec(self, tr, tc, rc_fn):
        if self.kind == "c":
            per = self.pcol // tc
            assert per * tc == self.pcol, (self.pcol, tc)

            def imap(*g):
                i, j = rc_fn(*g)
                return (j // per, i, j % per)
        else:
            per = self.prow // tr
            assert per * tr == self.prow, (self.prow, tr)

            def imap(*g):
                i, j = rc_fn(*g)
                return (i // per, i % per, j)
        return pl.BlockSpec((None, tr, tc), imap)

    def two_d(self):
        assert self.P == 1
        return self.arr[0]


def _out_mat(kind, P, rows, cols, dtype):
    shape = (P, rows, cols // P) if kind == "c" else (P, rows // P, cols)
    return jax.ShapeDtypeStruct(shape, dtype)


def _matmul(name, A, B, mode, outs, *, tm=1024, tn=1024, tk=2048, extras=(), epi=None):
    if mode == "nn":
        M, K, N = A.rows, A.cols, B.cols
        assert B.rows == K
    elif mode == "nt":
        M, K, N = A.rows, A.cols, B.rows
        assert B.cols == K
    else:
        K, M, N = A.rows, A.cols, B.cols
        assert B.rows == K
    mdims, ndims, kdims = [M], [N], [K]
    whole_b = mode == "nn" and B.kind == "r" and B.P > 1 and K <= tk
    if whole_b:
        kdims.append(A.pcol)
        ndims.append(B.pcol)
    elif mode == "tn":
        assert A.kind == "c" and B.kind == "c"
        mdims.append(A.pcol)
        ndims.append(B.pcol)
    else:
        (mdims if A.kind == "r" else kdims).append(A.prow if A.kind == "r" else A.pcol)
        if mode == "nn":
            (kdims if B.kind == "r" else ndims).append(B.prow if B.kind == "r" else B.pcol)
        else:
            (ndims if B.kind == "r" else kdims).append(B.prow if B.kind == "r" else B.pcol)
    for o in list(outs) + list(extras):
        if isinstance(o, Mat):
            (mdims if o.kind == "r" else ndims).append(o.prow if o.kind == "r" else o.pcol)
        elif isinstance(o[0], str):
            (mdims if o[0] == "r" else ndims).append((M if o[0] == "r" else N) // o[1])
    tm, tn = _pick(tm, mdims), _pick(tn, ndims)
    tk = K if mode == "tn" else _pick(tk, kdims)
    nk = K // tk
    grid = (M // tm, N // tn, nk)

    if mode == "tn":
        a_spec = A.spec(K, tm, lambda m, n, k: (0, m))
        b_spec = B.spec(K, tn, lambda m, n, k: (0, n))
    else:
        a_spec = A.spec(tm, tk, lambda m, n, k: (m, k))
        if whole_b:
            b_spec = pl.BlockSpec((B.P, B.prow, tn), lambda m, n, k: (0, 0, n))
        elif mode == "nn":
            b_spec = B.spec(tk, tn, lambda m, n, k: (k, n))
        else:
            b_spec = B.spec(tn, tk, lambda m, n, k: (n, k))

    def mn_spec(o):
        if isinstance(o, Mat):
            return o.spec(tm, tn, lambda m, n, k: (m, n))
        if isinstance(o[0], str):
            kind, P = o[0], o[1]
            fake = Mat.__new__(Mat)
            fake.kind, fake.P = kind, P
            fake.prow = M // P if kind == "r" else M
            fake.pcol = N // P if kind == "c" else N
            return Mat.spec(fake, tm, tn, lambda m, n, k: (m, n))
        return o[1](tm, tn)

    out_shapes = tuple(_out_mat(o[0], o[1], M, N, o[2]) if isinstance(o[0], str) else o[0]
                       for o in outs)
    out_specs = tuple(mn_spec(o) for o in outs)
    extra_arrays = tuple(e.arr if isinstance(e, Mat) else e[0] for e in extras)
    extra_specs = tuple(mn_spec(e) for e in extras)
    n_ex, n_out = len(extras), len(outs)
    tt = _pick(256, [tm])
    dims = (((1,), (1 if mode == "nt" else 0,)), ((), ()))

    def body(*refs):
        a_ref, b_ref = refs[:2]
        ex_refs = refs[2:2 + n_ex]
        out_refs = refs[2 + n_ex:2 + n_ex + n_out]
        scratch = refs[2 + n_ex + n_out:]
        if mode == "tn":
            at_ref = scratch[0]

            @pl.when(pl.program_id(1) == 0)
            def _():
                for c0 in range(0, tm, tt):
                    at_ref[c0:c0 + tt, :] = a_ref[:, c0:c0 + tt].astype(F32).T.astype(BF16)

            lhs = at_ref[...]
        else:
            lhs = a_ref[...].astype(BF16)
        rhs = b_ref[...].reshape(K, tn) if whole_b else b_ref[...]
        part = lax.dot_general(lhs, rhs.astype(BF16), dims, preferred_element_type=F32)

        def finish(acc):
            if epi is None:
                out_refs[0][...] = acc.astype(out_refs[0].dtype)
            else:
                epi(acc, ex_refs, out_refs)

        if nk == 1:
            finish(part)
        else:
            acc_ref = scratch[0]
            k = pl.program_id(2)

            @pl.when(k == 0)
            def _():
                acc_ref[...] = part

            @pl.when(k > 0)
            def _():
                acc_ref[...] += part

            @pl.when(k == nk - 1)
            def _():
                finish(acc_ref[...])

    scratch_shapes = []
    if mode == "tn":
        scratch_shapes.append(pltpu.VMEM((tm, K), BF16))
    elif nk > 1:
        scratch_shapes.append(pltpu.VMEM((tm, tn), F32))
    res = pl.pallas_call(
        body, name=name, grid=grid,
        in_specs=[a_spec, b_spec, *extra_specs], out_specs=out_specs, out_shape=out_shapes,
        scratch_shapes=scratch_shapes, compiler_params=_params(3),
    )(A.arr, B.arr, *extra_arrays)
    return res


def _row_tile(T):
    return _pick(256, [T])


def _tie(name, x, deps):
    def body(*refs):
        refs[-1][...] = jnp.zeros_like(refs[-1])

    return pl.pallas_call(
        body, name=name, in_specs=[_any_spec()] * (1 + len(deps)),
        out_specs=(_any_spec(), pl.BlockSpec(memory_space=pltpu.VMEM)),
        out_shape=(jax.ShapeDtypeStruct(x.shape, x.dtype),
                   jax.ShapeDtypeStruct((SUBLANE, LANE), F32)),
        input_output_aliases={0: 0},
    )(x, *deps)[0]


def _rmsnorm_fwd(name, x, g, *, into=None, col=0, deps=()):
    T, W = x.shape
    tr = _row_tile(T)

    def body(x_ref, g_ref, *rest):
        o_ref = rest[-1]
        xv = x_ref[...]
        rstd = lax.rsqrt(jnp.mean(xv * xv, axis=-1, keepdims=True) + EPS)
        o_ref[...] = (xv * rstd * g_ref[...]).astype(o_ref.dtype)

    in_specs = [pl.BlockSpec((tr, W), lambda i: (i, 0)), pl.BlockSpec((1, W), lambda i: (0, 0))]
    args = [x, g]
    kwargs = {}
    if into is None:
        out_shape = jax.ShapeDtypeStruct((T, W), BF16)
    else:
        out_shape = jax.ShapeDtypeStruct(into.shape, into.dtype)
        in_specs.append(_any_spec())
        args.append(into)
        kwargs["input_output_aliases"] = {2: 0}
    in_specs += [_any_spec()] * len(deps)
    args += list(deps)
    return pl.pallas_call(
        body, name=name, grid=(T // tr,), in_specs=in_specs,
        out_specs=pl.BlockSpec((tr, W), lambda i: (i, col)), out_shape=out_shape,
        compiler_params=_params(1), **kwargs)(*args)


def _rmsnorm_bwd(name, x, g, dn, *, dn_col=0, dres=None, want_dx=True, want_bf16=True):
    T, W = x.shape
    tr = _row_tile(T)
    has_res = dres is not None

    def body(*refs):
        x_ref, g_ref, dn_ref = refs[:3]
        pos = 3
        dres_ref = None
        if has_res:
            dres_ref = refs[pos]
            pos += 1
        outs = refs[pos:]
        dg_ref = outs[-1]
        xv = x_ref[...]
        rstd = lax.rsqrt(jnp.mean(xv * xv, axis=-1, keepdims=True) + EPS)
        xhat = xv * rstd
        dnv = dn_ref[...].astype(F32)

        @pl.when(pl.program_id(0) == 0)
        def _():
            dg_ref[...] = jnp.zeros_like(dg_ref)

        dg_ref[...] += jnp.sum(dnv * xhat, axis=0, keepdims=True)
        if want_dx:
            t = dnv * g_ref[...]
            dx = rstd * (t - xhat * jnp.mean(t * xhat, axis=-1, keepdims=True))
            if has_res:
                dx = dx + dres_ref[...]
            outs[0][...] = dx
            if want_bf16:
                outs[1][...] = dx.astype(BF16)

    row = pl.BlockSpec((tr, W), lambda i: (i, 0))
    in_specs = [row, pl.BlockSpec((1, W), lambda i: (0, 0)),
                pl.BlockSpec((tr, W), lambda i: (i, dn_col))]
    args = [x, g, dn]
    if has_res:
        in_specs.append(row)
        args.append(dres)
    out_shape, out_specs = [], []
    if want_dx:
        out_shape.append(jax.ShapeDtypeStruct((T, W), F32))
        out_specs.append(row)
        if want_bf16:
            out_shape.append(jax.ShapeDtypeStruct((T, W), BF16))
            out_specs.append(row)
    out_shape.append(jax.ShapeDtypeStruct((1, W), F32))
    out_specs.append(pl.BlockSpec((1, W), lambda i: (0, 0)))
    return pl.pallas_call(
        body, name=name, grid=(T // tr,), in_specs=in_specs, out_specs=out_specs,
        out_shape=out_shape, compiler_params=_params(1))(*args)


def _loss_head(name, h, g, target):
    T, W = h.shape
    tr = _row_tile(T)

    def body(h_ref, g_ref, t_ref, loss_ref, dx_ref, dxb_ref, dg_ref):
        xv = h_ref[...]
        gv = g_ref[...]
        rstd = lax.rsqrt(jnp.mean(xv * xv, axis=-1, keepdims=True) + EPS)
        xhat = xv * rstd
        diff = xhat * gv - t_ref[...]

        @pl.when(pl.program_id(0) == 0)
        def _():
            dg_ref[...] = jnp.zeros_like(dg_ref)
            loss_ref[...] = jnp.zeros_like(loss_ref)

        loss_ref[...] += 0.5 * jnp.sum(jnp.mean(diff * diff, axis=-1, keepdims=True))
        dnv = diff * (1.0 / W)
        dg_ref[...] += jnp.sum(dnv * xhat, axis=0, keepdims=True)
        t = dnv * gv
        dx = rstd * (t - xhat * jnp.mean(t * xhat, axis=-1, keepdims=True))
        dx_ref[...] = dx
        dxb_ref[...] = dx.astype(BF16)

    row = pl.BlockSpec((tr, W), lambda i: (i, 0))
    vec = pl.BlockSpec((1, W), lambda i: (0, 0))
    return pl.pallas_call(
        body, name=name, grid=(T // tr,), in_specs=[row, vec, row],
        out_specs=[pl.BlockSpec((SUBLANE, LANE), lambda i: (0, 0)), row, row, vec],
        out_shape=[jax.ShapeDtypeStruct((SUBLANE, LANE), F32), jax.ShapeDtypeStruct((T, W), F32),
                   jax.ShapeDtypeStruct((T, W), BF16), jax.ShapeDtypeStruct((1, W), F32)],
        compiler_params=_params(1))(h, g, target)


def _sigmoid(x):
    return 1.0 / (1.0 + jnp.exp(-x))


def _ffn_in(name, n, W, place, half, prev=None):
    T, D = n.shape
    F = W.cols // 2
    tm = _pick(2048, [T])
    tn = _pick(512, [W.pcol])
    per = W.pcol // tn

    def body(place_ref, a_ref, wg_ref, wu_ref, *rest):
        gu_ref, act_ref = rest[-2:]
        a = a_ref[...]
        gate = jnp.dot(a, wg_ref[...], preferred_element_type=F32)
        up = jnp.dot(a, wu_ref[...], preferred_element_type=F32)
        gu_ref[0] = gate.astype(BF16)
        gu_ref[1] = up.astype(BF16)
        act_ref[...] = (gate * _sigmoid(gate) * up).astype(BF16)

    def pair(pr):
        return (pr[0] + half) % 2

    in_specs = [pl.BlockSpec((tm, D), lambda m, j, pr: (m, 0)),
                pl.BlockSpec((None, D, tn), lambda m, j, pr: (pair(pr), 0, j)),
                pl.BlockSpec((None, D, tn), lambda m, j, pr: (2 + pair(pr), 0, j))]
    args = [place, n, W.arr, W.arr]
    kwargs = {}
    if prev is not None:
        in_specs += [_any_spec(), _any_spec()]
        args += list(prev)
        kwargs["input_output_aliases"] = {4: 0, 5: 1}
    grid_spec = pltpu.PrefetchScalarGridSpec(
        num_scalar_prefetch=1, grid=(T // tm, per), in_specs=in_specs,
        out_specs=[pl.BlockSpec((2, tm, tn), lambda m, j, pr: (0, m, pair(pr) * per + j)),
                   pl.BlockSpec((tm, tn), lambda m, j, pr: (m, pair(pr) * per + j))])
    return pl.pallas_call(
        body, name=name, grid_spec=grid_spec,
        out_shape=[jax.ShapeDtypeStruct((2, T, F), BF16), jax.ShapeDtypeStruct((T, F), BF16)],
        compiler_params=_params(2), **kwargs)(*args)


def _ffn_forward(tag, h, norm_g, weights_of, place, deps=()):
    n = _rmsnorm_fwd(f"{tag}_norm", h, norm_g, deps=deps)
    got = weights_of(f"{tag}_in", n)
    gu, act = _ffn_in(f"{tag}_in_a", n, got[f"{tag}_w_in"], place, 0)
    w_in = got["finish"](act)[f"{tag}_w_in"]
    gu, act = _ffn_in(f"{tag}_in_b", n, w_in, place, 1, (gu, act))
    w_out = weights_of(f"{tag}_out", act)[f"{tag}_w_out"]

    def epi(acc, ex, out):
        out[0][...] = ex[0][...] + 0.5 * acc

    (h_out,) = _matmul(f"{tag}_out", Mat(act), w_out, "nn", [("c", 1, F32)],
                       tm=1024, tn=512, tk=8192, extras=[Mat(h)], epi=epi)
    return h_out[0], (n, gu, act, w_in, w_out)


def _ffn_backward(tag, h_in, norm_g, saved, dh, dh_bf, grads_ready, grads_flush,
                  early_out=False):
    n, gu, act, w_in, w_out = saved
    T, F = act.shape

    def epi(acc, ex, out):
        dact = 0.5 * acc
        gate = ex[0][0].astype(F32)
        up = ex[0][1].astype(F32)
        sig = _sigmoid(gate)
        out[0][0] = (dact * up * sig * (1.0 + gate * (1.0 - sig))).astype(BF16)
        out[0][1] = (dact * gate * sig).astype(BF16)

    def pair_spec(tm, tn):
        return pl.BlockSpec((2, tm, tn), lambda m, j, k: (0, m, j))

    def half(acc, ex, out):
        out[0][...] = (0.5 * acc).astype(out[0].dtype)

    (dw_out,) = _matmul(f"{tag}_dwout", Mat(act), Mat(dh_bf), "tn", [("r", N_CHIPS, BF16)],
                        tm=1408, tn=512, epi=half)
    if early_out:
        token = grads_ready(f"{tag}_out", {f"{tag}_w_out": dw_out})
        dh_bf = _tie(f"{tag}_dh_after_swap", dh_bf, [token])
    (dgu,) = _matmul(f"{tag}_dact", Mat(dh_bf), w_out, "nt",
                     [(jax.ShapeDtypeStruct((2, T, F), BF16), pair_spec)],
                     tm=512, tn=1408, extras=[(gu, pair_spec)], epi=epi)
    if early_out:
        dgu = _tie(f"{tag}_dgu_after_scatter", dgu, [grads_flush(f"{tag}_out", dgu)])
    (dw_in,) = _matmul(f"{tag}_dwin", Mat(n), Mat(dgu), "tn", [("c", N_CHIPS, BF16)],
                       tm=1024, tn=1408)
    if early_out:
        group, partial = f"{tag}_in", {f"{tag}_w_in": dw_in}
    else:
        group, partial = tag, {f"{tag}_w_in": dw_in, f"{tag}_w_out": dw_out}
    dgu = _tie(f"{tag}_dgu_after_swap", dgu, [grads_ready(group, partial)])
    (dn,) = _matmul(f"{tag}_dn", Mat(dgu), w_in, "nt", [("c", 1, F32)],
                    tm=1024, tn=1024, tk=2816)
    dn = _tie(f"{tag}_dn_after_scatter", dn, [grads_flush(group, dn)])
    return _rmsnorm_bwd(f"{tag}_dnorm", h_in, norm_g, dn[0], dres=dh)


_GELU_C = math.sqrt(2.0 / math.pi)
_GELU_A = 0.044715


def _gelu(x):
    return 0.5 * x * (1.0 + jnp.tanh(_GELU_C * (x + _GELU_A * x * x * x)))


def _gelu_grad(x):
    th = jnp.tanh(_GELU_C * (x + _GELU_A * x * x * x))
    return 0.5 * (1.0 + th) + 0.5 * x * (1.0 - th * th) * _GELU_C * (1.0 + 3.0 * _GELU_A * x * x)


def _chunk_mask():
    t = lax.broadcasted_iota(jnp.int32, (SGU_BLOCK, SGU_BLOCK), 0) // CHUNK
    s = lax.broadcasted_iota(jnp.int32, (SGU_BLOCK, SGU_BLOCK), 1) // CHUNK
    return s <= t


def _sgu_group_forward(v_g, lg, lb, wm_bf, b_col):
    mu = jnp.mean(v_g, axis=-1, keepdims=True)
    xc = v_g - mu
    rstd = lax.rsqrt(jnp.mean(xc * xc, axis=-1, keepdims=True) + EPS)
    vhat = xc * rstd
    vn = vhat * lg + lb
    mixed = jnp.dot(wm_bf, vn.astype(BF16), preferred_element_type=F32) + b_col
    return vhat, rstd, vn, mixed


def _sgu_forward(name, z, ln_g, ln_b, w_s, b_t, gn, d_model):
    T = z.shape[0]
    W_A = ln_g.shape[1]
    G = W_A // GROUP_DIM

    def body(z_ref, lg_ref, lb_ref, w_ref, bt_ref, gn_ref, y_ref):
        mask = _chunk_mask()
        u = _gelu(z_ref[:, :W_A])
        v = _gelu(z_ref[:, W_A:])
        cols = []
        for g in range(G):
            sl = slice(g * GROUP_DIM, (g + 1) * GROUP_DIM)
            wm = jnp.where(mask, w_ref[g], 0.0).astype(BF16)
            _, _, _, mixed = _sgu_group_forward(v[:, sl], lg_ref[:, sl], lb_ref[:, sl], wm,
                                                bt_ref[:, g:g + 1])
            cols.append(u[:, sl] * mixed)
        ya = jnp.concatenate(cols, axis=1)
        rstd = lax.rsqrt(jnp.mean(ya * ya, axis=-1, keepdims=True) + EPS)
        y_ref[...] = (ya * rstd * gn_ref[...]).astype(BF16)

    vec = pl.BlockSpec((1, W_A), lambda i: (0, 0))
    return pl.pallas_call(
        body, name=name, grid=(T // SGU_BLOCK,),
        in_specs=[pl.BlockSpec((SGU_BLOCK, 2 * W_A), lambda i: (i, 0)), vec, vec,
                  pl.BlockSpec((G, SGU_BLOCK, SGU_BLOCK), lambda i: (0, 0, 0)),
                  pl.BlockSpec((SGU_BLOCK, G), lambda i: (0, 0)), vec],
        out_specs=pl.BlockSpec((SGU_BLOCK, W_A), lambda i: (i, 0)),
        out_shape=jax.ShapeDtypeStruct((T, d_model), BF16),
        compiler_params=_params(1))(z, ln_g, ln_b, w_s, b_t, gn)


def _sgu_backward(name, z, dy, ln_g, ln_b, w_s, b_t, gn):
    T = z.shape[0]
    W_A = ln_g.shape[1]
    G = W_A // GROUP_DIM

    def body(z_ref, dy_ref, lg_ref, lb_ref, w_ref, bt_ref, gn_ref,
             dz_ref, dlg_ref, dlb_ref, dw_ref, db_ref, dgn_ref):
        @pl.when(pl.program_id(0) == 0)
        def _():
            for r in (dlg_ref, dlb_ref, dw_ref, db_ref, dgn_ref):
                r[...] = jnp.zeros_like(r)

        mask = _chunk_mask()
        zu = z_ref[:, :W_A]
        zv = z_ref[:, W_A:]
        u = _gelu(zu)
        v = _gelu(zv)
        saved, cols = [], []
        for g in range(G):
            sl = slice(g * GROUP_DIM, (g + 1) * GROUP_DIM)
            wm = jnp.where(mask, w_ref[g], 0.0)
            vhat, rstd, vn, mixed = _sgu_group_forward(
                v[:, sl], lg_ref[:, sl], lb_ref[:, sl], wm.astype(BF16), bt_ref[:, g:g + 1])
            saved.append((wm, vhat, rstd, vn, mixed))
            cols.append(u[:, sl] * mixed)
        ya = jnp.concatenate(cols, axis=1)
        rstd_a = lax.rsqrt(jnp.mean(ya * ya, axis=-1, keepdims=True) + EPS)
        ya_hat = ya * rstd_a
        dyv = dy_ref[...].astype(F32)
        dgn_ref[...] += jnp.sum(dyv * ya_hat, axis=0, keepdims=True)
        t = dyv * gn_ref[...]
        dya = rstd_a * (t - ya_hat * jnp.mean(t * ya_hat, axis=-1, keepdims=True))
        du_cols, dv_cols, dlg_cols, dlb_cols = [], [], [], []
        for g in range(G):
            sl = slice(g * GROUP_DIM, (g + 1) * GROUP_DIM)
            wm, vhat, rstd, vn, mixed = saved[g]
            dya_g = dya[:, sl]
            du_cols.append(dya_g * mixed)
            dmix = dya_g * u[:, sl]
            dmix_bf = dmix.astype(BF16)
            db_ref[g] += jnp.sum(dmix, axis=1, keepdims=True)
            dw = lax.dot_general(dmix_bf, vn.astype(BF16), (((1,), (1,)), ((), ())),
                                 preferred_element_type=F32)
            dw_ref[g] += jnp.where(mask, dw, 0.0)
            dvn = jnp.dot(wm.T.astype(BF16), dmix_bf, preferred_element_type=F32)
            dlg_cols.append(jnp.sum(dvn * vhat, axis=0, keepdims=True))
            dlb_cols.append(jnp.sum(dvn, axis=0, keepdims=True))
            dvhat = dvn * lg_ref[:, sl]
            dv_cols.append(rstd * (dvhat - jnp.mean(dvhat, axis=-1, keepdims=True)
                                   - vhat * jnp.mean(dvhat * vhat, axis=-1, keepdims=True)))
        dlg_ref[...] += jnp.concatenate(dlg_cols, axis=1)
        dlb_ref[...] += jnp.concatenate(dlb_cols, axis=1)
        dz_ref[:, :W_A] = (jnp.concatenate(du_cols, axis=1) * _gelu_grad(zu)).astype(BF16)
        dz_ref[:, W_A:] = (jnp.concatenate(dv_cols, axis=1) * _gelu_grad(zv)).astype(BF16)

    vec = pl.BlockSpec((1, W_A), lambda i: (0, 0))
    wspec = pl.BlockSpec((G, SGU_BLOCK, SGU_BLOCK), lambda i: (0, 0, 0))
    return pl.pallas_call(
        body, name=name, grid=(T // SGU_BLOCK,),
        in_specs=[pl.BlockSpec((SGU_BLOCK, 2 * W_A), lambda i: (i, 0)),
                  pl.BlockSpec((SGU_BLOCK, W_A), lambda i: (i, 0)), vec, vec, wspec,
                  pl.BlockSpec((SGU_BLOCK, G), lambda i: (0, 0)), vec],
        out_specs=[pl.BlockSpec((SGU_BLOCK, 2 * W_A), lambda i: (i, 0)), vec, vec, wspec,
                   pl.BlockSpec((G, SGU_BLOCK, 1), lambda i: (0, 0, 0)), vec],
        out_shape=[jax.ShapeDtypeStruct((T, 2 * W_A), BF16), jax.ShapeDtypeStruct((1, W_A), F32),
                   jax.ShapeDtypeStruct((1, W_A), F32),
                   jax.ShapeDtypeStruct((G, SGU_BLOCK, SGU_BLOCK), F32),
                   jax.ShapeDtypeStruct((G, SGU_BLOCK, 1), F32),
                   jax.ShapeDtypeStruct((1, W_A), F32)],
        compiler_params=_params(1))(z, dy, ln_g, ln_b, w_s, b_t, gn)


def _split_dot(x, tri):
    hi = x.astype(BF16)
    lo = (x - hi.astype(F32)).astype(BF16)
    return (jnp.dot(hi, tri, preferred_element_type=F32)
            + jnp.dot(lo, tri, preferred_element_type=F32))


def _tri(n, rel):
    r = lax.broadcasted_iota(jnp.int32, (n, n), 0)
    c = lax.broadcasted_iota(jnp.int32, (n, n), 1)
    return rel(r, c).astype(BF16)


def _dot_nt(a, b):
    return lax.dot_general(a, b, (((1,), (1,)), ((), ())), preferred_element_type=F32)


def _dot_tn(a, b):
    return lax.dot_general(a, b, (((0,), (0,)), ((), ())), preferred_element_type=F32)


def _sb_scores(qs, kj, mask):
    zz = _dot_nt(qs, kj)
    log_beta = jnp.minimum(zz, 0.0) - jnp.log(1.0 + jnp.exp(-jnp.abs(zz)))
    log_1m = log_beta - zz
    if mask is not None:
        log_1m = jnp.where(mask, log_1m, 0.0)
    return log_beta, log_1m


def _masked(mask, x):
    return x if mask is None else jnp.where(mask, x, 0.0)


def _below(old, new, row0):
    if row0 == 0:
        return tuple(new)
    return tuple(jnp.concatenate([o[:row0], n], axis=0) for o, n in zip(old, new))


def _sb_tiles(T):
    tk = _pick(256, [T])
    tq = 2 * tk if T % (2 * tk) == 0 else tk
    return tq, tk


def _sb_cols(w_a, w_b):
    base = 2 * w_a // GROUP_DIM
    per = w_b // GROUP_DIM
    return base, base + per, base + 2 * per


def _sb_forward(name, z, w_a, w_b):
    T = z.shape[0]
    H = w_b // GROUP_DIM
    tq, tk = _sb_tiles(T)
    per = tq // tk
    qc, kc, vc = _sb_cols(w_a, w_b)
    scale = GROUP_DIM ** -0.5

    def body(q_ref, k_ref, v_ref, y_ref, tot_ref):
        i = pl.program_id(1)
        qs = (q_ref[...] * scale).astype(BF16)
        upper = _tri(tk, lambda r, c: r > c)
        ahead = (lax.broadcasted_iota(jnp.int32, (tq, tk), 1)
                 - lax.broadcasted_iota(jnp.int32, (tq, tk), 0))

        def step(j, carry, masked, row0=0):
            acc, later = (c[row0:] for c in carry)
            k0 = pl.multiple_of(j * tk, tk)
            kj = k_ref[pl.ds(k0, tk), :].astype(BF16)
            vj = v_ref[pl.ds(k0, tk), :].astype(BF16)
            mask = ahead[row0:] < i * tq - k0 if masked else None
            log_beta, log_1m = _sb_scores(qs[row0:], kj, mask)
            rest = _split_dot(log_1m, upper) + later
            a = _masked(mask, jnp.exp(log_beta + rest))
            acc = acc + jnp.dot(a.astype(BF16), vj, preferred_element_type=F32)
            later = later + jnp.sum(log_1m, axis=1, keepdims=True)
            return _below(carry, (acc, later), row0)

        def blocks(p, c):
            for d in reversed(range(per)):
                c = step(p * per + d, c, False)
            return c

        carry = (jnp.zeros((tq, GROUP_DIM), F32), jnp.zeros((tq, 1), F32))
        for d in reversed(range(per)):
            carry = step(i * per + d, carry, True, d * tk)
        acc, total = lax.fori_loop(0, i, lambda pp, c: blocks(i - 1 - pp, c), carry)
        y_ref[...] = acc
        tot_ref[...] = total

    return pl.pallas_call(
        body, name=name, grid=(H, T // tq),
        in_specs=[pl.BlockSpec((tq, GROUP_DIM), lambda h, i: (i, qc + h)),
                  pl.BlockSpec((T, GROUP_DIM), lambda h, i: (0, kc + h)),
                  pl.BlockSpec((T, GROUP_DIM), lambda h, i: (0, vc + h))],
        out_specs=[pl.BlockSpec((tq, GROUP_DIM), lambda h, i: (i, h)),
                   pl.BlockSpec((None, tq, 1), lambda h, i: (h, i, 0))],
        out_shape=[jax.ShapeDtypeStruct((T, w_b), F32), jax.ShapeDtypeStruct((H, T, 1), F32)],
        compiler_params=_params(2))(z, z, z)


def _sb_backward(name, z, do, total, w_a, w_b):
    T = z.shape[0]
    H = w_b // GROUP_DIM
    tq, tk = _sb_tiles(T)
    per = tq // tk
    qc, kc, vc = _sb_cols(w_a, w_b)
    scale = GROUP_DIM ** -0.5

    def body(q_ref, k_ref, v_ref, do_ref, tot_ref, dq_ref, dkv_ref):
        i = pl.program_id(1)

        @pl.when(i == 0)
        def _():
            dkv_ref[...] = jnp.zeros_like(dkv_ref)

        qs = (q_ref[...] * scale).astype(BF16)
        dob = do_ref[...].astype(BF16)
        upto = _tri(tk, lambda r, c: r <= c)
        before = _tri(tk, lambda r, c: r < c)
        ahead = (lax.broadcasted_iota(jnp.int32, (tq, tk), 1)
                 - lax.broadcasted_iota(jnp.int32, (tq, tk), 0))

        def step(j, carry, masked, row0=0):
            dq, left, e_seen = (c[row0:] for c in carry)
            qr, dor = qs[row0:], dob[row0:]
            k0 = pl.multiple_of(j * tk, tk)
            kj = k_ref[pl.ds(k0, tk), :].astype(BF16)
            vj = v_ref[pl.ds(k0, tk), :].astype(BF16)
            mask = ahead[row0:] < i * tq - k0 if masked else None
            log_beta, log_1m = _sb_scores(qr, kj, mask)
            rest = left - _split_dot(log_1m, upto)
            a = _masked(mask, jnp.exp(log_beta + rest))
            e = a * _dot_nt(dor, vj)
            e_before = e_seen + jnp.dot(e.astype(BF16), before, preferred_element_type=F32)
            beta = jnp.exp(log_beta)
            dz = _masked(mask, e * (1.0 - beta) - beta * e_before).astype(BF16)
            dq = dq + jnp.dot(dz, kj, preferred_element_type=F32)
            dkv_ref[0, pl.ds(k0, tk), :] += _dot_tn(dz, qr)
            dkv_ref[1, pl.ds(k0, tk), :] += _dot_tn(a.astype(BF16), dor)
            left = left - jnp.sum(log_1m, axis=1, keepdims=True)
            e_seen = e_seen + jnp.sum(e, axis=1, keepdims=True)
            return _below(carry, (dq, left, e_seen), row0)

        def blocks(p, c):
            for d in range(per):
                c = step(p * per + d, c, False)
            return c

        carry = (jnp.zeros((tq, GROUP_DIM), F32), tot_ref[...], jnp.zeros((tq, 1), F32))
        carry = lax.fori_loop(0, i, blocks, carry)
        for d in range(per):
            carry = step(i * per + d, carry, True, d * tk)
        dq_ref[...] = (carry[0] * scale).astype(BF16)

    return pl.pallas_call(
        body, name=name, grid=(H, T // tq),
        in_specs=[pl.BlockSpec((tq, GROUP_DIM), lambda h, i: (i, qc + h)),
                  pl.BlockSpec((T, GROUP_DIM), lambda h, i: (0, kc + h)),
                  pl.BlockSpec((T, GROUP_DIM), lambda h, i: (0, vc + h)),
                  pl.BlockSpec((tq, GROUP_DIM), lambda h, i: (i, h)),
                  pl.BlockSpec((None, tq, 1), lambda h, i: (h, i, 0))],
        out_specs=[pl.BlockSpec((tq, GROUP_DIM), lambda h, i: (i, h)),
                   pl.BlockSpec((2, T, GROUP_DIM), lambda h, i: (0, 0, h))],
        out_shape=[jax.ShapeDtypeStruct((T, w_b), BF16), jax.ShapeDtypeStruct((2, T, w_b), F32)],
        compiler_params=_params(2))(z, z, z, do, total)


def _softmax_rows(s):
    m = jnp.max(s, axis=-1, keepdims=True)
    p = jnp.exp(s - m)
    return p / jnp.sum(p, axis=-1, keepdims=True)


def _xattn_forward(name, q, kv):
    T, D = q.shape
    Nm = kv.shape[0]
    dh = D // X_HEADS
    tq = _pick(512, [T])

    def body(q_ref, k_ref, v_ref, o_ref):
        p = _softmax_rows(_dot_nt(q_ref[...], k_ref[...]))
        o_ref[...] = jnp.dot(p.astype(BF16), v_ref[...], preferred_element_type=F32).astype(BF16)

    return pl.pallas_call(
        body, name=name, grid=(T // tq, X_HEADS),
        in_specs=[pl.BlockSpec((tq, dh), lambda i, h: (i, h)),
                  pl.BlockSpec((Nm, dh), lambda i, h: (0, h)),
                  pl.BlockSpec((Nm, dh), lambda i, h: (0, X_HEADS + h))],
        out_specs=pl.BlockSpec((tq, dh), lambda i, h: (i, h)),
        out_shape=jax.ShapeDtypeStruct((T, D), BF16),
        compiler_params=_params(2))(q, kv, kv)


def _xattn_backward(name, q, kv, do):
    T, D = q.shape
    Nm = kv.shape[0]
    dh = D // X_HEADS
    tq = _pick(512, [T])
    scale = dh ** -0.5

    def body(q_ref, k_ref, v_ref, do_ref, dq_ref, dkv_ref):
        @pl.when(pl.program_id(1) == 0)
        def _():
            dkv_ref[...] = jnp.zeros_like(dkv_ref)

        qv, kk, vv, dov = q_ref[...], k_ref[...], v_ref[...], do_ref[...]
        p = _softmax_rows(_dot_nt(qv, kk))
        dp = _dot_nt(dov, vv)
        ds = (p * (dp - jnp.sum(dp * p, axis=-1, keepdims=True))).astype(BF16)
        dq_ref[...] = (jnp.dot(ds, kk, preferred_element_type=F32) * scale).astype(BF16)
        dkv_ref[0] += _dot_tn(ds, qv)
        dkv_ref[1] += _dot_tn(p.astype(BF16), dov)

    blk = pl.BlockSpec((tq, dh), lambda h, i: (i, h))
    return pl.pallas_call(
        body, name=name, grid=(X_HEADS, T // tq),
        in_specs=[blk, pl.BlockSpec((Nm, dh), lambda h, i: (0, h)),
                  pl.BlockSpec((Nm, dh), lambda h, i: (0, X_HEADS + h)), blk],
        out_specs=[blk, pl.BlockSpec((2, Nm, dh), lambda h, i: (0, 0, h))],
        out_shape=[jax.ShapeDtypeStruct((T, D), BF16), jax.ShapeDtypeStruct((2, Nm, D), F32)],
        compiler_params=_params(2))(q, kv, kv, do)


def _position():
    x, y, c = lax.axis_index("x"), lax.axis_index("y"), lax.axis_index("c")
    other_chips = [(1 - x, y), (x, 1 - y), (1 - x, 1 - y)]
    return x, y, c, other_chips


def _hbm_spec():
    return pl.BlockSpec(memory_space=pltpu.HBM)


def _sem_spec():
    return pl.BlockSpec(memory_space=pltpu.SEMAPHORE)


def _split_start(name, arrays, make_copies, n_sems, deps=()):
    n, d = len(arrays), len(deps)

    def body(*refs):
        ins = refs[:n]
        send_sems, recv_sems = refs[n + d], refs[n + d + 1]
        token = refs[-1]
        for cp in make_copies(ins, send_sems, recv_sems):
            cp.start()
        token[...] = jnp.zeros_like(token)

    res = pl.pallas_call(
        body, name=name,
        out_shape=(pltpu.SemaphoreType.DMA((n_sems,)), pltpu.SemaphoreType.DMA((n_sems,)),
                   *[pltpu.HBM(a.shape, a.dtype) for a in arrays],
                   jax.ShapeDtypeStruct((SUBLANE, LANE), F32)),
        in_specs=[_hbm_spec()] * n + [_any_spec()] * d,
        out_specs=(_sem_spec(), _sem_spec(), *[_hbm_spec()] * n,
                   pl.BlockSpec(memory_space=pltpu.VMEM)),
        input_output_aliases={i: 2 + i for i in range(n)},
        compiler_params=pltpu.CompilerParams(
            has_side_effects=pltpu.SideEffectType.DATAFLOW_SIDE_EFFECTING),
    )(*[pltpu.with_memory_space_constraint(a, pltpu.HBM) for a in arrays], *deps)
    return res[0], res[1], list(res[2:2 + n]), res[-1]


def _split_wait(name, arrays, send_sems, recv_sems, after, make_copies):
    n = len(arrays)
    after = list(after) if isinstance(after, (list, tuple)) else [after]

    def body(*refs):
        ins = refs[:n]
        send_ref, recv_ref = refs[n], refs[n + 1]
        for cp in make_copies(ins, send_ref, recv_ref):
            cp.wait_send()
            cp.wait_recv()

    return pl.pallas_call(
        body, name=name,
        out_shape=tuple(pltpu.HBM(a.shape, a.dtype) for a in arrays),
        in_specs=[_hbm_spec()] * n + [_sem_spec(), _sem_spec()] + [_any_spec()] * len(after),
        out_specs=tuple(_hbm_spec() for _ in arrays),
        input_output_aliases={i: i for i in range(n)},
        compiler_params=pltpu.CompilerParams(
            has_side_effects=pltpu.SideEffectType.DATAFLOW_SIDE_EFFECTING),
    )(*arrays, send_sems, recv_sems, *after)


def _gather_copies(refs, send_sems, recv_sems):
    x, y, c, chips = _position()
    me = 2 * x + y
    copies = []
    for i, ref in enumerate(refs):
        rows = ref.shape[1] // 2
        piece = ref.at[me, pl.ds(c * rows, rows), :]
        for j, (px, py) in enumerate(chips):
            copies.append(pltpu.make_async_remote_copy(
                src_ref=piece, dst_ref=piece, send_sem=send_sems.at[3 * i + j],
                recv_sem=recv_sems.at[3 * i + j], device_id=(px, py, c), device_id_type=MESH))
    return copies


def _near_copies(refs, send_sems, recv_sems):
    x, y, c, chips = _position()
    me = 2 * x + y
    copies = []
    for i, ref in enumerate(refs):
        rows = ref.shape[1] // 2
        piece = ref.at[me, pl.ds(c * rows, rows), :]
        for j, (px, py) in enumerate(chips[:2]):
            copies.append(pltpu.make_async_remote_copy(
                src_ref=piece, dst_ref=piece, send_sem=send_sems.at[2 * i + j],
                recv_sem=recv_sems.at[2 * i + j], device_id=(px, py, c), device_id_type=MESH))
    return copies


def _relay_copies(refs, send_sems, recv_sems):
    x, y, c, chips = _position()
    copies = []
    for i, ref in enumerate(refs):
        rows = ref.shape[1] // 4
        for j, (px, py) in enumerate(chips[:2]):
            ox, oy = chips[1 - j]
            piece = ref.at[2 * ox + oy, pl.ds((2 * c + j) * rows, rows), :]
            copies.append(pltpu.make_async_remote_copy(
                src_ref=piece, dst_ref=piece, send_sem=send_sems.at[2 * i + j],
                recv_sem=recv_sems.at[2 * i + j], device_id=(px, py, c), device_id_type=MESH))
    return copies


def _share_copies(refs, send_sems, recv_sems):
    x, y, c, _ = _position()
    copies = []
    for i, ref in enumerate(refs):
        rows = ref.shape[0] // 2
        mine = ref.at[pl.ds(c * rows, rows), :]
        copies.append(pltpu.make_async_remote_copy(
            src_ref=mine, dst_ref=mine, send_sem=send_sems.at[i], recv_sem=recv_sems.at[i],
            device_id=(x, y, 1 - c), device_id_type=MESH))
    return copies


def _scatter_copies(refs, send_sems, recv_sems):
    x, y, c, chips = _position()
    n = len(refs) // 2
    copies = []
    for i in range(n):
        for j, (px, py) in enumerate(chips):
            copies.append(pltpu.make_async_remote_copy(
                src_ref=refs[i].at[2 * px + py], dst_ref=refs[n + i].at[j],
                send_sem=send_sems.at[3 * i + j], recv_sem=recv_sems.at[3 * i + j],
                device_id=(px, py, c), device_id_type=MESH))
    return copies


def _cast_own(name, place, shard):
    rows, cols = shard.shape
    tr = _block_rows(rows, cols)

    def body(place_ref, w_ref, o_ref):
        o_ref[...] = w_ref[...].astype(BF16)

    grid_spec = pltpu.PrefetchScalarGridSpec(
        num_scalar_prefetch=1, grid=(rows // tr,),
        in_specs=[pl.BlockSpec((tr, cols), lambda r, pr: (r, 0))],
        out_specs=pl.BlockSpec((None, tr, cols), lambda r, pr: (pr[0], r, 0)))
    return pl.pallas_call(
        body, name=name, grid_spec=grid_spec,
        out_shape=jax.ShapeDtypeStruct((N_CHIPS, rows, cols), BF16),
        compiler_params=_params(1))(place, shard)


def _forward_to_sibling(name, arrays, deps=(), which=(0, 1, 2)):
    n = len(arrays)

    def body(*refs):
        ins = refs[:n]
        send_sems, recv_sems = refs[-2:]
        x, y, c, chips = _position()
        chips = [(j, chips[j]) for j in which]
        sends = []
        for i in range(n):
            rows = ins[i].shape[1] // 2
            for j, (px, py) in chips:
                piece = ins[i].at[2 * px + py, pl.ds(c * rows, rows), :]
                cp = pltpu.make_async_remote_copy(
                    src_ref=piece, dst_ref=piece, send_sem=send_sems.at[i, j],
                    recv_sem=recv_sems.at[i, j], device_id=(x, y, 1 - c), device_id_type=MESH)
                cp.start()
                sends.append(cp)
        for i in range(n):
            rows = ins[i].shape[1] // 2
            for j, (px, py) in chips:
                piece = ins[i].at[2 * px + py, pl.ds((1 - c) * rows, rows), :]
                pltpu.make_async_remote_copy(
                    src_ref=piece, dst_ref=piece, send_sem=send_sems.at[i, j],
                    recv_sem=recv_sems.at[i, j], device_id=(x, y, 1 - c),
                    device_id_type=MESH).wait_recv()
        for cp in sends:
            cp.wait_send()

    return pl.pallas_call(
        body, name=name,
        in_specs=[_any_spec()] * (n + len(deps)), out_specs=[_any_spec()] * n,
        out_shape=[jax.ShapeDtypeStruct(a.shape, a.dtype) for a in arrays],
        input_output_aliases={i: i for i in range(n)},
        scratch_shapes=[pltpu.SemaphoreType.DMA((n, 3))] * 2,
    )(*arrays, *deps)


def _swap_copies(refs, send_sems, recv_sems):
    x, y, c, _ = _position()
    n = len(refs) // 2
    copies = []
    for i in range(n):
        rows = refs[i].shape[1] // 2
        copies.append(pltpu.make_async_remote_copy(
            src_ref=refs[i].at[:, pl.ds((1 - c) * rows, rows), :], dst_ref=refs[n + i],
            send_sem=send_sems.at[i], recv_sem=recv_sems.at[i],
            device_id=(x, y, 1 - c), device_id_type=MESH))
    return copies


def _small_copies(refs, send_sems, recv_sems):
    packed, slots = refs
    x, y, c, _ = _position()
    me = 4 * x + 2 * y + c
    copies = []
    for r in range(1, N_DEV):
        peer = (x ^ ((r >> 2) & 1), y ^ ((r >> 1) & 1), c ^ (r & 1))
        copies.append(pltpu.make_async_remote_copy(
            src_ref=packed, dst_ref=slots.at[me], send_sem=send_sems.at[r - 1],
            recv_sem=recv_sems.at[r - 1], device_id=peer, device_id_type=MESH))
    return copies


def _block_rows(rows, cols, itemsize=4, target=1 << 20):
    return _pick(max(BF16_ROWS, target // (cols * itemsize)), [rows], unit=BF16_ROWS)


def _pair_sum(name, place, grad, received):
    P, rows, cols = received.shape
    tr = _block_rows(rows, cols, itemsize=2, target=2 << 20)
    nb = rows // tr

    def body(place_ref, g_ref, r_ref, o_ref):
        o_ref[...] = (g_ref[...].astype(F32) + r_ref[...].astype(F32)).astype(BF16)

    def panel(j, pr):
        return pr[0] ^ jnp.where(j == 2, 3, 2 - j)

    grid_spec = pltpu.PrefetchScalarGridSpec(
        num_scalar_prefetch=1, grid=(P - 1, nb),
        in_specs=[pl.BlockSpec((None, tr, cols),
                               lambda j, r, pr: (panel(j, pr), pr[1] * nb + r, 0)),
                  pl.BlockSpec((None, tr, cols), lambda j, r, pr: (panel(j, pr), r, 0))],
        out_specs=pl.BlockSpec((None, tr, cols), lambda j, r, pr: (panel(j, pr), r, 0)))
    return pl.pallas_call(
        body, name=name, grid_spec=grid_spec,
        out_shape=jax.ShapeDtypeStruct(received.shape, BF16),
        compiler_params=_params(2))(place, grad, received)


def _final_sum(name, place, grad, received, from_chips):
    _, rows, cols = received.shape
    tr = _block_rows(rows, cols, target=2 << 20)
    nb = rows // tr

    def body(place_ref, g_ref, r_ref, c_ref, o_ref):
        acc = g_ref[...].astype(F32) + r_ref[...].astype(F32)
        for j in range(3):
            acc = acc + c_ref[j].astype(F32)
        o_ref[...] = acc

    grid_spec = pltpu.PrefetchScalarGridSpec(
        num_scalar_prefetch=1, grid=(nb,),
        in_specs=[pl.BlockSpec((None, tr, cols), lambda r, pr: (pr[0], pr[1] * nb + r, 0)),
                  pl.BlockSpec((None, tr, cols), lambda r, pr: (pr[0], r, 0)),
                  pl.BlockSpec((3, tr, cols), lambda r, pr: (0, r, 0))],
        out_specs=pl.BlockSpec((tr, cols), lambda r, pr: (pr[1] * nb + r, 0)))
    return pl.pallas_call(
        body, name=name, grid_spec=grid_spec,
        out_shape=jax.ShapeDtypeStruct((2 * rows, cols), F32),
        compiler_params=_params(1))(place, grad, received, from_chips)


def _sum_devices(name, me, gathered, own):
    n_dev, rows, cols = gathered.shape
    tr = _pick(256, [rows])

    def body(me_ref, g_ref, own_ref, o_ref):
        term = lambda d: jnp.where(me_ref[0] == d, own_ref[...], g_ref[d])
        acc = term(0)
        for d in range(1, n_dev):
            acc = acc + term(d)
        o_ref[...] = acc

    grid_spec = pltpu.PrefetchScalarGridSpec(
        num_scalar_prefetch=1, grid=(rows // tr,),
        in_specs=[pl.BlockSpec((n_dev, tr, cols), lambda r, me_ref: (0, r, 0)),
                  pl.BlockSpec((tr, cols), lambda r, me_ref: (r, 0))],
        out_specs=pl.BlockSpec((tr, cols), lambda r, me_ref: (r, 0)))
    return pl.pallas_call(
        body, name=name, grid_spec=grid_spec,
        out_shape=jax.ShapeDtypeStruct((rows, cols), F32),
        compiler_params=_params(1))(me, gathered, own)


def _adamw(name, w, g, m, v):
    rows, cols = w.shape
    tr = _block_rows(rows, cols)
    c1 = 1.0 / (1.0 - ADAM_B1 ** ADAM_STEP)
    c2 = 1.0 / (1.0 - ADAM_B2 ** ADAM_STEP)

    def body(w_ref, g_ref, m_ref, v_ref, go_ref, d_ref, nm_ref, nv_ref):
        gv = g_ref[...]
        go_ref[...] = gv
        nm = ADAM_B1 * m_ref[...] + (1.0 - ADAM_B1) * gv
        nv = ADAM_B2 * v_ref[...] + (1.0 - ADAM_B2) * (gv * gv)
        nm_ref[...] = nm
        nv_ref[...] = nv
        d_ref[...] = -ADAM_LR * ((nm * c1) / (jnp.sqrt(nv * c2) + ADAM_EPS) + ADAM_WD * w_ref[...])

    blk = pl.BlockSpec((tr, cols), lambda r: (r, 0))
    shape = jax.ShapeDtypeStruct((rows, cols), F32)
    return pl.pallas_call(
        body, name=name, grid=(rows // tr,), in_specs=[blk] * 4, out_specs=[blk] * 4,
        out_shape=[shape] * 4, compiler_params=_params(1))(w, g, m, v)


BIG = ("ffn1_w_in", "ffn1_w_out", "w_mix_in", "w_mix_out", "w_cq", "w_ckv", "w_co",
       "ffn2_w_in", "ffn2_w_out")
BIG_KIND = {"ffn1_w_in": "c", "ffn1_w_out": "r", "w_mix_in": "c", "w_mix_out": "r", "w_cq": "r",
            "w_ckv": "c", "w_co": "r", "ffn2_w_in": "c", "ffn2_w_out": "r"}
GATHER_GROUPS = (("ffn1_in", ("ffn1_w_in",)), ("ffn1_out", ("ffn1_w_out",)),
                 ("mix_in", ("w_mix_in",)), ("mix_out", ("w_mix_out",)),
                 ("cross", ("w_cq", "w_ckv", "w_co")),
                 ("ffn2_in", ("ffn2_w_in",)), ("ffn2_out", ("ffn2_w_out",)))
GATHER_AFTER = (("ffn1_in", None), ("ffn1_out", "ffn1_in"), ("mix_in", "ffn1_out"),
                ("mix_out", "mix_in"), ("cross", "mix_in"), ("ffn2_in", "mix_in"),
                ("ffn2_out", "ffn2_in"))
RELAYED = ("ffn1_in", "ffn2_in")
TAIL_STAGES = (("sum", "ffn2"), ("sum", "cross"), ("sum", "mix"), ("sum", "ffn1_out"),
               ("update", "ffn2"), ("update", "cross"), ("sum", "ffn1_in"), ("update", "mix"),
               ("update", "ffn1_out"), ("update", "ffn1_in"))
SMALL = ("ffn1_norm", "mix_norm", "ln_v_gain", "ln_v_bias", "spatial_w", "spatial_b", "gnorm_a",
         "gnorm_b", "cross_norm", "mem_norm", "ffn2_norm", "final_norm")
WEIGHTS = ("ffn1_norm", "ffn1_w_in", "ffn1_w_out", "mix_norm", "w_mix_in", "ln_v_gain",
           "ln_v_bias", "spatial_w", "spatial_b", "gnorm_a", "gnorm_b", "w_mix_out", "cross_norm",
           "mem_norm", "w_cq", "w_ckv", "w_co", "ffn2_norm", "ffn2_w_in", "ffn2_w_out",
           "final_norm")


def _pack(arrays):
    return jnp.concatenate([a.reshape(-1, LANE) for a in arrays], axis=0)


def _unpack(packed, like):
    out, row = [], 0
    for a in like:
        rows = a.size // LANE
        out.append(packed[row:row + rows].reshape(a.shape))
        row += rows
    return out


def _local_step(x, mem, target, small, place, weights_of, start_tokens, grads_ready,
                grads_flush):
    T, D = x.shape
    vec = lambda name: small[name].reshape(1, -1)
    w_a = small["ln_v_gain"].size
    w_b = small["gnorm_b"].size
    G = w_a // GROUP_DIM
    w_s = small["spatial_w"].reshape(G, SGU_BLOCK, SGU_BLOCK)
    b_t = small["spatial_b"].reshape(G, SGU_BLOCK).T

    h1, ffn1_saved = _ffn_forward("ffn1", x, vec("ffn1_norm"), weights_of, place,
                                  deps=start_tokens)
    n2 = _rmsnorm_fwd("mix_norm", h1, vec("mix_norm"))
    big = weights_of("mix_in", n2)
    (z,) = _matmul("mix_in", Mat(n2), big["w_mix_in"], "nn", [("c", 1, F32)], tm=2048, tn=256)
    z = z[0]
    y = _sgu_forward("sgu", z, vec("ln_v_gain"), vec("ln_v_bias"), w_s, b_t, vec("gnorm_a"), D)
    yb, sb_total = _sb_forward("stickbreak", z, w_a, w_b)
    y = _rmsnorm_fwd("gnorm_b", yb, vec("gnorm_b"), into=y, col=w_a // w_b)

    def add_res(acc, ex, out):
        out[0][...] = ex[0][...] + acc

    big.update(weights_of("mix_out", y))
    (h2,) = _matmul("mix_out", Mat(y), big["w_mix_out"], "nn", [("c", 1, F32)],
                    tm=1024, tn=1024, extras=[Mat(h1)], epi=add_res)
    h2 = h2[0]
    n3 = _rmsnorm_fwd("cross_norm", h2, vec("cross_norm"))
    memn = _rmsnorm_fwd("mem_norm", mem, vec("mem_norm"))
    big.update(weights_of("cross", n3))
    x_scale = (D // X_HEADS) ** -0.5

    def scaled(acc, ex, out):
        out[0][...] = (acc * x_scale).astype(BF16)

    (q,) = _matmul("cross_q", Mat(n3), big["w_cq"], "nn", [("c", 1, BF16)],
                   tm=1024, tn=1024, epi=scaled)
    (kv,) = _matmul("cross_kv", Mat(memn), big["w_ckv"], "nn", [("c", 1, BF16)], tm=256, tn=1024)
    q, kv = q[0], kv[0]
    o = _xattn_forward("cross_attn", q, kv)
    (h3,) = _matmul("cross_out", Mat(o), big["w_co"], "nn", [("c", 1, F32)],
                    tm=1024, tn=1024, extras=[Mat(h2)], epi=add_res)
    h3 = h3[0]
    h4, ffn2_saved = _ffn_forward("ffn2", h3, vec("ffn2_norm"), weights_of, place)

    gs = {}
    loss_tile, dh4, dh4_bf, gs["final_norm"] = _loss_head("loss_head", h4, vec("final_norm"), target)
    dh3, dh3_bf, gs["ffn2_norm"] = _ffn_backward(
        "ffn2", h3, vec("ffn2_norm"), ffn2_saved, dh4, dh4_bf, grads_ready, grads_flush)

    (do,) = _matmul("cross_do", Mat(dh3_bf), big["w_co"], "nt", [("c", 1, BF16)], tm=1024, tn=512)
    (dw_co,) = _matmul("cross_dwo", Mat(o), Mat(dh3_bf), "tn", [("r", N_CHIPS, BF16)],
                       tm=512, tn=1024)
    dq, dkv = _xattn_backward("cross_attn_bwd", q, kv, do[0])
    (dw_cq,) = _matmul("cross_dwq", Mat(n3), Mat(dq), "tn", [("r", N_CHIPS, BF16)],
                       tm=512, tn=1024)
    (dw_ckv,) = _matmul("cross_dwkv", Mat(memn), Mat(dkv), "tn", [("c", N_CHIPS, BF16)],
                        tm=1024, tn=1024)
    token = grads_ready("cross", {"w_cq": dw_cq, "w_ckv": dw_ckv, "w_co": dw_co})
    dq = _tie("cross_dq_after_swap", dq, [token])
    (dn3,) = _matmul("cross_dn", Mat(dq), big["w_cq"], "nt", [("c", 1, F32)], tm=1024, tn=512)
    (dmemn,) = _matmul("cross_dmem", Mat(dkv), big["w_ckv"], "nt", [("c", 1, F32)],
                       tm=256, tn=1024, tk=1024)
    (gs["mem_norm"],) = _rmsnorm_bwd("mem_dnorm", mem, vec("mem_norm"), dmemn[0], want_dx=False)
    dn3 = _tie("cross_dn_after_scatter", dn3, [grads_flush("cross", gs["mem_norm"])])
    dh2, dh2_bf, gs["cross_norm"] = _rmsnorm_bwd("cross_dnorm", h2, vec("cross_norm"), dn3[0],
                                                 dres=dh3)

    (dy,) = _matmul("mix_dy", Mat(dh2_bf), big["w_mix_out"], "nt", [("c", 1, F32)], tm=1024, tn=512)
    dy = dy[0]
    (dw_mix_out,) = _matmul("mix_dwout", Mat(y), Mat(dh2_bf), "tn", [("r", N_CHIPS, BF16)],
                            tm=512, tn=1024)
    dza, gs["ln_v_gain"], gs["ln_v_bias"], gs["spatial_w"], db, gs["gnorm_a"] = _sgu_backward(
        "sgu_bwd", z, dy, vec("ln_v_gain"), vec("ln_v_bias"), w_s, b_t, vec("gnorm_a"))
    gs["spatial_b"] = db.reshape(G, SGU_BLOCK)
    dob, gs["gnorm_b"] = _rmsnorm_bwd("gnorm_b_bwd", yb, vec("gnorm_b"), dy, dn_col=w_a // w_b,
                                      want_bf16=False)
    dqb, dkvb = _sb_backward("stickbreak_bwd", z, dob, sb_total, w_a, w_b)
    dz = jnp.concatenate([dza, dqb, dkvb[0].astype(BF16), dkvb[1].astype(BF16)], axis=1)
    (dw_mix_in,) = _matmul("mix_dwin", Mat(n2), Mat(dz), "tn", [("c", N_CHIPS, BF16)],
                           tm=1024, tn=1280)
    token = grads_ready("mix", {"w_mix_in": dw_mix_in, "w_mix_out": dw_mix_out})
    dz = _tie("mix_dz_after_swap", dz, [token])
    (dn2,) = _matmul("mix_dn", Mat(dz), big["w_mix_in"], "nt", [("c", 1, F32)],
                     tm=1024, tn=1024, tk=1280)
    dn2 = _tie("mix_dn_after_scatter", dn2, [grads_flush("mix", dn2)])
    dh1, dh1_bf, gs["mix_norm"] = _rmsnorm_bwd("mix_dnorm", h1, vec("mix_norm"), dn2[0], dres=dh2)

    dx, _, gs["ffn1_norm"] = _ffn_backward(
        "ffn1", x, vec("ffn1_norm"), ffn1_saved, dh1, dh1_bf, grads_ready, grads_flush,
        early_out=True)
    gs = {k: g.reshape(small[k].shape) for k, g in gs.items()}
    return loss_tile, dx, gs


def kernel(x, mem, ffn1_norm, ffn1_w_in, ffn1_w_out, mix_norm, w_mix_in, ln_v_gain, ln_v_bias, spatial_w, spatial_b, gnorm_a, gnorm_b, w_mix_out, cross_norm, mem_norm, w_cq, w_ckv, w_co, ffn2_norm, ffn2_w_in, ffn2_w_out, final_norm, loss_target, m_ffn1_norm, m_ffn1_w_in, m_ffn1_w_out, m_mix_norm, m_w_mix_in, m_ln_v_gain, m_ln_v_bias, m_spatial_w, m_spatial_b, m_gnorm_a, m_gnorm_b, m_w_mix_out, m_cross_norm, m_mem_norm, m_w_cq, m_w_ckv, m_w_co, m_ffn2_norm, m_ffn2_w_in, m_ffn2_w_out, m_final_norm, v_ffn1_norm, v_ffn1_w_in, v_ffn1_w_out, v_mix_norm, v_w_mix_in, v_ln_v_gain, v_ln_v_bias, v_spatial_w, v_spatial_b, v_gnorm_a, v_gnorm_b, v_w_mix_out, v_cross_norm, v_mem_norm, v_w_cq, v_w_ckv, v_w_co, v_ffn2_norm, v_ffn2_w_in, v_ffn2_w_out, v_final_norm):
    given = dict(locals())
    w = {k: given[k] for k in WEIGHTS}
    m = {k: given["m_" + k] for k in WEIGHTS}
    v = {k: given["v_" + k] for k in WEIGHTS}

    cx, cy, cc = lax.axis_index("x"), lax.axis_index("y"), lax.axis_index("c")
    place = jnp.stack([2 * cx + cy, cc]).astype(jnp.int32)

    names_of = dict(GATHER_GROUPS)
    own = {g: [_cast_own(f"cast_{k}", place, w[k][0]) for k in names] for g, names in GATHER_GROUPS}
    gathers = {}

    def start_gather(group, deps):
        first_hop = _near_copies if group in RELAYED else _gather_copies
        n_sems = (2 if group in RELAYED else 3) * len(own[group])
        send, recv, arrays, token = _split_start(f"gather_start_{group}", own[group],
                                                 first_hop, n_sems, deps)
        gathers[group] = (send, recv, arrays)
        return token

    start_tokens = [start_gather(g, ()) for g, after in GATHER_AFTER if after is None]
    start_tokens += [a for g, after in GATHER_AFTER if after is not None for a in own[g]]

    def weights_of(group, after):
        send, recv, arrays = gathers[group]
        as_mats = lambda arrs: {k: Mat(a, BIG_KIND[k]) for k, a in zip(names_of[group], arrs)}
        if group not in RELAYED:
            arrays = _split_wait(f"gather_wait_{group}", arrays, send, recv, after, _gather_copies)
            tokens = [start_gather(g, (arrays[0],)) for g, a in GATHER_AFTER if a == group]
            return as_mats(_forward_to_sibling(f"gather_forward_{group}", list(arrays), tokens))
        arrays = _split_wait(f"gather_wait_{group}", arrays, send, recv, after, _near_copies)
        send, recv, arrays, token = _split_start(f"gather_relay_{group}", list(arrays),
                                                 _relay_copies, 2 * len(arrays))
        tokens = [token] + [start_gather(g, (arrays[0],)) for g, a in GATHER_AFTER if a == group]
        arrays = _forward_to_sibling(f"gather_forward_{group}", list(arrays), tokens, which=(0, 1))

        def finish(after):
            arrs = _split_wait(f"gather_relay_wait_{group}", arrays, send, recv, after,
                               _relay_copies)
            return as_mats(_forward_to_sibling(f"gather_forward_diag_{group}", list(arrs),
                                               which=(2,)))

        return {**as_mats(arrays), "finish": finish}

    swaps, scatters = {}, {}

    def grads_ready(group, partial):
        names = list(partial)
        grads_ = [partial[k] for k in names]
        lands = [lax.empty((g.shape[0], g.shape[1] // 2, g.shape[2]), g.dtype) for g in grads_]
        send, recv, arrays, token = _split_start(f"swap_start_{group}", grads_ + lands,
                                                 _swap_copies, len(names))
        swaps[group] = (names, send, recv, arrays)
        return token

    def grads_flush(group, after):
        names, send, recv, arrays = swaps[group]
        arrays = _split_wait(f"swap_wait_{group}", arrays, send, recv, after, _swap_copies)
        grads_, from_sibling = arrays[:len(names)], arrays[len(names):]
        sums = [_pair_sum(f"pair_sum_{k}", place, g, r)
                for k, g, r in zip(names, grads_, from_sibling)]
        lands = [lax.empty((3,) + s.shape[1:], s.dtype) for s in sums]
        send, recv, arrays, token = _split_start(f"scatter_start_{group}", sums + lands,
                                                 _scatter_copies, 3 * len(names))
        scatters[group] = (names, grads_, from_sibling, send, recv, arrays)
        return token

    small = {k: w[k] for k in SMALL}
    loss_tile, grad_x, gs = _local_step(x[0], mem[0], loss_target[0], small, place, weights_of,
                                        start_tokens, grads_ready, grads_flush)

    packed = _pack([gs[k] for k in SMALL] + [loss_tile])
    slots = jnp.zeros((N_DEV,) + packed.shape, packed.dtype)
    small_send, small_recv, small_arrays, _ = _split_start(
        "small_start", [packed, slots], _small_copies, N_DEV - 1)

    grad, delta, new_m, new_v = {}, {}, {}, {}
    shares = {}
    after = [grad_x]
    for stage, group in TAIL_STAGES:
        if stage == "sum":
            names, grads_, from_sibling, send, recv, arrays = scatters[group]
            arrays = _split_wait(f"scatter_wait_{group}", arrays, send, recv, after,
                                 _scatter_copies)
            from_chips = arrays[len(names):]
            shards = [_final_sum(f"final_sum_{k}", place, g, r, f)
                      for k, g, r, f in zip(names, grads_, from_sibling, from_chips)]
            send, recv, shards, token = _split_start(f"share_start_{group}", shards,
                                                     _share_copies, len(names))
            shares[group] = (names, send, recv, shards)
            after = [token]
        else:
            names, send, recv, shards = shares[group]
            shards = _split_wait(f"share_wait_{group}", shards, send, recv, after, _share_copies)
            after = []
            for k, g_ in zip(names, shards):
                g_, d_, m_, v_ = _adamw(f"adamw_{k}", w[k][0], g_, m[k][0], v[k][0])
                grad[k], delta[k], new_m[k], new_v[k] = g_[None], d_[None], m_[None], v_[None]
                after.append(v_)

    packed, slots = _split_wait("small_wait", small_arrays, small_send, small_recv, after,
                                _small_copies)
    me = (4 * cx + 2 * cy + cc).astype(jnp.int32).reshape(1)
    total = _sum_devices("sum_small", me, slots, packed)
    n_small = total.shape[0] - SUBLANE
    loss = total[n_small, 0]
    small_g = total[:n_small]
    g_s, d_s, m_s, v_s = _adamw("adamw_small", _pack([w[k] for k in SMALL]), small_g,
                                _pack([m[k] for k in SMALL]), _pack([v[k] for k in SMALL]))
    like = [w[k] for k in SMALL]
    for k, g_, d_, m_, v_ in zip(SMALL, _unpack(g_s, like), _unpack(d_s, like),
                                 _unpack(m_s, like), _unpack(v_s, like)):
        grad[k], delta[k], new_m[k], new_v[k] = g_, d_, m_, v_

    return (loss, grad_x[None], *[grad[k] for k in WEIGHTS], *[delta[k] for k in WEIGHTS],
            *[new_m[k] for k in WEIGHTS], *[new_v[k] for k in WEIGHTS])
```

```python
import functools
import math

import jax
import jax.numpy as jnp
from jax import lax
from jax.experimental import pallas as pl
from jax.experimental.pallas import tpu as pltpu

F32 = jnp.float32
BF16 = jnp.bfloat16
MESH = pl.DeviceIdType.MESH

EPS = 1e-6
CHUNK = 64
SGU_BLOCK = 128
GROUP_DIM = 128
X_HEADS = 4
N_CHIPS = 4
N_DEV = 8
LANE = 128
SUBLANE = 8
BF16_ROWS = 16

ADAM_LR = 0.001
ADAM_B1 = 0.9
ADAM_B2 = 0.999
ADAM_EPS = 1e-08
ADAM_WD = 0.01
ADAM_STEP = 10

V7X_VMEM_BYTES = 64 << 20
VMEM_LIMIT = V7X_VMEM_BYTES - (8 << 20)


def _params(n_grid):
    return pltpu.CompilerParams(dimension_semantics=("arbitrary",) * n_grid,
                                vmem_limit_bytes=VMEM_LIMIT)


def _pick(pref, dims, unit=None):
    g = functools.reduce(math.gcd, dims)
    if unit is None:
        unit = LANE if g % LANE == 0 else SUBLANE
    cands = [d for d in range(unit, g + 1, unit) if g % d == 0] or [g]
    return min(cands, key=lambda d: abs(math.log(d / pref)))


def _any_spec():
    return pl.BlockSpec(memory_space=pl.ANY)


class Mat:
    def __init__(self, arr, kind="c"):
        if arr.ndim == 2:
            arr = arr[None]
        self.arr, self.kind = arr, kind
        self.P, self.prow, self.pcol = arr.shape
        self.rows = self.prow * (self.P if kind == "r" else 1)
        self.cols = self.pcol * (self.P if kind == "c" else 1)
        self.dtype = arr.dtype

    def spec(self, tr, tc, rc_fn):
        if self.kind == "c":
            per = self.pcol // tc
            assert per * tc == self.pcol, (self.pcol, tc)

            def imap(*g):
                i, j = rc_fn(*g)
                return (j // per, i, j % per)
        else:
            per = self.prow // tr
            assert per * tr == self.prow, (self.prow, tr)

            def imap(*g):
                i, j = rc_fn(*g)
                return (i // per, i % per, j)
        return pl.BlockSpec((None, tr, tc), imap)

    def two_d(self):
        assert self.P == 1
        return self.arr[0]


def _out_mat(kind, P, rows, cols, dtype):
    shape = (P, rows, cols // P) if kind == "c" else (P, rows // P, cols)
    return jax.ShapeDtypeStruct(shape, dtype)


def _matmul(name, A, B, mode, outs, *, tm=1024, tn=1024, tk=2048, extras=(), epi=None):
    if mode == "nn":
        M, K, N = A.rows, A.cols, B.cols
        assert B.rows == K
    elif mode == "nt":
        M, K, N = A.rows, A.cols, B.rows
        assert B.cols == K
    else:
        K, M, N = A.rows, A.cols, B.cols
        assert B.rows == K
    mdims, ndims, kdims = [M], [N], [K]
    whole_b = mode == "nn" and B.kind == "r" and B.P > 1 and K <= tk
    whole_bt = mode == "nt" and B.kind == "r" and B.P > 1 and N <= tn
    if whole_b:
        kdims.append(A.pcol)
        ndims.append(B.pcol)
    elif whole_bt:
        kdims += [A.pcol, B.pcol]
    elif mode == "tn":
        assert A.kind == "c" and B.kind == "c"
        mdims.append(A.pcol)
        ndims.append(B.pcol)
    else:
        (mdims if A.kind == "r" else kdims).append(A.prow if A.kind == "r" else A.pcol)
        if mode == "nn":
            (kdims if B.kind == "r" else ndims).append(B.prow if B.kind == "r" else B.pcol)
        else:
            (ndims if B.kind == "r" else kdims).append(B.prow if B.kind == "r" else B.pcol)
    for o in list(outs) + list(extras):
        if isinstance(o, Mat):
            (mdims if o.kind == "r" else ndims).append(o.prow if o.kind == "r" else o.pcol)
        elif isinstance(o[0], str):
            (mdims if o[0] == "r" else ndims).append((M if o[0] == "r" else N) // o[1])
    tm, tn = _pick(tm, mdims), _pick(tn, ndims)
    tk = K if mode == "tn" else _pick(tk, kdims)
    nk = K // tk
    grid = (M // tm, N // tn, nk)

    if mode == "tn":
        a_spec = A.spec(K, tm, lambda m, n, k: (0, m))
        b_spec = B.spec(K, tn, lambda m, n, k: (0, n))
    else:
        a_spec = A.spec(tm, tk, lambda m, n, k: (m, k))
        if whole_b:
            b_spec = pl.BlockSpec((B.P, B.prow, tn), lambda m, n, k: (0, 0, n))
        elif whole_bt:
            b_spec = pl.BlockSpec((B.P, B.prow, tk), lambda m, n, k: (0, 0, k))
        elif mode == "nn":
            b_spec = B.spec(tk, tn, lambda m, n, k: (k, n))
        else:
            b_spec = B.spec(tn, tk, lambda m, n, k: (n, k))

    def mn_spec(o):
        if isinstance(o, Mat):
            return o.spec(tm, tn, lambda m, n, k: (m, n))
        if isinstance(o[0], str):
            kind, P = o[0], o[1]
            fake = Mat.__new__(Mat)
            fake.kind, fake.P = kind, P
            fake.prow = M // P if kind == "r" else M
            fake.pcol = N // P if kind == "c" else N
            return Mat.spec(fake, tm, tn, lambda m, n, k: (m, n))
        return o[1](tm, tn)

    out_shapes = tuple(_out_mat(o[0], o[1], M, N, o[2]) if isinstance(o[0], str) else o[0]
                       for o in outs)
    out_specs = tuple(mn_spec(o) for o in outs)
    extra_arrays = tuple(e.arr if isinstance(e, Mat) else e[0] for e in extras)
    extra_specs = tuple(mn_spec(e) for e in extras)
    n_ex, n_out = len(extras), len(outs)
    tt = _pick(256, [tm])
    dims = (((1,), (1 if mode == "nt" else 0,)), ((), ()))

    def body(*refs):
        a_ref, b_ref = refs[:2]
        ex_refs = refs[2:2 + n_ex]
        out_refs = refs[2 + n_ex:2 + n_ex + n_out]
        scratch = refs[2 + n_ex + n_out:]
        if mode == "tn":
            at_ref = scratch[0]

            @pl.when(pl.program_id(1) == 0)
            def _():
                for c0 in range(0, tm, tt):
                    at_ref[c0:c0 + tt, :] = a_ref[:, c0:c0 + tt].astype(F32).T.astype(BF16)

            lhs = at_ref[...]
        else:
            lhs = a_ref[...].astype(BF16)
        rhs = b_ref[...]
        if whole_b or whole_bt:
            rhs = rhs.reshape(B.P * B.prow, rhs.shape[-1])
        part = lax.dot_general(lhs, rhs.astype(BF16), dims, preferred_element_type=F32)

        def finish(acc):
            if epi is None:
                out_refs[0][...] = acc.astype(out_refs[0].dtype)
            else:
                epi(acc, ex_refs, out_refs)

        if nk == 1:
            finish(part)
        else:
            acc_ref = scratch[0]
            k = pl.program_id(2)

            @pl.when(k == 0)
            def _():
                acc_ref[...] = part

            @pl.when(k > 0)
            def _():
                acc_ref[...] += part

            @pl.when(k == nk - 1)
            def _():
                finish(acc_ref[...])

    scratch_shapes = []
    if mode == "tn":
        scratch_shapes.append(pltpu.VMEM((tm, K), BF16))
    elif nk > 1:
        scratch_shapes.append(pltpu.VMEM((tm, tn), F32))
    res = pl.pallas_call(
        body, name=name, grid=grid,
        in_specs=[a_spec, b_spec, *extra_specs], out_specs=out_specs, out_shape=out_shapes,
        scratch_shapes=scratch_shapes, compiler_params=_params(3),
    )(A.arr, B.arr, *extra_arrays)
    return res


def _row_tile(T):
    return _pick(256, [T])


def _tie(name, x, deps):
    def body(*refs):
        refs[-1][...] = jnp.zeros_like(refs[-1])

    return pl.pallas_call(
        body, name=name, in_specs=[_any_spec()] * (1 + len(deps)),
        out_specs=(_any_spec(), pl.BlockSpec(memory_space=pltpu.VMEM)),
        out_shape=(jax.ShapeDtypeStruct(x.shape, x.dtype),
                   jax.ShapeDtypeStruct((SUBLANE, LANE), F32)),
        input_output_aliases={0: 0},
    )(x, *deps)[0]


def _rmsnorm_fwd(name, x, g, *, into=None, col=0, deps=()):
    T, W = x.shape
    tr = _row_tile(T)

    def body(x_ref, g_ref, *rest):
        o_ref = rest[-1]
        xv = x_ref[...]
        rstd = lax.rsqrt(jnp.mean(xv * xv, axis=-1, keepdims=True) + EPS)
        o_ref[...] = (xv * rstd * g_ref[...]).astype(o_ref.dtype)

    in_specs = [pl.BlockSpec((tr, W), lambda i: (i, 0)), pl.BlockSpec((1, W), lambda i: (0, 0))]
    args = [x, g]
    kwargs = {}
    if into is None:
        out_shape = jax.ShapeDtypeStruct((T, W), BF16)
    else:
        out_shape = jax.ShapeDtypeStruct(into.shape, into.dtype)
        in_specs.append(_any_spec())
        args.append(into)
        kwargs["input_output_aliases"] = {2: 0}
    in_specs += [_any_spec()] * len(deps)
    args += list(deps)
    return pl.pallas_call(
        body, name=name, grid=(T // tr,), in_specs=in_specs,
        out_specs=pl.BlockSpec((tr, W), lambda i: (i, col)), out_shape=out_shape,
        compiler_params=_params(1), **kwargs)(*args)


def _rmsnorm_bwd(name, x, g, dn, *, dn_col=0, dres=None, want_dx=True, want_bf16=True):
    T, W = x.shape
    tr = _row_tile(T)
    has_res = dres is not None

    def body(*refs):
        x_ref, g_ref, dn_ref = refs[:3]
        pos = 3
        dres_ref = None
        if has_res:
            dres_ref = refs[pos]
            pos += 1
        outs = refs[pos:]
        dg_ref = outs[-1]
        xv = x_ref[...]
        rstd = lax.rsqrt(jnp.mean(xv * xv, axis=-1, keepdims=True) + EPS)
        xhat = xv * rstd
        dnv = dn_ref[...].astype(F32)

        @pl.when(pl.program_id(0) == 0)
        def _():
            dg_ref[...] = jnp.zeros_like(dg_ref)

        dg_ref[...] += jnp.sum(dnv * xhat, axis=0, keepdims=True)
        if want_dx:
            t = dnv * g_ref[...]
            dx = rstd * (t - xhat * jnp.mean(t * xhat, axis=-1, keepdims=True))
            if has_res:
                dx = dx + dres_ref[...]
            outs[0][...] = dx
            if want_bf16:
                outs[1][...] = dx.astype(BF16)

    row = pl.BlockSpec((tr, W), lambda i: (i, 0))
    in_specs = [row, pl.BlockSpec((1, W), lambda i: (0, 0)),
                pl.BlockSpec((tr, W), lambda i: (i, dn_col))]
    args = [x, g, dn]
    if has_res:
        in_specs.append(row)
        args.append(dres)
    out_shape, out_specs = [], []
    if want_dx:
        out_shape.append(jax.ShapeDtypeStruct((T, W), F32))
        out_specs.append(row)
        if want_bf16:
            out_shape.append(jax.ShapeDtypeStruct((T, W), BF16))
            out_specs.append(row)
    out_shape.append(jax.ShapeDtypeStruct((1, W), F32))
    out_specs.append(pl.BlockSpec((1, W), lambda i: (0, 0)))
    return pl.pallas_call(
        body, name=name, grid=(T // tr,), in_specs=in_specs, out_specs=out_specs,
        out_shape=out_shape, compiler_params=_params(1))(*args)


def _loss_head(name, h, g, target):
    T, W = h.shape
    tr = _row_tile(T)

    def body(h_ref, g_ref, t_ref, loss_ref, dx_ref, dxb_ref, dg_ref):
        xv = h_ref[...]
        gv = g_ref[...]
        rstd = lax.rsqrt(jnp.mean(xv * xv, axis=-1, keepdims=True) + EPS)
        xhat = xv * rstd
        diff = xhat * gv - t_ref[...]

        @pl.when(pl.program_id(0) == 0)
        def _():
            dg_ref[...] = jnp.zeros_like(dg_ref)
            loss_ref[...] = jnp.zeros_like(loss_ref)

        loss_ref[...] += 0.5 * jnp.sum(jnp.mean(diff * diff, axis=-1, keepdims=True))
        dnv = diff * (1.0 / W)
        dg_ref[...] += jnp.sum(dnv * xhat, axis=0, keepdims=True)
        t = dnv * gv
        dx = rstd * (t - xhat * jnp.mean(t * xhat, axis=-1, keepdims=True))
        dx_ref[...] = dx
        dxb_ref[...] = dx.astype(BF16)

    row = pl.BlockSpec((tr, W), lambda i: (i, 0))
    vec = pl.BlockSpec((1, W), lambda i: (0, 0))
    return pl.pallas_call(
        body, name=name, grid=(T // tr,), in_specs=[row, vec, row],
        out_specs=[pl.BlockSpec((SUBLANE, LANE), lambda i: (0, 0)), row, row, vec],
        out_shape=[jax.ShapeDtypeStruct((SUBLANE, LANE), F32), jax.ShapeDtypeStruct((T, W), F32),
                   jax.ShapeDtypeStruct((T, W), BF16), jax.ShapeDtypeStruct((1, W), F32)],
        compiler_params=_params(1))(h, g, target)


def _sigmoid(x):
    return 1.0 / (1.0 + jnp.exp(-x))


def _ffn_in(name, n, W, place, half, prev=None):
    T, D = n.shape
    F = W.cols // 2
    tm = _pick(2048, [T])
    tn = _pick(512, [W.pcol])
    per = W.pcol // tn

    def body(place_ref, a_ref, wg_ref, wu_ref, *rest):
        gu_ref, act_ref = rest[-2:]
        a = a_ref[...]
        gate = jnp.dot(a, wg_ref[...], preferred_element_type=F32)
        up = jnp.dot(a, wu_ref[...], preferred_element_type=F32)
        sig = _sigmoid(gate)
        silu = gate * sig
        gu_ref[0] = (up * sig * (1.0 + gate * (1.0 - sig))).astype(BF16)
        gu_ref[1] = silu.astype(BF16)
        act_ref[...] = (silu * up).astype(BF16)

    def pair(pr):
        return (pr[0] + half) % 2

    in_specs = [pl.BlockSpec((tm, D), lambda m, j, pr: (m, 0)),
                pl.BlockSpec((None, D, tn), lambda m, j, pr: (pair(pr), 0, j)),
                pl.BlockSpec((None, D, tn), lambda m, j, pr: (2 + pair(pr), 0, j))]
    args = [place, n, W.arr, W.arr]
    kwargs = {}
    if prev is not None:
        in_specs += [_any_spec(), _any_spec()]
        args += list(prev)
        kwargs["input_output_aliases"] = {4: 0, 5: 1}
    grid_spec = pltpu.PrefetchScalarGridSpec(
        num_scalar_prefetch=1, grid=(T // tm, per), in_specs=in_specs,
        out_specs=[pl.BlockSpec((2, tm, tn), lambda m, j, pr: (0, m, pair(pr) * per + j)),
                   pl.BlockSpec((tm, tn), lambda m, j, pr: (m, pair(pr) * per + j))])
    return pl.pallas_call(
        body, name=name, grid_spec=grid_spec,
        out_shape=[jax.ShapeDtypeStruct((2, T, F), BF16), jax.ShapeDtypeStruct((T, F), BF16)],
        compiler_params=_params(2), **kwargs)(*args)


def _ffn_forward(tag, h, norm_g, weights_of, place, deps=()):
    n = _rmsnorm_fwd(f"{tag}_norm", h, norm_g, deps=deps)
    got = weights_of(f"{tag}_in", n)
    gu, act = _ffn_in(f"{tag}_in_a", n, got[f"{tag}_w_in"], place, 0)
    w_in = got["finish"](act)[f"{tag}_w_in"]
    gu, act = _ffn_in(f"{tag}_in_b", n, w_in, place, 1, (gu, act))
    w_out = weights_of(f"{tag}_out", act)[f"{tag}_w_out"]

    def epi(acc, ex, out):
        out[0][...] = ex[0][...] + 0.5 * acc

    (h_out,) = _matmul(f"{tag}_out", Mat(act), w_out, "nn", [("c", 1, F32)],
                       tm=1024, tn=512, tk=8192, extras=[Mat(h)], epi=epi)
    return h_out[0], (n, gu, act, w_in, w_out)


def _ffn_backward(tag, h_in, norm_g, saved, dh, dh_bf, grads_ready, grads_flush,
                  early_out=False):
    n, gu, act, w_in, w_out = saved
    T, F = act.shape

    def epi(acc, ex, out):
        dact = 0.5 * acc
        out[0][0] = (dact * ex[0][0].astype(F32)).astype(BF16)
        out[0][1] = (dact * ex[0][1].astype(F32)).astype(BF16)

    def pair_spec(tm, tn):
        return pl.BlockSpec((2, tm, tn), lambda m, j, k: (0, m, j))

    def half(acc, ex, out):
        out[0][...] = (0.5 * acc).astype(out[0].dtype)

    (dw_out,) = _matmul(f"{tag}_dwout", Mat(act), Mat(dh_bf), "tn", [("r", N_CHIPS, BF16)],
                        tm=1408, tn=512, epi=half)
    if early_out:
        token = grads_ready(f"{tag}_out", {f"{tag}_w_out": dw_out})
        dh_bf = _tie(f"{tag}_dh_after_swap", dh_bf, [token])
    (dgu,) = _matmul(f"{tag}_dact", Mat(dh_bf), w_out, "nt",
                     [(jax.ShapeDtypeStruct((2, T, F), BF16), pair_spec)],
                     tm=512, tn=1408, extras=[(gu, pair_spec)], epi=epi)
    if early_out:
        dgu = _tie(f"{tag}_dgu_after_scatter", dgu, [grads_flush(f"{tag}_out", dgu)])
    (dw_in,) = _matmul(f"{tag}_dwin", Mat(n), Mat(dgu), "tn", [("c", N_CHIPS, BF16)],
                       tm=1024, tn=1408)
    if early_out:
        group, partial = f"{tag}_in", {f"{tag}_w_in": dw_in}
    else:
        group, partial = tag, {f"{tag}_w_in": dw_in, f"{tag}_w_out": dw_out}
    dgu = _tie(f"{tag}_dgu_after_swap", dgu, [grads_ready(group, partial)])
    (dn,) = _matmul(f"{tag}_dn", Mat(dgu), w_in, "nt", [("c", 1, F32)],
                    tm=1024, tn=1024, tk=2816)
    dn = _tie(f"{tag}_dn_after_scatter", dn, [grads_flush(group, dn)])
    return _rmsnorm_bwd(f"{tag}_dnorm", h_in, norm_g, dn[0], dres=dh)


_GELU_C = math.sqrt(2.0 / math.pi)
_GELU_A = 0.044715


def _gelu(x):
    return 0.5 * x * (1.0 + jnp.tanh(_GELU_C * (x + _GELU_A * x * x * x)))


def _gelu_grad(x):
    th = jnp.tanh(_GELU_C * (x + _GELU_A * x * x * x))
    return 0.5 * (1.0 + th) + 0.5 * x * (1.0 - th * th) * _GELU_C * (1.0 + 3.0 * _GELU_A * x * x)


def _chunk_mask():
    t = lax.broadcasted_iota(jnp.int32, (SGU_BLOCK, SGU_BLOCK), 0) // CHUNK
    s = lax.broadcasted_iota(jnp.int32, (SGU_BLOCK, SGU_BLOCK), 1) // CHUNK
    return s <= t


def _sgu_group_forward(v_g, lg, lb, wm_bf, b_col):
    mu = jnp.mean(v_g, axis=-1, keepdims=True)
    xc = v_g - mu
    rstd = lax.rsqrt(jnp.mean(xc * xc, axis=-1, keepdims=True) + EPS)
    vhat = xc * rstd
    vn = vhat * lg + lb
    mixed = jnp.dot(wm_bf, vn.astype(BF16), preferred_element_type=F32) + b_col
    return vhat, rstd, vn, mixed


def _sgu_forward(name, z, ln_g, ln_b, w_s, b_t, gn, d_model):
    T = z.shape[0]
    W_A = ln_g.shape[1]
    G = W_A // GROUP_DIM

    def body(z_ref, lg_ref, lb_ref, w_ref, bt_ref, gn_ref, y_ref):
        mask = _chunk_mask()
        u = _gelu(z_ref[:, :W_A])
        v = _gelu(z_ref[:, W_A:])
        cols = []
        for g in range(G):
            sl = slice(g * GROUP_DIM, (g + 1) * GROUP_DIM)
            wm = jnp.where(mask, w_ref[g], 0.0).astype(BF16)
            _, _, _, mixed = _sgu_group_forward(v[:, sl], lg_ref[:, sl], lb_ref[:, sl], wm,
                                                bt_ref[:, g:g + 1])
            cols.append(u[:, sl] * mixed)
        ya = jnp.concatenate(cols, axis=1)
        rstd = lax.rsqrt(jnp.mean(ya * ya, axis=-1, keepdims=True) + EPS)
        y_ref[...] = (ya * rstd * gn_ref[...]).astype(BF16)

    vec = pl.BlockSpec((1, W_A), lambda i: (0, 0))
    return pl.pallas_call(
        body, name=name, grid=(T // SGU_BLOCK,),
        in_specs=[pl.BlockSpec((SGU_BLOCK, 2 * W_A), lambda i: (i, 0)), vec, vec,
                  pl.BlockSpec((G, SGU_BLOCK, SGU_BLOCK), lambda i: (0, 0, 0)),
                  pl.BlockSpec((SGU_BLOCK, G), lambda i: (0, 0)), vec],
        out_specs=pl.BlockSpec((SGU_BLOCK, W_A), lambda i: (i, 0)),
        out_shape=jax.ShapeDtypeStruct((T, d_model), BF16),
        compiler_params=_params(1))(z, ln_g, ln_b, w_s, b_t, gn)


def _sgu_backward(name, z, dy, ln_g, ln_b, w_s, b_t, gn):
    T = z.shape[0]
    W_A = ln_g.shape[1]
    G = W_A // GROUP_DIM

    def body(z_ref, dy_ref, lg_ref, lb_ref, w_ref, bt_ref, gn_ref,
             dz_ref, dlg_ref, dlb_ref, dw_ref, db_ref, dgn_ref):
        @pl.when(pl.program_id(0) == 0)
        def _():
            for r in (dlg_ref, dlb_ref, dw_ref, db_ref, dgn_ref):
                r[...] = jnp.zeros_like(r)

        mask = _chunk_mask()
        zu = z_ref[:, :W_A]
        zv = z_ref[:, W_A:]
        u = _gelu(zu)
        v = _gelu(zv)
        saved, cols = [], []
        for g in range(G):
            sl = slice(g * GROUP_DIM, (g + 1) * GROUP_DIM)
            wm = jnp.where(mask, w_ref[g], 0.0)
            vhat, rstd, vn, mixed = _sgu_group_forward(
                v[:, sl], lg_ref[:, sl], lb_ref[:, sl], wm.astype(BF16), bt_ref[:, g:g + 1])
            saved.append((wm, vhat, rstd, vn, mixed))
            cols.append(u[:, sl] * mixed)
        ya = jnp.concatenate(cols, axis=1)
        rstd_a = lax.rsqrt(jnp.mean(ya * ya, axis=-1, keepdims=True) + EPS)
        ya_hat = ya * rstd_a
        dyv = dy_ref[...].astype(F32)
        dgn_ref[...] += jnp.sum(dyv * ya_hat, axis=0, keepdims=True)
        t = dyv * gn_ref[...]
        dya = rstd_a * (t - ya_hat * jnp.mean(t * ya_hat, axis=-1, keepdims=True))
        du_cols, dv_cols, dlg_cols, dlb_cols = [], [], [], []
        for g in range(G):
            sl = slice(g * GROUP_DIM, (g + 1) * GROUP_DIM)
            wm, vhat, rstd, vn, mixed = saved[g]
            dya_g = dya[:, sl]
            du_cols.append(dya_g * mixed)
            dmix = dya_g * u[:, sl]
            dmix_bf = dmix.astype(BF16)
            db_ref[g] += jnp.sum(dmix, axis=1, keepdims=True)
            dw = lax.dot_general(dmix_bf, vn.astype(BF16), (((1,), (1,)), ((), ())),
                                 preferred_element_type=F32)
            dw_ref[g] += jnp.where(mask, dw, 0.0)
            dvn = jnp.dot(wm.T.astype(BF16), dmix_bf, preferred_element_type=F32)
            dlg_cols.append(jnp.sum(dvn * vhat, axis=0, keepdims=True))
            dlb_cols.append(jnp.sum(dvn, axis=0, keepdims=True))
            dvhat = dvn * lg_ref[:, sl]
            dv_cols.append(rstd * (dvhat - jnp.mean(dvhat, axis=-1, keepdims=True)
                                   - vhat * jnp.mean(dvhat * vhat, axis=-1, keepdims=True)))
        dlg_ref[...] += jnp.concatenate(dlg_cols, axis=1)
        dlb_ref[...] += jnp.concatenate(dlb_cols, axis=1)
        dz_ref[:, :W_A] = (jnp.concatenate(du_cols, axis=1) * _gelu_grad(zu)).astype(BF16)
        dz_ref[:, W_A:] = (jnp.concatenate(dv_cols, axis=1) * _gelu_grad(zv)).astype(BF16)

    vec = pl.BlockSpec((1, W_A), lambda i: (0, 0))
    wspec = pl.BlockSpec((G, SGU_BLOCK, SGU_BLOCK), lambda i: (0, 0, 0))
    return pl.pallas_call(
        body, name=name, grid=(T // SGU_BLOCK,),
        in_specs=[pl.BlockSpec((SGU_BLOCK, 2 * W_A), lambda i: (i, 0)),
                  pl.BlockSpec((SGU_BLOCK, W_A), lambda i: (i, 0)), vec, vec, wspec,
                  pl.BlockSpec((SGU_BLOCK, G), lambda i: (0, 0)), vec],
        out_specs=[pl.BlockSpec((SGU_BLOCK, 2 * W_A), lambda i: (i, 0)), vec, vec, wspec,
                   pl.BlockSpec((G, SGU_BLOCK, 1), lambda i: (0, 0, 0)), vec],
        out_shape=[jax.ShapeDtypeStruct((T, 2 * W_A), BF16), jax.ShapeDtypeStruct((1, W_A), F32),
                   jax.ShapeDtypeStruct((1, W_A), F32),
                   jax.ShapeDtypeStruct((G, SGU_BLOCK, SGU_BLOCK), F32),
                   jax.ShapeDtypeStruct((G, SGU_BLOCK, 1), F32),
                   jax.ShapeDtypeStruct((1, W_A), F32)],
        compiler_params=_params(1))(z, dy, ln_g, ln_b, w_s, b_t, gn)


def _split_dot(x, tri):
    hi = x.astype(BF16)
    lo = (x - hi.astype(F32)).astype(BF16)
    return (jnp.dot(hi, tri, preferred_element_type=F32)
            + jnp.dot(lo, tri, preferred_element_type=F32))


def _tri(n, rel):
    r = lax.broadcasted_iota(jnp.int32, (n, n), 0)
    c = lax.broadcasted_iota(jnp.int32, (n, n), 1)
    return rel(r, c).astype(BF16)


def _dot_nt(a, b):
    return lax.dot_general(a, b, (((1,), (1,)), ((), ())), preferred_element_type=F32)


def _dot_tn(a, b):
    return lax.dot_general(a, b, (((0,), (0,)), ((), ())), preferred_element_type=F32)


def _sb_scores(qs, kj, mask):
    zz = _dot_nt(qs, kj)
    log_beta = jnp.minimum(zz, 0.0) - jnp.log(1.0 + jnp.exp(-jnp.abs(zz)))
    log_1m = log_beta - zz
    if mask is not None:
        log_1m = jnp.where(mask, log_1m, 0.0)
    return log_beta, log_1m


def _masked(mask, x):
    return x if mask is None else jnp.where(mask, x, 0.0)


def _below(old, new, row0):
    if row0 == 0:
        return tuple(new)
    return tuple(jnp.concatenate([o[:row0], n], axis=0) for o, n in zip(old, new))


def _sb_tiles(T):
    tk = _pick(256, [T])
    tq = 2 * tk if T % (2 * tk) == 0 else tk
    return tq, tk


def _sb_cols(w_a, w_b):
    base = 2 * w_a // GROUP_DIM
    per = w_b // GROUP_DIM
    return base, base + per, base + 2 * per


def _sb_forward(name, z, w_a, w_b):
    T = z.shape[0]
    H = w_b // GROUP_DIM
    tq, tk = _sb_tiles(T)
    per = tq // tk
    qc, kc, vc = _sb_cols(w_a, w_b)
    scale = GROUP_DIM ** -0.5

    def body(q_ref, k_ref, v_ref, y_ref, tot_ref):
        i = pl.program_id(1)
        qs = (q_ref[...] * scale).astype(BF16)
        upper = _tri(tk, lambda r, c: r > c)
        ahead = (lax.broadcasted_iota(jnp.int32, (tq, tk), 1)
                 - lax.broadcasted_iota(jnp.int32, (tq, tk), 0))

        def step(j, carry, masked, row0=0):
            acc, later = (c[row0:] for c in carry)
            k0 = pl.multiple_of(j * tk, tk)
            kj = k_ref[pl.ds(k0, tk), :].astype(BF16)
            vj = v_ref[pl.ds(k0, tk), :].astype(BF16)
            mask = ahead[row0:] < i * tq - k0 if masked else None
            log_beta, log_1m = _sb_scores(qs[row0:], kj, mask)
            rest = _split_dot(log_1m, upper) + later
            a = _masked(mask, jnp.exp(log_beta + rest))
            acc = acc + jnp.dot(a.astype(BF16), vj, preferred_element_type=F32)
            later = later + jnp.sum(log_1m, axis=1, keepdims=True)
            return _below(carry, (acc, later), row0)

        def blocks(p, c):
            for d in reversed(range(per)):
                c = step(p * per + d, c, False)
            return c

        carry = (jnp.zeros((tq, GROUP_DIM), F32), jnp.zeros((tq, 1), F32))
        for d in reversed(range(per)):
            carry = step(i * per + d, carry, True, d * tk)
        acc, total = lax.fori_loop(0, i, lambda pp, c: blocks(i - 1 - pp, c), carry)
        y_ref[...] = acc
        tot_ref[...] = total

    return pl.pallas_call(
        body, name=name, grid=(H, T // tq),
        in_specs=[pl.BlockSpec((tq, GROUP_DIM), lambda h, i: (i, qc + h)),
                  pl.BlockSpec((T, GROUP_DIM), lambda h, i: (0, kc + h)),
                  pl.BlockSpec((T, GROUP_DIM), lambda h, i: (0, vc + h))],
        out_specs=[pl.BlockSpec((tq, GROUP_DIM), lambda h, i: (i, h)),
                   pl.BlockSpec((None, tq, 1), lambda h, i: (h, i, 0))],
        out_shape=[jax.ShapeDtypeStruct((T, w_b), F32), jax.ShapeDtypeStruct((H, T, 1), F32)],
        compiler_params=_params(2))(z, z, z)


def _sb_backward(name, z, do, total, w_a, w_b):
    T = z.shape[0]
    H = w_b // GROUP_DIM
    tq, tk = _sb_tiles(T)
    per = tq // tk
    qc, kc, vc = _sb_cols(w_a, w_b)
    scale = GROUP_DIM ** -0.5

    def body(q_ref, k_ref, v_ref, do_ref, tot_ref, dq_ref, dkv_ref):
        i = pl.program_id(1)

        @pl.when(i == 0)
        def _():
            dkv_ref[...] = jnp.zeros_like(dkv_ref)

        qs = (q_ref[...] * scale).astype(BF16)
        dob = do_ref[...].astype(BF16)
        upto = _tri(tk, lambda r, c: r <= c)
        before = _tri(tk, lambda r, c: r < c)
        ahead = (lax.broadcasted_iota(jnp.int32, (tq, tk), 1)
                 - lax.broadcasted_iota(jnp.int32, (tq, tk), 0))

        def step(j, carry, masked, row0=0):
            dq, left, e_seen = (c[row0:] for c in carry)
            qr, dor = qs[row0:], dob[row0:]
            k0 = pl.multiple_of(j * tk, tk)
            kj = k_ref[pl.ds(k0, tk), :].astype(BF16)
            vj = v_ref[pl.ds(k0, tk), :].astype(BF16)
            mask = ahead[row0:] < i * tq - k0 if masked else None
            log_beta, log_1m = _sb_scores(qr, kj, mask)
            rest = left - _split_dot(log_1m, upto)
            a = _masked(mask, jnp.exp(log_beta + rest))
            e = a * _dot_nt(dor, vj)
            e_before = e_seen + jnp.dot(e.astype(BF16), before, preferred_element_type=F32)
            beta = jnp.exp(log_beta)
            dz = _masked(mask, e * (1.0 - beta) - beta * e_before).astype(BF16)
            dq = dq + jnp.dot(dz, kj, preferred_element_type=F32)
            dkv_ref[0, pl.ds(k0, tk), :] += _dot_tn(dz, qr)
            dkv_ref[1, pl.ds(k0, tk), :] += _dot_tn(a.astype(BF16), dor)
            left = left - jnp.sum(log_1m, axis=1, keepdims=True)
            e_seen = e_seen + jnp.sum(e, axis=1, keepdims=True)
            return _below(carry, (dq, left, e_seen), row0)

        def blocks(p, c):
            for d in range(per):
                c = step(p * per + d, c, False)
            return c

        carry = (jnp.zeros((tq, GROUP_DIM), F32), tot_ref[...], jnp.zeros((tq, 1), F32))
        carry = lax.fori_loop(0, i, blocks, carry)
        for d in range(per):
            carry = step(i * per + d, carry, True, d * tk)
        dq_ref[...] = (carry[0] * scale).astype(BF16)

    return pl.pallas_call(
        body, name=name, grid=(H, T // tq),
        in_specs=[pl.BlockSpec((tq, GROUP_DIM), lambda h, i: (i, qc + h)),
                  pl.BlockSpec((T, GROUP_DIM), lambda h, i: (0, kc + h)),
                  pl.BlockSpec((T, GROUP_DIM), lambda h, i: (0, vc + h)),
                  pl.BlockSpec((tq, GROUP_DIM), lambda h, i: (i, h)),
                  pl.BlockSpec((None, tq, 1), lambda h, i: (h, i, 0))],
        out_specs=[pl.BlockSpec((tq, GROUP_DIM), lambda h, i: (i, h)),
                   pl.BlockSpec((2, T, GROUP_DIM), lambda h, i: (0, 0, h))],
        out_shape=[jax.ShapeDtypeStruct((T, w_b), BF16), jax.ShapeDtypeStruct((2, T, w_b), F32)],
        compiler_params=_params(2))(z, z, z, do, total)


def _softmax_rows(s):
    m = jnp.max(s, axis=-1, keepdims=True)
    p = jnp.exp(s - m)
    return p / jnp.sum(p, axis=-1, keepdims=True)


def _xattn_forward(name, q, kv):
    T, D = q.shape
    Nm = kv.shape[0]
    dh = D // X_HEADS
    tq = _pick(512, [T])

    def body(q_ref, k_ref, v_ref, o_ref):
        p = _softmax_rows(_dot_nt(q_ref[...], k_ref[...]))
        o_ref[...] = jnp.dot(p.astype(BF16), v_ref[...], preferred_element_type=F32).astype(BF16)

    return pl.pallas_call(
        body, name=name, grid=(T // tq, X_HEADS),
        in_specs=[pl.BlockSpec((tq, dh), lambda i, h: (i, h)),
                  pl.BlockSpec((Nm, dh), lambda i, h: (0, h)),
                  pl.BlockSpec((Nm, dh), lambda i, h: (0, X_HEADS + h))],
        out_specs=pl.BlockSpec((tq, dh), lambda i, h: (i, h)),
        out_shape=jax.ShapeDtypeStruct((T, D), BF16),
        compiler_params=_params(2))(q, kv, kv)


def _xattn_backward(name, q, kv, do):
    T, D = q.shape
    Nm = kv.shape[0]
    dh = D // X_HEADS
    tq = _pick(512, [T])
    scale = dh ** -0.5

    def body(q_ref, k_ref, v_ref, do_ref, dq_ref, dkv_ref):
        @pl.when(pl.program_id(1) == 0)
        def _():
            dkv_ref[...] = jnp.zeros_like(dkv_ref)

        qv, kk, vv, dov = q_ref[...], k_ref[...], v_ref[...], do_ref[...]
        p = _softmax_rows(_dot_nt(qv, kk))
        dp = _dot_nt(dov, vv)
        ds = (p * (dp - jnp.sum(dp * p, axis=-1, keepdims=True))).astype(BF16)
        dq_ref[...] = (jnp.dot(ds, kk, preferred_element_type=F32) * scale).astype(BF16)
        dkv_ref[0] += _dot_tn(ds, qv)
        dkv_ref[1] += _dot_tn(p.astype(BF16), dov)

    blk = pl.BlockSpec((tq, dh), lambda h, i: (i, h))
    return pl.pallas_call(
        body, name=name, grid=(X_HEADS, T // tq),
        in_specs=[blk, pl.BlockSpec((Nm, dh), lambda h, i: (0, h)),
                  pl.BlockSpec((Nm, dh), lambda h, i: (0, X_HEADS + h)), blk],
        out_specs=[blk, pl.BlockSpec((2, Nm, dh), lambda h, i: (0, 0, h))],
        out_shape=[jax.ShapeDtypeStruct((T, D), BF16), jax.ShapeDtypeStruct((2, Nm, D), F32)],
        compiler_params=_params(2))(q, kv, kv, do)


def _position():
    x, y, c = lax.axis_index("x"), lax.axis_index("y"), lax.axis_index("c")
    other_chips = [(1 - x, y), (x, 1 - y), (1 - x, 1 - y)]
    return x, y, c, other_chips


def _hbm_spec():
    return pl.BlockSpec(memory_space=pltpu.HBM)


def _sem_spec():
    return pl.BlockSpec(memory_space=pltpu.SEMAPHORE)


def _split_start(name, arrays, make_copies, n_sems, deps=()):
    n, d = len(arrays), len(deps)

    def body(*refs):
        ins = refs[:n]
        send_sems, recv_sems = refs[n + d], refs[n + d + 1]
        token = refs[-1]
        for cp in make_copies(ins, send_sems, recv_sems):
            cp.start()
        token[...] = jnp.zeros_like(token)

    res = pl.pallas_call(
        body, name=name,
        out_shape=(pltpu.SemaphoreType.DMA((n_sems,)), pltpu.SemaphoreType.DMA((n_sems,)),
                   *[pltpu.HBM(a.shape, a.dtype) for a in arrays],
                   jax.ShapeDtypeStruct((SUBLANE, LANE), F32)),
        in_specs=[_hbm_spec()] * n + [_any_spec()] * d,
        out_specs=(_sem_spec(), _sem_spec(), *[_hbm_spec()] * n,
                   pl.BlockSpec(memory_space=pltpu.VMEM)),
        input_output_aliases={i: 2 + i for i in range(n)},
        compiler_params=pltpu.CompilerParams(
            has_side_effects=pltpu.SideEffectType.DATAFLOW_SIDE_EFFECTING),
    )(*[pltpu.with_memory_space_constraint(a, pltpu.HBM) for a in arrays], *deps)
    return res[0], res[1], list(res[2:2 + n]), res[-1]


def _split_wait(name, arrays, send_sems, recv_sems, after, make_copies):
    n = len(arrays)
    after = list(after) if isinstance(after, (list, tuple)) else [after]

    def body(*refs):
        ins = refs[:n]
        send_ref, recv_ref = refs[n], refs[n + 1]
        for cp in make_copies(ins, send_ref, recv_ref):
            cp.wait_send()
            cp.wait_recv()

    return pl.pallas_call(
        body, name=name,
        out_shape=tuple(pltpu.HBM(a.shape, a.dtype) for a in arrays),
        in_specs=[_hbm_spec()] * n + [_sem_spec(), _sem_spec()] + [_any_spec()] * len(after),
        out_specs=tuple(_hbm_spec() for _ in arrays),
        input_output_aliases={i: i for i in range(n)},
        compiler_params=pltpu.CompilerParams(
            has_side_effects=pltpu.SideEffectType.DATAFLOW_SIDE_EFFECTING),
    )(*arrays, send_sems, recv_sems, *after)


def _gather_copies(refs, send_sems, recv_sems):
    x, y, c, chips = _position()
    me = 2 * x + y
    copies = []
    for i, ref in enumerate(refs):
        rows = ref.shape[1] // 2
        piece = ref.at[me, pl.ds(c * rows, rows), :]
        for j, (px, py) in enumerate(chips):
            copies.append(pltpu.make_async_remote_copy(
                src_ref=piece, dst_ref=piece, send_sem=send_sems.at[3 * i + j],
                recv_sem=recv_sems.at[3 * i + j], device_id=(px, py, c), device_id_type=MESH))
    return copies


def _near_copies(refs, send_sems, recv_sems):
    x, y, c, chips = _position()
    me = 2 * x + y
    copies = []
    for i, ref in enumerate(refs):
        rows = ref.shape[1] // 2
        piece = ref.at[me, pl.ds(c * rows, rows), :]
        for j, (px, py) in enumerate(chips[:2]):
            copies.append(pltpu.make_async_remote_copy(
                src_ref=piece, dst_ref=piece, send_sem=send_sems.at[2 * i + j],
                recv_sem=recv_sems.at[2 * i + j], device_id=(px, py, c), device_id_type=MESH))
    return copies


def _relay_copies(refs, send_sems, recv_sems):
    x, y, c, chips = _position()
    copies = []
    for i, ref in enumerate(refs):
        rows = ref.shape[1] // 4
        for j, (px, py) in enumerate(chips[:2]):
            ox, oy = chips[1 - j]
            piece = ref.at[2 * ox + oy, pl.ds((2 * c + j) * rows, rows), :]
            copies.append(pltpu.make_async_remote_copy(
                src_ref=piece, dst_ref=piece, send_sem=send_sems.at[2 * i + j],
                recv_sem=recv_sems.at[2 * i + j], device_id=(px, py, c), device_id_type=MESH))
    return copies


def _share_copies(refs, send_sems, recv_sems):
    x, y, c, _ = _position()
    copies = []
    for i, ref in enumerate(refs):
        rows = ref.shape[0] // 2
        mine = ref.at[pl.ds(c * rows, rows), :]
        copies.append(pltpu.make_async_remote_copy(
            src_ref=mine, dst_ref=mine, send_sem=send_sems.at[i], recv_sem=recv_sems.at[i],
            device_id=(x, y, 1 - c), device_id_type=MESH))
    return copies


def _scatter_copies(refs, send_sems, recv_sems):
    x, y, c, chips = _position()
    n = len(refs) // 2
    copies = []
    for i in range(n):
        for j, (px, py) in enumerate(chips):
            copies.append(pltpu.make_async_remote_copy(
                src_ref=refs[i].at[2 * px + py], dst_ref=refs[n + i].at[j],
                send_sem=send_sems.at[3 * i + j], recv_sem=recv_sems.at[3 * i + j],
                device_id=(px, py, c), device_id_type=MESH))
    return copies


def _cast_own(name, place, shard):
    rows, cols = shard.shape
    tr = _block_rows(rows, cols)

    def body(place_ref, w_ref, o_ref):
        o_ref[...] = w_ref[...].astype(BF16)

    grid_spec = pltpu.PrefetchScalarGridSpec(
        num_scalar_prefetch=1, grid=(rows // tr,),
        in_specs=[pl.BlockSpec((tr, cols), lambda r, pr: (r, 0))],
        out_specs=pl.BlockSpec((None, tr, cols), lambda r, pr: (pr[0], r, 0)))
    return pl.pallas_call(
        body, name=name, grid_spec=grid_spec,
        out_shape=jax.ShapeDtypeStruct((N_CHIPS, rows, cols), BF16),
        compiler_params=_params(1))(place, shard)


def _forward_to_sibling(name, arrays, deps=(), which=(0, 1, 2)):
    n = len(arrays)

    def body(*refs):
        ins = refs[:n]
        send_sems, recv_sems = refs[-2:]
        x, y, c, chips = _position()
        chips = [(j, chips[j]) for j in which]
        sends = []
        for i in range(n):
            rows = ins[i].shape[1] // 2
            for j, (px, py) in chips:
                piece = ins[i].at[2 * px + py, pl.ds(c * rows, rows), :]
                cp = pltpu.make_async_remote_copy(
                    src_ref=piece, dst_ref=piece, send_sem=send_sems.at[i, j],
                    recv_sem=recv_sems.at[i, j], device_id=(x, y, 1 - c), device_id_type=MESH)
                cp.start()
                sends.append(cp)
        for i in range(n):
            rows = ins[i].shape[1] // 2
            for j, (px, py) in chips:
                piece = ins[i].at[2 * px + py, pl.ds((1 - c) * rows, rows), :]
                pltpu.make_async_remote_copy(
                    src_ref=piece, dst_ref=piece, send_sem=send_sems.at[i, j],
                    recv_sem=recv_sems.at[i, j], device_id=(x, y, 1 - c),
                    device_id_type=MESH).wait_recv()
        for cp in sends:
            cp.wait_send()

    return pl.pallas_call(
        body, name=name,
        in_specs=[_any_spec()] * (n + len(deps)), out_specs=[_any_spec()] * n,
        out_shape=[jax.ShapeDtypeStruct(a.shape, a.dtype) for a in arrays],
        input_output_aliases={i: i for i in range(n)},
        scratch_shapes=[pltpu.SemaphoreType.DMA((n, 3))] * 2,
    )(*arrays, *deps)


def _swap_copies(refs, send_sems, recv_sems):
    x, y, c, _ = _position()
    n = len(refs) // 2
    copies = []
    for i in range(n):
        rows = refs[i].shape[1] // 2
        copies.append(pltpu.make_async_remote_copy(
            src_ref=refs[i].at[:, pl.ds((1 - c) * rows, rows), :], dst_ref=refs[n + i],
            send_sem=send_sems.at[i], recv_sem=recv_sems.at[i],
            device_id=(x, y, 1 - c), device_id_type=MESH))
    return copies


def _small_copies(refs, send_sems, recv_sems):
    packed, slots = refs
    x, y, c, _ = _position()
    me = 4 * x + 2 * y + c
    copies = []
    for r in range(1, N_DEV):
        peer = (x ^ ((r >> 2) & 1), y ^ ((r >> 1) & 1), c ^ (r & 1))
        copies.append(pltpu.make_async_remote_copy(
            src_ref=packed, dst_ref=slots.at[me], send_sem=send_sems.at[r - 1],
            recv_sem=recv_sems.at[r - 1], device_id=peer, device_id_type=MESH))
    return copies


def _block_rows(rows, cols, itemsize=4, target=1 << 20):
    return _pick(max(BF16_ROWS, target // (cols * itemsize)), [rows], unit=BF16_ROWS)


def _pair_sum(name, place, grad, received):
    P, rows, cols = received.shape
    tr = _block_rows(rows, cols, itemsize=2, target=2 << 20)
    nb = rows // tr

    def body(place_ref, g_ref, r_ref, o_ref):
        o_ref[...] = (g_ref[...].astype(F32) + r_ref[...].astype(F32)).astype(BF16)

    def panel(j, pr):
        return pr[0] ^ jnp.where(j == 2, 3, 2 - j)

    grid_spec = pltpu.PrefetchScalarGridSpec(
        num_scalar_prefetch=1, grid=(P - 1, nb),
        in_specs=[pl.BlockSpec((None, tr, cols),
                               lambda j, r, pr: (panel(j, pr), pr[1] * nb + r, 0)),
                  pl.BlockSpec((None, tr, cols), lambda j, r, pr: (panel(j, pr), r, 0))],
        out_specs=pl.BlockSpec((None, tr, cols), lambda j, r, pr: (panel(j, pr), r, 0)))
    return pl.pallas_call(
        body, name=name, grid_spec=grid_spec,
        out_shape=jax.ShapeDtypeStruct(received.shape, BF16),
        compiler_params=_params(2))(place, grad, received)


def _final_sum(name, place, grad, received, from_chips):
    _, rows, cols = received.shape
    tr = _block_rows(rows, cols, target=2 << 20)
    nb = rows // tr

    def body(place_ref, g_ref, r_ref, c_ref, o_ref):
        acc = g_ref[...].astype(F32) + r_ref[...].astype(F32)
        for j in range(3):
            acc = acc + c_ref[j].astype(F32)
        o_ref[...] = acc

    grid_spec = pltpu.PrefetchScalarGridSpec(
        num_scalar_prefetch=1, grid=(nb,),
        in_specs=[pl.BlockSpec((None, tr, cols), lambda r, pr: (pr[0], pr[1] * nb + r, 0)),
                  pl.BlockSpec((None, tr, cols), lambda r, pr: (pr[0], r, 0)),
                  pl.BlockSpec((3, tr, cols), lambda r, pr: (0, r, 0))],
        out_specs=pl.BlockSpec((tr, cols), lambda r, pr: (pr[1] * nb + r, 0)))
    return pl.pallas_call(
        body, name=name, grid_spec=grid_spec,
        out_shape=jax.ShapeDtypeStruct((2 * rows, cols), F32),
        compiler_params=_params(1))(place, grad, received, from_chips)


def _sum_devices(name, me, gathered, own):
    n_dev, rows, cols = gathered.shape
    tr = _pick(256, [rows])

    def body(me_ref, g_ref, own_ref, o_ref):
        term = lambda d: jnp.where(me_ref[0] == d, own_ref[...], g_ref[d])
        acc = term(0)
        for d in range(1, n_dev):
            acc = acc + term(d)
        o_ref[...] = acc

    grid_spec = pltpu.PrefetchScalarGridSpec(
        num_scalar_prefetch=1, grid=(rows // tr,),
        in_specs=[pl.BlockSpec((n_dev, tr, cols), lambda r, me_ref: (0, r, 0)),
                  pl.BlockSpec((tr, cols), lambda r, me_ref: (r, 0))],
        out_specs=pl.BlockSpec((tr, cols), lambda r, me_ref: (r, 0)))
    return pl.pallas_call(
        body, name=name, grid_spec=grid_spec,
        out_shape=jax.ShapeDtypeStruct((rows, cols), F32),
        compiler_params=_params(1))(me, gathered, own)


def _adamw(name, w, g, m, v):
    rows, cols = w.shape
    tr = _block_rows(rows, cols)
    c1 = 1.0 / (1.0 - ADAM_B1 ** ADAM_STEP)
    c2 = 1.0 / (1.0 - ADAM_B2 ** ADAM_STEP)

    def body(w_ref, g_ref, m_ref, v_ref, go_ref, d_ref, nm_ref, nv_ref):
        gv = g_ref[...]
        go_ref[...] = gv
        nm = ADAM_B1 * m_ref[...] + (1.0 - ADAM_B1) * gv
        nv = ADAM_B2 * v_ref[...] + (1.0 - ADAM_B2) * (gv * gv)
        nm_ref[...] = nm
        nv_ref[...] = nv
        d_ref[...] = -ADAM_LR * ((nm * c1) / (jnp.sqrt(nv * c2) + ADAM_EPS) + ADAM_WD * w_ref[...])

    blk = pl.BlockSpec((tr, cols), lambda r: (r, 0))
    shape = jax.ShapeDtypeStruct((rows, cols), F32)
    return pl.pallas_call(
        body, name=name, grid=(rows // tr,), in_specs=[blk] * 4, out_specs=[blk] * 4,
        out_shape=[shape] * 4, compiler_params=_params(1))(w, g, m, v)


BIG = ("ffn1_w_in", "ffn1_w_out", "w_mix_in", "w_mix_out", "w_cq", "w_ckv", "w_co",
       "ffn2_w_in", "ffn2_w_out")
BIG_KIND = {"ffn1_w_in": "c", "ffn1_w_out": "r", "w_mix_in": "c", "w_mix_out": "r", "w_cq": "r",
            "w_ckv": "c", "w_co": "r", "ffn2_w_in": "c", "ffn2_w_out": "r"}
GATHER_GROUPS = (("ffn1_in", ("ffn1_w_in",)), ("ffn1_out", ("ffn1_w_out",)),
                 ("mix_in", ("w_mix_in",)), ("mix_out", ("w_mix_out",)),
                 ("cross", ("w_cq", "w_ckv", "w_co")),
                 ("ffn2_in", ("ffn2_w_in",)), ("ffn2_out", ("ffn2_w_out",)))
GATHER_AFTER = (("ffn1_in", None), ("ffn1_out", "ffn1_in"), ("mix_in", "ffn1_out"),
                ("mix_out", "mix_in"), ("cross", "mix_in"), ("ffn2_in", "mix_in"),
                ("ffn2_out", "ffn2_in"))
RELAYED = ("ffn1_in", "ffn2_in")
TAIL_STAGES = (("sum", "ffn2"), ("sum", "cross"), ("sum", "mix"), ("sum", "ffn1_out"),
               ("update", "ffn2"), ("update", "cross"), ("sum", "ffn1_in"), ("update", "mix"),
               ("update", "ffn1_out"), ("update", "ffn1_in"))
SMALL = ("ffn1_norm", "mix_norm", "ln_v_gain", "ln_v_bias", "spatial_w", "spatial_b", "gnorm_a",
         "gnorm_b", "cross_norm", "mem_norm", "ffn2_norm", "final_norm")
WEIGHTS = ("ffn1_norm", "ffn1_w_in", "ffn1_w_out", "mix_norm", "w_mix_in", "ln_v_gain",
           "ln_v_bias", "spatial_w", "spatial_b", "gnorm_a", "gnorm_b", "w_mix_out", "cross_norm",
           "mem_norm", "w_cq", "w_ckv", "w_co", "ffn2_norm", "ffn2_w_in", "ffn2_w_out",
           "final_norm")


def _pack(arrays):
    return jnp.concatenate([a.reshape(-1, LANE) for a in arrays], axis=0)


def _unpack(packed, like):
    out, row = [], 0
    for a in like:
        rows = a.size // LANE
        out.append(packed[row:row + rows].reshape(a.shape))
        row += rows
    return out


def _local_step(x, mem, target, small, place, weights_of, start_tokens, grads_ready,
                grads_flush):
    T, D = x.shape
    vec = lambda name: small[name].reshape(1, -1)
    w_a = small["ln_v_gain"].size
    w_b = small["gnorm_b"].size
    G = w_a // GROUP_DIM
    w_s = small["spatial_w"].reshape(G, SGU_BLOCK, SGU_BLOCK)
    b_t = small["spatial_b"].reshape(G, SGU_BLOCK).T

    h1, ffn1_saved = _ffn_forward("ffn1", x, vec("ffn1_norm"), weights_of, place,
                                  deps=start_tokens)
    n2 = _rmsnorm_fwd("mix_norm", h1, vec("mix_norm"))
    big = weights_of("mix_in", n2)
    (z,) = _matmul("mix_in", Mat(n2), big["w_mix_in"], "nn", [("c", 1, F32)], tm=2048, tn=256)
    z = z[0]
    y = _sgu_forward("sgu", z, vec("ln_v_gain"), vec("ln_v_bias"), w_s, b_t, vec("gnorm_a"), D)
    yb, sb_total = _sb_forward("stickbreak", z, w_a, w_b)
    y = _rmsnorm_fwd("gnorm_b", yb, vec("gnorm_b"), into=y, col=w_a // w_b)

    def add_res(acc, ex, out):
        out[0][...] = ex[0][...] + acc

    big.update(weights_of("mix_out", y))
    (h2,) = _matmul("mix_out", Mat(y), big["w_mix_out"], "nn", [("c", 1, F32)],
                    tm=1024, tn=1024, extras=[Mat(h1)], epi=add_res)
    h2 = h2[0]
    n3 = _rmsnorm_fwd("cross_norm", h2, vec("cross_norm"))
    memn = _rmsnorm_fwd("mem_norm", mem, vec("mem_norm"))
    big.update(weights_of("cross", n3))
    x_scale = (D // X_HEADS) ** -0.5

    def scaled(acc, ex, out):
        out[0][...] = (acc * x_scale).astype(BF16)

    (q,) = _matmul("cross_q", Mat(n3), big["w_cq"], "nn", [("c", 1, BF16)],
                   tm=1024, tn=1024, epi=scaled)
    (kv,) = _matmul("cross_kv", Mat(memn), big["w_ckv"], "nn", [("c", 1, BF16)], tm=256, tn=1024)
    q, kv = q[0], kv[0]
    o = _xattn_forward("cross_attn", q, kv)
    (h3,) = _matmul("cross_out", Mat(o), big["w_co"], "nn", [("c", 1, F32)],
                    tm=1024, tn=1024, extras=[Mat(h2)], epi=add_res)
    h3 = h3[0]
    h4, ffn2_saved = _ffn_forward("ffn2", h3, vec("ffn2_norm"), weights_of, place)

    gs = {}
    loss_tile, dh4, dh4_bf, gs["final_norm"] = _loss_head("loss_head", h4, vec("final_norm"), target)
    dh3, dh3_bf, gs["ffn2_norm"] = _ffn_backward(
        "ffn2", h3, vec("ffn2_norm"), ffn2_saved, dh4, dh4_bf, grads_ready, grads_flush)

    (do,) = _matmul("cross_do", Mat(dh3_bf), big["w_co"], "nt", [("c", 1, BF16)], tm=512, tn=2048)
    (dw_co,) = _matmul("cross_dwo", Mat(o), Mat(dh3_bf), "tn", [("r", N_CHIPS, BF16)],
                       tm=512, tn=1024)
    dq, dkv = _xattn_backward("cross_attn_bwd", q, kv, do[0])
    (dw_cq,) = _matmul("cross_dwq", Mat(n3), Mat(dq), "tn", [("r", N_CHIPS, BF16)],
                       tm=512, tn=1024)
    (dw_ckv,) = _matmul("cross_dwkv", Mat(memn), Mat(dkv), "tn", [("c", N_CHIPS, BF16)],
                        tm=1024, tn=1024)
    token = grads_ready("cross", {"w_cq": dw_cq, "w_ckv": dw_ckv, "w_co": dw_co})
    dq = _tie("cross_dq_after_swap", dq, [token])
    (dn3,) = _matmul("cross_dn", Mat(dq), big["w_cq"], "nt", [("c", 1, F32)], tm=512, tn=2048)
    (dmemn,) = _matmul("cross_dmem", Mat(dkv), big["w_ckv"], "nt", [("c", 1, F32)],
                       tm=256, tn=1024, tk=1024)
    (gs["mem_norm"],) = _rmsnorm_bwd("mem_dnorm", mem, vec("mem_norm"), dmemn[0], want_dx=False)
    dn3 = _tie("cross_dn_after_scatter", dn3, [grads_flush("cross", gs["mem_norm"])])
    dh2, dh2_bf, gs["cross_norm"] = _rmsnorm_bwd("cross_dnorm", h2, vec("cross_norm"), dn3[0],
                                                 dres=dh3)

    (dy,) = _matmul("mix_dy", Mat(dh2_bf), big["w_mix_out"], "nt", [("c", 1, F32)], tm=512, tn=2048)
    dy = dy[0]
    (dw_mix_out,) = _matmul("mix_dwout", Mat(y), Mat(dh2_bf), "tn", [("r", N_CHIPS, BF16)],
                            tm=512, tn=1024)
    dza, gs["ln_v_gain"], gs["ln_v_bias"], gs["spatial_w"], db, gs["gnorm_a"] = _sgu_backward(
        "sgu_bwd", z, dy, vec("ln_v_gain"), vec("ln_v_bias"), w_s, b_t, vec("gnorm_a"))
    gs["spatial_b"] = db.reshape(G, SGU_BLOCK)
    dob, gs["gnorm_b"] = _rmsnorm_bwd("gnorm_b_bwd", yb, vec("gnorm_b"), dy, dn_col=w_a // w_b,
                                      want_bf16=False)
    dqb, dkvb = _sb_backward("stickbreak_bwd", z, dob, sb_total, w_a, w_b)
    dz = jnp.concatenate([dza, dqb, dkvb[0].astype(BF16), dkvb[1].astype(BF16)], axis=1)
    (dw_mix_in,) = _matmul("mix_dwin", Mat(n2), Mat(dz), "tn", [("c", N_CHIPS, BF16)],
                           tm=1024, tn=1280)
    token = grads_ready("mix", {"w_mix_in": dw_mix_in, "w_mix_out": dw_mix_out})
    dz = _tie("mix_dz_after_swap", dz, [token])
    (dn2,) = _matmul("mix_dn", Mat(dz), big["w_mix_in"], "nt", [("c", 1, F32)],
                     tm=1024, tn=1024, tk=1280)
    dn2 = _tie("mix_dn_after_scatter", dn2, [grads_flush("mix", dn2)])
    dh1, dh1_bf, gs["mix_norm"] = _rmsnorm_bwd("mix_dnorm", h1, vec("mix_norm"), dn2[0], dres=dh2)

    dx, _, gs["ffn1_norm"] = _ffn_backward(
        "ffn1", x, vec("ffn1_norm"), ffn1_saved, dh1, dh1_bf, grads_ready, grads_flush,
        early_out=True)
    gs = {k: g.reshape(small[k].shape) for k, g in gs.items()}
    return loss_tile, dx, gs


def kernel(x, mem, ffn1_norm, ffn1_w_in, ffn1_w_out, mix_norm, w_mix_in, ln_v_gain, ln_v_bias, spatial_w, spatial_b, gnorm_a, gnorm_b, w_mix_out, cross_norm, mem_norm, w_cq, w_ckv, w_co, ffn2_norm, ffn2_w_in, ffn2_w_out, final_norm, loss_target, m_ffn1_norm, m_ffn1_w_in, m_ffn1_w_out, m_mix_norm, m_w_mix_in, m_ln_v_gain, m_ln_v_bias, m_spatial_w, m_spatial_b, m_gnorm_a, m_gnorm_b, m_w_mix_out, m_cross_norm, m_mem_norm, m_w_cq, m_w_ckv, m_w_co, m_ffn2_norm, m_ffn2_w_in, m_ffn2_w_out, m_final_norm, v_ffn1_norm, v_ffn1_w_in, v_ffn1_w_out, v_mix_norm, v_w_mix_in, v_ln_v_gain, v_ln_v_bias, v_spatial_w, v_spatial_b, v_gnorm_a, v_gnorm_b, v_w_mix_out, v_cross_norm, v_mem_norm, v_w_cq, v_w_ckv, v_w_co, v_ffn2_norm, v_ffn2_w_in, v_ffn2_w_out, v_final_norm):
    given = dict(locals())
    w = {k: given[k] for k in WEIGHTS}
    m = {k: given["m_" + k] for k in WEIGHTS}
    v = {k: given["v_" + k] for k in WEIGHTS}

    cx, cy, cc = lax.axis_index("x"), lax.axis_index("y"), lax.axis_index("c")
    place = jnp.stack([2 * cx + cy, cc]).astype(jnp.int32)

    names_of = dict(GATHER_GROUPS)
    own = {g: [_cast_own(f"cast_{k}", place, w[k][0]) for k in names] for g, names in GATHER_GROUPS}
    gathers = {}

    def start_gather(group, deps):
        first_hop = _near_copies if group in RELAYED else _gather_copies
        n_sems = (2 if group in RELAYED else 3) * len(own[group])
        send, recv, arrays, token = _split_start(f"gather_start_{group}", own[group],
                                                 first_hop, n_sems, deps)
        gathers[group] = (send, recv, arrays)
        return token

    start_tokens = [start_gather(g, ()) for g, after in GATHER_AFTER if after is None]
    start_tokens += [a for g, after in GATHER_AFTER if after is not None for a in own[g]]

    def weights_of(group, after):
        send, recv, arrays = gathers[group]
        as_mats = lambda arrs: {k: Mat(a, BIG_KIND[k]) for k, a in zip(names_of[group], arrs)}
        if group not in RELAYED:
            arrays = _split_wait(f"gather_wait_{group}", arrays, send, recv, after, _gather_copies)
            tokens = [start_gather(g, (arrays[0],)) for g, a in GATHER_AFTER if a == group]
            return as_mats(_forward_to_sibling(f"gather_forward_{group}", list(arrays), tokens))
        arrays = _split_wait(f"gather_wait_{group}", arrays, send, recv, after, _near_copies)
        send, recv, arrays, token = _split_start(f"gather_relay_{group}", list(arrays),
                                                 _relay_copies, 2 * len(arrays))
        tokens = [token] + [start_gather(g, (arrays[0],)) for g, a in GATHER_AFTER if a == group]
        arrays = _forward_to_sibling(f"gather_forward_{group}", list(arrays), tokens, which=(0, 1))

        def finish(after):
            arrs = _split_wait(f"gather_relay_wait_{group}", arrays, send, recv, after,
                               _relay_copies)
            return as_mats(_forward_to_sibling(f"gather_forward_diag_{group}", list(arrs),
                                               which=(2,)))

        return {**as_mats(arrays), "finish": finish}

    swaps, scatters = {}, {}

    def grads_ready(group, partial):
        names = list(partial)
        grads_ = [partial[k] for k in names]
        lands = [lax.empty((g.shape[0], g.shape[1] // 2, g.shape[2]), g.dtype) for g in grads_]
        send, recv, arrays, token = _split_start(f"swap_start_{group}", grads_ + lands,
                                                 _swap_copies, len(names))
        swaps[group] = (names, send, recv, arrays)
        return token

    def grads_flush(group, after):
        names, send, recv, arrays = swaps[group]
        arrays = _split_wait(f"swap_wait_{group}", arrays, send, recv, after, _swap_copies)
        grads_, from_sibling = arrays[:len(names)], arrays[len(names):]
        sums = [_pair_sum(f"pair_sum_{k}", place, g, r)
                for k, g, r in zip(names, grads_, from_sibling)]
        lands = [lax.empty((3,) + s.shape[1:], s.dtype) for s in sums]
        send, recv, arrays, token = _split_start(f"scatter_start_{group}", sums + lands,
                                                 _scatter_copies, 3 * len(names))
        scatters[group] = (names, grads_, from_sibling, send, recv, arrays)
        return token

    small = {k: w[k] for k in SMALL}
    loss_tile, grad_x, gs = _local_step(x[0], mem[0], loss_target[0], small, place, weights_of,
                                        start_tokens, grads_ready, grads_flush)

    packed = _pack([gs[k] for k in SMALL] + [loss_tile])
    slots = jnp.zeros((N_DEV,) + packed.shape, packed.dtype)
    small_send, small_recv, small_arrays, _ = _split_start(
        "small_start", [packed, slots], _small_copies, N_DEV - 1)

    grad, delta, new_m, new_v = {}, {}, {}, {}
    shares = {}
    after = [grad_x]
    for stage, group in TAIL_STAGES:
        if stage == "sum":
            names, grads_, from_sibling, send, recv, arrays = scatters[group]
            arrays = _split_wait(f"scatter_wait_{group}", arrays, send, recv, after,
                                 _scatter_copies)
            from_chips = arrays[len(names):]
            shards = [_final_sum(f"final_sum_{k}", place, g, r, f)
                      for k, g, r, f in zip(names, grads_, from_sibling, from_chips)]
            send, recv, shards, token = _split_start(f"share_start_{group}", shards,
                                                     _share_copies, len(names))
            shares[group] = (names, send, recv, shards)
            after = [token]
        else:
            names, send, recv, shards = shares[group]
            shards = _split_wait(f"share_wait_{group}", shards, send, recv, after, _share_copies)
            after = []
            for k, g_ in zip(names, shards):
                g_, d_, m_, v_ = _adamw(f"adamw_{k}", w[k][0], g_, m[k][0], v[k][0])
                grad[k], delta[k], new_m[k], new_v[k] = g_[None], d_[None], m_[None], v_[None]
                after.append(v_)

    packed, slots = _split_wait("small_wait", small_arrays, small_send, small_recv, after,
                                _small_copies)
    me = (4 * cx + 2 * cy + cc).astype(jnp.int32).reshape(1)
    total = _sum_devices("sum_small", me, slots, packed)
    n_small = total.shape[0] - SUBLANE
    loss = total[n_small, 0]
    small_g = total[:n_small]
    g_s, d_s, m_s, v_s = _adamw("adamw_small", _pack([w[k] for k in SMALL]), small_g,
                                _pack([m[k] for k in SMALL]), _pack([v[k] for k in SMALL]))
    like = [w[k] for k in SMALL]
    for k, g_, d_, m_, v_ in zip(SMALL, _unpack(g_s, like), _unpack(d_s, like),
                                 _unpack(m_s, like), _unpack(v_s, like)):
        grad[k], delta[k], new_m[k], new_v[k] = g_, d_, m_, v_

    return (loss, grad_x[None], *[grad[k] for k in WEIGHTS], *[delta[k] for k in WEIGHTS],
            *[new_m[k] for k in WEIGHTS], *[new_v[k] for k in WEIGHTS])
```

```python
import functools
import math

import jax
import jax.numpy as jnp
from jax import lax
from jax.experimental import pallas as pl
from jax.experimental.pallas import tpu as pltpu

F32 = jnp.float32
BF16 = jnp.bfloat16
MESH = pl.DeviceIdType.MESH

EPS = 1e-6
CHUNK = 64
SGU_BLOCK = 128
GROUP_DIM = 128
X_HEADS = 4
N_CHIPS = 4
N_DEV = 8
LANE = 128
SUBLANE = 8
BF16_ROWS = 16

ADAM_LR = 0.001
ADAM_B1 = 0.9
ADAM_B2 = 0.999
ADAM_EPS = 1e-08
ADAM_WD = 0.01
ADAM_STEP = 10

V7X_VMEM_BYTES = 64 << 20
VMEM_LIMIT = V7X_VMEM_BYTES - (8 << 20)


def _params(n_grid):
    return pltpu.CompilerParams(dimension_semantics=("arbitrary",) * n_grid,
                                vmem_limit_bytes=VMEM_LIMIT)


def _pick(pref, dims, unit=None):
    g = functools.reduce(math.gcd, dims)
    if unit is None:
        unit = LANE if g % LANE == 0 else SUBLANE
    cands = [d for d in range(unit, g + 1, unit) if g % d == 0] or [g]
    return min(cands, key=lambda d: abs(math.log(d / pref)))


def _any_spec():
    return pl.BlockSpec(memory_space=pl.ANY)


class Mat:
    def __init__(self, arr, kind="c"):
        if arr.ndim == 2:
            arr = arr[None]
        self.arr, self.kind = arr, kind
        self.P, self.prow, self.pcol = arr.shape
        self.rows = self.prow * (self.P if kind == "r" else 1)
        self.cols = self.pcol * (self.P if kind == "c" else 1)
        self.dtype = arr.dtype

    def spec(self, tr, tc, rc_fn):
        if self.kind == "c":
            per = self.pcol // tc
            assert per * tc == self.pcol, (self.pcol, tc)

            def imap(*g):
                i, j = rc_fn(*g)
                return (j // per, i, j % per)
        else:
            per = self.prow // tr
            assert per * tr == self.prow, (self.prow, tr)

            def imap(*g):
                i, j = rc_fn(*g)
                return (i // per, i % per, j)
        return pl.BlockSpec((None, tr, tc), imap)

    def two_d(self):
        assert self.P == 1
        return self.arr[0]


def _out_mat(kind, P, rows, cols, dtype):
    shape = (P, rows, cols // P) if kind == "c" else (P, rows // P, cols)
    return jax.ShapeDtypeStruct(shape, dtype)


def _matmul(name, A, B, mode, outs, *, tm=1024, tn=1024, tk=2048, extras=(), epi=None):
    if mode == "nn":
        M, K, N = A.rows, A.cols, B.cols
        assert B.rows == K
    elif mode == "nt":
        M, K, N = A.rows, A.cols, B.rows
        assert B.cols == K
    else:
        K, M, N = A.rows, A.cols, B.cols
        assert B.rows == K
    mdims, ndims, kdims = [M], [N], [K]
    whole_b = mode == "nn" and B.kind == "r" and B.P > 1 and K <= tk
    whole_bt = mode == "nt" and B.kind == "r" and B.P > 1 and N <= tn
    if whole_b:
        kdims.append(A.pcol)
        ndims.append(B.pcol)
    elif whole_bt:
        kdims += [A.pcol, B.pcol]
    elif mode == "tn":
        assert A.kind == "c" and B.kind == "c"
        mdims.append(A.pcol)
        ndims.append(B.pcol)
    else:
        (mdims if A.kind == "r" else kdims).append(A.prow if A.kind == "r" else A.pcol)
        if mode == "nn":
            (kdims if B.kind == "r" else ndims).append(B.prow if B.kind == "r" else B.pcol)
        else:
            (ndims if B.kind == "r" else kdims).append(B.prow if B.kind == "r" else B.pcol)
    for o in list(outs) + list(extras):
        if isinstance(o, Mat):
            (mdims if o.kind == "r" else ndims).append(o.prow if o.kind == "r" else o.pcol)
        elif isinstance(o[0], str):
            (mdims if o[0] == "r" else ndims).append((M if o[0] == "r" else N) // o[1])
    tm, tn = _pick(tm, mdims), _pick(tn, ndims)
    tk = K if mode == "tn" else _pick(tk, kdims)
    nk = K // tk
    grid = (M // tm, N // tn, nk)

    if mode == "tn":
        a_spec = A.spec(K, tm, lambda m, n, k: (0, m))
        b_spec = B.spec(K, tn, lambda m, n, k: (0, n))
    else:
        a_spec = A.spec(tm, tk, lambda m, n, k: (m, k))
        if whole_b:
            b_spec = pl.BlockSpec((B.P, B.prow, tn), lambda m, n, k: (0, 0, n))
        elif whole_bt:
            b_spec = pl.BlockSpec((B.P, B.prow, tk), lambda m, n, k: (0, 0, k))
        elif mode == "nn":
            b_spec = B.spec(tk, tn, lambda m, n, k: (k, n))
        else:
            b_spec = B.spec(tn, tk, lambda m, n, k: (n, k))

    def mn_spec(o):
        if isinstance(o, Mat):
            return o.spec(tm, tn, lambda m, n, k: (m, n))
        if isinstance(o[0], str):
            kind, P = o[0], o[1]
            fake = Mat.__new__(Mat)
            fake.kind, fake.P = kind, P
            fake.prow = M // P if kind == "r" else M
            fake.pcol = N // P if kind == "c" else N
            return Mat.spec(fake, tm, tn, lambda m, n, k: (m, n))
        return o[1](tm, tn)

    out_shapes = tuple(_out_mat(o[0], o[1], M, N, o[2]) if isinstance(o[0], str) else o[0]
                       for o in outs)
    out_specs = tuple(mn_spec(o) for o in outs)
    extra_arrays = tuple(e.arr if isinstance(e, Mat) else e[0] for e in extras)
    extra_specs = tuple(mn_spec(e) for e in extras)
    n_ex, n_out = len(extras), len(outs)
    tt = _pick(256, [tm])
    dims = (((1,), (1 if mode == "nt" else 0,)), ((), ()))

    def body(*refs):
        a_ref, b_ref = refs[:2]
        ex_refs = refs[2:2 + n_ex]
        out_refs = refs[2 + n_ex:2 + n_ex + n_out]
        scratch = refs[2 + n_ex + n_out:]
        if mode == "tn":
            at_ref = scratch[0]

            @pl.when(pl.program_id(1) == 0)
            def _():
                for c0 in range(0, tm, tt):
                    at_ref[c0:c0 + tt, :] = a_ref[:, c0:c0 + tt].astype(F32).T.astype(BF16)

            lhs = at_ref[...]
        else:
            lhs = a_ref[...].astype(BF16)
        rhs = b_ref[...]
        if whole_b or whole_bt:
            rhs = rhs.reshape(B.P * B.prow, rhs.shape[-1])
        part = lax.dot_general(lhs, rhs.astype(BF16), dims, preferred_element_type=F32)

        def finish(acc):
            if epi is None:
                out_refs[0][...] = acc.astype(out_refs[0].dtype)
            else:
                epi(acc, ex_refs, out_refs)

        if nk == 1:
            finish(part)
        else:
            acc_ref = scratch[0]
            k = pl.program_id(2)

            @pl.when(k == 0)
            def _():
                acc_ref[...] = part

            @pl.when(k > 0)
            def _():
                acc_ref[...] += part

            @pl.when(k == nk - 1)
            def _():
                finish(acc_ref[...])

    scratch_shapes = []
    if mode == "tn":
        scratch_shapes.append(pltpu.VMEM((tm, K), BF16))
    elif nk > 1:
        scratch_shapes.append(pltpu.VMEM((tm, tn), F32))
    res = pl.pallas_call(
        body, name=name, grid=grid,
        in_specs=[a_spec, b_spec, *extra_specs], out_specs=out_specs, out_shape=out_shapes,
        scratch_shapes=scratch_shapes, compiler_params=_params(3),
    )(A.arr, B.arr, *extra_arrays)
    return res


def _row_tile(T):
    return _pick(256, [T])


def _tie(name, x, deps):
    def body(*refs):
        refs[-1][...] = jnp.zeros_like(refs[-1])

    return pl.pallas_call(
        body, name=name, in_specs=[_any_spec()] * (1 + len(deps)),
        out_specs=(_any_spec(), pl.BlockSpec(memory_space=pltpu.VMEM)),
        out_shape=(jax.ShapeDtypeStruct(x.shape, x.dtype),
                   jax.ShapeDtypeStruct((SUBLANE, LANE), F32)),
        input_output_aliases={0: 0},
    )(x, *deps)[0]


def _rmsnorm_fwd(name, x, g, *, into=None, col=0, deps=()):
    T, W = x.shape
    tr = _row_tile(T)

    def body(x_ref, g_ref, *rest):
        o_ref = rest[-1]
        xv = x_ref[...]
        rstd = lax.rsqrt(jnp.mean(xv * xv, axis=-1, keepdims=True) + EPS)
        o_ref[...] = (xv * rstd * g_ref[...]).astype(o_ref.dtype)

    in_specs = [pl.BlockSpec((tr, W), lambda i: (i, 0)), pl.BlockSpec((1, W), lambda i: (0, 0))]
    args = [x, g]
    kwargs = {}
    if into is None:
        out_shape = jax.ShapeDtypeStruct((T, W), BF16)
    else:
        out_shape = jax.ShapeDtypeStruct(into.shape, into.dtype)
        in_specs.append(_any_spec())
        args.append(into)
        kwargs["input_output_aliases"] = {2: 0}
    in_specs += [_any_spec()] * len(deps)
    args += list(deps)
    return pl.pallas_call(
        body, name=name, grid=(T // tr,), in_specs=in_specs,
        out_specs=pl.BlockSpec((tr, W), lambda i: (i, col)), out_shape=out_shape,
        compiler_params=_params(1), **kwargs)(*args)


def _rmsnorm_bwd(name, x, g, dn, *, dn_col=0, dres=None, want_dx=True, want_bf16=True):
    T, W = x.shape
    tr = _row_tile(T)
    has_res = dres is not None

    def body(*refs):
        x_ref, g_ref, dn_ref = refs[:3]
        pos = 3
        dres_ref = None
        if has_res:
            dres_ref = refs[pos]
            pos += 1
        outs = refs[pos:]
        dg_ref = outs[-1]
        xv = x_ref[...]
        rstd = lax.rsqrt(jnp.mean(xv * xv, axis=-1, keepdims=True) + EPS)
        xhat = xv * rstd
        dnv = dn_ref[...].astype(F32)

        @pl.when(pl.program_id(0) == 0)
        def _():
            dg_ref[...] = jnp.zeros_like(dg_ref)

        dg_ref[...] += jnp.sum(dnv * xhat, axis=0, keepdims=True)
        if want_dx:
            t = dnv * g_ref[...]
            dx = rstd * (t - xhat * jnp.mean(t * xhat, axis=-1, keepdims=True))
            if has_res:
                dx = dx + dres_ref[...]
            outs[0][...] = dx
            if want_bf16:
                outs[1][...] = dx.astype(BF16)

    row = pl.BlockSpec((tr, W), lambda i: (i, 0))
    in_specs = [row, pl.BlockSpec((1, W), lambda i: (0, 0)),
                pl.BlockSpec((tr, W), lambda i: (i, dn_col))]
    args = [x, g, dn]
    if has_res:
        in_specs.append(row)
        args.append(dres)
    out_shape, out_specs = [], []
    if want_dx:
        out_shape.append(jax.ShapeDtypeStruct((T, W), F32))
        out_specs.append(row)
        if want_bf16:
            out_shape.append(jax.ShapeDtypeStruct((T, W), BF16))
            out_specs.append(row)
    out_shape.append(jax.ShapeDtypeStruct((1, W), F32))
    out_specs.append(pl.BlockSpec((1, W), lambda i: (0, 0)))
    return pl.pallas_call(
        body, name=name, grid=(T // tr,), in_specs=in_specs, out_specs=out_specs,
        out_shape=out_shape, compiler_params=_params(1))(*args)


def _loss_head(name, h, g, target):
    T, W = h.shape
    tr = _row_tile(T)

    def body(h_ref, g_ref, t_ref, loss_ref, dx_ref, dxb_ref, dg_ref):
        xv = h_ref[...]
        gv = g_ref[...]
        rstd = lax.rsqrt(jnp.mean(xv * xv, axis=-1, keepdims=True) + EPS)
        xhat = xv * rstd
        diff = xhat * gv - t_ref[...]

        @pl.when(pl.program_id(0) == 0)
        def _():
            dg_ref[...] = jnp.zeros_like(dg_ref)
            loss_ref[...] = jnp.zeros_like(loss_ref)

        loss_ref[...] += 0.5 * jnp.sum(jnp.mean(diff * diff, axis=-1, keepdims=True))
        dnv = diff * (1.0 / W)
        dg_ref[...] += jnp.sum(dnv * xhat, axis=0, keepdims=True)
        t = dnv * gv
        dx = rstd * (t - xhat * jnp.mean(t * xhat, axis=-1, keepdims=True))
        dx_ref[...] = dx
        dxb_ref[...] = dx.astype(BF16)

    row = pl.BlockSpec((tr, W), lambda i: (i, 0))
    vec = pl.BlockSpec((1, W), lambda i: (0, 0))
    return pl.pallas_call(
        body, name=name, grid=(T // tr,), in_specs=[row, vec, row],
        out_specs=[pl.BlockSpec((SUBLANE, LANE), lambda i: (0, 0)), row, row, vec],
        out_shape=[jax.ShapeDtypeStruct((SUBLANE, LANE), F32), jax.ShapeDtypeStruct((T, W), F32),
                   jax.ShapeDtypeStruct((T, W), BF16), jax.ShapeDtypeStruct((1, W), F32)],
        compiler_params=_params(1))(h, g, target)


def _sigmoid(x):
    return 1.0 / (1.0 + jnp.exp(-x))


def _ffn_in(name, n, W, place, half, prev=None):
    T, D = n.shape
    F = W.cols // 2
    tm = _pick(2048, [T])
    tn = _pick(512, [W.pcol])
    per = W.pcol // tn

    def body(place_ref, a_ref, wg_ref, wu_ref, *rest):
        gu_ref, act_ref = rest[-2:]
        a = a_ref[...]
        gate = jnp.dot(a, wg_ref[...], preferred_element_type=F32)
        up = jnp.dot(a, wu_ref[...], preferred_element_type=F32)
        sig = _sigmoid(gate)
        silu = gate * sig
        gu_ref[0] = (up * sig * (1.0 + gate * (1.0 - sig))).astype(BF16)
        gu_ref[1] = silu.astype(BF16)
        act_ref[...] = (silu * up).astype(BF16)

    def pair(pr):
        return (pr[0] + half) % 2

    in_specs = [pl.BlockSpec((tm, D), lambda m, j, pr: (m, 0)),
                pl.BlockSpec((None, D, tn), lambda m, j, pr: (pair(pr), 0, j)),
                pl.BlockSpec((None, D, tn), lambda m, j, pr: (2 + pair(pr), 0, j))]
    args = [place, n, W.arr, W.arr]
    kwargs = {}
    if prev is not None:
        in_specs += [_any_spec(), _any_spec()]
        args += list(prev)
        kwargs["input_output_aliases"] = {4: 0, 5: 1}
    grid_spec = pltpu.PrefetchScalarGridSpec(
        num_scalar_prefetch=1, grid=(T // tm, per), in_specs=in_specs,
        out_specs=[pl.BlockSpec((2, tm, tn), lambda m, j, pr: (0, m, pair(pr) * per + j)),
                   pl.BlockSpec((tm, tn), lambda m, j, pr: (m, pair(pr) * per + j))])
    return pl.pallas_call(
        body, name=name, grid_spec=grid_spec,
        out_shape=[jax.ShapeDtypeStruct((2, T, F), BF16), jax.ShapeDtypeStruct((T, F), BF16)],
        compiler_params=_params(2), **kwargs)(*args)


def _ffn_forward(tag, h, norm_g, weights_of, place, deps=()):
    n = _rmsnorm_fwd(f"{tag}_norm", h, norm_g, deps=deps)
    got = weights_of(f"{tag}_in", n)
    gu, act = _ffn_in(f"{tag}_in_a", n, got[f"{tag}_w_in"], place, 0)
    w_in = got["finish"](act)[f"{tag}_w_in"]
    gu, act = _ffn_in(f"{tag}_in_b", n, w_in, place, 1, (gu, act))
    w_out = weights_of(f"{tag}_out", act)[f"{tag}_w_out"]

    def epi(acc, ex, out):
        out[0][...] = ex[0][...] + 0.5 * acc

    (h_out,) = _matmul(f"{tag}_out", Mat(act), w_out, "nn", [("c", 1, F32)],
                       tm=1024, tn=512, tk=8192, extras=[Mat(h)], epi=epi)
    return h_out[0], (n, gu, act, w_in, w_out)


def _ffn_backward(tag, h_in, norm_g, saved, dh, dh_bf, grads_ready, grads_flush,
                  early_out=False):
    n, gu, act, w_in, w_out = saved
    T, F = act.shape

    def epi(acc, ex, out):
        dact = 0.5 * acc
        out[0][0] = (dact * ex[0][0].astype(F32)).astype(BF16)
        out[0][1] = (dact * ex[0][1].astype(F32)).astype(BF16)

    def pair_spec(tm, tn):
        return pl.BlockSpec((2, tm, tn), lambda m, j, k: (0, m, j))

    def half(acc, ex, out):
        out[0][...] = (0.5 * acc).astype(out[0].dtype)

    (dw_out,) = _matmul(f"{tag}_dwout", Mat(act), Mat(dh_bf), "tn", [("r", N_CHIPS, BF16)],
                        tm=1408, tn=512, epi=half)
    if early_out:
        token = grads_ready(f"{tag}_out", {f"{tag}_w_out": dw_out})
        dh_bf = _tie(f"{tag}_dh_after_swap", dh_bf, [token])
    (dgu,) = _matmul(f"{tag}_dact", Mat(dh_bf), w_out, "nt",
                     [(jax.ShapeDtypeStruct((2, T, F), BF16), pair_spec)],
                     tm=512, tn=1408, extras=[(gu, pair_spec)], epi=epi)
    if early_out:
        dgu = _tie(f"{tag}_dgu_after_scatter", dgu, [grads_flush(f"{tag}_out", dgu)])
    (dw_in,) = _matmul(f"{tag}_dwin", Mat(n), Mat(dgu), "tn", [("c", N_CHIPS, BF16)],
                       tm=1024, tn=1408)
    if early_out:
        group, partial = f"{tag}_in", {f"{tag}_w_in": dw_in}
    else:
        group, partial = tag, {f"{tag}_w_in": dw_in, f"{tag}_w_out": dw_out}
    dgu = _tie(f"{tag}_dgu_after_swap", dgu, [grads_ready(group, partial)])
    (dn,) = _matmul(f"{tag}_dn", Mat(dgu), w_in, "nt", [("c", 1, F32)],
                    tm=1024, tn=1024, tk=2816)
    dn = _tie(f"{tag}_dn_after_scatter", dn, [grads_flush(group, dn)])
    return _rmsnorm_bwd(f"{tag}_dnorm", h_in, norm_g, dn[0], dres=dh)


_GELU_C = math.sqrt(2.0 / math.pi)
_GELU_A = 0.044715


def _gelu(x):
    return 0.5 * x * (1.0 + jnp.tanh(_GELU_C * (x + _GELU_A * x * x * x)))


def _gelu_grad(x):
    th = jnp.tanh(_GELU_C * (x + _GELU_A * x * x * x))
    return 0.5 * (1.0 + th) + 0.5 * x * (1.0 - th * th) * _GELU_C * (1.0 + 3.0 * _GELU_A * x * x)


def _chunk_mask():
    t = lax.broadcasted_iota(jnp.int32, (SGU_BLOCK, SGU_BLOCK), 0) // CHUNK
    s = lax.broadcasted_iota(jnp.int32, (SGU_BLOCK, SGU_BLOCK), 1) // CHUNK
    return s <= t


def _sgu_group_forward(v_g, lg, lb, wm_bf, b_col):
    mu = jnp.mean(v_g, axis=-1, keepdims=True)
    xc = v_g - mu
    rstd = lax.rsqrt(jnp.mean(xc * xc, axis=-1, keepdims=True) + EPS)
    vhat = xc * rstd
    vn = vhat * lg + lb
    mixed = jnp.dot(wm_bf, vn.astype(BF16), preferred_element_type=F32) + b_col
    return vhat, rstd, vn, mixed


def _sgu_forward(name, z, ln_g, ln_b, w_s, b_t, gn, d_model):
    T = z.shape[0]
    W_A = ln_g.shape[1]
    G = W_A // GROUP_DIM

    def body(z_ref, lg_ref, lb_ref, w_ref, bt_ref, gn_ref, y_ref):
        mask = _chunk_mask()
        u = _gelu(z_ref[:, :W_A])
        v = _gelu(z_ref[:, W_A:])
        cols = []
        for g in range(G):
            sl = slice(g * GROUP_DIM, (g + 1) * GROUP_DIM)
            wm = jnp.where(mask, w_ref[g], 0.0).astype(BF16)
            _, _, _, mixed = _sgu_group_forward(v[:, sl], lg_ref[:, sl], lb_ref[:, sl], wm,
                                                bt_ref[:, g:g + 1])
            cols.append(u[:, sl] * mixed)
        ya = jnp.concatenate(cols, axis=1)
        rstd = lax.rsqrt(jnp.mean(ya * ya, axis=-1, keepdims=True) + EPS)
        y_ref[...] = (ya * rstd * gn_ref[...]).astype(BF16)

    vec = pl.BlockSpec((1, W_A), lambda i: (0, 0))
    return pl.pallas_call(
        body, name=name, grid=(T // SGU_BLOCK,),
        in_specs=[pl.BlockSpec((SGU_BLOCK, 2 * W_A), lambda i: (i, 0)), vec, vec,
                  pl.BlockSpec((G, SGU_BLOCK, SGU_BLOCK), lambda i: (0, 0, 0)),
                  pl.BlockSpec((SGU_BLOCK, G), lambda i: (0, 0)), vec],
        out_specs=pl.BlockSpec((SGU_BLOCK, W_A), lambda i: (i, 0)),
        out_shape=jax.ShapeDtypeStruct((T, d_model), BF16),
        compiler_params=_params(1))(z, ln_g, ln_b, w_s, b_t, gn)


def _sgu_backward(name, z, dy, ln_g, ln_b, w_s, b_t, gn):
    T = z.shape[0]
    W_A = ln_g.shape[1]
    G = W_A // GROUP_DIM

    def body(z_ref, dy_ref, lg_ref, lb_ref, w_ref, bt_ref, gn_ref,
             dz_ref, dlg_ref, dlb_ref, dw_ref, db_ref, dgn_ref):
        @pl.when(pl.program_id(0) == 0)
        def _():
            for r in (dlg_ref, dlb_ref, dw_ref, db_ref, dgn_ref):
                r[...] = jnp.zeros_like(r)

        mask = _chunk_mask()
        zu = z_ref[:, :W_A]
        zv = z_ref[:, W_A:]
        u = _gelu(zu)
        v = _gelu(zv)
        saved, cols = [], []
        for g in range(G):
            sl = slice(g * GROUP_DIM, (g + 1) * GROUP_DIM)
            wm = jnp.where(mask, w_ref[g], 0.0)
            vhat, rstd, vn, mixed = _sgu_group_forward(
                v[:, sl], lg_ref[:, sl], lb_ref[:, sl], wm.astype(BF16), bt_ref[:, g:g + 1])
            saved.append((wm, vhat, rstd, vn, mixed))
            cols.append(u[:, sl] * mixed)
        ya = jnp.concatenate(cols, axis=1)
        rstd_a = lax.rsqrt(jnp.mean(ya * ya, axis=-1, keepdims=True) + EPS)
        ya_hat = ya * rstd_a
        dyv = dy_ref[...].astype(F32)
        dgn_ref[...] += jnp.sum(dyv * ya_hat, axis=0, keepdims=True)
        t = dyv * gn_ref[...]
        dya = rstd_a * (t - ya_hat * jnp.mean(t * ya_hat, axis=-1, keepdims=True))
        du_cols, dv_cols, dlg_cols, dlb_cols = [], [], [], []
        for g in range(G):
            sl = slice(g * GROUP_DIM, (g + 1) * GROUP_DIM)
            wm, vhat, rstd, vn, mixed = saved[g]
            dya_g = dya[:, sl]
            du_cols.append(dya_g * mixed)
            dmix = dya_g * u[:, sl]
            dmix_bf = dmix.astype(BF16)
            db_ref[g] += jnp.sum(dmix, axis=1, keepdims=True)
            dw = lax.dot_general(dmix_bf, vn.astype(BF16), (((1,), (1,)), ((), ())),
                                 preferred_element_type=F32)
            dw_ref[g] += jnp.where(mask, dw, 0.0)
            dvn = jnp.dot(wm.T.astype(BF16), dmix_bf, preferred_element_type=F32)
            dlg_cols.append(jnp.sum(dvn * vhat, axis=0, keepdims=True))
            dlb_cols.append(jnp.sum(dvn, axis=0, keepdims=True))
            dvhat = dvn * lg_ref[:, sl]
            dv_cols.append(rstd * (dvhat - jnp.mean(dvhat, axis=-1, keepdims=True)
                                   - vhat * jnp.mean(dvhat * vhat, axis=-1, keepdims=True)))
        dlg_ref[...] += jnp.concatenate(dlg_cols, axis=1)
        dlb_ref[...] += jnp.concatenate(dlb_cols, axis=1)
        dz_ref[:, :W_A] = (jnp.concatenate(du_cols, axis=1) * _gelu_grad(zu)).astype(BF16)
        dz_ref[:, W_A:] = (jnp.concatenate(dv_cols, axis=1) * _gelu_grad(zv)).astype(BF16)

    vec = pl.BlockSpec((1, W_A), lambda i: (0, 0))
    wspec = pl.BlockSpec((G, SGU_BLOCK, SGU_BLOCK), lambda i: (0, 0, 0))
    return pl.pallas_call(
        body, name=name, grid=(T // SGU_BLOCK,),
        in_specs=[pl.BlockSpec((SGU_BLOCK, 2 * W_A), lambda i: (i, 0)),
                  pl.BlockSpec((SGU_BLOCK, W_A), lambda i: (i, 0)), vec, vec, wspec,
                  pl.BlockSpec((SGU_BLOCK, G), lambda i: (0, 0)), vec],
        out_specs=[pl.BlockSpec((SGU_BLOCK, 2 * W_A), lambda i: (i, 0)), vec, vec, wspec,
                   pl.BlockSpec((G, SGU_BLOCK, 1), lambda i: (0, 0, 0)), vec],
        out_shape=[jax.ShapeDtypeStruct((T, 2 * W_A), BF16), jax.ShapeDtypeStruct((1, W_A), F32),
                   jax.ShapeDtypeStruct((1, W_A), F32),
                   jax.ShapeDtypeStruct((G, SGU_BLOCK, SGU_BLOCK), F32),
                   jax.ShapeDtypeStruct((G, SGU_BLOCK, 1), F32),
                   jax.ShapeDtypeStruct((1, W_A), F32)],
        compiler_params=_params(1))(z, dy, ln_g, ln_b, w_s, b_t, gn)


def _split_dot(x, tri):
    hi = x.astype(BF16)
    lo = (x - hi.astype(F32)).astype(BF16)
    return (jnp.dot(hi, tri, preferred_element_type=F32)
            + jnp.dot(lo, tri, preferred_element_type=F32))


def _tri(n, rel):
    r = lax.broadcasted_iota(jnp.int32, (n, n), 0)
    c = lax.broadcasted_iota(jnp.int32, (n, n), 1)
    return rel(r, c).astype(BF16)


def _dot_nt(a, b):
    return lax.dot_general(a, b, (((1,), (1,)), ((), ())), preferred_element_type=F32)


def _dot_tn(a, b):
    return lax.dot_general(a, b, (((0,), (0,)), ((), ())), preferred_element_type=F32)


def _sb_scores(qs, kj, mask):
    zz = _dot_nt(qs, kj)
    log_beta = jnp.minimum(zz, 0.0) - jnp.log(1.0 + jnp.exp(-jnp.abs(zz)))
    log_1m = log_beta - zz
    if mask is not None:
        log_1m = jnp.where(mask, log_1m, 0.0)
    return log_beta, log_1m


def _masked(mask, x):
    return x if mask is None else jnp.where(mask, x, 0.0)


def _below(old, new, row0):
    if row0 == 0:
        return tuple(new)
    return tuple(jnp.concatenate([o[:row0], n], axis=0) for o, n in zip(old, new))


def _sb_tiles(T):
    tk = _pick(256, [T])
    tq = 2 * tk if T % (2 * tk) == 0 else tk
    return tq, tk


def _sb_cols(w_a, w_b):
    base = 2 * w_a // GROUP_DIM
    per = w_b // GROUP_DIM
    return base, base + per, base + 2 * per


def _sb_forward(name, z, w_a, w_b):
    T = z.shape[0]
    H = w_b // GROUP_DIM
    tq, tk = _sb_tiles(T)
    per = tq // tk
    qc, kc, vc = _sb_cols(w_a, w_b)
    scale = GROUP_DIM ** -0.5

    def body(q_ref, k_ref, v_ref, y_ref, tot_ref):
        i = pl.program_id(1)
        qs = (q_ref[...] * scale).astype(BF16)
        upper = _tri(tk, lambda r, c: r > c)
        ahead = (lax.broadcasted_iota(jnp.int32, (tq, tk), 1)
                 - lax.broadcasted_iota(jnp.int32, (tq, tk), 0))

        def step(j, carry, masked, row0=0):
            acc, later = (c[row0:] for c in carry)
            k0 = pl.multiple_of(j * tk, tk)
            kj = k_ref[pl.ds(k0, tk), :].astype(BF16)
            vj = v_ref[pl.ds(k0, tk), :].astype(BF16)
            mask = ahead[row0:] < i * tq - k0 if masked else None
            log_beta, log_1m = _sb_scores(qs[row0:], kj, mask)
            rest = _split_dot(log_1m, upper) + later
            a = _masked(mask, jnp.exp(log_beta + rest))
            acc = acc + jnp.dot(a.astype(BF16), vj, preferred_element_type=F32)
            later = later + jnp.sum(log_1m, axis=1, keepdims=True)
            return _below(carry, (acc, later), row0)

        def blocks(p, c):
            for d in reversed(range(per)):
                c = step(p * per + d, c, False)
            return c

        carry = (jnp.zeros((tq, GROUP_DIM), F32), jnp.zeros((tq, 1), F32))
        for d in reversed(range(per)):
            carry = step(i * per + d, carry, True, d * tk)
        acc, total = lax.fori_loop(0, i, lambda pp, c: blocks(i - 1 - pp, c), carry)
        y_ref[...] = acc
        tot_ref[...] = total

    return pl.pallas_call(
        body, name=name, grid=(H, T // tq),
        in_specs=[pl.BlockSpec((tq, GROUP_DIM), lambda h, i: (i, qc + h)),
                  pl.BlockSpec((T, GROUP_DIM), lambda h, i: (0, kc + h)),
                  pl.BlockSpec((T, GROUP_DIM), lambda h, i: (0, vc + h))],
        out_specs=[pl.BlockSpec((tq, GROUP_DIM), lambda h, i: (i, h)),
                   pl.BlockSpec((None, tq, 1), lambda h, i: (h, i, 0))],
        out_shape=[jax.ShapeDtypeStruct((T, w_b), F32), jax.ShapeDtypeStruct((H, T, 1), F32)],
        compiler_params=_params(2))(z, z, z)


def _sb_backward(name, z, do, total, w_a, w_b):
    T = z.shape[0]
    H = w_b // GROUP_DIM
    tq, tk = _sb_tiles(T)
    per = tq // tk
    qc, kc, vc = _sb_cols(w_a, w_b)
    scale = GROUP_DIM ** -0.5

    def body(q_ref, k_ref, v_ref, do_ref, tot_ref, dq_ref, dkv_ref):
        i = pl.program_id(1)

        @pl.when(i == 0)
        def _():
            dkv_ref[...] = jnp.zeros_like(dkv_ref)

        qs = (q_ref[...] * scale).astype(BF16)
        dob = do_ref[...].astype(BF16)
        upto = _tri(tk, lambda r, c: r <= c)
        before = _tri(tk, lambda r, c: r < c)
        ahead = (lax.broadcasted_iota(jnp.int32, (tq, tk), 1)
                 - lax.broadcasted_iota(jnp.int32, (tq, tk), 0))

        def step(j, carry, masked, row0=0):
            dq, left, e_seen = (c[row0:] for c in carry)
            qr, dor = qs[row0:], dob[row0:]
            k0 = pl.multiple_of(j * tk, tk)
            kj = k_ref[pl.ds(k0, tk), :].astype(BF16)
            vj = v_ref[pl.ds(k0, tk), :].astype(BF16)
            mask = ahead[row0:] < i * tq - k0 if masked else None
            log_beta, log_1m = _sb_scores(qr, kj, mask)
            rest = left - _split_dot(log_1m, upto)
            a = _masked(mask, jnp.exp(log_beta + rest))
            e = a * _dot_nt(dor, vj)
            e_before = e_seen + jnp.dot(e.astype(BF16), before, preferred_element_type=F32)
            beta = jnp.exp(log_beta)
            dz = _masked(mask, e * (1.0 - beta) - beta * e_before).astype(BF16)
            dq = dq + jnp.dot(dz, kj, preferred_element_type=F32)
            dkv_ref[0, pl.ds(k0, tk), :] += _dot_tn(dz, qr)
            dkv_ref[1, pl.ds(k0, tk), :] += _dot_tn(a.astype(BF16), dor)
            left = left - jnp.sum(log_1m, axis=1, keepdims=True)
            e_seen = e_seen + jnp.sum(e, axis=1, keepdims=True)
            return _below(carry, (dq, left, e_seen), row0)

        def blocks(p, c):
            for d in range(per):
                c = step(p * per + d, c, False)
            return c

        carry = (jnp.zeros((tq, GROUP_DIM), F32), tot_ref[...], jnp.zeros((tq, 1), F32))
        carry = lax.fori_loop(0, i, blocks, carry)
        for d in range(per):
            carry = step(i * per + d, carry, True, d * tk)
        dq_ref[...] = (carry[0] * scale).astype(BF16)

    return pl.pallas_call(
        body, name=name, grid=(H, T // tq),
        in_specs=[pl.BlockSpec((tq, GROUP_DIM), lambda h, i: (i, qc + h)),
                  pl.BlockSpec((T, GROUP_DIM), lambda h, i: (0, kc + h)),
                  pl.BlockSpec((T, GROUP_DIM), lambda h, i: (0, vc + h)),
                  pl.BlockSpec((tq, GROUP_DIM), lambda h, i: (i, h)),
                  pl.BlockSpec((None, tq, 1), lambda h, i: (h, i, 0))],
        out_specs=[pl.BlockSpec((tq, GROUP_DIM), lambda h, i: (i, h)),
                   pl.BlockSpec((2, T, GROUP_DIM), lambda h, i: (0, 0, h))],
        out_shape=[jax.ShapeDtypeStruct((T, w_b), BF16), jax.ShapeDtypeStruct((2, T, w_b), F32)],
        compiler_params=_params(2))(z, z, z, do, total)


def _softmax_rows(s):
    m = jnp.max(s, axis=-1, keepdims=True)
    p = jnp.exp(s - m)
    return p / jnp.sum(p, axis=-1, keepdims=True)


def _xattn_forward(name, q, kv):
    T, D = q.shape
    Nm = kv.shape[0]
    dh = D // X_HEADS
    tq = _pick(512, [T])

    def body(q_ref, k_ref, v_ref, o_ref):
        p = _softmax_rows(_dot_nt(q_ref[...], k_ref[...]))
        o_ref[...] = jnp.dot(p.astype(BF16), v_ref[...], preferred_element_type=F32).astype(BF16)

    return pl.pallas_call(
        body, name=name, grid=(T // tq, X_HEADS),
        in_specs=[pl.BlockSpec((tq, dh), lambda i, h: (i, h)),
                  pl.BlockSpec((Nm, dh), lambda i, h: (0, h)),
                  pl.BlockSpec((Nm, dh), lambda i, h: (0, X_HEADS + h))],
        out_specs=pl.BlockSpec((tq, dh), lambda i, h: (i, h)),
        out_shape=jax.ShapeDtypeStruct((T, D), BF16),
        compiler_params=_params(2))(q, kv, kv)


def _xattn_backward(name, q, kv, do):
    T, D = q.shape
    Nm = kv.shape[0]
    dh = D // X_HEADS
    tq = _pick(512, [T])
    scale = dh ** -0.5

    def body(q_ref, k_ref, v_ref, do_ref, dq_ref, dkv_ref):
        @pl.when(pl.program_id(1) == 0)
        def _():
            dkv_ref[...] = jnp.zeros_like(dkv_ref)

        qv, kk, vv, dov = q_ref[...], k_ref[...], v_ref[...], do_ref[...]
        p = _softmax_rows(_dot_nt(qv, kk))
        dp = _dot_nt(dov, vv)
        ds = (p * (dp - jnp.sum(dp * p, axis=-1, keepdims=True))).astype(BF16)
        dq_ref[...] = (jnp.dot(ds, kk, preferred_element_type=F32) * scale).astype(BF16)
        dkv_ref[0] += _dot_tn(ds, qv)
        dkv_ref[1] += _dot_tn(p.astype(BF16), dov)

    blk = pl.BlockSpec((tq, dh), lambda h, i: (i, h))
    return pl.pallas_call(
        body, name=name, grid=(X_HEADS, T // tq),
        in_specs=[blk, pl.BlockSpec((Nm, dh), lambda h, i: (0, h)),
                  pl.BlockSpec((Nm, dh), lambda h, i: (0, X_HEADS + h)), blk],
        out_specs=[blk, pl.BlockSpec((2, Nm, dh), lambda h, i: (0, 0, h))],
        out_shape=[jax.ShapeDtypeStruct((T, D), BF16), jax.ShapeDtypeStruct((2, Nm, D), F32)],
        compiler_params=_params(2))(q, kv, kv, do)


def _position():
    x, y, c = lax.axis_index("x"), lax.axis_index("y"), lax.axis_index("c")
    other_chips = [(1 - x, y), (x, 1 - y), (1 - x, 1 - y)]
    return x, y, c, other_chips


def _hbm_spec():
    return pl.BlockSpec(memory_space=pltpu.HBM)


def _sem_spec():
    return pl.BlockSpec(memory_space=pltpu.SEMAPHORE)


def _split_start(name, arrays, make_copies, n_sems, deps=()):
    n, d = len(arrays), len(deps)

    def body(*refs):
        ins = refs[:n]
        send_sems, recv_sems = refs[n + d], refs[n + d + 1]
        token = refs[-1]
        for cp in make_copies(ins, send_sems, recv_sems):
            cp.start()
        token[...] = jnp.zeros_like(token)

    res = pl.pallas_call(
        body, name=name,
        out_shape=(pltpu.SemaphoreType.DMA((n_sems,)), pltpu.SemaphoreType.DMA((n_sems,)),
                   *[pltpu.HBM(a.shape, a.dtype) for a in arrays],
                   jax.ShapeDtypeStruct((SUBLANE, LANE), F32)),
        in_specs=[_hbm_spec()] * n + [_any_spec()] * d,
        out_specs=(_sem_spec(), _sem_spec(), *[_hbm_spec()] * n,
                   pl.BlockSpec(memory_space=pltpu.VMEM)),
        input_output_aliases={i: 2 + i for i in range(n)},
        compiler_params=pltpu.CompilerParams(
            has_side_effects=pltpu.SideEffectType.DATAFLOW_SIDE_EFFECTING),
    )(*[pltpu.with_memory_space_constraint(a, pltpu.HBM) for a in arrays], *deps)
    return res[0], res[1], list(res[2:2 + n]), res[-1]


def _split_wait(name, arrays, send_sems, recv_sems, after, make_copies):
    n = len(arrays)
    after = list(after) if isinstance(after, (list, tuple)) else [after]

    def body(*refs):
        ins = refs[:n]
        send_ref, recv_ref = refs[n], refs[n + 1]
        for cp in make_copies(ins, send_ref, recv_ref):
            cp.wait_send()
            cp.wait_recv()

    return pl.pallas_call(
        body, name=name,
        out_shape=tuple(pltpu.HBM(a.shape, a.dtype) for a in arrays),
        in_specs=[_hbm_spec()] * n + [_sem_spec(), _sem_spec()] + [_any_spec()] * len(after),
        out_specs=tuple(_hbm_spec() for _ in arrays),
        input_output_aliases={i: i for i in range(n)},
        compiler_params=pltpu.CompilerParams(
            has_side_effects=pltpu.SideEffectType.DATAFLOW_SIDE_EFFECTING),
    )(*arrays, send_sems, recv_sems, *after)


def _gather_copies(refs, send_sems, recv_sems):
    x, y, c, chips = _position()
    me = 2 * x + y
    copies = []
    for i, ref in enumerate(refs):
        rows = ref.shape[1] // 2
        piece = ref.at[me, pl.ds(c * rows, rows), :]
        for j, (px, py) in enumerate(chips):
            copies.append(pltpu.make_async_remote_copy(
                src_ref=piece, dst_ref=piece, send_sem=send_sems.at[3 * i + j],
                recv_sem=recv_sems.at[3 * i + j], device_id=(px, py, c), device_id_type=MESH))
    return copies


def _near_copies(refs, send_sems, recv_sems):
    x, y, c, chips = _position()
    me = 2 * x + y
    copies = []
    for i, ref in enumerate(refs):
        rows = ref.shape[1] // 2
        piece = ref.at[me, pl.ds(c * rows, rows), :]
        for j, (px, py) in enumerate(chips[:2]):
            copies.append(pltpu.make_async_remote_copy(
                src_ref=piece, dst_ref=piece, send_sem=send_sems.at[2 * i + j],
                recv_sem=recv_sems.at[2 * i + j], device_id=(px, py, c), device_id_type=MESH))
    return copies


def _relay_copies(refs, send_sems, recv_sems):
    x, y, c, chips = _position()
    copies = []
    for i, ref in enumerate(refs):
        rows = ref.shape[1] // 4
        for j, (px, py) in enumerate(chips[:2]):
            ox, oy = chips[1 - j]
            piece = ref.at[2 * ox + oy, pl.ds((2 * c + j) * rows, rows), :]
            copies.append(pltpu.make_async_remote_copy(
                src_ref=piece, dst_ref=piece, send_sem=send_sems.at[2 * i + j],
                recv_sem=recv_sems.at[2 * i + j], device_id=(px, py, c), device_id_type=MESH))
    return copies


def _share_copies(refs, send_sems, recv_sems):
    x, y, c, _ = _position()
    copies = []
    for i, ref in enumerate(refs):
        rows = ref.shape[0] // 2
        mine = ref.at[pl.ds(c * rows, rows), :]
        copies.append(pltpu.make_async_remote_copy(
            src_ref=mine, dst_ref=mine, send_sem=send_sems.at[i], recv_sem=recv_sems.at[i],
            device_id=(x, y, 1 - c), device_id_type=MESH))
    return copies


def _scatter_copies(refs, send_sems, recv_sems):
    x, y, c, chips = _position()
    n = len(refs) // 2
    copies = []
    for i in range(n):
        for j, (px, py) in enumerate(chips):
            copies.append(pltpu.make_async_remote_copy(
                src_ref=refs[i].at[2 * px + py], dst_ref=refs[n + i].at[j],
                send_sem=send_sems.at[3 * i + j], recv_sem=recv_sems.at[3 * i + j],
                device_id=(px, py, c), device_id_type=MESH))
    return copies


def _cast_own(name, place, shard):
    rows, cols = shard.shape
    tr = _block_rows(rows, cols)

    def body(place_ref, w_ref, o_ref):
        o_ref[...] = w_ref[...].astype(BF16)

    grid_spec = pltpu.PrefetchScalarGridSpec(
        num_scalar_prefetch=1, grid=(rows // tr,),
        in_specs=[pl.BlockSpec((tr, cols), lambda r, pr: (r, 0))],
        out_specs=pl.BlockSpec((None, tr, cols), lambda r, pr: (pr[0], r, 0)))
    return pl.pallas_call(
        body, name=name, grid_spec=grid_spec,
        out_shape=jax.ShapeDtypeStruct((N_CHIPS, rows, cols), BF16),
        compiler_params=_params(1))(place, shard)


def _forward_to_sibling(name, arrays, deps=(), which=(0, 1, 2)):
    n = len(arrays)

    def body(*refs):
        ins = refs[:n]
        send_sems, recv_sems = refs[-2:]
        x, y, c, chips = _position()
        chips = [(j, chips[j]) for j in which]
        sends = []
        for i in range(n):
            rows = ins[i].shape[1] // 2
            for j, (px, py) in chips:
                piece = ins[i].at[2 * px + py, pl.ds(c * rows, rows), :]
                cp = pltpu.make_async_remote_copy(
                    src_ref=piece, dst_ref=piece, send_sem=send_sems.at[i, j],
                    recv_sem=recv_sems.at[i, j], device_id=(x, y, 1 - c), device_id_type=MESH)
                cp.start()
                sends.append(cp)
        for i in range(n):
            rows = ins[i].shape[1] // 2
            for j, (px, py) in chips:
                piece = ins[i].at[2 * px + py, pl.ds((1 - c) * rows, rows), :]
                pltpu.make_async_remote_copy(
                    src_ref=piece, dst_ref=piece, send_sem=send_sems.at[i, j],
                    recv_sem=recv_sems.at[i, j], device_id=(x, y, 1 - c),
                    device_id_type=MESH).wait_recv()
        for cp in sends:
            cp.wait_send()

    return pl.pallas_call(
        body, name=name,
        in_specs=[_any_spec()] * (n + len(deps)), out_specs=[_any_spec()] * n,
        out_shape=[jax.ShapeDtypeStruct(a.shape, a.dtype) for a in arrays],
        input_output_aliases={i: i for i in range(n)},
        scratch_shapes=[pltpu.SemaphoreType.DMA((n, 3))] * 2,
    )(*arrays, *deps)


def _forward_copies(which):
    def make(refs, send_sems, recv_sems):
        x, y, c, chips = _position()
        copies = []
        for i, ref in enumerate(refs):
            rows = ref.shape[1] // 2
            for k, j in enumerate(which):
                px, py = chips[j]
                piece = ref.at[2 * px + py, pl.ds(c * rows, rows), :]
                sem = len(which) * i + k
                copies.append(pltpu.make_async_remote_copy(
                    src_ref=piece, dst_ref=piece, send_sem=send_sems.at[sem],
                    recv_sem=recv_sems.at[sem], device_id=(x, y, 1 - c), device_id_type=MESH))
        return copies

    return make


def _swap_copies(refs, send_sems, recv_sems):
    x, y, c, _ = _position()
    n = len(refs) // 2
    copies = []
    for i in range(n):
        rows = refs[i].shape[1] // 2
        copies.append(pltpu.make_async_remote_copy(
            src_ref=refs[i].at[:, pl.ds((1 - c) * rows, rows), :], dst_ref=refs[n + i],
            send_sem=send_sems.at[i], recv_sem=recv_sems.at[i],
            device_id=(x, y, 1 - c), device_id_type=MESH))
    return copies


def _small_copies(refs, send_sems, recv_sems):
    packed, slots = refs
    x, y, c, _ = _position()
    me = 4 * x + 2 * y + c
    copies = []
    for r in range(1, N_DEV):
        peer = (x ^ ((r >> 2) & 1), y ^ ((r >> 1) & 1), c ^ (r & 1))
        copies.append(pltpu.make_async_remote_copy(
            src_ref=packed, dst_ref=slots.at[me], send_sem=send_sems.at[r - 1],
            recv_sem=recv_sems.at[r - 1], device_id=peer, device_id_type=MESH))
    return copies


def _block_rows(rows, cols, itemsize=4, target=1 << 20):
    return _pick(max(BF16_ROWS, target // (cols * itemsize)), [rows], unit=BF16_ROWS)


def _pair_sum(name, place, grad, received):
    P, rows, cols = received.shape
    tr = _block_rows(rows, cols, itemsize=2, target=2 << 20)
    nb = rows // tr

    def body(place_ref, g_ref, r_ref, o_ref):
        o_ref[...] = (g_ref[...].astype(F32) + r_ref[...].astype(F32)).astype(BF16)

    def panel(j, pr):
        return pr[0] ^ jnp.where(j == 2, 3, 2 - j)

    grid_spec = pltpu.PrefetchScalarGridSpec(
        num_scalar_prefetch=1, grid=(P - 1, nb),
        in_specs=[pl.BlockSpec((None, tr, cols),
                               lambda j, r, pr: (panel(j, pr), pr[1] * nb + r, 0)),
                  pl.BlockSpec((None, tr, cols), lambda j, r, pr: (panel(j, pr), r, 0))],
        out_specs=pl.BlockSpec((None, tr, cols), lambda j, r, pr: (panel(j, pr), r, 0)))
    return pl.pallas_call(
        body, name=name, grid_spec=grid_spec,
        out_shape=jax.ShapeDtypeStruct(received.shape, BF16),
        compiler_params=_params(2))(place, grad, received)


def _final_sum(name, place, grad, received, from_chips):
    _, rows, cols = received.shape
    tr = _block_rows(rows, cols, target=2 << 20)
    nb = rows // tr

    def body(place_ref, g_ref, r_ref, c_ref, o_ref):
        acc = g_ref[...].astype(F32) + r_ref[...].astype(F32)
        for j in range(3):
            acc = acc + c_ref[j].astype(F32)
        o_ref[...] = acc

    grid_spec = pltpu.PrefetchScalarGridSpec(
        num_scalar_prefetch=1, grid=(nb,),
        in_specs=[pl.BlockSpec((None, tr, cols), lambda r, pr: (pr[0], pr[1] * nb + r, 0)),
                  pl.BlockSpec((None, tr, cols), lambda r, pr: (pr[0], r, 0)),
                  pl.BlockSpec((3, tr, cols), lambda r, pr: (0, r, 0))],
        out_specs=pl.BlockSpec((tr, cols), lambda r, pr: (pr[1] * nb + r, 0)))
    return pl.pallas_call(
        body, name=name, grid_spec=grid_spec,
        out_shape=jax.ShapeDtypeStruct((2 * rows, cols), F32),
        compiler_params=_params(1))(place, grad, received, from_chips)


def _sum_devices(name, me, gathered, own):
    n_dev, rows, cols = gathered.shape
    tr = _pick(256, [rows])

    def body(me_ref, g_ref, own_ref, o_ref):
        term = lambda d: jnp.where(me_ref[0] == d, own_ref[...], g_ref[d])
        acc = term(0)
        for d in range(1, n_dev):
            acc = acc + term(d)
        o_ref[...] = acc

    grid_spec = pltpu.PrefetchScalarGridSpec(
        num_scalar_prefetch=1, grid=(rows // tr,),
        in_specs=[pl.BlockSpec((n_dev, tr, cols), lambda r, me_ref: (0, r, 0)),
                  pl.BlockSpec((tr, cols), lambda r, me_ref: (r, 0))],
        out_specs=pl.BlockSpec((tr, cols), lambda r, me_ref: (r, 0)))
    return pl.pallas_call(
        body, name=name, grid_spec=grid_spec,
        out_shape=jax.ShapeDtypeStruct((rows, cols), F32),
        compiler_params=_params(1))(me, gathered, own)


def _adamw(name, w, g, m, v):
    rows, cols = w.shape
    tr = _block_rows(rows, cols)
    c1 = 1.0 / (1.0 - ADAM_B1 ** ADAM_STEP)
    c2 = 1.0 / (1.0 - ADAM_B2 ** ADAM_STEP)

    def body(w_ref, g_ref, m_ref, v_ref, go_ref, d_ref, nm_ref, nv_ref):
        gv = g_ref[...]
        go_ref[...] = gv
        nm = ADAM_B1 * m_ref[...] + (1.0 - ADAM_B1) * gv
        nv = ADAM_B2 * v_ref[...] + (1.0 - ADAM_B2) * (gv * gv)
        nm_ref[...] = nm
        nv_ref[...] = nv
        d_ref[...] = -ADAM_LR * ((nm * c1) / (jnp.sqrt(nv * c2) + ADAM_EPS) + ADAM_WD * w_ref[...])

    blk = pl.BlockSpec((tr, cols), lambda r: (r, 0))
    shape = jax.ShapeDtypeStruct((rows, cols), F32)
    return pl.pallas_call(
        body, name=name, grid=(rows // tr,), in_specs=[blk] * 4, out_specs=[blk] * 4,
        out_shape=[shape] * 4, compiler_params=_params(1))(w, g, m, v)


BIG = ("ffn1_w_in", "ffn1_w_out", "w_mix_in", "w_mix_out", "w_cq", "w_ckv", "w_co",
       "ffn2_w_in", "ffn2_w_out")
BIG_KIND = {"ffn1_w_in": "c", "ffn1_w_out": "r", "w_mix_in": "c", "w_mix_out": "r", "w_cq": "r",
            "w_ckv": "c", "w_co": "r", "ffn2_w_in": "c", "ffn2_w_out": "r"}
GATHER_GROUPS = (("ffn1_in", ("ffn1_w_in",)), ("ffn1_out", ("ffn1_w_out",)),
                 ("mix_in", ("w_mix_in",)), ("mix_out", ("w_mix_out",)),
                 ("cross", ("w_cq", "w_ckv", "w_co")),
                 ("ffn2_in", ("ffn2_w_in",)), ("ffn2_out", ("ffn2_w_out",)))
GATHER_AFTER = (("ffn1_in", None), ("ffn1_out", "ffn1_in"), ("mix_in", "ffn1_out"),
                ("mix_out", "mix_in"), ("cross", "mix_in"), ("ffn2_in", "mix_in"),
                ("ffn2_out", "ffn2_in"))
RELAYED = ("ffn1_in", "ffn2_in")
TAIL_STAGES = (("sum", "ffn2"), ("sum", "cross"), ("sum", "mix"), ("sum", "ffn1_out"),
               ("update", "ffn2"), ("update", "cross"), ("sum", "ffn1_in"), ("update", "mix"),
               ("update", "ffn1_out"), ("update", "ffn1_in"))
SMALL = ("ffn1_norm", "mix_norm", "ln_v_gain", "ln_v_bias", "spatial_w", "spatial_b", "gnorm_a",
         "gnorm_b", "cross_norm", "mem_norm", "ffn2_norm", "final_norm")
WEIGHTS = ("ffn1_norm", "ffn1_w_in", "ffn1_w_out", "mix_norm", "w_mix_in", "ln_v_gain",
           "ln_v_bias", "spatial_w", "spatial_b", "gnorm_a", "gnorm_b", "w_mix_out", "cross_norm",
           "mem_norm", "w_cq", "w_ckv", "w_co", "ffn2_norm", "ffn2_w_in", "ffn2_w_out",
           "final_norm")


def _pack(arrays):
    return jnp.concatenate([a.reshape(-1, LANE) for a in arrays], axis=0)


def _unpack(packed, like):
    out, row = [], 0
    for a in like:
        rows = a.size // LANE
        out.append(packed[row:row + rows].reshape(a.shape))
        row += rows
    return out


def _local_step(x, mem, target, small, place, weights_of, forward_early, start_tokens,
                grads_ready, grads_flush):
    T, D = x.shape
    vec = lambda name: small[name].reshape(1, -1)
    w_a = small["ln_v_gain"].size
    w_b = small["gnorm_b"].size
    G = w_a // GROUP_DIM
    w_s = small["spatial_w"].reshape(G, SGU_BLOCK, SGU_BLOCK)
    b_t = small["spatial_b"].reshape(G, SGU_BLOCK).T

    h1, ffn1_saved = _ffn_forward("ffn1", x, vec("ffn1_norm"), weights_of, place,
                                  deps=start_tokens)
    n2 = _rmsnorm_fwd("mix_norm", h1, vec("mix_norm"))
    big = weights_of("mix_in", n2)
    (z,) = _matmul("mix_in", Mat(n2), big["w_mix_in"], "nn", [("c", 1, F32)], tm=2048, tn=256)
    z = _tie("z_after_forward_start", z[0], [forward_early("mix_out", z)])
    y = _sgu_forward("sgu", z, vec("ln_v_gain"), vec("ln_v_bias"), w_s, b_t, vec("gnorm_a"), D)
    yb, sb_total = _sb_forward("stickbreak", z, w_a, w_b)
    y = _rmsnorm_fwd("gnorm_b", yb, vec("gnorm_b"), into=y, col=w_a // w_b)
    y = _tie("y_after_forward_start", y, [forward_early("cross", y)])

    def add_res(acc, ex, out):
        out[0][...] = ex[0][...] + acc

    big.update(weights_of("mix_out", y))
    (h2,) = _matmul("mix_out", Mat(y), big["w_mix_out"], "nn", [("c", 1, F32)],
                    tm=1024, tn=1024, extras=[Mat(h1)], epi=add_res)
    h2 = h2[0]
    n3 = _rmsnorm_fwd("cross_norm", h2, vec("cross_norm"))
    memn = _rmsnorm_fwd("mem_norm", mem, vec("mem_norm"))
    big.update(weights_of("cross", n3))
    x_scale = (D // X_HEADS) ** -0.5

    def scaled(acc, ex, out):
        out[0][...] = (acc * x_scale).astype(BF16)

    (q,) = _matmul("cross_q", Mat(n3), big["w_cq"], "nn", [("c", 1, BF16)],
                   tm=1024, tn=1024, epi=scaled)
    (kv,) = _matmul("cross_kv", Mat(memn), big["w_ckv"], "nn", [("c", 1, BF16)], tm=256, tn=1024)
    q, kv = q[0], kv[0]
    o = _xattn_forward("cross_attn", q, kv)
    (h3,) = _matmul("cross_out", Mat(o), big["w_co"], "nn", [("c", 1, F32)],
                    tm=1024, tn=1024, extras=[Mat(h2)], epi=add_res)
    h3 = h3[0]
    h4, ffn2_saved = _ffn_forward("ffn2", h3, vec("ffn2_norm"), weights_of, place)

    gs = {}
    loss_tile, dh4, dh4_bf, gs["final_norm"] = _loss_head("loss_head", h4, vec("final_norm"), target)
    dh3, dh3_bf, gs["ffn2_norm"] = _ffn_backward(
        "ffn2", h3, vec("ffn2_norm"), ffn2_saved, dh4, dh4_bf, grads_ready, grads_flush)

    (do,) = _matmul("cross_do", Mat(dh3_bf), big["w_co"], "nt", [("c", 1, BF16)], tm=512, tn=2048)
    (dw_co,) = _matmul("cross_dwo", Mat(o), Mat(dh3_bf), "tn", [("r", N_CHIPS, BF16)],
                       tm=512, tn=1024)
    dq, dkv = _xattn_backward("cross_attn_bwd", q, kv, do[0])
    (dw_cq,) = _matmul("cross_dwq", Mat(n3), Mat(dq), "tn", [("r", N_CHIPS, BF16)],
                       tm=512, tn=1024)
    (dw_ckv,) = _matmul("cross_dwkv", Mat(memn), Mat(dkv), "tn", [("c", N_CHIPS, BF16)],
                        tm=1024, tn=1024)
    token = grads_ready("cross", {"w_cq": dw_cq, "w_ckv": dw_ckv, "w_co": dw_co})
    dq = _tie("cross_dq_after_swap", dq, [token])
    (dn3,) = _matmul("cross_dn", Mat(dq), big["w_cq"], "nt", [("c", 1, F32)], tm=512, tn=2048)
    (dmemn,) = _matmul("cross_dmem", Mat(dkv), big["w_ckv"], "nt", [("c", 1, F32)],
                       tm=256, tn=1024, tk=1024)
    (gs["mem_norm"],) = _rmsnorm_bwd("mem_dnorm", mem, vec("mem_norm"), dmemn[0], want_dx=False)
    dn3 = _tie("cross_dn_after_scatter", dn3, [grads_flush("cross", gs["mem_norm"])])
    dh2, dh2_bf, gs["cross_norm"] = _rmsnorm_bwd("cross_dnorm", h2, vec("cross_norm"), dn3[0],
                                                 dres=dh3)

    (dy,) = _matmul("mix_dy", Mat(dh2_bf), big["w_mix_out"], "nt", [("c", 1, F32)], tm=512, tn=2048)
    dy = dy[0]
    (dw_mix_out,) = _matmul("mix_dwout", Mat(y), Mat(dh2_bf), "tn", [("r", N_CHIPS, BF16)],
                            tm=512, tn=1024)
    dza, gs["ln_v_gain"], gs["ln_v_bias"], gs["spatial_w"], db, gs["gnorm_a"] = _sgu_backward(
        "sgu_bwd", z, dy, vec("ln_v_gain"), vec("ln_v_bias"), w_s, b_t, vec("gnorm_a"))
    gs["spatial_b"] = db.reshape(G, SGU_BLOCK)
    dob, gs["gnorm_b"] = _rmsnorm_bwd("gnorm_b_bwd", yb, vec("gnorm_b"), dy, dn_col=w_a // w_b,
                                      want_bf16=False)
    dqb, dkvb = _sb_backward("stickbreak_bwd", z, dob, sb_total, w_a, w_b)
    dz = jnp.concatenate([dza, dqb, dkvb[0].astype(BF16), dkvb[1].astype(BF16)], axis=1)
    (dw_mix_in,) = _matmul("mix_dwin", Mat(n2), Mat(dz), "tn", [("c", N_CHIPS, BF16)],
                           tm=1024, tn=1280)
    token = grads_ready("mix", {"w_mix_in": dw_mix_in, "w_mix_out": dw_mix_out})
    dz = _tie("mix_dz_after_swap", dz, [token])
    (dn2,) = _matmul("mix_dn", Mat(dz), big["w_mix_in"], "nt", [("c", 1, F32)],
                     tm=1024, tn=1024, tk=1280)
    dn2 = _tie("mix_dn_after_scatter", dn2, [grads_flush("mix", dn2)])
    dh1, dh1_bf, gs["mix_norm"] = _rmsnorm_bwd("mix_dnorm", h1, vec("mix_norm"), dn2[0], dres=dh2)

    dx, _, gs["ffn1_norm"] = _ffn_backward(
        "ffn1", x, vec("ffn1_norm"), ffn1_saved, dh1, dh1_bf, grads_ready, grads_flush,
        early_out=True)
    gs = {k: g.reshape(small[k].shape) for k, g in gs.items()}
    return loss_tile, dx, gs


def kernel(x, mem, ffn1_norm, ffn1_w_in, ffn1_w_out, mix_norm, w_mix_in, ln_v_gain, ln_v_bias, spatial_w, spatial_b, gnorm_a, gnorm_b, w_mix_out, cross_norm, mem_norm, w_cq, w_ckv, w_co, ffn2_norm, ffn2_w_in, ffn2_w_out, final_norm, loss_target, m_ffn1_norm, m_ffn1_w_in, m_ffn1_w_out, m_mix_norm, m_w_mix_in, m_ln_v_gain, m_ln_v_bias, m_spatial_w, m_spatial_b, m_gnorm_a, m_gnorm_b, m_w_mix_out, m_cross_norm, m_mem_norm, m_w_cq, m_w_ckv, m_w_co, m_ffn2_norm, m_ffn2_w_in, m_ffn2_w_out, m_final_norm, v_ffn1_norm, v_ffn1_w_in, v_ffn1_w_out, v_mix_norm, v_w_mix_in, v_ln_v_gain, v_ln_v_bias, v_spatial_w, v_spatial_b, v_gnorm_a, v_gnorm_b, v_w_mix_out, v_cross_norm, v_mem_norm, v_w_cq, v_w_ckv, v_w_co, v_ffn2_norm, v_ffn2_w_in, v_ffn2_w_out, v_final_norm):
    given = dict(locals())
    w = {k: given[k] for k in WEIGHTS}
    m = {k: given["m_" + k] for k in WEIGHTS}
    v = {k: given["v_" + k] for k in WEIGHTS}

    cx, cy, cc = lax.axis_index("x"), lax.axis_index("y"), lax.axis_index("c")
    place = jnp.stack([2 * cx + cy, cc]).astype(jnp.int32)

    names_of = dict(GATHER_GROUPS)
    own = {g: [_cast_own(f"cast_{k}", place, w[k][0]) for k in names] for g, names in GATHER_GROUPS}
    gathers = {}

    def start_gather(group, deps):
        first_hop = _near_copies if group in RELAYED else _gather_copies
        n_sems = (2 if group in RELAYED else 3) * len(own[group])
        send, recv, arrays, token = _split_start(f"gather_start_{group}", own[group],
                                                 first_hop, n_sems, deps)
        gathers[group] = (send, recv, arrays)
        return token

    start_tokens = [start_gather(g, ()) for g, after in GATHER_AFTER if after is None]
    start_tokens += [a for g, after in GATHER_AFTER if after is not None for a in own[g]]

    def weights_of(group, after):
        as_mats = lambda arrs: {k: Mat(a, BIG_KIND[k]) for k, a in zip(names_of[group], arrs)}
        if group in forwards:
            send, recv, arrays = forwards[group]
            return as_mats(_split_wait(f"gather_forward_wait_{group}", arrays, send, recv, after,
                                       _forward_copies((0, 1, 2))))
        send, recv, arrays = gathers[group]
        if group not in RELAYED:
            arrays = _split_wait(f"gather_wait_{group}", arrays, send, recv, after, _gather_copies)
            tokens = [start_gather(g, (arrays[0],)) for g, a in GATHER_AFTER if a == group]
            return as_mats(_forward_to_sibling(f"gather_forward_{group}", list(arrays), tokens))
        arrays = _split_wait(f"gather_wait_{group}", arrays, send, recv, after, _near_copies)
        send, recv, arrays, token = _split_start(f"gather_relay_{group}", list(arrays),
                                                 _relay_copies, 2 * len(arrays))
        tokens = [token] + [start_gather(g, (arrays[0],)) for g, a in GATHER_AFTER if a == group]
        arrays = _forward_to_sibling(f"gather_forward_{group}", list(arrays), tokens, which=(0,))
        y_send, y_recv, arrays, _ = _split_start(f"gather_forward_y_{group}", list(arrays),
                                                 _forward_copies((1,)), len(arrays))

        def finish(after):
            arrs = _split_wait(f"gather_relay_wait_{group}", arrays, send, recv, after,
                               _relay_copies)
            arrs = _split_wait(f"gather_forward_y_wait_{group}", arrs, y_send, y_recv, after,
                               _forward_copies((1,)))
            return as_mats(_forward_to_sibling(f"gather_forward_diag_{group}", list(arrs),
                                               which=(2,)))

        return {**as_mats(arrays), "finish": finish}

    forwards = {}

    def forward_early(group, after):
        send, recv, arrays = gathers[group]
        arrays = _split_wait(f"gather_wait_{group}", arrays, send, recv, after, _gather_copies)
        send, recv, arrays, token = _split_start(f"gather_forward_start_{group}", list(arrays),
                                                 _forward_copies((0, 1, 2)), 3 * len(arrays))
        forwards[group] = (send, recv, arrays)
        return token

    swaps, scatters = {}, {}

    def grads_ready(group, partial):
        names = list(partial)
        grads_ = [partial[k] for k in names]
        lands = [lax.empty((g.shape[0], g.shape[1] // 2, g.shape[2]), g.dtype) for g in grads_]
        send, recv, arrays, token = _split_start(f"swap_start_{group}", grads_ + lands,
                                                 _swap_copies, len(names))
        swaps[group] = (names, send, recv, arrays)
        return token

    def grads_flush(group, after):
        names, send, recv, arrays = swaps[group]
        arrays = _split_wait(f"swap_wait_{group}", arrays, send, recv, after, _swap_copies)
        grads_, from_sibling = arrays[:len(names)], arrays[len(names):]
        sums = [_pair_sum(f"pair_sum_{k}", place, g, r)
                for k, g, r in zip(names, grads_, from_sibling)]
        lands = [lax.empty((3,) + s.shape[1:], s.dtype) for s in sums]
        send, recv, arrays, token = _split_start(f"scatter_start_{group}", sums + lands,
                                                 _scatter_copies, 3 * len(names))
        scatters[group] = (names, grads_, from_sibling, send, recv, arrays)
        return token

    small = {k: w[k] for k in SMALL}
    loss_tile, grad_x, gs = _local_step(x[0], mem[0], loss_target[0], small, place, weights_of,
                                        forward_early, start_tokens, grads_ready, grads_flush)

    packed = _pack([gs[k] for k in SMALL] + [loss_tile])
    slots = jnp.zeros((N_DEV,) + packed.shape, packed.dtype)
    small_send, small_recv, small_arrays, _ = _split_start(
        "small_start", [packed, slots], _small_copies, N_DEV - 1)

    grad, delta, new_m, new_v = {}, {}, {}, {}
    shares = {}
    after = [grad_x]
    for stage, group in TAIL_STAGES:
        if stage == "sum":
            names, grads_, from_sibling, send, recv, arrays = scatters[group]
            arrays = _split_wait(f"scatter_wait_{group}", arrays, send, recv, after,
                                 _scatter_copies)
            from_chips = arrays[len(names):]
            shards = [_final_sum(f"final_sum_{k}", place, g, r, f)
                      for k, g, r, f in zip(names, grads_, from_sibling, from_chips)]
            send, recv, shards, token = _split_start(f"share_start_{group}", shards,
                                                     _share_copies, len(names))
            shares[group] = (names, send, recv, shards)
            after = [token]
        else:
            names, send, recv, shards = shares[group]
            shards = _split_wait(f"share_wait_{group}", shards, send, recv, after, _share_copies)
            after = []
            for k, g_ in zip(names, shards):
                g_, d_, m_, v_ = _adamw(f"adamw_{k}", w[k][0], g_, m[k][0], v[k][0])
                grad[k], delta[k], new_m[k], new_v[k] = g_[None], d_[None], m_[None], v_[None]
                after.append(v_)

    packed, slots = _split_wait("small_wait", small_arrays, small_send, small_recv, after,
                                _small_copies)
    me = (4 * cx + 2 * cy + cc).astype(jnp.int32).reshape(1)
    total = _sum_devices("sum_small", me, slots, packed)
    n_small = total.shape[0] - SUBLANE
    loss = total[n_small, 0]
    small_g = total[:n_small]
    g_s, d_s, m_s, v_s = _adamw("adamw_small", _pack([w[k] for k in SMALL]), small_g,
                                _pack([m[k] for k in SMALL]), _pack([v[k] for k in SMALL]))
    like = [w[k] for k in SMALL]
    for k, g_, d_, m_, v_ in zip(SMALL, _unpack(g_s, like), _unpack(d_s, like),
                                 _unpack(m_s, like), _unpack(v_s, like)):
        grad[k], delta[k], new_m[k], new_v[k] = g_, d_, m_, v_

    return (loss, grad_x[None], *[grad[k] for k in WEIGHTS], *[delta[k] for k in WEIGHTS],
            *[new_m[k] for k in WEIGHTS], *[new_v[k] for k in WEIGHTS])
```

```python
import functools
import math

import jax
import jax.numpy as jnp
from jax import lax
from jax.experimental import pallas as pl
from jax.experimental.pallas import tpu as pltpu

F32 = jnp.float32
BF16 = jnp.bfloat16
MESH = pl.DeviceIdType.MESH

EPS = 1e-6
CHUNK = 64
SGU_BLOCK = 128
GROUP_DIM = 128
X_HEADS = 4
N_CHIPS = 4
N_DEV = 8
LANE = 128
SUBLANE = 8
BF16_ROWS = 16

ADAM_LR = 0.001
ADAM_B1 = 0.9
ADAM_B2 = 0.999
ADAM_EPS = 1e-08
ADAM_WD = 0.01
ADAM_STEP = 10

V7X_VMEM_BYTES = 64 << 20
VMEM_LIMIT = V7X_VMEM_BYTES - (8 << 20)


def _params(n_grid):
    return pltpu.CompilerParams(dimension_semantics=("arbitrary",) * n_grid,
                                vmem_limit_bytes=VMEM_LIMIT)


def _pick(pref, dims, unit=None):
    g = functools.reduce(math.gcd, dims)
    if unit is None:
        unit = LANE if g % LANE == 0 else SUBLANE
    cands = [d for d in range(unit, g + 1, unit) if g % d == 0] or [g]
    return min(cands, key=lambda d: abs(math.log(d / pref)))


def _any_spec():
    return pl.BlockSpec(memory_space=pl.ANY)


class Mat:
    def __init__(self, arr, kind="c"):
        if arr.ndim == 2:
            arr = arr[None]
        self.arr, self.kind = arr, kind
        self.P, self.prow, self.pcol = arr.shape
        self.rows = self.prow * (self.P if kind == "r" else 1)
        self.cols = self.pcol * (self.P if kind == "c" else 1)
        self.dtype = arr.dtype

    def spec(self, tr, tc, rc_fn):
        if self.kind == "c":
            per = self.pcol // tc
            assert per * tc == self.pcol, (self.pcol, tc)

            def imap(*g):
                i, j = rc_fn(*g)
                return (j // per, i, j % per)
        else:
            per = self.prow // tr
            assert per * tr == self.prow, (self.prow, tr)

            def imap(*g):
                i, j = rc_fn(*g)
                return (i // per, i % per, j)
        return pl.BlockSpec((None, tr, tc), imap)

    def two_d(self):
        assert self.P == 1
        return self.arr[0]


def _out_mat(kind, P, rows, cols, dtype):
    shape = (P, rows, cols // P) if kind == "c" else (P, rows // P, cols)
    return jax.ShapeDtypeStruct(shape, dtype)


def _matmul(name, A, B, mode, outs, *, tm=1024, tn=1024, tk=2048, extras=(), epi=None):
    if mode == "nn":
        M, K, N = A.rows, A.cols, B.cols
        assert B.rows == K
    elif mode == "nt":
        M, K, N = A.rows, A.cols, B.rows
        assert B.cols == K
    else:
        K, M, N = A.rows, A.cols, B.cols
        assert B.rows == K
    mdims, ndims, kdims = [M], [N], [K]
    whole_b = mode == "nn" and B.kind == "r" and B.P > 1 and K <= tk
    whole_bt = mode == "nt" and B.kind == "r" and B.P > 1 and N <= tn
    if whole_b:
        kdims.append(A.pcol)
        ndims.append(B.pcol)
    elif whole_bt:
        kdims += [A.pcol, B.pcol]
    elif mode == "tn":
        assert A.kind == "c" and B.kind == "c"
        mdims.append(A.pcol)
        ndims.append(B.pcol)
    else:
        (mdims if A.kind == "r" else kdims).append(A.prow if A.kind == "r" else A.pcol)
        if mode == "nn":
            (kdims if B.kind == "r" else ndims).append(B.prow if B.kind == "r" else B.pcol)
        else:
            (ndims if B.kind == "r" else kdims).append(B.prow if B.kind == "r" else B.pcol)
    for o in list(outs) + list(extras):
        if isinstance(o, Mat):
            (mdims if o.kind == "r" else ndims).append(o.prow if o.kind == "r" else o.pcol)
        elif isinstance(o[0], str):
            (mdims if o[0] == "r" else ndims).append((M if o[0] == "r" else N) // o[1])
    tm, tn = _pick(tm, mdims), _pick(tn, ndims)
    tk = K if mode == "tn" else _pick(tk, kdims)
    nk = K // tk
    grid = (M // tm, N // tn, nk)

    if mode == "tn":
        a_spec = A.spec(K, tm, lambda m, n, k: (0, m))
        b_spec = B.spec(K, tn, lambda m, n, k: (0, n))
    else:
        a_spec = A.spec(tm, tk, lambda m, n, k: (m, k))
        if whole_b:
            b_spec = pl.BlockSpec((B.P, B.prow, tn), lambda m, n, k: (0, 0, n))
        elif whole_bt:
            b_spec = pl.BlockSpec((B.P, B.prow, tk), lambda m, n, k: (0, 0, k))
        elif mode == "nn":
            b_spec = B.spec(tk, tn, lambda m, n, k: (k, n))
        else:
            b_spec = B.spec(tn, tk, lambda m, n, k: (n, k))

    def mn_spec(o):
        if isinstance(o, Mat):
            return o.spec(tm, tn, lambda m, n, k: (m, n))
        if isinstance(o[0], str):
            kind, P = o[0], o[1]
            fake = Mat.__new__(Mat)
            fake.kind, fake.P = kind, P
            fake.prow = M // P if kind == "r" else M
            fake.pcol = N // P if kind == "c" else N
            return Mat.spec(fake, tm, tn, lambda m, n, k: (m, n))
        return o[1](tm, tn)

    out_shapes = tuple(_out_mat(o[0], o[1], M, N, o[2]) if isinstance(o[0], str) else o[0]
                       for o in outs)
    out_specs = tuple(mn_spec(o) for o in outs)
    extra_arrays = tuple(e.arr if isinstance(e, Mat) else e[0] for e in extras)
    extra_specs = tuple(mn_spec(e) for e in extras)
    n_ex, n_out = len(extras), len(outs)
    tt = _pick(256, [tm])
    dims = (((1,), (1 if mode == "nt" else 0,)), ((), ()))

    def body(*refs):
        a_ref, b_ref = refs[:2]
        ex_refs = refs[2:2 + n_ex]
        out_refs = refs[2 + n_ex:2 + n_ex + n_out]
        scratch = refs[2 + n_ex + n_out:]
        if mode == "tn":
            at_ref = scratch[0]

            @pl.when(pl.program_id(1) == 0)
            def _():
                for c0 in range(0, tm, tt):
                    at_ref[c0:c0 + tt, :] = a_ref[:, c0:c0 + tt].astype(F32).T.astype(BF16)

            lhs = at_ref[...]
        else:
            lhs = a_ref[...].astype(BF16)
        rhs = b_ref[...]
        if whole_b or whole_bt:
            rhs = rhs.reshape(B.P * B.prow, rhs.shape[-1])
        part = lax.dot_general(lhs, rhs.astype(BF16), dims, preferred_element_type=F32)

        def finish(acc):
            if epi is None:
                out_refs[0][...] = acc.astype(out_refs[0].dtype)
            else:
                epi(acc, ex_refs, out_refs)

        if nk == 1:
            finish(part)
        else:
            acc_ref = scratch[0]
            k = pl.program_id(2)

            @pl.when(k == 0)
            def _():
                acc_ref[...] = part

            @pl.when(k > 0)
            def _():
                acc_ref[...] += part

            @pl.when(k == nk - 1)
            def _():
                finish(acc_ref[...])

    scratch_shapes = []
    if mode == "tn":
        scratch_shapes.append(pltpu.VMEM((tm, K), BF16))
    elif nk > 1:
        scratch_shapes.append(pltpu.VMEM((tm, tn), F32))
    res = pl.pallas_call(
        body, name=name, grid=grid,
        in_specs=[a_spec, b_spec, *extra_specs], out_specs=out_specs, out_shape=out_shapes,
        scratch_shapes=scratch_shapes, compiler_params=_params(3),
    )(A.arr, B.arr, *extra_arrays)
    return res


def _row_tile(T, streams=5):
    return _pick(512 if streams <= 3 else 256, [T])


def _tie(name, x, deps):
    def body(*refs):
        refs[-1][...] = jnp.zeros_like(refs[-1])

    return pl.pallas_call(
        body, name=name, in_specs=[_any_spec()] * (1 + len(deps)),
        out_specs=(_any_spec(), pl.BlockSpec(memory_space=pltpu.VMEM)),
        out_shape=(jax.ShapeDtypeStruct(x.shape, x.dtype),
                   jax.ShapeDtypeStruct((SUBLANE, LANE), F32)),
        input_output_aliases={0: 0},
    )(x, *deps)[0]


def _rmsnorm_fwd(name, x, g, *, into=None, col=0, deps=()):
    T, W = x.shape
    tr = _row_tile(T, streams=2)

    def body(x_ref, g_ref, *rest):
        o_ref = rest[-1]
        xv = x_ref[...]
        rstd = lax.rsqrt(jnp.mean(xv * xv, axis=-1, keepdims=True) + EPS)
        o_ref[...] = (xv * rstd * g_ref[...]).astype(o_ref.dtype)

    in_specs = [pl.BlockSpec((tr, W), lambda i: (i, 0)), pl.BlockSpec((1, W), lambda i: (0, 0))]
    args = [x, g]
    kwargs = {}
    if into is None:
        out_shape = jax.ShapeDtypeStruct((T, W), BF16)
    else:
        out_shape = jax.ShapeDtypeStruct(into.shape, into.dtype)
        in_specs.append(_any_spec())
        args.append(into)
        kwargs["input_output_aliases"] = {2: 0}
    in_specs += [_any_spec()] * len(deps)
    args += list(deps)
    return pl.pallas_call(
        body, name=name, grid=(T // tr,), in_specs=in_specs,
        out_specs=pl.BlockSpec((tr, W), lambda i: (i, col)), out_shape=out_shape,
        compiler_params=_params(1), **kwargs)(*args)


def _rmsnorm_bwd(name, x, g, dn, *, dn_col=0, dres=None, want_dx=True, want_bf16=True):
    T, W = x.shape
    tr = _row_tile(T)
    has_res = dres is not None

    def body(*refs):
        x_ref, g_ref, dn_ref = refs[:3]
        pos = 3
        dres_ref = None
        if has_res:
            dres_ref = refs[pos]
            pos += 1
        outs = refs[pos:]
        dg_ref = outs[-1]
        xv = x_ref[...]
        rstd = lax.rsqrt(jnp.mean(xv * xv, axis=-1, keepdims=True) + EPS)
        xhat = xv * rstd
        dnv = dn_ref[...].astype(F32)

        @pl.when(pl.program_id(0) == 0)
        def _():
            dg_ref[...] = jnp.zeros_like(dg_ref)

        dg_ref[...] += jnp.sum(dnv * xhat, axis=0, keepdims=True)
        if want_dx:
            t = dnv * g_ref[...]
            dx = rstd * (t - xhat * jnp.mean(t * xhat, axis=-1, keepdims=True))
            if has_res:
                dx = dx + dres_ref[...]
            outs[0][...] = dx
            if want_bf16:
                outs[1][...] = dx.astype(BF16)

    row = pl.BlockSpec((tr, W), lambda i: (i, 0))
    in_specs = [row, pl.BlockSpec((1, W), lambda i: (0, 0)),
                pl.BlockSpec((tr, W), lambda i: (i, dn_col))]
    args = [x, g, dn]
    if has_res:
        in_specs.append(row)
        args.append(dres)
    out_shape, out_specs = [], []
    if want_dx:
        out_shape.append(jax.ShapeDtypeStruct((T, W), F32))
        out_specs.append(row)
        if want_bf16:
            out_shape.append(jax.ShapeDtypeStruct((T, W), BF16))
            out_specs.append(row)
    out_shape.append(jax.ShapeDtypeStruct((1, W), F32))
    out_specs.append(pl.BlockSpec((1, W), lambda i: (0, 0)))
    return pl.pallas_call(
        body, name=name, grid=(T // tr,), in_specs=in_specs, out_specs=out_specs,
        out_shape=out_shape, compiler_params=_params(1))(*args)


def _loss_head(name, h, g, target):
    T, W = h.shape
    tr = _row_tile(T)

    def body(h_ref, g_ref, t_ref, loss_ref, dx_ref, dxb_ref, dg_ref):
        xv = h_ref[...]
        gv = g_ref[...]
        rstd = lax.rsqrt(jnp.mean(xv * xv, axis=-1, keepdims=True) + EPS)
        xhat = xv * rstd
        diff = xhat * gv - t_ref[...]

        @pl.when(pl.program_id(0) == 0)
        def _():
            dg_ref[...] = jnp.zeros_like(dg_ref)
            loss_ref[...] = jnp.zeros_like(loss_ref)

        loss_ref[...] += 0.5 * jnp.sum(jnp.mean(diff * diff, axis=-1, keepdims=True))
        dnv = diff * (1.0 / W)
        dg_ref[...] += jnp.sum(dnv * xhat, axis=0, keepdims=True)
        t = dnv * gv
        dx = rstd * (t - xhat * jnp.mean(t * xhat, axis=-1, keepdims=True))
        dx_ref[...] = dx
        dxb_ref[...] = dx.astype(BF16)

    row = pl.BlockSpec((tr, W), lambda i: (i, 0))
    vec = pl.BlockSpec((1, W), lambda i: (0, 0))
    return pl.pallas_call(
        body, name=name, grid=(T // tr,), in_specs=[row, vec, row],
        out_specs=[pl.BlockSpec((SUBLANE, LANE), lambda i: (0, 0)), row, row, vec],
        out_shape=[jax.ShapeDtypeStruct((SUBLANE, LANE), F32), jax.ShapeDtypeStruct((T, W), F32),
                   jax.ShapeDtypeStruct((T, W), BF16), jax.ShapeDtypeStruct((1, W), F32)],
        compiler_params=_params(1))(h, g, target)


def _sigmoid(x):
    return 1.0 / (1.0 + jnp.exp(-x))


def _ffn_in(name, n, W, place, half, prev=None):
    T, D = n.shape
    F = W.cols // 2
    tm = _pick(2048, [T])
    tn = _pick(512, [W.pcol])
    per = W.pcol // tn

    def body(place_ref, a_ref, wg_ref, wu_ref, *rest):
        gu_ref, act_ref = rest[-2:]
        a = a_ref[...]
        gate = jnp.dot(a, wg_ref[...], preferred_element_type=F32)
        up = jnp.dot(a, wu_ref[...], preferred_element_type=F32)
        sig = _sigmoid(gate)
        silu = gate * sig
        gu_ref[0] = (up * sig * (1.0 + gate * (1.0 - sig))).astype(BF16)
        gu_ref[1] = silu.astype(BF16)
        act_ref[...] = (silu * up).astype(BF16)

    def pair(pr):
        return (pr[0] + half) % 2

    in_specs = [pl.BlockSpec((tm, D), lambda m, j, pr: (m, 0)),
                pl.BlockSpec((None, D, tn), lambda m, j, pr: (pair(pr), 0, j)),
                pl.BlockSpec((None, D, tn), lambda m, j, pr: (2 + pair(pr), 0, j))]
    args = [place, n, W.arr, W.arr]
    kwargs = {}
    if prev is not None:
        in_specs += [_any_spec(), _any_spec()]
        args += list(prev)
        kwargs["input_output_aliases"] = {4: 0, 5: 1}
    grid_spec = pltpu.PrefetchScalarGridSpec(
        num_scalar_prefetch=1, grid=(T // tm, per), in_specs=in_specs,
        out_specs=[pl.BlockSpec((2, tm, tn), lambda m, j, pr: (0, m, pair(pr) * per + j)),
                   pl.BlockSpec((tm, tn), lambda m, j, pr: (m, pair(pr) * per + j))])
    return pl.pallas_call(
        body, name=name, grid_spec=grid_spec,
        out_shape=[jax.ShapeDtypeStruct((2, T, F), BF16), jax.ShapeDtypeStruct((T, F), BF16)],
        compiler_params=_params(2), **kwargs)(*args)


def _ffn_forward(tag, h, norm_g, weights_of, place, deps=()):
    n = _rmsnorm_fwd(f"{tag}_norm", h, norm_g, deps=deps)
    got = weights_of(f"{tag}_in", n)
    gu, act = _ffn_in(f"{tag}_in_a", n, got[f"{tag}_w_in"], place, 0)
    w_in = got["finish"](act)[f"{tag}_w_in"]
    gu, act = _ffn_in(f"{tag}_in_b", n, w_in, place, 1, (gu, act))
    w_out = weights_of(f"{tag}_out", act)[f"{tag}_w_out"]

    def epi(acc, ex, out):
        out[0][...] = ex[0][...] + 0.5 * acc

    (h_out,) = _matmul(f"{tag}_out", Mat(act), w_out, "nn", [("c", 1, F32)],
                       tm=1024, tn=512, tk=8192, extras=[Mat(h)], epi=epi)
    return h_out[0], (n, gu, act, w_in, w_out)


def _ffn_backward(tag, h_in, norm_g, saved, dh, dh_bf, grads_ready, grads_flush,
                  early_out=False):
    n, gu, act, w_in, w_out = saved
    T, F = act.shape

    def epi(acc, ex, out):
        dact = 0.5 * acc
        out[0][0] = (dact * ex[0][0].astype(F32)).astype(BF16)
        out[0][1] = (dact * ex[0][1].astype(F32)).astype(BF16)

    def pair_spec(tm, tn):
        return pl.BlockSpec((2, tm, tn), lambda m, j, k: (0, m, j))

    def half(acc, ex, out):
        out[0][...] = (0.5 * acc).astype(out[0].dtype)

    (dw_out,) = _matmul(f"{tag}_dwout", Mat(act), Mat(dh_bf), "tn", [("r", N_CHIPS, BF16)],
                        tm=1408, tn=512, epi=half)
    if early_out:
        token = grads_ready(f"{tag}_out", {f"{tag}_w_out": dw_out})
        dh_bf = _tie(f"{tag}_dh_after_swap", dh_bf, [token])
    (dgu,) = _matmul(f"{tag}_dact", Mat(dh_bf), w_out, "nt",
                     [(jax.ShapeDtypeStruct((2, T, F), BF16), pair_spec)],
                     tm=512, tn=1408, extras=[(gu, pair_spec)], epi=epi)
    if early_out:
        dgu = _tie(f"{tag}_dgu_after_scatter", dgu, [grads_flush(f"{tag}_out", dgu)])
    (dw_in,) = _matmul(f"{tag}_dwin", Mat(n), Mat(dgu), "tn", [("c", N_CHIPS, BF16)],
                       tm=1024, tn=1408)
    if early_out:
        group, partial = f"{tag}_in", {f"{tag}_w_in": dw_in}
    else:
        group, partial = tag, {f"{tag}_w_in": dw_in, f"{tag}_w_out": dw_out}
    dgu = _tie(f"{tag}_dgu_after_swap", dgu, [grads_ready(group, partial)])
    (dn,) = _matmul(f"{tag}_dn", Mat(dgu), w_in, "nt", [("c", 1, F32)],
                    tm=1024, tn=1024, tk=2816)
    dn = _tie(f"{tag}_dn_after_scatter", dn, [grads_flush(group, dn)])
    return _rmsnorm_bwd(f"{tag}_dnorm", h_in, norm_g, dn[0], dres=dh)


_GELU_C = math.sqrt(2.0 / math.pi)
_GELU_A = 0.044715


def _gelu(x):
    return 0.5 * x * (1.0 + jnp.tanh(_GELU_C * (x + _GELU_A * x * x * x)))


def _gelu_grad(x):
    th = jnp.tanh(_GELU_C * (x + _GELU_A * x * x * x))
    return 0.5 * (1.0 + th) + 0.5 * x * (1.0 - th * th) * _GELU_C * (1.0 + 3.0 * _GELU_A * x * x)


def _chunk_mask():
    t = lax.broadcasted_iota(jnp.int32, (SGU_BLOCK, SGU_BLOCK), 0) // CHUNK
    s = lax.broadcasted_iota(jnp.int32, (SGU_BLOCK, SGU_BLOCK), 1) // CHUNK
    return s <= t


def _sgu_group_forward(v_g, lg, lb, wm_bf, b_col):
    mu = jnp.mean(v_g, axis=-1, keepdims=True)
    xc = v_g - mu
    rstd = lax.rsqrt(jnp.mean(xc * xc, axis=-1, keepdims=True) + EPS)
    vhat = xc * rstd
    vn = vhat * lg + lb
    mixed = jnp.dot(wm_bf, vn.astype(BF16), preferred_element_type=F32) + b_col
    return vhat, rstd, vn, mixed


def _sgu_forward(name, z, ln_g, ln_b, w_s, b_t, gn, d_model):
    T = z.shape[0]
    W_A = ln_g.shape[1]
    G = W_A // GROUP_DIM

    def body(z_ref, lg_ref, lb_ref, w_ref, bt_ref, gn_ref, y_ref):
        mask = _chunk_mask()
        u = _gelu(z_ref[:, :W_A])
        v = _gelu(z_ref[:, W_A:])
        cols = []
        for g in range(G):
            sl = slice(g * GROUP_DIM, (g + 1) * GROUP_DIM)
            wm = jnp.where(mask, w_ref[g], 0.0).astype(BF16)
            _, _, _, mixed = _sgu_group_forward(v[:, sl], lg_ref[:, sl], lb_ref[:, sl], wm,
                                                bt_ref[:, g:g + 1])
            cols.append(u[:, sl] * mixed)
        ya = jnp.concatenate(cols, axis=1)
        rstd = lax.rsqrt(jnp.mean(ya * ya, axis=-1, keepdims=True) + EPS)
        y_ref[...] = (ya * rstd * gn_ref[...]).astype(BF16)

    vec = pl.BlockSpec((1, W_A), lambda i: (0, 0))
    return pl.pallas_call(
        body, name=name, grid=(T // SGU_BLOCK,),
        in_specs=[pl.BlockSpec((SGU_BLOCK, 2 * W_A), lambda i: (i, 0)), vec, vec,
                  pl.BlockSpec((G, SGU_BLOCK, SGU_BLOCK), lambda i: (0, 0, 0)),
                  pl.BlockSpec((SGU_BLOCK, G), lambda i: (0, 0)), vec],
        out_specs=pl.BlockSpec((SGU_BLOCK, W_A), lambda i: (i, 0)),
        out_shape=jax.ShapeDtypeStruct((T, d_model), BF16),
        compiler_params=_params(1))(z, ln_g, ln_b, w_s, b_t, gn)


def _sgu_backward(name, z, dy, ln_g, ln_b, w_s, b_t, gn):
    T = z.shape[0]
    W_A = ln_g.shape[1]
    G = W_A // GROUP_DIM

    def body(z_ref, dy_ref, lg_ref, lb_ref, w_ref, bt_ref, gn_ref,
             dz_ref, dlg_ref, dlb_ref, dw_ref, db_ref, dgn_ref):
        @pl.when(pl.program_id(0) == 0)
        def _():
            for r in (dlg_ref, dlb_ref, dw_ref, db_ref, dgn_ref):
                r[...] = jnp.zeros_like(r)

        mask = _chunk_mask()
        zu = z_ref[:, :W_A]
        zv = z_ref[:, W_A:]
        u = _gelu(zu)
        v = _gelu(zv)
        saved, cols = [], []
        for g in range(G):
            sl = slice(g * GROUP_DIM, (g + 1) * GROUP_DIM)
            wm = jnp.where(mask, w_ref[g], 0.0)
            vhat, rstd, vn, mixed = _sgu_group_forward(
                v[:, sl], lg_ref[:, sl], lb_ref[:, sl], wm.astype(BF16), bt_ref[:, g:g + 1])
            saved.append((wm, vhat, rstd, vn, mixed))
            cols.append(u[:, sl] * mixed)
        ya = jnp.concatenate(cols, axis=1)
        rstd_a = lax.rsqrt(jnp.mean(ya * ya, axis=-1, keepdims=True) + EPS)
        ya_hat = ya * rstd_a
        dyv = dy_ref[...].astype(F32)
        dgn_ref[...] += jnp.sum(dyv * ya_hat, axis=0, keepdims=True)
        t = dyv * gn_ref[...]
        dya = rstd_a * (t - ya_hat * jnp.mean(t * ya_hat, axis=-1, keepdims=True))
        du_cols, dv_cols, dlg_cols, dlb_cols = [], [], [], []
        for g in range(G):
            sl = slice(g * GROUP_DIM, (g + 1) * GROUP_DIM)
            wm, vhat, rstd, vn, mixed = saved[g]
            dya_g = dya[:, sl]
            du_cols.append(dya_g * mixed)
            dmix = dya_g * u[:, sl]
            dmix_bf = dmix.astype(BF16)
            db_ref[g] += jnp.sum(dmix, axis=1, keepdims=True)
            dw = lax.dot_general(dmix_bf, vn.astype(BF16), (((1,), (1,)), ((), ())),
                                 preferred_element_type=F32)
            dw_ref[g] += jnp.where(mask, dw, 0.0)
            dvn = jnp.dot(wm.T.astype(BF16), dmix_bf, preferred_element_type=F32)
            dlg_cols.append(jnp.sum(dvn * vhat, axis=0, keepdims=True))
            dlb_cols.append(jnp.sum(dvn, axis=0, keepdims=True))
            dvhat = dvn * lg_ref[:, sl]
            dv_cols.append(rstd * (dvhat - jnp.mean(dvhat, axis=-1, keepdims=True)
                                   - vhat * jnp.mean(dvhat * vhat, axis=-1, keepdims=True)))
        dlg_ref[...] += jnp.concatenate(dlg_cols, axis=1)
        dlb_ref[...] += jnp.concatenate(dlb_cols, axis=1)
        dz_ref[:, :W_A] = (jnp.concatenate(du_cols, axis=1) * _gelu_grad(zu)).astype(BF16)
        dz_ref[:, W_A:] = (jnp.concatenate(dv_cols, axis=1) * _gelu_grad(zv)).astype(BF16)

    vec = pl.BlockSpec((1, W_A), lambda i: (0, 0))
    wspec = pl.BlockSpec((G, SGU_BLOCK, SGU_BLOCK), lambda i: (0, 0, 0))
    return pl.pallas_call(
        body, name=name, grid=(T // SGU_BLOCK,),
        in_specs=[pl.BlockSpec((SGU_BLOCK, 2 * W_A), lambda i: (i, 0)),
                  pl.BlockSpec((SGU_BLOCK, W_A), lambda i: (i, 0)), vec, vec, wspec,
                  pl.BlockSpec((SGU_BLOCK, G), lambda i: (0, 0)), vec],
        out_specs=[pl.BlockSpec((SGU_BLOCK, 2 * W_A), lambda i: (i, 0)), vec, vec, wspec,
                   pl.BlockSpec((G, SGU_BLOCK, 1), lambda i: (0, 0, 0)), vec],
        out_shape=[jax.ShapeDtypeStruct((T, 2 * W_A), BF16), jax.ShapeDtypeStruct((1, W_A), F32),
                   jax.ShapeDtypeStruct((1, W_A), F32),
                   jax.ShapeDtypeStruct((G, SGU_BLOCK, SGU_BLOCK), F32),
                   jax.ShapeDtypeStruct((G, SGU_BLOCK, 1), F32),
                   jax.ShapeDtypeStruct((1, W_A), F32)],
        compiler_params=_params(1))(z, dy, ln_g, ln_b, w_s, b_t, gn)


def _split_dot(x, tri):
    hi = x.astype(BF16)
    lo = (x - hi.astype(F32)).astype(BF16)
    return (jnp.dot(hi, tri, preferred_element_type=F32)
            + jnp.dot(lo, tri, preferred_element_type=F32))


def _tri(n, rel):
    r = lax.broadcasted_iota(jnp.int32, (n, n), 0)
    c = lax.broadcasted_iota(jnp.int32, (n, n), 1)
    return rel(r, c).astype(BF16)


def _dot_nt(a, b):
    return lax.dot_general(a, b, (((1,), (1,)), ((), ())), preferred_element_type=F32)


def _dot_tn(a, b):
    return lax.dot_general(a, b, (((0,), (0,)), ((), ())), preferred_element_type=F32)


def _sb_scores(qs, kj, mask):
    zz = _dot_nt(qs, kj)
    log_beta = jnp.minimum(zz, 0.0) - jnp.log(1.0 + jnp.exp(-jnp.abs(zz)))
    log_1m = log_beta - zz
    if mask is not None:
        log_1m = jnp.where(mask, log_1m, 0.0)
    return log_beta, log_1m


def _masked(mask, x):
    return x if mask is None else jnp.where(mask, x, 0.0)


def _below(old, new, row0):
    if row0 == 0:
        return tuple(new)
    return tuple(jnp.concatenate([o[:row0], n], axis=0) for o, n in zip(old, new))


def _sb_tiles(T):
    tk = _pick(256, [T])
    tq = 2 * tk if T % (2 * tk) == 0 else tk
    return tq, tk


def _sb_cols(w_a, w_b):
    base = 2 * w_a // GROUP_DIM
    per = w_b // GROUP_DIM
    return base, base + per, base + 2 * per


def _sb_forward(name, z, w_a, w_b):
    T = z.shape[0]
    H = w_b // GROUP_DIM
    tq, tk = _sb_tiles(T)
    per = tq // tk
    qc, kc, vc = _sb_cols(w_a, w_b)
    scale = GROUP_DIM ** -0.5

    def body(q_ref, k_ref, v_ref, y_ref, tot_ref):
        i = pl.program_id(1)
        qs = (q_ref[...] * scale).astype(BF16)
        upper = _tri(tk, lambda r, c: r > c)
        ahead = (lax.broadcasted_iota(jnp.int32, (tq, tk), 1)
                 - lax.broadcasted_iota(jnp.int32, (tq, tk), 0))

        def step(j, carry, masked, row0=0):
            acc, later = (c[row0:] for c in carry)
            k0 = pl.multiple_of(j * tk, tk)
            kj = k_ref[pl.ds(k0, tk), :].astype(BF16)
            vj = v_ref[pl.ds(k0, tk), :].astype(BF16)
            mask = ahead[row0:] < i * tq - k0 if masked else None
            log_beta, log_1m = _sb_scores(qs[row0:], kj, mask)
            rest = _split_dot(log_1m, upper) + later
            a = _masked(mask, jnp.exp(log_beta + rest))
            acc = acc + jnp.dot(a.astype(BF16), vj, preferred_element_type=F32)
            later = later + jnp.sum(log_1m, axis=1, keepdims=True)
            return _below(carry, (acc, later), row0)

        def blocks(p, c):
            for d in reversed(range(per)):
                c = step(p * per + d, c, False)
            return c

        carry = (jnp.zeros((tq, GROUP_DIM), F32), jnp.zeros((tq, 1), F32))
        for d in reversed(range(per)):
            carry = step(i * per + d, carry, True, d * tk)
        acc, total = lax.fori_loop(0, i, lambda pp, c: blocks(i - 1 - pp, c), carry)
        y_ref[...] = acc
        tot_ref[...] = total

    return pl.pallas_call(
        body, name=name, grid=(H, T // tq),
        in_specs=[pl.BlockSpec((tq, GROUP_DIM), lambda h, i: (i, qc + h)),
                  pl.BlockSpec((T, GROUP_DIM), lambda h, i: (0, kc + h)),
                  pl.BlockSpec((T, GROUP_DIM), lambda h, i: (0, vc + h))],
        out_specs=[pl.BlockSpec((tq, GROUP_DIM), lambda h, i: (i, h)),
                   pl.BlockSpec((None, tq, 1), lambda h, i: (h, i, 0))],
        out_shape=[jax.ShapeDtypeStruct((T, w_b), F32), jax.ShapeDtypeStruct((H, T, 1), F32)],
        compiler_params=_params(2))(z, z, z)


def _sb_backward(name, z, do, total, w_a, w_b):
    T = z.shape[0]
    H = w_b // GROUP_DIM
    tq, tk = _sb_tiles(T)
    per = tq // tk
    qc, kc, vc = _sb_cols(w_a, w_b)
    scale = GROUP_DIM ** -0.5

    def body(q_ref, k_ref, v_ref, do_ref, tot_ref, dq_ref, dkv_out_ref, dkv_ref):
        i = pl.program_id(1)

        @pl.when(i == 0)
        def _():
            dkv_ref[...] = jnp.zeros_like(dkv_ref)

        qs = (q_ref[...] * scale).astype(BF16)
        dob = do_ref[...].astype(BF16)
        upto = _tri(tk, lambda r, c: r <= c)
        before = _tri(tk, lambda r, c: r < c)
        ahead = (lax.broadcasted_iota(jnp.int32, (tq, tk), 1)
                 - lax.broadcasted_iota(jnp.int32, (tq, tk), 0))

        def step(j, carry, masked, row0=0):
            dq, left, e_seen = (c[row0:] for c in carry)
            qr, dor = qs[row0:], dob[row0:]
            k0 = pl.multiple_of(j * tk, tk)
            kj = k_ref[pl.ds(k0, tk), :].astype(BF16)
            vj = v_ref[pl.ds(k0, tk), :].astype(BF16)
            mask = ahead[row0:] < i * tq - k0 if masked else None
            log_beta, log_1m = _sb_scores(qr, kj, mask)
            rest = left - _split_dot(log_1m, upto)
            a = _masked(mask, jnp.exp(log_beta + rest))
            e = a * _dot_nt(dor, vj)
            e_before = e_seen + jnp.dot(e.astype(BF16), before, preferred_element_type=F32)
            beta = jnp.exp(log_beta)
            dz = _masked(mask, e * (1.0 - beta) - beta * e_before).astype(BF16)
            dq = dq + jnp.dot(dz, kj, preferred_element_type=F32)
            dkv_ref[0, pl.ds(k0, tk), :] += _dot_tn(dz, qr)
            dkv_ref[1, pl.ds(k0, tk), :] += _dot_tn(a.astype(BF16), dor)
            left = left - jnp.sum(log_1m, axis=1, keepdims=True)
            e_seen = e_seen + jnp.sum(e, axis=1, keepdims=True)
            return _below(carry, (dq, left, e_seen), row0)

        def blocks(p, c):
            for d in range(per):
                c = step(p * per + d, c, False)
            return c

        carry = (jnp.zeros((tq, GROUP_DIM), F32), tot_ref[...], jnp.zeros((tq, 1), F32))
        carry = lax.fori_loop(0, i, blocks, carry)
        for d in range(per):
            carry = step(i * per + d, carry, True, d * tk)
        dq_ref[...] = (carry[0] * scale).astype(BF16)

        @pl.when(i == T // tq - 1)
        def _():
            dkv_out_ref[...] = dkv_ref[...].astype(BF16)

    return pl.pallas_call(
        body, name=name, grid=(H, T // tq),
        in_specs=[pl.BlockSpec((tq, GROUP_DIM), lambda h, i: (i, qc + h)),
                  pl.BlockSpec((T, GROUP_DIM), lambda h, i: (0, kc + h)),
                  pl.BlockSpec((T, GROUP_DIM), lambda h, i: (0, vc + h)),
                  pl.BlockSpec((tq, GROUP_DIM), lambda h, i: (i, h)),
                  pl.BlockSpec((None, tq, 1), lambda h, i: (h, i, 0))],
        out_specs=[pl.BlockSpec((tq, GROUP_DIM), lambda h, i: (i, h)),
                   pl.BlockSpec((2, T, GROUP_DIM), lambda h, i: (0, 0, h))],
        out_shape=[jax.ShapeDtypeStruct((T, w_b), BF16), jax.ShapeDtypeStruct((2, T, w_b), BF16)],
        scratch_shapes=[pltpu.VMEM((2, T, GROUP_DIM), F32)],
        compiler_params=_params(2))(z, z, z, do, total)


def _softmax_rows(s):
    m = jnp.max(s, axis=-1, keepdims=True)
    p = jnp.exp(s - m)
    return p / jnp.sum(p, axis=-1, keepdims=True)


def _xattn_forward(name, q, kv):
    T, D = q.shape
    Nm = kv.shape[0]
    dh = D // X_HEADS
    tq = _pick(1024, [T])

    def body(q_ref, k_ref, v_ref, o_ref):
        p = _softmax_rows(_dot_nt(q_ref[...], k_ref[...]))
        o_ref[...] = jnp.dot(p.astype(BF16), v_ref[...], preferred_element_type=F32).astype(BF16)

    return pl.pallas_call(
        body, name=name, grid=(T // tq, X_HEADS),
        in_specs=[pl.BlockSpec((tq, dh), lambda i, h: (i, h)),
                  pl.BlockSpec((Nm, dh), lambda i, h: (0, h)),
                  pl.BlockSpec((Nm, dh), lambda i, h: (0, X_HEADS + h))],
        out_specs=pl.BlockSpec((tq, dh), lambda i, h: (i, h)),
        out_shape=jax.ShapeDtypeStruct((T, D), BF16),
        compiler_params=_params(2))(q, kv, kv)


def _xattn_backward(name, q, kv, do):
    T, D = q.shape
    Nm = kv.shape[0]
    dh = D // X_HEADS
    tq = _pick(1024, [T])
    scale = dh ** -0.5

    def body(q_ref, k_ref, v_ref, do_ref, dq_ref, dkv_ref):
        @pl.when(pl.program_id(1) == 0)
        def _():
            dkv_ref[...] = jnp.zeros_like(dkv_ref)

        qv, kk, vv, dov = q_ref[...], k_ref[...], v_ref[...], do_ref[...]
        p = _softmax_rows(_dot_nt(qv, kk))
        dp = _dot_nt(dov, vv)
        ds = (p * (dp - jnp.sum(dp * p, axis=-1, keepdims=True))).astype(BF16)
        dq_ref[...] = (jnp.dot(ds, kk, preferred_element_type=F32) * scale).astype(BF16)
        dkv_ref[0] += _dot_tn(ds, qv)
        dkv_ref[1] += _dot_tn(p.astype(BF16), dov)

    blk = pl.BlockSpec((tq, dh), lambda h, i: (i, h))
    return pl.pallas_call(
        body, name=name, grid=(X_HEADS, T // tq),
        in_specs=[blk, pl.BlockSpec((Nm, dh), lambda h, i: (0, h)),
                  pl.BlockSpec((Nm, dh), lambda h, i: (0, X_HEADS + h)), blk],
        out_specs=[blk, pl.BlockSpec((2, Nm, dh), lambda h, i: (0, 0, h))],
        out_shape=[jax.ShapeDtypeStruct((T, D), BF16), jax.ShapeDtypeStruct((2, Nm, D), F32)],
        compiler_params=_params(2))(q, kv, kv, do)


def _position():
    x, y, c = lax.axis_index("x"), lax.axis_index("y"), lax.axis_index("c")
    other_chips = [(1 - x, y), (x, 1 - y), (1 - x, 1 - y)]
    return x, y, c, other_chips


def _hbm_spec():
    return pl.BlockSpec(memory_space=pltpu.HBM)


def _sem_spec():
    return pl.BlockSpec(memory_space=pltpu.SEMAPHORE)


def _split_start(name, arrays, make_copies, n_sems, deps=()):
    n, d = len(arrays), len(deps)

    def body(*refs):
        ins = refs[:n]
        send_sems, recv_sems = refs[n + d], refs[n + d + 1]
        token = refs[-1]
        for cp in make_copies(ins, send_sems, recv_sems):
            cp.start()
        token[...] = jnp.zeros_like(token)

    res = pl.pallas_call(
        body, name=name,
        out_shape=(pltpu.SemaphoreType.DMA((n_sems,)), pltpu.SemaphoreType.DMA((n_sems,)),
                   *[pltpu.HBM(a.shape, a.dtype) for a in arrays],
                   jax.ShapeDtypeStruct((SUBLANE, LANE), F32)),
        in_specs=[_hbm_spec()] * n + [_any_spec()] * d,
        out_specs=(_sem_spec(), _sem_spec(), *[_hbm_spec()] * n,
                   pl.BlockSpec(memory_space=pltpu.VMEM)),
        input_output_aliases={i: 2 + i for i in range(n)},
        compiler_params=pltpu.CompilerParams(
            has_side_effects=pltpu.SideEffectType.DATAFLOW_SIDE_EFFECTING),
    )(*[pltpu.with_memory_space_constraint(a, pltpu.HBM) for a in arrays], *deps)
    return res[0], res[1], list(res[2:2 + n]), res[-1]


def _split_wait(name, arrays, send_sems, recv_sems, after, make_copies):
    n = len(arrays)
    after = list(after) if isinstance(after, (list, tuple)) else [after]

    def body(*refs):
        ins = refs[:n]
        send_ref, recv_ref = refs[n], refs[n + 1]
        for cp in make_copies(ins, send_ref, recv_ref):
            cp.wait_send()
            cp.wait_recv()

    return pl.pallas_call(
        body, name=name,
        out_shape=tuple(pltpu.HBM(a.shape, a.dtype) for a in arrays),
        in_specs=[_hbm_spec()] * n + [_sem_spec(), _sem_spec()] + [_any_spec()] * len(after),
        out_specs=tuple(_hbm_spec() for _ in arrays),
        input_output_aliases={i: i for i in range(n)},
        compiler_params=pltpu.CompilerParams(
            has_side_effects=pltpu.SideEffectType.DATAFLOW_SIDE_EFFECTING),
    )(*arrays, send_sems, recv_sems, *after)


def _gather_copies(refs, send_sems, recv_sems):
    x, y, c, chips = _position()
    me = 2 * x + y
    copies = []
    for i, ref in enumerate(refs):
        rows = ref.shape[1] // 2
        piece = ref.at[me, pl.ds(c * rows, rows), :]
        for j, (px, py) in enumerate(chips):
            copies.append(pltpu.make_async_remote_copy(
                src_ref=piece, dst_ref=piece, send_sem=send_sems.at[3 * i + j],
                recv_sem=recv_sems.at[3 * i + j], device_id=(px, py, c), device_id_type=MESH))
    return copies


def _near_copies(refs, send_sems, recv_sems):
    x, y, c, chips = _position()
    me = 2 * x + y
    copies = []
    for i, ref in enumerate(refs):
        rows = ref.shape[1] // 2
        piece = ref.at[me, pl.ds(c * rows, rows), :]
        for j, (px, py) in enumerate(chips[:2]):
            copies.append(pltpu.make_async_remote_copy(
                src_ref=piece, dst_ref=piece, send_sem=send_sems.at[2 * i + j],
                recv_sem=recv_sems.at[2 * i + j], device_id=(px, py, c), device_id_type=MESH))
    return copies


def _relay_copies(refs, send_sems, recv_sems):
    x, y, c, chips = _position()
    copies = []
    for i, ref in enumerate(refs):
        rows = ref.shape[1] // 4
        for j, (px, py) in enumerate(chips[:2]):
            ox, oy = chips[1 - j]
            piece = ref.at[2 * ox + oy, pl.ds((2 * c + j) * rows, rows), :]
            copies.append(pltpu.make_async_remote_copy(
                src_ref=piece, dst_ref=piece, send_sem=send_sems.at[2 * i + j],
                recv_sem=recv_sems.at[2 * i + j], device_id=(px, py, c), device_id_type=MESH))
    return copies


def _share_copies(refs, send_sems, recv_sems):
    x, y, c, _ = _position()
    copies = []
    for i, ref in enumerate(refs):
        rows = ref.shape[0] // 2
        mine = ref.at[pl.ds(c * rows, rows), :]
        copies.append(pltpu.make_async_remote_copy(
            src_ref=mine, dst_ref=mine, send_sem=send_sems.at[i], recv_sem=recv_sems.at[i],
            device_id=(x, y, 1 - c), device_id_type=MESH))
    return copies


def _scatter_copies(refs, send_sems, recv_sems):
    x, y, c, chips = _position()
    n = len(refs) // 2
    copies = []
    for i in range(n):
        for j, (px, py) in enumerate(chips):
            copies.append(pltpu.make_async_remote_copy(
                src_ref=refs[i].at[2 * px + py], dst_ref=refs[n + i].at[j],
                send_sem=send_sems.at[3 * i + j], recv_sem=recv_sems.at[3 * i + j],
                device_id=(px, py, c), device_id_type=MESH))
    return copies


def _cast_own(name, place, shard):
    rows, cols = shard.shape
    tr = _block_rows(rows, cols)

    def body(place_ref, w_ref, o_ref):
        o_ref[...] = w_ref[...].astype(BF16)

    grid_spec = pltpu.PrefetchScalarGridSpec(
        num_scalar_prefetch=1, grid=(rows // tr,),
        in_specs=[pl.BlockSpec((tr, cols), lambda r, pr: (r, 0))],
        out_specs=pl.BlockSpec((None, tr, cols), lambda r, pr: (pr[0], r, 0)))
    return pl.pallas_call(
        body, name=name, grid_spec=grid_spec,
        out_shape=jax.ShapeDtypeStruct((N_CHIPS, rows, cols), BF16),
        compiler_params=_params(1))(place, shard)


def _forward_to_sibling(name, arrays, deps=(), which=(0, 1, 2)):
    n = len(arrays)

    def body(*refs):
        ins = refs[:n]
        send_sems, recv_sems = refs[-2:]
        x, y, c, chips = _position()
        chips = [(j, chips[j]) for j in which]
        sends = []
        for i in range(n):
            rows = ins[i].shape[1] // 2
            for j, (px, py) in chips:
                piece = ins[i].at[2 * px + py, pl.ds(c * rows, rows), :]
                cp = pltpu.make_async_remote_copy(
                    src_ref=piece, dst_ref=piece, send_sem=send_sems.at[i, j],
                    recv_sem=recv_sems.at[i, j], device_id=(x, y, 1 - c), device_id_type=MESH)
                cp.start()
                sends.append(cp)
        for i in range(n):
            rows = ins[i].shape[1] // 2
            for j, (px, py) in chips:
                piece = ins[i].at[2 * px + py, pl.ds((1 - c) * rows, rows), :]
                pltpu.make_async_remote_copy(
                    src_ref=piece, dst_ref=piece, send_sem=send_sems.at[i, j],
                    recv_sem=recv_sems.at[i, j], device_id=(x, y, 1 - c),
                    device_id_type=MESH).wait_recv()
        for cp in sends:
            cp.wait_send()

    return pl.pallas_call(
        body, name=name,
        in_specs=[_any_spec()] * (n + len(deps)), out_specs=[_any_spec()] * n,
        out_shape=[jax.ShapeDtypeStruct(a.shape, a.dtype) for a in arrays],
        input_output_aliases={i: i for i in range(n)},
        scratch_shapes=[pltpu.SemaphoreType.DMA((n, 3))] * 2,
    )(*arrays, *deps)


def _forward_copies(which):
    def make(refs, send_sems, recv_sems):
        x, y, c, chips = _position()
        copies = []
        for i, ref in enumerate(refs):
            rows = ref.shape[1] // 2
            for k, j in enumerate(which):
                px, py = chips[j]
                piece = ref.at[2 * px + py, pl.ds(c * rows, rows), :]
                sem = len(which) * i + k
                copies.append(pltpu.make_async_remote_copy(
                    src_ref=piece, dst_ref=piece, send_sem=send_sems.at[sem],
                    recv_sem=recv_sems.at[sem], device_id=(x, y, 1 - c), device_id_type=MESH))
        return copies

    return make


def _swap_copies(refs, send_sems, recv_sems):
    x, y, c, _ = _position()
    n = len(refs) // 2
    copies = []
    for i in range(n):
        rows = refs[i].shape[1] // 2
        copies.append(pltpu.make_async_remote_copy(
            src_ref=refs[i].at[:, pl.ds((1 - c) * rows, rows), :], dst_ref=refs[n + i],
            send_sem=send_sems.at[i], recv_sem=recv_sems.at[i],
            device_id=(x, y, 1 - c), device_id_type=MESH))
    return copies


def _small_copies(refs, send_sems, recv_sems):
    packed, slots = refs
    x, y, c, _ = _position()
    me = 4 * x + 2 * y + c
    copies = []
    for r in range(1, N_DEV):
        peer = (x ^ ((r >> 2) & 1), y ^ ((r >> 1) & 1), c ^ (r & 1))
        copies.append(pltpu.make_async_remote_copy(
            src_ref=packed, dst_ref=slots.at[me], send_sem=send_sems.at[r - 1],
            recv_sem=recv_sems.at[r - 1], device_id=peer, device_id_type=MESH))
    return copies


def _block_rows(rows, cols, itemsize=4, target=1 << 20):
    return _pick(max(BF16_ROWS, target // (cols * itemsize)), [rows], unit=BF16_ROWS)


def _pair_sum(name, place, grad, received):
    P, rows, cols = received.shape
    tr = _block_rows(rows, cols, itemsize=2, target=2 << 20)
    nb = rows // tr

    def body(place_ref, g_ref, r_ref, o_ref):
        o_ref[...] = (g_ref[...].astype(F32) + r_ref[...].astype(F32)).astype(BF16)

    def panel(j, pr):
        return pr[0] ^ jnp.where(j == 2, 3, 2 - j)

    grid_spec = pltpu.PrefetchScalarGridSpec(
        num_scalar_prefetch=1, grid=(P - 1, nb),
        in_specs=[pl.BlockSpec((None, tr, cols),
                               lambda j, r, pr: (panel(j, pr), pr[1] * nb + r, 0)),
                  pl.BlockSpec((None, tr, cols), lambda j, r, pr: (panel(j, pr), r, 0))],
        out_specs=pl.BlockSpec((None, tr, cols), lambda j, r, pr: (panel(j, pr), r, 0)))
    return pl.pallas_call(
        body, name=name, grid_spec=grid_spec,
        out_shape=jax.ShapeDtypeStruct(received.shape, BF16),
        compiler_params=_params(2))(place, grad, received)


def _final_sum(name, place, grad, received, from_chips):
    _, rows, cols = received.shape
    tr = _block_rows(rows, cols, target=2 << 20)
    nb = rows // tr

    def body(place_ref, g_ref, r_ref, c_ref, o_ref):
        acc = g_ref[...].astype(F32) + r_ref[...].astype(F32)
        for j in range(3):
            acc = acc + c_ref[j].astype(F32)
        o_ref[...] = acc

    grid_spec = pltpu.PrefetchScalarGridSpec(
        num_scalar_prefetch=1, grid=(nb,),
        in_specs=[pl.BlockSpec((None, tr, cols), lambda r, pr: (pr[0], pr[1] * nb + r, 0)),
                  pl.BlockSpec((None, tr, cols), lambda r, pr: (pr[0], r, 0)),
                  pl.BlockSpec((3, tr, cols), lambda r, pr: (0, r, 0))],
        out_specs=pl.BlockSpec((tr, cols), lambda r, pr: (pr[1] * nb + r, 0)))
    return pl.pallas_call(
        body, name=name, grid_spec=grid_spec,
        out_shape=jax.ShapeDtypeStruct((2 * rows, cols), F32),
        compiler_params=_params(1))(place, grad, received, from_chips)


def _sum_devices(name, me, gathered, own):
    n_dev, rows, cols = gathered.shape
    tr = _pick(256, [rows])

    def body(me_ref, g_ref, own_ref, o_ref):
        term = lambda d: jnp.where(me_ref[0] == d, own_ref[...], g_ref[d])
        acc = term(0)
        for d in range(1, n_dev):
            acc = acc + term(d)
        o_ref[...] = acc

    grid_spec = pltpu.PrefetchScalarGridSpec(
        num_scalar_prefetch=1, grid=(rows // tr,),
        in_specs=[pl.BlockSpec((n_dev, tr, cols), lambda r, me_ref: (0, r, 0)),
                  pl.BlockSpec((tr, cols), lambda r, me_ref: (r, 0))],
        out_specs=pl.BlockSpec((tr, cols), lambda r, me_ref: (r, 0)))
    return pl.pallas_call(
        body, name=name, grid_spec=grid_spec,
        out_shape=jax.ShapeDtypeStruct((rows, cols), F32),
        compiler_params=_params(1))(me, gathered, own)


def _adamw(name, w, g, m, v):
    rows, cols = w.shape
    tr = _block_rows(rows, cols)
    c1 = 1.0 / (1.0 - ADAM_B1 ** ADAM_STEP)
    c2 = 1.0 / (1.0 - ADAM_B2 ** ADAM_STEP)

    def body(w_ref, g_ref, m_ref, v_ref, go_ref, d_ref, nm_ref, nv_ref):
        gv = g_ref[...]
        go_ref[...] = gv
        nm = ADAM_B1 * m_ref[...] + (1.0 - ADAM_B1) * gv
        nv = ADAM_B2 * v_ref[...] + (1.0 - ADAM_B2) * (gv * gv)
        nm_ref[...] = nm
        nv_ref[...] = nv
        d_ref[...] = -ADAM_LR * ((nm * c1) / (jnp.sqrt(nv * c2) + ADAM_EPS) + ADAM_WD * w_ref[...])

    blk = pl.BlockSpec((tr, cols), lambda r: (r, 0))
    shape = jax.ShapeDtypeStruct((rows, cols), F32)
    return pl.pallas_call(
        body, name=name, grid=(rows // tr,), in_specs=[blk] * 4, out_specs=[blk] * 4,
        out_shape=[shape] * 4, compiler_params=_params(1))(w, g, m, v)


BIG = ("ffn1_w_in", "ffn1_w_out", "w_mix_in", "w_mix_out", "w_cq", "w_ckv", "w_co",
       "ffn2_w_in", "ffn2_w_out")
BIG_KIND = {"ffn1_w_in": "c", "ffn1_w_out": "r", "w_mix_in": "c", "w_mix_out": "r", "w_cq": "r",
            "w_ckv": "c", "w_co": "r", "ffn2_w_in": "c", "ffn2_w_out": "r"}
GATHER_GROUPS = (("ffn1_in", ("ffn1_w_in",)), ("ffn1_out", ("ffn1_w_out",)),
                 ("mix_in", ("w_mix_in",)), ("mix_out", ("w_mix_out",)),
                 ("cross", ("w_cq", "w_ckv", "w_co")),
                 ("ffn2_in", ("ffn2_w_in",)), ("ffn2_out", ("ffn2_w_out",)))
GATHER_AFTER = (("ffn1_in", None), ("ffn1_out", "ffn1_in"), ("mix_in", "ffn1_out"),
                ("mix_out", "mix_in"), ("cross", "mix_in"), ("ffn2_in", "mix_in"),
                ("ffn2_out", "ffn2_in"))
RELAYED = ("ffn1_in", "ffn2_in")
TAIL_STAGES = (("sum", "ffn2"), ("sum", "cross"), ("sum", "mix"), ("sum", "ffn1_out"),
               ("update", "ffn2"), ("update", "cross"), ("sum", "ffn1_in"), ("update", "mix"),
               ("update", "ffn1_out"), ("update", "ffn1_in"))
SMALL = ("ffn1_norm", "mix_norm", "ln_v_gain", "ln_v_bias", "spatial_w", "spatial_b", "gnorm_a",
         "gnorm_b", "cross_norm", "mem_norm", "ffn2_norm", "final_norm")
WEIGHTS = ("ffn1_norm", "ffn1_w_in", "ffn1_w_out", "mix_norm", "w_mix_in", "ln_v_gain",
           "ln_v_bias", "spatial_w", "spatial_b", "gnorm_a", "gnorm_b", "w_mix_out", "cross_norm",
           "mem_norm", "w_cq", "w_ckv", "w_co", "ffn2_norm", "ffn2_w_in", "ffn2_w_out",
           "final_norm")


def _pack(arrays):
    return jnp.concatenate([a.reshape(-1, LANE) for a in arrays], axis=0)


def _unpack(packed, like):
    out, row = [], 0
    for a in like:
        rows = a.size // LANE
        out.append(packed[row:row + rows].reshape(a.shape))
        row += rows
    return out


def _local_step(x, mem, target, small, place, weights_of, forward_early, start_tokens,
                grads_ready, grads_flush):
    T, D = x.shape
    vec = lambda name: small[name].reshape(1, -1)
    w_a = small["ln_v_gain"].size
    w_b = small["gnorm_b"].size
    G = w_a // GROUP_DIM
    w_s = small["spatial_w"].reshape(G, SGU_BLOCK, SGU_BLOCK)
    b_t = small["spatial_b"].reshape(G, SGU_BLOCK).T

    h1, ffn1_saved = _ffn_forward("ffn1", x, vec("ffn1_norm"), weights_of, place,
                                  deps=start_tokens)
    n2 = _rmsnorm_fwd("mix_norm", h1, vec("mix_norm"))
    big = weights_of("mix_in", n2)
    (z,) = _matmul("mix_in", Mat(n2), big["w_mix_in"], "nn", [("c", 1, F32)], tm=2048, tn=256)
    z = _tie("z_after_forward_start", z[0], [forward_early("mix_out", z)])
    y = _sgu_forward("sgu", z, vec("ln_v_gain"), vec("ln_v_bias"), w_s, b_t, vec("gnorm_a"), D)
    yb, sb_total = _sb_forward("stickbreak", z, w_a, w_b)
    y = _rmsnorm_fwd("gnorm_b", yb, vec("gnorm_b"), into=y, col=w_a // w_b)
    y = _tie("y_after_forward_start", y, [forward_early("cross", y)])

    def add_res(acc, ex, out):
        out[0][...] = ex[0][...] + acc

    big.update(weights_of("mix_out", y))
    (h2,) = _matmul("mix_out", Mat(y), big["w_mix_out"], "nn", [("c", 1, F32)],
                    tm=1024, tn=1024, extras=[Mat(h1)], epi=add_res)
    h2 = h2[0]
    n3 = _rmsnorm_fwd("cross_norm", h2, vec("cross_norm"))
    memn = _rmsnorm_fwd("mem_norm", mem, vec("mem_norm"))
    big.update(weights_of("cross", n3))
    x_scale = (D // X_HEADS) ** -0.5

    def scaled(acc, ex, out):
        out[0][...] = (acc * x_scale).astype(BF16)

    (q,) = _matmul("cross_q", Mat(n3), big["w_cq"], "nn", [("c", 1, BF16)],
                   tm=1024, tn=1024, epi=scaled)
    (kv,) = _matmul("cross_kv", Mat(memn), big["w_ckv"], "nn", [("c", 1, BF16)], tm=256, tn=1024)
    q, kv = q[0], kv[0]
    o = _xattn_forward("cross_attn", q, kv)
    (h3,) = _matmul("cross_out", Mat(o), big["w_co"], "nn", [("c", 1, F32)],
                    tm=1024, tn=1024, extras=[Mat(h2)], epi=add_res)
    h3 = h3[0]
    h4, ffn2_saved = _ffn_forward("ffn2", h3, vec("ffn2_norm"), weights_of, place)

    gs = {}
    loss_tile, dh4, dh4_bf, gs["final_norm"] = _loss_head("loss_head", h4, vec("final_norm"), target)
    dh3, dh3_bf, gs["ffn2_norm"] = _ffn_backward(
        "ffn2", h3, vec("ffn2_norm"), ffn2_saved, dh4, dh4_bf, grads_ready, grads_flush)

    (do,) = _matmul("cross_do", Mat(dh3_bf), big["w_co"], "nt", [("c", 1, BF16)], tm=512, tn=2048)
    (dw_co,) = _matmul("cross_dwo", Mat(o), Mat(dh3_bf), "tn", [("r", N_CHIPS, BF16)],
                       tm=512, tn=1024)
    dq, dkv = _xattn_backward("cross_attn_bwd", q, kv, do[0])
    (dw_cq,) = _matmul("cross_dwq", Mat(n3), Mat(dq), "tn", [("r", N_CHIPS, BF16)],
                       tm=512, tn=1024)
    (dw_ckv,) = _matmul("cross_dwkv", Mat(memn), Mat(dkv), "tn", [("c", N_CHIPS, BF16)],
                        tm=1024, tn=1024)
    token = grads_ready("cross", {"w_cq": dw_cq, "w_ckv": dw_ckv, "w_co": dw_co})
    dq = _tie("cross_dq_after_swap", dq, [token])
    (dn3,) = _matmul("cross_dn", Mat(dq), big["w_cq"], "nt", [("c", 1, F32)], tm=512, tn=2048)
    (dmemn,) = _matmul("cross_dmem", Mat(dkv), big["w_ckv"], "nt", [("c", 1, F32)],
                       tm=256, tn=1024, tk=1024)
    (gs["mem_norm"],) = _rmsnorm_bwd("mem_dnorm", mem, vec("mem_norm"), dmemn[0], want_dx=False)
    dn3 = _tie("cross_dn_after_scatter", dn3, [grads_flush("cross", gs["mem_norm"])])
    dh2, dh2_bf, gs["cross_norm"] = _rmsnorm_bwd("cross_dnorm", h2, vec("cross_norm"), dn3[0],
                                                 dres=dh3)

    (dy,) = _matmul("mix_dy", Mat(dh2_bf), big["w_mix_out"], "nt", [("c", 1, F32)], tm=512, tn=2048)
    dy = dy[0]
    (dw_mix_out,) = _matmul("mix_dwout", Mat(y), Mat(dh2_bf), "tn", [("r", N_CHIPS, BF16)],
                            tm=512, tn=1024)
    dza, gs["ln_v_gain"], gs["ln_v_bias"], gs["spatial_w"], db, gs["gnorm_a"] = _sgu_backward(
        "sgu_bwd", z, dy, vec("ln_v_gain"), vec("ln_v_bias"), w_s, b_t, vec("gnorm_a"))
    gs["spatial_b"] = db.reshape(G, SGU_BLOCK)
    dob, gs["gnorm_b"] = _rmsnorm_bwd("gnorm_b_bwd", yb, vec("gnorm_b"), dy, dn_col=w_a // w_b,
                                      want_bf16=False)
    dqb, dkvb = _sb_backward("stickbreak_bwd", z, dob, sb_total, w_a, w_b)
    dz = jnp.concatenate([dza, dqb, dkvb[0], dkvb[1]], axis=1)
    (dw_mix_in,) = _matmul("mix_dwin", Mat(n2), Mat(dz), "tn", [("c", N_CHIPS, BF16)],
                           tm=1024, tn=1280)
    token = grads_ready("mix", {"w_mix_in": dw_mix_in, "w_mix_out": dw_mix_out})
    dz = _tie("mix_dz_after_swap", dz, [token])
    (dn2,) = _matmul("mix_dn", Mat(dz), big["w_mix_in"], "nt", [("c", 1, F32)],
                     tm=1024, tn=1024, tk=1280)
    dn2 = _tie("mix_dn_after_scatter", dn2, [grads_flush("mix", dn2)])
    dh1, dh1_bf, gs["mix_norm"] = _rmsnorm_bwd("mix_dnorm", h1, vec("mix_norm"), dn2[0], dres=dh2)

    dx, _, gs["ffn1_norm"] = _ffn_backward(
        "ffn1", x, vec("ffn1_norm"), ffn1_saved, dh1, dh1_bf, grads_ready, grads_flush,
        early_out=True)
    gs = {k: g.reshape(small[k].shape) for k, g in gs.items()}
    return loss_tile, dx, gs


def kernel(x, mem, ffn1_norm, ffn1_w_in, ffn1_w_out, mix_norm, w_mix_in, ln_v_gain, ln_v_bias, spatial_w, spatial_b, gnorm_a, gnorm_b, w_mix_out, cross_norm, mem_norm, w_cq, w_ckv, w_co, ffn2_norm, ffn2_w_in, ffn2_w_out, final_norm, loss_target, m_ffn1_norm, m_ffn1_w_in, m_ffn1_w_out, m_mix_norm, m_w_mix_in, m_ln_v_gain, m_ln_v_bias, m_spatial_w, m_spatial_b, m_gnorm_a, m_gnorm_b, m_w_mix_out, m_cross_norm, m_mem_norm, m_w_cq, m_w_ckv, m_w_co, m_ffn2_norm, m_ffn2_w_in, m_ffn2_w_out, m_final_norm, v_ffn1_norm, v_ffn1_w_in, v_ffn1_w_out, v_mix_norm, v_w_mix_in, v_ln_v_gain, v_ln_v_bias, v_spatial_w, v_spatial_b, v_gnorm_a, v_gnorm_b, v_w_mix_out, v_cross_norm, v_mem_norm, v_w_cq, v_w_ckv, v_w_co, v_ffn2_norm, v_ffn2_w_in, v_ffn2_w_out, v_final_norm):
    given = dict(locals())
    w = {k: given[k] for k in WEIGHTS}
    m = {k: given["m_" + k] for k in WEIGHTS}
    v = {k: given["v_" + k] for k in WEIGHTS}

    cx, cy, cc = lax.axis_index("x"), lax.axis_index("y"), lax.axis_index("c")
    place = jnp.stack([2 * cx + cy, cc]).astype(jnp.int32)

    names_of = dict(GATHER_GROUPS)
    own = {g: [_cast_own(f"cast_{k}", place, w[k][0]) for k in names] for g, names in GATHER_GROUPS}
    gathers = {}

    def start_gather(group, deps):
        first_hop = _near_copies if group in RELAYED else _gather_copies
        n_sems = (2 if group in RELAYED else 3) * len(own[group])
        send, recv, arrays, token = _split_start(f"gather_start_{group}", own[group],
                                                 first_hop, n_sems, deps)
        gathers[group] = (send, recv, arrays)
        return token

    start_tokens = [start_gather(g, ()) for g, after in GATHER_AFTER if after is None]
    start_tokens += [a for g, after in GATHER_AFTER if after is not None for a in own[g]]

    def weights_of(group, after):
        as_mats = lambda arrs: {k: Mat(a, BIG_KIND[k]) for k, a in zip(names_of[group], arrs)}
        if group in forwards:
            send, recv, arrays = forwards[group]
            return as_mats(_split_wait(f"gather_forward_wait_{group}", arrays, send, recv, after,
                                       _forward_copies((0, 1, 2))))
        send, recv, arrays = gathers[group]
        if group not in RELAYED:
            arrays = _split_wait(f"gather_wait_{group}", arrays, send, recv, after, _gather_copies)
            tokens = [start_gather(g, (arrays[0],)) for g, a in GATHER_AFTER if a == group]
            return as_mats(_forward_to_sibling(f"gather_forward_{group}", list(arrays), tokens))
        arrays = _split_wait(f"gather_wait_{group}", arrays, send, recv, after, _near_copies)
        send, recv, arrays, token = _split_start(f"gather_relay_{group}", list(arrays),
                                                 _relay_copies, 2 * len(arrays))
        tokens = [token] + [start_gather(g, (arrays[0],)) for g, a in GATHER_AFTER if a == group]
        arrays = _forward_to_sibling(f"gather_forward_{group}", list(arrays), tokens, which=(0,))
        y_send, y_recv, arrays, _ = _split_start(f"gather_forward_y_{group}", list(arrays),
                                                 _forward_copies((1,)), len(arrays))

        def finish(after):
            arrs = _split_wait(f"gather_relay_wait_{group}", arrays, send, recv, after,
                               _relay_copies)
            arrs = _split_wait(f"gather_forward_y_wait_{group}", arrs, y_send, y_recv, after,
                               _forward_copies((1,)))
            return as_mats(_forward_to_sibling(f"gather_forward_diag_{group}", list(arrs),
                                               which=(2,)))

        return {**as_mats(arrays), "finish": finish}

    forwards = {}

    def forward_early(group, after):
        send, recv, arrays = gathers[group]
        arrays = _split_wait(f"gather_wait_{group}", arrays, send, recv, after, _gather_copies)
        send, recv, arrays, token = _split_start(f"gather_forward_start_{group}", list(arrays),
                                                 _forward_copies((0, 1, 2)), 3 * len(arrays))
        forwards[group] = (send, recv, arrays)
        return token

    swaps, scatters = {}, {}

    def grads_ready(group, partial):
        names = list(partial)
        grads_ = [partial[k] for k in names]
        lands = [lax.empty((g.shape[0], g.shape[1] // 2, g.shape[2]), g.dtype) for g in grads_]
        send, recv, arrays, token = _split_start(f"swap_start_{group}", grads_ + lands,
                                                 _swap_copies, len(names))
        swaps[group] = (names, send, recv, arrays)
        return token

    def grads_flush(group, after):
        names, send, recv, arrays = swaps[group]
        arrays = _split_wait(f"swap_wait_{group}", arrays, send, recv, after, _swap_copies)
        grads_, from_sibling = arrays[:len(names)], arrays[len(names):]
        sums = [_pair_sum(f"pair_sum_{k}", place, g, r)
                for k, g, r in zip(names, grads_, from_sibling)]
        lands = [lax.empty((3,) + s.shape[1:], s.dtype) for s in sums]
        send, recv, arrays, token = _split_start(f"scatter_start_{group}", sums + lands,
                                                 _scatter_copies, 3 * len(names))
        scatters[group] = (names, grads_, from_sibling, send, recv, arrays)
        return token

    small = {k: w[k] for k in SMALL}
    loss_tile, grad_x, gs = _local_step(x[0], mem[0], loss_target[0], small, place, weights_of,
                                        forward_early, start_tokens, grads_ready, grads_flush)

    packed = _pack([gs[k] for k in SMALL] + [loss_tile])
    slots = jnp.zeros((N_DEV,) + packed.shape, packed.dtype)
    small_send, small_recv, small_arrays, _ = _split_start(
        "small_start", [packed, slots], _small_copies, N_DEV - 1)

    grad, delta, new_m, new_v = {}, {}, {}, {}
    shares = {}
    after = [grad_x]
    for stage, group in TAIL_STAGES:
        if stage == "sum":
            names, grads_, from_sibling, send, recv, arrays = scatters[group]
            arrays = _split_wait(f"scatter_wait_{group}", arrays, send, recv, after,
                                 _scatter_copies)
            from_chips = arrays[len(names):]
            shards = [_final_sum(f"final_sum_{k}", place, g, r, f)
                      for k, g, r, f in zip(names, grads_, from_sibling, from_chips)]
            send, recv, shards, token = _split_start(f"share_start_{group}", shards,
                                                     _share_copies, len(names))
            shares[group] = (names, send, recv, shards)
            after = [token]
        else:
            names, send, recv, shards = shares[group]
            shards = _split_wait(f"share_wait_{group}", shards, send, recv, after, _share_copies)
            after = []
            for k, g_ in zip(names, shards):
                g_, d_, m_, v_ = _adamw(f"adamw_{k}", w[k][0], g_, m[k][0], v[k][0])
                grad[k], delta[k], new_m[k], new_v[k] = g_[None], d_[None], m_[None], v_[None]
                after.append(v_)

    packed, slots = _split_wait("small_wait", small_arrays, small_send, small_recv, after,
                                _small_copies)
    me = (4 * cx + 2 * cy + cc).astype(jnp.int32).reshape(1)
    total = _sum_devices("sum_small", me, slots, packed)
    n_small = total.shape[0] - SUBLANE
    loss = total[n_small, 0]
    small_g = total[:n_small]
    g_s, d_s, m_s, v_s = _adamw("adamw_small", _pack([w[k] for k in SMALL]), small_g,
                                _pack([m[k] for k in SMALL]), _pack([v[k] for k in SMALL]))
    like = [w[k] for k in SMALL]
    for k, g_, d_, m_, v_ in zip(SMALL, _unpack(g_s, like), _unpack(d_s, like),
                                 _unpack(m_s, like), _unpack(v_s, like)):
        grad[k], delta[k], new_m[k], new_v[k] = g_, d_, m_, v_

    return (loss, grad_x[None], *[grad[k] for k in WEIGHTS], *[delta[k] for k in WEIGHTS],
            *[new_m[k] for k in WEIGHTS], *[new_v[k] for k in WEIGHTS])
```

```python
import functools
import math

import jax
import jax.numpy as jnp
from jax import lax
from jax.experimental import pallas as pl
from jax.experimental.pallas import tpu as pltpu

F32 = jnp.float32
BF16 = jnp.bfloat16
MESH = pl.DeviceIdType.MESH

EPS = 1e-6
CHUNK = 64
SGU_BLOCK = 128
GROUP_DIM = 128
X_HEADS = 4
N_CHIPS = 4
N_DEV = 8
LANE = 128
SUBLANE = 8
BF16_ROWS = 16

ADAM_LR = 0.001
ADAM_B1 = 0.9
ADAM_B2 = 0.999
ADAM_EPS = 1e-08
ADAM_WD = 0.01
ADAM_STEP = 10

V7X_VMEM_BYTES = 64 << 20
VMEM_LIMIT = V7X_VMEM_BYTES - (8 << 20)


def _params(n_grid):
    return pltpu.CompilerParams(dimension_semantics=("arbitrary",) * n_grid,
                                vmem_limit_bytes=VMEM_LIMIT)


def _pick(pref, dims, unit=None):
    g = functools.reduce(math.gcd, dims)
    if unit is None:
        unit = LANE if g % LANE == 0 else SUBLANE
    cands = [d for d in range(unit, g + 1, unit) if g % d == 0] or [g]
    return min(cands, key=lambda d: abs(math.log(d / pref)))


def _any_spec():
    return pl.BlockSpec(memory_space=pl.ANY)


class Mat:
    def __init__(self, arr, kind="c"):
        if arr.ndim == 2:
            arr = arr[None]
        self.arr, self.kind = arr, kind
        self.P, self.prow, self.pcol = arr.shape
        self.rows = self.prow * (self.P if kind == "r" else 1)
        self.cols = self.pcol * (self.P if kind == "c" else 1)
        self.dtype = arr.dtype

    def spec(self, tr, tc, rc_fn):
        if self.kind == "c":
            per = self.pcol // tc
            assert per * tc == self.pcol, (self.pcol, tc)

            def imap(*g):
                i, j = rc_fn(*g)
                return (j // per, i, j % per)
        else:
            per = self.prow // tr
            assert per * tr == self.prow, (self.prow, tr)

            def imap(*g):
                i, j = rc_fn(*g)
                return (i // per, i % per, j)
        return pl.BlockSpec((None, tr, tc), imap)

    def two_d(self):
        assert self.P == 1
        return self.arr[0]


def _out_mat(kind, P, rows, cols, dtype):
    shape = (P, rows, cols // P) if kind == "c" else (P, rows // P, cols)
    return jax.ShapeDtypeStruct(shape, dtype)


def _matmul(name, A, B, mode, outs, *, tm=1024, tn=1024, tk=2048, extras=(), epi=None):
    if mode == "nn":
        M, K, N = A.rows, A.cols, B.cols
        assert B.rows == K
    elif mode == "nt":
        M, K, N = A.rows, A.cols, B.rows
        assert B.cols == K
    else:
        K, M, N = A.rows, A.cols, B.cols
        assert B.rows == K
    mdims, ndims, kdims = [M], [N], [K]
    whole_b = mode == "nn" and B.kind == "r" and B.P > 1 and K <= tk
    whole_bt = mode == "nt" and B.kind == "r" and B.P > 1 and N <= tn
    if whole_b:
        kdims.append(A.pcol)
        ndims.append(B.pcol)
    elif whole_bt:
        kdims += [A.pcol, B.pcol]
    elif mode == "tn":
        assert A.kind == "c" and B.kind == "c"
        mdims.append(A.pcol)
        ndims.append(B.pcol)
    else:
        (mdims if A.kind == "r" else kdims).append(A.prow if A.kind == "r" else A.pcol)
        if mode == "nn":
            (kdims if B.kind == "r" else ndims).append(B.prow if B.kind == "r" else B.pcol)
        else:
            (ndims if B.kind == "r" else kdims).append(B.prow if B.kind == "r" else B.pcol)
    for o in list(outs) + list(extras):
        if isinstance(o, Mat):
            (mdims if o.kind == "r" else ndims).append(o.prow if o.kind == "r" else o.pcol)
        elif isinstance(o[0], str):
            (mdims if o[0] == "r" else ndims).append((M if o[0] == "r" else N) // o[1])
    tm, tn = _pick(tm, mdims), _pick(tn, ndims)
    tk = K if mode == "tn" else _pick(tk, kdims)
    nk = K // tk
    grid = (M // tm, N // tn, nk)

    if mode == "tn":
        a_spec = A.spec(K, tm, lambda m, n, k: (0, m))
        b_spec = B.spec(K, tn, lambda m, n, k: (0, n))
    else:
        a_spec = A.spec(tm, tk, lambda m, n, k: (m, k))
        if whole_b:
            b_spec = pl.BlockSpec((B.P, B.prow, tn), lambda m, n, k: (0, 0, n))
        elif whole_bt:
            b_spec = pl.BlockSpec((B.P, B.prow, tk), lambda m, n, k: (0, 0, k))
        elif mode == "nn":
            b_spec = B.spec(tk, tn, lambda m, n, k: (k, n))
        else:
            b_spec = B.spec(tn, tk, lambda m, n, k: (n, k))

    def mn_spec(o):
        if isinstance(o, Mat):
            return o.spec(tm, tn, lambda m, n, k: (m, n))
        if isinstance(o[0], str):
            kind, P = o[0], o[1]
            fake = Mat.__new__(Mat)
            fake.kind, fake.P = kind, P
            fake.prow = M // P if kind == "r" else M
            fake.pcol = N // P if kind == "c" else N
            return Mat.spec(fake, tm, tn, lambda m, n, k: (m, n))
        return o[1](tm, tn)

    out_shapes = tuple(_out_mat(o[0], o[1], M, N, o[2]) if isinstance(o[0], str) else o[0]
                       for o in outs)
    out_specs = tuple(mn_spec(o) for o in outs)
    extra_arrays = tuple(e.arr if isinstance(e, Mat) else e[0] for e in extras)
    extra_specs = tuple(mn_spec(e) for e in extras)
    n_ex, n_out = len(extras), len(outs)
    tt = _pick(256, [tm])
    dims = (((1,), (1 if mode == "nt" else 0,)), ((), ()))

    def body(*refs):
        a_ref, b_ref = refs[:2]
        ex_refs = refs[2:2 + n_ex]
        out_refs = refs[2 + n_ex:2 + n_ex + n_out]
        scratch = refs[2 + n_ex + n_out:]
        if mode == "tn":
            at_ref = scratch[0]

            @pl.when(pl.program_id(1) == 0)
            def _():
                for c0 in range(0, tm, tt):
                    at_ref[c0:c0 + tt, :] = a_ref[:, c0:c0 + tt].astype(F32).T.astype(BF16)

            lhs = at_ref[...]
        else:
            lhs = a_ref[...].astype(BF16)
        rhs = b_ref[...]
        if whole_b or whole_bt:
            rhs = rhs.reshape(B.P * B.prow, rhs.shape[-1])
        part = lax.dot_general(lhs, rhs.astype(BF16), dims, preferred_element_type=F32)

        def finish(acc):
            if epi is None:
                out_refs[0][...] = acc.astype(out_refs[0].dtype)
            else:
                epi(acc, ex_refs, out_refs)

        if nk == 1:
            finish(part)
        else:
            acc_ref = scratch[0]
            k = pl.program_id(2)

            @pl.when(k == 0)
            def _():
                acc_ref[...] = part

            @pl.when(k > 0)
            def _():
                acc_ref[...] += part

            @pl.when(k == nk - 1)
            def _():
                finish(acc_ref[...])

    scratch_shapes = []
    if mode == "tn":
        scratch_shapes.append(pltpu.VMEM((tm, K), BF16))
    elif nk > 1:
        scratch_shapes.append(pltpu.VMEM((tm, tn), F32))
    res = pl.pallas_call(
        body, name=name, grid=grid,
        in_specs=[a_spec, b_spec, *extra_specs], out_specs=out_specs, out_shape=out_shapes,
        scratch_shapes=scratch_shapes, compiler_params=_params(3),
    )(A.arr, B.arr, *extra_arrays)
    return res


def _row_tile(T, streams=5):
    return _pick(512 if streams <= 3 else 256, [T])


def _tie(name, x, deps):
    def body(*refs):
        refs[-1][...] = jnp.zeros_like(refs[-1])

    return pl.pallas_call(
        body, name=name, in_specs=[_any_spec()] * (1 + len(deps)),
        out_specs=(_any_spec(), pl.BlockSpec(memory_space=pltpu.VMEM)),
        out_shape=(jax.ShapeDtypeStruct(x.shape, x.dtype),
                   jax.ShapeDtypeStruct((SUBLANE, LANE), F32)),
        input_output_aliases={0: 0},
    )(x, *deps)[0]


def _rmsnorm_fwd(name, x, g, *, into=None, col=0, deps=()):
    T, W = x.shape
    tr = _row_tile(T, streams=2)

    def body(x_ref, g_ref, *rest):
        o_ref = rest[-1]
        xv = x_ref[...]
        rstd = lax.rsqrt(jnp.mean(xv * xv, axis=-1, keepdims=True) + EPS)
        o_ref[...] = (xv * rstd * g_ref[...]).astype(o_ref.dtype)

    in_specs = [pl.BlockSpec((tr, W), lambda i: (i, 0)), pl.BlockSpec((1, W), lambda i: (0, 0))]
    args = [x, g]
    kwargs = {}
    if into is None:
        out_shape = jax.ShapeDtypeStruct((T, W), BF16)
    else:
        out_shape = jax.ShapeDtypeStruct(into.shape, into.dtype)
        in_specs.append(_any_spec())
        args.append(into)
        kwargs["input_output_aliases"] = {2: 0}
    in_specs += [_any_spec()] * len(deps)
    args += list(deps)
    return pl.pallas_call(
        body, name=name, grid=(T // tr,), in_specs=in_specs,
        out_specs=pl.BlockSpec((tr, W), lambda i: (i, col)), out_shape=out_shape,
        compiler_params=_params(1), **kwargs)(*args)


def _rmsnorm_bwd(name, x, g, dn, *, dn_col=0, dres=None, want_dx=True, want_bf16=True):
    T, W = x.shape
    tr = _row_tile(T)
    has_res = dres is not None

    def body(*refs):
        x_ref, g_ref, dn_ref = refs[:3]
        pos = 3
        dres_ref = None
        if has_res:
            dres_ref = refs[pos]
            pos += 1
        outs = refs[pos:]
        dg_ref = outs[-1]
        xv = x_ref[...]
        rstd = lax.rsqrt(jnp.mean(xv * xv, axis=-1, keepdims=True) + EPS)
        xhat = xv * rstd
        dnv = dn_ref[...].astype(F32)

        @pl.when(pl.program_id(0) == 0)
        def _():
            dg_ref[...] = jnp.zeros_like(dg_ref)

        dg_ref[...] += jnp.sum(dnv * xhat, axis=0, keepdims=True)
        if want_dx:
            t = dnv * g_ref[...]
            dx = rstd * (t - xhat * jnp.mean(t * xhat, axis=-1, keepdims=True))
            if has_res:
                dx = dx + dres_ref[...]
            outs[0][...] = dx
            if want_bf16:
                outs[1][...] = dx.astype(BF16)

    row = pl.BlockSpec((tr, W), lambda i: (i, 0))
    in_specs = [row, pl.BlockSpec((1, W), lambda i: (0, 0)),
                pl.BlockSpec((tr, W), lambda i: (i, dn_col))]
    args = [x, g, dn]
    if has_res:
        in_specs.append(row)
        args.append(dres)
    out_shape, out_specs = [], []
    if want_dx:
        out_shape.append(jax.ShapeDtypeStruct((T, W), F32))
        out_specs.append(row)
        if want_bf16:
            out_shape.append(jax.ShapeDtypeStruct((T, W), BF16))
            out_specs.append(row)
    out_shape.append(jax.ShapeDtypeStruct((1, W), F32))
    out_specs.append(pl.BlockSpec((1, W), lambda i: (0, 0)))
    return pl.pallas_call(
        body, name=name, grid=(T // tr,), in_specs=in_specs, out_specs=out_specs,
        out_shape=out_shape, compiler_params=_params(1))(*args)


def _loss_head(name, h, g, target):
    T, W = h.shape
    tr = _row_tile(T)

    def body(h_ref, g_ref, t_ref, loss_ref, dx_ref, dxb_ref, dg_ref):
        xv = h_ref[...]
        gv = g_ref[...]
        rstd = lax.rsqrt(jnp.mean(xv * xv, axis=-1, keepdims=True) + EPS)
        xhat = xv * rstd
        diff = xhat * gv - t_ref[...]

        @pl.when(pl.program_id(0) == 0)
        def _():
            dg_ref[...] = jnp.zeros_like(dg_ref)
            loss_ref[...] = jnp.zeros_like(loss_ref)

        loss_ref[...] += 0.5 * jnp.sum(jnp.mean(diff * diff, axis=-1, keepdims=True))
        dnv = diff * (1.0 / W)
        dg_ref[...] += jnp.sum(dnv * xhat, axis=0, keepdims=True)
        t = dnv * gv
        dx = rstd * (t - xhat * jnp.mean(t * xhat, axis=-1, keepdims=True))
        dx_ref[...] = dx
        dxb_ref[...] = dx.astype(BF16)

    row = pl.BlockSpec((tr, W), lambda i: (i, 0))
    vec = pl.BlockSpec((1, W), lambda i: (0, 0))
    return pl.pallas_call(
        body, name=name, grid=(T // tr,), in_specs=[row, vec, row],
        out_specs=[pl.BlockSpec((SUBLANE, LANE), lambda i: (0, 0)), row, row, vec],
        out_shape=[jax.ShapeDtypeStruct((SUBLANE, LANE), F32), jax.ShapeDtypeStruct((T, W), F32),
                   jax.ShapeDtypeStruct((T, W), BF16), jax.ShapeDtypeStruct((1, W), F32)],
        compiler_params=_params(1))(h, g, target)


def _sigmoid(x):
    return 1.0 / (1.0 + jnp.exp(-x))


def _ffn_in(name, n, W, place, half, prev=None):
    T, D = n.shape
    F = W.cols // 2
    tm = _pick(2048, [T])
    tn = _pick(512, [W.pcol])
    per = W.pcol // tn

    def body(place_ref, a_ref, wg_ref, wu_ref, *rest):
        gu_ref, act_ref = rest[-2:]
        a = a_ref[...]
        gate = jnp.dot(a, wg_ref[...], preferred_element_type=F32)
        up = jnp.dot(a, wu_ref[...], preferred_element_type=F32)
        sig = _sigmoid(gate)
        silu = gate * sig
        gu_ref[0] = (up * sig * (1.0 + gate * (1.0 - sig))).astype(BF16)
        gu_ref[1] = silu.astype(BF16)
        act_ref[...] = (silu * up).astype(BF16)

    def pair(pr):
        return (pr[0] + half) % 2

    in_specs = [pl.BlockSpec((tm, D), lambda m, j, pr: (m, 0)),
                pl.BlockSpec((None, D, tn), lambda m, j, pr: (pair(pr), 0, j)),
                pl.BlockSpec((None, D, tn), lambda m, j, pr: (2 + pair(pr), 0, j))]
    args = [place, n, W.arr, W.arr]
    kwargs = {}
    if prev is not None:
        in_specs += [_any_spec(), _any_spec()]
        args += list(prev)
        kwargs["input_output_aliases"] = {4: 0, 5: 1}
    grid_spec = pltpu.PrefetchScalarGridSpec(
        num_scalar_prefetch=1, grid=(T // tm, per), in_specs=in_specs,
        out_specs=[pl.BlockSpec((2, tm, tn), lambda m, j, pr: (0, m, pair(pr) * per + j)),
                   pl.BlockSpec((tm, tn), lambda m, j, pr: (m, pair(pr) * per + j))])
    return pl.pallas_call(
        body, name=name, grid_spec=grid_spec,
        out_shape=[jax.ShapeDtypeStruct((2, T, F), BF16), jax.ShapeDtypeStruct((T, F), BF16)],
        compiler_params=_params(2), **kwargs)(*args)


def _ffn_forward(tag, h, norm_g, weights_of, place, deps=()):
    n = _rmsnorm_fwd(f"{tag}_norm", h, norm_g, deps=deps)
    got = weights_of(f"{tag}_in", n)
    gu, act = _ffn_in(f"{tag}_in_a", n, got[f"{tag}_w_in"], place, 0)
    w_in = got["finish"](act)[f"{tag}_w_in"]
    gu, act = _ffn_in(f"{tag}_in_b", n, w_in, place, 1, (gu, act))
    w_out = weights_of(f"{tag}_out", act)[f"{tag}_w_out"]

    def epi(acc, ex, out):
        out[0][...] = ex[0][...] + 0.5 * acc

    (h_out,) = _matmul(f"{tag}_out", Mat(act), w_out, "nn", [("c", 1, F32)],
                       tm=1024, tn=512, tk=8192, extras=[Mat(h)], epi=epi)
    return h_out[0], (n, gu, act, w_in, w_out)


def _ffn_backward(tag, h_in, norm_g, saved, dh, dh_bf, grads_ready, grads_flush,
                  early_out=False):
    n, gu, act, w_in, w_out = saved
    T, F = act.shape

    def epi(acc, ex, out):
        dact = 0.5 * acc
        out[0][0] = (dact * ex[0][0].astype(F32)).astype(BF16)
        out[0][1] = (dact * ex[0][1].astype(F32)).astype(BF16)

    def pair_spec(tm, tn):
        return pl.BlockSpec((2, tm, tn), lambda m, j, k: (0, m, j))

    def half(acc, ex, out):
        out[0][...] = (0.5 * acc).astype(out[0].dtype)

    (dw_out,) = _matmul(f"{tag}_dwout", Mat(act), Mat(dh_bf), "tn", [("r", N_CHIPS, BF16)],
                        tm=1408, tn=512, epi=half)
    if early_out:
        token = grads_ready(f"{tag}_out", {f"{tag}_w_out": dw_out})
        dh_bf = _tie(f"{tag}_dh_after_swap", dh_bf, [token])
    (dgu,) = _matmul(f"{tag}_dact", Mat(dh_bf), w_out, "nt",
                     [(jax.ShapeDtypeStruct((2, T, F), BF16), pair_spec)],
                     tm=512, tn=1408, extras=[(gu, pair_spec)], epi=epi)
    if early_out:
        dgu = _tie(f"{tag}_dgu_after_scatter", dgu, [grads_flush(f"{tag}_out", dgu)])
    (dw_in,) = _matmul(f"{tag}_dwin", Mat(n), Mat(dgu), "tn", [("c", N_CHIPS, BF16)],
                       tm=1024, tn=1408)
    if early_out:
        group, partial = f"{tag}_in", {f"{tag}_w_in": dw_in}
    else:
        group, partial = tag, {f"{tag}_w_in": dw_in, f"{tag}_w_out": dw_out}
    dgu = _tie(f"{tag}_dgu_after_swap", dgu, [grads_ready(group, partial)])
    (dn,) = _matmul(f"{tag}_dn", Mat(dgu), w_in, "nt", [("c", 1, F32)],
                    tm=1024, tn=1024, tk=2816)
    dn = _tie(f"{tag}_dn_after_scatter", dn, [grads_flush(group, dn)])
    return _rmsnorm_bwd(f"{tag}_dnorm", h_in, norm_g, dn[0], dres=dh)


_GELU_C = math.sqrt(2.0 / math.pi)
_GELU_A = 0.044715


def _gelu(x):
    return 0.5 * x * (1.0 + jnp.tanh(_GELU_C * (x + _GELU_A * x * x * x)))


def _gelu_grad(x):
    th = jnp.tanh(_GELU_C * (x + _GELU_A * x * x * x))
    return 0.5 * (1.0 + th) + 0.5 * x * (1.0 - th * th) * _GELU_C * (1.0 + 3.0 * _GELU_A * x * x)


def _chunk_mask():
    t = lax.broadcasted_iota(jnp.int32, (SGU_BLOCK, SGU_BLOCK), 0) // CHUNK
    s = lax.broadcasted_iota(jnp.int32, (SGU_BLOCK, SGU_BLOCK), 1) // CHUNK
    return s <= t


def _sgu_group_forward(v_g, lg, lb, wm_bf, b_col):
    mu = jnp.mean(v_g, axis=-1, keepdims=True)
    xc = v_g - mu
    rstd = lax.rsqrt(jnp.mean(xc * xc, axis=-1, keepdims=True) + EPS)
    vhat = xc * rstd
    vn = vhat * lg + lb
    mixed = jnp.dot(wm_bf, vn.astype(BF16), preferred_element_type=F32) + b_col
    return vhat, rstd, vn, mixed


def _sgu_forward(name, z, ln_g, ln_b, w_s, b_t, gn, d_model):
    T = z.shape[0]
    W_A = ln_g.shape[1]
    G = W_A // GROUP_DIM

    def body(z_ref, lg_ref, lb_ref, w_ref, bt_ref, gn_ref, y_ref):
        mask = _chunk_mask()
        u = _gelu(z_ref[:, :W_A])
        v = _gelu(z_ref[:, W_A:])
        cols = []
        for g in range(G):
            sl = slice(g * GROUP_DIM, (g + 1) * GROUP_DIM)
            wm = jnp.where(mask, w_ref[g], 0.0).astype(BF16)
            _, _, _, mixed = _sgu_group_forward(v[:, sl], lg_ref[:, sl], lb_ref[:, sl], wm,
                                                bt_ref[:, g:g + 1])
            cols.append(u[:, sl] * mixed)
        ya = jnp.concatenate(cols, axis=1)
        rstd = lax.rsqrt(jnp.mean(ya * ya, axis=-1, keepdims=True) + EPS)
        y_ref[...] = (ya * rstd * gn_ref[...]).astype(BF16)

    vec = pl.BlockSpec((1, W_A), lambda i: (0, 0))
    return pl.pallas_call(
        body, name=name, grid=(T // SGU_BLOCK,),
        in_specs=[pl.BlockSpec((SGU_BLOCK, 2 * W_A), lambda i: (i, 0)), vec, vec,
                  pl.BlockSpec((G, SGU_BLOCK, SGU_BLOCK), lambda i: (0, 0, 0)),
                  pl.BlockSpec((SGU_BLOCK, G), lambda i: (0, 0)), vec],
        out_specs=pl.BlockSpec((SGU_BLOCK, W_A), lambda i: (i, 0)),
        out_shape=jax.ShapeDtypeStruct((T, d_model), BF16),
        compiler_params=_params(1))(z, ln_g, ln_b, w_s, b_t, gn)


def _sgu_backward(name, z, dy, ln_g, ln_b, w_s, b_t, gn):
    T = z.shape[0]
    W_A = ln_g.shape[1]
    G = W_A // GROUP_DIM

    def body(z_ref, dy_ref, lg_ref, lb_ref, w_ref, bt_ref, gn_ref,
             dz_ref, dlg_ref, dlb_ref, dw_ref, db_ref, dgn_ref):
        @pl.when(pl.program_id(0) == 0)
        def _():
            for r in (dlg_ref, dlb_ref, dw_ref, db_ref, dgn_ref):
                r[...] = jnp.zeros_like(r)

        mask = _chunk_mask()
        zu = z_ref[:, :W_A]
        zv = z_ref[:, W_A:]
        u = _gelu(zu)
        v = _gelu(zv)
        saved, cols = [], []
        for g in range(G):
            sl = slice(g * GROUP_DIM, (g + 1) * GROUP_DIM)
            wm = jnp.where(mask, w_ref[g], 0.0)
            vhat, rstd, vn, mixed = _sgu_group_forward(
                v[:, sl], lg_ref[:, sl], lb_ref[:, sl], wm.astype(BF16), bt_ref[:, g:g + 1])
            saved.append((wm, vhat, rstd, vn, mixed))
            cols.append(u[:, sl] * mixed)
        ya = jnp.concatenate(cols, axis=1)
        rstd_a = lax.rsqrt(jnp.mean(ya * ya, axis=-1, keepdims=True) + EPS)
        ya_hat = ya * rstd_a
        dyv = dy_ref[...].astype(F32)
        dgn_ref[...] += jnp.sum(dyv * ya_hat, axis=0, keepdims=True)
        t = dyv * gn_ref[...]
        dya = rstd_a * (t - ya_hat * jnp.mean(t * ya_hat, axis=-1, keepdims=True))
        du_cols, dv_cols, dlg_cols, dlb_cols = [], [], [], []
        for g in range(G):
            sl = slice(g * GROUP_DIM, (g + 1) * GROUP_DIM)
            wm, vhat, rstd, vn, mixed = saved[g]
            dya_g = dya[:, sl]
            du_cols.append(dya_g * mixed)
            dmix = dya_g * u[:, sl]
            dmix_bf = dmix.astype(BF16)
            db_ref[g] += jnp.sum(dmix, axis=1, keepdims=True)
            dw = lax.dot_general(dmix_bf, vn.astype(BF16), (((1,), (1,)), ((), ())),
                                 preferred_element_type=F32)
            dw_ref[g] += jnp.where(mask, dw, 0.0)
            dvn = jnp.dot(wm.T.astype(BF16), dmix_bf, preferred_element_type=F32)
            dlg_cols.append(jnp.sum(dvn * vhat, axis=0, keepdims=True))
            dlb_cols.append(jnp.sum(dvn, axis=0, keepdims=True))
            dvhat = dvn * lg_ref[:, sl]
            dv_cols.append(rstd * (dvhat - jnp.mean(dvhat, axis=-1, keepdims=True)
                                   - vhat * jnp.mean(dvhat * vhat, axis=-1, keepdims=True)))
        dlg_ref[...] += jnp.concatenate(dlg_cols, axis=1)
        dlb_ref[...] += jnp.concatenate(dlb_cols, axis=1)
        dz_ref[:, :W_A] = (jnp.concatenate(du_cols, axis=1) * _gelu_grad(zu)).astype(BF16)
        dz_ref[:, W_A:] = (jnp.concatenate(dv_cols, axis=1) * _gelu_grad(zv)).astype(BF16)

    vec = pl.BlockSpec((1, W_A), lambda i: (0, 0))
    wspec = pl.BlockSpec((G, SGU_BLOCK, SGU_BLOCK), lambda i: (0, 0, 0))
    return pl.pallas_call(
        body, name=name, grid=(T // SGU_BLOCK,),
        in_specs=[pl.BlockSpec((SGU_BLOCK, 2 * W_A), lambda i: (i, 0)),
                  pl.BlockSpec((SGU_BLOCK, W_A), lambda i: (i, 0)), vec, vec, wspec,
                  pl.BlockSpec((SGU_BLOCK, G), lambda i: (0, 0)), vec],
        out_specs=[pl.BlockSpec((SGU_BLOCK, 2 * W_A), lambda i: (i, 0)), vec, vec, wspec,
                   pl.BlockSpec((G, SGU_BLOCK, 1), lambda i: (0, 0, 0)), vec],
        out_shape=[jax.ShapeDtypeStruct((T, 2 * W_A), BF16), jax.ShapeDtypeStruct((1, W_A), F32),
                   jax.ShapeDtypeStruct((1, W_A), F32),
                   jax.ShapeDtypeStruct((G, SGU_BLOCK, SGU_BLOCK), F32),
                   jax.ShapeDtypeStruct((G, SGU_BLOCK, 1), F32),
                   jax.ShapeDtypeStruct((1, W_A), F32)],
        compiler_params=_params(1))(z, dy, ln_g, ln_b, w_s, b_t, gn)


def _split_dot(x, tri):
    hi = x.astype(BF16)
    lo = (x - hi.astype(F32)).astype(BF16)
    return (jnp.dot(hi, tri, preferred_element_type=F32)
            + jnp.dot(lo, tri, preferred_element_type=F32))


def _tri(n, rel):
    r = lax.broadcasted_iota(jnp.int32, (n, n), 0)
    c = lax.broadcasted_iota(jnp.int32, (n, n), 1)
    return rel(r, c).astype(BF16)


def _dot_nt(a, b):
    return lax.dot_general(a, b, (((1,), (1,)), ((), ())), preferred_element_type=F32)


def _dot_tn(a, b):
    return lax.dot_general(a, b, (((0,), (0,)), ((), ())), preferred_element_type=F32)


def _sb_scores(qs, kj, mask):
    zz = _dot_nt(qs, kj)
    log_beta = jnp.minimum(zz, 0.0) - jnp.log(1.0 + jnp.exp(-jnp.abs(zz)))
    log_1m = log_beta - zz
    if mask is not None:
        log_1m = jnp.where(mask, log_1m, 0.0)
    return log_beta, log_1m


def _masked(mask, x):
    return x if mask is None else jnp.where(mask, x, 0.0)


def _below(old, new, row0):
    if row0 == 0:
        return tuple(new)
    return tuple(jnp.concatenate([o[:row0], n], axis=0) for o, n in zip(old, new))


def _sb_tiles(T, narrow=False):
    tk = _pick(128 if narrow else 256, [T])
    per = 4 if narrow else 2
    tq = per * tk if T % (per * tk) == 0 else tk
    return tq, tk


def _sb_cols(w_a, w_b):
    base = 2 * w_a // GROUP_DIM
    per = w_b // GROUP_DIM
    return base, base + per, base + 2 * per


def _sb_forward(name, z, w_a, w_b):
    T = z.shape[0]
    H = w_b // GROUP_DIM
    tq, tk = _sb_tiles(T, narrow=True)
    per = tq // tk
    qc, kc, vc = _sb_cols(w_a, w_b)
    scale = GROUP_DIM ** -0.5

    def body(q_ref, k_ref, v_ref, y_ref, tot_ref):
        i = pl.program_id(1)
        qs = (q_ref[...] * scale).astype(BF16)
        upper = _tri(tk, lambda r, c: r > c)
        ahead = (lax.broadcasted_iota(jnp.int32, (tq, tk), 1)
                 - lax.broadcasted_iota(jnp.int32, (tq, tk), 0))

        def step(j, carry, masked, row0=0):
            acc, later = (c[row0:] for c in carry)
            k0 = pl.multiple_of(j * tk, tk)
            kj = k_ref[pl.ds(k0, tk), :].astype(BF16)
            vj = v_ref[pl.ds(k0, tk), :].astype(BF16)
            mask = ahead[row0:] < i * tq - k0 if masked else None
            log_beta, log_1m = _sb_scores(qs[row0:], kj, mask)
            rest = _split_dot(log_1m, upper) + later
            a = _masked(mask, jnp.exp(log_beta + rest))
            acc = acc + jnp.dot(a.astype(BF16), vj, preferred_element_type=F32)
            later = later + jnp.sum(log_1m, axis=1, keepdims=True)
            return _below(carry, (acc, later), row0)

        def blocks(p, c):
            for d in reversed(range(per)):
                c = step(p * per + d, c, False)
            return c

        carry = (jnp.zeros((tq, GROUP_DIM), F32), jnp.zeros((tq, 1), F32))
        for d in reversed(range(per)):
            carry = step(i * per + d, carry, True, d * tk)
        acc, total = lax.fori_loop(0, i, lambda pp, c: blocks(i - 1 - pp, c), carry)
        y_ref[...] = acc
        tot_ref[...] = total

    return pl.pallas_call(
        body, name=name, grid=(H, T // tq),
        in_specs=[pl.BlockSpec((tq, GROUP_DIM), lambda h, i: (i, qc + h)),
                  pl.BlockSpec((T, GROUP_DIM), lambda h, i: (0, kc + h)),
                  pl.BlockSpec((T, GROUP_DIM), lambda h, i: (0, vc + h))],
        out_specs=[pl.BlockSpec((tq, GROUP_DIM), lambda h, i: (i, h)),
                   pl.BlockSpec((None, tq, 1), lambda h, i: (h, i, 0))],
        out_shape=[jax.ShapeDtypeStruct((T, w_b), F32), jax.ShapeDtypeStruct((H, T, 1), F32)],
        compiler_params=_params(2))(z, z, z)


def _sb_backward(name, z, do, total, w_a, w_b):
    T = z.shape[0]
    H = w_b // GROUP_DIM
    tq, tk = _sb_tiles(T)
    per = tq // tk
    qc, kc, vc = _sb_cols(w_a, w_b)
    scale = GROUP_DIM ** -0.5

    def body(q_ref, k_ref, v_ref, do_ref, tot_ref, dq_ref, dkv_out_ref, dkv_ref):
        i = pl.program_id(1)

        @pl.when(i == 0)
        def _():
            dkv_ref[...] = jnp.zeros_like(dkv_ref)

        qs = (q_ref[...] * scale).astype(BF16)
        dob = do_ref[...].astype(BF16)
        upto = _tri(tk, lambda r, c: r <= c)
        before = _tri(tk, lambda r, c: r < c)
        ahead = (lax.broadcasted_iota(jnp.int32, (tq, tk), 1)
                 - lax.broadcasted_iota(jnp.int32, (tq, tk), 0))

        def step(j, carry, masked, row0=0):
            dq, left, e_seen = (c[row0:] for c in carry)
            qr, dor = qs[row0:], dob[row0:]
            k0 = pl.multiple_of(j * tk, tk)
            kj = k_ref[pl.ds(k0, tk), :].astype(BF16)
            vj = v_ref[pl.ds(k0, tk), :].astype(BF16)
            mask = ahead[row0:] < i * tq - k0 if masked else None
            log_beta, log_1m = _sb_scores(qr, kj, mask)
            rest = left - _split_dot(log_1m, upto)
            a = _masked(mask, jnp.exp(log_beta + rest))
            e = a * _dot_nt(dor, vj)
            e_before = e_seen + jnp.dot(e.astype(BF16), before, preferred_element_type=F32)
            beta = jnp.exp(log_beta)
            dz = _masked(mask, e * (1.0 - beta) - beta * e_before).astype(BF16)
            dq = dq + jnp.dot(dz, kj, preferred_element_type=F32)
            dkv_ref[0, pl.ds(k0, tk), :] += _dot_tn(dz, qr)
            dkv_ref[1, pl.ds(k0, tk), :] += _dot_tn(a.astype(BF16), dor)
            left = left - jnp.sum(log_1m, axis=1, keepdims=True)
            e_seen = e_seen + jnp.sum(e, axis=1, keepdims=True)
            return _below(carry, (dq, left, e_seen), row0)

        def blocks(p, c):
            for d in range(per):
                c = step(p * per + d, c, False)
            return c

        carry = (jnp.zeros((tq, GROUP_DIM), F32), tot_ref[...], jnp.zeros((tq, 1), F32))
        carry = lax.fori_loop(0, i, blocks, carry)
        for d in range(per):
            carry = step(i * per + d, carry, True, d * tk)
        dq_ref[...] = (carry[0] * scale).astype(BF16)

        @pl.when(i == T // tq - 1)
        def _():
            dkv_out_ref[...] = dkv_ref[...].astype(BF16)

    return pl.pallas_call(
        body, name=name, grid=(H, T // tq),
        in_specs=[pl.BlockSpec((tq, GROUP_DIM), lambda h, i: (i, qc + h)),
                  pl.BlockSpec((T, GROUP_DIM), lambda h, i: (0, kc + h)),
                  pl.BlockSpec((T, GROUP_DIM), lambda h, i: (0, vc + h)),
                  pl.BlockSpec((tq, GROUP_DIM), lambda h, i: (i, h)),
                  pl.BlockSpec((None, tq, 1), lambda h, i: (h, i, 0))],
        out_specs=[pl.BlockSpec((tq, GROUP_DIM), lambda h, i: (i, h)),
                   pl.BlockSpec((2, T, GROUP_DIM), lambda h, i: (0, 0, h))],
        out_shape=[jax.ShapeDtypeStruct((T, w_b), BF16), jax.ShapeDtypeStruct((2, T, w_b), BF16)],
        scratch_shapes=[pltpu.VMEM((2, T, GROUP_DIM), F32)],
        compiler_params=_params(2))(z, z, z, do, total)


def _softmax_rows(s):
    m = jnp.max(s, axis=-1, keepdims=True)
    p = jnp.exp(s - m)
    return p / jnp.sum(p, axis=-1, keepdims=True)


def _xattn_forward(name, q, kv):
    T, D = q.shape
    Nm = kv.shape[0]
    dh = D // X_HEADS
    tq = _pick(1024, [T])

    def body(q_ref, k_ref, v_ref, o_ref):
        p = _softmax_rows(_dot_nt(q_ref[...], k_ref[...]))
        o_ref[...] = jnp.dot(p.astype(BF16), v_ref[...], preferred_element_type=F32).astype(BF16)

    return pl.pallas_call(
        body, name=name, grid=(T // tq, X_HEADS),
        in_specs=[pl.BlockSpec((tq, dh), lambda i, h: (i, h)),
                  pl.BlockSpec((Nm, dh), lambda i, h: (0, h)),
                  pl.BlockSpec((Nm, dh), lambda i, h: (0, X_HEADS + h))],
        out_specs=pl.BlockSpec((tq, dh), lambda i, h: (i, h)),
        out_shape=jax.ShapeDtypeStruct((T, D), BF16),
        compiler_params=_params(2))(q, kv, kv)


def _xattn_backward(name, q, kv, do):
    T, D = q.shape
    Nm = kv.shape[0]
    dh = D // X_HEADS
    tq = _pick(1024, [T])
    scale = dh ** -0.5

    def body(q_ref, k_ref, v_ref, do_ref, dq_ref, dkv_ref):
        @pl.when(pl.program_id(1) == 0)
        def _():
            dkv_ref[...] = jnp.zeros_like(dkv_ref)

        qv, kk, vv, dov = q_ref[...], k_ref[...], v_ref[...], do_ref[...]
        p = _softmax_rows(_dot_nt(qv, kk))
        dp = _dot_nt(dov, vv)
        ds = (p * (dp - jnp.sum(dp * p, axis=-1, keepdims=True))).astype(BF16)
        dq_ref[...] = (jnp.dot(ds, kk, preferred_element_type=F32) * scale).astype(BF16)
        dkv_ref[0] += _dot_tn(ds, qv)
        dkv_ref[1] += _dot_tn(p.astype(BF16), dov)

    blk = pl.BlockSpec((tq, dh), lambda h, i: (i, h))
    return pl.pallas_call(
        body, name=name, grid=(X_HEADS, T // tq),
        in_specs=[blk, pl.BlockSpec((Nm, dh), lambda h, i: (0, h)),
                  pl.BlockSpec((Nm, dh), lambda h, i: (0, X_HEADS + h)), blk],
        out_specs=[blk, pl.BlockSpec((2, Nm, dh), lambda h, i: (0, 0, h))],
        out_shape=[jax.ShapeDtypeStruct((T, D), BF16), jax.ShapeDtypeStruct((2, Nm, D), F32)],
        compiler_params=_params(2))(q, kv, kv, do)


def _position():
    x, y, c = lax.axis_index("x"), lax.axis_index("y"), lax.axis_index("c")
    other_chips = [(1 - x, y), (x, 1 - y), (1 - x, 1 - y)]
    return x, y, c, other_chips


def _hbm_spec():
    return pl.BlockSpec(memory_space=pltpu.HBM)


def _sem_spec():
    return pl.BlockSpec(memory_space=pltpu.SEMAPHORE)


def _split_start(name, arrays, make_copies, n_sems, deps=()):
    n, d = len(arrays), len(deps)

    def body(*refs):
        ins = refs[:n]
        send_sems, recv_sems = refs[n + d], refs[n + d + 1]
        token = refs[-1]
        for cp in make_copies(ins, send_sems, recv_sems):
            cp.start()
        token[...] = jnp.zeros_like(token)

    res = pl.pallas_call(
        body, name=name,
        out_shape=(pltpu.SemaphoreType.DMA((n_sems,)), pltpu.SemaphoreType.DMA((n_sems,)),
                   *[pltpu.HBM(a.shape, a.dtype) for a in arrays],
                   jax.ShapeDtypeStruct((SUBLANE, LANE), F32)),
        in_specs=[_hbm_spec()] * n + [_any_spec()] * d,
        out_specs=(_sem_spec(), _sem_spec(), *[_hbm_spec()] * n,
                   pl.BlockSpec(memory_space=pltpu.VMEM)),
        input_output_aliases={i: 2 + i for i in range(n)},
        compiler_params=pltpu.CompilerParams(
            has_side_effects=pltpu.SideEffectType.DATAFLOW_SIDE_EFFECTING),
    )(*[pltpu.with_memory_space_constraint(a, pltpu.HBM) for a in arrays], *deps)
    return res[0], res[1], list(res[2:2 + n]), res[-1]


def _split_wait(name, arrays, send_sems, recv_sems, after, make_copies):
    n = len(arrays)
    after = list(after) if isinstance(after, (list, tuple)) else [after]

    def body(*refs):
        ins = refs[:n]
        send_ref, recv_ref = refs[n], refs[n + 1]
        for cp in make_copies(ins, send_ref, recv_ref):
            cp.wait_send()
            cp.wait_recv()

    return pl.pallas_call(
        body, name=name,
        out_shape=tuple(pltpu.HBM(a.shape, a.dtype) for a in arrays),
        in_specs=[_hbm_spec()] * n + [_sem_spec(), _sem_spec()] + [_any_spec()] * len(after),
        out_specs=tuple(_hbm_spec() for _ in arrays),
        input_output_aliases={i: i for i in range(n)},
        compiler_params=pltpu.CompilerParams(
            has_side_effects=pltpu.SideEffectType.DATAFLOW_SIDE_EFFECTING),
    )(*arrays, send_sems, recv_sems, *after)


def _gather_copies(refs, send_sems, recv_sems):
    x, y, c, chips = _position()
    me = 2 * x + y
    copies = []
    for i, ref in enumerate(refs):
        rows = ref.shape[1] // 2
        piece = ref.at[me, pl.ds(c * rows, rows), :]
        for j, (px, py) in enumerate(chips):
            copies.append(pltpu.make_async_remote_copy(
                src_ref=piece, dst_ref=piece, send_sem=send_sems.at[3 * i + j],
                recv_sem=recv_sems.at[3 * i + j], device_id=(px, py, c), device_id_type=MESH))
    return copies


def _near_copies(refs, send_sems, recv_sems):
    x, y, c, chips = _position()
    me = 2 * x + y
    copies = []
    for i, ref in enumerate(refs):
        rows = ref.shape[1] // 2
        piece = ref.at[me, pl.ds(c * rows, rows), :]
        for j, (px, py) in enumerate(chips[:2]):
            copies.append(pltpu.make_async_remote_copy(
                src_ref=piece, dst_ref=piece, send_sem=send_sems.at[2 * i + j],
                recv_sem=recv_sems.at[2 * i + j], device_id=(px, py, c), device_id_type=MESH))
    return copies


def _relay_copies(refs, send_sems, recv_sems):
    x, y, c, chips = _position()
    copies = []
    for i, ref in enumerate(refs):
        rows = ref.shape[1] // 4
        for j, (px, py) in enumerate(chips[:2]):
            ox, oy = chips[1 - j]
            piece = ref.at[2 * ox + oy, pl.ds((2 * c + j) * rows, rows), :]
            copies.append(pltpu.make_async_remote_copy(
                src_ref=piece, dst_ref=piece, send_sem=send_sems.at[2 * i + j],
                recv_sem=recv_sems.at[2 * i + j], device_id=(px, py, c), device_id_type=MESH))
    return copies


def _share_copies(refs, send_sems, recv_sems):
    x, y, c, _ = _position()
    copies = []
    for i, ref in enumerate(refs):
        rows = ref.shape[0] // 2
        mine = ref.at[pl.ds(c * rows, rows), :]
        copies.append(pltpu.make_async_remote_copy(
            src_ref=mine, dst_ref=mine, send_sem=send_sems.at[i], recv_sem=recv_sems.at[i],
            device_id=(x, y, 1 - c), device_id_type=MESH))
    return copies


def _scatter_copies(refs, send_sems, recv_sems):
    x, y, c, chips = _position()
    n = len(refs) // 2
    copies = []
    for i in range(n):
        for j, (px, py) in enumerate(chips):
            copies.append(pltpu.make_async_remote_copy(
                src_ref=refs[i].at[2 * px + py], dst_ref=refs[n + i].at[j],
                send_sem=send_sems.at[3 * i + j], recv_sem=recv_sems.at[3 * i + j],
                device_id=(px, py, c), device_id_type=MESH))
    return copies


def _cast_own(name, place, shard):
    rows, cols = shard.shape
    tr = _block_rows(rows, cols)

    def body(place_ref, w_ref, o_ref):
        o_ref[...] = w_ref[...].astype(BF16)

    grid_spec = pltpu.PrefetchScalarGridSpec(
        num_scalar_prefetch=1, grid=(rows // tr,),
        in_specs=[pl.BlockSpec((tr, cols), lambda r, pr: (r, 0))],
        out_specs=pl.BlockSpec((None, tr, cols), lambda r, pr: (pr[0], r, 0)))
    return pl.pallas_call(
        body, name=name, grid_spec=grid_spec,
        out_shape=jax.ShapeDtypeStruct((N_CHIPS, rows, cols), BF16),
        compiler_params=_params(1))(place, shard)


def _forward_to_sibling(name, arrays, deps=(), which=(0, 1, 2)):
    n = len(arrays)

    def body(*refs):
        ins = refs[:n]
        send_sems, recv_sems = refs[-2:]
        x, y, c, chips = _position()
        chips = [(j, chips[j]) for j in which]
        sends = []
        for i in range(n):
            rows = ins[i].shape[1] // 2
            for j, (px, py) in chips:
                piece = ins[i].at[2 * px + py, pl.ds(c * rows, rows), :]
                cp = pltpu.make_async_remote_copy(
                    src_ref=piece, dst_ref=piece, send_sem=send_sems.at[i, j],
                    recv_sem=recv_sems.at[i, j], device_id=(x, y, 1 - c), device_id_type=MESH)
                cp.start()
                sends.append(cp)
        for i in range(n):
            rows = ins[i].shape[1] // 2
            for j, (px, py) in chips:
                piece = ins[i].at[2 * px + py, pl.ds((1 - c) * rows, rows), :]
                pltpu.make_async_remote_copy(
                    src_ref=piece, dst_ref=piece, send_sem=send_sems.at[i, j],
                    recv_sem=recv_sems.at[i, j], device_id=(x, y, 1 - c),
                    device_id_type=MESH).wait_recv()
        for cp in sends:
            cp.wait_send()

    return pl.pallas_call(
        body, name=name,
        in_specs=[_any_spec()] * (n + len(deps)), out_specs=[_any_spec()] * n,
        out_shape=[jax.ShapeDtypeStruct(a.shape, a.dtype) for a in arrays],
        input_output_aliases={i: i for i in range(n)},
        scratch_shapes=[pltpu.SemaphoreType.DMA((n, 3))] * 2,
    )(*arrays, *deps)


def _forward_copies(which):
    def make(refs, send_sems, recv_sems):
        x, y, c, chips = _position()
        copies = []
        for i, ref in enumerate(refs):
            rows = ref.shape[1] // 2
            for k, j in enumerate(which):
                px, py = chips[j]
                piece = ref.at[2 * px + py, pl.ds(c * rows, rows), :]
                sem = len(which) * i + k
                copies.append(pltpu.make_async_remote_copy(
                    src_ref=piece, dst_ref=piece, send_sem=send_sems.at[sem],
                    recv_sem=recv_sems.at[sem], device_id=(x, y, 1 - c), device_id_type=MESH))
        return copies

    return make


def _swap_copies(refs, send_sems, recv_sems):
    x, y, c, _ = _position()
    n = len(refs) // 2
    copies = []
    for i in range(n):
        rows = refs[i].shape[1] // 2
        copies.append(pltpu.make_async_remote_copy(
            src_ref=refs[i].at[:, pl.ds((1 - c) * rows, rows), :], dst_ref=refs[n + i],
            send_sem=send_sems.at[i], recv_sem=recv_sems.at[i],
            device_id=(x, y, 1 - c), device_id_type=MESH))
    return copies


def _small_copies(refs, send_sems, recv_sems):
    packed, slots = refs
    x, y, c, _ = _position()
    me = 4 * x + 2 * y + c
    copies = []
    for r in range(1, N_DEV):
        peer = (x ^ ((r >> 2) & 1), y ^ ((r >> 1) & 1), c ^ (r & 1))
        copies.append(pltpu.make_async_remote_copy(
            src_ref=packed, dst_ref=slots.at[me], send_sem=send_sems.at[r - 1],
            recv_sem=recv_sems.at[r - 1], device_id=peer, device_id_type=MESH))
    return copies


def _block_rows(rows, cols, itemsize=4, target=1 << 20):
    return _pick(max(BF16_ROWS, target // (cols * itemsize)), [rows], unit=BF16_ROWS)


def _pair_sum(name, place, grad, received):
    P, rows, cols = received.shape
    tr = _block_rows(rows, cols, itemsize=2, target=2 << 20)
    nb = rows // tr

    def body(place_ref, g_ref, r_ref, o_ref):
        o_ref[...] = (g_ref[...].astype(F32) + r_ref[...].astype(F32)).astype(BF16)

    def panel(j, pr):
        return pr[0] ^ jnp.where(j == 2, 3, 2 - j)

    grid_spec = pltpu.PrefetchScalarGridSpec(
        num_scalar_prefetch=1, grid=(P - 1, nb),
        in_specs=[pl.BlockSpec((None, tr, cols),
                               lambda j, r, pr: (panel(j, pr), pr[1] * nb + r, 0)),
                  pl.BlockSpec((None, tr, cols), lambda j, r, pr: (panel(j, pr), r, 0))],
        out_specs=pl.BlockSpec((None, tr, cols), lambda j, r, pr: (panel(j, pr), r, 0)))
    return pl.pallas_call(
        body, name=name, grid_spec=grid_spec,
        out_shape=jax.ShapeDtypeStruct(received.shape, BF16),
        compiler_params=_params(2))(place, grad, received)


def _final_sum(name, place, grad, received, from_chips):
    _, rows, cols = received.shape
    tr = _block_rows(rows, cols, target=2 << 20)
    nb = rows // tr

    def body(place_ref, g_ref, r_ref, c_ref, o_ref):
        acc = g_ref[...].astype(F32) + r_ref[...].astype(F32)
        for j in range(3):
            acc = acc + c_ref[j].astype(F32)
        o_ref[...] = acc

    grid_spec = pltpu.PrefetchScalarGridSpec(
        num_scalar_prefetch=1, grid=(nb,),
        in_specs=[pl.BlockSpec((None, tr, cols), lambda r, pr: (pr[0], pr[1] * nb + r, 0)),
                  pl.BlockSpec((None, tr, cols), lambda r, pr: (pr[0], r, 0)),
                  pl.BlockSpec((3, tr, cols), lambda r, pr: (0, r, 0))],
        out_specs=pl.BlockSpec((tr, cols), lambda r, pr: (pr[1] * nb + r, 0)))
    return pl.pallas_call(
        body, name=name, grid_spec=grid_spec,
        out_shape=jax.ShapeDtypeStruct((2 * rows, cols), F32),
        compiler_params=_params(1))(place, grad, received, from_chips)


def _sum_devices(name, me, gathered, own):
    n_dev, rows, cols = gathered.shape
    tr = _pick(256, [rows])

    def body(me_ref, g_ref, own_ref, o_ref):
        term = lambda d: jnp.where(me_ref[0] == d, own_ref[...], g_ref[d])
        acc = term(0)
        for d in range(1, n_dev):
            acc = acc + term(d)
        o_ref[...] = acc

    grid_spec = pltpu.PrefetchScalarGridSpec(
        num_scalar_prefetch=1, grid=(rows // tr,),
        in_specs=[pl.BlockSpec((n_dev, tr, cols), lambda r, me_ref: (0, r, 0)),
                  pl.BlockSpec((tr, cols), lambda r, me_ref: (r, 0))],
        out_specs=pl.BlockSpec((tr, cols), lambda r, me_ref: (r, 0)))
    return pl.pallas_call(
        body, name=name, grid_spec=grid_spec,
        out_shape=jax.ShapeDtypeStruct((rows, cols), F32),
        compiler_params=_params(1))(me, gathered, own)


def _adamw(name, w, g, m, v):
    rows, cols = w.shape
    tr = _block_rows(rows, cols)
    c1 = 1.0 / (1.0 - ADAM_B1 ** ADAM_STEP)
    c2 = 1.0 / (1.0 - ADAM_B2 ** ADAM_STEP)

    def body(w_ref, g_ref, m_ref, v_ref, go_ref, d_ref, nm_ref, nv_ref):
        gv = g_ref[...]
        go_ref[...] = gv
        nm = ADAM_B1 * m_ref[...] + (1.0 - ADAM_B1) * gv
        nv = ADAM_B2 * v_ref[...] + (1.0 - ADAM_B2) * (gv * gv)
        nm_ref[...] = nm
        nv_ref[...] = nv
        d_ref[...] = -ADAM_LR * ((nm * c1) / (jnp.sqrt(nv * c2) + ADAM_EPS) + ADAM_WD * w_ref[...])

    blk = pl.BlockSpec((tr, cols), lambda r: (r, 0))
    shape = jax.ShapeDtypeStruct((rows, cols), F32)
    return pl.pallas_call(
        body, name=name, grid=(rows // tr,), in_specs=[blk] * 4, out_specs=[blk] * 4,
        out_shape=[shape] * 4, compiler_params=_params(1))(w, g, m, v)


BIG = ("ffn1_w_in", "ffn1_w_out", "w_mix_in", "w_mix_out", "w_cq", "w_ckv", "w_co",
       "ffn2_w_in", "ffn2_w_out")
BIG_KIND = {"ffn1_w_in": "c", "ffn1_w_out": "r", "w_mix_in": "c", "w_mix_out": "r", "w_cq": "r",
            "w_ckv": "c", "w_co": "r", "ffn2_w_in": "c", "ffn2_w_out": "r"}
GATHER_GROUPS = (("ffn1_in", ("ffn1_w_in",)), ("ffn1_out", ("ffn1_w_out",)),
                 ("mix_in", ("w_mix_in",)), ("mix_out", ("w_mix_out",)),
                 ("cross", ("w_cq", "w_ckv", "w_co")),
                 ("ffn2_in", ("ffn2_w_in",)), ("ffn2_out", ("ffn2_w_out",)))
GATHER_AFTER = (("ffn1_in", None), ("ffn1_out", "ffn1_in"), ("mix_in", "ffn1_out"),
                ("mix_out", "mix_in"), ("cross", "mix_in"), ("ffn2_in", "mix_in"),
                ("ffn2_out", "ffn2_in"))
RELAYED = ("ffn1_in", "ffn2_in")
TAIL_STAGES = (("sum", "ffn2"), ("sum", "cross"), ("sum", "mix"), ("sum", "ffn1_out"),
               ("update", "ffn2"), ("update", "cross"), ("sum", "ffn1_in"), ("update", "mix"),
               ("update", "ffn1_out"), ("update", "ffn1_in"))
SMALL = ("ffn1_norm", "mix_norm", "ln_v_gain", "ln_v_bias", "spatial_w", "spatial_b", "gnorm_a",
         "gnorm_b", "cross_norm", "mem_norm", "ffn2_norm", "final_norm")
WEIGHTS = ("ffn1_norm", "ffn1_w_in", "ffn1_w_out", "mix_norm", "w_mix_in", "ln_v_gain",
           "ln_v_bias", "spatial_w", "spatial_b", "gnorm_a", "gnorm_b", "w_mix_out", "cross_norm",
           "mem_norm", "w_cq", "w_ckv", "w_co", "ffn2_norm", "ffn2_w_in", "ffn2_w_out",
           "final_norm")


def _pack(arrays):
    return jnp.concatenate([a.reshape(-1, LANE) for a in arrays], axis=0)


def _unpack(packed, like):
    out, row = [], 0
    for a in like:
        rows = a.size // LANE
        out.append(packed[row:row + rows].reshape(a.shape))
        row += rows
    return out


def _local_step(x, mem, target, small, place, weights_of, forward_early, start_tokens,
                grads_ready, grads_flush):
    T, D = x.shape
    vec = lambda name: small[name].reshape(1, -1)
    w_a = small["ln_v_gain"].size
    w_b = small["gnorm_b"].size
    G = w_a // GROUP_DIM
    w_s = small["spatial_w"].reshape(G, SGU_BLOCK, SGU_BLOCK)
    b_t = small["spatial_b"].reshape(G, SGU_BLOCK).T

    h1, ffn1_saved = _ffn_forward("ffn1", x, vec("ffn1_norm"), weights_of, place,
                                  deps=start_tokens)
    n2 = _rmsnorm_fwd("mix_norm", h1, vec("mix_norm"))
    big = weights_of("mix_in", n2)
    (z,) = _matmul("mix_in", Mat(n2), big["w_mix_in"], "nn", [("c", 1, F32)], tm=2048, tn=256)
    z = _tie("z_after_forward_start", z[0], [forward_early("mix_out", z)])
    y = _sgu_forward("sgu", z, vec("ln_v_gain"), vec("ln_v_bias"), w_s, b_t, vec("gnorm_a"), D)
    yb, sb_total = _sb_forward("stickbreak", z, w_a, w_b)
    y = _rmsnorm_fwd("gnorm_b", yb, vec("gnorm_b"), into=y, col=w_a // w_b)
    y = _tie("y_after_forward_start", y, [forward_early("cross", y)])

    def add_res(acc, ex, out):
        out[0][...] = ex[0][...] + acc

    big.update(weights_of("mix_out", y))
    (h2,) = _matmul("mix_out", Mat(y), big["w_mix_out"], "nn", [("c", 1, F32)],
                    tm=1024, tn=1024, extras=[Mat(h1)], epi=add_res)
    h2 = h2[0]
    n3 = _rmsnorm_fwd("cross_norm", h2, vec("cross_norm"))
    memn = _rmsnorm_fwd("mem_norm", mem, vec("mem_norm"))
    big.update(weights_of("cross", n3))
    x_scale = (D // X_HEADS) ** -0.5

    def scaled(acc, ex, out):
        out[0][...] = (acc * x_scale).astype(BF16)

    (q,) = _matmul("cross_q", Mat(n3), big["w_cq"], "nn", [("c", 1, BF16)],
                   tm=1024, tn=1024, epi=scaled)
    (kv,) = _matmul("cross_kv", Mat(memn), big["w_ckv"], "nn", [("c", 1, BF16)], tm=256, tn=1024)
    q, kv = q[0], kv[0]
    o = _xattn_forward("cross_attn", q, kv)
    (h3,) = _matmul("cross_out", Mat(o), big["w_co"], "nn", [("c", 1, F32)],
                    tm=1024, tn=1024, extras=[Mat(h2)], epi=add_res)
    h3 = h3[0]
    h4, ffn2_saved = _ffn_forward("ffn2", h3, vec("ffn2_norm"), weights_of, place)

    gs = {}
    loss_tile, dh4, dh4_bf, gs["final_norm"] = _loss_head("loss_head", h4, vec("final_norm"), target)
    dh3, dh3_bf, gs["ffn2_norm"] = _ffn_backward(
        "ffn2", h3, vec("ffn2_norm"), ffn2_saved, dh4, dh4_bf, grads_ready, grads_flush)

    (do,) = _matmul("cross_do", Mat(dh3_bf), big["w_co"], "nt", [("c", 1, BF16)], tm=512, tn=2048)
    (dw_co,) = _matmul("cross_dwo", Mat(o), Mat(dh3_bf), "tn", [("r", N_CHIPS, BF16)],
                       tm=512, tn=1024)
    dq, dkv = _xattn_backward("cross_attn_bwd", q, kv, do[0])
    (dw_cq,) = _matmul("cross_dwq", Mat(n3), Mat(dq), "tn", [("r", N_CHIPS, BF16)],
                       tm=512, tn=1024)
    (dw_ckv,) = _matmul("cross_dwkv", Mat(memn), Mat(dkv), "tn", [("c", N_CHIPS, BF16)],
                        tm=1024, tn=1024)
    token = grads_ready("cross", {"w_cq": dw_cq, "w_ckv": dw_ckv, "w_co": dw_co})
    dq = _tie("cross_dq_after_swap", dq, [token])
    (dn3,) = _matmul("cross_dn", Mat(dq), big["w_cq"], "nt", [("c", 1, F32)], tm=512, tn=2048)
    (dmemn,) = _matmul("cross_dmem", Mat(dkv), big["w_ckv"], "nt", [("c", 1, F32)],
                       tm=256, tn=1024, tk=1024)
    (gs["mem_norm"],) = _rmsnorm_bwd("mem_dnorm", mem, vec("mem_norm"), dmemn[0], want_dx=False)
    dn3 = _tie("cross_dn_after_scatter", dn3, [grads_flush("cross", gs["mem_norm"])])
    dh2, dh2_bf, gs["cross_norm"] = _rmsnorm_bwd("cross_dnorm", h2, vec("cross_norm"), dn3[0],
                                                 dres=dh3)

    (dy,) = _matmul("mix_dy", Mat(dh2_bf), big["w_mix_out"], "nt", [("c", 1, F32)], tm=512, tn=2048)
    dy = dy[0]
    (dw_mix_out,) = _matmul("mix_dwout", Mat(y), Mat(dh2_bf), "tn", [("r", N_CHIPS, BF16)],
                            tm=512, tn=1024)
    dza, gs["ln_v_gain"], gs["ln_v_bias"], gs["spatial_w"], db, gs["gnorm_a"] = _sgu_backward(
        "sgu_bwd", z, dy, vec("ln_v_gain"), vec("ln_v_bias"), w_s, b_t, vec("gnorm_a"))
    gs["spatial_b"] = db.reshape(G, SGU_BLOCK)
    dob, gs["gnorm_b"] = _rmsnorm_bwd("gnorm_b_bwd", yb, vec("gnorm_b"), dy, dn_col=w_a // w_b,
                                      want_bf16=False)
    dqb, dkvb = _sb_backward("stickbreak_bwd", z, dob, sb_total, w_a, w_b)
    dz = jnp.concatenate([dza, dqb, dkvb[0], dkvb[1]], axis=1)
    (dw_mix_in,) = _matmul("mix_dwin", Mat(n2), Mat(dz), "tn", [("c", N_CHIPS, BF16)],
                           tm=1024, tn=1280)
    token = grads_ready("mix", {"w_mix_in": dw_mix_in, "w_mix_out": dw_mix_out})
    dz = _tie("mix_dz_after_swap", dz, [token])
    (dn2,) = _matmul("mix_dn", Mat(dz), big["w_mix_in"], "nt", [("c", 1, F32)],
                     tm=1024, tn=1024, tk=1280)
    dn2 = _tie("mix_dn_after_scatter", dn2, [grads_flush("mix", dn2)])
    dh1, dh1_bf, gs["mix_norm"] = _rmsnorm_bwd("mix_dnorm", h1, vec("mix_norm"), dn2[0], dres=dh2)

    dx, _, gs["ffn1_norm"] = _ffn_backward(
        "ffn1", x, vec("ffn1_norm"), ffn1_saved, dh1, dh1_bf, grads_ready, grads_flush,
        early_out=True)
    gs = {k: g.reshape(small[k].shape) for k, g in gs.items()}
    return loss_tile, dx, gs


def kernel(x, mem, ffn1_norm, ffn1_w_in, ffn1_w_out, mix_norm, w_mix_in, ln_v_gain, ln_v_bias, spatial_w, spatial_b, gnorm_a, gnorm_b, w_mix_out, cross_norm, mem_norm, w_cq, w_ckv, w_co, ffn2_norm, ffn2_w_in, ffn2_w_out, final_norm, loss_target, m_ffn1_norm, m_ffn1_w_in, m_ffn1_w_out, m_mix_norm, m_w_mix_in, m_ln_v_gain, m_ln_v_bias, m_spatial_w, m_spatial_b, m_gnorm_a, m_gnorm_b, m_w_mix_out, m_cross_norm, m_mem_norm, m_w_cq, m_w_ckv, m_w_co, m_ffn2_norm, m_ffn2_w_in, m_ffn2_w_out, m_final_norm, v_ffn1_norm, v_ffn1_w_in, v_ffn1_w_out, v_mix_norm, v_w_mix_in, v_ln_v_gain, v_ln_v_bias, v_spatial_w, v_spatial_b, v_gnorm_a, v_gnorm_b, v_w_mix_out, v_cross_norm, v_mem_norm, v_w_cq, v_w_ckv, v_w_co, v_ffn2_norm, v_ffn2_w_in, v_ffn2_w_out, v_final_norm):
    given = dict(locals())
    w = {k: given[k] for k in WEIGHTS}
    m = {k: given["m_" + k] for k in WEIGHTS}
    v = {k: given["v_" + k] for k in WEIGHTS}

    cx, cy, cc = lax.axis_index("x"), lax.axis_index("y"), lax.axis_index("c")
    place = jnp.stack([2 * cx + cy, cc]).astype(jnp.int32)

    names_of = dict(GATHER_GROUPS)
    own = {g: [_cast_own(f"cast_{k}", place, w[k][0]) for k in names] for g, names in GATHER_GROUPS}
    gathers = {}

    def start_gather(group, deps):
        first_hop = _near_copies if group in RELAYED else _gather_copies
        n_sems = (2 if group in RELAYED else 3) * len(own[group])
        send, recv, arrays, token = _split_start(f"gather_start_{group}", own[group],
                                                 first_hop, n_sems, deps)
        gathers[group] = (send, recv, arrays)
        return token

    start_tokens = [start_gather(g, ()) for g, after in GATHER_AFTER if after is None]
    start_tokens += [a for g, after in GATHER_AFTER if after is not None for a in own[g]]

    def weights_of(group, after):
        as_mats = lambda arrs: {k: Mat(a, BIG_KIND[k]) for k, a in zip(names_of[group], arrs)}
        if group in forwards:
            send, recv, arrays = forwards[group]
            return as_mats(_split_wait(f"gather_forward_wait_{group}", arrays, send, recv, after,
                                       _forward_copies((0, 1, 2))))
        send, recv, arrays = gathers[group]
        if group not in RELAYED:
            arrays = _split_wait(f"gather_wait_{group}", arrays, send, recv, after, _gather_copies)
            tokens = [start_gather(g, (arrays[0],)) for g, a in GATHER_AFTER if a == group]
            return as_mats(_forward_to_sibling(f"gather_forward_{group}", list(arrays), tokens))
        arrays = _split_wait(f"gather_wait_{group}", arrays, send, recv, after, _near_copies)
        send, recv, arrays, token = _split_start(f"gather_relay_{group}", list(arrays),
                                                 _relay_copies, 2 * len(arrays))
        tokens = [token] + [start_gather(g, (arrays[0],)) for g, a in GATHER_AFTER if a == group]
        arrays = _forward_to_sibling(f"gather_forward_{group}", list(arrays), tokens, which=(0,))
        y_send, y_recv, arrays, _ = _split_start(f"gather_forward_y_{group}", list(arrays),
                                                 _forward_copies((1,)), len(arrays))

        def finish(after):
            arrs = _split_wait(f"gather_relay_wait_{group}", arrays, send, recv, after,
                               _relay_copies)
            arrs = _split_wait(f"gather_forward_y_wait_{group}", arrs, y_send, y_recv, after,
                               _forward_copies((1,)))
            return as_mats(_forward_to_sibling(f"gather_forward_diag_{group}", list(arrs),
                                               which=(2,)))

        return {**as_mats(arrays), "finish": finish}

    forwards = {}

    def forward_early(group, after):
        send, recv, arrays = gathers[group]
        arrays = _split_wait(f"gather_wait_{group}", arrays, send, recv, after, _gather_copies)
        send, recv, arrays, token = _split_start(f"gather_forward_start_{group}", list(arrays),
                                                 _forward_copies((0, 1, 2)), 3 * len(arrays))
        forwards[group] = (send, recv, arrays)
        return token

    swaps, scatters = {}, {}

    def grads_ready(group, partial):
        names = list(partial)
        grads_ = [partial[k] for k in names]
        lands = [lax.empty((g.shape[0], g.shape[1] // 2, g.shape[2]), g.dtype) for g in grads_]
        send, recv, arrays, token = _split_start(f"swap_start_{group}", grads_ + lands,
                                                 _swap_copies, len(names))
        swaps[group] = (names, send, recv, arrays)
        return token

    def grads_flush(group, after):
        names, send, recv, arrays = swaps[group]
        arrays = _split_wait(f"swap_wait_{group}", arrays, send, recv, after, _swap_copies)
        grads_, from_sibling = arrays[:len(names)], arrays[len(names):]
        sums = [_pair_sum(f"pair_sum_{k}", place, g, r)
                for k, g, r in zip(names, grads_, from_sibling)]
        lands = [lax.empty((3,) + s.shape[1:], s.dtype) for s in sums]
        send, recv, arrays, token = _split_start(f"scatter_start_{group}", sums + lands,
                                                 _scatter_copies, 3 * len(names))
        scatters[group] = (names, grads_, from_sibling, send, recv, arrays)
        return token

    small = {k: w[k] for k in SMALL}
    loss_tile, grad_x, gs = _local_step(x[0], mem[0], loss_target[0], small, place, weights_of,
                                        forward_early, start_tokens, grads_ready, grads_flush)

    packed = _pack([gs[k] for k in SMALL] + [loss_tile])
    slots = jnp.zeros((N_DEV,) + packed.shape, packed.dtype)
    small_send, small_recv, small_arrays, _ = _split_start(
        "small_start", [packed, slots], _small_copies, N_DEV - 1)

    grad, delta, new_m, new_v = {}, {}, {}, {}
    shares = {}
    after = [grad_x]
    for stage, group in TAIL_STAGES:
        if stage == "sum":
            names, grads_, from_sibling, send, recv, arrays = scatters[group]
            arrays = _split_wait(f"scatter_wait_{group}", arrays, send, recv, after,
                                 _scatter_copies)
            from_chips = arrays[len(names):]
            shards = [_final_sum(f"final_sum_{k}", place, g, r, f)
                      for k, g, r, f in zip(names, grads_, from_sibling, from_chips)]
            send, recv, shards, token = _split_start(f"share_start_{group}", shards,
                                                     _share_copies, len(names))
            shares[group] = (names, send, recv, shards)
            after = [token]
        else:
            names, send, recv, shards = shares[group]
            shards = _split_wait(f"share_wait_{group}", shards, send, recv, after, _share_copies)
            after = []
            for k, g_ in zip(names, shards):
                g_, d_, m_, v_ = _adamw(f"adamw_{k}", w[k][0], g_, m[k][0], v[k][0])
                grad[k], delta[k], new_m[k], new_v[k] = g_[None], d_[None], m_[None], v_[None]
                after.append(v_)

    packed, slots = _split_wait("small_wait", small_arrays, small_send, small_recv, after,
                                _small_copies)
    me = (4 * cx + 2 * cy + cc).astype(jnp.int32).reshape(1)
    total = _sum_devices("sum_small", me, slots, packed)
    n_small = total.shape[0] - SUBLANE
    loss = total[n_small, 0]
    small_g = total[:n_small]
    g_s, d_s, m_s, v_s = _adamw("adamw_small", _pack([w[k] for k in SMALL]), small_g,
                                _pack([m[k] for k in SMALL]), _pack([v[k] for k in SMALL]))
    like = [w[k] for k in SMALL]
    for k, g_, d_, m_, v_ in zip(SMALL, _unpack(g_s, like), _unpack(d_s, like),
                                 _unpack(m_s, like), _unpack(v_s, like)):
        grad[k], delta[k], new_m[k], new_v[k] = g_, d_, m_, v_

    return (loss, grad_x[None], *[grad[k] for k in WEIGHTS], *[delta[k] for k in WEIGHTS],
            *[new_m[k] for k in WEIGHTS], *[new_v[k] for k in WEIGHTS])
```

```python
import functools
import math

import jax
import jax.numpy as jnp
from jax import lax
from jax.experimental import pallas as pl
from jax.experimental.pallas import tpu as pltpu

F32 = jnp.float32
BF16 = jnp.bfloat16
MESH = pl.DeviceIdType.MESH

EPS = 1e-6
CHUNK = 64
SGU_BLOCK = 128
GROUP_DIM = 128
X_HEADS = 4
N_CHIPS = 4
N_DEV = 8
LANE = 128
SUBLANE = 8
BF16_ROWS = 16

ADAM_LR = 0.001
ADAM_B1 = 0.9
ADAM_B2 = 0.999
ADAM_EPS = 1e-08
ADAM_WD = 0.01
ADAM_STEP = 10

V7X_VMEM_BYTES = 64 << 20
VMEM_LIMIT = V7X_VMEM_BYTES - (8 << 20)


def _params(n_grid):
    return pltpu.CompilerParams(dimension_semantics=("arbitrary",) * n_grid,
                                vmem_limit_bytes=VMEM_LIMIT)


def _pick(pref, dims, unit=None):
    g = functools.reduce(math.gcd, dims)
    if unit is None:
        unit = LANE if g % LANE == 0 else SUBLANE
    cands = [d for d in range(unit, g + 1, unit) if g % d == 0] or [g]
    return min(cands, key=lambda d: abs(math.log(d / pref)))


def _any_spec():
    return pl.BlockSpec(memory_space=pl.ANY)


class Mat:
    def __init__(self, arr, kind="c"):
        if arr.ndim == 2:
            arr = arr[None]
        self.arr, self.kind = arr, kind
        self.P, self.prow, self.pcol = arr.shape
        self.rows = self.prow * (self.P if kind == "r" else 1)
        self.cols = self.pcol * (self.P if kind == "c" else 1)
        self.dtype = arr.dtype

    def spec(self, tr, tc, rc_fn):
        if self.kind == "c":
            per = self.pcol // tc
            assert per * tc == self.pcol, (self.pcol, tc)

            def imap(*g):
                i, j = rc_fn(*g)
                return (j // per, i, j % per)
        else:
            per = self.prow // tr
            assert per * tr == self.prow, (self.prow, tr)

            def imap(*g):
                i, j = rc_fn(*g)
                return (i // per, i % per, j)
        return pl.BlockSpec((None, tr, tc), imap)

    def two_d(self):
        assert self.P == 1
        return self.arr[0]


def _out_mat(kind, P, rows, cols, dtype):
    shape = (P, rows, cols // P) if kind == "c" else (P, rows // P, cols)
    return jax.ShapeDtypeStruct(shape, dtype)


def _matmul(name, A, B, mode, outs, *, tm=1024, tn=1024, tk=2048, extras=(), epi=None):
    if mode == "nn":
        M, K, N = A.rows, A.cols, B.cols
        assert B.rows == K
    elif mode == "nt":
        M, K, N = A.rows, A.cols, B.rows
        assert B.cols == K
    else:
        K, M, N = A.rows, A.cols, B.cols
        assert B.rows == K
    mdims, ndims, kdims = [M], [N], [K]
    whole_b = mode == "nn" and B.kind == "r" and B.P > 1 and K <= tk
    whole_bt = mode == "nt" and B.kind == "r" and B.P > 1 and N <= tn
    if whole_b:
        kdims.append(A.pcol)
        ndims.append(B.pcol)
    elif whole_bt:
        kdims += [A.pcol, B.pcol]
    elif mode == "tn":
        assert A.kind == "c" and B.kind == "c"
        mdims.append(A.pcol)
        ndims.append(B.pcol)
    else:
        (mdims if A.kind == "r" else kdims).append(A.prow if A.kind == "r" else A.pcol)
        if mode == "nn":
            (kdims if B.kind == "r" else ndims).append(B.prow if B.kind == "r" else B.pcol)
        else:
            (ndims if B.kind == "r" else kdims).append(B.prow if B.kind == "r" else B.pcol)
    for o in list(outs) + list(extras):
        if isinstance(o, Mat):
            (mdims if o.kind == "r" else ndims).append(o.prow if o.kind == "r" else o.pcol)
        elif isinstance(o[0], str):
            (mdims if o[0] == "r" else ndims).append((M if o[0] == "r" else N) // o[1])
    tm, tn = _pick(tm, mdims), _pick(tn, ndims)
    tk = K if mode == "tn" else _pick(tk, kdims)
    nk = K // tk
    grid = (M // tm, N // tn, nk)

    if mode == "tn":
        a_spec = A.spec(K, tm, lambda m, n, k: (0, m))
        b_spec = B.spec(K, tn, lambda m, n, k: (0, n))
    else:
        a_spec = A.spec(tm, tk, lambda m, n, k: (m, k))
        if whole_b:
            b_spec = pl.BlockSpec((B.P, B.prow, tn), lambda m, n, k: (0, 0, n))
        elif whole_bt:
            b_spec = pl.BlockSpec((B.P, B.prow, tk), lambda m, n, k: (0, 0, k))
        elif mode == "nn":
            b_spec = B.spec(tk, tn, lambda m, n, k: (k, n))
        else:
            b_spec = B.spec(tn, tk, lambda m, n, k: (n, k))

    def mn_spec(o):
        if isinstance(o, Mat):
            return o.spec(tm, tn, lambda m, n, k: (m, n))
        if isinstance(o[0], str):
            kind, P = o[0], o[1]
            fake = Mat.__new__(Mat)
            fake.kind, fake.P = kind, P
            fake.prow = M // P if kind == "r" else M
            fake.pcol = N // P if kind == "c" else N
            return Mat.spec(fake, tm, tn, lambda m, n, k: (m, n))
        return o[1](tm, tn)

    out_shapes = tuple(_out_mat(o[0], o[1], M, N, o[2]) if isinstance(o[0], str) else o[0]
                       for o in outs)
    out_specs = tuple(mn_spec(o) for o in outs)
    extra_arrays = tuple(e.arr if isinstance(e, Mat) else e[0] for e in extras)
    extra_specs = tuple(mn_spec(e) for e in extras)
    n_ex, n_out = len(extras), len(outs)
    tt = _pick(256, [tm])
    dims = (((1,), (1 if mode == "nt" else 0,)), ((), ()))

    def body(*refs):
        a_ref, b_ref = refs[:2]
        ex_refs = refs[2:2 + n_ex]
        out_refs = refs[2 + n_ex:2 + n_ex + n_out]
        scratch = refs[2 + n_ex + n_out:]
        if mode == "tn":
            at_ref = scratch[0]

            @pl.when(pl.program_id(1) == 0)
            def _():
                for c0 in range(0, tm, tt):
                    at_ref[c0:c0 + tt, :] = a_ref[:, c0:c0 + tt].astype(F32).T.astype(BF16)

            lhs = at_ref[...]
        else:
            lhs = a_ref[...].astype(BF16)
        rhs = b_ref[...]
        if whole_b or whole_bt:
            rhs = rhs.reshape(B.P * B.prow, rhs.shape[-1])
        part = lax.dot_general(lhs, rhs.astype(BF16), dims, preferred_element_type=F32)

        def finish(acc):
            if epi is None:
                out_refs[0][...] = acc.astype(out_refs[0].dtype)
            else:
                epi(acc, ex_refs, out_refs)

        if nk == 1:
            finish(part)
        else:
            acc_ref = scratch[0]
            k = pl.program_id(2)

            @pl.when(k == 0)
            def _():
                acc_ref[...] = part

            @pl.when(k > 0)
            def _():
                acc_ref[...] += part

            @pl.when(k == nk - 1)
            def _():
                finish(acc_ref[...])

    scratch_shapes = []
    if mode == "tn":
        scratch_shapes.append(pltpu.VMEM((tm, K), BF16))
    elif nk > 1:
        scratch_shapes.append(pltpu.VMEM((tm, tn), F32))
    res = pl.pallas_call(
        body, name=name, grid=grid,
        in_specs=[a_spec, b_spec, *extra_specs], out_specs=out_specs, out_shape=out_shapes,
        scratch_shapes=scratch_shapes, compiler_params=_params(3),
    )(A.arr, B.arr, *extra_arrays)
    return res


def _row_tile(T, streams=5):
    return _pick(512 if streams <= 3 else 256, [T])


def _tie(name, x, deps):
    def body(*refs):
        refs[-1][...] = jnp.zeros_like(refs[-1])

    return pl.pallas_call(
        body, name=name, in_specs=[_any_spec()] * (1 + len(deps)),
        out_specs=(_any_spec(), pl.BlockSpec(memory_space=pltpu.VMEM)),
        out_shape=(jax.ShapeDtypeStruct(x.shape, x.dtype),
                   jax.ShapeDtypeStruct((SUBLANE, LANE), F32)),
        input_output_aliases={0: 0},
    )(x, *deps)[0]


def _rmsnorm_fwd(name, x, g, *, into=None, col=0, deps=()):
    T, W = x.shape
    tr = _row_tile(T, streams=2)

    def body(x_ref, g_ref, *rest):
        o_ref = rest[-1]
        xv = x_ref[...]
        rstd = lax.rsqrt(jnp.mean(xv * xv, axis=-1, keepdims=True) + EPS)
        o_ref[...] = (xv * rstd * g_ref[...]).astype(o_ref.dtype)

    in_specs = [pl.BlockSpec((tr, W), lambda i: (i, 0)), pl.BlockSpec((1, W), lambda i: (0, 0))]
    args = [x, g]
    kwargs = {}
    if into is None:
        out_shape = jax.ShapeDtypeStruct((T, W), BF16)
    else:
        out_shape = jax.ShapeDtypeStruct(into.shape, into.dtype)
        in_specs.append(_any_spec())
        args.append(into)
        kwargs["input_output_aliases"] = {2: 0}
    in_specs += [_any_spec()] * len(deps)
    args += list(deps)
    return pl.pallas_call(
        body, name=name, grid=(T // tr,), in_specs=in_specs,
        out_specs=pl.BlockSpec((tr, W), lambda i: (i, col)), out_shape=out_shape,
        compiler_params=_params(1), **kwargs)(*args)


def _rmsnorm_bwd(name, x, g, dn, *, dn_col=0, dres=None, want_dx=True, want_bf16=True):
    T, W = x.shape
    tr = _row_tile(T)
    has_res = dres is not None

    def body(*refs):
        x_ref, g_ref, dn_ref = refs[:3]
        pos = 3
        dres_ref = None
        if has_res:
            dres_ref = refs[pos]
            pos += 1
        outs = refs[pos:]
        dg_ref = outs[-1]
        xv = x_ref[...]
        rstd = lax.rsqrt(jnp.mean(xv * xv, axis=-1, keepdims=True) + EPS)
        xhat = xv * rstd
        dnv = dn_ref[...].astype(F32)

        @pl.when(pl.program_id(0) == 0)
        def _():
            dg_ref[...] = jnp.zeros_like(dg_ref)

        dg_ref[...] += jnp.sum(dnv * xhat, axis=0, keepdims=True)
        if want_dx:
            t = dnv * g_ref[...]
            dx = rstd * (t - xhat * jnp.mean(t * xhat, axis=-1, keepdims=True))
            if has_res:
                dx = dx + dres_ref[...]
            outs[0][...] = dx
            if want_bf16:
                outs[1][...] = dx.astype(BF16)

    row = pl.BlockSpec((tr, W), lambda i: (i, 0))
    in_specs = [row, pl.BlockSpec((1, W), lambda i: (0, 0)),
                pl.BlockSpec((tr, W), lambda i: (i, dn_col))]
    args = [x, g, dn]
    if has_res:
        in_specs.append(row)
        args.append(dres)
    out_shape, out_specs = [], []
    if want_dx:
        out_shape.append(jax.ShapeDtypeStruct((T, W), F32))
        out_specs.append(row)
        if want_bf16:
            out_shape.append(jax.ShapeDtypeStruct((T, W), BF16))
            out_specs.append(row)
    out_shape.append(jax.ShapeDtypeStruct((1, W), F32))
    out_specs.append(pl.BlockSpec((1, W), lambda i: (0, 0)))
    return pl.pallas_call(
        body, name=name, grid=(T // tr,), in_specs=in_specs, out_specs=out_specs,
        out_shape=out_shape, compiler_params=_params(1))(*args)


def _loss_head(name, h, g, target):
    T, W = h.shape
    tr = _row_tile(T)

    def body(h_ref, g_ref, t_ref, loss_ref, dx_ref, dxb_ref, dg_ref):
        xv = h_ref[...]
        gv = g_ref[...]
        rstd = lax.rsqrt(jnp.mean(xv * xv, axis=-1, keepdims=True) + EPS)
        xhat = xv * rstd
        diff = xhat * gv - t_ref[...]

        @pl.when(pl.program_id(0) == 0)
        def _():
            dg_ref[...] = jnp.zeros_like(dg_ref)
            loss_ref[...] = jnp.zeros_like(loss_ref)

        loss_ref[...] += 0.5 * jnp.sum(jnp.mean(diff * diff, axis=-1, keepdims=True))
        dnv = diff * (1.0 / W)
        dg_ref[...] += jnp.sum(dnv * xhat, axis=0, keepdims=True)
        t = dnv * gv
        dx = rstd * (t - xhat * jnp.mean(t * xhat, axis=-1, keepdims=True))
        dx_ref[...] = dx
        dxb_ref[...] = dx.astype(BF16)

    row = pl.BlockSpec((tr, W), lambda i: (i, 0))
    vec = pl.BlockSpec((1, W), lambda i: (0, 0))
    return pl.pallas_call(
        body, name=name, grid=(T // tr,), in_specs=[row, vec, row],
        out_specs=[pl.BlockSpec((SUBLANE, LANE), lambda i: (0, 0)), row, row, vec],
        out_shape=[jax.ShapeDtypeStruct((SUBLANE, LANE), F32), jax.ShapeDtypeStruct((T, W), F32),
                   jax.ShapeDtypeStruct((T, W), BF16), jax.ShapeDtypeStruct((1, W), F32)],
        compiler_params=_params(1))(h, g, target)


def _sigmoid(x):
    return 1.0 / (1.0 + jnp.exp(-x))


def _ffn_in(name, n, W, place, half, prev=None):
    T, D = n.shape
    F = W.cols // 2
    tm = _pick(2048, [T])
    tn = _pick(512, [W.pcol])
    per = W.pcol // tn

    def body(place_ref, a_ref, wg_ref, wu_ref, *rest):
        gu_ref, act_ref = rest[-2:]
        a = a_ref[...]
        gate = jnp.dot(a, wg_ref[...], preferred_element_type=F32)
        up = jnp.dot(a, wu_ref[...], preferred_element_type=F32)
        sig = _sigmoid(gate)
        silu = gate * sig
        gu_ref[0] = (up * sig * (1.0 + gate * (1.0 - sig))).astype(BF16)
        gu_ref[1] = silu.astype(BF16)
        act_ref[...] = (silu * up).astype(BF16)

    def pair(pr):
        return (pr[0] + half) % 2

    in_specs = [pl.BlockSpec((tm, D), lambda m, j, pr: (m, 0)),
                pl.BlockSpec((None, D, tn), lambda m, j, pr: (pair(pr), 0, j)),
                pl.BlockSpec((None, D, tn), lambda m, j, pr: (2 + pair(pr), 0, j))]
    args = [place, n, W.arr, W.arr]
    kwargs = {}
    if prev is not None:
        in_specs += [_any_spec(), _any_spec()]
        args += list(prev)
        kwargs["input_output_aliases"] = {4: 0, 5: 1}
    grid_spec = pltpu.PrefetchScalarGridSpec(
        num_scalar_prefetch=1, grid=(T // tm, per), in_specs=in_specs,
        out_specs=[pl.BlockSpec((2, tm, tn), lambda m, j, pr: (0, m, pair(pr) * per + j)),
                   pl.BlockSpec((tm, tn), lambda m, j, pr: (m, pair(pr) * per + j))])
    return pl.pallas_call(
        body, name=name, grid_spec=grid_spec,
        out_shape=[jax.ShapeDtypeStruct((2, T, F), BF16), jax.ShapeDtypeStruct((T, F), BF16)],
        compiler_params=_params(2), **kwargs)(*args)


def _ffn_forward(tag, h, norm_g, weights_of, place, deps=()):
    n = _rmsnorm_fwd(f"{tag}_norm", h, norm_g, deps=deps)
    got = weights_of(f"{tag}_in", n)
    gu, act = _ffn_in(f"{tag}_in_a", n, got[f"{tag}_w_in"], place, 0)
    w_in = got["finish"](act)[f"{tag}_w_in"]
    gu, act = _ffn_in(f"{tag}_in_b", n, w_in, place, 1, (gu, act))
    w_out = weights_of(f"{tag}_out", act)[f"{tag}_w_out"]

    def epi(acc, ex, out):
        out[0][...] = ex[0][...] + 0.5 * acc

    (h_out,) = _matmul(f"{tag}_out", Mat(act), w_out, "nn", [("c", 1, F32)],
                       tm=1024, tn=512, tk=8192, extras=[Mat(h)], epi=epi)
    return h_out[0], (n, gu, act, w_in, w_out)


def _ffn_backward(tag, h_in, norm_g, saved, dh, dh_bf, grads_ready, grads_flush,
                  early_out=False):
    n, gu, act, w_in, w_out = saved
    T, F = act.shape

    def epi(acc, ex, out):
        dact = 0.5 * acc
        out[0][0] = (dact * ex[0][0].astype(F32)).astype(BF16)
        out[0][1] = (dact * ex[0][1].astype(F32)).astype(BF16)

    def pair_spec(tm, tn):
        return pl.BlockSpec((2, tm, tn), lambda m, j, k: (0, m, j))

    def half(acc, ex, out):
        out[0][...] = (0.5 * acc).astype(out[0].dtype)

    (dw_out,) = _matmul(f"{tag}_dwout", Mat(act), Mat(dh_bf), "tn", [("r", N_CHIPS, BF16)],
                        tm=1408, tn=512, epi=half)
    if early_out:
        token = grads_ready(f"{tag}_out", {f"{tag}_w_out": dw_out})
        dh_bf = _tie(f"{tag}_dh_after_swap", dh_bf, [token])
    (dgu,) = _matmul(f"{tag}_dact", Mat(dh_bf), w_out, "nt",
                     [(jax.ShapeDtypeStruct((2, T, F), BF16), pair_spec)],
                     tm=512, tn=1408, extras=[(gu, pair_spec)], epi=epi)
    if early_out:
        dgu = _tie(f"{tag}_dgu_after_scatter", dgu, [grads_flush(f"{tag}_out", dgu)])
    (dw_in,) = _matmul(f"{tag}_dwin", Mat(n), Mat(dgu), "tn", [("c", N_CHIPS, BF16)],
                       tm=1024, tn=1408)
    if early_out:
        group, partial = f"{tag}_in", {f"{tag}_w_in": dw_in}
    else:
        group, partial = tag, {f"{tag}_w_in": dw_in, f"{tag}_w_out": dw_out}
    dgu = _tie(f"{tag}_dgu_after_swap", dgu, [grads_ready(group, partial)])
    (dn,) = _matmul(f"{tag}_dn", Mat(dgu), w_in, "nt", [("c", 1, F32)],
                    tm=1024, tn=1024, tk=2816)
    dn = _tie(f"{tag}_dn_after_scatter", dn, [grads_flush(group, dn)])
    return _rmsnorm_bwd(f"{tag}_dnorm", h_in, norm_g, dn[0], dres=dh)


_GELU_C = math.sqrt(2.0 / math.pi)
_GELU_A = 0.044715


def _gelu(x):
    return 0.5 * x * (1.0 + jnp.tanh(_GELU_C * (x + _GELU_A * x * x * x)))


def _gelu_grad(x):
    th = jnp.tanh(_GELU_C * (x + _GELU_A * x * x * x))
    return 0.5 * (1.0 + th) + 0.5 * x * (1.0 - th * th) * _GELU_C * (1.0 + 3.0 * _GELU_A * x * x)


def _chunk_mask():
    t = lax.broadcasted_iota(jnp.int32, (SGU_BLOCK, SGU_BLOCK), 0) // CHUNK
    s = lax.broadcasted_iota(jnp.int32, (SGU_BLOCK, SGU_BLOCK), 1) // CHUNK
    return s <= t


def _sgu_group_forward(v_g, lg, lb, wm_bf, b_col):
    mu = jnp.mean(v_g, axis=-1, keepdims=True)
    xc = v_g - mu
    rstd = lax.rsqrt(jnp.mean(xc * xc, axis=-1, keepdims=True) + EPS)
    vhat = xc * rstd
    vn = vhat * lg + lb
    mixed = jnp.dot(wm_bf, vn.astype(BF16), preferred_element_type=F32) + b_col
    return vhat, rstd, vn, mixed


def _sgu_forward(name, z, ln_g, ln_b, w_s, b_t, gn, d_model):
    T = z.shape[0]
    W_A = ln_g.shape[1]
    G = W_A // GROUP_DIM

    def body(z_ref, lg_ref, lb_ref, w_ref, bt_ref, gn_ref, y_ref):
        mask = _chunk_mask()
        u = _gelu(z_ref[:, :W_A])
        v = _gelu(z_ref[:, W_A:])
        cols = []
        for g in range(G):
            sl = slice(g * GROUP_DIM, (g + 1) * GROUP_DIM)
            wm = jnp.where(mask, w_ref[g], 0.0).astype(BF16)
            _, _, _, mixed = _sgu_group_forward(v[:, sl], lg_ref[:, sl], lb_ref[:, sl], wm,
                                                bt_ref[:, g:g + 1])
            cols.append(u[:, sl] * mixed)
        ya = jnp.concatenate(cols, axis=1)
        rstd = lax.rsqrt(jnp.mean(ya * ya, axis=-1, keepdims=True) + EPS)
        y_ref[...] = (ya * rstd * gn_ref[...]).astype(BF16)

    vec = pl.BlockSpec((1, W_A), lambda i: (0, 0))
    return pl.pallas_call(
        body, name=name, grid=(T // SGU_BLOCK,),
        in_specs=[pl.BlockSpec((SGU_BLOCK, 2 * W_A), lambda i: (i, 0)), vec, vec,
                  pl.BlockSpec((G, SGU_BLOCK, SGU_BLOCK), lambda i: (0, 0, 0)),
                  pl.BlockSpec((SGU_BLOCK, G), lambda i: (0, 0)), vec],
        out_specs=pl.BlockSpec((SGU_BLOCK, W_A), lambda i: (i, 0)),
        out_shape=jax.ShapeDtypeStruct((T, d_model), BF16),
        compiler_params=_params(1))(z, ln_g, ln_b, w_s, b_t, gn)


def _sgu_backward(name, z, dy, ln_g, ln_b, w_s, b_t, gn):
    T = z.shape[0]
    W_A = ln_g.shape[1]
    G = W_A // GROUP_DIM

    def body(z_ref, dy_ref, lg_ref, lb_ref, w_ref, bt_ref, gn_ref,
             dz_ref, dlg_ref, dlb_ref, dw_ref, db_ref, dgn_ref):
        @pl.when(pl.program_id(0) == 0)
        def _():
            for r in (dlg_ref, dlb_ref, dw_ref, db_ref, dgn_ref):
                r[...] = jnp.zeros_like(r)

        mask = _chunk_mask()
        saved, cols = [], []
        for g in range(G):
            sl = slice(g * GROUP_DIM, (g + 1) * GROUP_DIM)
            sv = slice(W_A + g * GROUP_DIM, W_A + (g + 1) * GROUP_DIM)
            wm = jnp.where(mask, w_ref[g], 0.0)
            vhat, rstd, vn, mixed = _sgu_group_forward(
                _gelu(z_ref[:, sv]), lg_ref[:, sl], lb_ref[:, sl], wm.astype(BF16),
                bt_ref[:, g:g + 1])
            saved.append((wm, vhat, rstd, vn, mixed))
            cols.append(_gelu(z_ref[:, sl]) * mixed)
        ya = jnp.concatenate(cols, axis=1)
        rstd_a = lax.rsqrt(jnp.mean(ya * ya, axis=-1, keepdims=True) + EPS)
        ya_hat = ya * rstd_a
        dyv = dy_ref[...].astype(F32)
        dgn_ref[...] += jnp.sum(dyv * ya_hat, axis=0, keepdims=True)
        t = dyv * gn_ref[...]
        dya = rstd_a * (t - ya_hat * jnp.mean(t * ya_hat, axis=-1, keepdims=True))
        dlg_cols, dlb_cols = [], []
        for g in range(G):
            sl = slice(g * GROUP_DIM, (g + 1) * GROUP_DIM)
            sv = slice(W_A + g * GROUP_DIM, W_A + (g + 1) * GROUP_DIM)
            wm, vhat, rstd, vn, mixed = saved[g]
            dya_g = dya[:, sl]
            zu = z_ref[:, sl]
            dz_ref[:, sl] = (dya_g * mixed * _gelu_grad(zu)).astype(BF16)
            dmix = dya_g * _gelu(zu)
            dmix_bf = dmix.astype(BF16)
            db_ref[g] += jnp.sum(dmix, axis=1, keepdims=True)
            dw = lax.dot_general(dmix_bf, vn.astype(BF16), (((1,), (1,)), ((), ())),
                                 preferred_element_type=F32)
            dw_ref[g] += jnp.where(mask, dw, 0.0)
            dvn = jnp.dot(wm.T.astype(BF16), dmix_bf, preferred_element_type=F32)
            dlg_cols.append(jnp.sum(dvn * vhat, axis=0, keepdims=True))
            dlb_cols.append(jnp.sum(dvn, axis=0, keepdims=True))
            dvhat = dvn * lg_ref[:, sl]
            dv = rstd * (dvhat - jnp.mean(dvhat, axis=-1, keepdims=True)
                         - vhat * jnp.mean(dvhat * vhat, axis=-1, keepdims=True))
            dz_ref[:, sv] = (dv * _gelu_grad(z_ref[:, sv])).astype(BF16)
        dlg_ref[...] += jnp.concatenate(dlg_cols, axis=1)
        dlb_ref[...] += jnp.concatenate(dlb_cols, axis=1)

    vec = pl.BlockSpec((1, W_A), lambda i: (0, 0))
    wspec = pl.BlockSpec((G, SGU_BLOCK, SGU_BLOCK), lambda i: (0, 0, 0))
    return pl.pallas_call(
        body, name=name, grid=(T // SGU_BLOCK,),
        in_specs=[pl.BlockSpec((SGU_BLOCK, 2 * W_A), lambda i: (i, 0)),
                  pl.BlockSpec((SGU_BLOCK, W_A), lambda i: (i, 0)), vec, vec, wspec,
                  pl.BlockSpec((SGU_BLOCK, G), lambda i: (0, 0)), vec],
        out_specs=[pl.BlockSpec((SGU_BLOCK, 2 * W_A), lambda i: (i, 0)), vec, vec, wspec,
                   pl.BlockSpec((G, SGU_BLOCK, 1), lambda i: (0, 0, 0)), vec],
        out_shape=[jax.ShapeDtypeStruct((T, 2 * W_A), BF16), jax.ShapeDtypeStruct((1, W_A), F32),
                   jax.ShapeDtypeStruct((1, W_A), F32),
                   jax.ShapeDtypeStruct((G, SGU_BLOCK, SGU_BLOCK), F32),
                   jax.ShapeDtypeStruct((G, SGU_BLOCK, 1), F32),
                   jax.ShapeDtypeStruct((1, W_A), F32)],
        compiler_params=_params(1))(z, dy, ln_g, ln_b, w_s, b_t, gn)


def _split_dot(x, tri):
    hi = x.astype(BF16)
    lo = (x - hi.astype(F32)).astype(BF16)
    return (jnp.dot(hi, tri, preferred_element_type=F32)
            + jnp.dot(lo, tri, preferred_element_type=F32))


def _tri(n, rel):
    r = lax.broadcasted_iota(jnp.int32, (n, n), 0)
    c = lax.broadcasted_iota(jnp.int32, (n, n), 1)
    return rel(r, c).astype(BF16)


def _dot_nt(a, b):
    return lax.dot_general(a, b, (((1,), (1,)), ((), ())), preferred_element_type=F32)


def _dot_tn(a, b):
    return lax.dot_general(a, b, (((0,), (0,)), ((), ())), preferred_element_type=F32)


def _sb_scores(qs, kj, mask):
    zz = _dot_nt(qs, kj)
    log_beta = jnp.minimum(zz, 0.0) - jnp.log(1.0 + jnp.exp(-jnp.abs(zz)))
    log_1m = log_beta - zz
    if mask is not None:
        log_1m = jnp.where(mask, log_1m, 0.0)
    return log_beta, log_1m


def _masked(mask, x):
    return x if mask is None else jnp.where(mask, x, 0.0)


def _below(old, new, row0):
    if row0 == 0:
        return tuple(new)
    return tuple(jnp.concatenate([o[:row0], n], axis=0) for o, n in zip(old, new))


def _sb_tiles(T, narrow=False):
    tk = _pick(128 if narrow else 256, [T])
    per = 4 if narrow else 2
    tq = per * tk if T % (per * tk) == 0 else tk
    return tq, tk


def _sb_cols(w_a, w_b):
    base = 2 * w_a // GROUP_DIM
    per = w_b // GROUP_DIM
    return base, base + per, base + 2 * per


def _sb_forward(name, z, w_a, w_b):
    T = z.shape[0]
    H = w_b // GROUP_DIM
    tq, tk = _sb_tiles(T, narrow=True)
    per = tq // tk
    qc, kc, vc = _sb_cols(w_a, w_b)
    scale = GROUP_DIM ** -0.5

    def body(q_ref, k_ref, v_ref, y_ref, tot_ref):
        i = pl.program_id(1)
        qs = (q_ref[...] * scale).astype(BF16)
        upper = _tri(tk, lambda r, c: r > c)
        ahead = (lax.broadcasted_iota(jnp.int32, (tq, tk), 1)
                 - lax.broadcasted_iota(jnp.int32, (tq, tk), 0))

        def step(j, carry, masked, row0=0):
            acc, later = (c[row0:] for c in carry)
            k0 = pl.multiple_of(j * tk, tk)
            kj = k_ref[pl.ds(k0, tk), :].astype(BF16)
            vj = v_ref[pl.ds(k0, tk), :].astype(BF16)
            mask = ahead[row0:] < i * tq - k0 if masked else None
            log_beta, log_1m = _sb_scores(qs[row0:], kj, mask)
            rest = _split_dot(log_1m, upper) + later
            a = _masked(mask, jnp.exp(log_beta + rest))
            acc = acc + jnp.dot(a.astype(BF16), vj, preferred_element_type=F32)
            later = later + jnp.sum(log_1m, axis=1, keepdims=True)
            return _below(carry, (acc, later), row0)

        def blocks(p, c):
            for d in reversed(range(per)):
                c = step(p * per + d, c, False)
            return c

        carry = (jnp.zeros((tq, GROUP_DIM), F32), jnp.zeros((tq, 1), F32))
        for d in reversed(range(per)):
            carry = step(i * per + d, carry, True, d * tk)
        acc, total = lax.fori_loop(0, i, lambda pp, c: blocks(i - 1 - pp, c), carry)
        y_ref[...] = acc
        tot_ref[...] = total

    return pl.pallas_call(
        body, name=name, grid=(H, T // tq),
        in_specs=[pl.BlockSpec((tq, GROUP_DIM), lambda h, i: (i, qc + h)),
                  pl.BlockSpec((T, GROUP_DIM), lambda h, i: (0, kc + h)),
                  pl.BlockSpec((T, GROUP_DIM), lambda h, i: (0, vc + h))],
        out_specs=[pl.BlockSpec((tq, GROUP_DIM), lambda h, i: (i, h)),
                   pl.BlockSpec((None, tq, 1), lambda h, i: (h, i, 0))],
        out_shape=[jax.ShapeDtypeStruct((T, w_b), F32), jax.ShapeDtypeStruct((H, T, 1), F32)],
        compiler_params=_params(2))(z, z, z)


def _sb_backward(name, z, do, total, w_a, w_b):
    T = z.shape[0]
    H = w_b // GROUP_DIM
    tq, tk = _sb_tiles(T)
    per = tq // tk
    qc, kc, vc = _sb_cols(w_a, w_b)
    scale = GROUP_DIM ** -0.5

    def body(q_ref, k_ref, v_ref, do_ref, tot_ref, dq_ref, dkv_out_ref, dkv_ref):
        i = pl.program_id(1)

        @pl.when(i == 0)
        def _():
            dkv_ref[...] = jnp.zeros_like(dkv_ref)

        qs = (q_ref[...] * scale).astype(BF16)
        dob = do_ref[...].astype(BF16)
        upto = _tri(tk, lambda r, c: r <= c)
        before = _tri(tk, lambda r, c: r < c)
        ahead = (lax.broadcasted_iota(jnp.int32, (tq, tk), 1)
                 - lax.broadcasted_iota(jnp.int32, (tq, tk), 0))

        def step(j, carry, masked, row0=0):
            dq, left, e_seen = (c[row0:] for c in carry)
            qr, dor = qs[row0:], dob[row0:]
            k0 = pl.multiple_of(j * tk, tk)
            kj = k_ref[pl.ds(k0, tk), :].astype(BF16)
            vj = v_ref[pl.ds(k0, tk), :].astype(BF16)
            mask = ahead[row0:] < i * tq - k0 if masked else None
            log_beta, log_1m = _sb_scores(qr, kj, mask)
            rest = left - _split_dot(log_1m, upto)
            a = _masked(mask, jnp.exp(log_beta + rest))
            e = a * _dot_nt(dor, vj)
            e_before = e_seen + jnp.dot(e.astype(BF16), before, preferred_element_type=F32)
            beta = jnp.exp(log_beta)
            dz = _masked(mask, e * (1.0 - beta) - beta * e_before).astype(BF16)
            dq = dq + jnp.dot(dz, kj, preferred_element_type=F32)
            dkv_ref[0, pl.ds(k0, tk), :] += _dot_tn(dz, qr)
            dkv_ref[1, pl.ds(k0, tk), :] += _dot_tn(a.astype(BF16), dor)
            left = left - jnp.sum(log_1m, axis=1, keepdims=True)
            e_seen = e_seen + jnp.sum(e, axis=1, keepdims=True)
            return _below(carry, (dq, left, e_seen), row0)

        def blocks(p, c):
            for d in range(per):
                c = step(p * per + d, c, False)
            return c

        carry = (jnp.zeros((tq, GROUP_DIM), F32), tot_ref[...], jnp.zeros((tq, 1), F32))
        carry = lax.fori_loop(0, i, blocks, carry)
        for d in range(per):
            carry = step(i * per + d, carry, True, d * tk)
        dq_ref[...] = (carry[0] * scale).astype(BF16)

        @pl.when(i == T // tq - 1)
        def _():
            dkv_out_ref[...] = dkv_ref[...].astype(BF16)

    return pl.pallas_call(
        body, name=name, grid=(H, T // tq),
        in_specs=[pl.BlockSpec((tq, GROUP_DIM), lambda h, i: (i, qc + h)),
                  pl.BlockSpec((T, GROUP_DIM), lambda h, i: (0, kc + h)),
                  pl.BlockSpec((T, GROUP_DIM), lambda h, i: (0, vc + h)),
                  pl.BlockSpec((tq, GROUP_DIM), lambda h, i: (i, h)),
                  pl.BlockSpec((None, tq, 1), lambda h, i: (h, i, 0))],
        out_specs=[pl.BlockSpec((tq, GROUP_DIM), lambda h, i: (i, h)),
                   pl.BlockSpec((2, T, GROUP_DIM), lambda h, i: (0, 0, h))],
        out_shape=[jax.ShapeDtypeStruct((T, w_b), BF16), jax.ShapeDtypeStruct((2, T, w_b), BF16)],
        scratch_shapes=[pltpu.VMEM((2, T, GROUP_DIM), F32)],
        compiler_params=_params(2))(z, z, z, do, total)


def _softmax_rows(s):
    m = jnp.max(s, axis=-1, keepdims=True)
    p = jnp.exp(s - m)
    return p / jnp.sum(p, axis=-1, keepdims=True)


def _xattn_forward(name, q, kv):
    T, D = q.shape
    Nm = kv.shape[0]
    dh = D // X_HEADS
    tq = _pick(1024, [T])

    def body(q_ref, k_ref, v_ref, o_ref):
        p = _softmax_rows(_dot_nt(q_ref[...], k_ref[...]))
        o_ref[...] = jnp.dot(p.astype(BF16), v_ref[...], preferred_element_type=F32).astype(BF16)

    return pl.pallas_call(
        body, name=name, grid=(T // tq, X_HEADS),
        in_specs=[pl.BlockSpec((tq, dh), lambda i, h: (i, h)),
                  pl.BlockSpec((Nm, dh), lambda i, h: (0, h)),
                  pl.BlockSpec((Nm, dh), lambda i, h: (0, X_HEADS + h))],
        out_specs=pl.BlockSpec((tq, dh), lambda i, h: (i, h)),
        out_shape=jax.ShapeDtypeStruct((T, D), BF16),
        compiler_params=_params(2))(q, kv, kv)


def _xattn_backward(name, q, kv, do):
    T, D = q.shape
    Nm = kv.shape[0]
    dh = D // X_HEADS
    tq = _pick(1024, [T])
    scale = dh ** -0.5

    def body(q_ref, k_ref, v_ref, do_ref, dq_ref, dkv_ref):
        @pl.when(pl.program_id(1) == 0)
        def _():
            dkv_ref[...] = jnp.zeros_like(dkv_ref)

        qv, kk, vv, dov = q_ref[...], k_ref[...], v_ref[...], do_ref[...]
        p = _softmax_rows(_dot_nt(qv, kk))
        dp = _dot_nt(dov, vv)
        ds = (p * (dp - jnp.sum(dp * p, axis=-1, keepdims=True))).astype(BF16)
        dq_ref[...] = (jnp.dot(ds, kk, preferred_element_type=F32) * scale).astype(BF16)
        dkv_ref[0] += _dot_tn(ds, qv)
        dkv_ref[1] += _dot_tn(p.astype(BF16), dov)

    blk = pl.BlockSpec((tq, dh), lambda h, i: (i, h))
    return pl.pallas_call(
        body, name=name, grid=(X_HEADS, T // tq),
        in_specs=[blk, pl.BlockSpec((Nm, dh), lambda h, i: (0, h)),
                  pl.BlockSpec((Nm, dh), lambda h, i: (0, X_HEADS + h)), blk],
        out_specs=[blk, pl.BlockSpec((2, Nm, dh), lambda h, i: (0, 0, h))],
        out_shape=[jax.ShapeDtypeStruct((T, D), BF16), jax.ShapeDtypeStruct((2, Nm, D), F32)],
        compiler_params=_params(2))(q, kv, kv, do)


def _position():
    x, y, c = lax.axis_index("x"), lax.axis_index("y"), lax.axis_index("c")
    other_chips = [(1 - x, y), (x, 1 - y), (1 - x, 1 - y)]
    return x, y, c, other_chips


def _hbm_spec():
    return pl.BlockSpec(memory_space=pltpu.HBM)


def _sem_spec():
    return pl.BlockSpec(memory_space=pltpu.SEMAPHORE)


def _split_start(name, arrays, make_copies, n_sems, deps=()):
    n, d = len(arrays), len(deps)

    def body(*refs):
        ins = refs[:n]
        send_sems, recv_sems = refs[n + d], refs[n + d + 1]
        token = refs[-1]
        for cp in make_copies(ins, send_sems, recv_sems):
            cp.start()
        token[...] = jnp.zeros_like(token)

    res = pl.pallas_call(
        body, name=name,
        out_shape=(pltpu.SemaphoreType.DMA((n_sems,)), pltpu.SemaphoreType.DMA((n_sems,)),
                   *[pltpu.HBM(a.shape, a.dtype) for a in arrays],
                   jax.ShapeDtypeStruct((SUBLANE, LANE), F32)),
        in_specs=[_hbm_spec()] * n + [_any_spec()] * d,
        out_specs=(_sem_spec(), _sem_spec(), *[_hbm_spec()] * n,
                   pl.BlockSpec(memory_space=pltpu.VMEM)),
        input_output_aliases={i: 2 + i for i in range(n)},
        compiler_params=pltpu.CompilerParams(
            has_side_effects=pltpu.SideEffectType.DATAFLOW_SIDE_EFFECTING),
    )(*[pltpu.with_memory_space_constraint(a, pltpu.HBM) for a in arrays], *deps)
    return res[0], res[1], list(res[2:2 + n]), res[-1]


def _split_wait(name, arrays, send_sems, recv_sems, after, make_copies):
    n = len(arrays)
    after = list(after) if isinstance(after, (list, tuple)) else [after]

    def body(*refs):
        ins = refs[:n]
        send_ref, recv_ref = refs[n], refs[n + 1]
        for cp in make_copies(ins, send_ref, recv_ref):
            cp.wait_send()
            cp.wait_recv()

    return pl.pallas_call(
        body, name=name,
        out_shape=tuple(pltpu.HBM(a.shape, a.dtype) for a in arrays),
        in_specs=[_hbm_spec()] * n + [_sem_spec(), _sem_spec()] + [_any_spec()] * len(after),
        out_specs=tuple(_hbm_spec() for _ in arrays),
        input_output_aliases={i: i for i in range(n)},
        compiler_params=pltpu.CompilerParams(
            has_side_effects=pltpu.SideEffectType.DATAFLOW_SIDE_EFFECTING),
    )(*arrays, send_sems, recv_sems, *after)


def _gather_copies(refs, send_sems, recv_sems):
    x, y, c, chips = _position()
    me = 2 * x + y
    copies = []
    for i, ref in enumerate(refs):
        rows = ref.shape[1] // 2
        piece = ref.at[me, pl.ds(c * rows, rows), :]
        for j, (px, py) in enumerate(chips):
            copies.append(pltpu.make_async_remote_copy(
                src_ref=piece, dst_ref=piece, send_sem=send_sems.at[3 * i + j],
                recv_sem=recv_sems.at[3 * i + j], device_id=(px, py, c), device_id_type=MESH))
    return copies


def _near_copies(refs, send_sems, recv_sems):
    x, y, c, chips = _position()
    me = 2 * x + y
    copies = []
    for i, ref in enumerate(refs):
        rows = ref.shape[1] // 2
        piece = ref.at[me, pl.ds(c * rows, rows), :]
        for j, (px, py) in enumerate(chips[:2]):
            copies.append(pltpu.make_async_remote_copy(
                src_ref=piece, dst_ref=piece, send_sem=send_sems.at[2 * i + j],
                recv_sem=recv_sems.at[2 * i + j], device_id=(px, py, c), device_id_type=MESH))
    return copies


def _relay_copies(refs, send_sems, recv_sems):
    x, y, c, chips = _position()
    copies = []
    for i, ref in enumerate(refs):
        rows = ref.shape[1] // 4
        for j, (px, py) in enumerate(chips[:2]):
            ox, oy = chips[1 - j]
            piece = ref.at[2 * ox + oy, pl.ds((2 * c + j) * rows, rows), :]
            copies.append(pltpu.make_async_remote_copy(
                src_ref=piece, dst_ref=piece, send_sem=send_sems.at[2 * i + j],
                recv_sem=recv_sems.at[2 * i + j], device_id=(px, py, c), device_id_type=MESH))
    return copies


def _share_copies(refs, send_sems, recv_sems):
    x, y, c, _ = _position()
    copies = []
    for i, ref in enumerate(refs):
        rows = ref.shape[0] // 2
        mine = ref.at[pl.ds(c * rows, rows), :]
        copies.append(pltpu.make_async_remote_copy(
            src_ref=mine, dst_ref=mine, send_sem=send_sems.at[i], recv_sem=recv_sems.at[i],
            device_id=(x, y, 1 - c), device_id_type=MESH))
    return copies


def _scatter_copies(refs, send_sems, recv_sems):
    x, y, c, chips = _position()
    n = len(refs) // 2
    copies = []
    for i in range(n):
        for j, (px, py) in enumerate(chips):
            copies.append(pltpu.make_async_remote_copy(
                src_ref=refs[i].at[2 * px + py], dst_ref=refs[n + i].at[j],
                send_sem=send_sems.at[3 * i + j], recv_sem=recv_sems.at[3 * i + j],
                device_id=(px, py, c), device_id_type=MESH))
    return copies


def _cast_own(name, place, shard):
    rows, cols = shard.shape
    tr = _block_rows(rows, cols)

    def body(place_ref, w_ref, o_ref):
        o_ref[...] = w_ref[...].astype(BF16)

    grid_spec = pltpu.PrefetchScalarGridSpec(
        num_scalar_prefetch=1, grid=(rows // tr,),
        in_specs=[pl.BlockSpec((tr, cols), lambda r, pr: (r, 0))],
        out_specs=pl.BlockSpec((None, tr, cols), lambda r, pr: (pr[0], r, 0)))
    return pl.pallas_call(
        body, name=name, grid_spec=grid_spec,
        out_shape=jax.ShapeDtypeStruct((N_CHIPS, rows, cols), BF16),
        compiler_params=_params(1))(place, shard)


def _forward_to_sibling(name, arrays, deps=(), which=(0, 1, 2)):
    n = len(arrays)

    def body(*refs):
        ins = refs[:n]
        send_sems, recv_sems = refs[-2:]
        x, y, c, chips = _position()
        chips = [(j, chips[j]) for j in which]
        sends = []
        for i in range(n):
            rows = ins[i].shape[1] // 2
            for j, (px, py) in chips:
                piece = ins[i].at[2 * px + py, pl.ds(c * rows, rows), :]
                cp = pltpu.make_async_remote_copy(
                    src_ref=piece, dst_ref=piece, send_sem=send_sems.at[i, j],
                    recv_sem=recv_sems.at[i, j], device_id=(x, y, 1 - c), device_id_type=MESH)
                cp.start()
                sends.append(cp)
        for i in range(n):
            rows = ins[i].shape[1] // 2
            for j, (px, py) in chips:
                piece = ins[i].at[2 * px + py, pl.ds((1 - c) * rows, rows), :]
                pltpu.make_async_remote_copy(
                    src_ref=piece, dst_ref=piece, send_sem=send_sems.at[i, j],
                    recv_sem=recv_sems.at[i, j], device_id=(x, y, 1 - c),
                    device_id_type=MESH).wait_recv()
        for cp in sends:
            cp.wait_send()

    return pl.pallas_call(
        body, name=name,
        in_specs=[_any_spec()] * (n + len(deps)), out_specs=[_any_spec()] * n,
        out_shape=[jax.ShapeDtypeStruct(a.shape, a.dtype) for a in arrays],
        input_output_aliases={i: i for i in range(n)},
        scratch_shapes=[pltpu.SemaphoreType.DMA((n, 3))] * 2,
    )(*arrays, *deps)


def _forward_copies(which):
    def make(refs, send_sems, recv_sems):
        x, y, c, chips = _position()
        copies = []
        for i, ref in enumerate(refs):
            rows = ref.shape[1] // 2
            for k, j in enumerate(which):
                px, py = chips[j]
                piece = ref.at[2 * px + py, pl.ds(c * rows, rows), :]
                sem = len(which) * i + k
                copies.append(pltpu.make_async_remote_copy(
                    src_ref=piece, dst_ref=piece, send_sem=send_sems.at[sem],
                    recv_sem=recv_sems.at[sem], device_id=(x, y, 1 - c), device_id_type=MESH))
        return copies

    return make


def _swap_copies(refs, send_sems, recv_sems):
    x, y, c, _ = _position()
    n = len(refs) // 2
    copies = []
    for i in range(n):
        rows = refs[i].shape[1] // 2
        copies.append(pltpu.make_async_remote_copy(
            src_ref=refs[i].at[:, pl.ds((1 - c) * rows, rows), :], dst_ref=refs[n + i],
            send_sem=send_sems.at[i], recv_sem=recv_sems.at[i],
            device_id=(x, y, 1 - c), device_id_type=MESH))
    return copies


def _small_copies(refs, send_sems, recv_sems):
    packed, slots = refs
    x, y, c, _ = _position()
    me = 4 * x + 2 * y + c
    copies = []
    for r in range(1, N_DEV):
        peer = (x ^ ((r >> 2) & 1), y ^ ((r >> 1) & 1), c ^ (r & 1))
        copies.append(pltpu.make_async_remote_copy(
            src_ref=packed, dst_ref=slots.at[me], send_sem=send_sems.at[r - 1],
            recv_sem=recv_sems.at[r - 1], device_id=peer, device_id_type=MESH))
    return copies


def _block_rows(rows, cols, itemsize=4, target=1 << 20):
    return _pick(max(BF16_ROWS, target // (cols * itemsize)), [rows], unit=BF16_ROWS)


def _pair_sum(name, place, grad, received):
    P, rows, cols = received.shape
    tr = _block_rows(rows, cols, itemsize=2, target=2 << 20)
    nb = rows // tr

    def body(place_ref, g_ref, r_ref, o_ref):
        o_ref[...] = (g_ref[...].astype(F32) + r_ref[...].astype(F32)).astype(BF16)

    def panel(j, pr):
        return pr[0] ^ jnp.where(j == 2, 3, 2 - j)

    grid_spec = pltpu.PrefetchScalarGridSpec(
        num_scalar_prefetch=1, grid=(P - 1, nb),
        in_specs=[pl.BlockSpec((None, tr, cols),
                               lambda j, r, pr: (panel(j, pr), pr[1] * nb + r, 0)),
                  pl.BlockSpec((None, tr, cols), lambda j, r, pr: (panel(j, pr), r, 0))],
        out_specs=pl.BlockSpec((None, tr, cols), lambda j, r, pr: (panel(j, pr), r, 0)))
    return pl.pallas_call(
        body, name=name, grid_spec=grid_spec,
        out_shape=jax.ShapeDtypeStruct(received.shape, BF16),
        compiler_params=_params(2))(place, grad, received)


def _final_sum(name, place, grad, received, from_chips):
    _, rows, cols = received.shape
    tr = _block_rows(rows, cols, target=2 << 20)
    nb = rows // tr

    def body(place_ref, g_ref, r_ref, c_ref, o_ref):
        acc = g_ref[...].astype(F32) + r_ref[...].astype(F32)
        for j in range(3):
            acc = acc + c_ref[j].astype(F32)
        o_ref[...] = acc

    grid_spec = pltpu.PrefetchScalarGridSpec(
        num_scalar_prefetch=1, grid=(nb,),
        in_specs=[pl.BlockSpec((None, tr, cols), lambda r, pr: (pr[0], pr[1] * nb + r, 0)),
                  pl.BlockSpec((None, tr, cols), lambda r, pr: (pr[0], r, 0)),
                  pl.BlockSpec((3, tr, cols), lambda r, pr: (0, r, 0))],
        out_specs=pl.BlockSpec((tr, cols), lambda r, pr: (pr[1] * nb + r, 0)))
    return pl.pallas_call(
        body, name=name, grid_spec=grid_spec,
        out_shape=jax.ShapeDtypeStruct((2 * rows, cols), F32),
        compiler_params=_params(1))(place, grad, received, from_chips)


def _sum_devices(name, me, gathered, own):
    n_dev, rows, cols = gathered.shape
    tr = _pick(256, [rows])

    def body(me_ref, g_ref, own_ref, o_ref):
        term = lambda d: jnp.where(me_ref[0] == d, own_ref[...], g_ref[d])
        acc = term(0)
        for d in range(1, n_dev):
            acc = acc + term(d)
        o_ref[...] = acc

    grid_spec = pltpu.PrefetchScalarGridSpec(
        num_scalar_prefetch=1, grid=(rows // tr,),
        in_specs=[pl.BlockSpec((n_dev, tr, cols), lambda r, me_ref: (0, r, 0)),
                  pl.BlockSpec((tr, cols), lambda r, me_ref: (r, 0))],
        out_specs=pl.BlockSpec((tr, cols), lambda r, me_ref: (r, 0)))
    return pl.pallas_call(
        body, name=name, grid_spec=grid_spec,
        out_shape=jax.ShapeDtypeStruct((rows, cols), F32),
        compiler_params=_params(1))(me, gathered, own)


def _adamw(name, w, g, m, v):
    rows, cols = w.shape
    tr = _block_rows(rows, cols)
    c1 = 1.0 / (1.0 - ADAM_B1 ** ADAM_STEP)
    c2 = 1.0 / (1.0 - ADAM_B2 ** ADAM_STEP)

    def body(w_ref, g_ref, m_ref, v_ref, go_ref, d_ref, nm_ref, nv_ref):
        gv = g_ref[...]
        go_ref[...] = gv
        nm = ADAM_B1 * m_ref[...] + (1.0 - ADAM_B1) * gv
        nv = ADAM_B2 * v_ref[...] + (1.0 - ADAM_B2) * (gv * gv)
        nm_ref[...] = nm
        nv_ref[...] = nv
        d_ref[...] = -ADAM_LR * ((nm * c1) / (jnp.sqrt(nv * c2) + ADAM_EPS) + ADAM_WD * w_ref[...])

    blk = pl.BlockSpec((tr, cols), lambda r: (r, 0))
    shape = jax.ShapeDtypeStruct((rows, cols), F32)
    return pl.pallas_call(
        body, name=name, grid=(rows // tr,), in_specs=[blk] * 4, out_specs=[blk] * 4,
        out_shape=[shape] * 4, compiler_params=_params(1))(w, g, m, v)


BIG = ("ffn1_w_in", "ffn1_w_out", "w_mix_in", "w_mix_out", "w_cq", "w_ckv", "w_co",
       "ffn2_w_in", "ffn2_w_out")
BIG_KIND = {"ffn1_w_in": "c", "ffn1_w_out": "r", "w_mix_in": "c", "w_mix_out": "r", "w_cq": "r",
            "w_ckv": "c", "w_co": "r", "ffn2_w_in": "c", "ffn2_w_out": "r"}
GATHER_GROUPS = (("ffn1_in", ("ffn1_w_in",)), ("ffn1_out", ("ffn1_w_out",)),
                 ("mix_in", ("w_mix_in",)), ("mix_out", ("w_mix_out",)),
                 ("cross", ("w_cq", "w_ckv", "w_co")),
                 ("ffn2_in", ("ffn2_w_in",)), ("ffn2_out", ("ffn2_w_out",)))
GATHER_AFTER = (("ffn1_in", None), ("ffn1_out", "ffn1_in"), ("mix_in", "ffn1_out"),
                ("mix_out", "mix_in"), ("cross", "mix_in"), ("ffn2_in", "mix_in"),
                ("ffn2_out", "ffn2_in"))
RELAYED = ("ffn1_in", "ffn2_in")
TAIL_STAGES = (("sum", "ffn2"), ("sum", "cross"), ("sum", "mix"), ("sum", "ffn1_out"),
               ("update", "ffn2"), ("update", "cross"), ("sum", "ffn1_in"), ("update", "mix"),
               ("update", "ffn1_out"), ("update", "ffn1_in"))
SMALL = ("ffn1_norm", "mix_norm", "ln_v_gain", "ln_v_bias", "spatial_w", "spatial_b", "gnorm_a",
         "gnorm_b", "cross_norm", "mem_norm", "ffn2_norm", "final_norm")
WEIGHTS = ("ffn1_norm", "ffn1_w_in", "ffn1_w_out", "mix_norm", "w_mix_in", "ln_v_gain",
           "ln_v_bias", "spatial_w", "spatial_b", "gnorm_a", "gnorm_b", "w_mix_out", "cross_norm",
           "mem_norm", "w_cq", "w_ckv", "w_co", "ffn2_norm", "ffn2_w_in", "ffn2_w_out",
           "final_norm")


def _pack(arrays):
    return jnp.concatenate([a.reshape(-1, LANE) for a in arrays], axis=0)


def _unpack(packed, like):
    out, row = [], 0
    for a in like:
        rows = a.size // LANE
        out.append(packed[row:row + rows].reshape(a.shape))
        row += rows
    return out


def _local_step(x, mem, target, small, place, weights_of, forward_early, start_tokens,
                grads_ready, grads_flush):
    T, D = x.shape
    vec = lambda name: small[name].reshape(1, -1)
    w_a = small["ln_v_gain"].size
    w_b = small["gnorm_b"].size
    G = w_a // GROUP_DIM
    w_s = small["spatial_w"].reshape(G, SGU_BLOCK, SGU_BLOCK)
    b_t = small["spatial_b"].reshape(G, SGU_BLOCK).T

    h1, ffn1_saved = _ffn_forward("ffn1", x, vec("ffn1_norm"), weights_of, place,
                                  deps=start_tokens)
    n2 = _rmsnorm_fwd("mix_norm", h1, vec("mix_norm"))
    big = weights_of("mix_in", n2)
    (z,) = _matmul("mix_in", Mat(n2), big["w_mix_in"], "nn", [("c", 1, F32)], tm=2048, tn=256)
    z = _tie("z_after_forward_start", z[0], [forward_early("mix_out", z)])
    y = _sgu_forward("sgu", z, vec("ln_v_gain"), vec("ln_v_bias"), w_s, b_t, vec("gnorm_a"), D)
    yb, sb_total = _sb_forward("stickbreak", z, w_a, w_b)
    y = _rmsnorm_fwd("gnorm_b", yb, vec("gnorm_b"), into=y, col=w_a // w_b)
    y = _tie("y_after_forward_start", y, [forward_early("cross", y)])

    def add_res(acc, ex, out):
        out[0][...] = ex[0][...] + acc

    big.update(weights_of("mix_out", y))
    (h2,) = _matmul("mix_out", Mat(y), big["w_mix_out"], "nn", [("c", 1, F32)],
                    tm=1024, tn=1024, extras=[Mat(h1)], epi=add_res)
    h2 = h2[0]
    n3 = _rmsnorm_fwd("cross_norm", h2, vec("cross_norm"))
    memn = _rmsnorm_fwd("mem_norm", mem, vec("mem_norm"))
    big.update(weights_of("cross", n3))
    x_scale = (D // X_HEADS) ** -0.5

    def scaled(acc, ex, out):
        out[0][...] = (acc * x_scale).astype(BF16)

    (q,) = _matmul("cross_q", Mat(n3), big["w_cq"], "nn", [("c", 1, BF16)],
                   tm=1024, tn=1024, epi=scaled)
    (kv,) = _matmul("cross_kv", Mat(memn), big["w_ckv"], "nn", [("c", 1, BF16)], tm=256, tn=1024)
    q, kv = q[0], kv[0]
    o = _xattn_forward("cross_attn", q, kv)
    (h3,) = _matmul("cross_out", Mat(o), big["w_co"], "nn", [("c", 1, F32)],
                    tm=1024, tn=1024, extras=[Mat(h2)], epi=add_res)
    h3 = h3[0]
    h4, ffn2_saved = _ffn_forward("ffn2", h3, vec("ffn2_norm"), weights_of, place)

    gs = {}
    loss_tile, dh4, dh4_bf, gs["final_norm"] = _loss_head("loss_head", h4, vec("final_norm"), target)
    dh3, dh3_bf, gs["ffn2_norm"] = _ffn_backward(
        "ffn2", h3, vec("ffn2_norm"), ffn2_saved, dh4, dh4_bf, grads_ready, grads_flush)

    (do,) = _matmul("cross_do", Mat(dh3_bf), big["w_co"], "nt", [("c", 1, BF16)], tm=512, tn=2048)
    (dw_co,) = _matmul("cross_dwo", Mat(o), Mat(dh3_bf), "tn", [("r", N_CHIPS, BF16)],
                       tm=512, tn=1024)
    dq, dkv = _xattn_backward("cross_attn_bwd", q, kv, do[0])
    (dw_cq,) = _matmul("cross_dwq", Mat(n3), Mat(dq), "tn", [("r", N_CHIPS, BF16)],
                       tm=512, tn=1024)
    (dw_ckv,) = _matmul("cross_dwkv", Mat(memn), Mat(dkv), "tn", [("c", N_CHIPS, BF16)],
                        tm=1024, tn=1024)
    token = grads_ready("cross", {"w_cq": dw_cq, "w_ckv": dw_ckv, "w_co": dw_co})
    dq = _tie("cross_dq_after_swap", dq, [token])
    (dn3,) = _matmul("cross_dn", Mat(dq), big["w_cq"], "nt", [("c", 1, F32)], tm=512, tn=2048)
    (dmemn,) = _matmul("cross_dmem", Mat(dkv), big["w_ckv"], "nt", [("c", 1, F32)],
                       tm=256, tn=1024, tk=1024)
    (gs["mem_norm"],) = _rmsnorm_bwd("mem_dnorm", mem, vec("mem_norm"), dmemn[0], want_dx=False)
    dn3 = _tie("cross_dn_after_scatter", dn3, [grads_flush("cross", gs["mem_norm"])])
    dh2, dh2_bf, gs["cross_norm"] = _rmsnorm_bwd("cross_dnorm", h2, vec("cross_norm"), dn3[0],
                                                 dres=dh3)

    (dy,) = _matmul("mix_dy", Mat(dh2_bf), big["w_mix_out"], "nt", [("c", 1, F32)], tm=512, tn=2048)
    dy = dy[0]
    (dw_mix_out,) = _matmul("mix_dwout", Mat(y), Mat(dh2_bf), "tn", [("r", N_CHIPS, BF16)],
                            tm=512, tn=1024)
    dza, gs["ln_v_gain"], gs["ln_v_bias"], gs["spatial_w"], db, gs["gnorm_a"] = _sgu_backward(
        "sgu_bwd", z, dy, vec("ln_v_gain"), vec("ln_v_bias"), w_s, b_t, vec("gnorm_a"))
    gs["spatial_b"] = db.reshape(G, SGU_BLOCK)
    dob, gs["gnorm_b"] = _rmsnorm_bwd("gnorm_b_bwd", yb, vec("gnorm_b"), dy, dn_col=w_a // w_b,
                                      want_bf16=False)
    dqb, dkvb = _sb_backward("stickbreak_bwd", z, dob, sb_total, w_a, w_b)
    dz = jnp.concatenate([dza, dqb, dkvb[0], dkvb[1]], axis=1)
    (dw_mix_in,) = _matmul("mix_dwin", Mat(n2), Mat(dz), "tn", [("c", N_CHIPS, BF16)],
                           tm=1024, tn=1280)
    token = grads_ready("mix", {"w_mix_in": dw_mix_in, "w_mix_out": dw_mix_out})
    dz = _tie("mix_dz_after_swap", dz, [token])
    (dn2,) = _matmul("mix_dn", Mat(dz), big["w_mix_in"], "nt", [("c", 1, F32)],
                     tm=1024, tn=1024, tk=1280)
    dn2 = _tie("mix_dn_after_scatter", dn2, [grads_flush("mix", dn2)])
    dh1, dh1_bf, gs["mix_norm"] = _rmsnorm_bwd("mix_dnorm", h1, vec("mix_norm"), dn2[0], dres=dh2)

    dx, _, gs["ffn1_norm"] = _ffn_backward(
        "ffn1", x, vec("ffn1_norm"), ffn1_saved, dh1, dh1_bf, grads_ready, grads_flush,
        early_out=True)
    gs = {k: g.reshape(small[k].shape) for k, g in gs.items()}
    return loss_tile, dx, gs


def kernel(x, mem, ffn1_norm, ffn1_w_in, ffn1_w_out, mix_norm, w_mix_in, ln_v_gain, ln_v_bias, spatial_w, spatial_b, gnorm_a, gnorm_b, w_mix_out, cross_norm, mem_norm, w_cq, w_ckv, w_co, ffn2_norm, ffn2_w_in, ffn2_w_out, final_norm, loss_target, m_ffn1_norm, m_ffn1_w_in, m_ffn1_w_out, m_mix_norm, m_w_mix_in, m_ln_v_gain, m_ln_v_bias, m_spatial_w, m_spatial_b, m_gnorm_a, m_gnorm_b, m_w_mix_out, m_cross_norm, m_mem_norm, m_w_cq, m_w_ckv, m_w_co, m_ffn2_norm, m_ffn2_w_in, m_ffn2_w_out, m_final_norm, v_ffn1_norm, v_ffn1_w_in, v_ffn1_w_out, v_mix_norm, v_w_mix_in, v_ln_v_gain, v_ln_v_bias, v_spatial_w, v_spatial_b, v_gnorm_a, v_gnorm_b, v_w_mix_out, v_cross_norm, v_mem_norm, v_w_cq, v_w_ckv, v_w_co, v_ffn2_norm, v_ffn2_w_in, v_ffn2_w_out, v_final_norm):
    given = dict(locals())
    w = {k: given[k] for k in WEIGHTS}
    m = {k: given["m_" + k] for k in WEIGHTS}
    v = {k: given["v_" + k] for k in WEIGHTS}

    cx, cy, cc = lax.axis_index("x"), lax.axis_index("y"), lax.axis_index("c")
    place = jnp.stack([2 * cx + cy, cc]).astype(jnp.int32)

    names_of = dict(GATHER_GROUPS)
    own = {g: [_cast_own(f"cast_{k}", place, w[k][0]) for k in names] for g, names in GATHER_GROUPS}
    gathers = {}

    def start_gather(group, deps):
        first_hop = _near_copies if group in RELAYED else _gather_copies
        n_sems = (2 if group in RELAYED else 3) * len(own[group])
        send, recv, arrays, token = _split_start(f"gather_start_{group}", own[group],
                                                 first_hop, n_sems, deps)
        gathers[group] = (send, recv, arrays)
        return token

    start_tokens = [start_gather(g, ()) for g, after in GATHER_AFTER if after is None]
    start_tokens += [a for g, after in GATHER_AFTER if after is not None for a in own[g]]

    def weights_of(group, after):
        as_mats = lambda arrs: {k: Mat(a, BIG_KIND[k]) for k, a in zip(names_of[group], arrs)}
        if group in forwards:
            send, recv, arrays = forwards[group]
            return as_mats(_split_wait(f"gather_forward_wait_{group}", arrays, send, recv, after,
                                       _forward_copies((0, 1, 2))))
        send, recv, arrays = gathers[group]
        if group not in RELAYED:
            arrays = _split_wait(f"gather_wait_{group}", arrays, send, recv, after, _gather_copies)
            tokens = [start_gather(g, (arrays[0],)) for g, a in GATHER_AFTER if a == group]
            return as_mats(_forward_to_sibling(f"gather_forward_{group}", list(arrays), tokens))
        arrays = _split_wait(f"gather_wait_{group}", arrays, send, recv, after, _near_copies)
        send, recv, arrays, token = _split_start(f"gather_relay_{group}", list(arrays),
                                                 _relay_copies, 2 * len(arrays))
        tokens = [token] + [start_gather(g, (arrays[0],)) for g, a in GATHER_AFTER if a == group]
        arrays = _forward_to_sibling(f"gather_forward_{group}", list(arrays), tokens, which=(0,))
        y_send, y_recv, arrays, _ = _split_start(f"gather_forward_y_{group}", list(arrays),
                                                 _forward_copies((1,)), len(arrays))

        def finish(after):
            arrs = _split_wait(f"gather_relay_wait_{group}", arrays, send, recv, after,
                               _relay_copies)
            arrs = _split_wait(f"gather_forward_y_wait_{group}", arrs, y_send, y_recv, after,
                               _forward_copies((1,)))
            return as_mats(_forward_to_sibling(f"gather_forward_diag_{group}", list(arrs),
                                               which=(2,)))

        return {**as_mats(arrays), "finish": finish}

    forwards = {}

    def forward_early(group, after):
        send, recv, arrays = gathers[group]
        arrays = _split_wait(f"gather_wait_{group}", arrays, send, recv, after, _gather_copies)
        send, recv, arrays, token = _split_start(f"gather_forward_start_{group}", list(arrays),
                                                 _forward_copies((0, 1, 2)), 3 * len(arrays))
        forwards[group] = (send, recv, arrays)
        return token

    swaps, scatters = {}, {}

    def grads_ready(group, partial):
        names = list(partial)
        grads_ = [partial[k] for k in names]
        lands = [lax.empty((g.shape[0], g.shape[1] // 2, g.shape[2]), g.dtype) for g in grads_]
        send, recv, arrays, token = _split_start(f"swap_start_{group}", grads_ + lands,
                                                 _swap_copies, len(names))
        swaps[group] = (names, send, recv, arrays)
        return token

    def grads_flush(group, after):
        names, send, recv, arrays = swaps[group]
        arrays = _split_wait(f"swap_wait_{group}", arrays, send, recv, after, _swap_copies)
        grads_, from_sibling = arrays[:len(names)], arrays[len(names):]
        sums = [_pair_sum(f"pair_sum_{k}", place, g, r)
                for k, g, r in zip(names, grads_, from_sibling)]
        lands = [lax.empty((3,) + s.shape[1:], s.dtype) for s in sums]
        send, recv, arrays, token = _split_start(f"scatter_start_{group}", sums + lands,
                                                 _scatter_copies, 3 * len(names))
        scatters[group] = (names, grads_, from_sibling, send, recv, arrays)
        return token

    small = {k: w[k] for k in SMALL}
    loss_tile, grad_x, gs = _local_step(x[0], mem[0], loss_target[0], small, place, weights_of,
                                        forward_early, start_tokens, grads_ready, grads_flush)

    packed = _pack([gs[k] for k in SMALL] + [loss_tile])
    slots = jnp.zeros((N_DEV,) + packed.shape, packed.dtype)
    small_send, small_recv, small_arrays, _ = _split_start(
        "small_start", [packed, slots], _small_copies, N_DEV - 1)

    grad, delta, new_m, new_v = {}, {}, {}, {}
    shares = {}
    after = [grad_x]
    for stage, group in TAIL_STAGES:
        if stage == "sum":
            names, grads_, from_sibling, send, recv, arrays = scatters[group]
            arrays = _split_wait(f"scatter_wait_{group}", arrays, send, recv, after,
                                 _scatter_copies)
            from_chips = arrays[len(names):]
            shards = [_final_sum(f"final_sum_{k}", place, g, r, f)
                      for k, g, r, f in zip(names, grads_, from_sibling, from_chips)]
            send, recv, shards, token = _split_start(f"share_start_{group}", shards,
                                                     _share_copies, len(names))
            shares[group] = (names, send, recv, shards)
            after = [token]
        else:
            names, send, recv, shards = shares[group]
            shards = _split_wait(f"share_wait_{group}", shards, send, recv, after, _share_copies)
            after = []
            for k, g_ in zip(names, shards):
                g_, d_, m_, v_ = _adamw(f"adamw_{k}", w[k][0], g_, m[k][0], v[k][0])
                grad[k], delta[k], new_m[k], new_v[k] = g_[None], d_[None], m_[None], v_[None]
                after.append(v_)

    packed, slots = _split_wait("small_wait", small_arrays, small_send, small_recv, after,
                                _small_copies)
    me = (4 * cx + 2 * cy + cc).astype(jnp.int32).reshape(1)
    total = _sum_devices("sum_small", me, slots, packed)
    n_small = total.shape[0] - SUBLANE
    loss = total[n_small, 0]
    small_g = total[:n_small]
    g_s, d_s, m_s, v_s = _adamw("adamw_small", _pack([w[k] for k in SMALL]), small_g,
                                _pack([m[k] for k in SMALL]), _pack([v[k] for k in SMALL]))
    like = [w[k] for k in SMALL]
    for k, g_, d_, m_, v_ in zip(SMALL, _unpack(g_s, like), _unpack(d_s, like),
                                 _unpack(m_s, like), _unpack(v_s, like)):
        grad[k], delta[k], new_m[k], new_v[k] = g_, d_, m_, v_

    return (loss, grad_x[None], *[grad[k] for k in WEIGHTS], *[delta[k] for k in WEIGHTS],
            *[new_m[k] for k in WEIGHTS], *[new_v[k] for k in WEIGHTS])
```
